```python
import jax, jax.numpy as jnp
from jax import lax
import numpy as np

D_MODEL = 1024
BATCH = 8
SEQ = 16384
DEPTH = 1

CHUNK = 64
N_META = 16
META_PAD = CHUNK - N_META
Q_BLOCK = 128
FOX_HEADS = 8
FOX_HEAD_DIM = 64
FOX_WIDTH = FOX_HEADS * FOX_HEAD_DIM
GDN_HEADS = 8
GDN_HEAD_DIM = 64
GDN_WIDTH = GDN_HEADS * GDN_HEAD_DIM
GDN_CONV = 4
D_FF = 2816
FFN_CONV = 3
IN_SPLITS = (3 * FOX_WIDTH, FOX_HEADS, 3 * GDN_WIDTH, GDN_WIDTH, GDN_HEADS, GDN_HEADS, 2 * D_MODEL)
IN_WIDTH = 3 * FOX_WIDTH + FOX_HEADS + 4 * GDN_WIDTH + 2 * GDN_HEADS + 2 * D_MODEL
RMS_EPS = 1e-6
NEG_INF = -1e30

kernel_name = "hybrid_fox_gdn_convffn_meta"


def rms_norm(x, gain):
    xf = x.astype(jnp.float32)
    y = xf * lax.rsqrt(jnp.mean(xf * xf, axis=-1, keepdims=True) + RMS_EPS)
    return (y * gain.astype(jnp.float32)).astype(x.dtype)


def l2_norm(x):
    return x * lax.rsqrt(jnp.sum(x * x, axis=-1, keepdims=True) + RMS_EPS)


def causal_dwconv(x, w, b=None):
    k_width, ch = w.shape
    y = lax.conv_general_dilated(
        x, w[:, None, :].astype(x.dtype), window_strides=(1,), padding=[(k_width - 1, 0)],
        dimension_numbers=('NWC', 'WIO', 'NWC'), feature_group_count=ch)
    if b is not None:
        y = y + b.astype(x.dtype)
    return y


def split_cols(a, sizes):
    idx = []
    acc = 0
    for s in sizes[:-1]:
        acc += s
        idx.append(acc)
    return jnp.split(a, idx, axis=-1)


def forgetting_attention(q, k, v, log_f):
    bn, seq_len, heads, dh = q.shape
    lp = -(-seq_len // Q_BLOCK) * Q_BLOCK
    pad = lp - seq_len
    q, k, v = [jnp.pad(a, ((0, 0), (0, pad), (0, 0), (0, 0))) for a in (q, k, v)]
    c = jnp.cumsum(jnp.pad(log_f, ((0, 0), (0, pad), (0, 0))), axis=1).transpose(0, 2, 1)
    nblk = lp // Q_BLOCK
    qb = q.reshape(bn, nblk, Q_BLOCK, heads, dh).transpose(1, 0, 3, 2, 4)
    cb = c.reshape(bn, heads, nblk, Q_BLOCK).transpose(2, 0, 1, 3)
    kpos = jnp.arange(lp)
    scale = dh ** -0.5

    def block(args):
        q_i, c_i, i = args
        s = jnp.einsum('bhqd,bkhd->bhqk', q_i, k, preferred_element_type=jnp.float32) * scale
        s = s + c_i[..., None] - c[:, :, None, :]
        qpos = i * Q_BLOCK + jnp.arange(Q_BLOCK)
        s = jnp.where(kpos[None, :] <= qpos[:, None], s, NEG_INF)
        p = jax.nn.softmax(s, axis=-1).astype(v.dtype)
        return jnp.einsum('bhqk,bkhd->bqhd', p, v)

    o = lax.map(block, (qb, cb, jnp.arange(nblk)))
    return o.transpose(1, 0, 2, 3, 4).reshape(bn, lp, heads * dh)[:, :seq_len]


def gated_delta_rule(q, k, v, beta, g):
    bn, seq_len, heads, dk = q.shape
    dv = v.shape[-1]
    q = l2_norm(q.astype(jnp.float32)) * (dk ** -0.5)
    k = l2_norm(k.astype(jnp.float32))
    v = v.astype(jnp.float32)
    total = META_PAD + seq_len
    back = (-total) % CHUNK
    total = total + back
    n_chunks = total // CHUNK
    pad4 = ((0, 0), (META_PAD, back), (0, 0), (0, 0))
    pad3 = ((0, 0), (META_PAD, back), (0, 0))
    q, k, v = [jnp.pad(a, pad4).reshape(bn, n_chunks, CHUNK, heads, -1).transpose(0, 3, 1, 2, 4)
               for a in (q, k, v)]
    beta, g = [jnp.pad(a, pad3).reshape(bn, n_chunks, CHUNK, heads).transpose(0, 3, 1, 2)
               for a in (beta, g)]
    gc = jnp.cumsum(g, axis=-1)
    tril = jnp.tril(jnp.ones((CHUNK, CHUNK), dtype=bool))
    strict = jnp.tril(jnp.ones((CHUNK, CHUNK), dtype=bool), -1)
    decay = jnp.exp(jnp.where(tril, gc[..., :, None] - gc[..., None, :], NEG_INF))
    kb = k * beta[..., None]
    vb = v * beta[..., None]
    lmat = jnp.where(strict, jnp.einsum('bhncd,bhnsd->bhncs', kb, k) * decay, 0.0)
    eye = jnp.eye(CHUNK, dtype=jnp.float32)
    rhs = jnp.concatenate([vb, kb * jnp.exp(gc)[..., None]], axis=-1)
    sol = lax.linalg.triangular_solve(lmat + eye, rhs, left_side=True, lower=True, unit_diagonal=True)
    value, k_cumdecay = sol[..., :dv], sol[..., dv:]
    attn_intra = jnp.einsum('bhncd,bhnsd->bhncs', q, k) * decay
    q_dec = q * jnp.exp(gc)[..., None]
    k_tail = k * jnp.exp(gc[..., -1:] - gc)[..., None]
    chunk_decay = jnp.exp(gc[..., -1])

    def step(state, xs):
        value_n, kcd_n, attn_n, qd_n, kt_n, cd_n = xs
        v_new = value_n - jnp.einsum('bhcd,bhdv->bhcv', kcd_n, state)
        o_n = jnp.einsum('bhcd,bhdv->bhcv', qd_n, state) + jnp.einsum('bhcs,bhsv->bhcv', attn_n, v_new)
        state = state * cd_n[..., None, None] + jnp.einsum('bhcd,bhcv->bhdv', kt_n, v_new)
        return state, o_n

    xs = tuple(jnp.moveaxis(a, 2, 0) for a in (value, k_cumdecay, attn_intra, q_dec, k_tail, chunk_decay))
    state0 = jnp.zeros((bn, heads, dk, dv), jnp.float32)
    _, o = lax.scan(step, state0, xs)
    o = o.transpose(1, 0, 3, 2, 4).reshape(bn, total, heads, dv)
    return o[:, META_PAD:META_PAD + seq_len]


def hybrid_mixer(h, w_in, fgt_bias, gdn_conv_w, gdn_a_log, gdn_dt_bias, gdn_norm_w, gate_bias,
                 w_branch_fox, w_branch_gdn, w_out):
    bn, seq_len, _ = h.shape
    proj = h @ w_in.astype(h.dtype)
    fox_qkv, fox_f, gdn_qkv, gdn_z, gdn_b, gdn_a, gates = split_cols(proj, IN_SPLITS)
    fox_qkv = fox_qkv.reshape(bn, seq_len, 3, FOX_HEADS, FOX_HEAD_DIM)
    log_f = jax.nn.log_sigmoid(fox_f.astype(jnp.float32) + fgt_bias.astype(jnp.float32))
    o_fox = forgetting_attention(fox_qkv[:, :, 0], fox_qkv[:, :, 1], fox_qkv[:, :, 2], log_f)
    gdn_qkv = jax.nn.silu(causal_dwconv(gdn_qkv, gdn_conv_w))
    gdn_qkv = gdn_qkv.reshape(bn, seq_len, 3, GDN_HEADS, GDN_HEAD_DIM)
    beta = jax.nn.sigmoid(gdn_b.astype(jnp.float32))
    g = -jnp.exp(gdn_a_log.astype(jnp.float32)) * jax.nn.softplus(
        gdn_a.astype(jnp.float32) + gdn_dt_bias.astype(jnp.float32))
    o_gdn = gated_delta_rule(gdn_qkv[:, :, 0], gdn_qkv[:, :, 1], gdn_qkv[:, :, 2], beta, g)
    z = gdn_z.reshape(bn, seq_len, GDN_HEADS, GDN_HEAD_DIM).astype(jnp.float32)
    o_gdn = (rms_norm(o_gdn, gdn_norm_w) * jax.nn.silu(z)).astype(h.dtype).reshape(bn, seq_len, GDN_WIDTH)
    y_fox = o_fox @ w_branch_fox.astype(h.dtype)
    y_gdn = o_gdn @ w_branch_gdn.astype(h.dtype)
    gate = jax.nn.sigmoid(gates.astype(jnp.float32) + gate_bias.astype(jnp.float32)).astype(h.dtype)
    g_fox, g_gdn = gate[..., :D_MODEL], gate[..., D_MODEL:]
    return (g_fox * y_fox + g_gdn * y_gdn) @ w_out.astype(h.dtype)


def conv_ffn(h, w_up, conv_w, conv_b, w_down):
    u = causal_dwconv(h @ w_up.astype(h.dtype), conv_w, conv_b)
    gate, up = u[..., :D_FF], u[..., D_FF:]
    return (jax.nn.silu(gate) * up) @ w_down.astype(h.dtype)


def _fwd_setup_inputs(seed: int = 0) -> dict:
    key = jax.random.key(seed)
    ks = jax.random.split(key, 20)
    f32 = jnp.float32

    def nrm(k, shape, scale):
        return jax.random.normal(k, shape, f32) * scale

    dt = jnp.exp(jax.random.uniform(ks[5], (DEPTH, GDN_HEADS), f32, np.log(1e-3), np.log(1e-1)))
    return {
        "x": nrm(ks[0], (BATCH, SEQ, D_MODEL), 1.0),
        "meta_tokens": nrm(ks[1], (N_META, D_MODEL), 1.0),
        "w_in": nrm(ks[2], (DEPTH, D_MODEL, IN_WIDTH), D_MODEL ** -0.5),
        "fgt_bias": 2.0 + nrm(ks[3], (DEPTH, FOX_HEADS), 0.5),
        "gdn_conv_w": nrm(ks[4], (DEPTH, GDN_CONV, 3 * GDN_WIDTH), GDN_CONV ** -0.5),
        "gdn_a_log": jnp.log(jax.random.uniform(ks[6], (DEPTH, GDN_HEADS), f32, 1.0, 16.0)),
        "gdn_dt_bias": dt + jnp.log(-jnp.expm1(-dt)),
        "gdn_norm_w": 1.0 + nrm(ks[7], (DEPTH, GDN_HEAD_DIM), 0.01),
        "gate_bias": nrm(ks[8], (DEPTH, 2 * D_MODEL), 0.01),
        "w_branch_fox": nrm(ks[9], (DEPTH, FOX_WIDTH, D_MODEL), FOX_WIDTH ** -0.5),
        "w_branch_gdn": nrm(ks[10], (DEPTH, GDN_WIDTH, D_MODEL), GDN_WIDTH ** -0.5),
        "w_out": nrm(ks[11], (DEPTH, D_MODEL, D_MODEL), D_MODEL ** -0.5),
        "norm_mix_w": 1.0 + nrm(ks[12], (DEPTH, D_MODEL), 0.01),
        "norm_ffn_w": 1.0 + nrm(ks[13], (DEPTH, D_MODEL), 0.01),
        "ffn_w_up": nrm(ks[14], (DEPTH, D_MODEL, 2 * D_FF), D_MODEL ** -0.5),
        "ffn_conv_w": nrm(ks[15], (DEPTH, FFN_CONV, 2 * D_FF), FFN_CONV ** -0.5),
        "ffn_conv_b": nrm(ks[16], (DEPTH, 2 * D_FF), 0.01),
        "ffn_w_down": nrm(ks[17], (DEPTH, D_FF, D_MODEL), D_FF ** -0.5),
        "norm_final_w": 1.0 + nrm(ks[18], (D_MODEL,), 0.01),
    }


def _fwd_reference(x, meta_tokens, w_in, fgt_bias, gdn_conv_w, gdn_a_log, gdn_dt_bias, gdn_norm_w,
              gate_bias, w_branch_fox, w_branch_gdn, w_out, norm_mix_w, norm_ffn_w, ffn_w_up,
              ffn_conv_w, ffn_conv_b, ffn_w_down, norm_final_w):
    bn = x.shape[0]
    meta = jnp.broadcast_to(meta_tokens[None].astype(x.dtype), (bn, N_META, D_MODEL))
    h = jnp.concatenate([meta, x], axis=1)
    for layer in range(DEPTH):
        h = h + hybrid_mixer(rms_norm(h, norm_mix_w[layer]), w_in[layer], fgt_bias[layer],
                             gdn_conv_w[layer], gdn_a_log[layer], gdn_dt_bias[layer],
                             gdn_norm_w[layer], gate_bias[layer], w_branch_fox[layer],
                             w_branch_gdn[layer], w_out[layer])
        h = h + conv_ffn(rms_norm(h, norm_ffn_w[layer]), ffn_w_up[layer], ffn_conv_w[layer],
                         ffn_conv_b[layer], ffn_w_down[layer])
    return rms_norm(h, norm_final_w)[:, N_META:]


import jax as _jax
import jax.numpy as _jnp

TWIN_FORMAT = 'train_step'
FWD_PARAMS = ['x', 'meta_tokens', 'w_in', 'fgt_bias', 'gdn_conv_w', 'gdn_a_log', 'gdn_dt_bias', 'gdn_norm_w', 'gate_bias', 'w_branch_fox', 'w_branch_gdn', 'w_out', 'norm_mix_w', 'norm_ffn_w', 'ffn_w_up', 'ffn_conv_w', 'ffn_conv_b', 'ffn_w_down', 'norm_final_w']
TWIN_WEIGHTS = ['meta_tokens', 'w_in', 'fgt_bias', 'gdn_conv_w', 'gdn_a_log', 'gdn_dt_bias', 'gdn_norm_w', 'gate_bias', 'w_branch_fox', 'w_branch_gdn', 'w_out', 'norm_mix_w', 'norm_ffn_w', 'ffn_w_up', 'ffn_conv_w', 'ffn_conv_b', 'ffn_w_down', 'norm_final_w']
TWIN_DIFF_INPUT = 'x'
TWIN_INPUTS = ['x', 'meta_tokens', 'w_in', 'fgt_bias', 'gdn_conv_w', 'gdn_a_log', 'gdn_dt_bias', 'gdn_norm_w', 'gate_bias', 'w_branch_fox', 'w_branch_gdn', 'w_out', 'norm_mix_w', 'norm_ffn_w', 'ffn_w_up', 'ffn_conv_w', 'ffn_conv_b', 'ffn_w_down', 'norm_final_w', 'loss_target', 'm_meta_tokens', 'm_w_in', 'm_fgt_bias', 'm_gdn_conv_w', 'm_gdn_a_log', 'm_gdn_dt_bias', 'm_gdn_norm_w', 'm_gate_bias', 'm_w_branch_fox', 'm_w_branch_gdn', 'm_w_out', 'm_norm_mix_w', 'm_norm_ffn_w', 'm_ffn_w_up', 'm_ffn_conv_w', 'm_ffn_conv_b', 'm_ffn_w_down', 'm_norm_final_w', 'v_meta_tokens', 'v_w_in', 'v_fgt_bias', 'v_gdn_conv_w', 'v_gdn_a_log', 'v_gdn_dt_bias', 'v_gdn_norm_w', 'v_gate_bias', 'v_w_branch_fox', 'v_w_branch_gdn', 'v_w_out', 'v_norm_mix_w', 'v_norm_ffn_w', 'v_ffn_w_up', 'v_ffn_conv_w', 'v_ffn_conv_b', 'v_ffn_w_down', 'v_norm_final_w']
TWIN_OUTPUTS = ['loss', 'grad_x', 'grad_meta_tokens', 'grad_w_in', 'grad_fgt_bias', 'grad_gdn_conv_w', 'grad_gdn_a_log', 'grad_gdn_dt_bias', 'grad_gdn_norm_w', 'grad_gate_bias', 'grad_w_branch_fox', 'grad_w_branch_gdn', 'grad_w_out', 'grad_norm_mix_w', 'grad_norm_ffn_w', 'grad_ffn_w_up', 'grad_ffn_conv_w', 'grad_ffn_conv_b', 'grad_ffn_w_down', 'grad_norm_final_w', 'delta_meta_tokens', 'delta_w_in', 'delta_fgt_bias', 'delta_gdn_conv_w', 'delta_gdn_a_log', 'delta_gdn_dt_bias', 'delta_gdn_norm_w', 'delta_gate_bias', 'delta_w_branch_fox', 'delta_w_branch_gdn', 'delta_w_out', 'delta_norm_mix_w', 'delta_norm_ffn_w', 'delta_ffn_w_up', 'delta_ffn_conv_w', 'delta_ffn_conv_b', 'delta_ffn_w_down', 'delta_norm_final_w', 'new_m_meta_tokens', 'new_m_w_in', 'new_m_fgt_bias', 'new_m_gdn_conv_w', 'new_m_gdn_a_log', 'new_m_gdn_dt_bias', 'new_m_gdn_norm_w', 'new_m_gate_bias', 'new_m_w_branch_fox', 'new_m_w_branch_gdn', 'new_m_w_out', 'new_m_norm_mix_w', 'new_m_norm_ffn_w', 'new_m_ffn_w_up', 'new_m_ffn_conv_w', 'new_m_ffn_conv_b', 'new_m_ffn_w_down', 'new_m_norm_final_w', 'new_v_meta_tokens', 'new_v_w_in', 'new_v_fgt_bias', 'new_v_gdn_conv_w', 'new_v_gdn_a_log', 'new_v_gdn_dt_bias', 'new_v_gdn_norm_w', 'new_v_gate_bias', 'new_v_w_branch_fox', 'new_v_w_branch_gdn', 'new_v_w_out', 'new_v_norm_mix_w', 'new_v_norm_ffn_w', 'new_v_ffn_w_up', 'new_v_ffn_conv_w', 'new_v_ffn_conv_b', 'new_v_ffn_w_down', 'new_v_norm_final_w']
TWIN_LEAF_KINDS = {'loss': 'loss', 'grad_x': 'grad_x', 'grad_meta_tokens': 'grad_w', 'grad_w_in': 'grad_w', 'grad_fgt_bias': 'grad_w', 'grad_gdn_conv_w': 'grad_w', 'grad_gdn_a_log': 'grad_w', 'grad_gdn_dt_bias': 'grad_w', 'grad_gdn_norm_w': 'grad_w', 'grad_gate_bias': 'grad_w', 'grad_w_branch_fox': 'grad_w', 'grad_w_branch_gdn': 'grad_w', 'grad_w_out': 'grad_w', 'grad_norm_mix_w': 'grad_w', 'grad_norm_ffn_w': 'grad_w', 'grad_ffn_w_up': 'grad_w', 'grad_ffn_conv_w': 'grad_w', 'grad_ffn_conv_b': 'grad_w', 'grad_ffn_w_down': 'grad_w', 'grad_norm_final_w': 'grad_w', 'delta_meta_tokens': 'delta_w', 'delta_w_in': 'delta_w', 'delta_fgt_bias': 'delta_w', 'delta_gdn_conv_w': 'delta_w', 'delta_gdn_a_log': 'delta_w', 'delta_gdn_dt_bias': 'delta_w', 'delta_gdn_norm_w': 'delta_w', 'delta_gate_bias': 'delta_w', 'delta_w_branch_fox': 'delta_w', 'delta_w_branch_gdn': 'delta_w', 'delta_w_out': 'delta_w', 'delta_norm_mix_w': 'delta_w', 'delta_norm_ffn_w': 'delta_w', 'delta_ffn_w_up': 'delta_w', 'delta_ffn_conv_w': 'delta_w', 'delta_ffn_conv_b': 'delta_w', 'delta_ffn_w_down': 'delta_w', 'delta_norm_final_w': 'delta_w', 'new_m_meta_tokens': 'new_m', 'new_m_w_in': 'new_m', 'new_m_fgt_bias': 'new_m', 'new_m_gdn_conv_w': 'new_m', 'new_m_gdn_a_log': 'new_m', 'new_m_gdn_dt_bias': 'new_m', 'new_m_gdn_norm_w': 'new_m', 'new_m_gate_bias': 'new_m', 'new_m_w_branch_fox': 'new_m', 'new_m_w_branch_gdn': 'new_m', 'new_m_w_out': 'new_m', 'new_m_norm_mix_w': 'new_m', 'new_m_norm_ffn_w': 'new_m', 'new_m_ffn_w_up': 'new_m', 'new_m_ffn_conv_w': 'new_m', 'new_m_ffn_conv_b': 'new_m', 'new_m_ffn_w_down': 'new_m', 'new_m_norm_final_w': 'new_m', 'new_v_meta_tokens': 'new_v', 'new_v_w_in': 'new_v', 'new_v_fgt_bias': 'new_v', 'new_v_gdn_conv_w': 'new_v', 'new_v_gdn_a_log': 'new_v', 'new_v_gdn_dt_bias': 'new_v', 'new_v_gdn_norm_w': 'new_v', 'new_v_gate_bias': 'new_v', 'new_v_w_branch_fox': 'new_v', 'new_v_w_branch_gdn': 'new_v', 'new_v_w_out': 'new_v', 'new_v_norm_mix_w': 'new_v', 'new_v_norm_ffn_w': 'new_v', 'new_v_ffn_w_up': 'new_v', 'new_v_ffn_conv_w': 'new_v', 'new_v_ffn_conv_b': 'new_v', 'new_v_ffn_w_down': 'new_v', 'new_v_norm_final_w': 'new_v'}


def _forward(args):
    return _fwd_reference(*[args[k] for k in FWD_PARAMS])


def _output_shape():
    def fwd():
        inp = _fwd_setup_inputs(0)
        return _fwd_reference(*[inp[k] for k in FWD_PARAMS])
    out = _jax.eval_shape(fwd)
    return out.shape, out.dtype

N_MICROBATCH = 1
ADAM_LR = 0.001
ADAM_B1 = 0.9
ADAM_B2 = 0.999
ADAM_EPS = 1e-08
ADAM_WD = 0.01
ADAM_STEP = 10
PER_EXAMPLE_BATCH_AXIS = {'x': 0, 'loss_target': 0}
SHARED_INPUTS = []
_WEIGHT_DTYPES = {'meta_tokens': _jnp.float32, 'w_in': _jnp.float32, 'fgt_bias': _jnp.float32, 'gdn_conv_w': _jnp.float32, 'gdn_a_log': _jnp.float32, 'gdn_dt_bias': _jnp.float32, 'gdn_norm_w': _jnp.float32, 'gate_bias': _jnp.float32, 'w_branch_fox': _jnp.float32, 'w_branch_gdn': _jnp.float32, 'w_out': _jnp.float32, 'norm_mix_w': _jnp.float32, 'norm_ffn_w': _jnp.float32, 'ffn_w_up': _jnp.float32, 'ffn_conv_w': _jnp.float32, 'ffn_conv_b': _jnp.float32, 'ffn_w_down': _jnp.float32, 'norm_final_w': _jnp.float32}
MOMENT_SCALE = {'meta_tokens': 7.498754e-03, 'w_in': 1.167344e-01, 'fgt_bias': 8.636955e-01, 'gdn_conv_w': 1.654930e-01, 'gdn_a_log': 7.329971e-01, 'gdn_dt_bias': 7.421784e-01, 'gdn_norm_w': 4.916646e-01, 'gate_bias': 4.003006e-02, 'w_branch_fox': 8.311633e-02, 'w_branch_gdn': 1.172446e-01, 'w_out': 1.438511e-01, 'norm_mix_w': 2.932954e-01, 'norm_ffn_w': 2.541004e-01, 'ffn_w_up': 1.090978e-01, 'ffn_conv_w': 1.070916e-01, 'ffn_conv_b': 1.050800e-01, 'ffn_w_down': 1.780838e-01, 'norm_final_w': 1.278612e+02}


def _to_microbatches(a, axis):
    t = _jnp.moveaxis(a, axis, 0)
    t = t.reshape((N_MICROBATCH, t.shape[0] // N_MICROBATCH) + t.shape[1:])
    return _jnp.moveaxis(t, 1, axis + 1)


def setup_inputs(seed: int = 0) -> dict:
    inp = _fwd_setup_inputs(seed)
    key = _jax.random.fold_in(_jax.random.key(seed), 7919)
    shape, _ = _output_shape()
    out = dict(inp)
    out["loss_target"] = _jax.random.normal(_jax.random.fold_in(key, 0), shape, _jnp.float32)
    for i, name in enumerate(TWIN_WEIGHTS):
        w = inp[name].astype(_jnp.float32)
        if MOMENT_SCALE is None:
            s = _jnp.sqrt(_jnp.mean(_jnp.square(w)) + 1e-30)
        else:
            s = MOMENT_SCALE[name]
        km, kv = _jax.random.split(_jax.random.fold_in(key, i + 1))
        out[name] = w
        out["m_" + name] = s * _jax.random.normal(km, w.shape, _jnp.float32)
        out["v_" + name] = (s * s) * _jax.random.uniform(kv, w.shape, _jnp.float32, 0.5, 1.5)
    if N_MICROBATCH > 1:
        for name, axis in PER_EXAMPLE_BATCH_AXIS.items():
            out[name] = _to_microbatches(out[name], axis)
    return {'x': out['x'], 'meta_tokens': out['meta_tokens'], 'w_in': out['w_in'], 'fgt_bias': out['fgt_bias'], 'gdn_conv_w': out['gdn_conv_w'], 'gdn_a_log': out['gdn_a_log'], 'gdn_dt_bias': out['gdn_dt_bias'], 'gdn_norm_w': out['gdn_norm_w'], 'gate_bias': out['gate_bias'], 'w_branch_fox': out['w_branch_fox'], 'w_branch_gdn': out['w_branch_gdn'], 'w_out': out['w_out'], 'norm_mix_w': out['norm_mix_w'], 'norm_ffn_w': out['norm_ffn_w'], 'ffn_w_up': out['ffn_w_up'], 'ffn_conv_w': out['ffn_conv_w'], 'ffn_conv_b': out['ffn_conv_b'], 'ffn_w_down': out['ffn_w_down'], 'norm_final_w': out['norm_final_w'], 'loss_target': out['loss_target'], 'm_meta_tokens': out['m_meta_tokens'], 'm_w_in': out['m_w_in'], 'm_fgt_bias': out['m_fgt_bias'], 'm_gdn_conv_w': out['m_gdn_conv_w'], 'm_gdn_a_log': out['m_gdn_a_log'], 'm_gdn_dt_bias': out['m_gdn_dt_bias'], 'm_gdn_norm_w': out['m_gdn_norm_w'], 'm_gate_bias': out['m_gate_bias'], 'm_w_branch_fox': out['m_w_branch_fox'], 'm_w_branch_gdn': out['m_w_branch_gdn'], 'm_w_out': out['m_w_out'], 'm_norm_mix_w': out['m_norm_mix_w'], 'm_norm_ffn_w': out['m_norm_ffn_w'], 'm_ffn_w_up': out['m_ffn_w_up'], 'm_ffn_conv_w': out['m_ffn_conv_w'], 'm_ffn_conv_b': out['m_ffn_conv_b'], 'm_ffn_w_down': out['m_ffn_w_down'], 'm_norm_final_w': out['m_norm_final_w'], 'v_meta_tokens': out['v_meta_tokens'], 'v_w_in': out['v_w_in'], 'v_fgt_bias': out['v_fgt_bias'], 'v_gdn_conv_w': out['v_gdn_conv_w'], 'v_gdn_a_log': out['v_gdn_a_log'], 'v_gdn_dt_bias': out['v_gdn_dt_bias'], 'v_gdn_norm_w': out['v_gdn_norm_w'], 'v_gate_bias': out['v_gate_bias'], 'v_w_branch_fox': out['v_w_branch_fox'], 'v_w_branch_gdn': out['v_w_branch_gdn'], 'v_w_out': out['v_w_out'], 'v_norm_mix_w': out['v_norm_mix_w'], 'v_norm_ffn_w': out['v_norm_ffn_w'], 'v_ffn_w_up': out['v_ffn_w_up'], 'v_ffn_conv_w': out['v_ffn_conv_w'], 'v_ffn_conv_b': out['v_ffn_conv_b'], 'v_ffn_w_down': out['v_ffn_w_down'], 'v_norm_final_w': out['v_norm_final_w']}


def _loss(weights, diff, rest, loss_target):
    with _jax.named_scope("forward"):
        args = {**rest, TWIN_DIFF_INPUT: diff, **{k: w.astype(_WEIGHT_DTYPES[k]) for k, w in weights.items()}}
        y = _forward(args)
    with _jax.named_scope("loss_head"):
        err = _jnp.square(y.astype(_jnp.float32) - loss_target)
        return 0.5 * _jnp.sum(_jnp.mean(err, axis=-1)) if err.ndim else 0.5 * err


def _adamw(w, g, m, v):
    m = ADAM_B1 * m + (1.0 - ADAM_B1) * g
    v = ADAM_B2 * v + (1.0 - ADAM_B2) * _jnp.square(g)
    m_hat = m / (1.0 - ADAM_B1 ** ADAM_STEP)
    v_hat = v / (1.0 - ADAM_B2 ** ADAM_STEP)
    delta = -ADAM_LR * (m_hat / (_jnp.sqrt(v_hat) + ADAM_EPS) + ADAM_WD * w)
    return delta, m, v


def reference(x, meta_tokens, w_in, fgt_bias, gdn_conv_w, gdn_a_log, gdn_dt_bias, gdn_norm_w, gate_bias, w_branch_fox, w_branch_gdn, w_out, norm_mix_w, norm_ffn_w, ffn_w_up, ffn_conv_w, ffn_conv_b, ffn_w_down, norm_final_w, loss_target, m_meta_tokens, m_w_in, m_fgt_bias, m_gdn_conv_w, m_gdn_a_log, m_gdn_dt_bias, m_gdn_norm_w, m_gate_bias, m_w_branch_fox, m_w_branch_gdn, m_w_out, m_norm_mix_w, m_norm_ffn_w, m_ffn_w_up, m_ffn_conv_w, m_ffn_conv_b, m_ffn_w_down, m_norm_final_w, v_meta_tokens, v_w_in, v_fgt_bias, v_gdn_conv_w, v_gdn_a_log, v_gdn_dt_bias, v_gdn_norm_w, v_gate_bias, v_w_branch_fox, v_w_branch_gdn, v_w_out, v_norm_mix_w, v_norm_ffn_w, v_ffn_w_up, v_ffn_conv_w, v_ffn_conv_b, v_ffn_w_down, v_norm_final_w):
    given = dict(x=x, meta_tokens=meta_tokens, w_in=w_in, fgt_bias=fgt_bias, gdn_conv_w=gdn_conv_w, gdn_a_log=gdn_a_log, gdn_dt_bias=gdn_dt_bias, gdn_norm_w=gdn_norm_w, gate_bias=gate_bias, w_branch_fox=w_branch_fox, w_branch_gdn=w_branch_gdn, w_out=w_out, norm_mix_w=norm_mix_w, norm_ffn_w=norm_ffn_w, ffn_w_up=ffn_w_up, ffn_conv_w=ffn_conv_w, ffn_conv_b=ffn_conv_b, ffn_w_down=ffn_w_down, norm_final_w=norm_final_w, loss_target=loss_target, m_meta_tokens=m_meta_tokens, m_w_in=m_w_in, m_fgt_bias=m_fgt_bias, m_gdn_conv_w=m_gdn_conv_w, m_gdn_a_log=m_gdn_a_log, m_gdn_dt_bias=m_gdn_dt_bias, m_gdn_norm_w=m_gdn_norm_w, m_gate_bias=m_gate_bias, m_w_branch_fox=m_w_branch_fox, m_w_branch_gdn=m_w_branch_gdn, m_w_out=m_w_out, m_norm_mix_w=m_norm_mix_w, m_norm_ffn_w=m_norm_ffn_w, m_ffn_w_up=m_ffn_w_up, m_ffn_conv_w=m_ffn_conv_w, m_ffn_conv_b=m_ffn_conv_b, m_ffn_w_down=m_ffn_w_down, m_norm_final_w=m_norm_final_w, v_meta_tokens=v_meta_tokens, v_w_in=v_w_in, v_fgt_bias=v_fgt_bias, v_gdn_conv_w=v_gdn_conv_w, v_gdn_a_log=v_gdn_a_log, v_gdn_dt_bias=v_gdn_dt_bias, v_gdn_norm_w=v_gdn_norm_w, v_gate_bias=v_gate_bias, v_w_branch_fox=v_w_branch_fox, v_w_branch_gdn=v_w_branch_gdn, v_w_out=v_w_out, v_norm_mix_w=v_norm_mix_w, v_norm_ffn_w=v_norm_ffn_w, v_ffn_w_up=v_ffn_w_up, v_ffn_conv_w=v_ffn_conv_w, v_ffn_conv_b=v_ffn_conv_b, v_ffn_w_down=v_ffn_w_down, v_norm_final_w=v_norm_final_w)
    weights = {n: given[n] for n in TWIN_WEIGHTS}
    shared = {n: given[n] for n in SHARED_INPUTS}
    per_example = {n: given[n] for n in ['x']}
    grad_fn = _jax.value_and_grad(_loss, argnums=(0, 1))

    def one_microbatch(ex, loss_target):
        ex = dict(ex)
        diff = ex.pop(TWIN_DIFF_INPUT)
        return grad_fn(weights, diff, {**shared, **ex}, loss_target)

    if N_MICROBATCH == 1:
        loss, (grad_w, grad_x) = one_microbatch(per_example, given["loss_target"])
    else:
        def body(carry, xs):
            loss_sum, grad_sum = carry
            l_k, (gw_k, gx_k) = one_microbatch(xs[0], xs[1])
            with _jax.named_scope("update"):
                return (loss_sum + l_k, _jax.tree.map(_jnp.add, grad_sum, gw_k)), gx_k

        init = (_jnp.zeros((), _jnp.float32), _jax.tree.map(_jnp.zeros_like, weights))
        (loss, grad_w), grad_x = _jax.lax.scan(body, init, (per_example, given["loss_target"]))
    with _jax.named_scope("update"):
        delta_w, new_m, new_v = {}, {}, {}
        for n in TWIN_WEIGHTS:
            delta_w[n], new_m[n], new_v[n] = _adamw(weights[n], grad_w[n], given["m_" + n], given["v_" + n])
    return (loss, grad_x, *[grad_w[n] for n in TWIN_WEIGHTS], *[delta_w[n] for n in TWIN_WEIGHTS],
            *[new_m[n] for n in TWIN_WEIGHTS], *[new_v[n] for n in TWIN_WEIGHTS])
```

```python
import functools

import numpy as np
import jax
import jax.numpy as jnp
from jax import lax
from jax.experimental import pallas as pl
from jax.experimental.pallas import tpu as pltpu

F32 = jnp.float32
_BF = jnp.bfloat16
HI = lax.Precision.HIGHEST

D = 1024
N_META = 16
CH = 64
FRONT = CH - N_META
X0 = CH
HEADS = 8
HD = 64
FW = HEADS * HD
DFF = 2816
EPS = 1e-6
NEG = -1e30
T_ALIGN = 256
SMALL_W = 128
REST_W = 3 * FW + FW + 2 * D + SMALL_W
OFF_Z = 3 * FW
OFF_GATES = 4 * FW
OFF_SMALL = 4 * FW + 2 * D

LR, B1, B2, AEPS, WD, STEP = 0.001, 0.9, 0.999, 1e-08, 0.01, 10

VMEM_LIMIT = 56 * 1024 * 1024
ROW_TILE = 640
MM_TM, MM_TN, MM_TK = 1280, 512, 2816
ATT_TILE = 640


def _div(n, target, mult=128):
    if n <= target:
        return n
    best = None
    for d in range(mult, target + 1, mult):
        if n % d == 0:
            best = d
    assert best is not None, (n, target, mult)
    return best


def _cp(*sem):
    return pltpu.CompilerParams(dimension_semantics=sem, vmem_limit_bytes=VMEM_LIMIT)


def _sig(x):
    return 1.0 / (1.0 + jnp.exp(-x))


def _mm(a, b, *, ta=False, tb=False, out_dtype=F32, resid=None, name, tm=MM_TM, tn=MM_TN, tk=MM_TK):
    K, M = a.shape if ta else a.shape[::-1]
    N = b.shape[0] if tb else b.shape[1]
    assert (b.shape[1] if tb else b.shape[0]) == K
    tm, tn, tk = _div(M, tm), _div(N, tn), _div(K, tk)
    nk = K // tk
    dims = (((0 if ta else 1,), (1 if tb else 0,)), ((), ()))
    mxu = _BF

    def body(*refs):
        if resid is None:
            a_ref, b_ref, o_ref, acc = refs
            r_ref = None
        else:
            a_ref, b_ref, r_ref, o_ref, acc = refs
        k = pl.program_id(2)
        part = lax.dot_general(a_ref[...].astype(mxu), b_ref[...].astype(mxu), dims, preferred_element_type=F32)

        @pl.when(k == 0)
        def _():
            acc[...] = part

        @pl.when(k > 0)
        def _():
            acc[...] += part

        @pl.when(k == nk - 1)
        def _():
            r = acc[...]
            if r_ref is not None:
                r = r + r_ref[...]
            o_ref[...] = r.astype(o_ref.dtype)

    a_spec = pl.BlockSpec((tk, tm), lambda i, j, k: (k, i)) if ta else pl.BlockSpec((tm, tk), lambda i, j, k: (i, k))
    b_spec = pl.BlockSpec((tn, tk), lambda i, j, k: (j, k)) if tb else pl.BlockSpec((tk, tn), lambda i, j, k: (k, j))
    o_spec = pl.BlockSpec((tm, tn), lambda i, j, k: (i, j))
    in_specs = [a_spec, b_spec] + ([o_spec] if resid is not None else [])
    args = (a, b) + ((resid,) if resid is not None else ())
    return pl.pallas_call(
        body, name=name, grid=(M // tm, N // tn, nk), in_specs=in_specs, out_specs=o_spec,
        out_shape=jax.ShapeDtypeStruct((M, N), out_dtype), scratch_shapes=[pltpu.VMEM((tm, tn), F32)],
        compiler_params=_cp("parallel", "parallel", "arbitrary"))(*args)


def _rms_fwd(h, w, name):
    T = h.shape[0]
    tb = _div(T, ROW_TILE, 8)

    def body(h_ref, w_ref, o_ref):
        x = h_ref[...]
        r = lax.rsqrt(jnp.mean(x * x, axis=-1, keepdims=True) + EPS)
        o_ref[...] = (x * r * w_ref[...]).astype(o_ref.dtype)

    return pl.pallas_call(
        body, name=name, grid=(T // tb,),
        in_specs=[pl.BlockSpec((tb, D), lambda i: (i, 0)), pl.BlockSpec((1, D), lambda i: (0, 0))],
        out_specs=pl.BlockSpec((tb, D), lambda i: (i, 0)), out_shape=jax.ShapeDtypeStruct((T, D), _BF),
        compiler_params=_cp("parallel"))(h, w)


def _rms_bwd(h, w, dy, dres, name):
    T = h.shape[0]
    tb = _div(T, ROW_TILE, 8)

    def body(h_ref, w_ref, dy_ref, dr_ref, dh_ref, dw_ref):
        x = h_ref[...]
        r = lax.rsqrt(jnp.mean(x * x, axis=-1, keepdims=True) + EPS)
        xh = x * r
        dy = dy_ref[...]
        g = dy * w_ref[...]
        dh_ref[...] = dr_ref[...] + r * (g - xh * jnp.mean(xh * g, axis=-1, keepdims=True))
        part = jnp.sum(dy * xh, axis=0, keepdims=True)

        @pl.when(pl.program_id(0) == 0)
        def _():
            dw_ref[...] = part

        @pl.when(pl.program_id(0) > 0)
        def _():
            dw_ref[...] += part

    row = pl.BlockSpec((tb, D), lambda i: (i, 0))
    one = pl.BlockSpec((1, D), lambda i: (0, 0))
    return pl.pallas_call(
        body, name=name, grid=(T // tb,), in_specs=[row, one, row, row], out_specs=[row, one],
        out_shape=[jax.ShapeDtypeStruct((T, D), F32), jax.ShapeDtypeStruct((1, D), F32)],
        compiler_params=_cp("arbitrary"))(h, w, dy, dres)


def _fgate_fwd(small, bias):
    T = small.shape[0]
    tb = _div(T, ROW_TILE, 8)

    def body(s_ref, b_ref, c_ref, carry):
        @pl.when(pl.program_id(0) == 0)
        def _():
            carry[...] = jnp.zeros_like(carry)

        lf = jax.nn.log_sigmoid(s_ref[...] + b_ref[...])
        r = lax.broadcasted_iota(jnp.int32, (tb, tb), 0)
        c = lax.broadcasted_iota(jnp.int32, (tb, tb), 1)
        tri = (r >= c).astype(F32)
        cs = jnp.dot(tri, lf, precision=HI, preferred_element_type=F32) + carry[...]
        c_ref[...] = cs
        carry[...] = cs[tb - 1:tb, :]

    return pl.pallas_call(
        body, name="fgate_fwd", grid=(T // tb,),
        in_specs=[pl.BlockSpec((tb, SMALL_W), lambda i: (i, 0)), pl.BlockSpec((1, SMALL_W), lambda i: (0, 0))],
        out_specs=pl.BlockSpec((tb, SMALL_W), lambda i: (i, 0)), out_shape=jax.ShapeDtypeStruct((T, SMALL_W), F32),
        scratch_shapes=[pltpu.VMEM((1, SMALL_W), F32)], compiler_params=_cp("arbitrary"))(small, bias)


def _fgate_bwd(small, bias, dc):
    T = small.shape[0]
    tb = _div(T, ROW_TILE, 8)
    nb = T // tb

    def body(s_ref, b_ref, dc_ref, df_ref, db_ref, carry):
        @pl.when(pl.program_id(0) == 0)
        def _():
            carry[...] = jnp.zeros_like(carry)

        r = lax.broadcasted_iota(jnp.int32, (tb, tb), 0)
        c = lax.broadcasted_iota(jnp.int32, (tb, tb), 1)
        tri = (r <= c).astype(F32)
        dlf = jnp.dot(tri, dc_ref[...], precision=HI, preferred_element_type=F32) + carry[...]
        carry[...] = dlf[0:1, :]
        df = dlf * _sig(-(s_ref[...] + b_ref[...]))
        df_ref[...] = df
        part = jnp.sum(df, axis=0, keepdims=True)

        @pl.when(pl.program_id(0) == 0)
        def _():
            db_ref[...] = part

        @pl.when(pl.program_id(0) > 0)
        def _():
            db_ref[...] += part

    rev = pl.BlockSpec((tb, SMALL_W), lambda i: (nb - 1 - i, 0))
    one = pl.BlockSpec((1, SMALL_W), lambda i: (0, 0))
    return pl.pallas_call(
        body, name="fgate_bwd", grid=(nb,), in_specs=[rev, one, rev], out_specs=[rev, one],
        out_shape=[jax.ShapeDtypeStruct((T, SMALL_W), F32), jax.ShapeDtypeStruct((1, SMALL_W), F32)],
        scratch_shapes=[pltpu.VMEM((1, SMALL_W), F32)], compiler_params=_cp("arbitrary"))(small, bias, dc)


def _att_rows(a8, tb):
    T = a8.shape[0]
    return a8.T.reshape(HEADS // 2, 2, T // tb, tb).transpose(0, 2, 1, 3)


def _att_cols(a8):
    return jnp.repeat(a8, HD, axis=1)


def _fox_fwd(qkv, c_cols, c_rows):
    T = qkv.shape[0]
    tb = _div(T, ATT_TILE)
    nb = T // tb
    npair = HEADS // 2
    scale = HD ** -0.5
    dn = (((1,), (1,)), ((), ()))

    def body(q_ref, k_ref, v_ref, cq_ref, ck_ref, o_ref, l_ref):
        i = pl.program_id(1)
        q = q_ref[...]
        lane = lax.broadcasted_iota(jnp.int32, (1, 2 * HD), 1)
        sel0 = lane < HD
        zero = jnp.zeros_like(q)
        qh = (jnp.where(sel0, q, zero) * scale, jnp.where(sel0, zero, q) * scale)
        cq = (cq_ref[:, 0:1], cq_ref[:, HD:HD + 1])
        row = i * tb + lax.broadcasted_iota(jnp.int32, (tb, 1), 0)

        def step(j, carry):
            start = pl.multiple_of(j * tb, tb)
            kb = k_ref[pl.ds(start, tb), :]
            vb = v_ref[pl.ds(start, tb), :]
            ck = ck_ref[0, j]
            col = j * tb + lax.broadcasted_iota(jnp.int32, (1, tb), 1)
            mask = (col <= row) & (col >= FRONT)
            out = []
            for h in range(2):
                m, l, acc = carry[3 * h:3 * h + 3]
                s = lax.dot_general(qh[h], kb, dn, preferred_element_type=F32) + cq[h] - ck[h:h + 1, :]
                s = jnp.where(mask, s, NEG)
                m_new = jnp.maximum(m, jnp.max(s, axis=-1, keepdims=True))
                p = jnp.where(mask, jnp.exp(s - m_new), 0.0)
                alpha = jnp.exp(m - m_new)
                l = l * alpha + jnp.sum(p, axis=-1, keepdims=True)
                acc = acc * alpha + jnp.dot(p.astype(vb.dtype), vb, preferred_element_type=F32)
                out += [m_new, l, acc]
            return tuple(out)

        m0 = jnp.full((tb, 1), NEG, F32)
        l0 = jnp.zeros((tb, 1), F32)
        a0 = jnp.zeros((tb, 2 * HD), F32)
        res = lax.fori_loop(0, i + 1, step, (m0, l0, a0, m0, l0, a0))
        outs, lses = [], []
        for h in range(2):
            m, l, acc = res[3 * h:3 * h + 3]
            ok = l > 0.0
            ls = jnp.where(ok, l, 1.0)
            outs.append(jnp.where(ok, acc / ls, 0.0))
            lses.append(jnp.where(ok, m + jnp.log(ls), 0.0))
        o_ref[...] = jnp.where(sel0, outs[0], outs[1])
        l_ref[...] = jnp.where(sel0, lses[0], lses[1])

    blk = lambda off: pl.BlockSpec((tb, 2 * HD), lambda p, i: (i, off + p))
    full = lambda off: pl.BlockSpec((T, 2 * HD), lambda p, i: (0, off + p))
    rows = pl.BlockSpec((1, nb, 2, tb), lambda p, i: (p, 0, 0, 0))
    return pl.pallas_call(
        body, name="fox_fwd", grid=(npair, nb),
        in_specs=[blk(0), full(npair), full(2 * npair), blk(0), rows],
        out_specs=[blk(0), blk(0)],
        out_shape=[jax.ShapeDtypeStruct((T, FW), F32), jax.ShapeDtypeStruct((T, FW), F32)],
        compiler_params=_cp("parallel", "arbitrary"))(qkv, qkv, qkv, c_cols, c_rows)


def _fox_bwd_dq(qkv, do, c_cols, c_rows, lse, delta):
    T = qkv.shape[0]
    tb = _div(T, ATT_TILE)
    nb = T // tb
    npair = HEADS // 2
    scale = HD ** -0.5
    dn = (((1,), (1,)), ((), ()))

    def body(q_ref, k_ref, v_ref, do_ref, cq_ref, ck_ref, l_ref, dl_ref, dq_ref, dc_ref):
        i = pl.program_id(1)
        q = q_ref[...]
        do = do_ref[...]
        lane = lax.broadcasted_iota(jnp.int32, (1, 2 * HD), 1)
        sel0 = lane < HD
        qh = (jnp.where(sel0, q, jnp.zeros_like(q)) * scale, jnp.where(sel0, jnp.zeros_like(q), q) * scale)
        doh = (jnp.where(sel0, do, jnp.zeros_like(do)), jnp.where(sel0, jnp.zeros_like(do), do))
        cq = (cq_ref[:, 0:1], cq_ref[:, HD:HD + 1])
        ls = (l_ref[:, 0:1], l_ref[:, HD:HD + 1])
        dl = (dl_ref[:, 0:1], dl_ref[:, HD:HD + 1])
        row = i * tb + lax.broadcasted_iota(jnp.int32, (tb, 1), 0)

        def step(j, carry):
            start = pl.multiple_of(j * tb, tb)
            kb = k_ref[pl.ds(start, tb), :]
            vb = v_ref[pl.ds(start, tb), :]
            ck = ck_ref[0, j]
            col = j * tb + lax.broadcasted_iota(jnp.int32, (1, tb), 1)
            mask = (col <= row) & (col >= FRONT)
            out = []
            for h in range(2):
                dq, dc = carry[2 * h:2 * h + 2]
                s = lax.dot_general(qh[h], kb, dn, preferred_element_type=F32) + cq[h] - ck[h:h + 1, :]
                p = jnp.where(mask, jnp.exp(jnp.where(mask, s, NEG) - ls[h]), 0.0)
                dp = lax.dot_general(doh[h], vb, dn, preferred_element_type=F32)
                ds = p * (dp - dl[h])
                dq = dq + jnp.dot(ds.astype(kb.dtype), kb, preferred_element_type=F32)
                dc = dc + jnp.sum(ds, axis=-1, keepdims=True)
                out += [dq, dc]
            return tuple(out)

        z2 = jnp.zeros((tb, 2 * HD), F32)
        z1 = jnp.zeros((tb, 1), F32)
        res = lax.fori_loop(0, i + 1, step, (z2, z1, z2, z1))
        dq_ref[...] = jnp.where(sel0, res[0], res[2]) * scale
        dc_ref[...] = jnp.where(sel0, res[1], res[3])

    blk = lambda off: pl.BlockSpec((tb, 2 * HD), lambda p, i: (i, off + p))
    full = lambda off: pl.BlockSpec((T, 2 * HD), lambda p, i: (0, off + p))
    rows = pl.BlockSpec((1, nb, 2, tb), lambda p, i: (p, 0, 0, 0))
    return pl.pallas_call(
        body, name="fox_bwd_dq", grid=(npair, nb),
        in_specs=[blk(0), full(npair), full(2 * npair), blk(0), blk(0), rows, blk(0), blk(0)],
        out_specs=[blk(0), blk(0)],
        out_shape=[jax.ShapeDtypeStruct((T, FW), F32), jax.ShapeDtypeStruct((T, FW), F32)],
        compiler_params=_cp("parallel", "arbitrary"))(qkv, qkv, qkv, do, c_cols, c_rows, lse, delta)


def _fox_bwd_dkv(qkv, do, c_cols, c_rows, lse_rows, delta_rows):
    T = qkv.shape[0]
    tb = _div(T, ATT_TILE)
    nb = T // tb
    npair = HEADS // 2
    scale = HD ** -0.5
    dn = (((1,), (1,)), ((), ()))

    def body(q_ref, k_ref, v_ref, do_ref, ck_ref, cq_ref, l_ref, dl_ref, dk_ref, dv_ref, dc_ref):
        jb = pl.program_id(1)
        k = k_ref[...]
        v = v_ref[...]
        lane = lax.broadcasted_iota(jnp.int32, (1, 2 * HD), 1)
        sel0 = lane < HD
        kh = (jnp.where(sel0, k, jnp.zeros_like(k)) * scale, jnp.where(sel0, jnp.zeros_like(k), k) * scale)
        vh = (jnp.where(sel0, v, jnp.zeros_like(v)), jnp.where(sel0, jnp.zeros_like(v), v))
        ck = (ck_ref[:, 0:1], ck_ref[:, HD:HD + 1])
        kidx = jb * tb + lax.broadcasted_iota(jnp.int32, (tb, 1), 0)

        def step(i, carry):
            start = pl.multiple_of(i * tb, tb)
            qb = q_ref[pl.ds(start, tb), :]
            dob = do_ref[pl.ds(start, tb), :]
            cq = cq_ref[0, i]
            ls = l_ref[0, i]
            dl = dl_ref[0, i]
            qidx = i * tb + lax.broadcasted_iota(jnp.int32, (1, tb), 1)
            mask = (kidx <= qidx) & (kidx >= FRONT)
            out = []
            for h in range(2):
                dk, dv, dc = carry[3 * h:3 * h + 3]
                st = lax.dot_general(kh[h], qb, dn, preferred_element_type=F32) + cq[h:h + 1, :] - ck[h]
                pt = jnp.where(mask, jnp.exp(jnp.where(mask, st, NEG) - ls[h:h + 1, :]), 0.0)
                dv = dv + jnp.dot(pt.astype(dob.dtype), dob, preferred_element_type=F32)
                dpt = lax.dot_general(vh[h], dob, dn, preferred_element_type=F32)
                dst = pt * (dpt - dl[h:h + 1, :])
                dk = dk + jnp.dot(dst.astype(qb.dtype), qb, preferred_element_type=F32)
                dc = dc - jnp.sum(dst, axis=-1, keepdims=True)
                out += [dk, dv, dc]
            return tuple(out)

        z2 = jnp.zeros((tb, 2 * HD), F32)
        z1 = jnp.zeros((tb, 1), F32)
        res = lax.fori_loop(jb, nb, step, (z2, z2, z1, z2, z2, z1))
        dk_ref[...] = jnp.where(sel0, res[0], res[3]) * scale
        dv_ref[...] = jnp.where(sel0, res[1], res[4])
        dc_ref[...] = jnp.where(sel0, res[2], res[5])

    blk = lambda off: pl.BlockSpec((tb, 2 * HD), lambda p, j: (j, off + p))
    full = lambda off: pl.BlockSpec((T, 2 * HD), lambda p, j: (0, off + p))
    rows = pl.BlockSpec((1, nb, 2, tb), lambda p, j: (p, 0, 0, 0))
    return pl.pallas_call(
        body, name="fox_bwd_dkv", grid=(npair, nb),
        in_specs=[full(0), blk(npair), blk(2 * npair), full(0), blk(0), rows, rows, rows],
        out_specs=[blk(0), blk(0), blk(0)],
        out_shape=[jax.ShapeDtypeStruct((T, FW), F32)] * 3,
        compiler_params=_cp("parallel", "arbitrary"))(qkv, qkv, qkv, do, c_cols, c_rows, lse_rows, delta_rows)


def _head_dot(a, b, ones_blk):
    T = a.shape[0]
    tb = _div(T, ROW_TILE, 8)

    def body(a_ref, b_ref, e_ref, o_ref):
        prod = a_ref[...].astype(F32) * b_ref[...].astype(F32)
        o_ref[...] = jnp.dot(prod, e_ref[...], precision=HI, preferred_element_type=F32)

    row = pl.BlockSpec((tb, FW), lambda i: (i, 0))
    return pl.pallas_call(
        body, name="head_dot", grid=(T // tb,), in_specs=[row, row, pl.BlockSpec((FW, FW), lambda i: (0, 0))],
        out_specs=row, out_shape=jax.ShapeDtypeStruct((T, FW), F32), compiler_params=_cp("parallel"))(a, b, ones_blk)


def _conv_windows(scr, x_ref, h_ref, first, kw, tb):
    scr[0:8, :] = jnp.where(first, 0.0, h_ref[...])
    scr[8:8 + tb, :] = x_ref[...]
    return [scr[8 - (kw - 1 - k):8 - (kw - 1 - k) + tb, :] for k in range(kw)]


def _halo_prev(tb, tc, off=0):
    return pl.BlockSpec((8, tc), lambda j, i: (jnp.maximum(i * (tb // 8) - 1, 0), j + off))


def _gconv_fwd(rest, w):
    T = rest.shape[0]
    C = 3 * FW
    kw = w.shape[0]
    tb, tc = _div(T, 1280, 8), 512

    def body(x_ref, h_ref, w_ref, o_ref, scr):
        win = _conv_windows(scr, x_ref, h_ref, pl.program_id(1) == 0, kw, tb)
        u = sum(w_ref[k:k + 1, :] * win[k] for k in range(kw))
        o_ref[...] = u * _sig(u)

    return pl.pallas_call(
        body, name="gconv_fwd", grid=(C // tc, T // tb),
        in_specs=[pl.BlockSpec((tb, tc), lambda j, i: (i, j)), _halo_prev(tb, tc), pl.BlockSpec((kw, tc), lambda j, i: (0, j))],
        out_specs=pl.BlockSpec((tb, tc), lambda j, i: (i, j)), out_shape=jax.ShapeDtypeStruct((T, C), F32),
        scratch_shapes=[pltpu.VMEM((tb + 8, tc), F32)], compiler_params=_cp("parallel", "arbitrary"))(rest, rest, w)


def _gconv_bwd_du(rest, w, dy):
    T = rest.shape[0]
    C = 3 * FW
    kw = w.shape[0]
    tb, tc = _div(T, 1280, 8), 512

    def body(x_ref, h_ref, w_ref, dy_ref, du_ref, dw_ref, scr):
        i = pl.program_id(1)
        win = _conv_windows(scr, x_ref, h_ref, i == 0, kw, tb)
        u = sum(w_ref[k:k + 1, :] * win[k] for k in range(kw))
        sg = _sig(u)
        du = dy_ref[...] * sg * (1.0 + u * (1.0 - sg))
        du_ref[...] = du
        part = jnp.concatenate([jnp.sum(du * win[k], axis=0, keepdims=True) for k in range(kw)], axis=0)

        @pl.when(i == 0)
        def _():
            dw_ref[...] = part

        @pl.when(i > 0)
        def _():
            dw_ref[...] += part

    blk = pl.BlockSpec((tb, tc), lambda j, i: (i, j))
    wsp = pl.BlockSpec((kw, tc), lambda j, i: (0, j))
    return pl.pallas_call(
        body, name="gconv_bwd_du", grid=(C // tc, T // tb), in_specs=[blk, _halo_prev(tb, tc), wsp, blk],
        out_specs=[blk, wsp], out_shape=[jax.ShapeDtypeStruct((T, C), F32), jax.ShapeDtypeStruct((kw, C), F32)],
        scratch_shapes=[pltpu.VMEM((tb + 8, tc), F32)], compiler_params=_cp("parallel", "arbitrary"))(rest, rest, w, dy)


def _conv_bwd_dx(du, w, out_dtype, name):
    T, C = du.shape
    kw = w.shape[0]
    tb = _div(T, 1280, 8)
    tc = _div(C, 512)
    nb = T // tb

    def body(x_ref, h_ref, w_ref, o_ref, scr):
        scr[0:tb, :] = x_ref[...]
        scr[tb:tb + 8, :] = jnp.where(pl.program_id(1) == nb - 1, 0.0, h_ref[...])
        acc = sum(w_ref[k:k + 1, :] * scr[kw - 1 - k:kw - 1 - k + tb, :] for k in range(kw))
        o_ref[...] = acc.astype(o_ref.dtype)

    halo = pl.BlockSpec((8, tc), lambda j, i: (jnp.minimum((i + 1) * (tb // 8), T // 8 - 1), j))
    return pl.pallas_call(
        body, name=name, grid=(C // tc, nb),
        in_specs=[pl.BlockSpec((tb, tc), lambda j, i: (i, j)), halo, pl.BlockSpec((kw, tc), lambda j, i: (0, j))],
        out_specs=pl.BlockSpec((tb, tc), lambda j, i: (i, j)), out_shape=jax.ShapeDtypeStruct((T, C), out_dtype),
        scratch_shapes=[pltpu.VMEM((tb + 8, tc), F32)], compiler_params=_cp("parallel", "arbitrary"))(du, du, w)


def _glu_fwd(up, w, b):
    T = up.shape[0]
    kw = w.shape[0]
    tb, tc = _div(T, 1280, 8), 256
    nc = DFF // tc

    def body(xg, hg, xu, hu, wg, wu, bg, bu, o_ref, sg, su):
        first = pl.program_id(1) == 0
        wing = _conv_windows(sg, xg, hg, first, kw, tb)
        winu = _conv_windows(su, xu, hu, first, kw, tb)
        ug = bg[...] + sum(wg[k:k + 1, :] * wing[k] for k in range(kw))
        uu = bu[...] + sum(wu[k:k + 1, :] * winu[k] for k in range(kw))
        o_ref[...] = (ug * _sig(ug) * uu).astype(o_ref.dtype)

    blk = lambda off: pl.BlockSpec((tb, tc), lambda j, i: (i, j + off))
    wsp = lambda off: pl.BlockSpec((kw, tc), lambda j, i: (0, j + off))
    bsp = lambda off: pl.BlockSpec((1, tc), lambda j, i: (0, j + off))
    return pl.pallas_call(
        body, name="glu_fwd", grid=(nc, T // tb),
        in_specs=[blk(0), _halo_prev(tb, tc), blk(nc), _halo_prev(tb, tc, nc), wsp(0), wsp(nc), bsp(0), bsp(nc)],
        out_specs=blk(0), out_shape=jax.ShapeDtypeStruct((T, DFF), _BF),
        scratch_shapes=[pltpu.VMEM((tb + 8, tc), F32)] * 2, compiler_params=_cp("parallel", "arbitrary"))(
            up, up, up, up, w, w, b, b)


def _glu_bwd_du(up, w, b, df):
    T = up.shape[0]
    kw = w.shape[0]
    tb, tc = _div(T, 1280, 8), 256
    nc = DFF // tc

    def body(xg, hg, xu, hu, wg, wu, bg, bu, df_ref, dug_ref, duu_ref, dwg_ref, dwu_ref, dbg_ref, dbu_ref, sg, su):
        i = pl.program_id(1)
        wing = _conv_windows(sg, xg, hg, i == 0, kw, tb)
        winu = _conv_windows(su, xu, hu, i == 0, kw, tb)
        ug = bg[...] + sum(wg[k:k + 1, :] * wing[k] for k in range(kw))
        uu = bu[...] + sum(wu[k:k + 1, :] * winu[k] for k in range(kw))
        s = _sig(ug)
        df = df_ref[...]
        dug = df * uu * s * (1.0 + ug * (1.0 - s))
        duu = df * ug * s
        dug_ref[...] = dug
        duu_ref[...] = duu
        parts = (jnp.concatenate([jnp.sum(dug * wing[k], axis=0, keepdims=True) for k in range(kw)], axis=0),
                 jnp.concatenate([jnp.sum(duu * winu[k], axis=0, keepdims=True) for k in range(kw)], axis=0),
                 jnp.sum(dug, axis=0, keepdims=True), jnp.sum(duu, axis=0, keepdims=True))
        accs = (dwg_ref, dwu_ref, dbg_ref, dbu_ref)

        @pl.when(i == 0)
        def _():
            for r, p in zip(accs, parts):
                r[...] = p

        @pl.when(i > 0)
        def _():
            for r, p in zip(accs, parts):
                r[...] += p

    blk = lambda off: pl.BlockSpec((tb, tc), lambda j, i: (i, j + off))
    wsp = lambda off: pl.BlockSpec((kw, tc), lambda j, i: (0, j + off))
    bsp = lambda off: pl.BlockSpec((1, tc), lambda j, i: (0, j + off))
    return pl.pallas_call(
        body, name="glu_bwd_du", grid=(nc, T // tb),
        in_specs=[blk(0), _halo_prev(tb, tc), blk(nc), _halo_prev(tb, tc, nc), wsp(0), wsp(nc), bsp(0), bsp(nc), blk(0)],
        out_specs=[blk(0), blk(0), wsp(0), wsp(0), bsp(0), bsp(0)],
        out_shape=[jax.ShapeDtypeStruct((T, DFF), F32)] * 2 + [jax.ShapeDtypeStruct((kw, DFF), F32)] * 2
        + [jax.ShapeDtypeStruct((1, DFF), F32)] * 2,
        scratch_shapes=[pltpu.VMEM((tb + 8, tc), F32)] * 2, compiler_params=_cp("parallel", "arbitrary"))(
            up, up, up, up, w, w, b, b, df)


def _mix_fwd(rest, gate_bias, y_fox, y_gdn):
    T = rest.shape[0]
    tb, tc = _div(T, ROW_TILE, 8), 512
    nc = D // tc
    og = OFF_GATES // tc

    def body(gf, gg, bf, bg, yf, yg, o_ref):
        o_ref[...] = (_sig(gf[...] + bf[...]) * yf[...] + _sig(gg[...] + bg[...]) * yg[...]).astype(o_ref.dtype)

    blk = lambda off: pl.BlockSpec((tb, tc), lambda i, j: (i, j + off))
    bsp = lambda off: pl.BlockSpec((1, tc), lambda i, j: (0, j + off))
    return pl.pallas_call(
        body, name="mix_fwd", grid=(T // tb, nc),
        in_specs=[blk(og), blk(og + nc), bsp(0), bsp(nc), blk(0), blk(0)], out_specs=blk(0),
        out_shape=jax.ShapeDtypeStruct((T, D), _BF), compiler_params=_cp("parallel", "parallel"))(
            rest, rest, gate_bias, gate_bias, y_fox, y_gdn)


def _mix_bwd(rest, gate_bias, y_fox, y_gdn, dmix):
    T = rest.shape[0]
    tb, tc = _div(T, ROW_TILE, 8), 512
    nc = D // tc
    og = OFF_GATES // tc

    def body(gf, gg, bf, bg, yf, yg, dm, dyf, dyg, dgf, dgg, dbf, dbg):
        i = pl.program_id(1)
        d = dm[...]
        sf = _sig(gf[...] + bf[...])
        sg = _sig(gg[...] + bg[...])
        dyf[...] = (d * sf).astype(dyf.dtype)
        dyg[...] = (d * sg).astype(dyg.dtype)
        a = d * yf[...] * sf * (1.0 - sf)
        b = d * yg[...] * sg * (1.0 - sg)
        dgf[...] = a
        dgg[...] = b
        pa = jnp.sum(a, axis=0, keepdims=True)
        pb = jnp.sum(b, axis=0, keepdims=True)

        @pl.when(i == 0)
        def _():
            dbf[...] = pa
            dbg[...] = pb

        @pl.when(i > 0)
        def _():
            dbf[...] += pa
            dbg[...] += pb

    blk = lambda off: pl.BlockSpec((tb, tc), lambda j, i: (i, j + off))
    bsp = lambda off: pl.BlockSpec((1, tc), lambda j, i: (0, j + off))
    return pl.pallas_call(
        body, name="mix_bwd", grid=(nc, T // tb),
        in_specs=[blk(og), blk(og + nc), bsp(0), bsp(nc), blk(0), blk(0), blk(0)],
        out_specs=[blk(0), blk(0), blk(0), blk(0), bsp(0), bsp(0)],
        out_shape=[jax.ShapeDtypeStruct((T, D), _BF)] * 2 + [jax.ShapeDtypeStruct((T, D), F32)] * 2
        + [jax.ShapeDtypeStruct((1, D), F32)] * 2,
        compiler_params=_cp("parallel", "arbitrary"))(rest, rest, gate_bias, gate_bias, y_fox, y_gdn, dmix)


def _loss_head(h2, w, target, n_valid):
    T = h2.shape[0]
    tb = _div(T, ROW_TILE, 8)

    def body(h_ref, w_ref, t_ref, dh_ref, loss_ref, dw_ref):
        i = pl.program_id(0)
        x = h_ref[...]
        r = lax.rsqrt(jnp.mean(x * x, axis=-1, keepdims=True) + EPS)
        xh = x * r
        row = i * tb + lax.broadcasted_iota(jnp.int32, (tb, 1), 0)
        valid = (row >= X0) & (row < X0 + n_valid)
        e = jnp.where(valid, xh * w_ref[...] - t_ref[...], 0.0)
        dy = e * (1.0 / D)
        g = dy * w_ref[...]
        dh_ref[...] = r * (g - xh * jnp.mean(xh * g, axis=-1, keepdims=True))
        lpart = 0.5 * jnp.sum(jnp.sum(e * e, axis=-1, keepdims=True) * (1.0 / D), axis=0, keepdims=True)
        wpart = jnp.sum(dy * xh, axis=0, keepdims=True)

        @pl.when(i == 0)
        def _():
            loss_ref[...] = lpart
            dw_ref[...] = wpart

        @pl.when(i > 0)
        def _():
            loss_ref[...] += lpart
            dw_ref[...] += wpart

    row = pl.BlockSpec((tb, D), lambda i: (i, 0))
    one = pl.BlockSpec((1, D), lambda i: (0, 0))
    return pl.pallas_call(
        body, name="loss_head", grid=(T // tb,), in_specs=[row, one, row],
        out_specs=[row, pl.BlockSpec((1, 1), lambda i: (0, 0)), one],
        out_shape=[jax.ShapeDtypeStruct((T, D), F32), jax.ShapeDtypeStruct((1, 1), F32), jax.ShapeDtypeStruct((1, D), F32)],
        compiler_params=_cp("arbitrary"))(h2, w, target)


def _bd_lo(a, b, ca, cb):
    return lax.dot_general(a.astype(_BF), b.astype(_BF), (((ca,), (cb,)), ((0,), (0,))), preferred_element_type=F32)


def _bd_hi(a, b, ca, cb):
    return lax.dot_general(a, b, (((ca,), (cb,)), ((0,), (0,))), preferred_element_type=F32, precision=HI)


def _gdn_chunk(q, k, v, bpre, apre, alog, dtb):
    H = HEADS
    r = lax.broadcasted_iota(jnp.int32, (1, CH, CH), 1)
    c = lax.broadcasted_iota(jnp.int32, (1, CH, CH), 2)
    tril, strict = r >= c, r > c
    lb = jnp.broadcast_to(tril.astype(F32), (H, CH, CH))
    rq = lax.rsqrt(jnp.sum(q * q, axis=-1, keepdims=True) + EPS)
    rk = lax.rsqrt(jnp.sum(k * k, axis=-1, keepdims=True) + EPS)
    qh = q * rq
    qn = qh * (HD ** -0.5)
    kn = k * rk
    beta = _sig(bpre)
    x = apre + dtb
    ea = jnp.exp(alog)
    g = -ea * jax.nn.softplus(x)
    gb = jnp.broadcast_to(g, (H, CH, HD))
    gc = _bd_hi(lb, gb, 2, 1)
    dm = _bd_hi(lb, jnp.where(strict, gb, 0.0), 2, 1)
    decay = jnp.where(tril, jnp.exp(jnp.where(tril, dm, 0.0)), 0.0)
    eg = jnp.exp(gc)
    gl = gc[:, CH - 1:CH, :]
    egl = jnp.exp(gl - gc)
    cd = jnp.exp(gl)
    kb = kn * beta
    vb = v * beta
    kk = _bd_lo(kb, kn, 2, 2)
    a = jnp.where(strict, kk * decay, 0.0)
    tm = jnp.where(r == c, 1.0, 0.0) - a
    p = a
    for _ in range(5):
        p = _bd_hi(p, p, 2, 1)
        tm = tm + _bd_hi(tm, p, 2, 1)
    kbg = kb * eg
    value = _bd_hi(tm, vb, 2, 1)
    kcd = _bd_hi(tm, kbg, 2, 1)
    qk = _bd_lo(qn, kn, 2, 2)
    attn = jnp.where(tril, qk * decay, 0.0)
    return dict(tril=tril, strict=strict, lb=lb, rq=rq, rk=rk, qh=qh, qn=qn, kn=kn, beta=beta, x=x, ea=ea, g=g,
                decay=decay, eg=eg, egl=egl, cd=cd, kb=kb, vb=vb, kk=kk, tm=tm, kbg=kbg, value=value, kcd=kcd, qk=qk,
                attn=attn, qd=qn * eg, kt=kn * egl)


def _gdn_specs(T, rev):
    nc = T // CH
    pos = (lambda n: nc - 1 - n) if rev else (lambda n: n)
    mat = pl.BlockSpec((HEADS, CH, HD), lambda n: (0, pos(n), 0))
    col = pl.BlockSpec((HEADS, CH, 1), lambda n: (0, pos(n), 0))
    sca = pl.BlockSpec((HEADS, 1, 1), lambda n: (0, 0, 0))
    nw = pl.BlockSpec((1, 1, HD), lambda n: (0, 0, 0))
    st = pl.BlockSpec((1, HEADS, HD, HD), lambda n: (pos(n), 0, 0, 0))
    return nc, mat, col, sca, nw, st


def _gdn_fwd(q, k, v, z, bpre, apre, alog, dtb, nw):
    T = q.shape[1]
    nc, mat, col, sca, nws, st = _gdn_specs(T, False)

    def body(q_ref, k_ref, v_ref, z_ref, b_ref, a_ref, al_ref, dt_ref, nw_ref, o_ref, og_ref, st_ref, s_scr):
        @pl.when(pl.program_id(0) == 0)
        def _():
            s_scr[...] = jnp.zeros_like(s_scr)

        L = _gdn_chunk(q_ref[...], k_ref[...], v_ref[...], b_ref[...], a_ref[...], al_ref[...], dt_ref[...])
        s = s_scr[...]
        st_ref[0] = s
        v_new = L["value"] - _bd_lo(L["kcd"], s, 2, 1)
        o = _bd_lo(L["qd"], s, 2, 1) + _bd_lo(L["attn"], v_new, 2, 1)
        s_scr[...] = s * L["cd"] + _bd_lo(L["kt"], v_new, 1, 1)
        o_ref[...] = o
        zz = z_ref[...]
        rs = lax.rsqrt(jnp.mean(o * o, axis=-1, keepdims=True) + EPS)
        og_ref[...] = o * rs * nw_ref[...] * zz * _sig(zz)

    return pl.pallas_call(
        body, name="gdn_fwd", grid=(nc,), in_specs=[mat, mat, mat, mat, col, col, sca, sca, nws],
        out_specs=[mat, mat, st],
        out_shape=[jax.ShapeDtypeStruct((HEADS, T, HD), F32)] * 2 + [jax.ShapeDtypeStruct((nc, HEADS, HD, HD), F32)],
        scratch_shapes=[pltpu.VMEM((HEADS, HD, HD), F32)], compiler_params=_cp("arbitrary"))(
            q, k, v, z, bpre, apre, alog, dtb, nw)


def _gdn_bwd(q, k, v, z, bpre, apre, alog, dtb, nw, states, o, dog):
    T = q.shape[1]
    nc, mat, col, sca, nws, st = _gdn_specs(T, True)

    def body(q_ref, k_ref, v_ref, z_ref, b_ref, a_ref, al_ref, dt_ref, nw_ref, st_ref, o_ref, dog_ref,
             dq_ref, dk_ref, dv_ref, dz_ref, db_ref, da_ref, dal_ref, ddt_ref, dnw_ref, ds_scr):
        @pl.when(pl.program_id(0) == 0)
        def _():
            ds_scr[...] = jnp.zeros_like(ds_scr)
            dal_ref[...] = jnp.zeros_like(dal_ref)
            ddt_ref[...] = jnp.zeros_like(ddt_ref)
            dnw_ref[...] = jnp.zeros_like(dnw_ref)

        vv = v_ref[...]
        L = _gdn_chunk(q_ref[...], k_ref[...], vv, b_ref[...], a_ref[...], al_ref[...], dt_ref[...])
        tril, strict, lb = L["tril"], L["strict"], L["lb"]
        qn, kn, kb, beta, decay, eg, egl, cd = L["qn"], L["kn"], L["kb"], L["beta"], L["decay"], L["eg"], L["egl"], L["cd"]
        value, kcd, attn, qd, kt, tm = L["value"], L["kcd"], L["attn"], L["qd"], L["kt"], L["tm"]
        s = st_ref[0]
        v_new = value - _bd_lo(kcd, s, 2, 1)
        oo, zz, dog_ = o_ref[...], z_ref[...], dog_ref[...]
        sz = _sig(zz)
        rs = lax.rsqrt(jnp.mean(oo * oo, axis=-1, keepdims=True) + EPS)
        oh = oo * rs
        dz_ref[...] = dog_ * oh * nw_ref[...] * sz * (1.0 + zz * (1.0 - sz))
        don = dog_ * zz * sz
        gdy = don * nw_ref[...]
        do = rs * (gdy - oh * jnp.mean(oh * gdy, axis=-1, keepdims=True))
        dnw_ref[...] += jnp.sum(don * oh, axis=(0, 1), keepdims=True)
        dsn = ds_scr[...]
        d_vnew = _bd_lo(attn, do, 1, 1) + _bd_lo(kt, dsn, 2, 1)
        d_qd = _bd_lo(do, s, 2, 2)
        d_attn = jnp.where(tril, _bd_lo(do, v_new, 2, 2), 0.0)
        d_kt = _bd_lo(v_new, dsn, 2, 2)
        d_cd = jnp.sum(s * dsn, axis=(1, 2), keepdims=True)
        ds_scr[...] = _bd_lo(qd, do, 1, 1) + cd * dsn - _bd_lo(kcd, d_vnew, 1, 1)
        d_kcd = -_bd_lo(d_vnew, s, 2, 2)
        d_vb = _bd_hi(tm, d_vnew, 1, 1)
        d_kbg = _bd_hi(tm, d_kcd, 1, 1)
        da = -jnp.where(strict, _bd_lo(d_vb, value, 2, 2) + _bd_lo(d_kbg, kcd, 2, 2), 0.0)
        dkk = da * decay
        dqk = d_attn * decay
        d_decay = da * L["kk"] + d_attn * L["qk"]
        d_kb = _bd_lo(dkk, kn, 2, 1) + d_kbg * eg
        d_kn = _bd_lo(dkk, kb, 1, 1) + _bd_lo(dqk, qn, 1, 1) + d_kt * egl
        d_qn = _bd_lo(dqk, kn, 2, 1) + d_qd * eg
        dmm = _bd_hi(lb, d_decay * decay, 1, 1)
        dg = jnp.sum(jnp.where(strict, dmm, 0.0), axis=-1, keepdims=True)
        d_gc = jnp.sum(d_qd * qd + d_kbg * L["kbg"] - d_kt * kt, axis=-1, keepdims=True)
        d_gl = jnp.sum(d_kt * kt, axis=(1, 2), keepdims=True) + d_cd * cd[:, :, 0:1]
        last = lax.broadcasted_iota(jnp.int32, (1, CH, 1), 1) == CH - 1
        d_gc = d_gc + jnp.where(last, d_gl, 0.0)
        dg = dg + _bd_hi(lb, jnp.broadcast_to(d_gc, (HEADS, CH, HD)), 1, 1)[:, :, 0:1]
        d_apre = -dg * L["ea"] * _sig(L["x"])
        da_ref[...] = d_apre
        dal_ref[...] += jnp.sum(dg * L["g"], axis=1, keepdims=True)
        ddt_ref[...] += jnp.sum(d_apre, axis=1, keepdims=True)
        d_beta = jnp.sum(d_kb * kn + d_vb * vv, axis=-1, keepdims=True)
        db_ref[...] = d_beta * beta * (1.0 - beta)
        d_kn = d_kn + d_kb * beta
        dv_ref[...] = d_vb * beta
        qh = L["qh"]
        dq_ref[...] = (HD ** -0.5) * L["rq"] * (d_qn - qh * jnp.sum(qh * d_qn, axis=-1, keepdims=True))
        dk_ref[...] = L["rk"] * (d_kn - kn * jnp.sum(kn * d_kn, axis=-1, keepdims=True))

    m3 = jax.ShapeDtypeStruct((HEADS, T, HD), F32)
    c3 = jax.ShapeDtypeStruct((HEADS, T, 1), F32)
    s3 = jax.ShapeDtypeStruct((HEADS, 1, 1), F32)
    return pl.pallas_call(
        body, name="gdn_bwd", grid=(nc,),
        in_specs=[mat, mat, mat, mat, col, col, sca, sca, nws, st, mat, mat],
        out_specs=[mat, mat, mat, mat, col, col, sca, sca, nws],
        out_shape=[m3, m3, m3, m3, c3, c3, s3, s3, jax.ShapeDtypeStruct((1, 1, HD), F32)],
        scratch_shapes=[pltpu.VMEM((HEADS, HD, HD), F32)], compiler_params=_cp("arbitrary"))(
            q, k, v, z, bpre, apre, alog, dtb, nw, states, o, dog)


def _heads_major(a):
    T = a.shape[0]
    return a.reshape(T, -1, HD).transpose(1, 0, 2)


def _tokens_major(a):
    return a.transpose(1, 0, 2).reshape(a.shape[1], -1)


def _local_step(x, target, w):
    seq = x.shape[0]
    T = -(-(X0 + seq) // T_ALIGN) * T_ALIGN
    back = T - X0 - seq
    tb_att = _div(T, ATT_TILE)
    h0 = jnp.concatenate([jnp.zeros((FRONT, D), F32), w["meta"], x, jnp.zeros((back, D), F32)], axis=0)
    tgt = jnp.concatenate([jnp.zeros((X0, D), F32), target, jnp.zeros((back, D), F32)], axis=0)
    row = lambda v: v.reshape(1, -1)
    nmix, nffn, nfin = row(w["norm_mix"]), row(w["norm_ffn"]), row(w["norm_final"])
    gate_b = row(w["gate_bias"])
    fconv_b = row(w["ffn_conv_b"])
    bias128 = jnp.zeros((1, SMALL_W), F32).at[0, :HEADS].set(w["fgt_bias"])
    alog = w["a_log"].reshape(HEADS, 1, 1)
    dtb = w["dt_bias"].reshape(HEADS, 1, 1)
    gnw = w["gdn_norm"].reshape(1, 1, HD)

    a = _rms_fwd(h0, nmix, "rms_mix")
    pf = _mm(a, w["w_fox"], out_dtype=_BF, name="proj_fox")
    rest = _mm(a, w["w_rest"], name="proj_rest")
    small = rest[:, OFF_SMALL:]
    c8 = _fgate_fwd(small, bias128)[:, :HEADS]
    c_cols, c_rows = _att_cols(c8), _att_rows(c8, tb_att)
    o_fox, lse = _fox_fwd(pf, c_cols, c_rows)
    conv = _gconv_fwd(rest, w["gdn_conv"])
    qkvh = _heads_major(conv)
    qh, kh, vh = qkvh[:HEADS], qkvh[HEADS:2 * HEADS], qkvh[2 * HEADS:]
    zh = _heads_major(rest[:, OFF_Z:OFF_Z + FW])
    bpre = small[:, HEADS:2 * HEADS].T[:, :, None]
    apre = small[:, 2 * HEADS:3 * HEADS].T[:, :, None]
    o_raw, ogh, states = _gdn_fwd(qh, kh, vh, zh, bpre, apre, alog, dtb, gnw)
    og = _tokens_major(ogh)
    y_fox = _mm(o_fox, w["w_bfox"], name="y_fox")
    y_gdn = _mm(og, w["w_bgdn"], name="y_gdn")
    mix = _mix_fwd(rest, gate_b, y_fox, y_gdn)
    h1 = _mm(mix, w["w_out"], resid=h0, name="out_proj")
    b = _rms_fwd(h1, nffn, "rms_ffn")
    up = _mm(b, w["w_up"], name="ffn_up")
    f = _glu_fwd(up, w["ffn_conv"], fconv_b)
    h2 = _mm(f, w["w_down"], resid=h1, name="ffn_down")
    dh2, loss, d_nfin = _loss_head(h2, nfin, tgt, seq)

    d_f = _mm(dh2, w["w_down"], tb=True, name="d_f")
    g_down = _mm(f, dh2, ta=True, name="g_down")
    dug, duu, dwg, dwu, dbg, dbu = _glu_bwd_du(up, w["ffn_conv"], fconv_b, d_f)
    dxg = _conv_bwd_dx(dug, w["ffn_conv"][:, :DFF], _BF, "fconv_dx_gate")
    dxu = _conv_bwd_dx(duu, w["ffn_conv"][:, DFF:], _BF, "fconv_dx_up")
    d_b = _mm(dxg, w["w_up"][:, :DFF], tb=True, name="d_b_gate")
    d_b = _mm(dxu, w["w_up"][:, DFF:], tb=True, resid=d_b, name="d_b_up")
    g_up = jnp.concatenate([_mm(b, dxg, ta=True, name="g_up_gate"), _mm(b, dxu, ta=True, name="g_up_up")], axis=1)
    dh1, d_nffn = _rms_bwd(h1, nffn, d_b, dh2, "rms_ffn_bwd")

    dmix = _mm(dh1, w["w_out"], tb=True, name="d_mix")
    g_out = _mm(mix, dh1, ta=True, name="g_out")
    dyf, dyg, dgf, dgg, dgbf, dgbg = _mix_bwd(rest, gate_b, y_fox, y_gdn, dmix)
    do_fox = _mm(dyf, w["w_bfox"], tb=True, out_dtype=_BF, name="d_o_fox")
    g_bfox = _mm(o_fox, dyf, ta=True, name="g_bfox")
    d_og = _mm(dyg, w["w_bgdn"], tb=True, name="d_o_gdn")
    g_bgdn = _mm(og, dyg, ta=True, name="g_bgdn")

    ones_blk = jnp.asarray(np.kron(np.eye(HEADS, dtype=np.float32), np.ones((HD, HD), np.float32)))
    delta = _head_dot(do_fox, o_fox, ones_blk)
    dq, dcq = _fox_bwd_dq(pf, do_fox, c_cols, c_rows, lse, delta)
    dk, dv, dck = _fox_bwd_dkv(pf, do_fox, c_cols, c_rows, _att_rows(lse[:, ::HD], tb_att), _att_rows(delta[:, ::HD], tb_att))
    dc = jnp.pad((dcq + dck)[:, ::HD], ((0, 0), (0, SMALL_W - HEADS)))
    dfp, d_fb = _fgate_bwd(small, bias128, dc)

    dqh, dkh, dvh, dzh, dbp, dap, d_alog, d_dtb, d_gnw = _gdn_bwd(
        qh, kh, vh, zh, bpre, apre, alog, dtb, gnw, states, o_raw, _heads_major(d_og))
    dconv = _tokens_major(jnp.concatenate([dqh, dkh, dvh], axis=0))
    du_g, g_gconv = _gconv_bwd_du(rest, w["gdn_conv"], dconv)
    dgx = _conv_bwd_dx(du_g, w["gdn_conv"], F32, "gconv_dx")
    dsmall = jnp.concatenate([dfp[:, :HEADS], dbp[:, :, 0].T, dap[:, :, 0].T,
                              jnp.zeros((T, SMALL_W - 3 * HEADS), F32)], axis=1)
    drest = jnp.concatenate([dgx, _tokens_major(dzh), dgf, dgg, dsmall], axis=1)
    dfox = jnp.concatenate([dq, dk, dv], axis=1)
    d_a = _mm(dfox, w["w_fox"], tb=True, name="d_a_fox")
    d_a = _mm(drest, w["w_rest"], tb=True, resid=d_a, name="d_a_rest")
    g_fox = _mm(a, dfox, ta=True, name="g_w_fox")
    g_rest = _mm(a, drest, ta=True, name="g_w_rest")
    dh0, d_nmix = _rms_bwd(h0, nmix, d_a, dh1, "rms_mix_bwd")

    sm = lambda lo: g_rest[:, OFF_SMALL + lo:OFF_SMALL + lo + HEADS]
    g_w_in = jnp.concatenate([g_fox, sm(0), g_rest[:, :3 * FW], g_rest[:, OFF_Z:OFF_Z + FW], sm(HEADS), sm(2 * HEADS),
                              g_rest[:, OFF_GATES:OFF_GATES + 2 * D]], axis=1)
    grads = dict(
        meta_tokens=dh0[FRONT:X0], w_in=g_w_in, fgt_bias=d_fb[0, :HEADS], gdn_conv_w=g_gconv,
        gdn_a_log=d_alog.reshape(HEADS), gdn_dt_bias=d_dtb.reshape(HEADS), gdn_norm_w=d_gnw.reshape(HD),
        gate_bias=jnp.concatenate([dgbf, dgbg], axis=1).reshape(2 * D), w_branch_fox=g_bfox, w_branch_gdn=g_bgdn,
        w_out=g_out, norm_mix_w=d_nmix.reshape(D), norm_ffn_w=d_nffn.reshape(D), ffn_w_up=g_up,
        ffn_conv_w=jnp.concatenate([dwg, dwu], axis=1), ffn_conv_b=jnp.concatenate([dbg, dbu], axis=1).reshape(2 * DFF),
        ffn_w_down=g_down, norm_final_w=d_nfin.reshape(D))
    return loss, dh0[X0:X0 + seq], grads


N_CHIPS = 4
PACK_W = 1024
PACK_ROW_ALIGN = 512
WEIGHTS = (
    ("meta_tokens", (N_META, D), 1), ("w_in", (D, 3 * FW + HEADS + 4 * FW + 2 * HEADS + 2 * D), 1), ("fgt_bias", (1, HEADS), None),
    ("gdn_conv_w", (4, 3 * FW), 1), ("gdn_a_log", (1, HEADS), None), ("gdn_dt_bias", (1, HEADS), None),
    ("gdn_norm_w", (1, HD), None), ("gate_bias", (1, 2 * D), None), ("w_branch_fox", (FW, D), 1),
    ("w_branch_gdn", (FW, D), 1), ("w_out", (D, D), 0), ("norm_mix_w", (1, D), None), ("norm_ffn_w", (1, D), None),
    ("ffn_w_up", (D, 2 * DFF), 1), ("ffn_conv_w", (3, 2 * DFF), 1), ("ffn_conv_b", (1, 2 * DFF), None),
    ("ffn_w_down", (DFF, D), 0), ("norm_final_w", (1, D), None))
SPLIT_F32 = ("meta_tokens", "gdn_conv_w", "ffn_conv_w")


def _shard_shape(shape, axis):
    if axis is None:
        return shape
    return tuple(s // N_CHIPS if a == axis else s for a, s in enumerate(shape))


def _shard_of(full, axis, q):
    if axis is None:
        return full
    n = full.shape[axis] // N_CHIPS
    return lax.slice_in_dim(full, q * n, (q + 1) * n, axis=axis)


def _pack_rows(n_elems):
    rows = -(-n_elems // PACK_W)
    return -(-rows // PACK_ROW_ALIGN) * PACK_ROW_ALIGN


def _pack(pieces, dtype):
    flat = jnp.concatenate([p.reshape(-1).astype(dtype) for p in pieces])
    rows = _pack_rows(flat.shape[0])
    return jnp.pad(flat, (0, rows * PACK_W - flat.shape[0])).reshape(rows, PACK_W)


def _unpack(slab, shapes):
    flat = slab.reshape(-1)
    out, off = [], 0
    for s in shapes:
        n = int(np.prod(s))
        out.append(flat[off:off + n].reshape(s))
        off += n
    return out


HBM_SPEC = pl.BlockSpec(memory_space=pltpu.HBM)
MESH_ID = pl.DeviceIdType.MESH


def _chip_exchange(src, name, scatter):
    def body(src_ref, out_ref, send_sems, recv_sems, local_sem):
        x, y, c = lax.axis_index("x"), lax.axis_index("y"), lax.axis_index("c")
        q = 2 * x + y
        peers = [(1 - x, y), (x, 1 - y), (1 - x, 1 - y)]

        def remote(k, src_slot, dst_slot):
            px, py = peers[k]
            return pltpu.make_async_remote_copy(
                src_ref=src_ref.at[src_slot], dst_ref=out_ref.at[dst_slot], send_sem=send_sems.at[k],
                recv_sem=recv_sems.at[k], device_id=(px, py, c), device_id_type=MESH_ID)

        mine = pltpu.make_async_copy(src_ref.at[q if scatter else 0], out_ref.at[q], local_sem)
        mine.start()
        sends = [remote(k, 2 * px + py if scatter else 0, q) for k, (px, py) in enumerate(peers)]
        for cp in sends:
            cp.start()
        for k, (px, py) in enumerate(peers):
            remote(k, 0, 2 * px + py).wait_recv()
        for cp in sends:
            cp.wait_send()
        mine.wait()

    out_shape = jax.ShapeDtypeStruct((N_CHIPS,) + src.shape[1:], src.dtype)
    return pl.pallas_call(
        body, name=name, in_specs=[HBM_SPEC], out_specs=HBM_SPEC, out_shape=out_shape,
        scratch_shapes=[pltpu.SemaphoreType.DMA((3,)), pltpu.SemaphoreType.DMA((3,)), pltpu.SemaphoreType.DMA(())],
    )(src)


def _sibling_swap(slab, name):
    def body(src_ref, out_ref, send_sem, recv_sem):
        x, y, c = lax.axis_index("x"), lax.axis_index("y"), lax.axis_index("c")
        cp = pltpu.make_async_remote_copy(src_ref=src_ref, dst_ref=out_ref, send_sem=send_sem, recv_sem=recv_sem,
                                          device_id=(x, y, 1 - c), device_id_type=MESH_ID)
        cp.start()
        cp.wait_recv()
        cp.wait_send()

    return pl.pallas_call(
        body, name=name, in_specs=[HBM_SPEC], out_specs=HBM_SPEC, out_shape=jax.ShapeDtypeStruct(slab.shape, slab.dtype),
        scratch_shapes=[pltpu.SemaphoreType.DMA(()), pltpu.SemaphoreType.DMA(())])(slab)


def _sum_slabs(r):
    rows = r.shape[1]
    tb = _div(rows, 512, 8)

    def body(r0, r1, r2, r3, o_ref):
        o_ref[...] = ((r0[0] + r1[0]) + r2[0]) + r3[0]

    spec = lambda j: pl.BlockSpec((1, tb, PACK_W), lambda i: (j, i, 0))
    return pl.pallas_call(
        body, name="sum_slabs", grid=(rows // tb,), in_specs=[spec(0), spec(1), spec(2), spec(3)],
        out_specs=pl.BlockSpec((tb, PACK_W), lambda i: (i, 0)), out_shape=jax.ShapeDtypeStruct((rows, PACK_W), F32),
        compiler_params=_cp("parallel"))(r, r, r, r)


def _adamw(w, m, v, p, q):
    rows = w.shape[0]
    tb = _div(rows, 512, 8)

    def body(w_ref, m_ref, v_ref, p_ref, q_ref, g_ref, d_ref, nm_ref, nv_ref):
        g = p_ref[...] + q_ref[...]
        m_new = B1 * m_ref[...] + (1.0 - B1) * g
        v_new = B2 * v_ref[...] + (1.0 - B2) * (g * g)
        g_ref[...] = g
        nm_ref[...] = m_new
        nv_ref[...] = v_new
        m_hat = m_new / (1.0 - B1 ** STEP)
        v_hat = v_new / (1.0 - B2 ** STEP)
        d_ref[...] = -LR * (m_hat / (jnp.sqrt(v_hat) + AEPS) + WD * w_ref[...])

    spec = pl.BlockSpec((tb, PACK_W), lambda i: (i, 0))
    return pl.pallas_call(
        body, name="adamw", grid=(rows // tb,), in_specs=[spec] * 5, out_specs=[spec] * 4,
        out_shape=[jax.ShapeDtypeStruct((rows, PACK_W), F32)] * 4, compiler_params=_cp("parallel"))(w, m, v, p, q)


def _split_w_in(w_in):
    o1 = 3 * FW
    o2 = o1 + HEADS
    o3 = o2 + 3 * FW
    o4 = o3 + FW
    o5 = o4 + HEADS
    o6 = o5 + HEADS
    pad = jnp.zeros((w_in.shape[0], SMALL_W - 3 * HEADS), w_in.dtype)
    rest = jnp.concatenate([w_in[:, o2:o3], w_in[:, o3:o4], w_in[:, o6:], w_in[:, o1:o2], w_in[:, o4:o5], w_in[:, o5:o6], pad],
                           axis=1)
    return w_in[:, :o1], rest


def _gather_weights(shards):
    sharded = [(n, s, a) for n, s, a in WEIGHTS if a is not None]
    hi = {n: shards[n].astype(jnp.bfloat16) for n, _, _ in sharded}
    lo = [(shards[n] - hi[n].astype(F32)).astype(jnp.bfloat16) for n in SPLIT_F32]
    slab = _pack([hi[n] for n, _, _ in sharded] + lo, jnp.bfloat16)
    got = _chip_exchange(slab[None], "gather_weights", scatter=False)
    shapes = [_shard_shape(s, a) for _, s, a in sharded] + [_shard_shape(dict((n, s) for n, s, _ in WEIGHTS)[n], 1) for n in SPLIT_F32]
    per_chip = [_unpack(got[j], shapes) for j in range(N_CHIPS)]
    full = {}
    for i, (n, _, a) in enumerate(sharded):
        full[n] = jnp.concatenate([per_chip[j][i] for j in range(N_CHIPS)], axis=a)
    for i, n in enumerate(SPLIT_F32):
        lo_full = jnp.concatenate([per_chip[j][len(sharded) + i] for j in range(N_CHIPS)], axis=1)
        full[n] = full[n].astype(F32) + lo_full.astype(F32)
    return full


def kernel(x, meta_tokens, w_in, fgt_bias, gdn_conv_w, gdn_a_log, gdn_dt_bias, gdn_norm_w, gate_bias, w_branch_fox, w_branch_gdn, w_out, norm_mix_w, norm_ffn_w, ffn_w_up, ffn_conv_w, ffn_conv_b, ffn_w_down, norm_final_w, loss_target, m_meta_tokens, m_w_in, m_fgt_bias, m_gdn_conv_w, m_gdn_a_log, m_gdn_dt_bias, m_gdn_norm_w, m_gate_bias, m_w_branch_fox, m_w_branch_gdn, m_w_out, m_norm_mix_w, m_norm_ffn_w, m_ffn_w_up, m_ffn_conv_w, m_ffn_conv_b, m_ffn_w_down, m_norm_final_w, v_meta_tokens, v_w_in, v_fgt_bias, v_gdn_conv_w, v_gdn_a_log, v_gdn_dt_bias, v_gdn_norm_w, v_gate_bias, v_w_branch_fox, v_w_branch_gdn, v_w_out, v_norm_mix_w, v_norm_ffn_w, v_ffn_w_up, v_ffn_conv_w, v_ffn_conv_b, v_ffn_w_down, v_norm_final_w):
    weights = dict(meta_tokens=meta_tokens, w_in=w_in, fgt_bias=fgt_bias, gdn_conv_w=gdn_conv_w, gdn_a_log=gdn_a_log, gdn_dt_bias=gdn_dt_bias, gdn_norm_w=gdn_norm_w, gate_bias=gate_bias, w_branch_fox=w_branch_fox, w_branch_gdn=w_branch_gdn, w_out=w_out, norm_mix_w=norm_mix_w, norm_ffn_w=norm_ffn_w, ffn_w_up=ffn_w_up, ffn_conv_w=ffn_conv_w, ffn_conv_b=ffn_conv_b, ffn_w_down=ffn_w_down, norm_final_w=norm_final_w)
    m_in = dict(meta_tokens=m_meta_tokens, w_in=m_w_in, fgt_bias=m_fgt_bias, gdn_conv_w=m_gdn_conv_w, gdn_a_log=m_gdn_a_log, gdn_dt_bias=m_gdn_dt_bias, gdn_norm_w=m_gdn_norm_w, gate_bias=m_gate_bias, w_branch_fox=m_w_branch_fox, w_branch_gdn=m_w_branch_gdn, w_out=m_w_out, norm_mix_w=m_norm_mix_w, norm_ffn_w=m_norm_ffn_w, ffn_w_up=m_ffn_w_up, ffn_conv_w=m_ffn_conv_w, ffn_conv_b=m_ffn_conv_b, ffn_w_down=m_ffn_w_down, norm_final_w=m_norm_final_w)
    v_in = dict(meta_tokens=v_meta_tokens, w_in=v_w_in, fgt_bias=v_fgt_bias, gdn_conv_w=v_gdn_conv_w, gdn_a_log=v_gdn_a_log, gdn_dt_bias=v_gdn_dt_bias, gdn_norm_w=v_gdn_norm_w, gate_bias=v_gate_bias, w_branch_fox=v_w_branch_fox, w_branch_gdn=v_w_branch_gdn, w_out=v_w_out, norm_mix_w=v_norm_mix_w, norm_ffn_w=v_norm_ffn_w, ffn_w_up=v_ffn_w_up, ffn_conv_w=v_ffn_conv_w, ffn_conv_b=v_ffn_conv_b, ffn_w_down=v_ffn_w_down, norm_final_w=v_norm_final_w)
    shard2d = {n: _shard_shape(s, a) for n, s, a in WEIGHTS}
    as2d = lambda d: {n: d[n].reshape(shard2d[n]) for n, _, _ in WEIGHTS}
    w2, m2, v2 = as2d(weights), as2d(m_in), as2d(v_in)

    full = _gather_weights(w2)
    w_fox, w_rest = _split_w_in(full["w_in"])
    flat = lambda n: w2[n].reshape(-1)
    local_w = dict(
        meta=full["meta_tokens"], w_fox=w_fox, w_rest=w_rest, fgt_bias=flat("fgt_bias"), gdn_conv=full["gdn_conv_w"],
        a_log=flat("gdn_a_log"), dt_bias=flat("gdn_dt_bias"), gdn_norm=flat("gdn_norm_w"), gate_bias=flat("gate_bias"),
        w_bfox=full["w_branch_fox"], w_bgdn=full["w_branch_gdn"], w_out=full["w_out"], norm_mix=flat("norm_mix_w"),
        norm_ffn=flat("norm_ffn_w"), w_up=full["ffn_w_up"], ffn_conv=full["ffn_conv_w"], ffn_conv_b=flat("ffn_conv_b"),
        w_down=full["ffn_w_down"], norm_final=flat("norm_final_w"))

    loss, grad_x, grads = _local_step(x[0], loss_target[0], local_w)

    g2 = {n: grads[n].reshape(s) for n, s, _ in WEIGHTS}
    send = jnp.stack([_pack([_shard_of(g2[n], a, j) for n, _, a in WEIGHTS], F32) for j in range(N_CHIPS)])
    part = _sum_slabs(_chip_exchange(send, "scatter_grads", scatter=True))
    other = _sibling_swap(part, "swap_grads")
    order = [n for n, _, _ in WEIGHTS]
    slab = lambda d: _pack([d[n] for n in order], F32)
    outs = _adamw(slab(w2), slab(m2), slab(v2), part, other)
    shapes = [weights[n].shape for n in order]
    g_out, d_out, m_out, v_out = [_unpack(o, shapes) for o in outs]
    total = lax.psum(loss[0, 0], ("x", "y", "c"))
    return (total, grad_x[None], *g_out, *d_out, *m_out, *v_out)
```

```python
import functools

import numpy as np
import jax
import jax.numpy as jnp
from jax import lax
from jax.experimental import pallas as pl
from jax.experimental.pallas import tpu as pltpu

F32 = jnp.float32
_BF = jnp.bfloat16
HI = lax.Precision.HIGHEST

D = 1024
N_META = 16
CH = 64
FRONT = CH - N_META
X0 = CH
HEADS = 8
HD = 64
FW = HEADS * HD
DFF = 2816
EPS = 1e-6
NEG = -1e30
T_ALIGN = 256
SMALL_W = 128
REST_W = 3 * FW + FW + 2 * D + SMALL_W
OFF_Z = 3 * FW
OFF_GATES = 4 * FW
OFF_SMALL = 4 * FW + 2 * D

LR, B1, B2, AEPS, WD, STEP = 0.001, 0.9, 0.999, 1e-08, 0.01, 10

VMEM_LIMIT = 56 * 1024 * 1024
ROW_TILE = 640
MM_TM, MM_TN, MM_TK = 1280, 512, 2816
ATT_TILE = 640


def _div(n, target, mult=128):
    if n <= target:
        return n
    best = None
    for d in range(mult, target + 1, mult):
        if n % d == 0:
            best = d
    assert best is not None, (n, target, mult)
    return best


def _cp(*sem):
    return pltpu.CompilerParams(dimension_semantics=sem, vmem_limit_bytes=VMEM_LIMIT)


def _sig(x):
    return 1.0 / (1.0 + jnp.exp(-x))


def _mm(a, b, *, ta=False, tb=False, out_dtype=F32, resid=None, name, tm=MM_TM, tn=MM_TN, tk=MM_TK):
    K, M = a.shape if ta else a.shape[::-1]
    N = b.shape[0] if tb else b.shape[1]
    assert (b.shape[1] if tb else b.shape[0]) == K
    tm, tn, tk = _div(M, tm), _div(N, tn), _div(K, tk)
    nk = K // tk
    dims = (((0 if ta else 1,), (1 if tb else 0,)), ((), ()))
    mxu = _BF

    def body(*refs):
        if resid is None:
            a_ref, b_ref, o_ref, acc = refs
            r_ref = None
        else:
            a_ref, b_ref, r_ref, o_ref, acc = refs
        k = pl.program_id(2)
        part = lax.dot_general(a_ref[...].astype(mxu), b_ref[...].astype(mxu), dims, preferred_element_type=F32)

        @pl.when(k == 0)
        def _():
            acc[...] = part

        @pl.when(k > 0)
        def _():
            acc[...] += part

        @pl.when(k == nk - 1)
        def _():
            r = acc[...]
            if r_ref is not None:
                r = r + r_ref[...]
            o_ref[...] = r.astype(o_ref.dtype)

    a_spec = pl.BlockSpec((tk, tm), lambda i, j, k: (k, i)) if ta else pl.BlockSpec((tm, tk), lambda i, j, k: (i, k))
    b_spec = pl.BlockSpec((tn, tk), lambda i, j, k: (j, k)) if tb else pl.BlockSpec((tk, tn), lambda i, j, k: (k, j))
    o_spec = pl.BlockSpec((tm, tn), lambda i, j, k: (i, j))
    in_specs = [a_spec, b_spec] + ([o_spec] if resid is not None else [])
    args = (a, b) + ((resid,) if resid is not None else ())
    return pl.pallas_call(
        body, name=name, grid=(M // tm, N // tn, nk), in_specs=in_specs, out_specs=o_spec,
        out_shape=jax.ShapeDtypeStruct((M, N), out_dtype), scratch_shapes=[pltpu.VMEM((tm, tn), F32)],
        compiler_params=_cp("parallel", "parallel", "arbitrary"))(*args)


def _rms_fwd(h, w, name):
    T = h.shape[0]
    tb = _div(T, ROW_TILE, 8)

    def body(h_ref, w_ref, o_ref):
        x = h_ref[...]
        r = lax.rsqrt(jnp.mean(x * x, axis=-1, keepdims=True) + EPS)
        o_ref[...] = (x * r * w_ref[...]).astype(o_ref.dtype)

    return pl.pallas_call(
        body, name=name, grid=(T // tb,),
        in_specs=[pl.BlockSpec((tb, D), lambda i: (i, 0)), pl.BlockSpec((1, D), lambda i: (0, 0))],
        out_specs=pl.BlockSpec((tb, D), lambda i: (i, 0)), out_shape=jax.ShapeDtypeStruct((T, D), _BF),
        compiler_params=_cp("parallel"))(h, w)


def _rms_bwd(h, w, dy, dres, name):
    T = h.shape[0]
    tb = _div(T, ROW_TILE, 8)

    def body(h_ref, w_ref, dy_ref, dr_ref, dh_ref, dw_ref):
        x = h_ref[...]
        r = lax.rsqrt(jnp.mean(x * x, axis=-1, keepdims=True) + EPS)
        xh = x * r
        dy = dy_ref[...]
        g = dy * w_ref[...]
        dh_ref[...] = dr_ref[...] + r * (g - xh * jnp.mean(xh * g, axis=-1, keepdims=True))
        part = jnp.sum(dy * xh, axis=0, keepdims=True)

        @pl.when(pl.program_id(0) == 0)
        def _():
            dw_ref[...] = part

        @pl.when(pl.program_id(0) > 0)
        def _():
            dw_ref[...] += part

    row = pl.BlockSpec((tb, D), lambda i: (i, 0))
    one = pl.BlockSpec((1, D), lambda i: (0, 0))
    return pl.pallas_call(
        body, name=name, grid=(T // tb,), in_specs=[row, one, row, row], out_specs=[row, one],
        out_shape=[jax.ShapeDtypeStruct((T, D), F32), jax.ShapeDtypeStruct((1, D), F32)],
        compiler_params=_cp("arbitrary"))(h, w, dy, dres)


def _fgate_fwd(small, bias):
    T = small.shape[0]
    tb = _div(T, ROW_TILE, 8)

    def body(s_ref, b_ref, c_ref, carry):
        @pl.when(pl.program_id(0) == 0)
        def _():
            carry[...] = jnp.zeros_like(carry)

        lf = jax.nn.log_sigmoid(s_ref[...] + b_ref[...])
        r = lax.broadcasted_iota(jnp.int32, (tb, tb), 0)
        c = lax.broadcasted_iota(jnp.int32, (tb, tb), 1)
        tri = (r >= c).astype(F32)
        cs = jnp.dot(tri, lf, precision=HI, preferred_element_type=F32) + carry[...]
        c_ref[...] = cs
        carry[...] = cs[tb - 1:tb, :]

    return pl.pallas_call(
        body, name="fgate_fwd", grid=(T // tb,),
        in_specs=[pl.BlockSpec((tb, SMALL_W), lambda i: (i, 0)), pl.BlockSpec((1, SMALL_W), lambda i: (0, 0))],
        out_specs=pl.BlockSpec((tb, SMALL_W), lambda i: (i, 0)), out_shape=jax.ShapeDtypeStruct((T, SMALL_W), F32),
        scratch_shapes=[pltpu.VMEM((1, SMALL_W), F32)], compiler_params=_cp("arbitrary"))(small, bias)


def _fgate_bwd(small, bias, dc):
    T = small.shape[0]
    tb = _div(T, ROW_TILE, 8)
    nb = T // tb

    def body(s_ref, b_ref, dc_ref, df_ref, db_ref, carry):
        @pl.when(pl.program_id(0) == 0)
        def _():
            carry[...] = jnp.zeros_like(carry)

        r = lax.broadcasted_iota(jnp.int32, (tb, tb), 0)
        c = lax.broadcasted_iota(jnp.int32, (tb, tb), 1)
        tri = (r <= c).astype(F32)
        dlf = jnp.dot(tri, dc_ref[...], precision=HI, preferred_element_type=F32) + carry[...]
        carry[...] = dlf[0:1, :]
        df = dlf * _sig(-(s_ref[...] + b_ref[...]))
        df_ref[...] = df
        part = jnp.sum(df, axis=0, keepdims=True)

        @pl.when(pl.program_id(0) == 0)
        def _():
            db_ref[...] = part

        @pl.when(pl.program_id(0) > 0)
        def _():
            db_ref[...] += part

    rev = pl.BlockSpec((tb, SMALL_W), lambda i: (nb - 1 - i, 0))
    one = pl.BlockSpec((1, SMALL_W), lambda i: (0, 0))
    return pl.pallas_call(
        body, name="fgate_bwd", grid=(nb,), in_specs=[rev, one, rev], out_specs=[rev, one],
        out_shape=[jax.ShapeDtypeStruct((T, SMALL_W), F32), jax.ShapeDtypeStruct((1, SMALL_W), F32)],
        scratch_shapes=[pltpu.VMEM((1, SMALL_W), F32)], compiler_params=_cp("arbitrary"))(small, bias, dc)


def _att_rows(a8, tb):
    T = a8.shape[0]
    return a8.T.reshape(HEADS // 2, 2, T // tb, tb).transpose(0, 2, 1, 3)


def _att_cols(a8):
    return jnp.repeat(a8, HD, axis=1)


EXP_ZERO = -104.0
SKIP_SLACK = 2.0


def _att_stats(qkv, c_cols, ones_blk):
    T = qkv.shape[0]
    tb = _div(T, ATT_TILE)

    def body(q_ref, k_ref, c_ref, e_ref, o_ref):
        q = q_ref[...].astype(F32)
        k = k_ref[...].astype(F32)
        qn = jnp.max(jnp.dot(q * q, e_ref[...], precision=HI, preferred_element_type=F32), axis=0, keepdims=True)
        kn = jnp.max(jnp.dot(k * k, e_ref[...], precision=HI, preferred_element_type=F32), axis=0, keepdims=True)
        c = c_ref[...]
        o_ref[0] = jnp.concatenate([jnp.sqrt(qn), jnp.sqrt(kn), jnp.max(c, axis=0, keepdims=True),
                                    jnp.min(c, axis=0, keepdims=True), jnp.zeros((4, FW), F32)], axis=0)

    blk = lambda off: pl.BlockSpec((tb, FW), lambda i: (i, off))
    return pl.pallas_call(
        body, name="att_stats", grid=(T // tb,),
        in_specs=[blk(0), blk(1), blk(0), pl.BlockSpec((FW, FW), lambda i: (0, 0))],
        out_specs=pl.BlockSpec((1, 8, FW), lambda i: (i, 0, 0)), out_shape=jax.ShapeDtypeStruct((T // tb, 8, FW), F32),
        compiler_params=_cp("parallel"))(qkv, qkv, c_cols, ones_blk)


def _att_plan(stats):
    nb = stats.shape[0]
    st = stats[:, :4, ::HD]
    qmax, kmax, cmax, cmin = (st[:, r, :].T for r in range(4))
    bound = (HD ** -0.5) * 1.01 * qmax[:, :, None] * (kmax[:, None, :] + kmax[:, :, None]) \
        + cmax[:, :, None] - cmin[:, None, :] + SKIP_SLACK
    ii = lax.broadcasted_iota(jnp.int32, (nb, nb), 0)
    jj = lax.broadcasted_iota(jnp.int32, (nb, nb), 1)
    skip = (bound < EXP_ZERO) & (jj < ii)[None]
    live = (~skip & (jj <= ii)[None]).reshape(HEADS // 2, 2, nb, nb).any(axis=1)
    jfirst = jnp.argmax(live, axis=2).astype(jnp.int32)
    ilast = (nb - 1 - jnp.argmax(live[:, ::-1, :], axis=1)).astype(jnp.int32)
    return skip.astype(jnp.int32).reshape(-1), jfirst.reshape(-1), ilast.reshape(-1)


def _fox_fwd(qkv, c_cols, c_rows, plan):
    T = qkv.shape[0]
    tb = _div(T, ATT_TILE)
    nb = T // tb
    npair = HEADS // 2
    scale = HD ** -0.5
    dn = (((1,), (1,)), ((), ()))

    def body(skip_ref, jfirst_ref, ilast_ref, q_ref, k_ref, v_ref, cq_ref, ck_ref, o_ref, l_ref, m_scr, l_scr, acc_scr):
        pr = pl.program_id(0)
        i = pl.program_id(1)
        q = q_ref[...]
        lane = lax.broadcasted_iota(jnp.int32, (1, 2 * HD), 1)
        sel0 = lane < HD
        zero = jnp.zeros_like(q)
        qh = (jnp.where(sel0, q, zero) * scale, jnp.where(sel0, zero, q) * scale)
        cq = (cq_ref[:, 0:1], cq_ref[:, HD:HD + 1])
        row = i * tb + lax.broadcasted_iota(jnp.int32, (tb, 1), 0)
        m_scr[...] = jnp.full(m_scr.shape, NEG, F32)
        l_scr[...] = jnp.zeros(l_scr.shape, F32)
        acc_scr[...] = jnp.zeros(acc_scr.shape, F32)

        def step(j, _):
            start = pl.multiple_of(j * tb, tb)
            for h in range(2):
                @pl.when(skip_ref[((2 * pr + h) * nb + i) * nb + j] == 0)
                def _():
                    kb = k_ref[pl.ds(start, tb), :]
                    vb = v_ref[pl.ds(start, tb), :]
                    col = j * tb + lax.broadcasted_iota(jnp.int32, (1, tb), 1)
                    mask = (col <= row) & (col >= FRONT)
                    s = lax.dot_general(qh[h], kb, dn, preferred_element_type=F32) + cq[h] - ck_ref[0, j, h:h + 1, :]
                    s = jnp.where(mask, s, NEG)
                    m = m_scr[h]
                    m_new = jnp.maximum(m, jnp.max(s, axis=-1, keepdims=True))
                    p = jnp.where(mask, jnp.exp(s - m_new), 0.0)
                    alpha = jnp.exp(m - m_new)
                    m_scr[h] = m_new
                    l_scr[h] = l_scr[h] * alpha + jnp.sum(p, axis=-1, keepdims=True)
                    acc_scr[h] = acc_scr[h] * alpha + jnp.dot(p.astype(vb.dtype), vb, preferred_element_type=F32)
            return 0

        lax.fori_loop(jfirst_ref[pr * nb + i], i + 1, step, 0)
        outs, lses = [], []
        for h in range(2):
            l = l_scr[h]
            ok = l > 0.0
            ls = jnp.where(ok, l, 1.0)
            outs.append(jnp.where(ok, acc_scr[h] / ls, 0.0))
            lses.append(jnp.where(ok, m_scr[h] + jnp.log(ls), 0.0))
        o_ref[...] = jnp.where(sel0, outs[0], outs[1])
        l_ref[...] = jnp.where(sel0, lses[0], lses[1])

    blk = lambda off: pl.BlockSpec((tb, 2 * HD), lambda p, i, *_: (i, off + p))
    full = lambda off: pl.BlockSpec((T, 2 * HD), lambda p, i, *_: (0, off + p))
    rows = pl.BlockSpec((1, nb, 2, tb), lambda p, i, *_: (p, 0, 0, 0))
    return pl.pallas_call(
        body, name="fox_fwd",
        grid_spec=pltpu.PrefetchScalarGridSpec(
            num_scalar_prefetch=3, grid=(npair, nb),
            in_specs=[blk(0), full(npair), full(2 * npair), blk(0), rows], out_specs=[blk(0), blk(0)],
            scratch_shapes=[pltpu.VMEM((2, tb, 1), F32), pltpu.VMEM((2, tb, 1), F32), pltpu.VMEM((2, tb, 2 * HD), F32)]),
        out_shape=[jax.ShapeDtypeStruct((T, FW), F32), jax.ShapeDtypeStruct((T, FW), F32)],
        compiler_params=_cp("parallel", "arbitrary"))(*plan, qkv, qkv, qkv, c_cols, c_rows)


def _fox_bwd_dq(qkv, do, c_cols, c_rows, lse, delta, plan):
    T = qkv.shape[0]
    tb = _div(T, ATT_TILE)
    nb = T // tb
    npair = HEADS // 2
    scale = HD ** -0.5
    dn = (((1,), (1,)), ((), ()))

    def body(skip_ref, jfirst_ref, ilast_ref, q_ref, k_ref, v_ref, do_ref, cq_ref, ck_ref, l_ref, dl_ref, dq_ref, dc_ref,
             dq_scr, dc_scr):
        pr = pl.program_id(0)
        i = pl.program_id(1)
        q = q_ref[...]
        do = do_ref[...]
        lane = lax.broadcasted_iota(jnp.int32, (1, 2 * HD), 1)
        sel0 = lane < HD
        qh = (jnp.where(sel0, q, jnp.zeros_like(q)) * scale, jnp.where(sel0, jnp.zeros_like(q), q) * scale)
        doh = (jnp.where(sel0, do, jnp.zeros_like(do)), jnp.where(sel0, jnp.zeros_like(do), do))
        cq = (cq_ref[:, 0:1], cq_ref[:, HD:HD + 1])
        ls = (l_ref[:, 0:1], l_ref[:, HD:HD + 1])
        dl = (dl_ref[:, 0:1], dl_ref[:, HD:HD + 1])
        row = i * tb + lax.broadcasted_iota(jnp.int32, (tb, 1), 0)
        dq_scr[...] = jnp.zeros(dq_scr.shape, F32)
        dc_scr[...] = jnp.zeros(dc_scr.shape, F32)

        def step(j, _):
            start = pl.multiple_of(j * tb, tb)
            for h in range(2):
                @pl.when(skip_ref[((2 * pr + h) * nb + i) * nb + j] == 0)
                def _():
                    kb = k_ref[pl.ds(start, tb), :]
                    vb = v_ref[pl.ds(start, tb), :]
                    col = j * tb + lax.broadcasted_iota(jnp.int32, (1, tb), 1)
                    mask = (col <= row) & (col >= FRONT)
                    s = lax.dot_general(qh[h], kb, dn, preferred_element_type=F32) + cq[h] - ck_ref[0, j, h:h + 1, :]
                    p = jnp.where(mask, jnp.exp(jnp.where(mask, s, NEG) - ls[h]), 0.0)
                    dp = lax.dot_general(doh[h], vb, dn, preferred_element_type=F32)
                    ds = p * (dp - dl[h])
                    dq_scr[h] += jnp.dot(ds.astype(kb.dtype), kb, preferred_element_type=F32)
                    dc_scr[h] += jnp.sum(ds, axis=-1, keepdims=True)
            return 0

        lax.fori_loop(jfirst_ref[pr * nb + i], i + 1, step, 0)
        dq_ref[...] = jnp.where(sel0, dq_scr[0], dq_scr[1]) * scale
        dc_ref[...] = jnp.where(sel0, dc_scr[0], dc_scr[1])

    blk = lambda off: pl.BlockSpec((tb, 2 * HD), lambda p, i, *_: (i, off + p))
    full = lambda off: pl.BlockSpec((T, 2 * HD), lambda p, i, *_: (0, off + p))
    rows = pl.BlockSpec((1, nb, 2, tb), lambda p, i, *_: (p, 0, 0, 0))
    return pl.pallas_call(
        body, name="fox_bwd_dq",
        grid_spec=pltpu.PrefetchScalarGridSpec(
            num_scalar_prefetch=3, grid=(npair, nb),
            in_specs=[blk(0), full(npair), full(2 * npair), blk(0), blk(0), rows, blk(0), blk(0)], out_specs=[blk(0), blk(0)],
            scratch_shapes=[pltpu.VMEM((2, tb, 2 * HD), F32), pltpu.VMEM((2, tb, 1), F32)]),
        out_shape=[jax.ShapeDtypeStruct((T, FW), F32), jax.ShapeDtypeStruct((T, FW), F32)],
        compiler_params=_cp("parallel", "arbitrary"))(*plan, qkv, qkv, qkv, do, c_cols, c_rows, lse, delta)


def _fox_bwd_dkv(qkv, do, c_cols, c_rows, lse_rows, delta_rows, plan):
    T = qkv.shape[0]
    tb = _div(T, ATT_TILE)
    nb = T // tb
    npair = HEADS // 2
    scale = HD ** -0.5
    dn = (((1,), (1,)), ((), ()))

    def body(skip_ref, jfirst_ref, ilast_ref, q_ref, k_ref, v_ref, do_ref, ck_ref, cq_ref, l_ref, dl_ref,
             dk_ref, dv_ref, dc_ref, dk_scr, dv_scr, dc_scr):
        pr = pl.program_id(0)
        jb = pl.program_id(1)
        k = k_ref[...]
        v = v_ref[...]
        lane = lax.broadcasted_iota(jnp.int32, (1, 2 * HD), 1)
        sel0 = lane < HD
        kh = (jnp.where(sel0, k, jnp.zeros_like(k)) * scale, jnp.where(sel0, jnp.zeros_like(k), k) * scale)
        vh = (jnp.where(sel0, v, jnp.zeros_like(v)), jnp.where(sel0, jnp.zeros_like(v), v))
        ck = (ck_ref[:, 0:1], ck_ref[:, HD:HD + 1])
        kidx = jb * tb + lax.broadcasted_iota(jnp.int32, (tb, 1), 0)
        dk_scr[...] = jnp.zeros(dk_scr.shape, F32)
        dv_scr[...] = jnp.zeros(dv_scr.shape, F32)
        dc_scr[...] = jnp.zeros(dc_scr.shape, F32)

        def step(i, _):
            start = pl.multiple_of(i * tb, tb)
            for h in range(2):
                @pl.when(skip_ref[((2 * pr + h) * nb + i) * nb + jb] == 0)
                def _():
                    qb = q_ref[pl.ds(start, tb), :]
                    dob = do_ref[pl.ds(start, tb), :]
                    qidx = i * tb + lax.broadcasted_iota(jnp.int32, (1, tb), 1)
                    mask = (kidx <= qidx) & (kidx >= FRONT)
                    st = lax.dot_general(kh[h], qb, dn, preferred_element_type=F32) + cq_ref[0, i, h:h + 1, :] - ck[h]
                    pt = jnp.where(mask, jnp.exp(jnp.where(mask, st, NEG) - l_ref[0, i, h:h + 1, :]), 0.0)
                    dv_scr[h] += jnp.dot(pt.astype(dob.dtype), dob, preferred_element_type=F32)
                    dpt = lax.dot_general(vh[h], dob, dn, preferred_element_type=F32)
                    dst = pt * (dpt - dl_ref[0, i, h:h + 1, :])
                    dk_scr[h] += jnp.dot(dst.astype(qb.dtype), qb, preferred_element_type=F32)
                    dc_scr[h] -= jnp.sum(dst, axis=-1, keepdims=True)
            return 0

        lax.fori_loop(jb, ilast_ref[pr * nb + jb] + 1, step, 0)
        dk_ref[...] = jnp.where(sel0, dk_scr[0], dk_scr[1]) * scale
        dv_ref[...] = jnp.where(sel0, dv_scr[0], dv_scr[1])
        dc_ref[...] = jnp.where(sel0, dc_scr[0], dc_scr[1])

    blk = lambda off: pl.BlockSpec((tb, 2 * HD), lambda p, j, *_: (j, off + p))
    full = lambda off: pl.BlockSpec((T, 2 * HD), lambda p, j, *_: (0, off + p))
    rows = pl.BlockSpec((1, nb, 2, tb), lambda p, j, *_: (p, 0, 0, 0))
    return pl.pallas_call(
        body, name="fox_bwd_dkv",
        grid_spec=pltpu.PrefetchScalarGridSpec(
            num_scalar_prefetch=3, grid=(npair, nb),
            in_specs=[full(0), blk(npair), blk(2 * npair), full(0), blk(0), rows, rows, rows],
            out_specs=[blk(0), blk(0), blk(0)],
            scratch_shapes=[pltpu.VMEM((2, tb, 2 * HD), F32), pltpu.VMEM((2, tb, 2 * HD), F32), pltpu.VMEM((2, tb, 1), F32)]),
        out_shape=[jax.ShapeDtypeStruct((T, FW), F32)] * 3,
        compiler_params=_cp("parallel", "arbitrary"))(*plan, qkv, qkv, qkv, do, c_cols, c_rows, lse_rows, delta_rows)


def _head_dot(a, b, ones_blk):
    T = a.shape[0]
    tb = _div(T, ROW_TILE, 8)

    def body(a_ref, b_ref, e_ref, o_ref):
        prod = a_ref[...].astype(F32) * b_ref[...].astype(F32)
        o_ref[...] = jnp.dot(prod, e_ref[...], precision=HI, preferred_element_type=F32)

    row = pl.BlockSpec((tb, FW), lambda i: (i, 0))
    return pl.pallas_call(
        body, name="head_dot", grid=(T // tb,), in_specs=[row, row, pl.BlockSpec((FW, FW), lambda i: (0, 0))],
        out_specs=row, out_shape=jax.ShapeDtypeStruct((T, FW), F32), compiler_params=_cp("parallel"))(a, b, ones_blk)


def _conv_windows(scr, x_ref, h_ref, first, kw, tb):
    scr[0:8, :] = jnp.where(first, 0.0, h_ref[...])
    scr[8:8 + tb, :] = x_ref[...]
    return [scr[8 - (kw - 1 - k):8 - (kw - 1 - k) + tb, :] for k in range(kw)]


def _halo_prev(tb, tc, off=0):
    return pl.BlockSpec((8, tc), lambda j, i: (jnp.maximum(i * (tb // 8) - 1, 0), j + off))


def _gconv_fwd(rest, w):
    T = rest.shape[0]
    C = 3 * FW
    kw = w.shape[0]
    tb, tc = _div(T, 1280, 8), 512

    def body(x_ref, h_ref, w_ref, o_ref, scr):
        win = _conv_windows(scr, x_ref, h_ref, pl.program_id(1) == 0, kw, tb)
        u = sum(w_ref[k:k + 1, :] * win[k] for k in range(kw))
        o_ref[...] = u * _sig(u)

    return pl.pallas_call(
        body, name="gconv_fwd", grid=(C // tc, T // tb),
        in_specs=[pl.BlockSpec((tb, tc), lambda j, i: (i, j)), _halo_prev(tb, tc), pl.BlockSpec((kw, tc), lambda j, i: (0, j))],
        out_specs=pl.BlockSpec((tb, tc), lambda j, i: (i, j)), out_shape=jax.ShapeDtypeStruct((T, C), F32),
        scratch_shapes=[pltpu.VMEM((tb + 8, tc), F32)], compiler_params=_cp("parallel", "arbitrary"))(rest, rest, w)


def _gconv_bwd_du(rest, w, dy):
    T = rest.shape[0]
    C = 3 * FW
    kw = w.shape[0]
    tb, tc = _div(T, 1280, 8), 512

    def body(x_ref, h_ref, w_ref, dy_ref, du_ref, dw_ref, scr):
        i = pl.program_id(1)
        win = _conv_windows(scr, x_ref, h_ref, i == 0, kw, tb)
        u = sum(w_ref[k:k + 1, :] * win[k] for k in range(kw))
        sg = _sig(u)
        du = dy_ref[...] * sg * (1.0 + u * (1.0 - sg))
        du_ref[...] = du
        part = jnp.concatenate([jnp.sum(du * win[k], axis=0, keepdims=True) for k in range(kw)], axis=0)

        @pl.when(i == 0)
        def _():
            dw_ref[...] = part

        @pl.when(i > 0)
        def _():
            dw_ref[...] += part

    blk = pl.BlockSpec((tb, tc), lambda j, i: (i, j))
    wsp = pl.BlockSpec((kw, tc), lambda j, i: (0, j))
    return pl.pallas_call(
        body, name="gconv_bwd_du", grid=(C // tc, T // tb), in_specs=[blk, _halo_prev(tb, tc), wsp, blk],
        out_specs=[blk, wsp], out_shape=[jax.ShapeDtypeStruct((T, C), F32), jax.ShapeDtypeStruct((kw, C), F32)],
        scratch_shapes=[pltpu.VMEM((tb + 8, tc), F32)], compiler_params=_cp("parallel", "arbitrary"))(rest, rest, w, dy)


def _conv_bwd_dx(du, w, out_dtype, name):
    T, C = du.shape
    kw = w.shape[0]
    tb = _div(T, 1280, 8)
    tc = _div(C, 512)
    nb = T // tb

    def body(x_ref, h_ref, w_ref, o_ref, scr):
        scr[0:tb, :] = x_ref[...]
        scr[tb:tb + 8, :] = jnp.where(pl.program_id(1) == nb - 1, 0.0, h_ref[...])
        acc = sum(w_ref[k:k + 1, :] * scr[kw - 1 - k:kw - 1 - k + tb, :] for k in range(kw))
        o_ref[...] = acc.astype(o_ref.dtype)

    halo = pl.BlockSpec((8, tc), lambda j, i: (jnp.minimum((i + 1) * (tb // 8), T // 8 - 1), j))
    return pl.pallas_call(
        body, name=name, grid=(C // tc, nb),
        in_specs=[pl.BlockSpec((tb, tc), lambda j, i: (i, j)), halo, pl.BlockSpec((kw, tc), lambda j, i: (0, j))],
        out_specs=pl.BlockSpec((tb, tc), lambda j, i: (i, j)), out_shape=jax.ShapeDtypeStruct((T, C), out_dtype),
        scratch_shapes=[pltpu.VMEM((tb + 8, tc), F32)], compiler_params=_cp("parallel", "arbitrary"))(du, du, w)


def _glu_fwd(up, w, b):
    T = up.shape[0]
    kw = w.shape[0]
    tb, tc = _div(T, 1280, 8), 256
    nc = DFF // tc

    def body(xg, hg, xu, hu, wg, wu, bg, bu, o_ref, sg, su):
        first = pl.program_id(1) == 0
        wing = _conv_windows(sg, xg, hg, first, kw, tb)
        winu = _conv_windows(su, xu, hu, first, kw, tb)
        ug = bg[...] + sum(wg[k:k + 1, :] * wing[k] for k in range(kw))
        uu = bu[...] + sum(wu[k:k + 1, :] * winu[k] for k in range(kw))
        o_ref[...] = (ug * _sig(ug) * uu).astype(o_ref.dtype)

    blk = lambda off: pl.BlockSpec((tb, tc), lambda j, i: (i, j + off))
    wsp = lambda off: pl.BlockSpec((kw, tc), lambda j, i: (0, j + off))
    bsp = lambda off: pl.BlockSpec((1, tc), lambda j, i: (0, j + off))
    return pl.pallas_call(
        body, name="glu_fwd", grid=(nc, T // tb),
        in_specs=[blk(0), _halo_prev(tb, tc), blk(nc), _halo_prev(tb, tc, nc), wsp(0), wsp(nc), bsp(0), bsp(nc)],
        out_specs=blk(0), out_shape=jax.ShapeDtypeStruct((T, DFF), _BF),
        scratch_shapes=[pltpu.VMEM((tb + 8, tc), F32)] * 2, compiler_params=_cp("parallel", "arbitrary"))(
            up, up, up, up, w, w, b, b)


def _glu_bwd_du(up, w, b, df):
    T = up.shape[0]
    kw = w.shape[0]
    tb, tc = _div(T, 1280, 8), 256
    nc = DFF // tc

    def body(xg, hg, xu, hu, wg, wu, bg, bu, df_ref, dug_ref, duu_ref, dwg_ref, dwu_ref, dbg_ref, dbu_ref, sg, su):
        i = pl.program_id(1)
        wing = _conv_windows(sg, xg, hg, i == 0, kw, tb)
        winu = _conv_windows(su, xu, hu, i == 0, kw, tb)
        ug = bg[...] + sum(wg[k:k + 1, :] * wing[k] for k in range(kw))
        uu = bu[...] + sum(wu[k:k + 1, :] * winu[k] for k in range(kw))
        s = _sig(ug)
        df = df_ref[...]
        dug = df * uu * s * (1.0 + ug * (1.0 - s))
        duu = df * ug * s
        dug_ref[...] = dug
        duu_ref[...] = duu
        parts = (jnp.concatenate([jnp.sum(dug * wing[k], axis=0, keepdims=True) for k in range(kw)], axis=0),
                 jnp.concatenate([jnp.sum(duu * winu[k], axis=0, keepdims=True) for k in range(kw)], axis=0),
                 jnp.sum(dug, axis=0, keepdims=True), jnp.sum(duu, axis=0, keepdims=True))
        accs = (dwg_ref, dwu_ref, dbg_ref, dbu_ref)

        @pl.when(i == 0)
        def _():
            for r, p in zip(accs, parts):
                r[...] = p

        @pl.when(i > 0)
        def _():
            for r, p in zip(accs, parts):
                r[...] += p

    blk = lambda off: pl.BlockSpec((tb, tc), lambda j, i: (i, j + off))
    wsp = lambda off: pl.BlockSpec((kw, tc), lambda j, i: (0, j + off))
    bsp = lambda off: pl.BlockSpec((1, tc), lambda j, i: (0, j + off))
    return pl.pallas_call(
        body, name="glu_bwd_du", grid=(nc, T // tb),
        in_specs=[blk(0), _halo_prev(tb, tc), blk(nc), _halo_prev(tb, tc, nc), wsp(0), wsp(nc), bsp(0), bsp(nc), blk(0)],
        out_specs=[blk(0), blk(0), wsp(0), wsp(0), bsp(0), bsp(0)],
        out_shape=[jax.ShapeDtypeStruct((T, DFF), F32)] * 2 + [jax.ShapeDtypeStruct((kw, DFF), F32)] * 2
        + [jax.ShapeDtypeStruct((1, DFF), F32)] * 2,
        scratch_shapes=[pltpu.VMEM((tb + 8, tc), F32)] * 2, compiler_params=_cp("parallel", "arbitrary"))(
            up, up, up, up, w, w, b, b, df)


def _mix_fwd(rest, gate_bias, y_fox, y_gdn):
    T = rest.shape[0]
    tb, tc = _div(T, ROW_TILE, 8), 512
    nc = D // tc
    og = OFF_GATES // tc

    def body(gf, gg, bf, bg, yf, yg, o_ref):
        o_ref[...] = (_sig(gf[...] + bf[...]) * yf[...] + _sig(gg[...] + bg[...]) * yg[...]).astype(o_ref.dtype)

    blk = lambda off: pl.BlockSpec((tb, tc), lambda i, j: (i, j + off))
    bsp = lambda off: pl.BlockSpec((1, tc), lambda i, j: (0, j + off))
    return pl.pallas_call(
        body, name="mix_fwd", grid=(T // tb, nc),
        in_specs=[blk(og), blk(og + nc), bsp(0), bsp(nc), blk(0), blk(0)], out_specs=blk(0),
        out_shape=jax.ShapeDtypeStruct((T, D), _BF), compiler_params=_cp("parallel", "parallel"))(
            rest, rest, gate_bias, gate_bias, y_fox, y_gdn)


def _mix_bwd(rest, gate_bias, y_fox, y_gdn, dmix):
    T = rest.shape[0]
    tb, tc = _div(T, ROW_TILE, 8), 512
    nc = D // tc
    og = OFF_GATES // tc

    def body(gf, gg, bf, bg, yf, yg, dm, dyf, dyg, dgf, dgg, dbf, dbg):
        i = pl.program_id(1)
        d = dm[...]
        sf = _sig(gf[...] + bf[...])
        sg = _sig(gg[...] + bg[...])
        dyf[...] = (d * sf).astype(dyf.dtype)
        dyg[...] = (d * sg).astype(dyg.dtype)
        a = d * yf[...] * sf * (1.0 - sf)
        b = d * yg[...] * sg * (1.0 - sg)
        dgf[...] = a
        dgg[...] = b
        pa = jnp.sum(a, axis=0, keepdims=True)
        pb = jnp.sum(b, axis=0, keepdims=True)

        @pl.when(i == 0)
        def _():
            dbf[...] = pa
            dbg[...] = pb

        @pl.when(i > 0)
        def _():
            dbf[...] += pa
            dbg[...] += pb

    blk = lambda off: pl.BlockSpec((tb, tc), lambda j, i: (i, j + off))
    bsp = lambda off: pl.BlockSpec((1, tc), lambda j, i: (0, j + off))
    return pl.pallas_call(
        body, name="mix_bwd", grid=(nc, T // tb),
        in_specs=[blk(og), blk(og + nc), bsp(0), bsp(nc), blk(0), blk(0), blk(0)],
        out_specs=[blk(0), blk(0), blk(0), blk(0), bsp(0), bsp(0)],
        out_shape=[jax.ShapeDtypeStruct((T, D), _BF)] * 2 + [jax.ShapeDtypeStruct((T, D), F32)] * 2
        + [jax.ShapeDtypeStruct((1, D), F32)] * 2,
        compiler_params=_cp("parallel", "arbitrary"))(rest, rest, gate_bias, gate_bias, y_fox, y_gdn, dmix)


def _loss_head(h2, w, target, n_valid):
    T = h2.shape[0]
    tb = _div(T, ROW_TILE, 8)

    def body(h_ref, w_ref, t_ref, dh_ref, loss_ref, dw_ref):
        i = pl.program_id(0)
        x = h_ref[...]
        r = lax.rsqrt(jnp.mean(x * x, axis=-1, keepdims=True) + EPS)
        xh = x * r
        row = i * tb + lax.broadcasted_iota(jnp.int32, (tb, 1), 0)
        valid = (row >= X0) & (row < X0 + n_valid)
        e = jnp.where(valid, xh * w_ref[...] - t_ref[...], 0.0)
        dy = e * (1.0 / D)
        g = dy * w_ref[...]
        dh_ref[...] = r * (g - xh * jnp.mean(xh * g, axis=-1, keepdims=True))
        lpart = 0.5 * jnp.sum(jnp.sum(e * e, axis=-1, keepdims=True) * (1.0 / D), axis=0, keepdims=True)
        wpart = jnp.sum(dy * xh, axis=0, keepdims=True)

        @pl.when(i == 0)
        def _():
            loss_ref[...] = lpart
            dw_ref[...] = wpart

        @pl.when(i > 0)
        def _():
            loss_ref[...] += lpart
            dw_ref[...] += wpart

    row = pl.BlockSpec((tb, D), lambda i: (i, 0))
    one = pl.BlockSpec((1, D), lambda i: (0, 0))
    return pl.pallas_call(
        body, name="loss_head", grid=(T // tb,), in_specs=[row, one, row],
        out_specs=[row, pl.BlockSpec((1, 1), lambda i: (0, 0)), one],
        out_shape=[jax.ShapeDtypeStruct((T, D), F32), jax.ShapeDtypeStruct((1, 1), F32), jax.ShapeDtypeStruct((1, D), F32)],
        compiler_params=_cp("arbitrary"))(h2, w, target)


def _bd_lo(a, b, ca, cb):
    return lax.dot_general(a.astype(_BF), b.astype(_BF), (((ca,), (cb,)), ((0,), (0,))), preferred_element_type=F32)


def _split2(a):
    hi = a.astype(_BF)
    return hi, (a - hi.astype(F32)).astype(_BF)


def _bd_hi(a, b, ca, cb, exact_a=False):
    dn = (((ca,), (cb,)), ((0,), (0,)))
    dot = lambda x, y: lax.dot_general(x, y, dn, preferred_element_type=F32)
    bh, bl = _split2(b)
    if exact_a:
        ah = a.astype(_BF)
        return dot(ah, bh) + dot(ah, bl)
    ah, al = _split2(a)
    return dot(ah, bh) + (dot(ah, bl) + dot(al, bh))


def _row_to_col(x):
    eye = lax.broadcasted_iota(jnp.int32, (1, CH, CH), 1) == lax.broadcasted_iota(jnp.int32, (1, CH, CH), 2)
    return jnp.sum(jnp.where(eye, jnp.broadcast_to(x, (HEADS, CH, CH)), 0.0), axis=2, keepdims=True)


def _col_to_row(x):
    eye = lax.broadcasted_iota(jnp.int32, (1, CH, CH), 1) == lax.broadcasted_iota(jnp.int32, (1, CH, CH), 2)
    return jnp.sum(jnp.where(eye, jnp.broadcast_to(x, (HEADS, CH, CH)), 0.0), axis=1, keepdims=True)


def _gdn_chunk(q, k, v, bpre, apre, alog, dtb):
    H = HEADS
    r = lax.broadcasted_iota(jnp.int32, (1, CH, CH), 1)
    c = lax.broadcasted_iota(jnp.int32, (1, CH, CH), 2)
    tril, strict = r >= c, r > c
    lb = jnp.broadcast_to(tril.astype(F32), (H, CH, CH))
    rq = lax.rsqrt(jnp.sum(q * q, axis=-1, keepdims=True) + EPS)
    rk = lax.rsqrt(jnp.sum(k * k, axis=-1, keepdims=True) + EPS)
    qh = q * rq
    qn = qh * (HD ** -0.5)
    kn = k * rk
    beta = _sig(bpre)
    x = apre + dtb
    ea = jnp.exp(alog)
    g = -ea * jax.nn.softplus(x)
    gb = jnp.broadcast_to(g, (H, CH, HD))
    gc = _bd_hi(lb, gb, 2, 1, True)
    dm = _bd_hi(lb, jnp.where(strict, gb, 0.0), 2, 1, True)
    decay = jnp.where(tril, jnp.exp(jnp.where(tril, dm, 0.0)), 0.0)
    eg = jnp.exp(gc)
    gl = gc[:, CH - 1:CH, :]
    egl = jnp.exp(gl - gc)
    cd = jnp.exp(gl)
    kb = kn * beta
    vb = v * beta
    kk = _bd_lo(kb, kn, 2, 2)
    a = jnp.where(strict, kk * decay, 0.0)
    tm = jnp.where(r == c, 1.0, 0.0) - a
    p = a
    for _ in range(5):
        p = _bd_hi(p, p, 2, 1)
        tm = tm + _bd_hi(tm, p, 2, 1)
    kbg = kb * eg
    value = _bd_hi(tm, vb, 2, 1)
    kcd = _bd_hi(tm, kbg, 2, 1)
    qk = _bd_lo(qn, kn, 2, 2)
    attn = jnp.where(tril, qk * decay, 0.0)
    return dict(tril=tril, strict=strict, lb=lb, rq=rq, rk=rk, qh=qh, qn=qn, kn=kn, beta=beta, x=x, ea=ea, g=g,
                decay=decay, eg=eg, egl=egl, cd=cd, kb=kb, vb=vb, kk=kk, tm=tm, kbg=kbg, value=value, kcd=kcd, qk=qk,
                attn=attn, qd=qn * eg, kt=kn * egl)


def _gdn_specs(T, rev):
    nc = T // CH
    pos = (lambda n: nc - 1 - n) if rev else (lambda n: n)
    mat = pl.BlockSpec((HEADS, CH, HD), lambda n: (0, pos(n), 0))
    col = pl.BlockSpec((1, HEADS, 1, CH), lambda n: (pos(n), 0, 0, 0))
    sca = pl.BlockSpec((HEADS, 1, 1), lambda n: (0, 0, 0))
    nw = pl.BlockSpec((1, 1, HD), lambda n: (0, 0, 0))
    st = pl.BlockSpec((1, HEADS, HD, HD), lambda n: (pos(n), 0, 0, 0))
    return nc, mat, col, sca, nw, st


def _gdn_fwd(q, k, v, z, bpre, apre, alog, dtb, nw):
    T = q.shape[1]
    nc, mat, col, sca, nws, st = _gdn_specs(T, False)

    def body(q_ref, k_ref, v_ref, z_ref, b_ref, a_ref, al_ref, dt_ref, nw_ref, o_ref, og_ref, st_ref, s_scr):
        @pl.when(pl.program_id(0) == 0)
        def _():
            s_scr[...] = jnp.zeros_like(s_scr)

        L = _gdn_chunk(q_ref[...], k_ref[...], v_ref[...], _row_to_col(b_ref[0]), _row_to_col(a_ref[0]), al_ref[...],
                       dt_ref[...])
        s = s_scr[...]
        st_ref[0] = s
        v_new = L["value"] - _bd_lo(L["kcd"], s, 2, 1)
        o = _bd_lo(L["qd"], s, 2, 1) + _bd_lo(L["attn"], v_new, 2, 1)
        s_scr[...] = s * L["cd"] + _bd_lo(L["kt"], v_new, 1, 1)
        o_ref[...] = o
        zz = z_ref[...]
        rs = lax.rsqrt(jnp.mean(o * o, axis=-1, keepdims=True) + EPS)
        og_ref[...] = o * rs * nw_ref[...] * zz * _sig(zz)

    return pl.pallas_call(
        body, name="gdn_fwd", grid=(nc,), in_specs=[mat, mat, mat, mat, col, col, sca, sca, nws],
        out_specs=[mat, mat, st],
        out_shape=[jax.ShapeDtypeStruct((HEADS, T, HD), F32)] * 2 + [jax.ShapeDtypeStruct((nc, HEADS, HD, HD), F32)],
        scratch_shapes=[pltpu.VMEM((HEADS, HD, HD), F32)], compiler_params=_cp("arbitrary"))(
            q, k, v, z, bpre, apre, alog, dtb, nw)


def _gdn_bwd(q, k, v, z, bpre, apre, alog, dtb, nw, states, o, dog):
    T = q.shape[1]
    nc, mat, col, sca, nws, st = _gdn_specs(T, True)

    def body(q_ref, k_ref, v_ref, z_ref, b_ref, a_ref, al_ref, dt_ref, nw_ref, st_ref, o_ref, dog_ref,
             dq_ref, dk_ref, dv_ref, dz_ref, db_ref, da_ref, dal_ref, ddt_ref, dnw_ref, ds_scr):
        @pl.when(pl.program_id(0) == 0)
        def _():
            ds_scr[...] = jnp.zeros_like(ds_scr)
            dal_ref[...] = jnp.zeros_like(dal_ref)
            ddt_ref[...] = jnp.zeros_like(ddt_ref)
            dnw_ref[...] = jnp.zeros_like(dnw_ref)

        vv = v_ref[...]
        L = _gdn_chunk(q_ref[...], k_ref[...], vv, _row_to_col(b_ref[0]), _row_to_col(a_ref[0]), al_ref[...], dt_ref[...])
        tril, strict, lb = L["tril"], L["strict"], L["lb"]
        qn, kn, kb, beta, decay, eg, egl, cd = L["qn"], L["kn"], L["kb"], L["beta"], L["decay"], L["eg"], L["egl"], L["cd"]
        value, kcd, attn, qd, kt, tm = L["value"], L["kcd"], L["attn"], L["qd"], L["kt"], L["tm"]
        s = st_ref[0]
        v_new = value - _bd_lo(kcd, s, 2, 1)
        oo, zz, dog_ = o_ref[...], z_ref[...], dog_ref[...]
        sz = _sig(zz)
        rs = lax.rsqrt(jnp.mean(oo * oo, axis=-1, keepdims=True) + EPS)
        oh = oo * rs
        dz_ref[...] = dog_ * oh * nw_ref[...] * sz * (1.0 + zz * (1.0 - sz))
        don = dog_ * zz * sz
        gdy = don * nw_ref[...]
        do = rs * (gdy - oh * jnp.mean(oh * gdy, axis=-1, keepdims=True))
        dnw_ref[...] += jnp.sum(don * oh, axis=(0, 1), keepdims=True)
        dsn = ds_scr[...]
        d_vnew = _bd_lo(attn, do, 1, 1) + _bd_lo(kt, dsn, 2, 1)
        d_qd = _bd_lo(do, s, 2, 2)
        d_attn = jnp.where(tril, _bd_lo(do, v_new, 2, 2), 0.0)
        d_kt = _bd_lo(v_new, dsn, 2, 2)
        d_cd = jnp.sum(s * dsn, axis=(1, 2), keepdims=True)
        ds_scr[...] = _bd_lo(qd, do, 1, 1) + cd * dsn - _bd_lo(kcd, d_vnew, 1, 1)
        d_kcd = -_bd_lo(d_vnew, s, 2, 2)
        d_vb = _bd_hi(tm, d_vnew, 1, 1)
        d_kbg = _bd_hi(tm, d_kcd, 1, 1)
        da = -jnp.where(strict, _bd_lo(d_vb, value, 2, 2) + _bd_lo(d_kbg, kcd, 2, 2), 0.0)
        dkk = da * decay
        dqk = d_attn * decay
        d_decay = da * L["kk"] + d_attn * L["qk"]
        d_kb = _bd_lo(dkk, kn, 2, 1) + d_kbg * eg
        d_kn = _bd_lo(dkk, kb, 1, 1) + _bd_lo(dqk, qn, 1, 1) + d_kt * egl
        d_qn = _bd_lo(dqk, kn, 2, 1) + d_qd * eg
        dmm = _bd_hi(lb, d_decay * decay, 1, 1, True)
        dg = jnp.sum(jnp.where(strict, dmm, 0.0), axis=-1, keepdims=True)
        d_gc = jnp.sum(d_qd * qd + d_kbg * L["kbg"] - d_kt * kt, axis=-1, keepdims=True)
        d_gl = jnp.sum(d_kt * kt, axis=(1, 2), keepdims=True) + d_cd * cd[:, :, 0:1]
        last = lax.broadcasted_iota(jnp.int32, (1, CH, 1), 1) == CH - 1
        d_gc = d_gc + jnp.where(last, d_gl, 0.0)
        dg = dg + _bd_hi(lb, jnp.broadcast_to(d_gc, (HEADS, CH, HD)), 1, 1, True)[:, :, 0:1]
        d_apre = -dg * L["ea"] * _sig(L["x"])
        da_ref[0] = _col_to_row(d_apre)
        dal_ref[...] += jnp.sum(dg * L["g"], axis=1, keepdims=True)
        ddt_ref[...] += jnp.sum(d_apre, axis=1, keepdims=True)
        d_beta = jnp.sum(d_kb * kn + d_vb * vv, axis=-1, keepdims=True)
        db_ref[0] = _col_to_row(d_beta * beta * (1.0 - beta))
        d_kn = d_kn + d_kb * beta
        dv_ref[...] = d_vb * beta
        qh = L["qh"]
        dq_ref[...] = (HD ** -0.5) * L["rq"] * (d_qn - qh * jnp.sum(qh * d_qn, axis=-1, keepdims=True))
        dk_ref[...] = L["rk"] * (d_kn - kn * jnp.sum(kn * d_kn, axis=-1, keepdims=True))

    m3 = jax.ShapeDtypeStruct((HEADS, T, HD), F32)
    c3 = jax.ShapeDtypeStruct((nc, HEADS, 1, CH), F32)
    s3 = jax.ShapeDtypeStruct((HEADS, 1, 1), F32)
    return pl.pallas_call(
        body, name="gdn_bwd", grid=(nc,),
        in_specs=[mat, mat, mat, mat, col, col, sca, sca, nws, st, mat, mat],
        out_specs=[mat, mat, mat, mat, col, col, sca, sca, nws],
        out_shape=[m3, m3, m3, m3, c3, c3, s3, s3, jax.ShapeDtypeStruct((1, 1, HD), F32)],
        scratch_shapes=[pltpu.VMEM((HEADS, HD, HD), F32)], compiler_params=_cp("arbitrary"))(
            q, k, v, z, bpre, apre, alog, dtb, nw, states, o, dog)


def _heads_major(a):
    T = a.shape[0]
    return a.reshape(T, -1, HD).transpose(1, 0, 2)


def _tokens_major(a):
    return a.transpose(1, 0, 2).reshape(a.shape[1], -1)


def _local_step(x, target, w):
    seq = x.shape[0]
    T = -(-(X0 + seq) // T_ALIGN) * T_ALIGN
    back = T - X0 - seq
    tb_att = _div(T, ATT_TILE)
    h0 = jnp.concatenate([jnp.zeros((FRONT, D), F32), w["meta"], x, jnp.zeros((back, D), F32)], axis=0)
    tgt = jnp.concatenate([jnp.zeros((X0, D), F32), target, jnp.zeros((back, D), F32)], axis=0)
    row = lambda v: v.reshape(1, -1)
    nmix, nffn, nfin = row(w["norm_mix"]), row(w["norm_ffn"]), row(w["norm_final"])
    gate_b = row(w["gate_bias"])
    fconv_b = row(w["ffn_conv_b"])
    bias128 = jnp.zeros((1, SMALL_W), F32).at[0, :HEADS].set(w["fgt_bias"])
    alog = w["a_log"].reshape(HEADS, 1, 1)
    dtb = w["dt_bias"].reshape(HEADS, 1, 1)
    gnw = w["gdn_norm"].reshape(1, 1, HD)

    a = _rms_fwd(h0, nmix, "rms_mix")
    pf = _mm(a, w["w_fox"], out_dtype=_BF, name="proj_fox")
    rest = _mm(a, w["w_rest"], name="proj_rest")
    small = rest[:, OFF_SMALL:]
    c8 = _fgate_fwd(small, bias128)[:, :HEADS]
    c_cols, c_rows = _att_cols(c8), _att_rows(c8, tb_att)
    ones_blk = jnp.asarray(np.kron(np.eye(HEADS, dtype=np.float32), np.ones((HD, HD), np.float32)))
    plan = _att_plan(_att_stats(pf, c_cols, ones_blk))
    o_fox, lse = _fox_fwd(pf, c_cols, c_rows, plan)
    conv = _gconv_fwd(rest, w["gdn_conv"])
    qkvh = _heads_major(conv)
    qh, kh, vh = qkvh[:HEADS], qkvh[HEADS:2 * HEADS], qkvh[2 * HEADS:]
    zh = _heads_major(rest[:, OFF_Z:OFF_Z + FW])
    chunk_rows = lambda a8: a8.reshape(T // CH, CH, HEADS).transpose(0, 2, 1)[:, :, None, :]
    bpre = chunk_rows(small[:, HEADS:2 * HEADS])
    apre = chunk_rows(small[:, 2 * HEADS:3 * HEADS])
    o_raw, ogh, states = _gdn_fwd(qh, kh, vh, zh, bpre, apre, alog, dtb, gnw)
    og = _tokens_major(ogh)
    y_fox = _mm(o_fox, w["w_bfox"], name="y_fox")
    y_gdn = _mm(og, w["w_bgdn"], name="y_gdn")
    mix = _mix_fwd(rest, gate_b, y_fox, y_gdn)
    h1 = _mm(mix, w["w_out"], resid=h0, name="out_proj")
    b = _rms_fwd(h1, nffn, "rms_ffn")
    up = _mm(b, w["w_up"], name="ffn_up")
    f = _glu_fwd(up, w["ffn_conv"], fconv_b)
    h2 = _mm(f, w["w_down"], resid=h1, name="ffn_down")
    dh2, loss, d_nfin = _loss_head(h2, nfin, tgt, seq)

    d_f = _mm(dh2, w["w_down"], tb=True, name="d_f")
    g_down = _mm(f, dh2, ta=True, name="g_down")
    dug, duu, dwg, dwu, dbg, dbu = _glu_bwd_du(up, w["ffn_conv"], fconv_b, d_f)
    dxg = _conv_bwd_dx(dug, w["ffn_conv"][:, :DFF], _BF, "fconv_dx_gate")
    dxu = _conv_bwd_dx(duu, w["ffn_conv"][:, DFF:], _BF, "fconv_dx_up")
    d_b = _mm(dxg, w["w_up"][:, :DFF], tb=True, name="d_b_gate")
    d_b = _mm(dxu, w["w_up"][:, DFF:], tb=True, resid=d_b, name="d_b_up")
    g_up = jnp.concatenate([_mm(b, dxg, ta=True, name="g_up_gate"), _mm(b, dxu, ta=True, name="g_up_up")], axis=1)
    dh1, d_nffn = _rms_bwd(h1, nffn, d_b, dh2, "rms_ffn_bwd")

    dmix = _mm(dh1, w["w_out"], tb=True, name="d_mix")
    g_out = _mm(mix, dh1, ta=True, name="g_out")
    dyf, dyg, dgf, dgg, dgbf, dgbg = _mix_bwd(rest, gate_b, y_fox, y_gdn, dmix)
    do_fox = _mm(dyf, w["w_bfox"], tb=True, out_dtype=_BF, name="d_o_fox")
    g_bfox = _mm(o_fox, dyf, ta=True, name="g_bfox")
    d_og = _mm(dyg, w["w_bgdn"], tb=True, name="d_o_gdn")
    g_bgdn = _mm(og, dyg, ta=True, name="g_bgdn")

    delta = _head_dot(do_fox, o_fox, ones_blk)
    dq, dcq = _fox_bwd_dq(pf, do_fox, c_cols, c_rows, lse, delta, plan)
    dk, dv, dck = _fox_bwd_dkv(pf, do_fox, c_cols, c_rows, _att_rows(lse[:, ::HD], tb_att), _att_rows(delta[:, ::HD], tb_att),
                               plan)
    dc = jnp.pad((dcq + dck)[:, ::HD], ((0, 0), (0, SMALL_W - HEADS)))
    dfp, d_fb = _fgate_bwd(small, bias128, dc)

    dqh, dkh, dvh, dzh, dbp, dap, d_alog, d_dtb, d_gnw = _gdn_bwd(
        qh, kh, vh, zh, bpre, apre, alog, dtb, gnw, states, o_raw, _heads_major(d_og))
    dconv = _tokens_major(jnp.concatenate([dqh, dkh, dvh], axis=0))
    du_g, g_gconv = _gconv_bwd_du(rest, w["gdn_conv"], dconv)
    dgx = _conv_bwd_dx(du_g, w["gdn_conv"], F32, "gconv_dx")
    token_rows = lambda a: a[:, :, 0, :].transpose(0, 2, 1).reshape(T, HEADS)
    dsmall = jnp.concatenate([dfp[:, :HEADS], token_rows(dbp), token_rows(dap),
                              jnp.zeros((T, SMALL_W - 3 * HEADS), F32)], axis=1)
    drest = jnp.concatenate([dgx, _tokens_major(dzh), dgf, dgg, dsmall], axis=1)
    dfox = jnp.concatenate([dq, dk, dv], axis=1)
    d_a = _mm(dfox, w["w_fox"], tb=True, name="d_a_fox")
    d_a = _mm(drest, w["w_rest"], tb=True, resid=d_a, name="d_a_rest")
    g_fox = _mm(a, dfox, ta=True, name="g_w_fox")
    g_rest = _mm(a, drest, ta=True, name="g_w_rest")
    dh0, d_nmix = _rms_bwd(h0, nmix, d_a, dh1, "rms_mix_bwd")

    sm = lambda lo: g_rest[:, OFF_SMALL + lo:OFF_SMALL + lo + HEADS]
    g_w_in = jnp.concatenate([g_fox, sm(0), g_rest[:, :3 * FW], g_rest[:, OFF_Z:OFF_Z + FW], sm(HEADS), sm(2 * HEADS),
                              g_rest[:, OFF_GATES:OFF_GATES + 2 * D]], axis=1)
    grads = dict(
        meta_tokens=dh0[FRONT:X0], w_in=g_w_in, fgt_bias=d_fb[0, :HEADS], gdn_conv_w=g_gconv,
        gdn_a_log=d_alog.reshape(HEADS), gdn_dt_bias=d_dtb.reshape(HEADS), gdn_norm_w=d_gnw.reshape(HD),
        gate_bias=jnp.concatenate([dgbf, dgbg], axis=1).reshape(2 * D), w_branch_fox=g_bfox, w_branch_gdn=g_bgdn,
        w_out=g_out, norm_mix_w=d_nmix.reshape(D), norm_ffn_w=d_nffn.reshape(D), ffn_w_up=g_up,
        ffn_conv_w=jnp.concatenate([dwg, dwu], axis=1), ffn_conv_b=jnp.concatenate([dbg, dbu], axis=1).reshape(2 * DFF),
        ffn_w_down=g_down, norm_final_w=d_nfin.reshape(D))
    return loss, dh0[X0:X0 + seq], grads


N_CHIPS = 4
PACK_W = 1024
PACK_ROW_ALIGN = 512
WEIGHTS = (
    ("meta_tokens", (N_META, D), 1), ("w_in", (D, 3 * FW + HEADS + 4 * FW + 2 * HEADS + 2 * D), 1), ("fgt_bias", (1, HEADS), None),
    ("gdn_conv_w", (4, 3 * FW), 1), ("gdn_a_log", (1, HEADS), None), ("gdn_dt_bias", (1, HEADS), None),
    ("gdn_norm_w", (1, HD), None), ("gate_bias", (1, 2 * D), None), ("w_branch_fox", (FW, D), 1),
    ("w_branch_gdn", (FW, D), 1), ("w_out", (D, D), 0), ("norm_mix_w", (1, D), None), ("norm_ffn_w", (1, D), None),
    ("ffn_w_up", (D, 2 * DFF), 1), ("ffn_conv_w", (3, 2 * DFF), 1), ("ffn_conv_b", (1, 2 * DFF), None),
    ("ffn_w_down", (DFF, D), 0), ("norm_final_w", (1, D), None))
SPLIT_F32 = ("meta_tokens", "gdn_conv_w", "ffn_conv_w")


def _shard_shape(shape, axis):
    if axis is None:
        return shape
    return tuple(s // N_CHIPS if a == axis else s for a, s in enumerate(shape))


def _shard_of(full, axis, q):
    if axis is None:
        return full
    n = full.shape[axis] // N_CHIPS
    return lax.slice_in_dim(full, q * n, (q + 1) * n, axis=axis)


def _pack_rows(n_elems):
    rows = -(-n_elems // PACK_W)
    return -(-rows // PACK_ROW_ALIGN) * PACK_ROW_ALIGN


def _pack(pieces, dtype):
    flat = jnp.concatenate([p.reshape(-1).astype(dtype) for p in pieces])
    rows = _pack_rows(flat.shape[0])
    return jnp.pad(flat, (0, rows * PACK_W - flat.shape[0])).reshape(rows, PACK_W)


def _unpack(slab, shapes):
    flat = slab.reshape(-1)
    out, off = [], 0
    for s in shapes:
        n = int(np.prod(s))
        out.append(flat[off:off + n].reshape(s))
        off += n
    return out


HBM_SPEC = pl.BlockSpec(memory_space=pltpu.HBM)
MESH_ID = pl.DeviceIdType.MESH


def _chip_exchange(src, name, scatter):
    def body(src_ref, out_ref, send_sems, recv_sems, local_sem):
        x, y, c = lax.axis_index("x"), lax.axis_index("y"), lax.axis_index("c")
        q = 2 * x + y
        peers = [(1 - x, y), (x, 1 - y), (1 - x, 1 - y)]

        def remote(k, src_slot, dst_slot):
            px, py = peers[k]
            return pltpu.make_async_remote_copy(
                src_ref=src_ref.at[src_slot], dst_ref=out_ref.at[dst_slot], send_sem=send_sems.at[k],
                recv_sem=recv_sems.at[k], device_id=(px, py, c), device_id_type=MESH_ID)

        mine = pltpu.make_async_copy(src_ref.at[q if scatter else 0], out_ref.at[q], local_sem)
        mine.start()
        sends = [remote(k, 2 * px + py if scatter else 0, q) for k, (px, py) in enumerate(peers)]
        for cp in sends:
            cp.start()
        for k, (px, py) in enumerate(peers):
            remote(k, 0, 2 * px + py).wait_recv()
        for cp in sends:
            cp.wait_send()
        mine.wait()

    out_shape = jax.ShapeDtypeStruct((N_CHIPS,) + src.shape[1:], src.dtype)
    return pl.pallas_call(
        body, name=name, in_specs=[HBM_SPEC], out_specs=HBM_SPEC, out_shape=out_shape,
        scratch_shapes=[pltpu.SemaphoreType.DMA((3,)), pltpu.SemaphoreType.DMA((3,)), pltpu.SemaphoreType.DMA(())],
    )(src)


def _sibling_swap(slab, name):
    def body(src_ref, out_ref, send_sem, recv_sem):
        x, y, c = lax.axis_index("x"), lax.axis_index("y"), lax.axis_index("c")
        cp = pltpu.make_async_remote_copy(src_ref=src_ref, dst_ref=out_ref, send_sem=send_sem, recv_sem=recv_sem,
                                          device_id=(x, y, 1 - c), device_id_type=MESH_ID)
        cp.start()
        cp.wait_recv()
        cp.wait_send()

    return pl.pallas_call(
        body, name=name, in_specs=[HBM_SPEC], out_specs=HBM_SPEC, out_shape=jax.ShapeDtypeStruct(slab.shape, slab.dtype),
        scratch_shapes=[pltpu.SemaphoreType.DMA(()), pltpu.SemaphoreType.DMA(())])(slab)


def _sum_slabs(r):
    rows = r.shape[1]
    tb = _div(rows, 512, 8)

    def body(r0, r1, r2, r3, o_ref):
        o_ref[...] = ((r0[0].astype(F32) + r1[0].astype(F32)) + r2[0].astype(F32)) + r3[0].astype(F32)

    spec = lambda j: pl.BlockSpec((1, tb, PACK_W), lambda i: (j, i, 0))
    return pl.pallas_call(
        body, name="sum_slabs", grid=(rows // tb,), in_specs=[spec(0), spec(1), spec(2), spec(3)],
        out_specs=pl.BlockSpec((tb, PACK_W), lambda i: (i, 0)), out_shape=jax.ShapeDtypeStruct((rows, PACK_W), F32),
        compiler_params=_cp("parallel"))(r, r, r, r)


def _adamw(w, m, v, p, q):
    rows = w.shape[0]
    tb = _div(rows, 512, 8)

    def body(w_ref, m_ref, v_ref, p_ref, q_ref, g_ref, d_ref, nm_ref, nv_ref):
        g = p_ref[...] + q_ref[...]
        m_new = B1 * m_ref[...] + (1.0 - B1) * g
        v_new = B2 * v_ref[...] + (1.0 - B2) * (g * g)
        g_ref[...] = g
        nm_ref[...] = m_new
        nv_ref[...] = v_new
        m_hat = m_new / (1.0 - B1 ** STEP)
        v_hat = v_new / (1.0 - B2 ** STEP)
        d_ref[...] = -LR * (m_hat / (jnp.sqrt(v_hat) + AEPS) + WD * w_ref[...])

    spec = pl.BlockSpec((tb, PACK_W), lambda i: (i, 0))
    return pl.pallas_call(
        body, name="adamw", grid=(rows // tb,), in_specs=[spec] * 5, out_specs=[spec] * 4,
        out_shape=[jax.ShapeDtypeStruct((rows, PACK_W), F32)] * 4, compiler_params=_cp("parallel"))(w, m, v, p, q)


def _split_w_in(w_in):
    o1 = 3 * FW
    o2 = o1 + HEADS
    o3 = o2 + 3 * FW
    o4 = o3 + FW
    o5 = o4 + HEADS
    o6 = o5 + HEADS
    pad = jnp.zeros((w_in.shape[0], SMALL_W - 3 * HEADS), w_in.dtype)
    rest = jnp.concatenate([w_in[:, o2:o3], w_in[:, o3:o4], w_in[:, o6:], w_in[:, o1:o2], w_in[:, o4:o5], w_in[:, o5:o6], pad],
                           axis=1)
    return w_in[:, :o1], rest


def _gather_weights(shards):
    sharded = [(n, s, a) for n, s, a in WEIGHTS if a is not None]
    hi = {n: shards[n].astype(jnp.bfloat16) for n, _, _ in sharded}
    lo = [(shards[n] - hi[n].astype(F32)).astype(jnp.bfloat16) for n in SPLIT_F32]
    slab = _pack([hi[n] for n, _, _ in sharded] + lo, jnp.bfloat16)
    got = _chip_exchange(slab[None], "gather_weights", scatter=False)
    shapes = [_shard_shape(s, a) for _, s, a in sharded] + [_shard_shape(dict((n, s) for n, s, _ in WEIGHTS)[n], 1) for n in SPLIT_F32]
    per_chip = [_unpack(got[j], shapes) for j in range(N_CHIPS)]
    full = {}
    for i, (n, _, a) in enumerate(sharded):
        full[n] = jnp.concatenate([per_chip[j][i] for j in range(N_CHIPS)], axis=a)
    for i, n in enumerate(SPLIT_F32):
        lo_full = jnp.concatenate([per_chip[j][len(sharded) + i] for j in range(N_CHIPS)], axis=1)
        full[n] = full[n].astype(F32) + lo_full.astype(F32)
    return full


def kernel(x, meta_tokens, w_in, fgt_bias, gdn_conv_w, gdn_a_log, gdn_dt_bias, gdn_norm_w, gate_bias, w_branch_fox, w_branch_gdn, w_out, norm_mix_w, norm_ffn_w, ffn_w_up, ffn_conv_w, ffn_conv_b, ffn_w_down, norm_final_w, loss_target, m_meta_tokens, m_w_in, m_fgt_bias, m_gdn_conv_w, m_gdn_a_log, m_gdn_dt_bias, m_gdn_norm_w, m_gate_bias, m_w_branch_fox, m_w_branch_gdn, m_w_out, m_norm_mix_w, m_norm_ffn_w, m_ffn_w_up, m_ffn_conv_w, m_ffn_conv_b, m_ffn_w_down, m_norm_final_w, v_meta_tokens, v_w_in, v_fgt_bias, v_gdn_conv_w, v_gdn_a_log, v_gdn_dt_bias, v_gdn_norm_w, v_gate_bias, v_w_branch_fox, v_w_branch_gdn, v_w_out, v_norm_mix_w, v_norm_ffn_w, v_ffn_w_up, v_ffn_conv_w, v_ffn_conv_b, v_ffn_w_down, v_norm_final_w):
    weights = dict(meta_tokens=meta_tokens, w_in=w_in, fgt_bias=fgt_bias, gdn_conv_w=gdn_conv_w, gdn_a_log=gdn_a_log, gdn_dt_bias=gdn_dt_bias, gdn_norm_w=gdn_norm_w, gate_bias=gate_bias, w_branch_fox=w_branch_fox, w_branch_gdn=w_branch_gdn, w_out=w_out, norm_mix_w=norm_mix_w, norm_ffn_w=norm_ffn_w, ffn_w_up=ffn_w_up, ffn_conv_w=ffn_conv_w, ffn_conv_b=ffn_conv_b, ffn_w_down=ffn_w_down, norm_final_w=norm_final_w)
    m_in = dict(meta_tokens=m_meta_tokens, w_in=m_w_in, fgt_bias=m_fgt_bias, gdn_conv_w=m_gdn_conv_w, gdn_a_log=m_gdn_a_log, gdn_dt_bias=m_gdn_dt_bias, gdn_norm_w=m_gdn_norm_w, gate_bias=m_gate_bias, w_branch_fox=m_w_branch_fox, w_branch_gdn=m_w_branch_gdn, w_out=m_w_out, norm_mix_w=m_norm_mix_w, norm_ffn_w=m_norm_ffn_w, ffn_w_up=m_ffn_w_up, ffn_conv_w=m_ffn_conv_w, ffn_conv_b=m_ffn_conv_b, ffn_w_down=m_ffn_w_down, norm_final_w=m_norm_final_w)
    v_in = dict(meta_tokens=v_meta_tokens, w_in=v_w_in, fgt_bias=v_fgt_bias, gdn_conv_w=v_gdn_conv_w, gdn_a_log=v_gdn_a_log, gdn_dt_bias=v_gdn_dt_bias, gdn_norm_w=v_gdn_norm_w, gate_bias=v_gate_bias, w_branch_fox=v_w_branch_fox, w_branch_gdn=v_w_branch_gdn, w_out=v_w_out, norm_mix_w=v_norm_mix_w, norm_ffn_w=v_norm_ffn_w, ffn_w_up=v_ffn_w_up, ffn_conv_w=v_ffn_conv_w, ffn_conv_b=v_ffn_conv_b, ffn_w_down=v_ffn_w_down, norm_final_w=v_norm_final_w)
    shard2d = {n: _shard_shape(s, a) for n, s, a in WEIGHTS}
    as2d = lambda d: {n: d[n].reshape(shard2d[n]) for n, _, _ in WEIGHTS}
    w2, m2, v2 = as2d(weights), as2d(m_in), as2d(v_in)

    full = _gather_weights(w2)
    w_fox, w_rest = _split_w_in(full["w_in"])
    flat = lambda n: w2[n].reshape(-1)
    local_w = dict(
        meta=full["meta_tokens"], w_fox=w_fox, w_rest=w_rest, fgt_bias=flat("fgt_bias"), gdn_conv=full["gdn_conv_w"],
        a_log=flat("gdn_a_log"), dt_bias=flat("gdn_dt_bias"), gdn_norm=flat("gdn_norm_w"), gate_bias=flat("gate_bias"),
        w_bfox=full["w_branch_fox"], w_bgdn=full["w_branch_gdn"], w_out=full["w_out"], norm_mix=flat("norm_mix_w"),
        norm_ffn=flat("norm_ffn_w"), w_up=full["ffn_w_up"], ffn_conv=full["ffn_conv_w"], ffn_conv_b=flat("ffn_conv_b"),
        w_down=full["ffn_w_down"], norm_final=flat("norm_final_w"))

    loss, grad_x, grads = _local_step(x[0], loss_target[0], local_w)

    g2 = {n: grads[n].reshape(s) for n, s, _ in WEIGHTS}
    send = jnp.stack([_pack([_shard_of(g2[n], a, j) for n, _, a in WEIGHTS], jnp.bfloat16) for j in range(N_CHIPS)])
    part = _sum_slabs(_chip_exchange(send, "scatter_grads", scatter=True))
    other = _sibling_swap(part, "swap_grads")
    order = [n for n, _, _ in WEIGHTS]
    slab = lambda d: _pack([d[n] for n in order], F32)
    outs = _adamw(slab(w2), slab(m2), slab(v2), part, other)
    shapes = [weights[n].shape for n in order]
    g_out, d_out, m_out, v_out = [_unpack(o, shapes) for o in outs]
    total = lax.psum(loss[0, 0], ("x", "y", "c"))
    return (total, grad_x[None], *g_out, *d_out, *m_out, *v_out)
```

```python
import functools

import numpy as np
import jax
import jax.numpy as jnp
from jax import lax
from jax.experimental import pallas as pl
from jax.experimental.pallas import tpu as pltpu

F32 = jnp.float32
_BF = jnp.bfloat16
HI = lax.Precision.HIGHEST

D = 1024
N_META = 16
CH = 64
FRONT = CH - N_META
X0 = CH
HEADS = 8
HD = 64
FW = HEADS * HD
DFF = 2816
EPS = 1e-6
NEG = -1e30
T_ALIGN = 256
SMALL_W = 128
REST_W = 3 * FW + FW + 2 * D + SMALL_W
OFF_Z = 3 * FW
OFF_GATES = 4 * FW
OFF_SMALL = 4 * FW + 2 * D

LR, B1, B2, AEPS, WD, STEP = 0.001, 0.9, 0.999, 1e-08, 0.01, 10

VMEM_LIMIT = 56 * 1024 * 1024
ROW_TILE = 640
MM_TM, MM_TN, MM_TK = 1280, 512, 2816
ATT_TILE = 640


def _div(n, target, mult=128):
    if n <= target:
        return n
    best = None
    for d in range(mult, target + 1, mult):
        if n % d == 0:
            best = d
    assert best is not None, (n, target, mult)
    return best


def _cp(*sem):
    return pltpu.CompilerParams(dimension_semantics=sem, vmem_limit_bytes=VMEM_LIMIT)


def _sig(x):
    return 1.0 / (1.0 + jnp.exp(-x))


MM_VMEM_BUDGET = 40 * 1024 * 1024
MM_STEP_BYTES = 1 << 20


def _mm_tiles(m, n, k, sa, sb, so, has_resid):
    divs = lambda d, cap: [t for t in range(128, min(d, cap) + 1, 128) if d % t == 0] or [d]
    best = None
    for tm in divs(m, MM_TM * 2):
        for tn in divs(n, 4608):
            fixed = tm * tn * (4 + 2 * so + (8 if has_resid else 0))
            tks = [t for t in divs(k, MM_TK) if 2 * t * (tm * sa + tn * sb) + fixed <= MM_VMEM_BUDGET]
            if not tks:
                continue
            tk = tks[-1]
            steps = (m // tm) * (n // tn) * (k // tk)
            rmw = (k // tk - 1) * m * n * 4
            cost = (n // tn) * m * k * sa + (m // tm) * k * n * sb + steps * MM_STEP_BYTES + rmw
            if best is None or cost < best[0]:
                best = (cost, tm, tn, tk)
    assert best is not None, (m, n, k)
    return best[1:]


def _mm_t(a, b, name):
    return _mm(a, b, ta=True, name=name)


def _mm(a, b, *, ta=False, tb=False, out_dtype=F32, resid=None, name):
    K, M = a.shape if ta else a.shape[::-1]
    N = b.shape[0] if tb else b.shape[1]
    assert (b.shape[1] if tb else b.shape[0]) == K
    tm, tn, tk = _mm_tiles(M, N, K, a.dtype.itemsize, b.dtype.itemsize, jnp.dtype(out_dtype).itemsize, resid is not None)
    nk = K // tk
    dims = (((0 if ta else 1,), (1 if tb else 0,)), ((), ()))
    mxu = _BF

    def body(*refs):
        if resid is None:
            a_ref, b_ref, o_ref, acc = refs
            r_ref = None
        else:
            a_ref, b_ref, r_ref, o_ref, acc = refs
        k = pl.program_id(2)
        part = lax.dot_general(a_ref[...].astype(mxu), b_ref[...].astype(mxu), dims, preferred_element_type=F32)

        @pl.when(k == 0)
        def _():
            acc[...] = part

        @pl.when(k > 0)
        def _():
            acc[...] += part

        @pl.when(k == nk - 1)
        def _():
            r = acc[...]
            if r_ref is not None:
                r = r + r_ref[...]
            o_ref[...] = r.astype(o_ref.dtype)

    a_spec = pl.BlockSpec((tk, tm), lambda i, j, k: (k, i)) if ta else pl.BlockSpec((tm, tk), lambda i, j, k: (i, k))
    b_spec = pl.BlockSpec((tn, tk), lambda i, j, k: (j, k)) if tb else pl.BlockSpec((tk, tn), lambda i, j, k: (k, j))
    o_spec = pl.BlockSpec((tm, tn), lambda i, j, k: (i, j))
    in_specs = [a_spec, b_spec] + ([o_spec] if resid is not None else [])
    args = (a, b) + ((resid,) if resid is not None else ())
    return pl.pallas_call(
        body, name=name, grid=(M // tm, N // tn, nk), in_specs=in_specs, out_specs=o_spec,
        out_shape=jax.ShapeDtypeStruct((M, N), out_dtype), scratch_shapes=[pltpu.VMEM((tm, tn), F32)],
        compiler_params=_cp("parallel", "parallel", "arbitrary"))(*args)


def _rms_fwd(h, w, name):
    T = h.shape[0]
    tb = _div(T, ROW_TILE, 8)

    def body(h_ref, w_ref, o_ref):
        x = h_ref[...]
        r = lax.rsqrt(jnp.mean(x * x, axis=-1, keepdims=True) + EPS)
        o_ref[...] = (x * r * w_ref[...]).astype(o_ref.dtype)

    return pl.pallas_call(
        body, name=name, grid=(T // tb,),
        in_specs=[pl.BlockSpec((tb, D), lambda i: (i, 0)), pl.BlockSpec((1, D), lambda i: (0, 0))],
        out_specs=pl.BlockSpec((tb, D), lambda i: (i, 0)), out_shape=jax.ShapeDtypeStruct((T, D), _BF),
        compiler_params=_cp("parallel"))(h, w)


def _rms_bwd(h, w, dy, dres, name):
    T = h.shape[0]
    tb = _div(T, ROW_TILE, 8)

    def body(h_ref, w_ref, dy_ref, dr_ref, dh_ref, dw_ref):
        x = h_ref[...]
        r = lax.rsqrt(jnp.mean(x * x, axis=-1, keepdims=True) + EPS)
        xh = x * r
        dy = dy_ref[...]
        g = dy * w_ref[...]
        dh_ref[...] = dr_ref[...] + r * (g - xh * jnp.mean(xh * g, axis=-1, keepdims=True))
        part = jnp.sum(dy * xh, axis=0, keepdims=True)

        @pl.when(pl.program_id(0) == 0)
        def _():
            dw_ref[...] = part

        @pl.when(pl.program_id(0) > 0)
        def _():
            dw_ref[...] += part

    row = pl.BlockSpec((tb, D), lambda i: (i, 0))
    one = pl.BlockSpec((1, D), lambda i: (0, 0))
    return pl.pallas_call(
        body, name=name, grid=(T // tb,), in_specs=[row, one, row, row], out_specs=[row, one],
        out_shape=[jax.ShapeDtypeStruct((T, D), F32), jax.ShapeDtypeStruct((1, D), F32)],
        compiler_params=_cp("arbitrary"))(h, w, dy, dres)


def _fgate_fwd(small, bias):
    T = small.shape[0]
    tb = _div(T, ROW_TILE, 8)

    def body(s_ref, b_ref, c_ref, carry):
        @pl.when(pl.program_id(0) == 0)
        def _():
            carry[...] = jnp.zeros_like(carry)

        lf = jax.nn.log_sigmoid(s_ref[...] + b_ref[...])
        r = lax.broadcasted_iota(jnp.int32, (tb, tb), 0)
        c = lax.broadcasted_iota(jnp.int32, (tb, tb), 1)
        tri = (r >= c).astype(F32)
        cs = jnp.dot(tri, lf, precision=HI, preferred_element_type=F32) + carry[...]
        c_ref[...] = cs
        carry[...] = cs[tb - 1:tb, :]

    return pl.pallas_call(
        body, name="fgate_fwd", grid=(T // tb,),
        in_specs=[pl.BlockSpec((tb, SMALL_W), lambda i: (i, 0)), pl.BlockSpec((1, SMALL_W), lambda i: (0, 0))],
        out_specs=pl.BlockSpec((tb, SMALL_W), lambda i: (i, 0)), out_shape=jax.ShapeDtypeStruct((T, SMALL_W), F32),
        scratch_shapes=[pltpu.VMEM((1, SMALL_W), F32)], compiler_params=_cp("arbitrary"))(small, bias)


def _fgate_bwd(small, bias, dc):
    T = small.shape[0]
    tb = _div(T, ROW_TILE, 8)
    nb = T // tb

    def body(s_ref, b_ref, dc_ref, df_ref, db_ref, carry):
        @pl.when(pl.program_id(0) == 0)
        def _():
            carry[...] = jnp.zeros_like(carry)

        r = lax.broadcasted_iota(jnp.int32, (tb, tb), 0)
        c = lax.broadcasted_iota(jnp.int32, (tb, tb), 1)
        tri = (r <= c).astype(F32)
        dlf = jnp.dot(tri, dc_ref[...], precision=HI, preferred_element_type=F32) + carry[...]
        carry[...] = dlf[0:1, :]
        df = dlf * _sig(-(s_ref[...] + b_ref[...]))
        df_ref[...] = df
        part = jnp.sum(df, axis=0, keepdims=True)

        @pl.when(pl.program_id(0) == 0)
        def _():
            db_ref[...] = part

        @pl.when(pl.program_id(0) > 0)
        def _():
            db_ref[...] += part

    rev = pl.BlockSpec((tb, SMALL_W), lambda i: (nb - 1 - i, 0))
    one = pl.BlockSpec((1, SMALL_W), lambda i: (0, 0))
    return pl.pallas_call(
        body, name="fgate_bwd", grid=(nb,), in_specs=[rev, one, rev], out_specs=[rev, one],
        out_shape=[jax.ShapeDtypeStruct((T, SMALL_W), F32), jax.ShapeDtypeStruct((1, SMALL_W), F32)],
        scratch_shapes=[pltpu.VMEM((1, SMALL_W), F32)], compiler_params=_cp("arbitrary"))(small, bias, dc)


def _att_rows(a8, tb):
    T = a8.shape[0]
    return a8.T.reshape(HEADS // 2, 2, T // tb, tb).transpose(0, 2, 1, 3)


def _att_cols(a8):
    return jnp.repeat(a8, HD, axis=1)


EXP_ZERO = -104.0
SKIP_SLACK = 2.0


def _att_stats(qkv, c_cols, ones_blk):
    T = qkv.shape[0]
    tb = _div(T, ATT_TILE)

    def body(q_ref, k_ref, c_ref, e_ref, o_ref):
        q = q_ref[...].astype(F32)
        k = k_ref[...].astype(F32)
        qn = jnp.max(jnp.dot(q * q, e_ref[...], precision=HI, preferred_element_type=F32), axis=0, keepdims=True)
        kn = jnp.max(jnp.dot(k * k, e_ref[...], precision=HI, preferred_element_type=F32), axis=0, keepdims=True)
        c = c_ref[...]
        o_ref[0] = jnp.concatenate([jnp.sqrt(qn), jnp.sqrt(kn), jnp.max(c, axis=0, keepdims=True),
                                    jnp.min(c, axis=0, keepdims=True), jnp.zeros((4, FW), F32)], axis=0)

    blk = lambda off: pl.BlockSpec((tb, FW), lambda i: (i, off))
    return pl.pallas_call(
        body, name="att_stats", grid=(T // tb,),
        in_specs=[blk(0), blk(1), blk(0), pl.BlockSpec((FW, FW), lambda i: (0, 0))],
        out_specs=pl.BlockSpec((1, 8, FW), lambda i: (i, 0, 0)), out_shape=jax.ShapeDtypeStruct((T // tb, 8, FW), F32),
        compiler_params=_cp("parallel"))(qkv, qkv, c_cols, ones_blk)


def _att_plan(stats):
    nb = stats.shape[0]
    st = stats[:, :4, ::HD]
    qmax, kmax, cmax, cmin = (st[:, r, :].T for r in range(4))
    bound = (HD ** -0.5) * 1.01 * qmax[:, :, None] * (kmax[:, None, :] + kmax[:, :, None]) \
        + cmax[:, :, None] - cmin[:, None, :] + SKIP_SLACK
    ii = lax.broadcasted_iota(jnp.int32, (nb, nb), 0)
    jj = lax.broadcasted_iota(jnp.int32, (nb, nb), 1)
    skip = (bound < EXP_ZERO) & (jj < ii)[None]
    live = (~skip & (jj <= ii)[None]).reshape(HEADS // 2, 2, nb, nb).any(axis=1)
    jfirst = jnp.argmax(live, axis=2).astype(jnp.int32)
    ilast = (nb - 1 - jnp.argmax(live[:, ::-1, :], axis=1)).astype(jnp.int32)
    return skip.astype(jnp.int32).reshape(-1), jfirst.reshape(-1), ilast.reshape(-1)


def _fox_fwd(qkv, c_cols, c_rows, plan):
    T = qkv.shape[0]
    tb = _div(T, ATT_TILE)
    nb = T // tb
    npair = HEADS // 2
    scale = HD ** -0.5
    dn = (((1,), (1,)), ((), ()))

    def body(skip_ref, jfirst_ref, ilast_ref, q_ref, k_ref, v_ref, cq_ref, ck_ref, o_ref, l_ref, m_scr, l_scr, acc_scr):
        pr = pl.program_id(0)
        i = pl.program_id(1)
        q = q_ref[...]
        lane = lax.broadcasted_iota(jnp.int32, (1, 2 * HD), 1)
        sel0 = lane < HD
        zero = jnp.zeros_like(q)
        qh = (jnp.where(sel0, q, zero) * scale, jnp.where(sel0, zero, q) * scale)
        cq = (cq_ref[:, 0:1], cq_ref[:, HD:HD + 1])
        row = i * tb + lax.broadcasted_iota(jnp.int32, (tb, 1), 0)
        m_scr[...] = jnp.full(m_scr.shape, NEG, F32)
        l_scr[...] = jnp.zeros(l_scr.shape, F32)
        acc_scr[...] = jnp.zeros(acc_scr.shape, F32)

        def step(j, _):
            start = pl.multiple_of(j * tb, tb)
            for h in range(2):
                @pl.when(skip_ref[((2 * pr + h) * nb + i) * nb + j] == 0)
                def _():
                    kb = k_ref[pl.ds(start, tb), :]
                    vb = v_ref[pl.ds(start, tb), :]
                    col = j * tb + lax.broadcasted_iota(jnp.int32, (1, tb), 1)
                    mask = (col <= row) & (col >= FRONT)
                    s = lax.dot_general(qh[h], kb, dn, preferred_element_type=F32) + cq[h] - ck_ref[0, j, h:h + 1, :]
                    s = jnp.where(mask, s, NEG)
                    m = m_scr[h]
                    m_new = jnp.maximum(m, jnp.max(s, axis=-1, keepdims=True))
                    p = jnp.where(mask, jnp.exp(s - m_new), 0.0)
                    alpha = jnp.exp(m - m_new)
                    m_scr[h] = m_new
                    l_scr[h] = l_scr[h] * alpha + jnp.sum(p, axis=-1, keepdims=True)
                    acc_scr[h] = acc_scr[h] * alpha + jnp.dot(p.astype(vb.dtype), vb, preferred_element_type=F32)
            return 0

        lax.fori_loop(jfirst_ref[pr * nb + i], i + 1, step, 0)
        outs, lses = [], []
        for h in range(2):
            l = l_scr[h]
            ok = l > 0.0
            ls = jnp.where(ok, l, 1.0)
            outs.append(jnp.where(ok, acc_scr[h] / ls, 0.0))
            lses.append(jnp.where(ok, m_scr[h] + jnp.log(ls), 0.0))
        o_ref[...] = jnp.where(sel0, outs[0], outs[1])
        l_ref[...] = jnp.where(sel0, lses[0], lses[1])

    blk = lambda off: pl.BlockSpec((tb, 2 * HD), lambda p, i, *_: (i, off + p))
    full = lambda off: pl.BlockSpec((T, 2 * HD), lambda p, i, *_: (0, off + p))
    rows = pl.BlockSpec((1, nb, 2, tb), lambda p, i, *_: (p, 0, 0, 0))
    return pl.pallas_call(
        body, name="fox_fwd",
        grid_spec=pltpu.PrefetchScalarGridSpec(
            num_scalar_prefetch=3, grid=(npair, nb),
            in_specs=[blk(0), full(npair), full(2 * npair), blk(0), rows], out_specs=[blk(0), blk(0)],
            scratch_shapes=[pltpu.VMEM((2, tb, 1), F32), pltpu.VMEM((2, tb, 1), F32), pltpu.VMEM((2, tb, 2 * HD), F32)]),
        out_shape=[jax.ShapeDtypeStruct((T, FW), F32), jax.ShapeDtypeStruct((T, FW), F32)],
        compiler_params=_cp("parallel", "arbitrary"))(*plan, qkv, qkv, qkv, c_cols, c_rows)


def _fox_bwd_dq(qkv, do, c_cols, c_rows, lse, delta, plan):
    T = qkv.shape[0]
    tb = _div(T, ATT_TILE)
    nb = T // tb
    npair = HEADS // 2
    scale = HD ** -0.5
    dn = (((1,), (1,)), ((), ()))

    def body(skip_ref, jfirst_ref, ilast_ref, q_ref, k_ref, v_ref, do_ref, cq_ref, ck_ref, l_ref, dl_ref, dq_ref, dc_ref,
             dq_scr, dc_scr):
        pr = pl.program_id(0)
        i = pl.program_id(1)
        q = q_ref[...]
        do = do_ref[...]
        lane = lax.broadcasted_iota(jnp.int32, (1, 2 * HD), 1)
        sel0 = lane < HD
        qh = (jnp.where(sel0, q, jnp.zeros_like(q)) * scale, jnp.where(sel0, jnp.zeros_like(q), q) * scale)
        doh = (jnp.where(sel0, do, jnp.zeros_like(do)), jnp.where(sel0, jnp.zeros_like(do), do))
        cq = (cq_ref[:, 0:1], cq_ref[:, HD:HD + 1])
        ls = (l_ref[:, 0:1], l_ref[:, HD:HD + 1])
        dl = (dl_ref[:, 0:1], dl_ref[:, HD:HD + 1])
        row = i * tb + lax.broadcasted_iota(jnp.int32, (tb, 1), 0)
        dq_scr[...] = jnp.zeros(dq_scr.shape, F32)
        dc_scr[...] = jnp.zeros(dc_scr.shape, F32)

        def step(j, _):
            start = pl.multiple_of(j * tb, tb)
            for h in range(2):
                @pl.when(skip_ref[((2 * pr + h) * nb + i) * nb + j] == 0)
                def _():
                    kb = k_ref[pl.ds(start, tb), :]
                    vb = v_ref[pl.ds(start, tb), :]
                    col = j * tb + lax.broadcasted_iota(jnp.int32, (1, tb), 1)
                    mask = (col <= row) & (col >= FRONT)
                    s = lax.dot_general(qh[h], kb, dn, preferred_element_type=F32) + cq[h] - ck_ref[0, j, h:h + 1, :]
                    p = jnp.where(mask, jnp.exp(jnp.where(mask, s, NEG) - ls[h]), 0.0)
                    dp = lax.dot_general(doh[h], vb, dn, preferred_element_type=F32)
                    ds = p * (dp - dl[h])
                    dq_scr[h] += jnp.dot(ds.astype(kb.dtype), kb, preferred_element_type=F32)
                    dc_scr[h] += jnp.sum(ds, axis=-1, keepdims=True)
            return 0

        lax.fori_loop(jfirst_ref[pr * nb + i], i + 1, step, 0)
        dq_ref[...] = jnp.where(sel0, dq_scr[0], dq_scr[1]) * scale
        dc_ref[...] = jnp.where(sel0, dc_scr[0], dc_scr[1])

    blk = lambda off: pl.BlockSpec((tb, 2 * HD), lambda p, i, *_: (i, off + p))
    full = lambda off: pl.BlockSpec((T, 2 * HD), lambda p, i, *_: (0, off + p))
    rows = pl.BlockSpec((1, nb, 2, tb), lambda p, i, *_: (p, 0, 0, 0))
    return pl.pallas_call(
        body, name="fox_bwd_dq",
        grid_spec=pltpu.PrefetchScalarGridSpec(
            num_scalar_prefetch=3, grid=(npair, nb),
            in_specs=[blk(0), full(npair), full(2 * npair), blk(0), blk(0), rows, blk(0), blk(0)], out_specs=[blk(0), blk(0)],
            scratch_shapes=[pltpu.VMEM((2, tb, 2 * HD), F32), pltpu.VMEM((2, tb, 1), F32)]),
        out_shape=[jax.ShapeDtypeStruct((T, FW), F32), jax.ShapeDtypeStruct((T, FW), F32)],
        compiler_params=_cp("parallel", "arbitrary"))(*plan, qkv, qkv, qkv, do, c_cols, c_rows, lse, delta)


def _fox_bwd_dkv(qkv, do, c_cols, c_rows, lse_rows, delta_rows, plan):
    T = qkv.shape[0]
    tb = _div(T, ATT_TILE)
    nb = T // tb
    npair = HEADS // 2
    scale = HD ** -0.5
    dn = (((1,), (1,)), ((), ()))

    def body(skip_ref, jfirst_ref, ilast_ref, q_ref, k_ref, v_ref, do_ref, ck_ref, cq_ref, l_ref, dl_ref,
             dk_ref, dv_ref, dc_ref, dk_scr, dv_scr, dc_scr):
        pr = pl.program_id(0)
        jb = pl.program_id(1)
        k = k_ref[...]
        v = v_ref[...]
        lane = lax.broadcasted_iota(jnp.int32, (1, 2 * HD), 1)
        sel0 = lane < HD
        kh = (jnp.where(sel0, k, jnp.zeros_like(k)) * scale, jnp.where(sel0, jnp.zeros_like(k), k) * scale)
        vh = (jnp.where(sel0, v, jnp.zeros_like(v)), jnp.where(sel0, jnp.zeros_like(v), v))
        ck = (ck_ref[:, 0:1], ck_ref[:, HD:HD + 1])
        kidx = jb * tb + lax.broadcasted_iota(jnp.int32, (tb, 1), 0)
        dk_scr[...] = jnp.zeros(dk_scr.shape, F32)
        dv_scr[...] = jnp.zeros(dv_scr.shape, F32)
        dc_scr[...] = jnp.zeros(dc_scr.shape, F32)

        def step(i, _):
            start = pl.multiple_of(i * tb, tb)
            for h in range(2):
                @pl.when(skip_ref[((2 * pr + h) * nb + i) * nb + jb] == 0)
                def _():
                    qb = q_ref[pl.ds(start, tb), :]
                    dob = do_ref[pl.ds(start, tb), :]
                    qidx = i * tb + lax.broadcasted_iota(jnp.int32, (1, tb), 1)
                    mask = (kidx <= qidx) & (kidx >= FRONT)
                    st = lax.dot_general(kh[h], qb, dn, preferred_element_type=F32) + cq_ref[0, i, h:h + 1, :] - ck[h]
                    pt = jnp.where(mask, jnp.exp(jnp.where(mask, st, NEG) - l_ref[0, i, h:h + 1, :]), 0.0)
                    dv_scr[h] += jnp.dot(pt.astype(dob.dtype), dob, preferred_element_type=F32)
                    dpt = lax.dot_general(vh[h], dob, dn, preferred_element_type=F32)
                    dst = pt * (dpt - dl_ref[0, i, h:h + 1, :])
                    dk_scr[h] += jnp.dot(dst.astype(qb.dtype), qb, preferred_element_type=F32)
                    dc_scr[h] -= jnp.sum(dst, axis=-1, keepdims=True)
            return 0

        lax.fori_loop(jb, ilast_ref[pr * nb + jb] + 1, step, 0)
        dk_ref[...] = jnp.where(sel0, dk_scr[0], dk_scr[1]) * scale
        dv_ref[...] = jnp.where(sel0, dv_scr[0], dv_scr[1])
        dc_ref[...] = jnp.where(sel0, dc_scr[0], dc_scr[1])

    blk = lambda off: pl.BlockSpec((tb, 2 * HD), lambda p, j, *_: (j, off + p))
    full = lambda off: pl.BlockSpec((T, 2 * HD), lambda p, j, *_: (0, off + p))
    rows = pl.BlockSpec((1, nb, 2, tb), lambda p, j, *_: (p, 0, 0, 0))
    return pl.pallas_call(
        body, name="fox_bwd_dkv",
        grid_spec=pltpu.PrefetchScalarGridSpec(
            num_scalar_prefetch=3, grid=(npair, nb),
            in_specs=[full(0), blk(npair), blk(2 * npair), full(0), blk(0), rows, rows, rows],
            out_specs=[blk(0), blk(0), blk(0)],
            scratch_shapes=[pltpu.VMEM((2, tb, 2 * HD), F32), pltpu.VMEM((2, tb, 2 * HD), F32), pltpu.VMEM((2, tb, 1), F32)]),
        out_shape=[jax.ShapeDtypeStruct((T, FW), F32)] * 3,
        compiler_params=_cp("parallel", "arbitrary"))(*plan, qkv, qkv, qkv, do, c_cols, c_rows, lse_rows, delta_rows)


def _head_dot(a, b, ones_blk):
    T = a.shape[0]
    tb = _div(T, ROW_TILE, 8)

    def body(a_ref, b_ref, e_ref, o_ref):
        prod = a_ref[...].astype(F32) * b_ref[...].astype(F32)
        o_ref[...] = jnp.dot(prod, e_ref[...], precision=HI, preferred_element_type=F32)

    row = pl.BlockSpec((tb, FW), lambda i: (i, 0))
    return pl.pallas_call(
        body, name="head_dot", grid=(T // tb,), in_specs=[row, row, pl.BlockSpec((FW, FW), lambda i: (0, 0))],
        out_specs=row, out_shape=jax.ShapeDtypeStruct((T, FW), F32), compiler_params=_cp("parallel"))(a, b, ones_blk)


def _conv_windows(scr, x_ref, h_ref, first, kw, tb):
    scr[0:8, :] = jnp.where(first, 0.0, h_ref[...])
    scr[8:8 + tb, :] = x_ref[...]
    return [scr[8 - (kw - 1 - k):8 - (kw - 1 - k) + tb, :] for k in range(kw)]


def _halo_prev(tb, tc, off=0):
    return pl.BlockSpec((8, tc), lambda j, i: (jnp.maximum(i * (tb // 8) - 1, 0), j + off))


def _gconv_fwd(rest, w):
    T = rest.shape[0]
    C = 3 * FW
    kw = w.shape[0]
    tb, tc = _div(T, 1280, 8), 512

    def body(x_ref, h_ref, w_ref, o_ref, scr):
        win = _conv_windows(scr, x_ref, h_ref, pl.program_id(1) == 0, kw, tb)
        u = sum(w_ref[k:k + 1, :] * win[k] for k in range(kw))
        o_ref[...] = u * _sig(u)

    return pl.pallas_call(
        body, name="gconv_fwd", grid=(C // tc, T // tb),
        in_specs=[pl.BlockSpec((tb, tc), lambda j, i: (i, j)), _halo_prev(tb, tc), pl.BlockSpec((kw, tc), lambda j, i: (0, j))],
        out_specs=pl.BlockSpec((tb, tc), lambda j, i: (i, j)), out_shape=jax.ShapeDtypeStruct((T, C), F32),
        scratch_shapes=[pltpu.VMEM((tb + 8, tc), F32)], compiler_params=_cp("parallel", "arbitrary"))(rest, rest, w)


def _gconv_bwd_du(rest, w, dy):
    T = rest.shape[0]
    C = 3 * FW
    kw = w.shape[0]
    tb, tc = _div(T, 1280, 8), 512

    def body(x_ref, h_ref, w_ref, dy_ref, du_ref, dw_ref, scr):
        i = pl.program_id(1)
        win = _conv_windows(scr, x_ref, h_ref, i == 0, kw, tb)
        u = sum(w_ref[k:k + 1, :] * win[k] for k in range(kw))
        sg = _sig(u)
        du = dy_ref[...] * sg * (1.0 + u * (1.0 - sg))
        du_ref[...] = du
        part = jnp.concatenate([jnp.sum(du * win[k], axis=0, keepdims=True) for k in range(kw)], axis=0)

        @pl.when(i == 0)
        def _():
            dw_ref[...] = part

        @pl.when(i > 0)
        def _():
            dw_ref[...] += part

    blk = pl.BlockSpec((tb, tc), lambda j, i: (i, j))
    wsp = pl.BlockSpec((kw, tc), lambda j, i: (0, j))
    return pl.pallas_call(
        body, name="gconv_bwd_du", grid=(C // tc, T // tb), in_specs=[blk, _halo_prev(tb, tc), wsp, blk],
        out_specs=[blk, wsp], out_shape=[jax.ShapeDtypeStruct((T, C), F32), jax.ShapeDtypeStruct((kw, C), F32)],
        scratch_shapes=[pltpu.VMEM((tb + 8, tc), F32)], compiler_params=_cp("parallel", "arbitrary"))(rest, rest, w, dy)


def _conv_bwd_dx(du, w, out_dtype, name):
    T, C = du.shape
    kw = w.shape[0]
    tb = _div(T, 1280, 8)
    tc = _div(C, 512)
    nb = T // tb

    def body(x_ref, h_ref, w_ref, o_ref, scr):
        scr[0:tb, :] = x_ref[...]
        scr[tb:tb + 8, :] = jnp.where(pl.program_id(1) == nb - 1, 0.0, h_ref[...])
        acc = sum(w_ref[k:k + 1, :] * scr[kw - 1 - k:kw - 1 - k + tb, :] for k in range(kw))
        o_ref[...] = acc.astype(o_ref.dtype)

    halo = pl.BlockSpec((8, tc), lambda j, i: (jnp.minimum((i + 1) * (tb // 8), T // 8 - 1), j))
    return pl.pallas_call(
        body, name=name, grid=(C // tc, nb),
        in_specs=[pl.BlockSpec((tb, tc), lambda j, i: (i, j)), halo, pl.BlockSpec((kw, tc), lambda j, i: (0, j))],
        out_specs=pl.BlockSpec((tb, tc), lambda j, i: (i, j)), out_shape=jax.ShapeDtypeStruct((T, C), out_dtype),
        scratch_shapes=[pltpu.VMEM((tb + 8, tc), F32)], compiler_params=_cp("parallel", "arbitrary"))(du, du, w)


def _glu_fwd(up, w, b):
    T = up.shape[0]
    kw = w.shape[0]
    tb, tc = _div(T, 1280, 8), 256
    nc = DFF // tc

    def body(xg, hg, xu, hu, wg, wu, bg, bu, o_ref, sg, su):
        first = pl.program_id(1) == 0
        wing = _conv_windows(sg, xg, hg, first, kw, tb)
        winu = _conv_windows(su, xu, hu, first, kw, tb)
        ug = bg[...] + sum(wg[k:k + 1, :] * wing[k] for k in range(kw))
        uu = bu[...] + sum(wu[k:k + 1, :] * winu[k] for k in range(kw))
        o_ref[...] = (ug * _sig(ug) * uu).astype(o_ref.dtype)

    blk = lambda off: pl.BlockSpec((tb, tc), lambda j, i: (i, j + off))
    wsp = lambda off: pl.BlockSpec((kw, tc), lambda j, i: (0, j + off))
    bsp = lambda off: pl.BlockSpec((1, tc), lambda j, i: (0, j + off))
    return pl.pallas_call(
        body, name="glu_fwd", grid=(nc, T // tb),
        in_specs=[blk(0), _halo_prev(tb, tc), blk(nc), _halo_prev(tb, tc, nc), wsp(0), wsp(nc), bsp(0), bsp(nc)],
        out_specs=blk(0), out_shape=jax.ShapeDtypeStruct((T, DFF), _BF),
        scratch_shapes=[pltpu.VMEM((tb + 8, tc), F32)] * 2, compiler_params=_cp("parallel", "arbitrary"))(
            up, up, up, up, w, w, b, b)


def _glu_bwd_du(up, w, b, df):
    T = up.shape[0]
    kw = w.shape[0]
    tb, tc = _div(T, 1280, 8), 256
    nc = DFF // tc

    def body(xg, hg, xu, hu, wg, wu, bg, bu, df_ref, dug_ref, duu_ref, dwg_ref, dwu_ref, dbg_ref, dbu_ref, sg, su):
        i = pl.program_id(1)
        wing = _conv_windows(sg, xg, hg, i == 0, kw, tb)
        winu = _conv_windows(su, xu, hu, i == 0, kw, tb)
        ug = bg[...] + sum(wg[k:k + 1, :] * wing[k] for k in range(kw))
        uu = bu[...] + sum(wu[k:k + 1, :] * winu[k] for k in range(kw))
        s = _sig(ug)
        df = df_ref[...]
        dug = df * uu * s * (1.0 + ug * (1.0 - s))
        duu = df * ug * s
        dug_ref[...] = dug
        duu_ref[...] = duu
        parts = (jnp.concatenate([jnp.sum(dug * wing[k], axis=0, keepdims=True) for k in range(kw)], axis=0),
                 jnp.concatenate([jnp.sum(duu * winu[k], axis=0, keepdims=True) for k in range(kw)], axis=0),
                 jnp.sum(dug, axis=0, keepdims=True), jnp.sum(duu, axis=0, keepdims=True))
        accs = (dwg_ref, dwu_ref, dbg_ref, dbu_ref)

        @pl.when(i == 0)
        def _():
            for r, p in zip(accs, parts):
                r[...] = p

        @pl.when(i > 0)
        def _():
            for r, p in zip(accs, parts):
                r[...] += p

    blk = lambda off: pl.BlockSpec((tb, tc), lambda j, i: (i, j + off))
    wsp = lambda off: pl.BlockSpec((kw, tc), lambda j, i: (0, j + off))
    bsp = lambda off: pl.BlockSpec((1, tc), lambda j, i: (0, j + off))
    return pl.pallas_call(
        body, name="glu_bwd_du", grid=(nc, T // tb),
        in_specs=[blk(0), _halo_prev(tb, tc), blk(nc), _halo_prev(tb, tc, nc), wsp(0), wsp(nc), bsp(0), bsp(nc), blk(0)],
        out_specs=[blk(0), blk(0), wsp(0), wsp(0), bsp(0), bsp(0)],
        out_shape=[jax.ShapeDtypeStruct((T, DFF), F32)] * 2 + [jax.ShapeDtypeStruct((kw, DFF), F32)] * 2
        + [jax.ShapeDtypeStruct((1, DFF), F32)] * 2,
        scratch_shapes=[pltpu.VMEM((tb + 8, tc), F32)] * 2, compiler_params=_cp("parallel", "arbitrary"))(
            up, up, up, up, w, w, b, b, df)


def _mix_fwd(rest, gate_bias, y_fox, y_gdn):
    T = rest.shape[0]
    tb, tc = _div(T, ROW_TILE, 8), 512
    nc = D // tc
    og = OFF_GATES // tc

    def body(gf, gg, bf, bg, yf, yg, o_ref):
        o_ref[...] = (_sig(gf[...] + bf[...]) * yf[...] + _sig(gg[...] + bg[...]) * yg[...]).astype(o_ref.dtype)

    blk = lambda off: pl.BlockSpec((tb, tc), lambda i, j: (i, j + off))
    bsp = lambda off: pl.BlockSpec((1, tc), lambda i, j: (0, j + off))
    return pl.pallas_call(
        body, name="mix_fwd", grid=(T // tb, nc),
        in_specs=[blk(og), blk(og + nc), bsp(0), bsp(nc), blk(0), blk(0)], out_specs=blk(0),
        out_shape=jax.ShapeDtypeStruct((T, D), _BF), compiler_params=_cp("parallel", "parallel"))(
            rest, rest, gate_bias, gate_bias, y_fox, y_gdn)


def _mix_bwd(rest, gate_bias, y_fox, y_gdn, dmix):
    T = rest.shape[0]
    tb, tc = _div(T, ROW_TILE, 8), 512
    nc = D // tc
    og = OFF_GATES // tc

    def body(gf, gg, bf, bg, yf, yg, dm, dyf, dyg, dgf, dgg, dbf, dbg):
        i = pl.program_id(1)
        d = dm[...]
        sf = _sig(gf[...] + bf[...])
        sg = _sig(gg[...] + bg[...])
        dyf[...] = (d * sf).astype(dyf.dtype)
        dyg[...] = (d * sg).astype(dyg.dtype)
        a = d * yf[...] * sf * (1.0 - sf)
        b = d * yg[...] * sg * (1.0 - sg)
        dgf[...] = a
        dgg[...] = b
        pa = jnp.sum(a, axis=0, keepdims=True)
        pb = jnp.sum(b, axis=0, keepdims=True)

        @pl.when(i == 0)
        def _():
            dbf[...] = pa
            dbg[...] = pb

        @pl.when(i > 0)
        def _():
            dbf[...] += pa
            dbg[...] += pb

    blk = lambda off: pl.BlockSpec((tb, tc), lambda j, i: (i, j + off))
    bsp = lambda off: pl.BlockSpec((1, tc), lambda j, i: (0, j + off))
    return pl.pallas_call(
        body, name="mix_bwd", grid=(nc, T // tb),
        in_specs=[blk(og), blk(og + nc), bsp(0), bsp(nc), blk(0), blk(0), blk(0)],
        out_specs=[blk(0), blk(0), blk(0), blk(0), bsp(0), bsp(0)],
        out_shape=[jax.ShapeDtypeStruct((T, D), _BF)] * 2 + [jax.ShapeDtypeStruct((T, D), F32)] * 2
        + [jax.ShapeDtypeStruct((1, D), F32)] * 2,
        compiler_params=_cp("parallel", "arbitrary"))(rest, rest, gate_bias, gate_bias, y_fox, y_gdn, dmix)


def _loss_head(h2, w, target, n_valid):
    T = h2.shape[0]
    tb = _div(T, ROW_TILE, 8)

    def body(h_ref, w_ref, t_ref, dh_ref, loss_ref, dw_ref):
        i = pl.program_id(0)
        x = h_ref[...]
        r = lax.rsqrt(jnp.mean(x * x, axis=-1, keepdims=True) + EPS)
        xh = x * r
        row = i * tb + lax.broadcasted_iota(jnp.int32, (tb, 1), 0)
        valid = (row >= X0) & (row < X0 + n_valid)
        e = jnp.where(valid, xh * w_ref[...] - t_ref[...], 0.0)
        dy = e * (1.0 / D)
        g = dy * w_ref[...]
        dh_ref[...] = r * (g - xh * jnp.mean(xh * g, axis=-1, keepdims=True))
        lpart = 0.5 * jnp.sum(jnp.sum(e * e, axis=-1, keepdims=True) * (1.0 / D), axis=0, keepdims=True)
        wpart = jnp.sum(dy * xh, axis=0, keepdims=True)

        @pl.when(i == 0)
        def _():
            loss_ref[...] = lpart
            dw_ref[...] = wpart

        @pl.when(i > 0)
        def _():
            loss_ref[...] += lpart
            dw_ref[...] += wpart

    row = pl.BlockSpec((tb, D), lambda i: (i, 0))
    one = pl.BlockSpec((1, D), lambda i: (0, 0))
    return pl.pallas_call(
        body, name="loss_head", grid=(T // tb,), in_specs=[row, one, row],
        out_specs=[row, pl.BlockSpec((1, 1), lambda i: (0, 0)), one],
        out_shape=[jax.ShapeDtypeStruct((T, D), F32), jax.ShapeDtypeStruct((1, 1), F32), jax.ShapeDtypeStruct((1, D), F32)],
        compiler_params=_cp("arbitrary"))(h2, w, target)


def _bd_lo(a, b, ca, cb):
    return lax.dot_general(a.astype(_BF), b.astype(_BF), (((ca,), (cb,)), ((0,), (0,))), preferred_element_type=F32)


def _split2(a):
    hi = a.astype(_BF)
    return hi, (a - hi.astype(F32)).astype(_BF)


def _bd_hi(a, b, ca, cb, exact_a=False):
    dn = (((ca,), (cb,)), ((0,), (0,)))
    dot = lambda x, y: lax.dot_general(x, y, dn, preferred_element_type=F32)
    bh, bl = _split2(b)
    if exact_a:
        ah = a.astype(_BF)
        return dot(ah, bh) + dot(ah, bl)
    ah, al = _split2(a)
    return dot(ah, bh) + (dot(ah, bl) + dot(al, bh))


def _row_to_col(x):
    eye = lax.broadcasted_iota(jnp.int32, (1, CH, CH), 1) == lax.broadcasted_iota(jnp.int32, (1, CH, CH), 2)
    return jnp.sum(jnp.where(eye, jnp.broadcast_to(x, (HEADS, CH, CH)), 0.0), axis=2, keepdims=True)


def _col_to_row(x):
    eye = lax.broadcasted_iota(jnp.int32, (1, CH, CH), 1) == lax.broadcasted_iota(jnp.int32, (1, CH, CH), 2)
    return jnp.sum(jnp.where(eye, jnp.broadcast_to(x, (HEADS, CH, CH)), 0.0), axis=1, keepdims=True)


def _gdn_chunk(q, k, v, bpre, apre, alog, dtb):
    H = HEADS
    r = lax.broadcasted_iota(jnp.int32, (1, CH, CH), 1)
    c = lax.broadcasted_iota(jnp.int32, (1, CH, CH), 2)
    tril, strict = r >= c, r > c
    lb = jnp.broadcast_to(tril.astype(F32), (H, CH, CH))
    rq = lax.rsqrt(jnp.sum(q * q, axis=-1, keepdims=True) + EPS)
    rk = lax.rsqrt(jnp.sum(k * k, axis=-1, keepdims=True) + EPS)
    qh = q * rq
    qn = qh * (HD ** -0.5)
    kn = k * rk
    beta = _sig(bpre)
    x = apre + dtb
    ea = jnp.exp(alog)
    g = -ea * jax.nn.softplus(x)
    gb = jnp.broadcast_to(g, (H, CH, HD))
    gc = _bd_hi(lb, gb, 2, 1, True)
    dm = _bd_hi(lb, jnp.where(strict, gb, 0.0), 2, 1, True)
    decay = jnp.where(tril, jnp.exp(jnp.where(tril, dm, 0.0)), 0.0)
    eg = jnp.exp(gc)
    gl = gc[:, CH - 1:CH, :]
    egl = jnp.exp(gl - gc)
    cd = jnp.exp(gl)
    kb = kn * beta
    vb = v * beta
    kk = _bd_lo(kb, kn, 2, 2)
    a = jnp.where(strict, kk * decay, 0.0)
    tm = jnp.where(r == c, 1.0, 0.0) - a
    p = a
    for it in range(5):
        mul = _bd_hi if it < 2 else _bd_lo
        p = mul(p, p, 2, 1)
        tm = tm + mul(tm, p, 2, 1)
    kbg = kb * eg
    value = _bd_hi(tm, vb, 2, 1)
    kcd = _bd_hi(tm, kbg, 2, 1)
    qk = _bd_lo(qn, kn, 2, 2)
    attn = jnp.where(tril, qk * decay, 0.0)
    return dict(tril=tril, strict=strict, lb=lb, rq=rq, rk=rk, qh=qh, qn=qn, kn=kn, beta=beta, x=x, ea=ea, g=g,
                decay=decay, eg=eg, egl=egl, cd=cd, kb=kb, vb=vb, kk=kk, tm=tm, kbg=kbg, value=value, kcd=kcd, qk=qk,
                attn=attn, qd=qn * eg, kt=kn * egl)


def _gdn_specs(T, rev):
    nc = T // CH
    pos = (lambda n: nc - 1 - n) if rev else (lambda n: n)
    mat = pl.BlockSpec((HEADS, CH, HD), lambda n: (0, pos(n), 0))
    col = pl.BlockSpec((1, HEADS, 1, CH), lambda n: (pos(n), 0, 0, 0))
    sca = pl.BlockSpec((HEADS, 1, 1), lambda n: (0, 0, 0))
    nw = pl.BlockSpec((1, 1, HD), lambda n: (0, 0, 0))
    st = pl.BlockSpec((1, HEADS, HD, HD), lambda n: (pos(n), 0, 0, 0))
    return nc, mat, col, sca, nw, st


def _gdn_fwd(q, k, v, z, bpre, apre, alog, dtb, nw):
    T = q.shape[1]
    nc, mat, col, sca, nws, st = _gdn_specs(T, False)

    def body(q_ref, k_ref, v_ref, z_ref, b_ref, a_ref, al_ref, dt_ref, nw_ref, o_ref, og_ref, st_ref, s_scr):
        @pl.when(pl.program_id(0) == 0)
        def _():
            s_scr[...] = jnp.zeros_like(s_scr)

        L = _gdn_chunk(q_ref[...], k_ref[...], v_ref[...], _row_to_col(b_ref[0]), _row_to_col(a_ref[0]), al_ref[...],
                       dt_ref[...])
        s = s_scr[...]
        st_ref[0] = s
        v_new = L["value"] - _bd_lo(L["kcd"], s, 2, 1)
        o = _bd_lo(L["qd"], s, 2, 1) + _bd_lo(L["attn"], v_new, 2, 1)
        s_scr[...] = s * L["cd"] + _bd_lo(L["kt"], v_new, 1, 1)
        o_ref[...] = o
        zz = z_ref[...]
        rs = lax.rsqrt(jnp.mean(o * o, axis=-1, keepdims=True) + EPS)
        og_ref[...] = o * rs * nw_ref[...] * zz * _sig(zz)

    return pl.pallas_call(
        body, name="gdn_fwd", grid=(nc,), in_specs=[mat, mat, mat, mat, col, col, sca, sca, nws],
        out_specs=[mat, mat, st],
        out_shape=[jax.ShapeDtypeStruct((HEADS, T, HD), F32)] * 2 + [jax.ShapeDtypeStruct((nc, HEADS, HD, HD), F32)],
        scratch_shapes=[pltpu.VMEM((HEADS, HD, HD), F32)], compiler_params=_cp("arbitrary"))(
            q, k, v, z, bpre, apre, alog, dtb, nw)


def _gdn_bwd(q, k, v, z, bpre, apre, alog, dtb, nw, states, o, dog):
    T = q.shape[1]
    nc, mat, col, sca, nws, st = _gdn_specs(T, True)

    def body(q_ref, k_ref, v_ref, z_ref, b_ref, a_ref, al_ref, dt_ref, nw_ref, st_ref, o_ref, dog_ref,
             dq_ref, dk_ref, dv_ref, dz_ref, db_ref, da_ref, dal_ref, ddt_ref, dnw_ref, ds_scr):
        @pl.when(pl.program_id(0) == 0)
        def _():
            ds_scr[...] = jnp.zeros_like(ds_scr)
            dal_ref[...] = jnp.zeros_like(dal_ref)
            ddt_ref[...] = jnp.zeros_like(ddt_ref)
            dnw_ref[...] = jnp.zeros_like(dnw_ref)

        vv = v_ref[...]
        L = _gdn_chunk(q_ref[...], k_ref[...], vv, _row_to_col(b_ref[0]), _row_to_col(a_ref[0]), al_ref[...], dt_ref[...])
        tril, strict, lb = L["tril"], L["strict"], L["lb"]
        qn, kn, kb, beta, decay, eg, egl, cd = L["qn"], L["kn"], L["kb"], L["beta"], L["decay"], L["eg"], L["egl"], L["cd"]
        value, kcd, attn, qd, kt, tm = L["value"], L["kcd"], L["attn"], L["qd"], L["kt"], L["tm"]
        s = st_ref[0]
        v_new = value - _bd_lo(kcd, s, 2, 1)
        oo, zz, dog_ = o_ref[...], z_ref[...], dog_ref[...]
        sz = _sig(zz)
        rs = lax.rsqrt(jnp.mean(oo * oo, axis=-1, keepdims=True) + EPS)
        oh = oo * rs
        dz_ref[...] = dog_ * oh * nw_ref[...] * sz * (1.0 + zz * (1.0 - sz))
        don = dog_ * zz * sz
        gdy = don * nw_ref[...]
        do = rs * (gdy - oh * jnp.mean(oh * gdy, axis=-1, keepdims=True))
        dnw_ref[...] += jnp.sum(don * oh, axis=(0, 1), keepdims=True)
        dsn = ds_scr[...]
        d_vnew = _bd_lo(attn, do, 1, 1) + _bd_lo(kt, dsn, 2, 1)
        d_qd = _bd_lo(do, s, 2, 2)
        d_attn = jnp.where(tril, _bd_lo(do, v_new, 2, 2), 0.0)
        d_kt = _bd_lo(v_new, dsn, 2, 2)
        d_cd = jnp.sum(s * dsn, axis=(1, 2), keepdims=True)
        ds_scr[...] = _bd_lo(qd, do, 1, 1) + cd * dsn - _bd_lo(kcd, d_vnew, 1, 1)
        d_kcd = -_bd_lo(d_vnew, s, 2, 2)
        d_vb = _bd_hi(tm, d_vnew, 1, 1)
        d_kbg = _bd_hi(tm, d_kcd, 1, 1)
        da = -jnp.where(strict, _bd_lo(d_vb, value, 2, 2) + _bd_lo(d_kbg, kcd, 2, 2), 0.0)
        dkk = da * decay
        dqk = d_attn * decay
        d_decay = da * L["kk"] + d_attn * L["qk"]
        d_kb = _bd_lo(dkk, kn, 2, 1) + d_kbg * eg
        d_kn = _bd_lo(dkk, kb, 1, 1) + _bd_lo(dqk, qn, 1, 1) + d_kt * egl
        d_qn = _bd_lo(dqk, kn, 2, 1) + d_qd * eg
        dmm = _bd_hi(lb, d_decay * decay, 1, 1, True)
        dg = jnp.sum(jnp.where(strict, dmm, 0.0), axis=-1, keepdims=True)
        d_gc = jnp.sum(d_qd * qd + d_kbg * L["kbg"] - d_kt * kt, axis=-1, keepdims=True)
        d_gl = jnp.sum(d_kt * kt, axis=(1, 2), keepdims=True) + d_cd * cd[:, :, 0:1]
        last = lax.broadcasted_iota(jnp.int32, (1, CH, 1), 1) == CH - 1
        d_gc = d_gc + jnp.where(last, d_gl, 0.0)
        dg = dg + _bd_hi(lb, jnp.broadcast_to(d_gc, (HEADS, CH, HD)), 1, 1, True)[:, :, 0:1]
        d_apre = -dg * L["ea"] * _sig(L["x"])
        da_ref[0] = _col_to_row(d_apre)
        dal_ref[...] += jnp.sum(dg * L["g"], axis=1, keepdims=True)
        ddt_ref[...] += jnp.sum(d_apre, axis=1, keepdims=True)
        d_beta = jnp.sum(d_kb * kn + d_vb * vv, axis=-1, keepdims=True)
        db_ref[0] = _col_to_row(d_beta * beta * (1.0 - beta))
        d_kn = d_kn + d_kb * beta
        dv_ref[...] = d_vb * beta
        qh = L["qh"]
        dq_ref[...] = (HD ** -0.5) * L["rq"] * (d_qn - qh * jnp.sum(qh * d_qn, axis=-1, keepdims=True))
        dk_ref[...] = L["rk"] * (d_kn - kn * jnp.sum(kn * d_kn, axis=-1, keepdims=True))

    m3 = jax.ShapeDtypeStruct((HEADS, T, HD), F32)
    c3 = jax.ShapeDtypeStruct((nc, HEADS, 1, CH), F32)
    s3 = jax.ShapeDtypeStruct((HEADS, 1, 1), F32)
    return pl.pallas_call(
        body, name="gdn_bwd", grid=(nc,),
        in_specs=[mat, mat, mat, mat, col, col, sca, sca, nws, st, mat, mat],
        out_specs=[mat, mat, mat, mat, col, col, sca, sca, nws],
        out_shape=[m3, m3, m3, m3, c3, c3, s3, s3, jax.ShapeDtypeStruct((1, 1, HD), F32)],
        scratch_shapes=[pltpu.VMEM((HEADS, HD, HD), F32)], compiler_params=_cp("arbitrary"))(
            q, k, v, z, bpre, apre, alog, dtb, nw, states, o, dog)


def _heads_major(a):
    T = a.shape[0]
    return a.reshape(T, -1, HD).transpose(1, 0, 2)


def _tokens_major(a):
    return a.transpose(1, 0, 2).reshape(a.shape[1], -1)


def _local_step(x, target, w):
    seq = x.shape[0]
    T = -(-(X0 + seq) // T_ALIGN) * T_ALIGN
    back = T - X0 - seq
    tb_att = _div(T, ATT_TILE)
    h0 = jnp.concatenate([jnp.zeros((FRONT, D), F32), w["meta"], x, jnp.zeros((back, D), F32)], axis=0)
    tgt = jnp.concatenate([jnp.zeros((X0, D), F32), target, jnp.zeros((back, D), F32)], axis=0)
    row = lambda v: v.reshape(1, -1)
    nmix, nffn, nfin = row(w["norm_mix"]), row(w["norm_ffn"]), row(w["norm_final"])
    gate_b = row(w["gate_bias"])
    fconv_b = row(w["ffn_conv_b"])
    bias128 = jnp.zeros((1, SMALL_W), F32).at[0, :HEADS].set(w["fgt_bias"])
    alog = w["a_log"].reshape(HEADS, 1, 1)
    dtb = w["dt_bias"].reshape(HEADS, 1, 1)
    gnw = w["gdn_norm"].reshape(1, 1, HD)

    a = _rms_fwd(h0, nmix, "rms_mix")
    pf = _mm(a, w["w_fox"], out_dtype=_BF, name="proj_fox")
    rest = _mm(a, w["w_rest"], name="proj_rest")
    small = rest[:, OFF_SMALL:]
    c8 = _fgate_fwd(small, bias128)[:, :HEADS]
    c_cols, c_rows = _att_cols(c8), _att_rows(c8, tb_att)
    ones_blk = jnp.asarray(np.kron(np.eye(HEADS, dtype=np.float32), np.ones((HD, HD), np.float32)))
    plan = _att_plan(_att_stats(pf, c_cols, ones_blk))
    o_fox, lse = _fox_fwd(pf, c_cols, c_rows, plan)
    conv = _gconv_fwd(rest, w["gdn_conv"])
    qkvh = _heads_major(conv)
    qh, kh, vh = qkvh[:HEADS], qkvh[HEADS:2 * HEADS], qkvh[2 * HEADS:]
    zh = _heads_major(rest[:, OFF_Z:OFF_Z + FW])
    chunk_rows = lambda a8: a8.reshape(T // CH, CH, HEADS).transpose(0, 2, 1)[:, :, None, :]
    bpre = chunk_rows(small[:, HEADS:2 * HEADS])
    apre = chunk_rows(small[:, 2 * HEADS:3 * HEADS])
    o_raw, ogh, states = _gdn_fwd(qh, kh, vh, zh, bpre, apre, alog, dtb, gnw)
    og = _tokens_major(ogh)
    y_fox = _mm(o_fox, w["w_bfox"], name="y_fox")
    y_gdn = _mm(og, w["w_bgdn"], name="y_gdn")
    mix = _mix_fwd(rest, gate_b, y_fox, y_gdn)
    h1 = _mm(mix, w["w_out"], resid=h0, name="out_proj")
    b = _rms_fwd(h1, nffn, "rms_ffn")
    up = _mm(b, w["w_up"], name="ffn_up")
    f = _glu_fwd(up, w["ffn_conv"], fconv_b)
    h2 = _mm(f, w["w_down"], resid=h1, name="ffn_down")
    dh2, loss, d_nfin = _loss_head(h2, nfin, tgt, seq)

    d_f = _mm(dh2, w["w_down"], tb=True, name="d_f")
    g_down = _mm_t(f, dh2, "g_down")
    dug, duu, dwg, dwu, dbg, dbu = _glu_bwd_du(up, w["ffn_conv"], fconv_b, d_f)
    dxg = _conv_bwd_dx(dug, w["ffn_conv"][:, :DFF], _BF, "fconv_dx_gate")
    dxu = _conv_bwd_dx(duu, w["ffn_conv"][:, DFF:], _BF, "fconv_dx_up")
    d_b = _mm(dxg, w["w_up"][:, :DFF], tb=True, name="d_b_gate")
    d_b = _mm(dxu, w["w_up"][:, DFF:], tb=True, resid=d_b, name="d_b_up")
    g_up = jnp.concatenate([_mm_t(b, dxg, "g_up_gate"), _mm_t(b, dxu, "g_up_up")], axis=1)
    dh1, d_nffn = _rms_bwd(h1, nffn, d_b, dh2, "rms_ffn_bwd")

    dmix = _mm(dh1, w["w_out"], tb=True, name="d_mix")
    g_out = _mm_t(mix, dh1, "g_out")
    dyf, dyg, dgf, dgg, dgbf, dgbg = _mix_bwd(rest, gate_b, y_fox, y_gdn, dmix)
    do_fox = _mm(dyf, w["w_bfox"], tb=True, out_dtype=_BF, name="d_o_fox")
    g_bfox = _mm_t(o_fox, dyf, "g_bfox")
    d_og = _mm(dyg, w["w_bgdn"], tb=True, name="d_o_gdn")
    g_bgdn = _mm_t(og, dyg, "g_bgdn")

    delta = _head_dot(do_fox, o_fox, ones_blk)
    dq, dcq = _fox_bwd_dq(pf, do_fox, c_cols, c_rows, lse, delta, plan)
    dk, dv, dck = _fox_bwd_dkv(pf, do_fox, c_cols, c_rows, _att_rows(lse[:, ::HD], tb_att), _att_rows(delta[:, ::HD], tb_att),
                               plan)
    dc = jnp.pad((dcq + dck)[:, ::HD], ((0, 0), (0, SMALL_W - HEADS)))
    dfp, d_fb = _fgate_bwd(small, bias128, dc)

    dqh, dkh, dvh, dzh, dbp, dap, d_alog, d_dtb, d_gnw = _gdn_bwd(
        qh, kh, vh, zh, bpre, apre, alog, dtb, gnw, states, o_raw, _heads_major(d_og))
    dconv = _tokens_major(jnp.concatenate([dqh, dkh, dvh], axis=0))
    du_g, g_gconv = _gconv_bwd_du(rest, w["gdn_conv"], dconv)
    dgx = _conv_bwd_dx(du_g, w["gdn_conv"], F32, "gconv_dx")
    token_rows = lambda a: a[:, :, 0, :].transpose(0, 2, 1).reshape(T, HEADS)
    dsmall = jnp.concatenate([dfp[:, :HEADS], token_rows(dbp), token_rows(dap),
                              jnp.zeros((T, SMALL_W - 3 * HEADS), F32)], axis=1)
    drest = jnp.concatenate([dgx, _tokens_major(dzh), dgf, dgg, dsmall], axis=1)
    dfox = jnp.concatenate([dq, dk, dv], axis=1)
    d_a = _mm(dfox, w["w_fox"], tb=True, name="d_a_fox")
    d_a = _mm(drest, w["w_rest"], tb=True, resid=d_a, name="d_a_rest")
    g_fox = _mm_t(a, dfox, "g_w_fox")
    g_rest = _mm_t(a, drest, "g_w_rest")
    dh0, d_nmix = _rms_bwd(h0, nmix, d_a, dh1, "rms_mix_bwd")

    sm = lambda lo: g_rest[:, OFF_SMALL + lo:OFF_SMALL + lo + HEADS]
    g_w_in = jnp.concatenate([g_fox, sm(0), g_rest[:, :3 * FW], g_rest[:, OFF_Z:OFF_Z + FW], sm(HEADS), sm(2 * HEADS),
                              g_rest[:, OFF_GATES:OFF_GATES + 2 * D]], axis=1)
    grads = dict(
        meta_tokens=dh0[FRONT:X0], w_in=g_w_in, fgt_bias=d_fb[0, :HEADS], gdn_conv_w=g_gconv,
        gdn_a_log=d_alog.reshape(HEADS), gdn_dt_bias=d_dtb.reshape(HEADS), gdn_norm_w=d_gnw.reshape(HD),
        gate_bias=jnp.concatenate([dgbf, dgbg], axis=1).reshape(2 * D), w_branch_fox=g_bfox, w_branch_gdn=g_bgdn,
        w_out=g_out, norm_mix_w=d_nmix.reshape(D), norm_ffn_w=d_nffn.reshape(D), ffn_w_up=g_up,
        ffn_conv_w=jnp.concatenate([dwg, dwu], axis=1), ffn_conv_b=jnp.concatenate([dbg, dbu], axis=1).reshape(2 * DFF),
        ffn_w_down=g_down, norm_final_w=d_nfin.reshape(D))
    return loss, dh0[X0:X0 + seq], grads


N_CHIPS = 4
PACK_W = 1024
PACK_ROW_ALIGN = 32
BIG = ("w_in", "w_branch_fox", "w_branch_gdn", "w_out", "ffn_w_up", "ffn_w_down")
WEIGHTS = (
    ("meta_tokens", (N_META, D), 1), ("w_in", (D, 3 * FW + HEADS + 4 * FW + 2 * HEADS + 2 * D), 1), ("fgt_bias", (1, HEADS), None),
    ("gdn_conv_w", (4, 3 * FW), 1), ("gdn_a_log", (1, HEADS), None), ("gdn_dt_bias", (1, HEADS), None),
    ("gdn_norm_w", (1, HD), None), ("gate_bias", (1, 2 * D), None), ("w_branch_fox", (FW, D), 1),
    ("w_branch_gdn", (FW, D), 1), ("w_out", (D, D), 0), ("norm_mix_w", (1, D), None), ("norm_ffn_w", (1, D), None),
    ("ffn_w_up", (D, 2 * DFF), 1), ("ffn_conv_w", (3, 2 * DFF), 1), ("ffn_conv_b", (1, 2 * DFF), None),
    ("ffn_w_down", (DFF, D), 0), ("norm_final_w", (1, D), None))
SPLIT_F32 = ("meta_tokens", "gdn_conv_w", "ffn_conv_w")


def _shard_shape(shape, axis):
    if axis is None:
        return shape
    return tuple(s // N_CHIPS if a == axis else s for a, s in enumerate(shape))


def _shard_of(full, axis, q):
    if axis is None:
        return full
    n = full.shape[axis] // N_CHIPS
    return lax.slice_in_dim(full, q * n, (q + 1) * n, axis=axis)


def _pack_rows(n_elems):
    rows = -(-n_elems // PACK_W)
    return -(-rows // PACK_ROW_ALIGN) * PACK_ROW_ALIGN


def _pack(pieces, dtype):
    flat = jnp.concatenate([p.reshape(-1).astype(dtype) for p in pieces])
    rows = _pack_rows(flat.shape[0])
    return jnp.pad(flat, (0, rows * PACK_W - flat.shape[0])).reshape(rows, PACK_W)


def _unpack(slab, shapes):
    flat = slab.reshape(-1)
    out, off = [], 0
    for s in shapes:
        n = int(np.prod(s))
        out.append(flat[off:off + n].reshape(s))
        off += n
    return out


HBM_SPEC = pl.BlockSpec(memory_space=pltpu.HBM)
MESH_ID = pl.DeviceIdType.MESH


def _chip_exchange(srcs, name, scatter):
    n = len(srcs)

    def body(*refs):
        src_refs, out_refs = refs[:n], refs[n:2 * n]
        send_sems, recv_sems, local_sems = refs[2 * n:]
        x, y, c = lax.axis_index("x"), lax.axis_index("y"), lax.axis_index("c")
        q = 2 * x + y
        peers = [(1 - x, y), (x, 1 - y), (1 - x, 1 - y)]

        def remote(a, k, src_slot, dst_slot):
            px, py = peers[k]
            src = src_refs[a].at[src_slot] if scatter else src_refs[a]
            return pltpu.make_async_remote_copy(
                src_ref=src, dst_ref=out_refs[a].at[dst_slot], send_sem=send_sems.at[3 * a + k],
                recv_sem=recv_sems.at[3 * a + k], device_id=(px, py, c), device_id_type=MESH_ID)

        mine = [pltpu.make_async_copy(src_refs[a].at[q] if scatter else src_refs[a], out_refs[a].at[q], local_sems.at[a])
                for a in range(n)]
        for cp in mine:
            cp.start()
        sends = [remote(a, k, 2 * px + py, q) for a in range(n) for k, (px, py) in enumerate(peers)]
        for cp in sends:
            cp.start()
        for a in range(n):
            for k, (px, py) in enumerate(peers):
                remote(a, k, 0, 2 * px + py).wait_recv()
        for cp in sends:
            cp.wait_send()
        for cp in mine:
            cp.wait()

    out_shape = [jax.ShapeDtypeStruct((N_CHIPS,) + s.shape[-2:], s.dtype) for s in srcs]
    return pl.pallas_call(
        body, name=name, in_specs=[HBM_SPEC] * n, out_specs=[HBM_SPEC] * n, out_shape=out_shape,
        scratch_shapes=[pltpu.SemaphoreType.DMA((3 * n,)), pltpu.SemaphoreType.DMA((3 * n,)), pltpu.SemaphoreType.DMA((n,))],
    )(*srcs)


def _sibling_swap(slabs, name):
    n = len(slabs)

    def body(*refs):
        src_refs, out_refs, send_sems, recv_sems = refs[:n], refs[n:2 * n], refs[2 * n], refs[2 * n + 1]
        x, y, c = lax.axis_index("x"), lax.axis_index("y"), lax.axis_index("c")
        cps = [pltpu.make_async_remote_copy(src_ref=src_refs[a], dst_ref=out_refs[a], send_sem=send_sems.at[a],
                                            recv_sem=recv_sems.at[a], device_id=(x, y, 1 - c), device_id_type=MESH_ID)
               for a in range(n)]
        for cp in cps:
            cp.start()
        for cp in cps:
            cp.wait_recv()
        for cp in cps:
            cp.wait_send()

    return pl.pallas_call(
        body, name=name, in_specs=[HBM_SPEC] * n, out_specs=[HBM_SPEC] * n,
        out_shape=[jax.ShapeDtypeStruct(s.shape, s.dtype) for s in slabs],
        scratch_shapes=[pltpu.SemaphoreType.DMA((n,)), pltpu.SemaphoreType.DMA((n,))])(*slabs)


def _sum_chips(r, name):
    rows, cols = r.shape[1:]
    tb = _div(rows, 256, 16)

    def body(r0, r1, r2, r3, o_ref):
        o_ref[...] = ((r0[0].astype(F32) + r1[0].astype(F32)) + r2[0].astype(F32)) + r3[0].astype(F32)

    spec = lambda j: pl.BlockSpec((1, tb, cols), lambda i: (j, i, 0))
    return pl.pallas_call(
        body, name=name, grid=(rows // tb,), in_specs=[spec(0), spec(1), spec(2), spec(3)],
        out_specs=pl.BlockSpec((tb, cols), lambda i: (i, 0)), out_shape=jax.ShapeDtypeStruct((rows, cols), F32),
        compiler_params=_cp("parallel"))(r, r, r, r)


def _adamw(w, m, v, p, q, name):
    rows, cols = w.shape
    tb = _div(rows, 256, 8)

    def body(w_ref, m_ref, v_ref, p_ref, q_ref, g_ref, d_ref, nm_ref, nv_ref):
        g = p_ref[...] + q_ref[...]
        m_new = B1 * m_ref[...] + (1.0 - B1) * g
        v_new = B2 * v_ref[...] + (1.0 - B2) * (g * g)
        g_ref[...] = g
        nm_ref[...] = m_new
        nv_ref[...] = v_new
        m_hat = m_new / (1.0 - B1 ** STEP)
        v_hat = v_new / (1.0 - B2 ** STEP)
        d_ref[...] = -LR * (m_hat / (jnp.sqrt(v_hat) + AEPS) + WD * w_ref[...])

    spec = pl.BlockSpec((tb, cols), lambda i: (i, 0))
    return pl.pallas_call(
        body, name=name, grid=(rows // tb,), in_specs=[spec] * 5, out_specs=[spec] * 4,
        out_shape=[jax.ShapeDtypeStruct((rows, cols), F32)] * 4, compiler_params=_cp("parallel"))(w, m, v, p, q)


def _split_w_in(w_in):
    o1 = 3 * FW
    o2 = o1 + HEADS
    o3 = o2 + 3 * FW
    o4 = o3 + FW
    o5 = o4 + HEADS
    o6 = o5 + HEADS
    pad = jnp.zeros((w_in.shape[0], SMALL_W - 3 * HEADS), w_in.dtype)
    rest = jnp.concatenate([w_in[:, o2:o3], w_in[:, o3:o4], w_in[:, o6:], w_in[:, o1:o2], w_in[:, o4:o5], w_in[:, o5:o6], pad],
                           axis=1)
    return w_in[:, :o1], rest


AXIS = {n: a for n, _, a in WEIGHTS}
SMALL = tuple(n for n, _, _ in WEIGHTS if n not in BIG)


def _by_chip(full, axis):
    rows, cols = full.shape
    if axis == 0:
        return full.reshape(N_CHIPS, rows // N_CHIPS, cols)
    return full.reshape(rows, N_CHIPS, cols // N_CHIPS).transpose(1, 0, 2)


def _from_chips(parts, axis):
    _, r, c = parts.shape
    if axis == 0:
        return parts.reshape(N_CHIPS * r, c)
    return parts.transpose(1, 0, 2).reshape(r, N_CHIPS * c)


def _gather_weights(shards):
    hi = {n: shards[n].astype(jnp.bfloat16) for n in SPLIT_F32}
    lo = [(shards[n] - hi[n].astype(F32)).astype(jnp.bfloat16) for n in SPLIT_F32]
    slab = _pack([hi[n] for n in SPLIT_F32] + lo, jnp.bfloat16)
    got = _chip_exchange([shards[n].astype(jnp.bfloat16) for n in BIG] + [slab], "gather_weights", scatter=False)
    full = {n: _from_chips(g, AXIS[n]) for n, g in zip(BIG, got)}
    shapes = [shards[n].shape for n in SPLIT_F32] * 2
    per_chip = [_unpack(got[-1][j], shapes) for j in range(N_CHIPS)]
    for i, n in enumerate(SPLIT_F32):
        join = lambda off: jnp.concatenate([per_chip[j][off + i] for j in range(N_CHIPS)], axis=1).astype(F32)
        full[n] = join(0) + join(len(SPLIT_F32))
    return full


def kernel(x, meta_tokens, w_in, fgt_bias, gdn_conv_w, gdn_a_log, gdn_dt_bias, gdn_norm_w, gate_bias, w_branch_fox, w_branch_gdn, w_out, norm_mix_w, norm_ffn_w, ffn_w_up, ffn_conv_w, ffn_conv_b, ffn_w_down, norm_final_w, loss_target, m_meta_tokens, m_w_in, m_fgt_bias, m_gdn_conv_w, m_gdn_a_log, m_gdn_dt_bias, m_gdn_norm_w, m_gate_bias, m_w_branch_fox, m_w_branch_gdn, m_w_out, m_norm_mix_w, m_norm_ffn_w, m_ffn_w_up, m_ffn_conv_w, m_ffn_conv_b, m_ffn_w_down, m_norm_final_w, v_meta_tokens, v_w_in, v_fgt_bias, v_gdn_conv_w, v_gdn_a_log, v_gdn_dt_bias, v_gdn_norm_w, v_gate_bias, v_w_branch_fox, v_w_branch_gdn, v_w_out, v_norm_mix_w, v_norm_ffn_w, v_ffn_w_up, v_ffn_conv_w, v_ffn_conv_b, v_ffn_w_down, v_norm_final_w):
    weights = dict(meta_tokens=meta_tokens, w_in=w_in, fgt_bias=fgt_bias, gdn_conv_w=gdn_conv_w, gdn_a_log=gdn_a_log, gdn_dt_bias=gdn_dt_bias, gdn_norm_w=gdn_norm_w, gate_bias=gate_bias, w_branch_fox=w_branch_fox, w_branch_gdn=w_branch_gdn, w_out=w_out, norm_mix_w=norm_mix_w, norm_ffn_w=norm_ffn_w, ffn_w_up=ffn_w_up, ffn_conv_w=ffn_conv_w, ffn_conv_b=ffn_conv_b, ffn_w_down=ffn_w_down, norm_final_w=norm_final_w)
    m_in = dict(meta_tokens=m_meta_tokens, w_in=m_w_in, fgt_bias=m_fgt_bias, gdn_conv_w=m_gdn_conv_w, gdn_a_log=m_gdn_a_log, gdn_dt_bias=m_gdn_dt_bias, gdn_norm_w=m_gdn_norm_w, gate_bias=m_gate_bias, w_branch_fox=m_w_branch_fox, w_branch_gdn=m_w_branch_gdn, w_out=m_w_out, norm_mix_w=m_norm_mix_w, norm_ffn_w=m_norm_ffn_w, ffn_w_up=m_ffn_w_up, ffn_conv_w=m_ffn_conv_w, ffn_conv_b=m_ffn_conv_b, ffn_w_down=m_ffn_w_down, norm_final_w=m_norm_final_w)
    v_in = dict(meta_tokens=v_meta_tokens, w_in=v_w_in, fgt_bias=v_fgt_bias, gdn_conv_w=v_gdn_conv_w, gdn_a_log=v_gdn_a_log, gdn_dt_bias=v_gdn_dt_bias, gdn_norm_w=v_gdn_norm_w, gate_bias=v_gate_bias, w_branch_fox=v_w_branch_fox, w_branch_gdn=v_w_branch_gdn, w_out=v_w_out, norm_mix_w=v_norm_mix_w, norm_ffn_w=v_norm_ffn_w, ffn_w_up=v_ffn_w_up, ffn_conv_w=v_ffn_conv_w, ffn_conv_b=v_ffn_conv_b, ffn_w_down=v_ffn_w_down, norm_final_w=v_norm_final_w)
    shard2d = {n: _shard_shape(s, a) for n, s, a in WEIGHTS}
    as2d = lambda d: {n: d[n].reshape(shard2d[n]) for n, _, _ in WEIGHTS}
    w2, m2, v2 = as2d(weights), as2d(m_in), as2d(v_in)

    full = _gather_weights(w2)
    w_fox, w_rest = _split_w_in(full["w_in"])
    flat = lambda n: w2[n].reshape(-1)
    local_w = dict(
        meta=full["meta_tokens"], w_fox=w_fox, w_rest=w_rest, fgt_bias=flat("fgt_bias"), gdn_conv=full["gdn_conv_w"],
        a_log=flat("gdn_a_log"), dt_bias=flat("gdn_dt_bias"), gdn_norm=flat("gdn_norm_w"), gate_bias=flat("gate_bias"),
        w_bfox=full["w_branch_fox"], w_bgdn=full["w_branch_gdn"], w_out=full["w_out"], norm_mix=flat("norm_mix_w"),
        norm_ffn=flat("norm_ffn_w"), w_up=full["ffn_w_up"], ffn_conv=full["ffn_conv_w"], ffn_conv_b=flat("ffn_conv_b"),
        w_down=full["ffn_w_down"], norm_final=flat("norm_final_w"))

    loss, grad_x, grads = _local_step(x[0], loss_target[0], local_w)

    g2 = {n: grads[n].reshape(s) for n, s, _ in WEIGHTS}
    send = [_by_chip(g2[n].astype(jnp.bfloat16), AXIS[n]) for n in BIG]
    send.append(jnp.stack([_pack([_shard_of(g2[n], AXIS[n], j) for n in SMALL], F32) for j in range(N_CHIPS)]))
    parts = [_sum_chips(r, "sum_" + n) for r, n in zip(_chip_exchange(send, "scatter_grads", scatter=True), BIG + ("small",))]
    others = _sibling_swap(parts, "swap_grads")
    slab = lambda d: _pack([d[n] for n in SMALL], F32)
    state = [(w2[n], m2[n], v2[n]) for n in BIG] + [(slab(w2), slab(m2), slab(v2))]
    outs = [_adamw(w, m, v, p, q, "adamw_" + n) for (w, m, v), p, q, n in zip(state, parts, others, BIG + ("small",))]
    small = [_unpack(o, [weights[n].shape for n in SMALL]) for o in outs[-1]]
    result = []
    for kind in range(4):
        by_name = {n: outs[i][kind].reshape(weights[n].shape) for i, n in enumerate(BIG)}
        by_name.update(zip(SMALL, small[kind]))
        result += [by_name[n] for n, _, _ in WEIGHTS]
    total = lax.psum(loss[0, 0], ("x", "y", "c"))
    return (total, grad_x[None], *result)
```

```python
import functools

import numpy as np
import jax
import jax.numpy as jnp
from jax import lax
from jax.experimental import pallas as pl
from jax.experimental.pallas import tpu as pltpu

F32 = jnp.float32
_BF = jnp.bfloat16
HI = lax.Precision.HIGHEST

D = 1024
N_META = 16
CH = 64
FRONT = CH - N_META
X0 = CH
HEADS = 8
HD = 64
FW = HEADS * HD
DFF = 2816
EPS = 1e-6
NEG = -1e30
T_ALIGN = 256
SMALL_W = 128
REST_W = 3 * FW + FW + 2 * D + SMALL_W
OFF_Z = 3 * FW
OFF_GATES = 4 * FW
OFF_SMALL = 4 * FW + 2 * D

LR, B1, B2, AEPS, WD, STEP = 0.001, 0.9, 0.999, 1e-08, 0.01, 10

VMEM_LIMIT = 56 * 1024 * 1024
ROW_TILE = 640
MM_TM, MM_TN, MM_TK = 1280, 512, 2816
ATT_TILE = 640


def _div(n, target, mult=128):
    if n <= target:
        return n
    best = None
    for d in range(mult, target + 1, mult):
        if n % d == 0:
            best = d
    assert best is not None, (n, target, mult)
    return best


def _cp(*sem):
    return pltpu.CompilerParams(dimension_semantics=sem, vmem_limit_bytes=VMEM_LIMIT)


def _sig(x):
    return 1.0 / (1.0 + jnp.exp(-x))


MM_VMEM_BUDGET = 40 * 1024 * 1024
MM_STEP_BYTES = 1 << 20


def _mm_tiles(m, n, k, sa, sb, so, has_resid):
    divs = lambda d, cap: [t for t in range(128, min(d, cap) + 1, 128) if d % t == 0] or [d]
    best = None
    for tm in divs(m, MM_TM * 2):
        for tn in divs(n, 4608):
            fixed = tm * tn * (4 + 2 * so + (8 if has_resid else 0))
            tks = [t for t in divs(k, MM_TK) if 2 * t * (tm * sa + tn * sb) + fixed <= MM_VMEM_BUDGET]
            if not tks:
                continue
            tk = tks[-1]
            steps = (m // tm) * (n // tn) * (k // tk)
            rmw = (k // tk - 1) * m * n * 4
            cost = (n // tn) * m * k * sa + (m // tm) * k * n * sb + steps * MM_STEP_BYTES + rmw
            if best is None or cost < best[0]:
                best = (cost, tm, tn, tk)
    assert best is not None, (m, n, k)
    return best[1:]


def _mm_t(a, b, name):
    return _mm(a, b, ta=True, name=name)


def _mm(a, b, *, ta=False, tb=False, out_dtype=F32, resid=None, name):
    K, M = a.shape if ta else a.shape[::-1]
    N = b.shape[0] if tb else b.shape[1]
    assert (b.shape[1] if tb else b.shape[0]) == K
    tm, tn, tk = _mm_tiles(M, N, K, a.dtype.itemsize, b.dtype.itemsize, jnp.dtype(out_dtype).itemsize, resid is not None)
    nk = K // tk
    dims = (((0 if ta else 1,), (1 if tb else 0,)), ((), ()))
    mxu = _BF

    def body(*refs):
        if resid is None:
            a_ref, b_ref, o_ref, acc = refs
            r_ref = None
        else:
            a_ref, b_ref, r_ref, o_ref, acc = refs
        k = pl.program_id(2)
        part = lax.dot_general(a_ref[...].astype(mxu), b_ref[...].astype(mxu), dims, preferred_element_type=F32)

        @pl.when(k == 0)
        def _():
            acc[...] = part

        @pl.when(k > 0)
        def _():
            acc[...] += part

        @pl.when(k == nk - 1)
        def _():
            r = acc[...]
            if r_ref is not None:
                r = r + r_ref[...]
            o_ref[...] = r.astype(o_ref.dtype)

    a_spec = pl.BlockSpec((tk, tm), lambda i, j, k: (k, i)) if ta else pl.BlockSpec((tm, tk), lambda i, j, k: (i, k))
    b_spec = pl.BlockSpec((tn, tk), lambda i, j, k: (j, k)) if tb else pl.BlockSpec((tk, tn), lambda i, j, k: (k, j))
    o_spec = pl.BlockSpec((tm, tn), lambda i, j, k: (i, j))
    in_specs = [a_spec, b_spec] + ([o_spec] if resid is not None else [])
    args = (a, b) + ((resid,) if resid is not None else ())
    return pl.pallas_call(
        body, name=name, grid=(M // tm, N // tn, nk), in_specs=in_specs, out_specs=o_spec,
        out_shape=jax.ShapeDtypeStruct((M, N), out_dtype), scratch_shapes=[pltpu.VMEM((tm, tn), F32)],
        compiler_params=_cp("parallel", "parallel", "arbitrary"))(*args)


def _rms_fwd(h, w, name):
    T = h.shape[0]
    tb = _div(T, ROW_TILE, 8)

    def body(h_ref, w_ref, o_ref):
        x = h_ref[...]
        r = lax.rsqrt(jnp.mean(x * x, axis=-1, keepdims=True) + EPS)
        o_ref[...] = (x * r * w_ref[...]).astype(o_ref.dtype)

    return pl.pallas_call(
        body, name=name, grid=(T // tb,),
        in_specs=[pl.BlockSpec((tb, D), lambda i: (i, 0)), pl.BlockSpec((1, D), lambda i: (0, 0))],
        out_specs=pl.BlockSpec((tb, D), lambda i: (i, 0)), out_shape=jax.ShapeDtypeStruct((T, D), _BF),
        compiler_params=_cp("parallel"))(h, w)


def _rms_bwd(h, w, dy, dres, name):
    T = h.shape[0]
    tb = _div(T, ROW_TILE, 8)

    def body(h_ref, w_ref, dy_ref, dr_ref, dh_ref, dw_ref):
        x = h_ref[...]
        r = lax.rsqrt(jnp.mean(x * x, axis=-1, keepdims=True) + EPS)
        xh = x * r
        dy = dy_ref[...]
        g = dy * w_ref[...]
        dh_ref[...] = dr_ref[...] + r * (g - xh * jnp.mean(xh * g, axis=-1, keepdims=True))
        part = jnp.sum(dy * xh, axis=0, keepdims=True)

        @pl.when(pl.program_id(0) == 0)
        def _():
            dw_ref[...] = part

        @pl.when(pl.program_id(0) > 0)
        def _():
            dw_ref[...] += part

    row = pl.BlockSpec((tb, D), lambda i: (i, 0))
    one = pl.BlockSpec((1, D), lambda i: (0, 0))
    return pl.pallas_call(
        body, name=name, grid=(T // tb,), in_specs=[row, one, row, row], out_specs=[row, one],
        out_shape=[jax.ShapeDtypeStruct((T, D), F32), jax.ShapeDtypeStruct((1, D), F32)],
        compiler_params=_cp("arbitrary"))(h, w, dy, dres)


def _fgate_fwd(small, bias):
    T = small.shape[0]
    tb = _div(T, ROW_TILE, 8)

    def body(s_ref, b_ref, c_ref, carry):
        @pl.when(pl.program_id(0) == 0)
        def _():
            carry[...] = jnp.zeros_like(carry)

        lf = jax.nn.log_sigmoid(s_ref[...] + b_ref[...])
        r = lax.broadcasted_iota(jnp.int32, (tb, tb), 0)
        c = lax.broadcasted_iota(jnp.int32, (tb, tb), 1)
        tri = (r >= c).astype(F32)
        cs = jnp.dot(tri, lf, precision=HI, preferred_element_type=F32) + carry[...]
        c_ref[...] = cs
        carry[...] = cs[tb - 1:tb, :]

    return pl.pallas_call(
        body, name="fgate_fwd", grid=(T // tb,),
        in_specs=[pl.BlockSpec((tb, SMALL_W), lambda i: (i, 0)), pl.BlockSpec((1, SMALL_W), lambda i: (0, 0))],
        out_specs=pl.BlockSpec((tb, SMALL_W), lambda i: (i, 0)), out_shape=jax.ShapeDtypeStruct((T, SMALL_W), F32),
        scratch_shapes=[pltpu.VMEM((1, SMALL_W), F32)], compiler_params=_cp("arbitrary"))(small, bias)


def _fgate_bwd(small, bias, dc):
    T = small.shape[0]
    tb = _div(T, ROW_TILE, 8)
    nb = T // tb

    def body(s_ref, b_ref, dc_ref, df_ref, db_ref, carry):
        @pl.when(pl.program_id(0) == 0)
        def _():
            carry[...] = jnp.zeros_like(carry)

        r = lax.broadcasted_iota(jnp.int32, (tb, tb), 0)
        c = lax.broadcasted_iota(jnp.int32, (tb, tb), 1)
        tri = (r <= c).astype(F32)
        dlf = jnp.dot(tri, dc_ref[...], precision=HI, preferred_element_type=F32) + carry[...]
        carry[...] = dlf[0:1, :]
        df = dlf * _sig(-(s_ref[...] + b_ref[...]))
        df_ref[...] = df
        part = jnp.sum(df, axis=0, keepdims=True)

        @pl.when(pl.program_id(0) == 0)
        def _():
            db_ref[...] = part

        @pl.when(pl.program_id(0) > 0)
        def _():
            db_ref[...] += part

    rev = pl.BlockSpec((tb, SMALL_W), lambda i: (nb - 1 - i, 0))
    one = pl.BlockSpec((1, SMALL_W), lambda i: (0, 0))
    return pl.pallas_call(
        body, name="fgate_bwd", grid=(nb,), in_specs=[rev, one, rev], out_specs=[rev, one],
        out_shape=[jax.ShapeDtypeStruct((T, SMALL_W), F32), jax.ShapeDtypeStruct((1, SMALL_W), F32)],
        scratch_shapes=[pltpu.VMEM((1, SMALL_W), F32)], compiler_params=_cp("arbitrary"))(small, bias, dc)


def _att_rows(a8, tb):
    T = a8.shape[0]
    return a8.T.reshape(HEADS // 2, 2, T // tb, tb).transpose(0, 2, 1, 3)


def _att_cols(a8):
    return jnp.repeat(a8, HD, axis=1)


EXP_ZERO = -104.0
SKIP_SLACK = 2.0


def _att_stats(qkv, c_cols, ones_blk):
    T = qkv.shape[0]
    tb = _div(T, ATT_TILE)

    def body(q_ref, k_ref, c_ref, e_ref, o_ref):
        q = q_ref[...].astype(F32)
        k = k_ref[...].astype(F32)
        qn = jnp.max(jnp.dot(q * q, e_ref[...], precision=HI, preferred_element_type=F32), axis=0, keepdims=True)
        kn = jnp.max(jnp.dot(k * k, e_ref[...], precision=HI, preferred_element_type=F32), axis=0, keepdims=True)
        c = c_ref[...]
        o_ref[0] = jnp.concatenate([jnp.sqrt(qn), jnp.sqrt(kn), jnp.max(c, axis=0, keepdims=True),
                                    jnp.min(c, axis=0, keepdims=True), jnp.zeros((4, FW), F32)], axis=0)

    blk = lambda off: pl.BlockSpec((tb, FW), lambda i: (i, off))
    return pl.pallas_call(
        body, name="att_stats", grid=(T // tb,),
        in_specs=[blk(0), blk(1), blk(0), pl.BlockSpec((FW, FW), lambda i: (0, 0))],
        out_specs=pl.BlockSpec((1, 8, FW), lambda i: (i, 0, 0)), out_shape=jax.ShapeDtypeStruct((T // tb, 8, FW), F32),
        compiler_params=_cp("parallel"))(qkv, qkv, c_cols, ones_blk)


def _att_plan(stats):
    nb = stats.shape[0]
    st = stats[:, :4, ::HD]
    qmax, kmax, cmax, cmin = (st[:, r, :].T for r in range(4))
    bound = (HD ** -0.5) * 1.01 * qmax[:, :, None] * (kmax[:, None, :] + kmax[:, :, None]) \
        + cmax[:, :, None] - cmin[:, None, :] + SKIP_SLACK
    ii = lax.broadcasted_iota(jnp.int32, (nb, nb), 0)
    jj = lax.broadcasted_iota(jnp.int32, (nb, nb), 1)
    skip = (bound < EXP_ZERO) & (jj < ii)[None]
    live = (~skip & (jj <= ii)[None]).reshape(HEADS // 2, 2, nb, nb).any(axis=1)
    jfirst = jnp.argmax(live, axis=2).astype(jnp.int32)
    ilast = (nb - 1 - jnp.argmax(live[:, ::-1, :], axis=1)).astype(jnp.int32)
    return skip.astype(jnp.int32).reshape(-1), jfirst.reshape(-1), ilast.reshape(-1)


def _fox_fwd(qkv, c_cols, c_rows, plan):
    T = qkv.shape[0]
    tb = _div(T, ATT_TILE)
    nb = T // tb
    npair = HEADS // 2
    scale = HD ** -0.5
    dn = (((1,), (1,)), ((), ()))

    def body(skip_ref, jfirst_ref, ilast_ref, q_ref, k_ref, v_ref, cq_ref, ck_ref, o_ref, l_ref, m_scr, l_scr, acc_scr):
        pr = pl.program_id(0)
        i = pl.program_id(1)
        q = q_ref[...]
        lane = lax.broadcasted_iota(jnp.int32, (1, 2 * HD), 1)
        sel0 = lane < HD
        zero = jnp.zeros_like(q)
        qh = (jnp.where(sel0, q, zero) * scale, jnp.where(sel0, zero, q) * scale)
        cq = (cq_ref[:, 0:1], cq_ref[:, HD:HD + 1])
        row = i * tb + lax.broadcasted_iota(jnp.int32, (tb, 1), 0)
        m_scr[...] = jnp.full(m_scr.shape, NEG, F32)
        l_scr[...] = jnp.zeros(l_scr.shape, F32)
        acc_scr[...] = jnp.zeros(acc_scr.shape, F32)

        def step(j, _):
            start = pl.multiple_of(j * tb, tb)
            for h in range(2):
                @pl.when(skip_ref[((2 * pr + h) * nb + i) * nb + j] == 0)
                def _():
                    kb = k_ref[pl.ds(start, tb), :]
                    vb = v_ref[pl.ds(start, tb), :]
                    col = j * tb + lax.broadcasted_iota(jnp.int32, (1, tb), 1)
                    mask = (col <= row) & (col >= FRONT)
                    s = lax.dot_general(qh[h], kb, dn, preferred_element_type=F32) + cq[h] - ck_ref[0, j, h:h + 1, :]
                    s = jnp.where(mask, s, NEG)
                    m = m_scr[h]
                    m_new = jnp.maximum(m, jnp.max(s, axis=-1, keepdims=True))
                    p = jnp.where(mask, jnp.exp(s - m_new), 0.0)
                    alpha = jnp.exp(m - m_new)
                    m_scr[h] = m_new
                    l_scr[h] = l_scr[h] * alpha + jnp.sum(p, axis=-1, keepdims=True)
                    acc_scr[h] = acc_scr[h] * alpha + jnp.dot(p.astype(vb.dtype), vb, preferred_element_type=F32)
            return 0

        lax.fori_loop(jfirst_ref[pr * nb + i], i + 1, step, 0)
        outs, lses = [], []
        for h in range(2):
            l = l_scr[h]
            ok = l > 0.0
            ls = jnp.where(ok, l, 1.0)
            outs.append(jnp.where(ok, acc_scr[h] / ls, 0.0))
            lses.append(jnp.where(ok, m_scr[h] + jnp.log(ls), 0.0))
        o_ref[...] = jnp.where(sel0, outs[0], outs[1])
        l_ref[...] = jnp.where(sel0, lses[0], lses[1])

    blk = lambda off: pl.BlockSpec((tb, 2 * HD), lambda p, i, *_: (i, off + p))
    full = lambda off: pl.BlockSpec((T, 2 * HD), lambda p, i, *_: (0, off + p))
    rows = pl.BlockSpec((1, nb, 2, tb), lambda p, i, *_: (p, 0, 0, 0))
    return pl.pallas_call(
        body, name="fox_fwd",
        grid_spec=pltpu.PrefetchScalarGridSpec(
            num_scalar_prefetch=3, grid=(npair, nb),
            in_specs=[blk(0), full(npair), full(2 * npair), blk(0), rows], out_specs=[blk(0), blk(0)],
            scratch_shapes=[pltpu.VMEM((2, tb, 1), F32), pltpu.VMEM((2, tb, 1), F32), pltpu.VMEM((2, tb, 2 * HD), F32)]),
        out_shape=[jax.ShapeDtypeStruct((T, FW), F32), jax.ShapeDtypeStruct((T, FW), F32)],
        compiler_params=_cp("parallel", "arbitrary"))(*plan, qkv, qkv, qkv, c_cols, c_rows)


def _fox_bwd_dq(qkv, do, c_cols, c_rows, lse, delta, plan):
    T = qkv.shape[0]
    tb = _div(T, ATT_TILE)
    nb = T // tb
    npair = HEADS // 2
    scale = HD ** -0.5
    dn = (((1,), (1,)), ((), ()))

    def body(skip_ref, jfirst_ref, ilast_ref, q_ref, k_ref, v_ref, do_ref, cq_ref, ck_ref, l_ref, dl_ref, dq_ref, dc_ref,
             dq_scr, dc_scr):
        pr = pl.program_id(0)
        i = pl.program_id(1)
        q = q_ref[...]
        do = do_ref[...]
        lane = lax.broadcasted_iota(jnp.int32, (1, 2 * HD), 1)
        sel0 = lane < HD
        qh = (jnp.where(sel0, q, jnp.zeros_like(q)) * scale, jnp.where(sel0, jnp.zeros_like(q), q) * scale)
        doh = (jnp.where(sel0, do, jnp.zeros_like(do)), jnp.where(sel0, jnp.zeros_like(do), do))
        cq = (cq_ref[:, 0:1], cq_ref[:, HD:HD + 1])
        ls = (l_ref[:, 0:1], l_ref[:, HD:HD + 1])
        dl = (dl_ref[:, 0:1], dl_ref[:, HD:HD + 1])
        row = i * tb + lax.broadcasted_iota(jnp.int32, (tb, 1), 0)
        dq_scr[...] = jnp.zeros(dq_scr.shape, F32)
        dc_scr[...] = jnp.zeros(dc_scr.shape, F32)

        def step(j, _):
            start = pl.multiple_of(j * tb, tb)
            for h in range(2):
                @pl.when(skip_ref[((2 * pr + h) * nb + i) * nb + j] == 0)
                def _():
                    kb = k_ref[pl.ds(start, tb), :]
                    vb = v_ref[pl.ds(start, tb), :]
                    col = j * tb + lax.broadcasted_iota(jnp.int32, (1, tb), 1)
                    mask = (col <= row) & (col >= FRONT)
                    s = lax.dot_general(qh[h], kb, dn, preferred_element_type=F32) + cq[h] - ck_ref[0, j, h:h + 1, :]
                    p = jnp.where(mask, jnp.exp(jnp.where(mask, s, NEG) - ls[h]), 0.0)
                    dp = lax.dot_general(doh[h], vb, dn, preferred_element_type=F32)
                    ds = p * (dp - dl[h])
                    dq_scr[h] += jnp.dot(ds.astype(kb.dtype), kb, preferred_element_type=F32)
                    dc_scr[h] += jnp.sum(ds, axis=-1, keepdims=True)
            return 0

        lax.fori_loop(jfirst_ref[pr * nb + i], i + 1, step, 0)
        dq_ref[...] = jnp.where(sel0, dq_scr[0], dq_scr[1]) * scale
        dc_ref[...] = jnp.where(sel0, dc_scr[0], dc_scr[1])

    blk = lambda off: pl.BlockSpec((tb, 2 * HD), lambda p, i, *_: (i, off + p))
    full = lambda off: pl.BlockSpec((T, 2 * HD), lambda p, i, *_: (0, off + p))
    rows = pl.BlockSpec((1, nb, 2, tb), lambda p, i, *_: (p, 0, 0, 0))
    return pl.pallas_call(
        body, name="fox_bwd_dq",
        grid_spec=pltpu.PrefetchScalarGridSpec(
            num_scalar_prefetch=3, grid=(npair, nb),
            in_specs=[blk(0), full(npair), full(2 * npair), blk(0), blk(0), rows, blk(0), blk(0)], out_specs=[blk(0), blk(0)],
            scratch_shapes=[pltpu.VMEM((2, tb, 2 * HD), F32), pltpu.VMEM((2, tb, 1), F32)]),
        out_shape=[jax.ShapeDtypeStruct((T, FW), F32), jax.ShapeDtypeStruct((T, FW), F32)],
        compiler_params=_cp("parallel", "arbitrary"))(*plan, qkv, qkv, qkv, do, c_cols, c_rows, lse, delta)


def _fox_bwd_dkv(qkv, do, c_cols, c_rows, lse_rows, delta_rows, plan):
    T = qkv.shape[0]
    tb = _div(T, ATT_TILE)
    nb = T // tb
    npair = HEADS // 2
    scale = HD ** -0.5
    dn = (((1,), (1,)), ((), ()))

    def body(skip_ref, jfirst_ref, ilast_ref, q_ref, k_ref, v_ref, do_ref, ck_ref, cq_ref, l_ref, dl_ref,
             dk_ref, dv_ref, dc_ref, dk_scr, dv_scr, dc_scr):
        pr = pl.program_id(0)
        jb = pl.program_id(1)
        k = k_ref[...]
        v = v_ref[...]
        lane = lax.broadcasted_iota(jnp.int32, (1, 2 * HD), 1)
        sel0 = lane < HD
        kh = (jnp.where(sel0, k, jnp.zeros_like(k)) * scale, jnp.where(sel0, jnp.zeros_like(k), k) * scale)
        vh = (jnp.where(sel0, v, jnp.zeros_like(v)), jnp.where(sel0, jnp.zeros_like(v), v))
        ck = (ck_ref[:, 0:1], ck_ref[:, HD:HD + 1])
        kidx = jb * tb + lax.broadcasted_iota(jnp.int32, (tb, 1), 0)
        dk_scr[...] = jnp.zeros(dk_scr.shape, F32)
        dv_scr[...] = jnp.zeros(dv_scr.shape, F32)
        dc_scr[...] = jnp.zeros(dc_scr.shape, F32)

        def step(i, _):
            start = pl.multiple_of(i * tb, tb)
            for h in range(2):
                @pl.when(skip_ref[((2 * pr + h) * nb + i) * nb + jb] == 0)
                def _():
                    qb = q_ref[pl.ds(start, tb), :]
                    dob = do_ref[pl.ds(start, tb), :]
                    qidx = i * tb + lax.broadcasted_iota(jnp.int32, (1, tb), 1)
                    mask = (kidx <= qidx) & (kidx >= FRONT)
                    st = lax.dot_general(kh[h], qb, dn, preferred_element_type=F32) + cq_ref[0, i, h:h + 1, :] - ck[h]
                    pt = jnp.where(mask, jnp.exp(jnp.where(mask, st, NEG) - l_ref[0, i, h:h + 1, :]), 0.0)
                    dv_scr[h] += jnp.dot(pt.astype(dob.dtype), dob, preferred_element_type=F32)
                    dpt = lax.dot_general(vh[h], dob, dn, preferred_element_type=F32)
                    dst = pt * (dpt - dl_ref[0, i, h:h + 1, :])
                    dk_scr[h] += jnp.dot(dst.astype(qb.dtype), qb, preferred_element_type=F32)
                    dc_scr[h] -= jnp.sum(dst, axis=-1, keepdims=True)
            return 0

        lax.fori_loop(jb, ilast_ref[pr * nb + jb] + 1, step, 0)
        dk_ref[...] = jnp.where(sel0, dk_scr[0], dk_scr[1]) * scale
        dv_ref[...] = jnp.where(sel0, dv_scr[0], dv_scr[1])
        dc_ref[...] = jnp.where(sel0, dc_scr[0], dc_scr[1])

    blk = lambda off: pl.BlockSpec((tb, 2 * HD), lambda p, j, *_: (j, off + p))
    full = lambda off: pl.BlockSpec((T, 2 * HD), lambda p, j, *_: (0, off + p))
    rows = pl.BlockSpec((1, nb, 2, tb), lambda p, j, *_: (p, 0, 0, 0))
    return pl.pallas_call(
        body, name="fox_bwd_dkv",
        grid_spec=pltpu.PrefetchScalarGridSpec(
            num_scalar_prefetch=3, grid=(npair, nb),
            in_specs=[full(0), blk(npair), blk(2 * npair), full(0), blk(0), rows, rows, rows],
            out_specs=[blk(0), blk(0), blk(0)],
            scratch_shapes=[pltpu.VMEM((2, tb, 2 * HD), F32), pltpu.VMEM((2, tb, 2 * HD), F32), pltpu.VMEM((2, tb, 1), F32)]),
        out_shape=[jax.ShapeDtypeStruct((T, FW), F32)] * 3,
        compiler_params=_cp("parallel", "arbitrary"))(*plan, qkv, qkv, qkv, do, c_cols, c_rows, lse_rows, delta_rows)


def _head_dot(a, b, ones_blk):
    T = a.shape[0]
    tb = _div(T, ROW_TILE, 8)

    def body(a_ref, b_ref, e_ref, o_ref):
        prod = a_ref[...].astype(F32) * b_ref[...].astype(F32)
        o_ref[...] = jnp.dot(prod, e_ref[...], precision=HI, preferred_element_type=F32)

    row = pl.BlockSpec((tb, FW), lambda i: (i, 0))
    return pl.pallas_call(
        body, name="head_dot", grid=(T // tb,), in_specs=[row, row, pl.BlockSpec((FW, FW), lambda i: (0, 0))],
        out_specs=row, out_shape=jax.ShapeDtypeStruct((T, FW), F32), compiler_params=_cp("parallel"))(a, b, ones_blk)


CONV_RC = 32
CONV_TC = 256


def _stage_prev(scr, x_ref, h_ref, first, tb):
    scr[0:8, :] = jnp.where(first, 0.0, h_ref[...])
    scr[8:8 + tb, :] = x_ref[...]


def _windows_prev(scr, r, kw):
    x = scr[pl.ds(pl.multiple_of(r * CONV_RC, CONV_RC), CONV_RC + 8), :]
    return [x[8:] if k == kw - 1 else pltpu.roll(x, kw - 1 - k, 0)[8:] for k in range(kw)]


def _fold8(a):
    return a.reshape(CONV_RC // 8, 8, a.shape[-1]).sum(axis=0)


def _halo_prev(tb, tc, off=0):
    return pl.BlockSpec((8, tc), lambda j, i: (jnp.maximum(i * (tb // 8) - 1, 0), j + off))


def _gconv_fwd(rest, w):
    T = rest.shape[0]
    C = 3 * FW
    kw = w.shape[0]
    tb, tc = _div(T, 1280, CONV_RC), CONV_TC

    def body(x_ref, h_ref, w_ref, o_ref, scr):
        _stage_prev(scr, x_ref, h_ref, pl.program_id(1) == 0, tb)
        wv = w_ref[...]

        def chunk(r, _):
            win = _windows_prev(scr, r, kw)
            u = sum(wv[k:k + 1, :] * win[k] for k in range(kw))
            o_ref[pl.ds(pl.multiple_of(r * CONV_RC, CONV_RC), CONV_RC), :] = u * _sig(u)
            return 0

        lax.fori_loop(0, tb // CONV_RC, chunk, 0)

    return pl.pallas_call(
        body, name="gconv_fwd", grid=(C // tc, T // tb),
        in_specs=[pl.BlockSpec((tb, tc), lambda j, i: (i, j)), _halo_prev(tb, tc), pl.BlockSpec((kw, tc), lambda j, i: (0, j))],
        out_specs=pl.BlockSpec((tb, tc), lambda j, i: (i, j)), out_shape=jax.ShapeDtypeStruct((T, C), F32),
        scratch_shapes=[pltpu.VMEM((tb + 8, tc), F32)], compiler_params=_cp("parallel", "arbitrary"))(rest, rest, w)


def _gconv_bwd_du(rest, w, dy):
    T = rest.shape[0]
    C = 3 * FW
    kw = w.shape[0]
    tb, tc = _div(T, 1280, CONV_RC), CONV_TC

    def body(x_ref, h_ref, w_ref, dy_ref, du_ref, dw_ref, scr):
        i = pl.program_id(1)
        _stage_prev(scr, x_ref, h_ref, i == 0, tb)
        wv = w_ref[...]

        def chunk(r, acc):
            rows = pl.ds(pl.multiple_of(r * CONV_RC, CONV_RC), CONV_RC)
            win = _windows_prev(scr, r, kw)
            u = sum(wv[k:k + 1, :] * win[k] for k in range(kw))
            sg = _sig(u)
            du = dy_ref[rows, :] * sg * (1.0 + u * (1.0 - sg))
            du_ref[rows, :] = du
            return tuple(acc[k] + _fold8(du * win[k]) for k in range(kw))

        acc = lax.fori_loop(0, tb // CONV_RC, chunk, tuple(jnp.zeros((8, tc), F32) for _ in range(kw)))
        part = jnp.concatenate([jnp.sum(a, axis=0, keepdims=True) for a in acc], axis=0)

        @pl.when(i == 0)
        def _():
            dw_ref[...] = part

        @pl.when(i > 0)
        def _():
            dw_ref[...] += part

    blk = pl.BlockSpec((tb, tc), lambda j, i: (i, j))
    wsp = pl.BlockSpec((kw, tc), lambda j, i: (0, j))
    return pl.pallas_call(
        body, name="gconv_bwd_du", grid=(C // tc, T // tb), in_specs=[blk, _halo_prev(tb, tc), wsp, blk],
        out_specs=[blk, wsp], out_shape=[jax.ShapeDtypeStruct((T, C), F32), jax.ShapeDtypeStruct((kw, C), F32)],
        scratch_shapes=[pltpu.VMEM((tb + 8, tc), F32)], compiler_params=_cp("parallel", "arbitrary"))(rest, rest, w, dy)


def _conv_bwd_dx(du, w, out_dtype, name):
    T, C = du.shape
    kw = w.shape[0]
    tb = _div(T, 1280, CONV_RC)
    tc = CONV_TC
    nb = T // tb

    def body(x_ref, h_ref, w_ref, o_ref, scr):
        scr[0:tb, :] = x_ref[...]
        scr[tb:tb + 8, :] = jnp.where(pl.program_id(1) == nb - 1, 0.0, h_ref[...])
        wv = w_ref[...]

        def chunk(r, _):
            start = pl.multiple_of(r * CONV_RC, CONV_RC)
            x = scr[pl.ds(start, CONV_RC + 8), :]
            acc = wv[kw - 1:kw, :] * x[:CONV_RC]
            for k in range(kw - 1):
                acc = acc + wv[k:k + 1, :] * pltpu.roll(x, CONV_RC + 8 - (kw - 1 - k), 0)[:CONV_RC]
            o_ref[pl.ds(start, CONV_RC), :] = acc.astype(o_ref.dtype)
            return 0

        lax.fori_loop(0, tb // CONV_RC, chunk, 0)

    halo = pl.BlockSpec((8, tc), lambda j, i: (jnp.minimum((i + 1) * (tb // 8), T // 8 - 1), j))
    return pl.pallas_call(
        body, name=name, grid=(C // tc, nb),
        in_specs=[pl.BlockSpec((tb, tc), lambda j, i: (i, j)), halo, pl.BlockSpec((kw, tc), lambda j, i: (0, j))],
        out_specs=pl.BlockSpec((tb, tc), lambda j, i: (i, j)), out_shape=jax.ShapeDtypeStruct((T, C), out_dtype),
        scratch_shapes=[pltpu.VMEM((tb + 8, tc), F32)], compiler_params=_cp("parallel", "arbitrary"))(du, du, w)


def _glu_fwd(up, w, b):
    T = up.shape[0]
    kw = w.shape[0]
    tb, tc = _div(T, 1280, CONV_RC), CONV_TC
    nc = DFF // tc

    def body(xg, hg, xu, hu, wg, wu, bg, bu, o_ref, sg, su):
        first = pl.program_id(1) == 0
        _stage_prev(sg, xg, hg, first, tb)
        _stage_prev(su, xu, hu, first, tb)
        wgv, wuv, bgv, buv = wg[...], wu[...], bg[...], bu[...]

        def chunk(r, _):
            wing, winu = _windows_prev(sg, r, kw), _windows_prev(su, r, kw)
            ug = bgv + sum(wgv[k:k + 1, :] * wing[k] for k in range(kw))
            uu = buv + sum(wuv[k:k + 1, :] * winu[k] for k in range(kw))
            o_ref[pl.ds(pl.multiple_of(r * CONV_RC, CONV_RC), CONV_RC), :] = (ug * _sig(ug) * uu).astype(o_ref.dtype)
            return 0

        lax.fori_loop(0, tb // CONV_RC, chunk, 0)

    blk = lambda off: pl.BlockSpec((tb, tc), lambda j, i: (i, j + off))
    wsp = lambda off: pl.BlockSpec((kw, tc), lambda j, i: (0, j + off))
    bsp = lambda off: pl.BlockSpec((1, tc), lambda j, i: (0, j + off))
    return pl.pallas_call(
        body, name="glu_fwd", grid=(nc, T // tb),
        in_specs=[blk(0), _halo_prev(tb, tc), blk(nc), _halo_prev(tb, tc, nc), wsp(0), wsp(nc), bsp(0), bsp(nc)],
        out_specs=blk(0), out_shape=jax.ShapeDtypeStruct((T, DFF), _BF),
        scratch_shapes=[pltpu.VMEM((tb + 8, tc), F32)] * 2, compiler_params=_cp("parallel", "arbitrary"))(
            up, up, up, up, w, w, b, b)


def _glu_bwd_du(up, w, b, df):
    T = up.shape[0]
    kw = w.shape[0]
    tb, tc = _div(T, 1280, CONV_RC), CONV_TC
    nc = DFF // tc

    def body(xg, hg, xu, hu, wg, wu, bg, bu, df_ref, dug_ref, duu_ref, dwg_ref, dwu_ref, dbg_ref, dbu_ref, sg, su):
        i = pl.program_id(1)
        _stage_prev(sg, xg, hg, i == 0, tb)
        _stage_prev(su, xu, hu, i == 0, tb)
        wgv, wuv, bgv, buv = wg[...], wu[...], bg[...], bu[...]

        def chunk(r, acc):
            rows = pl.ds(pl.multiple_of(r * CONV_RC, CONV_RC), CONV_RC)
            wing, winu = _windows_prev(sg, r, kw), _windows_prev(su, r, kw)
            ug = bgv + sum(wgv[k:k + 1, :] * wing[k] for k in range(kw))
            uu = buv + sum(wuv[k:k + 1, :] * winu[k] for k in range(kw))
            s = _sig(ug)
            df = df_ref[rows, :]
            dug = df * uu * s * (1.0 + ug * (1.0 - s))
            duu = df * ug * s
            dug_ref[rows, :] = dug
            duu_ref[rows, :] = duu
            new = [dug * wing[k] for k in range(kw)] + [duu * winu[k] for k in range(kw)] + [dug, duu]
            return tuple(a + _fold8(v) for a, v in zip(acc, new))

        acc = lax.fori_loop(0, tb // CONV_RC, chunk, tuple(jnp.zeros((8, tc), F32) for _ in range(2 * kw + 2)))
        col = [jnp.sum(a, axis=0, keepdims=True) for a in acc]
        parts = (jnp.concatenate(col[:kw], axis=0), jnp.concatenate(col[kw:2 * kw], axis=0), col[2 * kw], col[2 * kw + 1])
        accs = (dwg_ref, dwu_ref, dbg_ref, dbu_ref)

        @pl.when(i == 0)
        def _():
            for r, p in zip(accs, parts):
                r[...] = p

        @pl.when(i > 0)
        def _():
            for r, p in zip(accs, parts):
                r[...] += p

    blk = lambda off: pl.BlockSpec((tb, tc), lambda j, i: (i, j + off))
    wsp = lambda off: pl.BlockSpec((kw, tc), lambda j, i: (0, j + off))
    bsp = lambda off: pl.BlockSpec((1, tc), lambda j, i: (0, j + off))
    return pl.pallas_call(
        body, name="glu_bwd_du", grid=(nc, T // tb),
        in_specs=[blk(0), _halo_prev(tb, tc), blk(nc), _halo_prev(tb, tc, nc), wsp(0), wsp(nc), bsp(0), bsp(nc), blk(0)],
        out_specs=[blk(0), blk(0), wsp(0), wsp(0), bsp(0), bsp(0)],
        out_shape=[jax.ShapeDtypeStruct((T, DFF), F32)] * 2 + [jax.ShapeDtypeStruct((kw, DFF), F32)] * 2
        + [jax.ShapeDtypeStruct((1, DFF), F32)] * 2,
        scratch_shapes=[pltpu.VMEM((tb + 8, tc), F32)] * 2, compiler_params=_cp("parallel", "arbitrary"))(
            up, up, up, up, w, w, b, b, df)


def _mix_fwd(rest, gate_bias, y_fox, y_gdn):
    T = rest.shape[0]
    tb, tc = _div(T, ROW_TILE, 8), 512
    nc = D // tc
    og = OFF_GATES // tc

    def body(gf, gg, bf, bg, yf, yg, o_ref):
        o_ref[...] = (_sig(gf[...] + bf[...]) * yf[...] + _sig(gg[...] + bg[...]) * yg[...]).astype(o_ref.dtype)

    blk = lambda off: pl.BlockSpec((tb, tc), lambda i, j: (i, j + off))
    bsp = lambda off: pl.BlockSpec((1, tc), lambda i, j: (0, j + off))
    return pl.pallas_call(
        body, name="mix_fwd", grid=(T // tb, nc),
        in_specs=[blk(og), blk(og + nc), bsp(0), bsp(nc), blk(0), blk(0)], out_specs=blk(0),
        out_shape=jax.ShapeDtypeStruct((T, D), _BF), compiler_params=_cp("parallel", "parallel"))(
            rest, rest, gate_bias, gate_bias, y_fox, y_gdn)


def _mix_bwd(rest, gate_bias, y_fox, y_gdn, dmix):
    T = rest.shape[0]
    tb, tc = _div(T, ROW_TILE, 8), 512
    nc = D // tc
    og = OFF_GATES // tc

    def body(gf, gg, bf, bg, yf, yg, dm, dyf, dyg, dgf, dgg, dbf, dbg):
        i = pl.program_id(1)
        d = dm[...]
        sf = _sig(gf[...] + bf[...])
        sg = _sig(gg[...] + bg[...])
        dyf[...] = (d * sf).astype(dyf.dtype)
        dyg[...] = (d * sg).astype(dyg.dtype)
        a = d * yf[...] * sf * (1.0 - sf)
        b = d * yg[...] * sg * (1.0 - sg)
        dgf[...] = a
        dgg[...] = b
        pa = jnp.sum(a, axis=0, keepdims=True)
        pb = jnp.sum(b, axis=0, keepdims=True)

        @pl.when(i == 0)
        def _():
            dbf[...] = pa
            dbg[...] = pb

        @pl.when(i > 0)
        def _():
            dbf[...] += pa
            dbg[...] += pb

    blk = lambda off: pl.BlockSpec((tb, tc), lambda j, i: (i, j + off))
    bsp = lambda off: pl.BlockSpec((1, tc), lambda j, i: (0, j + off))
    return pl.pallas_call(
        body, name="mix_bwd", grid=(nc, T // tb),
        in_specs=[blk(og), blk(og + nc), bsp(0), bsp(nc), blk(0), blk(0), blk(0)],
        out_specs=[blk(0), blk(0), blk(0), blk(0), bsp(0), bsp(0)],
        out_shape=[jax.ShapeDtypeStruct((T, D), _BF)] * 2 + [jax.ShapeDtypeStruct((T, D), F32)] * 2
        + [jax.ShapeDtypeStruct((1, D), F32)] * 2,
        compiler_params=_cp("parallel", "arbitrary"))(rest, rest, gate_bias, gate_bias, y_fox, y_gdn, dmix)


def _loss_head(h2, w, target, n_valid):
    T = h2.shape[0]
    tb = _div(T, ROW_TILE, 8)

    def body(h_ref, w_ref, t_ref, dh_ref, loss_ref, dw_ref):
        i = pl.program_id(0)
        x = h_ref[...]
        r = lax.rsqrt(jnp.mean(x * x, axis=-1, keepdims=True) + EPS)
        xh = x * r
        row = i * tb + lax.broadcasted_iota(jnp.int32, (tb, 1), 0)
        valid = (row >= X0) & (row < X0 + n_valid)
        e = jnp.where(valid, xh * w_ref[...] - t_ref[...], 0.0)
        dy = e * (1.0 / D)
        g = dy * w_ref[...]
        dh_ref[...] = r * (g - xh * jnp.mean(xh * g, axis=-1, keepdims=True))
        lpart = 0.5 * jnp.sum(jnp.sum(e * e, axis=-1, keepdims=True) * (1.0 / D), axis=0, keepdims=True)
        wpart = jnp.sum(dy * xh, axis=0, keepdims=True)

        @pl.when(i == 0)
        def _():
            loss_ref[...] = lpart
            dw_ref[...] = wpart

        @pl.when(i > 0)
        def _():
            loss_ref[...] += lpart
            dw_ref[...] += wpart

    row = pl.BlockSpec((tb, D), lambda i: (i, 0))
    one = pl.BlockSpec((1, D), lambda i: (0, 0))
    return pl.pallas_call(
        body, name="loss_head", grid=(T // tb,), in_specs=[row, one, row],
        out_specs=[row, pl.BlockSpec((1, 1), lambda i: (0, 0)), one],
        out_shape=[jax.ShapeDtypeStruct((T, D), F32), jax.ShapeDtypeStruct((1, 1), F32), jax.ShapeDtypeStruct((1, D), F32)],
        compiler_params=_cp("arbitrary"))(h2, w, target)


def _bd_lo(a, b, ca, cb):
    return lax.dot_general(a.astype(_BF), b.astype(_BF), (((ca,), (cb,)), ((0,), (0,))), preferred_element_type=F32)


def _split2(a):
    hi = a.astype(_BF)
    return hi, (a - hi.astype(F32)).astype(_BF)


def _bd_hi(a, b, ca, cb, exact_a=False):
    dn = (((ca,), (cb,)), ((0,), (0,)))
    dot = lambda x, y: lax.dot_general(x, y, dn, preferred_element_type=F32)
    bh, bl = _split2(b)
    if exact_a:
        ah = a.astype(_BF)
        return dot(ah, bh) + dot(ah, bl)
    ah, al = _split2(a)
    return dot(ah, bh) + (dot(ah, bl) + dot(al, bh))


def _row_to_col(x):
    eye = lax.broadcasted_iota(jnp.int32, (1, CH, CH), 1) == lax.broadcasted_iota(jnp.int32, (1, CH, CH), 2)
    return jnp.sum(jnp.where(eye, jnp.broadcast_to(x, (HEADS, CH, CH)), 0.0), axis=2, keepdims=True)


def _col_to_row(x):
    eye = lax.broadcasted_iota(jnp.int32, (1, CH, CH), 1) == lax.broadcasted_iota(jnp.int32, (1, CH, CH), 2)
    return jnp.sum(jnp.where(eye, jnp.broadcast_to(x, (HEADS, CH, CH)), 0.0), axis=1, keepdims=True)


def _gdn_chunk(q, k, v, bpre, apre, alog, dtb):
    H = HEADS
    r = lax.broadcasted_iota(jnp.int32, (1, CH, CH), 1)
    c = lax.broadcasted_iota(jnp.int32, (1, CH, CH), 2)
    tril, strict = r >= c, r > c
    lb = jnp.broadcast_to(tril.astype(F32), (H, CH, CH))
    rq = lax.rsqrt(jnp.sum(q * q, axis=-1, keepdims=True) + EPS)
    rk = lax.rsqrt(jnp.sum(k * k, axis=-1, keepdims=True) + EPS)
    qh = q * rq
    qn = qh * (HD ** -0.5)
    kn = k * rk
    beta = _sig(bpre)
    x = apre + dtb
    ea = jnp.exp(alog)
    g = -ea * jax.nn.softplus(x)
    gb = jnp.broadcast_to(g, (H, CH, HD))
    gc = _bd_hi(lb, gb, 2, 1, True)
    dm = _bd_hi(lb, jnp.where(strict, gb, 0.0), 2, 1, True)
    decay = jnp.where(tril, jnp.exp(jnp.where(tril, dm, 0.0)), 0.0)
    eg = jnp.exp(gc)
    gl = gc[:, CH - 1:CH, :]
    egl = jnp.exp(gl - gc)
    cd = jnp.exp(gl)
    kb = kn * beta
    vb = v * beta
    kk = _bd_lo(kb, kn, 2, 2)
    a = jnp.where(strict, kk * decay, 0.0)
    tm = jnp.where(r == c, 1.0, 0.0) - a
    p = a
    for it in range(5):
        mul = _bd_hi if it < 2 else _bd_lo
        p = mul(p, p, 2, 1)
        tm = tm + mul(tm, p, 2, 1)
    kbg = kb * eg
    value = _bd_hi(tm, vb, 2, 1)
    kcd = _bd_hi(tm, kbg, 2, 1)
    qk = _bd_lo(qn, kn, 2, 2)
    attn = jnp.where(tril, qk * decay, 0.0)
    return dict(tril=tril, strict=strict, lb=lb, rq=rq, rk=rk, qh=qh, qn=qn, kn=kn, beta=beta, x=x, ea=ea, g=g,
                decay=decay, eg=eg, egl=egl, cd=cd, kb=kb, vb=vb, kk=kk, tm=tm, kbg=kbg, value=value, kcd=kcd, qk=qk,
                attn=attn, qd=qn * eg, kt=kn * egl)


def _gdn_specs(T, rev):
    nc = T // CH
    pos = (lambda n: nc - 1 - n) if rev else (lambda n: n)
    mat = pl.BlockSpec((HEADS, CH, HD), lambda n: (0, pos(n), 0))
    col = pl.BlockSpec((1, HEADS, 1, CH), lambda n: (pos(n), 0, 0, 0))
    sca = pl.BlockSpec((HEADS, 1, 1), lambda n: (0, 0, 0))
    nw = pl.BlockSpec((1, 1, HD), lambda n: (0, 0, 0))
    st = pl.BlockSpec((1, HEADS, HD, HD), lambda n: (pos(n), 0, 0, 0))
    tok = lambda width, off: pl.BlockSpec((CH, width), lambda n: (pos(n), off))
    return nc, mat, col, sca, nw, st, tok


def _split_heads(x):
    return [jnp.stack([x[:, (g * HEADS + h) * HD:(g * HEADS + h + 1) * HD] for h in range(HEADS)])
            for g in range(x.shape[1] // FW)]


def _store_heads(ref, g, val):
    for h in range(HEADS):
        ref[:, (g * HEADS + h) * HD:(g * HEADS + h + 1) * HD] = val[h]


def _gdn_fwd(conv, rest, bpre, apre, alog, dtb, nw):
    T = conv.shape[0]
    nc, mat, col, sca, nws, st, tok = _gdn_specs(T, False)

    def body(c_ref, z_ref, b_ref, a_ref, al_ref, dt_ref, nw_ref, o_ref, og_ref, st_ref, s_scr):
        @pl.when(pl.program_id(0) == 0)
        def _():
            s_scr[...] = jnp.zeros_like(s_scr)

        q, k, v = _split_heads(c_ref[...])
        L = _gdn_chunk(q, k, v, _row_to_col(b_ref[0]), _row_to_col(a_ref[0]), al_ref[...], dt_ref[...])
        s = s_scr[...]
        st_ref[0] = s
        v_new = L["value"] - _bd_lo(L["kcd"], s, 2, 1)
        o = _bd_lo(L["qd"], s, 2, 1) + _bd_lo(L["attn"], v_new, 2, 1)
        s_scr[...] = s * L["cd"] + _bd_lo(L["kt"], v_new, 1, 1)
        o_ref[...] = o
        zz, = _split_heads(z_ref[...])
        rs = lax.rsqrt(jnp.mean(o * o, axis=-1, keepdims=True) + EPS)
        _store_heads(og_ref, 0, o * rs * nw_ref[...] * zz * _sig(zz))

    return pl.pallas_call(
        body, name="gdn_fwd", grid=(nc,), in_specs=[tok(3 * FW, 0), tok(FW, OFF_Z // FW), col, col, sca, sca, nws],
        out_specs=[mat, tok(FW, 0), st],
        out_shape=[jax.ShapeDtypeStruct((HEADS, T, HD), F32), jax.ShapeDtypeStruct((T, FW), F32),
                   jax.ShapeDtypeStruct((nc, HEADS, HD, HD), F32)],
        scratch_shapes=[pltpu.VMEM((HEADS, HD, HD), F32)], compiler_params=_cp("arbitrary"))(
            conv, rest, bpre, apre, alog, dtb, nw)


def _gdn_bwd(conv, rest, bpre, apre, alog, dtb, nw, states, o, dog):
    T = conv.shape[0]
    nc, mat, col, sca, nws, st, tok = _gdn_specs(T, True)

    def body(c_ref, z_ref, b_ref, a_ref, al_ref, dt_ref, nw_ref, st_ref, o_ref, dog_ref,
             dc_ref, dz_ref, db_ref, da_ref, dal_ref, ddt_ref, dnw_ref, ds_scr):
        @pl.when(pl.program_id(0) == 0)
        def _():
            ds_scr[...] = jnp.zeros_like(ds_scr)
            dal_ref[...] = jnp.zeros_like(dal_ref)
            ddt_ref[...] = jnp.zeros_like(ddt_ref)
            dnw_ref[...] = jnp.zeros_like(dnw_ref)

        qq, kk_, vv = _split_heads(c_ref[...])
        L = _gdn_chunk(qq, kk_, vv, _row_to_col(b_ref[0]), _row_to_col(a_ref[0]), al_ref[...], dt_ref[...])
        tril, strict, lb = L["tril"], L["strict"], L["lb"]
        qn, kn, kb, beta, decay, eg, egl, cd = L["qn"], L["kn"], L["kb"], L["beta"], L["decay"], L["eg"], L["egl"], L["cd"]
        value, kcd, attn, qd, kt, tm = L["value"], L["kcd"], L["attn"], L["qd"], L["kt"], L["tm"]
        s = st_ref[0]
        v_new = value - _bd_lo(kcd, s, 2, 1)
        oo = o_ref[...]
        zz, = _split_heads(z_ref[...])
        dog_, = _split_heads(dog_ref[...])
        sz = _sig(zz)
        rs = lax.rsqrt(jnp.mean(oo * oo, axis=-1, keepdims=True) + EPS)
        oh = oo * rs
        _store_heads(dz_ref, 0, dog_ * oh * nw_ref[...] * sz * (1.0 + zz * (1.0 - sz)))
        don = dog_ * zz * sz
        gdy = don * nw_ref[...]
        do = rs * (gdy - oh * jnp.mean(oh * gdy, axis=-1, keepdims=True))
        dnw_ref[...] += jnp.sum(don * oh, axis=(0, 1), keepdims=True)
        dsn = ds_scr[...]
        d_vnew = _bd_lo(attn, do, 1, 1) + _bd_lo(kt, dsn, 2, 1)
        d_qd = _bd_lo(do, s, 2, 2)
        d_attn = jnp.where(tril, _bd_lo(do, v_new, 2, 2), 0.0)
        d_kt = _bd_lo(v_new, dsn, 2, 2)
        d_cd = jnp.sum(s * dsn, axis=(1, 2), keepdims=True)
        ds_scr[...] = _bd_lo(qd, do, 1, 1) + cd * dsn - _bd_lo(kcd, d_vnew, 1, 1)
        d_kcd = -_bd_lo(d_vnew, s, 2, 2)
        d_vb = _bd_hi(tm, d_vnew, 1, 1)
        d_kbg = _bd_hi(tm, d_kcd, 1, 1)
        da = -jnp.where(strict, _bd_lo(d_vb, value, 2, 2) + _bd_lo(d_kbg, kcd, 2, 2), 0.0)
        dkk = da * decay
        dqk = d_attn * decay
        d_decay = da * L["kk"] + d_attn * L["qk"]
        d_kb = _bd_lo(dkk, kn, 2, 1) + d_kbg * eg
        d_kn = _bd_lo(dkk, kb, 1, 1) + _bd_lo(dqk, qn, 1, 1) + d_kt * egl
        d_qn = _bd_lo(dqk, kn, 2, 1) + d_qd * eg
        dmm = _bd_hi(lb, d_decay * decay, 1, 1, True)
        dg = jnp.sum(jnp.where(strict, dmm, 0.0), axis=-1, keepdims=True)
        d_gc = jnp.sum(d_qd * qd + d_kbg * L["kbg"] - d_kt * kt, axis=-1, keepdims=True)
        d_gl = jnp.sum(d_kt * kt, axis=(1, 2), keepdims=True) + d_cd * cd[:, :, 0:1]
        last = lax.broadcasted_iota(jnp.int32, (1, CH, 1), 1) == CH - 1
        d_gc = d_gc + jnp.where(last, d_gl, 0.0)
        dg = dg + _bd_hi(lb, jnp.broadcast_to(d_gc, (HEADS, CH, HD)), 1, 1, True)[:, :, 0:1]
        d_apre = -dg * L["ea"] * _sig(L["x"])
        da_ref[0] = _col_to_row(d_apre)
        dal_ref[...] += jnp.sum(dg * L["g"], axis=1, keepdims=True)
        ddt_ref[...] += jnp.sum(d_apre, axis=1, keepdims=True)
        d_beta = jnp.sum(d_kb * kn + d_vb * vv, axis=-1, keepdims=True)
        db_ref[0] = _col_to_row(d_beta * beta * (1.0 - beta))
        d_kn = d_kn + d_kb * beta
        qh = L["qh"]
        _store_heads(dc_ref, 0, (HD ** -0.5) * L["rq"] * (d_qn - qh * jnp.sum(qh * d_qn, axis=-1, keepdims=True)))
        _store_heads(dc_ref, 1, L["rk"] * (d_kn - kn * jnp.sum(kn * d_kn, axis=-1, keepdims=True)))
        _store_heads(dc_ref, 2, d_vb * beta)

    c3 = jax.ShapeDtypeStruct((nc, HEADS, 1, CH), F32)
    s3 = jax.ShapeDtypeStruct((HEADS, 1, 1), F32)
    return pl.pallas_call(
        body, name="gdn_bwd", grid=(nc,),
        in_specs=[tok(3 * FW, 0), tok(FW, OFF_Z // FW), col, col, sca, sca, nws, st, mat, tok(FW, 0)],
        out_specs=[tok(3 * FW, 0), tok(FW, 0), col, col, sca, sca, nws],
        out_shape=[jax.ShapeDtypeStruct((T, 3 * FW), F32), jax.ShapeDtypeStruct((T, FW), F32), c3, c3, s3, s3,
                   jax.ShapeDtypeStruct((1, 1, HD), F32)],
        scratch_shapes=[pltpu.VMEM((HEADS, HD, HD), F32)], compiler_params=_cp("arbitrary"))(
            conv, rest, bpre, apre, alog, dtb, nw, states, o, dog)


def _local_step(x, target, w):
    seq = x.shape[0]
    T = -(-(X0 + seq) // T_ALIGN) * T_ALIGN
    back = T - X0 - seq
    tb_att = _div(T, ATT_TILE)
    h0 = jnp.concatenate([jnp.zeros((FRONT, D), F32), w["meta"], x, jnp.zeros((back, D), F32)], axis=0)
    tgt = jnp.concatenate([jnp.zeros((X0, D), F32), target, jnp.zeros((back, D), F32)], axis=0)
    row = lambda v: v.reshape(1, -1)
    nmix, nffn, nfin = row(w["norm_mix"]), row(w["norm_ffn"]), row(w["norm_final"])
    gate_b = row(w["gate_bias"])
    fconv_b = row(w["ffn_conv_b"])
    bias128 = jnp.zeros((1, SMALL_W), F32).at[0, :HEADS].set(w["fgt_bias"])
    alog = w["a_log"].reshape(HEADS, 1, 1)
    dtb = w["dt_bias"].reshape(HEADS, 1, 1)
    gnw = w["gdn_norm"].reshape(1, 1, HD)

    a = _rms_fwd(h0, nmix, "rms_mix")
    pf = _mm(a, w["w_fox"], out_dtype=_BF, name="proj_fox")
    rest = _mm(a, w["w_rest"], name="proj_rest")
    small = rest[:, OFF_SMALL:]
    c8 = _fgate_fwd(small, bias128)[:, :HEADS]
    c_cols, c_rows = _att_cols(c8), _att_rows(c8, tb_att)
    ones_blk = jnp.asarray(np.kron(np.eye(HEADS, dtype=np.float32), np.ones((HD, HD), np.float32)))
    plan = _att_plan(_att_stats(pf, c_cols, ones_blk))
    o_fox, lse = _fox_fwd(pf, c_cols, c_rows, plan)
    conv = _gconv_fwd(rest, w["gdn_conv"])
    chunk_rows = lambda a8: a8.reshape(T // CH, CH, HEADS).transpose(0, 2, 1)[:, :, None, :]
    bpre = chunk_rows(small[:, HEADS:2 * HEADS])
    apre = chunk_rows(small[:, 2 * HEADS:3 * HEADS])
    o_raw, og, states = _gdn_fwd(conv, rest, bpre, apre, alog, dtb, gnw)
    y_fox = _mm(o_fox, w["w_bfox"], name="y_fox")
    y_gdn = _mm(og, w["w_bgdn"], name="y_gdn")
    mix = _mix_fwd(rest, gate_b, y_fox, y_gdn)
    h1 = _mm(mix, w["w_out"], resid=h0, name="out_proj")
    b = _rms_fwd(h1, nffn, "rms_ffn")
    up = _mm(b, w["w_up"], name="ffn_up")
    f = _glu_fwd(up, w["ffn_conv"], fconv_b)
    h2 = _mm(f, w["w_down"], resid=h1, name="ffn_down")
    dh2, loss, d_nfin = _loss_head(h2, nfin, tgt, seq)

    d_f = _mm(dh2, w["w_down"], tb=True, name="d_f")
    g_down = _mm_t(f, dh2, "g_down")
    dug, duu, dwg, dwu, dbg, dbu = _glu_bwd_du(up, w["ffn_conv"], fconv_b, d_f)
    dxg = _conv_bwd_dx(dug, w["ffn_conv"][:, :DFF], _BF, "fconv_dx_gate")
    dxu = _conv_bwd_dx(duu, w["ffn_conv"][:, DFF:], _BF, "fconv_dx_up")
    d_b = _mm(dxg, w["w_up"][:, :DFF], tb=True, name="d_b_gate")
    d_b = _mm(dxu, w["w_up"][:, DFF:], tb=True, resid=d_b, name="d_b_up")
    g_up = jnp.concatenate([_mm_t(b, dxg, "g_up_gate"), _mm_t(b, dxu, "g_up_up")], axis=1)
    dh1, d_nffn = _rms_bwd(h1, nffn, d_b, dh2, "rms_ffn_bwd")

    dmix = _mm(dh1, w["w_out"], tb=True, name="d_mix")
    g_out = _mm_t(mix, dh1, "g_out")
    dyf, dyg, dgf, dgg, dgbf, dgbg = _mix_bwd(rest, gate_b, y_fox, y_gdn, dmix)
    do_fox = _mm(dyf, w["w_bfox"], tb=True, out_dtype=_BF, name="d_o_fox")
    g_bfox = _mm_t(o_fox, dyf, "g_bfox")
    d_og = _mm(dyg, w["w_bgdn"], tb=True, name="d_o_gdn")
    g_bgdn = _mm_t(og, dyg, "g_bgdn")

    delta = _head_dot(do_fox, o_fox, ones_blk)
    dq, dcq = _fox_bwd_dq(pf, do_fox, c_cols, c_rows, lse, delta, plan)
    dk, dv, dck = _fox_bwd_dkv(pf, do_fox, c_cols, c_rows, _att_rows(lse[:, ::HD], tb_att), _att_rows(delta[:, ::HD], tb_att),
                               plan)
    dc = jnp.pad((dcq + dck)[:, ::HD], ((0, 0), (0, SMALL_W - HEADS)))
    dfp, d_fb = _fgate_bwd(small, bias128, dc)

    dconv, dz, dbp, dap, d_alog, d_dtb, d_gnw = _gdn_bwd(conv, rest, bpre, apre, alog, dtb, gnw, states, o_raw, d_og)
    du_g, g_gconv = _gconv_bwd_du(rest, w["gdn_conv"], dconv)
    dgx = _conv_bwd_dx(du_g, w["gdn_conv"], F32, "gconv_dx")
    token_rows = lambda a: a[:, :, 0, :].transpose(0, 2, 1).reshape(T, HEADS)
    dsmall = jnp.concatenate([dfp[:, :HEADS], token_rows(dbp), token_rows(dap),
                              jnp.zeros((T, SMALL_W - 3 * HEADS), F32)], axis=1)
    drest = jnp.concatenate([dgx, dz, dgf, dgg, dsmall], axis=1)
    dfox = jnp.concatenate([dq, dk, dv], axis=1)
    d_a = _mm(dfox, w["w_fox"], tb=True, name="d_a_fox")
    d_a = _mm(drest, w["w_rest"], tb=True, resid=d_a, name="d_a_rest")
    g_fox = _mm_t(a, dfox, "g_w_fox")
    g_rest = _mm_t(a, drest, "g_w_rest")
    dh0, d_nmix = _rms_bwd(h0, nmix, d_a, dh1, "rms_mix_bwd")

    sm = lambda lo: g_rest[:, OFF_SMALL + lo:OFF_SMALL + lo + HEADS]
    g_w_in = jnp.concatenate([g_fox, sm(0), g_rest[:, :3 * FW], g_rest[:, OFF_Z:OFF_Z + FW], sm(HEADS), sm(2 * HEADS),
                              g_rest[:, OFF_GATES:OFF_GATES + 2 * D]], axis=1)
    grads = dict(
        meta_tokens=dh0[FRONT:X0], w_in=g_w_in, fgt_bias=d_fb[0, :HEADS], gdn_conv_w=g_gconv,
        gdn_a_log=d_alog.reshape(HEADS), gdn_dt_bias=d_dtb.reshape(HEADS), gdn_norm_w=d_gnw.reshape(HD),
        gate_bias=jnp.concatenate([dgbf, dgbg], axis=1).reshape(2 * D), w_branch_fox=g_bfox, w_branch_gdn=g_bgdn,
        w_out=g_out, norm_mix_w=d_nmix.reshape(D), norm_ffn_w=d_nffn.reshape(D), ffn_w_up=g_up,
        ffn_conv_w=jnp.concatenate([dwg, dwu], axis=1), ffn_conv_b=jnp.concatenate([dbg, dbu], axis=1).reshape(2 * DFF),
        ffn_w_down=g_down, norm_final_w=d_nfin.reshape(D))
    return loss, dh0[X0:X0 + seq], grads


N_CHIPS = 4
PACK_W = 1024
PACK_ROW_ALIGN = 32
BIG = ("w_in", "w_branch_fox", "w_branch_gdn", "w_out", "ffn_w_up", "ffn_w_down")
WEIGHTS = (
    ("meta_tokens", (N_META, D), 1), ("w_in", (D, 3 * FW + HEADS + 4 * FW + 2 * HEADS + 2 * D), 1), ("fgt_bias", (1, HEADS), None),
    ("gdn_conv_w", (4, 3 * FW), 1), ("gdn_a_log", (1, HEADS), None), ("gdn_dt_bias", (1, HEADS), None),
    ("gdn_norm_w", (1, HD), None), ("gate_bias", (1, 2 * D), None), ("w_branch_fox", (FW, D), 1),
    ("w_branch_gdn", (FW, D), 1), ("w_out", (D, D), 0), ("norm_mix_w", (1, D), None), ("norm_ffn_w", (1, D), None),
    ("ffn_w_up", (D, 2 * DFF), 1), ("ffn_conv_w", (3, 2 * DFF), 1), ("ffn_conv_b", (1, 2 * DFF), None),
    ("ffn_w_down", (DFF, D), 0), ("norm_final_w", (1, D), None))
SPLIT_F32 = ("meta_tokens", "gdn_conv_w", "ffn_conv_w")


def _shard_shape(shape, axis):
    if axis is None:
        return shape
    return tuple(s // N_CHIPS if a == axis else s for a, s in enumerate(shape))


def _shard_of(full, axis, q):
    if axis is None:
        return full
    n = full.shape[axis] // N_CHIPS
    return lax.slice_in_dim(full, q * n, (q + 1) * n, axis=axis)


def _pack_rows(n_elems):
    rows = -(-n_elems // PACK_W)
    return -(-rows // PACK_ROW_ALIGN) * PACK_ROW_ALIGN


def _pack(pieces, dtype):
    flat = jnp.concatenate([p.reshape(-1).astype(dtype) for p in pieces])
    rows = _pack_rows(flat.shape[0])
    return jnp.pad(flat, (0, rows * PACK_W - flat.shape[0])).reshape(rows, PACK_W)


def _unpack(slab, shapes):
    flat = slab.reshape(-1)
    out, off = [], 0
    for s in shapes:
        n = int(np.prod(s))
        out.append(flat[off:off + n].reshape(s))
        off += n
    return out


HBM_SPEC = pl.BlockSpec(memory_space=pltpu.HBM)
MESH_ID = pl.DeviceIdType.MESH


def _chip_exchange(srcs, name, scatter):
    n = len(srcs)

    def body(*refs):
        src_refs, out_refs = refs[:n], refs[n:2 * n]
        send_sems, recv_sems, local_sems = refs[2 * n:]
        x, y, c = lax.axis_index("x"), lax.axis_index("y"), lax.axis_index("c")
        q = 2 * x + y
        peers = [(1 - x, y), (x, 1 - y), (1 - x, 1 - y)]

        def remote(a, k, src_slot, dst_slot):
            px, py = peers[k]
            src = src_refs[a].at[src_slot] if scatter else src_refs[a]
            return pltpu.make_async_remote_copy(
                src_ref=src, dst_ref=out_refs[a].at[dst_slot], send_sem=send_sems.at[3 * a + k],
                recv_sem=recv_sems.at[3 * a + k], device_id=(px, py, c), device_id_type=MESH_ID)

        mine = [pltpu.make_async_copy(src_refs[a].at[q] if scatter else src_refs[a], out_refs[a].at[q], local_sems.at[a])
                for a in range(n)]
        for cp in mine:
            cp.start()
        sends = [remote(a, k, 2 * px + py, q) for a in range(n) for k, (px, py) in enumerate(peers)]
        for cp in sends:
            cp.start()
        for a in range(n):
            for k, (px, py) in enumerate(peers):
                remote(a, k, 0, 2 * px + py).wait_recv()
        for cp in sends:
            cp.wait_send()
        for cp in mine:
            cp.wait()

    out_shape = [jax.ShapeDtypeStruct((N_CHIPS,) + s.shape[-2:], s.dtype) for s in srcs]
    return pl.pallas_call(
        body, name=name, in_specs=[HBM_SPEC] * n, out_specs=[HBM_SPEC] * n, out_shape=out_shape,
        scratch_shapes=[pltpu.SemaphoreType.DMA((3 * n,)), pltpu.SemaphoreType.DMA((3 * n,)), pltpu.SemaphoreType.DMA((n,))],
    )(*srcs)


def _sibling_swap(slabs, name):
    n = len(slabs)

    def body(*refs):
        src_refs, out_refs, send_sems, recv_sems = refs[:n], refs[n:2 * n], refs[2 * n], refs[2 * n + 1]
        x, y, c = lax.axis_index("x"), lax.axis_index("y"), lax.axis_index("c")
        cps = [pltpu.make_async_remote_copy(src_ref=src_refs[a], dst_ref=out_refs[a], send_sem=send_sems.at[a],
                                            recv_sem=recv_sems.at[a], device_id=(x, y, 1 - c), device_id_type=MESH_ID)
               for a in range(n)]
        for cp in cps:
            cp.start()
        for cp in cps:
            cp.wait_recv()
        for cp in cps:
            cp.wait_send()

    return pl.pallas_call(
        body, name=name, in_specs=[HBM_SPEC] * n, out_specs=[HBM_SPEC] * n,
        out_shape=[jax.ShapeDtypeStruct(s.shape, s.dtype) for s in slabs],
        scratch_shapes=[pltpu.SemaphoreType.DMA((n,)), pltpu.SemaphoreType.DMA((n,))])(*slabs)


def _sum_chips(r, name):
    rows, cols = r.shape[1:]
    tb = _div(rows, 256, 16)

    def body(r0, r1, r2, r3, o_ref):
        o_ref[...] = ((r0[0].astype(F32) + r1[0].astype(F32)) + r2[0].astype(F32)) + r3[0].astype(F32)

    spec = lambda j: pl.BlockSpec((1, tb, cols), lambda i: (j, i, 0))
    return pl.pallas_call(
        body, name=name, grid=(rows // tb,), in_specs=[spec(0), spec(1), spec(2), spec(3)],
        out_specs=pl.BlockSpec((tb, cols), lambda i: (i, 0)), out_shape=jax.ShapeDtypeStruct((rows, cols), F32),
        compiler_params=_cp("parallel"))(r, r, r, r)


def _adamw(w, m, v, p, q, name):
    rows, cols = w.shape
    tb = _div(rows, 256, 8)

    def body(w_ref, m_ref, v_ref, p_ref, q_ref, g_ref, d_ref, nm_ref, nv_ref):
        g = p_ref[...] + q_ref[...]
        m_new = B1 * m_ref[...] + (1.0 - B1) * g
        v_new = B2 * v_ref[...] + (1.0 - B2) * (g * g)
        g_ref[...] = g
        nm_ref[...] = m_new
        nv_ref[...] = v_new
        m_hat = m_new / (1.0 - B1 ** STEP)
        v_hat = v_new / (1.0 - B2 ** STEP)
        d_ref[...] = -LR * (m_hat / (jnp.sqrt(v_hat) + AEPS) + WD * w_ref[...])

    spec = pl.BlockSpec((tb, cols), lambda i: (i, 0))
    return pl.pallas_call(
        body, name=name, grid=(rows // tb,), in_specs=[spec] * 5, out_specs=[spec] * 4,
        out_shape=[jax.ShapeDtypeStruct((rows, cols), F32)] * 4, compiler_params=_cp("parallel"))(w, m, v, p, q)


def _split_w_in(w_in):
    o1 = 3 * FW
    o2 = o1 + HEADS
    o3 = o2 + 3 * FW
    o4 = o3 + FW
    o5 = o4 + HEADS
    o6 = o5 + HEADS
    pad = jnp.zeros((w_in.shape[0], SMALL_W - 3 * HEADS), w_in.dtype)
    rest = jnp.concatenate([w_in[:, o2:o3], w_in[:, o3:o4], w_in[:, o6:], w_in[:, o1:o2], w_in[:, o4:o5], w_in[:, o5:o6], pad],
                           axis=1)
    return w_in[:, :o1], rest


AXIS = {n: a for n, _, a in WEIGHTS}
SMALL = tuple(n for n, _, _ in WEIGHTS if n not in BIG)


def _by_chip(full, axis):
    rows, cols = full.shape
    if axis == 0:
        return full.reshape(N_CHIPS, rows // N_CHIPS, cols)
    return full.reshape(rows, N_CHIPS, cols // N_CHIPS).transpose(1, 0, 2)


def _from_chips(parts, axis):
    _, r, c = parts.shape
    if axis == 0:
        return parts.reshape(N_CHIPS * r, c)
    return parts.transpose(1, 0, 2).reshape(r, N_CHIPS * c)


def _gather_weights(shards):
    hi = {n: shards[n].astype(jnp.bfloat16) for n in SPLIT_F32}
    lo = [(shards[n] - hi[n].astype(F32)).astype(jnp.bfloat16) for n in SPLIT_F32]
    slab = _pack([hi[n] for n in SPLIT_F32] + lo, jnp.bfloat16)
    got = _chip_exchange([shards[n].astype(jnp.bfloat16) for n in BIG] + [slab], "gather_weights", scatter=False)
    full = {n: _from_chips(g, AXIS[n]) for n, g in zip(BIG, got)}
    shapes = [shards[n].shape for n in SPLIT_F32] * 2
    per_chip = [_unpack(got[-1][j], shapes) for j in range(N_CHIPS)]
    for i, n in enumerate(SPLIT_F32):
        join = lambda off: jnp.concatenate([per_chip[j][off + i] for j in range(N_CHIPS)], axis=1).astype(F32)
        full[n] = join(0) + join(len(SPLIT_F32))
    return full


def kernel(x, meta_tokens, w_in, fgt_bias, gdn_conv_w, gdn_a_log, gdn_dt_bias, gdn_norm_w, gate_bias, w_branch_fox, w_branch_gdn, w_out, norm_mix_w, norm_ffn_w, ffn_w_up, ffn_conv_w, ffn_conv_b, ffn_w_down, norm_final_w, loss_target, m_meta_tokens, m_w_in, m_fgt_bias, m_gdn_conv_w, m_gdn_a_log, m_gdn_dt_bias, m_gdn_norm_w, m_gate_bias, m_w_branch_fox, m_w_branch_gdn, m_w_out, m_norm_mix_w, m_norm_ffn_w, m_ffn_w_up, m_ffn_conv_w, m_ffn_conv_b, m_ffn_w_down, m_norm_final_w, v_meta_tokens, v_w_in, v_fgt_bias, v_gdn_conv_w, v_gdn_a_log, v_gdn_dt_bias, v_gdn_norm_w, v_gate_bias, v_w_branch_fox, v_w_branch_gdn, v_w_out, v_norm_mix_w, v_norm_ffn_w, v_ffn_w_up, v_ffn_conv_w, v_ffn_conv_b, v_ffn_w_down, v_norm_final_w):
    weights = dict(meta_tokens=meta_tokens, w_in=w_in, fgt_bias=fgt_bias, gdn_conv_w=gdn_conv_w, gdn_a_log=gdn_a_log, gdn_dt_bias=gdn_dt_bias, gdn_norm_w=gdn_norm_w, gate_bias=gate_bias, w_branch_fox=w_branch_fox, w_branch_gdn=w_branch_gdn, w_out=w_out, norm_mix_w=norm_mix_w, norm_ffn_w=norm_ffn_w, ffn_w_up=ffn_w_up, ffn_conv_w=ffn_conv_w, ffn_conv_b=ffn_conv_b, ffn_w_down=ffn_w_down, norm_final_w=norm_final_w)
    m_in = dict(meta_tokens=m_meta_tokens, w_in=m_w_in, fgt_bias=m_fgt_bias, gdn_conv_w=m_gdn_conv_w, gdn_a_log=m_gdn_a_log, gdn_dt_bias=m_gdn_dt_bias, gdn_norm_w=m_gdn_norm_w, gate_bias=m_gate_bias, w_branch_fox=m_w_branch_fox, w_branch_gdn=m_w_branch_gdn, w_out=m_w_out, norm_mix_w=m_norm_mix_w, norm_ffn_w=m_norm_ffn_w, ffn_w_up=m_ffn_w_up, ffn_conv_w=m_ffn_conv_w, ffn_conv_b=m_ffn_conv_b, ffn_w_down=m_ffn_w_down, norm_final_w=m_norm_final_w)
    v_in = dict(meta_tokens=v_meta_tokens, w_in=v_w_in, fgt_bias=v_fgt_bias, gdn_conv_w=v_gdn_conv_w, gdn_a_log=v_gdn_a_log, gdn_dt_bias=v_gdn_dt_bias, gdn_norm_w=v_gdn_norm_w, gate_bias=v_gate_bias, w_branch_fox=v_w_branch_fox, w_branch_gdn=v_w_branch_gdn, w_out=v_w_out, norm_mix_w=v_norm_mix_w, norm_ffn_w=v_norm_ffn_w, ffn_w_up=v_ffn_w_up, ffn_conv_w=v_ffn_conv_w, ffn_conv_b=v_ffn_conv_b, ffn_w_down=v_ffn_w_down, norm_final_w=v_norm_final_w)
    shard2d = {n: _shard_shape(s, a) for n, s, a in WEIGHTS}
    as2d = lambda d: {n: d[n].reshape(shard2d[n]) for n, _, _ in WEIGHTS}
    w2, m2, v2 = as2d(weights), as2d(m_in), as2d(v_in)

    full = _gather_weights(w2)
    w_fox, w_rest = _split_w_in(full["w_in"])
    flat = lambda n: w2[n].reshape(-1)
    local_w = dict(
        meta=full["meta_tokens"], w_fox=w_fox, w_rest=w_rest, fgt_bias=flat("fgt_bias"), gdn_conv=full["gdn_conv_w"],
        a_log=flat("gdn_a_log"), dt_bias=flat("gdn_dt_bias"), gdn_norm=flat("gdn_norm_w"), gate_bias=flat("gate_bias"),
        w_bfox=full["w_branch_fox"], w_bgdn=full["w_branch_gdn"], w_out=full["w_out"], norm_mix=flat("norm_mix_w"),
        norm_ffn=flat("norm_ffn_w"), w_up=full["ffn_w_up"], ffn_conv=full["ffn_conv_w"], ffn_conv_b=flat("ffn_conv_b"),
        w_down=full["ffn_w_down"], norm_final=flat("norm_final_w"))

    loss, grad_x, grads = _local_step(x[0], loss_target[0], local_w)

    g2 = {n: grads[n].reshape(s) for n, s, _ in WEIGHTS}
    send = [_by_chip(g2[n].astype(jnp.bfloat16), AXIS[n]) for n in BIG]
    send.append(jnp.stack([_pack([_shard_of(g2[n], AXIS[n], j) for n in SMALL], F32) for j in range(N_CHIPS)]))
    parts = [_sum_chips(r, "sum_" + n) for r, n in zip(_chip_exchange(send, "scatter_grads", scatter=True), BIG + ("small",))]
    others = _sibling_swap(parts, "swap_grads")
    slab = lambda d: _pack([d[n] for n in SMALL], F32)
    state = [(w2[n], m2[n], v2[n]) for n in BIG] + [(slab(w2), slab(m2), slab(v2))]
    outs = [_adamw(w, m, v, p, q, "adamw_" + n) for (w, m, v), p, q, n in zip(state, parts, others, BIG + ("small",))]
    small = [_unpack(o, [weights[n].shape for n in SMALL]) for o in outs[-1]]
    result = []
    for kind in range(4):
        by_name = {n: outs[i][kind].reshape(weights[n].shape) for i, n in enumerate(BIG)}
        by_name.update(zip(SMALL, small[kind]))
        result += [by_name[n] for n, _, _ in WEIGHTS]
    total = lax.psum(loss[0, 0], ("x", "y", "c"))
    return (total, grad_x[None], *result)
```

```python
import functools

import numpy as np
import jax
import jax.numpy as jnp
from jax import lax
from jax.experimental import pallas as pl
from jax.experimental.pallas import tpu as pltpu

F32 = jnp.float32
_BF = jnp.bfloat16
HI = lax.Precision.HIGHEST

D = 1024
N_META = 16
CH = 64
FRONT = CH - N_META
X0 = CH
HEADS = 8
HD = 64
FW = HEADS * HD
DFF = 2816
EPS = 1e-6
NEG = -1e30
T_ALIGN = 256
SMALL_W = 128
REST_W = 3 * FW + FW + 2 * D + SMALL_W
OFF_Z = 3 * FW
OFF_GATES = 4 * FW
OFF_SMALL = 4 * FW + 2 * D

LR, B1, B2, AEPS, WD, STEP = 0.001, 0.9, 0.999, 1e-08, 0.01, 10

VMEM_LIMIT = 56 * 1024 * 1024
ROW_TILE = 640
MM_TM, MM_TN, MM_TK = 1280, 512, 2816
ATT_TILE = 640


def _div(n, target, mult=128):
    if n <= target:
        return n
    best = None
    for d in range(mult, target + 1, mult):
        if n % d == 0:
            best = d
    assert best is not None, (n, target, mult)
    return best


def _cp(*sem):
    return pltpu.CompilerParams(dimension_semantics=sem, vmem_limit_bytes=VMEM_LIMIT)


def _sig(x):
    return 1.0 / (1.0 + jnp.exp(-x))


MM_VMEM_BUDGET = 40 * 1024 * 1024
MM_STEP_BYTES = 1 << 20


def _mm_tiles(m, n, k, sa, sb, so, has_resid):
    divs = lambda d, cap: [t for t in range(128, min(d, cap) + 1, 128) if d % t == 0] or [d]
    best = None
    for tm in divs(m, MM_TM * 2):
        for tn in divs(n, 4608):
            fixed = tm * tn * (4 + 2 * so + (8 if has_resid else 0))
            tks = [t for t in divs(k, MM_TK) if 2 * t * (tm * sa + tn * sb) + fixed <= MM_VMEM_BUDGET]
            if not tks:
                continue
            tk = tks[-1]
            steps = (m // tm) * (n // tn) * (k // tk)
            rmw = (k // tk - 1) * m * n * 4
            cost = (n // tn) * m * k * sa + (m // tm) * k * n * sb + steps * MM_STEP_BYTES + rmw
            if best is None or cost < best[0]:
                best = (cost, tm, tn, tk)
    assert best is not None, (m, n, k)
    return best[1:]


def _mm_t(a, b, name):
    return _mm(a, b, ta=True, name=name)


def _mm(a, b, *, ta=False, tb=False, out_dtype=F32, resid=None, name):
    K, M = a.shape if ta else a.shape[::-1]
    N = b.shape[0] if tb else b.shape[1]
    assert (b.shape[1] if tb else b.shape[0]) == K
    tm, tn, tk = _mm_tiles(M, N, K, a.dtype.itemsize, b.dtype.itemsize, jnp.dtype(out_dtype).itemsize, resid is not None)
    nk = K // tk
    dims = (((0 if ta else 1,), (1 if tb else 0,)), ((), ()))
    mxu = _BF

    def body(*refs):
        if resid is None:
            a_ref, b_ref, o_ref, acc = refs
            r_ref = None
        else:
            a_ref, b_ref, r_ref, o_ref, acc = refs
        k = pl.program_id(2)
        part = lax.dot_general(a_ref[...].astype(mxu), b_ref[...].astype(mxu), dims, preferred_element_type=F32)

        @pl.when(k == 0)
        def _():
            acc[...] = part

        @pl.when(k > 0)
        def _():
            acc[...] += part

        @pl.when(k == nk - 1)
        def _():
            r = acc[...]
            if r_ref is not None:
                r = r + r_ref[...]
            o_ref[...] = r.astype(o_ref.dtype)

    a_spec = pl.BlockSpec((tk, tm), lambda i, j, k: (k, i)) if ta else pl.BlockSpec((tm, tk), lambda i, j, k: (i, k))
    b_spec = pl.BlockSpec((tn, tk), lambda i, j, k: (j, k)) if tb else pl.BlockSpec((tk, tn), lambda i, j, k: (k, j))
    o_spec = pl.BlockSpec((tm, tn), lambda i, j, k: (i, j))
    in_specs = [a_spec, b_spec] + ([o_spec] if resid is not None else [])
    args = (a, b) + ((resid,) if resid is not None else ())
    return pl.pallas_call(
        body, name=name, grid=(M // tm, N // tn, nk), in_specs=in_specs, out_specs=o_spec,
        out_shape=jax.ShapeDtypeStruct((M, N), out_dtype), scratch_shapes=[pltpu.VMEM((tm, tn), F32)],
        compiler_params=_cp("parallel", "parallel", "arbitrary"))(*args)


def _rms_fwd(h, w, name):
    T = h.shape[0]
    tb = _div(T, ROW_TILE, 8)

    def body(h_ref, w_ref, o_ref):
        x = h_ref[...]
        r = lax.rsqrt(jnp.mean(x * x, axis=-1, keepdims=True) + EPS)
        o_ref[...] = (x * r * w_ref[...]).astype(o_ref.dtype)

    return pl.pallas_call(
        body, name=name, grid=(T // tb,),
        in_specs=[pl.BlockSpec((tb, D), lambda i: (i, 0)), pl.BlockSpec((1, D), lambda i: (0, 0))],
        out_specs=pl.BlockSpec((tb, D), lambda i: (i, 0)), out_shape=jax.ShapeDtypeStruct((T, D), _BF),
        compiler_params=_cp("parallel"))(h, w)


def _rms_bwd(h, w, dy, dres, name):
    T = h.shape[0]
    tb = _div(T, ROW_TILE, 8)

    def body(h_ref, w_ref, dy_ref, dr_ref, dh_ref, dw_ref):
        x = h_ref[...]
        r = lax.rsqrt(jnp.mean(x * x, axis=-1, keepdims=True) + EPS)
        xh = x * r
        dy = dy_ref[...]
        g = dy * w_ref[...]
        dh_ref[...] = dr_ref[...] + r * (g - xh * jnp.mean(xh * g, axis=-1, keepdims=True))
        part = jnp.sum(dy * xh, axis=0, keepdims=True)

        @pl.when(pl.program_id(0) == 0)
        def _():
            dw_ref[...] = part

        @pl.when(pl.program_id(0) > 0)
        def _():
            dw_ref[...] += part

    row = pl.BlockSpec((tb, D), lambda i: (i, 0))
    one = pl.BlockSpec((1, D), lambda i: (0, 0))
    return pl.pallas_call(
        body, name=name, grid=(T // tb,), in_specs=[row, one, row, row], out_specs=[row, one],
        out_shape=[jax.ShapeDtypeStruct((T, D), F32), jax.ShapeDtypeStruct((1, D), F32)],
        compiler_params=_cp("arbitrary"))(h, w, dy, dres)


def _fgate_fwd(small, bias):
    T = small.shape[0]
    tb = _div(T, ROW_TILE, 8)

    def body(s_ref, b_ref, c_ref, carry):
        @pl.when(pl.program_id(0) == 0)
        def _():
            carry[...] = jnp.zeros_like(carry)

        lf = jax.nn.log_sigmoid(s_ref[...] + b_ref[...])
        r = lax.broadcasted_iota(jnp.int32, (tb, tb), 0)
        c = lax.broadcasted_iota(jnp.int32, (tb, tb), 1)
        tri = (r >= c).astype(F32)
        cs = jnp.dot(tri, lf, precision=HI, preferred_element_type=F32) + carry[...]
        c_ref[...] = cs
        carry[...] = cs[tb - 1:tb, :]

    return pl.pallas_call(
        body, name="fgate_fwd", grid=(T // tb,),
        in_specs=[pl.BlockSpec((tb, SMALL_W), lambda i: (i, 0)), pl.BlockSpec((1, SMALL_W), lambda i: (0, 0))],
        out_specs=pl.BlockSpec((tb, SMALL_W), lambda i: (i, 0)), out_shape=jax.ShapeDtypeStruct((T, SMALL_W), F32),
        scratch_shapes=[pltpu.VMEM((1, SMALL_W), F32)], compiler_params=_cp("arbitrary"))(small, bias)


def _fgate_bwd(small, bias, dc):
    T = small.shape[0]
    tb = _div(T, ROW_TILE, 8)
    nb = T // tb

    def body(s_ref, b_ref, dc_ref, df_ref, db_ref, carry):
        @pl.when(pl.program_id(0) == 0)
        def _():
            carry[...] = jnp.zeros_like(carry)

        r = lax.broadcasted_iota(jnp.int32, (tb, tb), 0)
        c = lax.broadcasted_iota(jnp.int32, (tb, tb), 1)
        tri = (r <= c).astype(F32)
        dlf = jnp.dot(tri, dc_ref[...], precision=HI, preferred_element_type=F32) + carry[...]
        carry[...] = dlf[0:1, :]
        df = dlf * _sig(-(s_ref[...] + b_ref[...]))
        df_ref[...] = df
        part = jnp.sum(df, axis=0, keepdims=True)

        @pl.when(pl.program_id(0) == 0)
        def _():
            db_ref[...] = part

        @pl.when(pl.program_id(0) > 0)
        def _():
            db_ref[...] += part

    rev = pl.BlockSpec((tb, SMALL_W), lambda i: (nb - 1 - i, 0))
    one = pl.BlockSpec((1, SMALL_W), lambda i: (0, 0))
    return pl.pallas_call(
        body, name="fgate_bwd", grid=(nb,), in_specs=[rev, one, rev], out_specs=[rev, one],
        out_shape=[jax.ShapeDtypeStruct((T, SMALL_W), F32), jax.ShapeDtypeStruct((1, SMALL_W), F32)],
        scratch_shapes=[pltpu.VMEM((1, SMALL_W), F32)], compiler_params=_cp("arbitrary"))(small, bias, dc)


def _att_rows(a8, tb):
    T = a8.shape[0]
    return a8.T.reshape(HEADS // 2, 2, T // tb, tb).transpose(0, 2, 1, 3)


def _att_cols(a8):
    return jnp.repeat(a8, HD, axis=1)


EXP_ZERO = -104.0
SKIP_SLACK = 2.0


def _att_stats(qkv, c_cols, ones_blk):
    T = qkv.shape[0]
    tb = _div(T, ATT_TILE)

    def body(q_ref, k_ref, c_ref, e_ref, o_ref):
        q = q_ref[...].astype(F32)
        k = k_ref[...].astype(F32)
        qn = jnp.max(jnp.dot(q * q, e_ref[...], precision=HI, preferred_element_type=F32), axis=0, keepdims=True)
        kn = jnp.max(jnp.dot(k * k, e_ref[...], precision=HI, preferred_element_type=F32), axis=0, keepdims=True)
        c = c_ref[...]
        o_ref[0] = jnp.concatenate([jnp.sqrt(qn), jnp.sqrt(kn), jnp.max(c, axis=0, keepdims=True),
                                    jnp.min(c, axis=0, keepdims=True), jnp.zeros((4, FW), F32)], axis=0)

    blk = lambda off: pl.BlockSpec((tb, FW), lambda i: (i, off))
    return pl.pallas_call(
        body, name="att_stats", grid=(T // tb,),
        in_specs=[blk(0), blk(1), blk(0), pl.BlockSpec((FW, FW), lambda i: (0, 0))],
        out_specs=pl.BlockSpec((1, 8, FW), lambda i: (i, 0, 0)), out_shape=jax.ShapeDtypeStruct((T // tb, 8, FW), F32),
        compiler_params=_cp("parallel"))(qkv, qkv, c_cols, ones_blk)


def _att_plan(stats):
    nb = stats.shape[0]
    st = stats[:, :4, ::HD]
    qmax, kmax, cmax, cmin = (st[:, r, :].T for r in range(4))
    bound = (HD ** -0.5) * 1.01 * qmax[:, :, None] * (kmax[:, None, :] + kmax[:, :, None]) \
        + cmax[:, :, None] - cmin[:, None, :] + SKIP_SLACK
    ii = lax.broadcasted_iota(jnp.int32, (nb, nb), 0)
    jj = lax.broadcasted_iota(jnp.int32, (nb, nb), 1)
    skip = (bound < EXP_ZERO) & (jj < ii)[None]
    live = (~skip & (jj <= ii)[None]).reshape(HEADS // 2, 2, nb, nb).any(axis=1)
    jfirst = jnp.argmax(live, axis=2).astype(jnp.int32)
    ilast = (nb - 1 - jnp.argmax(live[:, ::-1, :], axis=1)).astype(jnp.int32)
    return skip.astype(jnp.int32).reshape(-1), jfirst.reshape(-1), ilast.reshape(-1)


def _fox_fwd(qkv, c_cols, c_rows, plan):
    T = qkv.shape[0]
    tb = _div(T, ATT_TILE)
    nb = T // tb
    npair = HEADS // 2
    scale = HD ** -0.5
    dn = (((1,), (1,)), ((), ()))

    def body(skip_ref, jfirst_ref, ilast_ref, q_ref, k_ref, v_ref, cq_ref, ck_ref, o_ref, l_ref, m_scr, l_scr, acc_scr):
        pr = pl.program_id(0)
        i = pl.program_id(1)
        q = q_ref[...]
        lane = lax.broadcasted_iota(jnp.int32, (1, 2 * HD), 1)
        sel0 = lane < HD
        zero = jnp.zeros_like(q)
        qh = (jnp.where(sel0, q, zero) * scale, jnp.where(sel0, zero, q) * scale)
        cq = (cq_ref[:, 0:1], cq_ref[:, HD:HD + 1])
        row = i * tb + lax.broadcasted_iota(jnp.int32, (tb, 1), 0)
        m_scr[...] = jnp.full(m_scr.shape, NEG, F32)
        l_scr[...] = jnp.zeros(l_scr.shape, F32)
        acc_scr[...] = jnp.zeros(acc_scr.shape, F32)

        def step(j, _):
            start = pl.multiple_of(j * tb, tb)
            for h in range(2):
                @pl.when(skip_ref[((2 * pr + h) * nb + i) * nb + j] == 0)
                def _():
                    kb = k_ref[pl.ds(start, tb), :]
                    vb = v_ref[pl.ds(start, tb), :]
                    col = j * tb + lax.broadcasted_iota(jnp.int32, (1, tb), 1)
                    mask = (col <= row) & (col >= FRONT)
                    s = lax.dot_general(qh[h], kb, dn, preferred_element_type=F32) + cq[h] - ck_ref[0, j, h:h + 1, :]
                    s = jnp.where(mask, s, NEG)
                    m = m_scr[h]
                    m_new = jnp.maximum(m, jnp.max(s, axis=-1, keepdims=True))
                    p = jnp.where(mask, jnp.exp(s - m_new), 0.0)
                    alpha = jnp.exp(m - m_new)
                    m_scr[h] = m_new
                    l_scr[h] = l_scr[h] * alpha + jnp.sum(p, axis=-1, keepdims=True)
                    acc_scr[h] = acc_scr[h] * alpha + jnp.dot(p.astype(vb.dtype), vb, preferred_element_type=F32)
            return 0

        lax.fori_loop(jfirst_ref[pr * nb + i], i + 1, step, 0)
        outs, lses = [], []
        for h in range(2):
            l = l_scr[h]
            ok = l > 0.0
            ls = jnp.where(ok, l, 1.0)
            outs.append(jnp.where(ok, acc_scr[h] / ls, 0.0))
            lses.append(jnp.where(ok, m_scr[h] + jnp.log(ls), 0.0))
        o_ref[...] = jnp.where(sel0, outs[0], outs[1])
        l_ref[...] = jnp.where(sel0, lses[0], lses[1])

    blk = lambda off: pl.BlockSpec((tb, 2 * HD), lambda p, i, *_: (i, off + p))
    full = lambda off: pl.BlockSpec((T, 2 * HD), lambda p, i, *_: (0, off + p))
    rows = pl.BlockSpec((1, nb, 2, tb), lambda p, i, *_: (p, 0, 0, 0))
    return pl.pallas_call(
        body, name="fox_fwd",
        grid_spec=pltpu.PrefetchScalarGridSpec(
            num_scalar_prefetch=3, grid=(npair, nb),
            in_specs=[blk(0), full(npair), full(2 * npair), blk(0), rows], out_specs=[blk(0), blk(0)],
            scratch_shapes=[pltpu.VMEM((2, tb, 1), F32), pltpu.VMEM((2, tb, 1), F32), pltpu.VMEM((2, tb, 2 * HD), F32)]),
        out_shape=[jax.ShapeDtypeStruct((T, FW), F32), jax.ShapeDtypeStruct((T, FW), F32)],
        compiler_params=_cp("parallel", "arbitrary"))(*plan, qkv, qkv, qkv, c_cols, c_rows)


def _fox_bwd_dq(qkv, do, c_cols, c_rows, lse, delta, plan):
    T = qkv.shape[0]
    tb = _div(T, ATT_TILE)
    nb = T // tb
    npair = HEADS // 2
    scale = HD ** -0.5
    dn = (((1,), (1,)), ((), ()))

    def body(skip_ref, jfirst_ref, ilast_ref, q_ref, k_ref, v_ref, do_ref, cq_ref, ck_ref, l_ref, dl_ref, dq_ref, dc_ref,
             dq_scr, dc_scr):
        pr = pl.program_id(0)
        i = pl.program_id(1)
        q = q_ref[...]
        do = do_ref[...]
        lane = lax.broadcasted_iota(jnp.int32, (1, 2 * HD), 1)
        sel0 = lane < HD
        qh = (jnp.where(sel0, q, jnp.zeros_like(q)) * scale, jnp.where(sel0, jnp.zeros_like(q), q) * scale)
        doh = (jnp.where(sel0, do, jnp.zeros_like(do)), jnp.where(sel0, jnp.zeros_like(do), do))
        cq = (cq_ref[:, 0:1], cq_ref[:, HD:HD + 1])
        ls = (l_ref[:, 0:1], l_ref[:, HD:HD + 1])
        dl = (dl_ref[:, 0:1], dl_ref[:, HD:HD + 1])
        row = i * tb + lax.broadcasted_iota(jnp.int32, (tb, 1), 0)
        dq_scr[...] = jnp.zeros(dq_scr.shape, F32)
        dc_scr[...] = jnp.zeros(dc_scr.shape, F32)

        def step(j, _):
            start = pl.multiple_of(j * tb, tb)
            for h in range(2):
                @pl.when(skip_ref[((2 * pr + h) * nb + i) * nb + j] == 0)
                def _():
                    kb = k_ref[pl.ds(start, tb), :]
                    vb = v_ref[pl.ds(start, tb), :]
                    col = j * tb + lax.broadcasted_iota(jnp.int32, (1, tb), 1)
                    mask = (col <= row) & (col >= FRONT)
                    s = lax.dot_general(qh[h], kb, dn, preferred_element_type=F32) + cq[h] - ck_ref[0, j, h:h + 1, :]
                    p = jnp.where(mask, jnp.exp(jnp.where(mask, s, NEG) - ls[h]), 0.0)
                    dp = lax.dot_general(doh[h], vb, dn, preferred_element_type=F32)
                    ds = p * (dp - dl[h])
                    dq_scr[h] += jnp.dot(ds.astype(kb.dtype), kb, preferred_element_type=F32)
                    dc_scr[h] += jnp.sum(ds, axis=-1, keepdims=True)
            return 0

        lax.fori_loop(jfirst_ref[pr * nb + i], i + 1, step, 0)
        dq_ref[...] = jnp.where(sel0, dq_scr[0], dq_scr[1]) * scale
        dc_ref[...] = jnp.where(sel0, dc_scr[0], dc_scr[1])

    blk = lambda off: pl.BlockSpec((tb, 2 * HD), lambda p, i, *_: (i, off + p))
    full = lambda off: pl.BlockSpec((T, 2 * HD), lambda p, i, *_: (0, off + p))
    rows = pl.BlockSpec((1, nb, 2, tb), lambda p, i, *_: (p, 0, 0, 0))
    return pl.pallas_call(
        body, name="fox_bwd_dq",
        grid_spec=pltpu.PrefetchScalarGridSpec(
            num_scalar_prefetch=3, grid=(npair, nb),
            in_specs=[blk(0), full(npair), full(2 * npair), blk(0), blk(0), rows, blk(0), blk(0)], out_specs=[blk(0), blk(0)],
            scratch_shapes=[pltpu.VMEM((2, tb, 2 * HD), F32), pltpu.VMEM((2, tb, 1), F32)]),
        out_shape=[jax.ShapeDtypeStruct((T, FW), F32), jax.ShapeDtypeStruct((T, FW), F32)],
        compiler_params=_cp("parallel", "arbitrary"))(*plan, qkv, qkv, qkv, do, c_cols, c_rows, lse, delta)


def _fox_bwd_dkv(qkv, do, c_cols, c_rows, lse_rows, delta_rows, plan):
    T = qkv.shape[0]
    tb = _div(T, ATT_TILE)
    nb = T // tb
    npair = HEADS // 2
    scale = HD ** -0.5
    dn = (((1,), (1,)), ((), ()))

    def body(skip_ref, jfirst_ref, ilast_ref, q_ref, k_ref, v_ref, do_ref, ck_ref, cq_ref, l_ref, dl_ref,
             dk_ref, dv_ref, dc_ref, dk_scr, dv_scr, dc_scr):
        pr = pl.program_id(0)
        jb = pl.program_id(1)
        k = k_ref[...]
        v = v_ref[...]
        lane = lax.broadcasted_iota(jnp.int32, (1, 2 * HD), 1)
        sel0 = lane < HD
        kh = (jnp.where(sel0, k, jnp.zeros_like(k)) * scale, jnp.where(sel0, jnp.zeros_like(k), k) * scale)
        vh = (jnp.where(sel0, v, jnp.zeros_like(v)), jnp.where(sel0, jnp.zeros_like(v), v))
        ck = (ck_ref[:, 0:1], ck_ref[:, HD:HD + 1])
        kidx = jb * tb + lax.broadcasted_iota(jnp.int32, (tb, 1), 0)
        dk_scr[...] = jnp.zeros(dk_scr.shape, F32)
        dv_scr[...] = jnp.zeros(dv_scr.shape, F32)
        dc_scr[...] = jnp.zeros(dc_scr.shape, F32)

        def step(i, _):
            start = pl.multiple_of(i * tb, tb)
            for h in range(2):
                @pl.when(skip_ref[((2 * pr + h) * nb + i) * nb + jb] == 0)
                def _():
                    qb = q_ref[pl.ds(start, tb), :]
                    dob = do_ref[pl.ds(start, tb), :]
                    qidx = i * tb + lax.broadcasted_iota(jnp.int32, (1, tb), 1)
                    mask = (kidx <= qidx) & (kidx >= FRONT)
                    st = lax.dot_general(kh[h], qb, dn, preferred_element_type=F32) + cq_ref[0, i, h:h + 1, :] - ck[h]
                    pt = jnp.where(mask, jnp.exp(jnp.where(mask, st, NEG) - l_ref[0, i, h:h + 1, :]), 0.0)
                    dv_scr[h] += jnp.dot(pt.astype(dob.dtype), dob, preferred_element_type=F32)
                    dpt = lax.dot_general(vh[h], dob, dn, preferred_element_type=F32)
                    dst = pt * (dpt - dl_ref[0, i, h:h + 1, :])
                    dk_scr[h] += jnp.dot(dst.astype(qb.dtype), qb, preferred_element_type=F32)
                    dc_scr[h] -= jnp.sum(dst, axis=-1, keepdims=True)
            return 0

        lax.fori_loop(jb, ilast_ref[pr * nb + jb] + 1, step, 0)
        dk_ref[...] = jnp.where(sel0, dk_scr[0], dk_scr[1]) * scale
        dv_ref[...] = jnp.where(sel0, dv_scr[0], dv_scr[1])
        dc_ref[...] = jnp.where(sel0, dc_scr[0], dc_scr[1])

    blk = lambda off: pl.BlockSpec((tb, 2 * HD), lambda p, j, *_: (j, off + p))
    full = lambda off: pl.BlockSpec((T, 2 * HD), lambda p, j, *_: (0, off + p))
    rows = pl.BlockSpec((1, nb, 2, tb), lambda p, j, *_: (p, 0, 0, 0))
    return pl.pallas_call(
        body, name="fox_bwd_dkv",
        grid_spec=pltpu.PrefetchScalarGridSpec(
            num_scalar_prefetch=3, grid=(npair, nb),
            in_specs=[full(0), blk(npair), blk(2 * npair), full(0), blk(0), rows, rows, rows],
            out_specs=[blk(0), blk(0), blk(0)],
            scratch_shapes=[pltpu.VMEM((2, tb, 2 * HD), F32), pltpu.VMEM((2, tb, 2 * HD), F32), pltpu.VMEM((2, tb, 1), F32)]),
        out_shape=[jax.ShapeDtypeStruct((T, FW), F32)] * 3,
        compiler_params=_cp("parallel", "arbitrary"))(*plan, qkv, qkv, qkv, do, c_cols, c_rows, lse_rows, delta_rows)


def _head_dot(a, b, ones_blk):
    T = a.shape[0]
    tb = _div(T, ROW_TILE, 8)

    def body(a_ref, b_ref, e_ref, o_ref):
        prod = a_ref[...].astype(F32) * b_ref[...].astype(F32)
        o_ref[...] = jnp.dot(prod, e_ref[...], precision=HI, preferred_element_type=F32)

    row = pl.BlockSpec((tb, FW), lambda i: (i, 0))
    return pl.pallas_call(
        body, name="head_dot", grid=(T // tb,), in_specs=[row, row, pl.BlockSpec((FW, FW), lambda i: (0, 0))],
        out_specs=row, out_shape=jax.ShapeDtypeStruct((T, FW), F32), compiler_params=_cp("parallel"))(a, b, ones_blk)


CONV_RC = 32
CONV_TC = 256


def _stage_prev(scr, x_ref, h_ref, first, tb):
    scr[0:8, :] = jnp.where(first, 0.0, h_ref[...])
    scr[8:8 + tb, :] = x_ref[...]


def _windows_prev(scr, r, kw):
    x = scr[pl.ds(pl.multiple_of(r * CONV_RC, CONV_RC), CONV_RC + 8), :]
    return [x[8:] if k == kw - 1 else pltpu.roll(x, kw - 1 - k, 0)[8:] for k in range(kw)]


def _fold8(a):
    return a.reshape(CONV_RC // 8, 8, a.shape[-1]).sum(axis=0)


def _halo_prev(tb, tc, off=0):
    return pl.BlockSpec((8, tc), lambda j, i: (jnp.maximum(i * (tb // 8) - 1, 0), j + off))


def _gconv_fwd(rest, w):
    T = rest.shape[0]
    C = 3 * FW
    kw = w.shape[0]
    tb, tc = _div(T, 1280, CONV_RC), CONV_TC

    def body(x_ref, h_ref, w_ref, o_ref, scr):
        _stage_prev(scr, x_ref, h_ref, pl.program_id(1) == 0, tb)
        wv = w_ref[...]

        def chunk(r, _):
            win = _windows_prev(scr, r, kw)
            u = sum(wv[k:k + 1, :] * win[k] for k in range(kw))
            o_ref[pl.ds(pl.multiple_of(r * CONV_RC, CONV_RC), CONV_RC), :] = u * _sig(u)
            return 0

        lax.fori_loop(0, tb // CONV_RC, chunk, 0)

    return pl.pallas_call(
        body, name="gconv_fwd", grid=(C // tc, T // tb),
        in_specs=[pl.BlockSpec((tb, tc), lambda j, i: (i, j)), _halo_prev(tb, tc), pl.BlockSpec((kw, tc), lambda j, i: (0, j))],
        out_specs=pl.BlockSpec((tb, tc), lambda j, i: (i, j)), out_shape=jax.ShapeDtypeStruct((T, C), F32),
        scratch_shapes=[pltpu.VMEM((tb + 8, tc), F32)], compiler_params=_cp("parallel", "arbitrary"))(rest, rest, w)


def _gconv_bwd_du(rest, w, dy):
    T = rest.shape[0]
    C = 3 * FW
    kw = w.shape[0]
    tb, tc = _div(T, 1280, CONV_RC), CONV_TC

    def body(x_ref, h_ref, w_ref, dy_ref, du_ref, dw_ref, scr):
        i = pl.program_id(1)
        _stage_prev(scr, x_ref, h_ref, i == 0, tb)
        wv = w_ref[...]

        def chunk(r, acc):
            rows = pl.ds(pl.multiple_of(r * CONV_RC, CONV_RC), CONV_RC)
            win = _windows_prev(scr, r, kw)
            u = sum(wv[k:k + 1, :] * win[k] for k in range(kw))
            sg = _sig(u)
            du = dy_ref[rows, :] * sg * (1.0 + u * (1.0 - sg))
            du_ref[rows, :] = du
            return tuple(acc[k] + _fold8(du * win[k]) for k in range(kw))

        acc = lax.fori_loop(0, tb // CONV_RC, chunk, tuple(jnp.zeros((8, tc), F32) for _ in range(kw)))
        part = jnp.concatenate([jnp.sum(a, axis=0, keepdims=True) for a in acc], axis=0)

        @pl.when(i == 0)
        def _():
            dw_ref[...] = part

        @pl.when(i > 0)
        def _():
            dw_ref[...] += part

    blk = pl.BlockSpec((tb, tc), lambda j, i: (i, j))
    wsp = pl.BlockSpec((kw, tc), lambda j, i: (0, j))
    return pl.pallas_call(
        body, name="gconv_bwd_du", grid=(C // tc, T // tb), in_specs=[blk, _halo_prev(tb, tc), wsp, blk],
        out_specs=[blk, wsp], out_shape=[jax.ShapeDtypeStruct((T, C), F32), jax.ShapeDtypeStruct((kw, C), F32)],
        scratch_shapes=[pltpu.VMEM((tb + 8, tc), F32)], compiler_params=_cp("parallel", "arbitrary"))(rest, rest, w, dy)


def _conv_bwd_dx(du, w, out_dtype, name):
    T, C = du.shape
    kw = w.shape[0]
    tb = _div(T, 1280, CONV_RC)
    tc = CONV_TC
    nb = T // tb

    def body(x_ref, h_ref, w_ref, o_ref, scr):
        scr[0:tb, :] = x_ref[...]
        scr[tb:tb + 8, :] = jnp.where(pl.program_id(1) == nb - 1, 0.0, h_ref[...])
        wv = w_ref[...]

        def chunk(r, _):
            start = pl.multiple_of(r * CONV_RC, CONV_RC)
            x = scr[pl.ds(start, CONV_RC + 8), :]
            acc = wv[kw - 1:kw, :] * x[:CONV_RC]
            for k in range(kw - 1):
                acc = acc + wv[k:k + 1, :] * pltpu.roll(x, CONV_RC + 8 - (kw - 1 - k), 0)[:CONV_RC]
            o_ref[pl.ds(start, CONV_RC), :] = acc.astype(o_ref.dtype)
            return 0

        lax.fori_loop(0, tb // CONV_RC, chunk, 0)

    halo = pl.BlockSpec((8, tc), lambda j, i: (jnp.minimum((i + 1) * (tb // 8), T // 8 - 1), j))
    return pl.pallas_call(
        body, name=name, grid=(C // tc, nb),
        in_specs=[pl.BlockSpec((tb, tc), lambda j, i: (i, j)), halo, pl.BlockSpec((kw, tc), lambda j, i: (0, j))],
        out_specs=pl.BlockSpec((tb, tc), lambda j, i: (i, j)), out_shape=jax.ShapeDtypeStruct((T, C), out_dtype),
        scratch_shapes=[pltpu.VMEM((tb + 8, tc), F32)], compiler_params=_cp("parallel", "arbitrary"))(du, du, w)


def _glu_fwd(up, w, b):
    T = up.shape[0]
    kw = w.shape[0]
    tb, tc = _div(T, 1280, CONV_RC), CONV_TC
    nc = DFF // tc

    def body(xg, hg, xu, hu, wg, wu, bg, bu, o_ref, sg, su):
        first = pl.program_id(1) == 0
        _stage_prev(sg, xg, hg, first, tb)
        _stage_prev(su, xu, hu, first, tb)
        wgv, wuv, bgv, buv = wg[...], wu[...], bg[...], bu[...]

        def chunk(r, _):
            wing, winu = _windows_prev(sg, r, kw), _windows_prev(su, r, kw)
            ug = bgv + sum(wgv[k:k + 1, :] * wing[k] for k in range(kw))
            uu = buv + sum(wuv[k:k + 1, :] * winu[k] for k in range(kw))
            o_ref[pl.ds(pl.multiple_of(r * CONV_RC, CONV_RC), CONV_RC), :] = (ug * _sig(ug) * uu).astype(o_ref.dtype)
            return 0

        lax.fori_loop(0, tb // CONV_RC, chunk, 0)

    blk = lambda off: pl.BlockSpec((tb, tc), lambda j, i: (i, j + off))
    wsp = lambda off: pl.BlockSpec((kw, tc), lambda j, i: (0, j + off))
    bsp = lambda off: pl.BlockSpec((1, tc), lambda j, i: (0, j + off))
    return pl.pallas_call(
        body, name="glu_fwd", grid=(nc, T // tb),
        in_specs=[blk(0), _halo_prev(tb, tc), blk(nc), _halo_prev(tb, tc, nc), wsp(0), wsp(nc), bsp(0), bsp(nc)],
        out_specs=blk(0), out_shape=jax.ShapeDtypeStruct((T, DFF), _BF),
        scratch_shapes=[pltpu.VMEM((tb + 8, tc), F32)] * 2, compiler_params=_cp("parallel", "arbitrary"))(
            up, up, up, up, w, w, b, b)


def _glu_bwd_du(up, w, b, df):
    T = up.shape[0]
    kw = w.shape[0]
    tb, tc = _div(T, 1280, CONV_RC), CONV_TC
    nc = DFF // tc

    def body(xg, hg, xu, hu, wg, wu, bg, bu, df_ref, dug_ref, duu_ref, dwg_ref, dwu_ref, dbg_ref, dbu_ref, sg, su):
        i = pl.program_id(1)
        _stage_prev(sg, xg, hg, i == 0, tb)
        _stage_prev(su, xu, hu, i == 0, tb)
        wgv, wuv, bgv, buv = wg[...], wu[...], bg[...], bu[...]

        def chunk(r, acc):
            rows = pl.ds(pl.multiple_of(r * CONV_RC, CONV_RC), CONV_RC)
            wing, winu = _windows_prev(sg, r, kw), _windows_prev(su, r, kw)
            ug = bgv + sum(wgv[k:k + 1, :] * wing[k] for k in range(kw))
            uu = buv + sum(wuv[k:k + 1, :] * winu[k] for k in range(kw))
            s = _sig(ug)
            df = df_ref[rows, :]
            dug = df * uu * s * (1.0 + ug * (1.0 - s))
            duu = df * ug * s
            dug_ref[rows, :] = dug
            duu_ref[rows, :] = duu
            new = [dug * wing[k] for k in range(kw)] + [duu * winu[k] for k in range(kw)] + [dug, duu]
            return tuple(a + _fold8(v) for a, v in zip(acc, new))

        acc = lax.fori_loop(0, tb // CONV_RC, chunk, tuple(jnp.zeros((8, tc), F32) for _ in range(2 * kw + 2)))
        col = [jnp.sum(a, axis=0, keepdims=True) for a in acc]
        parts = (jnp.concatenate(col[:kw], axis=0), jnp.concatenate(col[kw:2 * kw], axis=0), col[2 * kw], col[2 * kw + 1])
        accs = (dwg_ref, dwu_ref, dbg_ref, dbu_ref)

        @pl.when(i == 0)
        def _():
            for r, p in zip(accs, parts):
                r[...] = p

        @pl.when(i > 0)
        def _():
            for r, p in zip(accs, parts):
                r[...] += p

    blk = lambda off: pl.BlockSpec((tb, tc), lambda j, i: (i, j + off))
    wsp = lambda off: pl.BlockSpec((kw, tc), lambda j, i: (0, j + off))
    bsp = lambda off: pl.BlockSpec((1, tc), lambda j, i: (0, j + off))
    return pl.pallas_call(
        body, name="glu_bwd_du", grid=(nc, T // tb),
        in_specs=[blk(0), _halo_prev(tb, tc), blk(nc), _halo_prev(tb, tc, nc), wsp(0), wsp(nc), bsp(0), bsp(nc), blk(0)],
        out_specs=[blk(0), blk(0), wsp(0), wsp(0), bsp(0), bsp(0)],
        out_shape=[jax.ShapeDtypeStruct((T, DFF), F32)] * 2 + [jax.ShapeDtypeStruct((kw, DFF), F32)] * 2
        + [jax.ShapeDtypeStruct((1, DFF), F32)] * 2,
        scratch_shapes=[pltpu.VMEM((tb + 8, tc), F32)] * 2, compiler_params=_cp("parallel", "arbitrary"))(
            up, up, up, up, w, w, b, b, df)


def _mix_fwd(rest, gate_bias, y_fox, y_gdn):
    T = rest.shape[0]
    tb, tc = _div(T, ROW_TILE, 8), 512
    nc = D // tc
    og = OFF_GATES // tc

    def body(gf, gg, bf, bg, yf, yg, o_ref):
        o_ref[...] = (_sig(gf[...] + bf[...]) * yf[...] + _sig(gg[...] + bg[...]) * yg[...]).astype(o_ref.dtype)

    blk = lambda off: pl.BlockSpec((tb, tc), lambda i, j: (i, j + off))
    bsp = lambda off: pl.BlockSpec((1, tc), lambda i, j: (0, j + off))
    return pl.pallas_call(
        body, name="mix_fwd", grid=(T // tb, nc),
        in_specs=[blk(og), blk(og + nc), bsp(0), bsp(nc), blk(0), blk(0)], out_specs=blk(0),
        out_shape=jax.ShapeDtypeStruct((T, D), _BF), compiler_params=_cp("parallel", "parallel"))(
            rest, rest, gate_bias, gate_bias, y_fox, y_gdn)


def _mix_bwd(rest, gate_bias, y_fox, y_gdn, dmix):
    T = rest.shape[0]
    tb, tc = _div(T, ROW_TILE, 8), 512
    nc = D // tc
    og = OFF_GATES // tc

    def body(gf, gg, bf, bg, yf, yg, dm, dyf, dyg, dgf, dgg, dbf, dbg):
        i = pl.program_id(1)
        d = dm[...]
        sf = _sig(gf[...] + bf[...])
        sg = _sig(gg[...] + bg[...])
        dyf[...] = (d * sf).astype(dyf.dtype)
        dyg[...] = (d * sg).astype(dyg.dtype)
        a = d * yf[...] * sf * (1.0 - sf)
        b = d * yg[...] * sg * (1.0 - sg)
        dgf[...] = a
        dgg[...] = b
        pa = jnp.sum(a, axis=0, keepdims=True)
        pb = jnp.sum(b, axis=0, keepdims=True)

        @pl.when(i == 0)
        def _():
            dbf[...] = pa
            dbg[...] = pb

        @pl.when(i > 0)
        def _():
            dbf[...] += pa
            dbg[...] += pb

    blk = lambda off: pl.BlockSpec((tb, tc), lambda j, i: (i, j + off))
    bsp = lambda off: pl.BlockSpec((1, tc), lambda j, i: (0, j + off))
    return pl.pallas_call(
        body, name="mix_bwd", grid=(nc, T // tb),
        in_specs=[blk(og), blk(og + nc), bsp(0), bsp(nc), blk(0), blk(0), blk(0)],
        out_specs=[blk(0), blk(0), blk(0), blk(0), bsp(0), bsp(0)],
        out_shape=[jax.ShapeDtypeStruct((T, D), _BF)] * 2 + [jax.ShapeDtypeStruct((T, D), F32)] * 2
        + [jax.ShapeDtypeStruct((1, D), F32)] * 2,
        compiler_params=_cp("parallel", "arbitrary"))(rest, rest, gate_bias, gate_bias, y_fox, y_gdn, dmix)


def _loss_head(h2, w, target, n_valid):
    T = h2.shape[0]
    tb = _div(T, ROW_TILE, 8)

    def body(h_ref, w_ref, t_ref, dh_ref, loss_ref, dw_ref):
        i = pl.program_id(0)
        x = h_ref[...]
        r = lax.rsqrt(jnp.mean(x * x, axis=-1, keepdims=True) + EPS)
        xh = x * r
        row = i * tb + lax.broadcasted_iota(jnp.int32, (tb, 1), 0)
        valid = (row >= X0) & (row < X0 + n_valid)
        e = jnp.where(valid, xh * w_ref[...] - t_ref[...], 0.0)
        dy = e * (1.0 / D)
        g = dy * w_ref[...]
        dh_ref[...] = r * (g - xh * jnp.mean(xh * g, axis=-1, keepdims=True))
        lpart = 0.5 * jnp.sum(jnp.sum(e * e, axis=-1, keepdims=True) * (1.0 / D), axis=0, keepdims=True)
        wpart = jnp.sum(dy * xh, axis=0, keepdims=True)

        @pl.when(i == 0)
        def _():
            loss_ref[...] = lpart
            dw_ref[...] = wpart

        @pl.when(i > 0)
        def _():
            loss_ref[...] += lpart
            dw_ref[...] += wpart

    row = pl.BlockSpec((tb, D), lambda i: (i, 0))
    one = pl.BlockSpec((1, D), lambda i: (0, 0))
    return pl.pallas_call(
        body, name="loss_head", grid=(T // tb,), in_specs=[row, one, row],
        out_specs=[row, pl.BlockSpec((1, 1), lambda i: (0, 0)), one],
        out_shape=[jax.ShapeDtypeStruct((T, D), F32), jax.ShapeDtypeStruct((1, 1), F32), jax.ShapeDtypeStruct((1, D), F32)],
        compiler_params=_cp("arbitrary"))(h2, w, target)


def _bd_lo(a, b, ca, cb):
    return lax.dot_general(a.astype(_BF), b.astype(_BF), (((ca,), (cb,)), ((0,), (0,))), preferred_element_type=F32)


def _split2(a):
    hi = a.astype(_BF)
    return hi, (a - hi.astype(F32)).astype(_BF)


def _bd_hi(a, b, ca, cb, exact_a=False):
    dn = (((ca,), (cb,)), ((0,), (0,)))
    dot = lambda x, y: lax.dot_general(x, y, dn, preferred_element_type=F32)
    bh, bl = _split2(b)
    if exact_a:
        ah = a.astype(_BF)
        return dot(ah, bh) + dot(ah, bl)
    ah, al = _split2(a)
    return dot(ah, bh) + (dot(ah, bl) + dot(al, bh))


def _row_to_col(x):
    eye = lax.broadcasted_iota(jnp.int32, (1, CH, CH), 1) == lax.broadcasted_iota(jnp.int32, (1, CH, CH), 2)
    return jnp.sum(jnp.where(eye, jnp.broadcast_to(x, (HEADS, CH, CH)), 0.0), axis=2, keepdims=True)


def _col_to_row(x):
    eye = lax.broadcasted_iota(jnp.int32, (1, CH, CH), 1) == lax.broadcasted_iota(jnp.int32, (1, CH, CH), 2)
    return jnp.sum(jnp.where(eye, jnp.broadcast_to(x, (HEADS, CH, CH)), 0.0), axis=1, keepdims=True)


def _gdn_chunk(q, k, v, bpre, apre, alog, dtb):
    H = HEADS
    r = lax.broadcasted_iota(jnp.int32, (1, CH, CH), 1)
    c = lax.broadcasted_iota(jnp.int32, (1, CH, CH), 2)
    tril, strict = r >= c, r > c
    lb = jnp.broadcast_to(tril.astype(F32), (H, CH, CH))
    rq = lax.rsqrt(jnp.sum(q * q, axis=-1, keepdims=True) + EPS)
    rk = lax.rsqrt(jnp.sum(k * k, axis=-1, keepdims=True) + EPS)
    qh = q * rq
    qn = qh * (HD ** -0.5)
    kn = k * rk
    beta = _sig(bpre)
    x = apre + dtb
    ea = jnp.exp(alog)
    g = -ea * jax.nn.softplus(x)
    gb = jnp.broadcast_to(g, (H, CH, HD))
    gc = _bd_hi(lb, gb, 2, 1, True)
    dm = gc - _col_to_row(gc[:, :, 0:1])
    decay = jnp.where(tril, jnp.exp(jnp.where(tril, dm, 0.0)), 0.0)
    eg = jnp.exp(gc)
    gl = gc[:, CH - 1:CH, :]
    egl = jnp.exp(gl - gc)
    cd = jnp.exp(gl)
    kb = kn * beta
    vb = v * beta
    both = _bd_lo(_rows2(kb, qn), kn, 2, 2)
    kk, qk = both[:, :CH], both[:, CH:]
    pw = -jnp.where(strict, kk * decay, 0.0)
    tm = jnp.where(r == c, 1.0, 0.0) + pw
    pw = _bd_hi(pw, pw, 2, 1)
    for it in range(4):
        mul = _bd_hi if it < 2 else _bd_lo
        both = mul(_rows2(pw, tm), pw, 2, 1)
        pw, tm = both[:, :CH], tm + both[:, CH:]
    tm = tm + _bd_lo(tm, pw, 2, 1)
    kbg = kb * eg
    sol = _bd_hi(tm, _lanes2(vb, kbg), 2, 1)
    value, kcd = sol[:, :, :HD], sol[:, :, HD:]
    attn = jnp.where(tril, qk * decay, 0.0)
    return dict(tril=tril, strict=strict, lb=lb, rq=rq, rk=rk, qh=qh, qn=qn, kn=kn, beta=beta, x=x, ea=ea, g=g,
                decay=decay, eg=eg, egl=egl, cd=cd, kb=kb, vb=vb, kk=kk, tm=tm, kbg=kbg, value=value, kcd=kcd, qk=qk,
                attn=attn, qd=qn * eg, kt=kn * egl, sol=sol)


def _rows2(a, b):
    return jnp.concatenate([a, b], axis=1)


def _lanes2(a, b):
    return jnp.concatenate([a, b], axis=2)


GDN_CPS = 2


def _gdn_specs(T, rev):
    G = GDN_CPS
    ns = T // (G * CH)
    pos = (lambda n: ns - 1 - n) if rev else (lambda n: n)
    mat = pl.BlockSpec((HEADS, G * CH, HD), lambda n: (0, pos(n), 0))
    col = pl.BlockSpec((G, HEADS, 1, CH), lambda n: (pos(n), 0, 0, 0))
    sca = pl.BlockSpec((HEADS, 1, 1), lambda n: (0, 0, 0))
    nw = pl.BlockSpec((1, 1, HD), lambda n: (0, 0, 0))
    st = pl.BlockSpec((G, HEADS, HD, HD), lambda n: (pos(n), 0, 0, 0))
    tok = lambda width, off: pl.BlockSpec((G * CH, width), lambda n: (pos(n), off))
    return ns, mat, col, sca, nw, st, tok


def _split_heads(x):
    return [jnp.stack([x[:, (g * HEADS + h) * HD:(g * HEADS + h + 1) * HD] for h in range(HEADS)])
            for g in range(x.shape[1] // FW)]


def _store_heads(ref, rows, g, val):
    for h in range(HEADS):
        ref[rows, (g * HEADS + h) * HD:(g * HEADS + h + 1) * HD] = val[h]


def _gdn_fwd(conv, rest, bpre, apre, alog, dtb, nw):
    T = conv.shape[0]
    ns, mat, col, sca, nws, st, tok = _gdn_specs(T, False)
    rows = [slice(g * CH, (g + 1) * CH) for g in range(GDN_CPS)]

    def body(c_ref, z_ref, b_ref, a_ref, al_ref, dt_ref, nw_ref, o_ref, og_ref, st_ref, s_scr):
        @pl.when(pl.program_id(0) == 0)
        def _():
            s_scr[...] = jnp.zeros_like(s_scr)

        locs = [_gdn_chunk(*_split_heads(c_ref[rows[g], :]), _row_to_col(b_ref[g]), _row_to_col(a_ref[g]), al_ref[...],
                           dt_ref[...]) for g in range(GDN_CPS)]
        s = s_scr[...]
        for g, L in enumerate(locs):
            st_ref[g] = s
            both = _bd_lo(_rows2(L["kcd"], L["qd"]), s, 2, 1)
            v_new = L["value"] - both[:, :CH]
            o = both[:, CH:] + _bd_lo(L["attn"], v_new, 2, 1)
            s = s * L["cd"] + _bd_lo(L["kt"], v_new, 1, 1)
            o_ref[:, rows[g], :] = o
            zz, = _split_heads(z_ref[rows[g], :])
            rs = lax.rsqrt(jnp.mean(o * o, axis=-1, keepdims=True) + EPS)
            _store_heads(og_ref, rows[g], 0, o * rs * nw_ref[...] * zz * _sig(zz))
        s_scr[...] = s

    return pl.pallas_call(
        body, name="gdn_fwd", grid=(ns,), in_specs=[tok(3 * FW, 0), tok(FW, OFF_Z // FW), col, col, sca, sca, nws],
        out_specs=[mat, tok(FW, 0), st],
        out_shape=[jax.ShapeDtypeStruct((HEADS, T, HD), F32), jax.ShapeDtypeStruct((T, FW), F32),
                   jax.ShapeDtypeStruct((T // CH, HEADS, HD, HD), F32)],
        scratch_shapes=[pltpu.VMEM((HEADS, HD, HD), F32)], compiler_params=_cp("arbitrary"))(
            conv, rest, bpre, apre, alog, dtb, nw)


def _gdn_bwd(conv, rest, bpre, apre, alog, dtb, nw, states, o, dog):
    T = conv.shape[0]
    ns, mat, col, sca, nws, st, tok = _gdn_specs(T, True)
    rows = [slice(g * CH, (g + 1) * CH) for g in range(GDN_CPS)]

    def body(c_ref, z_ref, b_ref, a_ref, al_ref, dt_ref, nw_ref, st_ref, o_ref, dog_ref,
             dc_ref, dz_ref, db_ref, da_ref, dal_ref, ddt_ref, dnw_ref, ds_scr):
        @pl.when(pl.program_id(0) == 0)
        def _():
            ds_scr[...] = jnp.zeros_like(ds_scr)
            dal_ref[...] = jnp.zeros_like(dal_ref)
            ddt_ref[...] = jnp.zeros_like(ddt_ref)
            dnw_ref[...] = jnp.zeros_like(dnw_ref)

        splits = [_split_heads(c_ref[rows[g], :]) for g in range(GDN_CPS)]
        locs = [_gdn_chunk(*splits[g], _row_to_col(b_ref[g]), _row_to_col(a_ref[g]), al_ref[...], dt_ref[...])
                for g in range(GDN_CPS)]
        dsn = ds_scr[...]
        acc_al, acc_dt, acc_nw = 0.0, 0.0, 0.0
        for g in reversed(range(GDN_CPS)):
            L, vv = locs[g], splits[g][2]
            tril, strict, lb = L["tril"], L["strict"], L["lb"]
            qn, kn, kb, beta, decay, eg, egl, cd = L["qn"], L["kn"], L["kb"], L["beta"], L["decay"], L["eg"], L["egl"], L["cd"]
            value, kcd, attn, qd, kt, tm = L["value"], L["kcd"], L["attn"], L["qd"], L["kt"], L["tm"]
            s = st_ref[g]
            v_new = value - _bd_lo(kcd, s, 2, 1)
            oo = o_ref[:, rows[g], :]
            zz, = _split_heads(z_ref[rows[g], :])
            dog_, = _split_heads(dog_ref[rows[g], :])
            sz = _sig(zz)
            rs = lax.rsqrt(jnp.mean(oo * oo, axis=-1, keepdims=True) + EPS)
            oh = oo * rs
            _store_heads(dz_ref, rows[g], 0, dog_ * oh * nw_ref[...] * sz * (1.0 + zz * (1.0 - sz)))
            don = dog_ * zz * sz
            gdy = don * nw_ref[...]
            do = rs * (gdy - oh * jnp.mean(oh * gdy, axis=-1, keepdims=True))
            acc_nw = acc_nw + jnp.sum(don * oh, axis=(0, 1), keepdims=True)
            d_vnew = _bd_lo(attn, do, 1, 1) + _bd_lo(kt, dsn, 2, 1)
            both = _bd_lo(_rows2(do, d_vnew), s, 2, 2)
            d_qd, d_kcd = both[:, :CH], -both[:, CH:]
            d_attn = jnp.where(tril, _bd_lo(do, v_new, 2, 2), 0.0)
            d_kt = _bd_lo(v_new, dsn, 2, 2)
            d_cd = jnp.sum(s * dsn, axis=(1, 2), keepdims=True)
            dsn_next = cd * dsn + _bd_lo(_rows2(qd, kcd), _rows2(do, -d_vnew), 1, 1)
            dsol = _bd_hi(tm, _lanes2(d_vnew, d_kcd), 1, 1)
            d_vb, d_kbg = dsol[:, :, :HD], dsol[:, :, HD:]
            da = -jnp.where(strict, _bd_lo(dsol, L["sol"], 2, 2), 0.0)
            dkk = da * decay
            dqk = d_attn * decay
            d_decay = da * L["kk"] + d_attn * L["qk"]
            both = _bd_lo(_rows2(dkk, dqk), kn, 2, 1)
            d_kb = both[:, :CH] + d_kbg * eg
            d_qn = both[:, CH:] + d_qd * eg
            d_kn = _bd_lo(_rows2(dkk, dqk), _rows2(kb, qn), 1, 1) + d_kt * egl
            dd = d_decay * decay
            d_gc = jnp.sum(dd + d_qd * qd + d_kbg * L["kbg"] - d_kt * kt, axis=-1, keepdims=True) \
                - _row_to_col(jnp.sum(dd, axis=1, keepdims=True))
            d_gl = jnp.sum(d_kt * kt, axis=(1, 2), keepdims=True) + d_cd * cd[:, :, 0:1]
            last = lax.broadcasted_iota(jnp.int32, (1, CH, 1), 1) == CH - 1
            d_gc = d_gc + jnp.where(last, d_gl, 0.0)
            dg = _bd_hi(lb, jnp.broadcast_to(d_gc, (HEADS, CH, HD)), 1, 1, True)[:, :, 0:1]
            d_apre = -dg * L["ea"] * _sig(L["x"])
            da_ref[g] = _col_to_row(d_apre)
            acc_al = acc_al + jnp.sum(dg * L["g"], axis=1, keepdims=True)
            acc_dt = acc_dt + jnp.sum(d_apre, axis=1, keepdims=True)
            d_beta = jnp.sum(d_kb * kn + d_vb * vv, axis=-1, keepdims=True)
            db_ref[g] = _col_to_row(d_beta * beta * (1.0 - beta))
            d_kn = d_kn + d_kb * beta
            qh = L["qh"]
            _store_heads(dc_ref, rows[g], 0, (HD ** -0.5) * L["rq"] * (d_qn - qh * jnp.sum(qh * d_qn, axis=-1, keepdims=True)))
            _store_heads(dc_ref, rows[g], 1, L["rk"] * (d_kn - kn * jnp.sum(kn * d_kn, axis=-1, keepdims=True)))
            _store_heads(dc_ref, rows[g], 2, d_vb * beta)
            dsn = dsn_next
        ds_scr[...] = dsn
        dal_ref[...] += acc_al
        ddt_ref[...] += acc_dt
        dnw_ref[...] += acc_nw

    c3 = jax.ShapeDtypeStruct((T // CH, HEADS, 1, CH), F32)
    s3 = jax.ShapeDtypeStruct((HEADS, 1, 1), F32)
    return pl.pallas_call(
        body, name="gdn_bwd", grid=(ns,),
        in_specs=[tok(3 * FW, 0), tok(FW, OFF_Z // FW), col, col, sca, sca, nws, st, mat, tok(FW, 0)],
        out_specs=[tok(3 * FW, 0), tok(FW, 0), col, col, sca, sca, nws],
        out_shape=[jax.ShapeDtypeStruct((T, 3 * FW), F32), jax.ShapeDtypeStruct((T, FW), F32), c3, c3, s3, s3,
                   jax.ShapeDtypeStruct((1, 1, HD), F32)],
        scratch_shapes=[pltpu.VMEM((HEADS, HD, HD), F32)], compiler_params=_cp("arbitrary"))(
            conv, rest, bpre, apre, alog, dtb, nw, states, o, dog)


def _local_step(x, target, w):
    seq = x.shape[0]
    T = -(-(X0 + seq) // T_ALIGN) * T_ALIGN
    back = T - X0 - seq
    tb_att = _div(T, ATT_TILE)
    h0 = jnp.concatenate([jnp.zeros((FRONT, D), F32), w["meta"], x, jnp.zeros((back, D), F32)], axis=0)
    tgt = jnp.concatenate([jnp.zeros((X0, D), F32), target, jnp.zeros((back, D), F32)], axis=0)
    row = lambda v: v.reshape(1, -1)
    nmix, nffn, nfin = row(w["norm_mix"]), row(w["norm_ffn"]), row(w["norm_final"])
    gate_b = row(w["gate_bias"])
    fconv_b = row(w["ffn_conv_b"])
    bias128 = jnp.zeros((1, SMALL_W), F32).at[0, :HEADS].set(w["fgt_bias"])
    alog = w["a_log"].reshape(HEADS, 1, 1)
    dtb = w["dt_bias"].reshape(HEADS, 1, 1)
    gnw = w["gdn_norm"].reshape(1, 1, HD)

    a = _rms_fwd(h0, nmix, "rms_mix")
    pf = _mm(a, w["w_fox"], out_dtype=_BF, name="proj_fox")
    rest = _mm(a, w["w_rest"], name="proj_rest")
    small = rest[:, OFF_SMALL:]
    c8 = _fgate_fwd(small, bias128)[:, :HEADS]
    c_cols, c_rows = _att_cols(c8), _att_rows(c8, tb_att)
    ones_blk = jnp.asarray(np.kron(np.eye(HEADS, dtype=np.float32), np.ones((HD, HD), np.float32)))
    plan = _att_plan(_att_stats(pf, c_cols, ones_blk))
    o_fox, lse = _fox_fwd(pf, c_cols, c_rows, plan)
    conv = _gconv_fwd(rest, w["gdn_conv"])
    chunk_rows = lambda a8: a8.reshape(T // CH, CH, HEADS).transpose(0, 2, 1)[:, :, None, :]
    bpre = chunk_rows(small[:, HEADS:2 * HEADS])
    apre = chunk_rows(small[:, 2 * HEADS:3 * HEADS])
    o_raw, og, states = _gdn_fwd(conv, rest, bpre, apre, alog, dtb, gnw)
    y_fox = _mm(o_fox, w["w_bfox"], name="y_fox")
    y_gdn = _mm(og, w["w_bgdn"], name="y_gdn")
    mix = _mix_fwd(rest, gate_b, y_fox, y_gdn)
    h1 = _mm(mix, w["w_out"], resid=h0, name="out_proj")
    b = _rms_fwd(h1, nffn, "rms_ffn")
    up = _mm(b, w["w_up"], name="ffn_up")
    f = _glu_fwd(up, w["ffn_conv"], fconv_b)
    h2 = _mm(f, w["w_down"], resid=h1, name="ffn_down")
    dh2, loss, d_nfin = _loss_head(h2, nfin, tgt, seq)

    d_f = _mm(dh2, w["w_down"], tb=True, name="d_f")
    g_down = _mm_t(f, dh2, "g_down")
    dug, duu, dwg, dwu, dbg, dbu = _glu_bwd_du(up, w["ffn_conv"], fconv_b, d_f)
    dxg = _conv_bwd_dx(dug, w["ffn_conv"][:, :DFF], _BF, "fconv_dx_gate")
    dxu = _conv_bwd_dx(duu, w["ffn_conv"][:, DFF:], _BF, "fconv_dx_up")
    d_b = _mm(dxg, w["w_up"][:, :DFF], tb=True, name="d_b_gate")
    d_b = _mm(dxu, w["w_up"][:, DFF:], tb=True, resid=d_b, name="d_b_up")
    g_up = jnp.concatenate([_mm_t(b, dxg, "g_up_gate"), _mm_t(b, dxu, "g_up_up")], axis=1)
    dh1, d_nffn = _rms_bwd(h1, nffn, d_b, dh2, "rms_ffn_bwd")

    dmix = _mm(dh1, w["w_out"], tb=True, name="d_mix")
    g_out = _mm_t(mix, dh1, "g_out")
    dyf, dyg, dgf, dgg, dgbf, dgbg = _mix_bwd(rest, gate_b, y_fox, y_gdn, dmix)
    do_fox = _mm(dyf, w["w_bfox"], tb=True, out_dtype=_BF, name="d_o_fox")
    g_bfox = _mm_t(o_fox, dyf, "g_bfox")
    d_og = _mm(dyg, w["w_bgdn"], tb=True, name="d_o_gdn")
    g_bgdn = _mm_t(og, dyg, "g_bgdn")

    delta = _head_dot(do_fox, o_fox, ones_blk)
    dq, dcq = _fox_bwd_dq(pf, do_fox, c_cols, c_rows, lse, delta, plan)
    dk, dv, dck = _fox_bwd_dkv(pf, do_fox, c_cols, c_rows, _att_rows(lse[:, ::HD], tb_att), _att_rows(delta[:, ::HD], tb_att),
                               plan)
    dc = jnp.pad((dcq + dck)[:, ::HD], ((0, 0), (0, SMALL_W - HEADS)))
    dfp, d_fb = _fgate_bwd(small, bias128, dc)

    dconv, dz, dbp, dap, d_alog, d_dtb, d_gnw = _gdn_bwd(conv, rest, bpre, apre, alog, dtb, gnw, states, o_raw, d_og)
    du_g, g_gconv = _gconv_bwd_du(rest, w["gdn_conv"], dconv)
    dgx = _conv_bwd_dx(du_g, w["gdn_conv"], F32, "gconv_dx")
    token_rows = lambda a: a[:, :, 0, :].transpose(0, 2, 1).reshape(T, HEADS)
    dsmall = jnp.concatenate([dfp[:, :HEADS], token_rows(dbp), token_rows(dap),
                              jnp.zeros((T, SMALL_W - 3 * HEADS), F32)], axis=1)
    drest = jnp.concatenate([dgx, dz, dgf, dgg, dsmall], axis=1)
    dfox = jnp.concatenate([dq, dk, dv], axis=1)
    d_a = _mm(dfox, w["w_fox"], tb=True, name="d_a_fox")
    d_a = _mm(drest, w["w_rest"], tb=True, resid=d_a, name="d_a_rest")
    g_fox = _mm_t(a, dfox, "g_w_fox")
    g_rest = _mm_t(a, drest, "g_w_rest")
    dh0, d_nmix = _rms_bwd(h0, nmix, d_a, dh1, "rms_mix_bwd")

    sm = lambda lo: g_rest[:, OFF_SMALL + lo:OFF_SMALL + lo + HEADS]
    g_w_in = jnp.concatenate([g_fox, sm(0), g_rest[:, :3 * FW], g_rest[:, OFF_Z:OFF_Z + FW], sm(HEADS), sm(2 * HEADS),
                              g_rest[:, OFF_GATES:OFF_GATES + 2 * D]], axis=1)
    grads = dict(
        meta_tokens=dh0[FRONT:X0], w_in=g_w_in, fgt_bias=d_fb[0, :HEADS], gdn_conv_w=g_gconv,
        gdn_a_log=d_alog.reshape(HEADS), gdn_dt_bias=d_dtb.reshape(HEADS), gdn_norm_w=d_gnw.reshape(HD),
        gate_bias=jnp.concatenate([dgbf, dgbg], axis=1).reshape(2 * D), w_branch_fox=g_bfox, w_branch_gdn=g_bgdn,
        w_out=g_out, norm_mix_w=d_nmix.reshape(D), norm_ffn_w=d_nffn.reshape(D), ffn_w_up=g_up,
        ffn_conv_w=jnp.concatenate([dwg, dwu], axis=1), ffn_conv_b=jnp.concatenate([dbg, dbu], axis=1).reshape(2 * DFF),
        ffn_w_down=g_down, norm_final_w=d_nfin.reshape(D))
    return loss, dh0[X0:X0 + seq], grads


N_CHIPS = 4
PACK_W = 1024
PACK_ROW_ALIGN = 32
BIG = ("w_in", "w_branch_fox", "w_branch_gdn", "w_out", "ffn_w_up", "ffn_w_down")
WEIGHTS = (
    ("meta_tokens", (N_META, D), 1), ("w_in", (D, 3 * FW + HEADS + 4 * FW + 2 * HEADS + 2 * D), 1), ("fgt_bias", (1, HEADS), None),
    ("gdn_conv_w", (4, 3 * FW), 1), ("gdn_a_log", (1, HEADS), None), ("gdn_dt_bias", (1, HEADS), None),
    ("gdn_norm_w", (1, HD), None), ("gate_bias", (1, 2 * D), None), ("w_branch_fox", (FW, D), 1),
    ("w_branch_gdn", (FW, D), 1), ("w_out", (D, D), 0), ("norm_mix_w", (1, D), None), ("norm_ffn_w", (1, D), None),
    ("ffn_w_up", (D, 2 * DFF), 1), ("ffn_conv_w", (3, 2 * DFF), 1), ("ffn_conv_b", (1, 2 * DFF), None),
    ("ffn_w_down", (DFF, D), 0), ("norm_final_w", (1, D), None))
SPLIT_F32 = ("meta_tokens", "gdn_conv_w", "ffn_conv_w")


def _shard_shape(shape, axis):
    if axis is None:
        return shape
    return tuple(s // N_CHIPS if a == axis else s for a, s in enumerate(shape))


def _shard_of(full, axis, q):
    if axis is None:
        return full
    n = full.shape[axis] // N_CHIPS
    return lax.slice_in_dim(full, q * n, (q + 1) * n, axis=axis)


def _pack_rows(n_elems):
    rows = -(-n_elems // PACK_W)
    return -(-rows // PACK_ROW_ALIGN) * PACK_ROW_ALIGN


def _pack(pieces, dtype):
    flat = jnp.concatenate([p.reshape(-1).astype(dtype) for p in pieces])
    rows = _pack_rows(flat.shape[0])
    return jnp.pad(flat, (0, rows * PACK_W - flat.shape[0])).reshape(rows, PACK_W)


def _unpack(slab, shapes):
    flat = slab.reshape(-1)
    out, off = [], 0
    for s in shapes:
        n = int(np.prod(s))
        out.append(flat[off:off + n].reshape(s))
        off += n
    return out


HBM_SPEC = pl.BlockSpec(memory_space=pltpu.HBM)
MESH_ID = pl.DeviceIdType.MESH


def _scatter_chips(srcs, name):
    n = len(srcs)

    def body(*refs):
        src_refs, out_refs = refs[:n], refs[n:2 * n]
        send_sems, recv_sems, local_sems = refs[2 * n:]
        x, y, c = lax.axis_index("x"), lax.axis_index("y"), lax.axis_index("c")
        q = 2 * x + y
        peers = [(1 - x, y), (x, 1 - y), (1 - x, 1 - y)]

        def remote(a, k, src_slot, dst_slot):
            px, py = peers[k]
            return pltpu.make_async_remote_copy(
                src_ref=src_refs[a].at[src_slot], dst_ref=out_refs[a].at[dst_slot], send_sem=send_sems.at[3 * a + k],
                recv_sem=recv_sems.at[3 * a + k], device_id=(px, py, c), device_id_type=MESH_ID)

        mine = [pltpu.make_async_copy(src_refs[a].at[q], out_refs[a].at[q], local_sems.at[a]) for a in range(n)]
        for cp in mine:
            cp.start()
        sends = [remote(a, k, 2 * px + py, q) for a in range(n) for k, (px, py) in enumerate(peers)]
        for cp in sends:
            cp.start()
        for a in range(n):
            for k, (px, py) in enumerate(peers):
                remote(a, k, 0, 2 * px + py).wait_recv()
        for cp in sends:
            cp.wait_send()
        for cp in mine:
            cp.wait()

    out_shape = [jax.ShapeDtypeStruct((N_CHIPS,) + s.shape[-2:], s.dtype) for s in srcs]
    return pl.pallas_call(
        body, name=name, in_specs=[HBM_SPEC] * n, out_specs=[HBM_SPEC] * n, out_shape=out_shape,
        scratch_shapes=[pltpu.SemaphoreType.DMA((3 * n,)), pltpu.SemaphoreType.DMA((3 * n,)), pltpu.SemaphoreType.DMA((n,))],
    )(*srcs)


def _gather_chips(srcs, name):
    n = len(srcs)

    def body(*refs):
        src_refs, out_refs = refs[:n], refs[n:2 * n]
        ici_send, ici_recv, d2d_send, d2d_recv, local_sems = refs[2 * n:]
        x, y, c = lax.axis_index("x"), lax.axis_index("y"), lax.axis_index("c")
        q = 2 * x + y
        peers = [(1 - x, y), (x, 1 - y), (1 - x, 1 - y)]

        def half(a, which):
            r = srcs[a].shape[0] // 2
            return pl.ds(which * r, r)

        def ici(a, k, slot):
            px, py = peers[k]
            return pltpu.make_async_remote_copy(
                src_ref=src_refs[a].at[half(a, c)], dst_ref=out_refs[a].at[slot, half(a, c)], send_sem=ici_send.at[3 * a + k],
                recv_sem=ici_recv.at[3 * a + k], device_id=(px, py, c), device_id_type=MESH_ID)

        def d2d(a, k, which):
            px, py = peers[k]
            rows = out_refs[a].at[2 * px + py, half(a, which)]
            return pltpu.make_async_remote_copy(
                src_ref=rows, dst_ref=rows, send_sem=d2d_send.at[3 * a + k], recv_sem=d2d_recv.at[3 * a + k],
                device_id=(x, y, 1 - c), device_id_type=MESH_ID)

        mine = [pltpu.make_async_copy(src_refs[a], out_refs[a].at[q], local_sems.at[a]) for a in range(n)]
        for cp in mine:
            cp.start()
        sends = [ici(a, k, q) for a in range(n) for k in range(3)]
        for cp in sends:
            cp.start()
        passed = []
        for a in range(n):
            for k, (px, py) in enumerate(peers):
                ici(a, k, 2 * px + py).wait_recv()
                passed.append(d2d(a, k, c))
                passed[-1].start()
        for a in range(n):
            for k in range(3):
                d2d(a, k, 1 - c).wait_recv()
        for cp in sends + passed:
            cp.wait_send()
        for cp in mine:
            cp.wait()

    out_shape = [jax.ShapeDtypeStruct((N_CHIPS,) + s.shape, s.dtype) for s in srcs]
    sems = [pltpu.SemaphoreType.DMA((3 * n,))] * 4 + [pltpu.SemaphoreType.DMA((n,))]
    return pl.pallas_call(body, name=name, in_specs=[HBM_SPEC] * n, out_specs=[HBM_SPEC] * n, out_shape=out_shape,
                          scratch_shapes=sems)(*srcs)


def _sibling_swap(slabs, name):
    n = len(slabs)

    def body(*refs):
        src_refs, out_refs, send_sems, recv_sems = refs[:n], refs[n:2 * n], refs[2 * n], refs[2 * n + 1]
        x, y, c = lax.axis_index("x"), lax.axis_index("y"), lax.axis_index("c")
        cps = [pltpu.make_async_remote_copy(src_ref=src_refs[a], dst_ref=out_refs[a], send_sem=send_sems.at[a],
                                            recv_sem=recv_sems.at[a], device_id=(x, y, 1 - c), device_id_type=MESH_ID)
               for a in range(n)]
        for cp in cps:
            cp.start()
        for cp in cps:
            cp.wait_recv()
        for cp in cps:
            cp.wait_send()

    return pl.pallas_call(
        body, name=name, in_specs=[HBM_SPEC] * n, out_specs=[HBM_SPEC] * n,
        out_shape=[jax.ShapeDtypeStruct(s.shape, s.dtype) for s in slabs],
        scratch_shapes=[pltpu.SemaphoreType.DMA((n,)), pltpu.SemaphoreType.DMA((n,))])(*slabs)


def _sum_chips(r, name):
    rows, cols = r.shape[1:]
    tb = _div(rows, 256, 16)

    def body(r0, r1, r2, r3, o_ref):
        o_ref[...] = ((r0[0].astype(F32) + r1[0].astype(F32)) + r2[0].astype(F32)) + r3[0].astype(F32)

    spec = lambda j: pl.BlockSpec((1, tb, cols), lambda i: (j, i, 0))
    return pl.pallas_call(
        body, name=name, grid=(rows // tb,), in_specs=[spec(0), spec(1), spec(2), spec(3)],
        out_specs=pl.BlockSpec((tb, cols), lambda i: (i, 0)), out_shape=jax.ShapeDtypeStruct((rows, cols), F32),
        compiler_params=_cp("parallel"))(r, r, r, r)


def _adamw(w, m, v, p, q, name):
    rows, cols = w.shape
    tb = _div(rows, 256, 8)

    def body(w_ref, m_ref, v_ref, p_ref, q_ref, g_ref, d_ref, nm_ref, nv_ref):
        g = p_ref[...] + q_ref[...]
        m_new = B1 * m_ref[...] + (1.0 - B1) * g
        v_new = B2 * v_ref[...] + (1.0 - B2) * (g * g)
        g_ref[...] = g
        nm_ref[...] = m_new
        nv_ref[...] = v_new
        m_hat = m_new / (1.0 - B1 ** STEP)
        v_hat = v_new / (1.0 - B2 ** STEP)
        d_ref[...] = -LR * (m_hat / (jnp.sqrt(v_hat) + AEPS) + WD * w_ref[...])

    spec = pl.BlockSpec((tb, cols), lambda i: (i, 0))
    return pl.pallas_call(
        body, name=name, grid=(rows // tb,), in_specs=[spec] * 5, out_specs=[spec] * 4,
        out_shape=[jax.ShapeDtypeStruct((rows, cols), F32)] * 4, compiler_params=_cp("parallel"))(w, m, v, p, q)


def _split_w_in(w_in):
    o1 = 3 * FW
    o2 = o1 + HEADS
    o3 = o2 + 3 * FW
    o4 = o3 + FW
    o5 = o4 + HEADS
    o6 = o5 + HEADS
    pad = jnp.zeros((w_in.shape[0], SMALL_W - 3 * HEADS), w_in.dtype)
    rest = jnp.concatenate([w_in[:, o2:o3], w_in[:, o3:o4], w_in[:, o6:], w_in[:, o1:o2], w_in[:, o4:o5], w_in[:, o5:o6], pad],
                           axis=1)
    return w_in[:, :o1], rest


AXIS = {n: a for n, _, a in WEIGHTS}
SMALL = tuple(n for n, _, _ in WEIGHTS if n not in BIG)


def _by_chip(full, axis):
    rows, cols = full.shape
    if axis == 0:
        return full.reshape(N_CHIPS, rows // N_CHIPS, cols)
    return full.reshape(rows, N_CHIPS, cols // N_CHIPS).transpose(1, 0, 2)


def _from_chips(parts, axis):
    _, r, c = parts.shape
    if axis == 0:
        return parts.reshape(N_CHIPS * r, c)
    return parts.transpose(1, 0, 2).reshape(r, N_CHIPS * c)


def _gather_weights(shards):
    hi = {n: shards[n].astype(jnp.bfloat16) for n in SPLIT_F32}
    lo = [(shards[n] - hi[n].astype(F32)).astype(jnp.bfloat16) for n in SPLIT_F32]
    slab = _pack([hi[n] for n in SPLIT_F32] + lo, jnp.bfloat16)
    got = _gather_chips([shards[n].astype(jnp.bfloat16) for n in BIG] + [slab], "gather_weights")
    full = {n: _from_chips(g, AXIS[n]) for n, g in zip(BIG, got)}
    shapes = [shards[n].shape for n in SPLIT_F32] * 2
    per_chip = [_unpack(got[-1][j], shapes) for j in range(N_CHIPS)]
    for i, n in enumerate(SPLIT_F32):
        join = lambda off: jnp.concatenate([per_chip[j][off + i] for j in range(N_CHIPS)], axis=1).astype(F32)
        full[n] = join(0) + join(len(SPLIT_F32))
    return full


def kernel(x, meta_tokens, w_in, fgt_bias, gdn_conv_w, gdn_a_log, gdn_dt_bias, gdn_norm_w, gate_bias, w_branch_fox, w_branch_gdn, w_out, norm_mix_w, norm_ffn_w, ffn_w_up, ffn_conv_w, ffn_conv_b, ffn_w_down, norm_final_w, loss_target, m_meta_tokens, m_w_in, m_fgt_bias, m_gdn_conv_w, m_gdn_a_log, m_gdn_dt_bias, m_gdn_norm_w, m_gate_bias, m_w_branch_fox, m_w_branch_gdn, m_w_out, m_norm_mix_w, m_norm_ffn_w, m_ffn_w_up, m_ffn_conv_w, m_ffn_conv_b, m_ffn_w_down, m_norm_final_w, v_meta_tokens, v_w_in, v_fgt_bias, v_gdn_conv_w, v_gdn_a_log, v_gdn_dt_bias, v_gdn_norm_w, v_gate_bias, v_w_branch_fox, v_w_branch_gdn, v_w_out, v_norm_mix_w, v_norm_ffn_w, v_ffn_w_up, v_ffn_conv_w, v_ffn_conv_b, v_ffn_w_down, v_norm_final_w):
    weights = dict(meta_tokens=meta_tokens, w_in=w_in, fgt_bias=fgt_bias, gdn_conv_w=gdn_conv_w, gdn_a_log=gdn_a_log, gdn_dt_bias=gdn_dt_bias, gdn_norm_w=gdn_norm_w, gate_bias=gate_bias, w_branch_fox=w_branch_fox, w_branch_gdn=w_branch_gdn, w_out=w_out, norm_mix_w=norm_mix_w, norm_ffn_w=norm_ffn_w, ffn_w_up=ffn_w_up, ffn_conv_w=ffn_conv_w, ffn_conv_b=ffn_conv_b, ffn_w_down=ffn_w_down, norm_final_w=norm_final_w)
    m_in = dict(meta_tokens=m_meta_tokens, w_in=m_w_in, fgt_bias=m_fgt_bias, gdn_conv_w=m_gdn_conv_w, gdn_a_log=m_gdn_a_log, gdn_dt_bias=m_gdn_dt_bias, gdn_norm_w=m_gdn_norm_w, gate_bias=m_gate_bias, w_branch_fox=m_w_branch_fox, w_branch_gdn=m_w_branch_gdn, w_out=m_w_out, norm_mix_w=m_norm_mix_w, norm_ffn_w=m_norm_ffn_w, ffn_w_up=m_ffn_w_up, ffn_conv_w=m_ffn_conv_w, ffn_conv_b=m_ffn_conv_b, ffn_w_down=m_ffn_w_down, norm_final_w=m_norm_final_w)
    v_in = dict(meta_tokens=v_meta_tokens, w_in=v_w_in, fgt_bias=v_fgt_bias, gdn_conv_w=v_gdn_conv_w, gdn_a_log=v_gdn_a_log, gdn_dt_bias=v_gdn_dt_bias, gdn_norm_w=v_gdn_norm_w, gate_bias=v_gate_bias, w_branch_fox=v_w_branch_fox, w_branch_gdn=v_w_branch_gdn, w_out=v_w_out, norm_mix_w=v_norm_mix_w, norm_ffn_w=v_norm_ffn_w, ffn_w_up=v_ffn_w_up, ffn_conv_w=v_ffn_conv_w, ffn_conv_b=v_ffn_conv_b, ffn_w_down=v_ffn_w_down, norm_final_w=v_norm_final_w)
    shard2d = {n: _shard_shape(s, a) for n, s, a in WEIGHTS}
    as2d = lambda d: {n: d[n].reshape(shard2d[n]) for n, _, _ in WEIGHTS}
    w2, m2, v2 = as2d(weights), as2d(m_in), as2d(v_in)

    full = _gather_weights(w2)
    w_fox, w_rest = _split_w_in(full["w_in"])
    flat = lambda n: w2[n].reshape(-1)
    local_w = dict(
        meta=full["meta_tokens"], w_fox=w_fox, w_rest=w_rest, fgt_bias=flat("fgt_bias"), gdn_conv=full["gdn_conv_w"],
        a_log=flat("gdn_a_log"), dt_bias=flat("gdn_dt_bias"), gdn_norm=flat("gdn_norm_w"), gate_bias=flat("gate_bias"),
        w_bfox=full["w_branch_fox"], w_bgdn=full["w_branch_gdn"], w_out=full["w_out"], norm_mix=flat("norm_mix_w"),
        norm_ffn=flat("norm_ffn_w"), w_up=full["ffn_w_up"], ffn_conv=full["ffn_conv_w"], ffn_conv_b=flat("ffn_conv_b"),
        w_down=full["ffn_w_down"], norm_final=flat("norm_final_w"))

    loss, grad_x, grads = _local_step(x[0], loss_target[0], local_w)

    g2 = {n: grads[n].reshape(s) for n, s, _ in WEIGHTS}
    send = [_by_chip(g2[n].astype(jnp.bfloat16), AXIS[n]) for n in BIG]
    send.append(jnp.stack([_pack([_shard_of(g2[n], AXIS[n], j) for n in SMALL], F32) for j in range(N_CHIPS)]))
    parts = [_sum_chips(r, "sum_" + n) for r, n in zip(_scatter_chips(send, "scatter_grads"), BIG + ("small",))]
    others = _sibling_swap(parts, "swap_grads")
    slab = lambda d: _pack([d[n] for n in SMALL], F32)
    state = [(w2[n], m2[n], v2[n]) for n in BIG] + [(slab(w2), slab(m2), slab(v2))]
    outs = [_adamw(w, m, v, p, q, "adamw_" + n) for (w, m, v), p, q, n in zip(state, parts, others, BIG + ("small",))]
    small = [_unpack(o, [weights[n].shape for n in SMALL]) for o in outs[-1]]
    result = []
    for kind in range(4):
        by_name = {n: outs[i][kind].reshape(weights[n].shape) for i, n in enumerate(BIG)}
        by_name.update(zip(SMALL, small[kind]))
        result += [by_name[n] for n, _, _ in WEIGHTS]
    total = lax.psum(loss[0, 0], ("x", "y", "c"))
    return (total, grad_x[None], *result)
```

```python
import functools

import numpy as np
import jax
import jax.numpy as jnp
from jax import lax
from jax.experimental import pallas as pl
from jax.experimental.pallas import tpu as pltpu

F32 = jnp.float32
_BF = jnp.bfloat16
HI = lax.Precision.HIGHEST

D = 1024
N_META = 16
CH = 64
FRONT = CH - N_META
X0 = CH
HEADS = 8
HD = 64
FW = HEADS * HD
DFF = 2816
EPS = 1e-6
NEG = -1e30
T_ALIGN = 256
SMALL_W = 128
REST_W = 3 * FW + FW + 2 * D + SMALL_W
OFF_Z = 3 * FW
OFF_GATES = 4 * FW
OFF_SMALL = 4 * FW + 2 * D

LR, B1, B2, AEPS, WD, STEP = 0.001, 0.9, 0.999, 1e-08, 0.01, 10

VMEM_LIMIT = 56 * 1024 * 1024
ROW_TILE = 640
MM_TM, MM_TN, MM_TK = 1280, 512, 2816
ATT_TILE = 640


def _div(n, target, mult=128):
    if n <= target:
        return n
    best = None
    for d in range(mult, target + 1, mult):
        if n % d == 0:
            best = d
    assert best is not None, (n, target, mult)
    return best


def _cp(*sem):
    return pltpu.CompilerParams(dimension_semantics=sem, vmem_limit_bytes=VMEM_LIMIT)


def _sig(x):
    return 1.0 / (1.0 + jnp.exp(-x))


MM_VMEM_BUDGET = 40 * 1024 * 1024
MM_STEP_BYTES = 1 << 20


def _mm_tiles(m, n, k, sa, sb, so, has_resid):
    divs = lambda d, cap: [t for t in range(128, min(d, cap) + 1, 128) if d % t == 0] or [d]
    best = None
    for tm in divs(m, MM_TM * 2):
        for tn in divs(n, 4608):
            fixed = tm * tn * (4 + 2 * so + (8 if has_resid else 0))
            tks = [t for t in divs(k, MM_TK) if 2 * t * (tm * sa + tn * sb) + fixed <= MM_VMEM_BUDGET]
            if not tks:
                continue
            tk = tks[-1]
            steps = (m // tm) * (n // tn) * (k // tk)
            rmw = (k // tk - 1) * m * n * 4
            cost = (n // tn) * m * k * sa + (m // tm) * k * n * sb + steps * MM_STEP_BYTES + rmw
            if best is None or cost < best[0]:
                best = (cost, tm, tn, tk)
    assert best is not None, (m, n, k)
    return best[1:]


def _mm_t(a, b, name):
    return _mm(a, b, ta=True, name=name)


def _mm(a, b, *, ta=False, tb=False, out_dtype=F32, resid=None, name):
    K, M = a.shape if ta else a.shape[::-1]
    N = b.shape[0] if tb else b.shape[1]
    assert (b.shape[1] if tb else b.shape[0]) == K
    tm, tn, tk = _mm_tiles(M, N, K, a.dtype.itemsize, b.dtype.itemsize, jnp.dtype(out_dtype).itemsize, resid is not None)
    nk = K // tk
    dims = (((0 if ta else 1,), (1 if tb else 0,)), ((), ()))
    mxu = _BF

    def body(*refs):
        if resid is None:
            a_ref, b_ref, o_ref, acc = refs
            r_ref = None
        else:
            a_ref, b_ref, r_ref, o_ref, acc = refs
        k = pl.program_id(2)
        part = lax.dot_general(a_ref[...].astype(mxu), b_ref[...].astype(mxu), dims, preferred_element_type=F32)

        @pl.when(k == 0)
        def _():
            acc[...] = part

        @pl.when(k > 0)
        def _():
            acc[...] += part

        @pl.when(k == nk - 1)
        def _():
            r = acc[...]
            if r_ref is not None:
                r = r + r_ref[...]
            o_ref[...] = r.astype(o_ref.dtype)

    a_spec = pl.BlockSpec((tk, tm), lambda i, j, k: (k, i)) if ta else pl.BlockSpec((tm, tk), lambda i, j, k: (i, k))
    b_spec = pl.BlockSpec((tn, tk), lambda i, j, k: (j, k)) if tb else pl.BlockSpec((tk, tn), lambda i, j, k: (k, j))
    o_spec = pl.BlockSpec((tm, tn), lambda i, j, k: (i, j))
    in_specs = [a_spec, b_spec] + ([o_spec] if resid is not None else [])
    args = (a, b) + ((resid,) if resid is not None else ())
    return pl.pallas_call(
        body, name=name, grid=(M // tm, N // tn, nk), in_specs=in_specs, out_specs=o_spec,
        out_shape=jax.ShapeDtypeStruct((M, N), out_dtype), scratch_shapes=[pltpu.VMEM((tm, tn), F32)],
        compiler_params=_cp("parallel", "parallel", "arbitrary"))(*args)


def _rms_fwd(h, w, name):
    T = h.shape[0]
    tb = _div(T, ROW_TILE, 8)

    def body(h_ref, w_ref, o_ref):
        x = h_ref[...]
        r = lax.rsqrt(jnp.mean(x * x, axis=-1, keepdims=True) + EPS)
        o_ref[...] = (x * r * w_ref[...]).astype(o_ref.dtype)

    return pl.pallas_call(
        body, name=name, grid=(T // tb,),
        in_specs=[pl.BlockSpec((tb, D), lambda i: (i, 0)), pl.BlockSpec((1, D), lambda i: (0, 0))],
        out_specs=pl.BlockSpec((tb, D), lambda i: (i, 0)), out_shape=jax.ShapeDtypeStruct((T, D), _BF),
        compiler_params=_cp("parallel"))(h, w)


def _rms_bwd(h, w, dy, dres, name):
    T = h.shape[0]
    tb = _div(T, ROW_TILE, 8)

    def body(h_ref, w_ref, dy_ref, dr_ref, dh_ref, dw_ref):
        x = h_ref[...]
        r = lax.rsqrt(jnp.mean(x * x, axis=-1, keepdims=True) + EPS)
        xh = x * r
        dy = dy_ref[...]
        g = dy * w_ref[...]
        dh_ref[...] = dr_ref[...] + r * (g - xh * jnp.mean(xh * g, axis=-1, keepdims=True))
        part = jnp.sum(dy * xh, axis=0, keepdims=True)

        @pl.when(pl.program_id(0) == 0)
        def _():
            dw_ref[...] = part

        @pl.when(pl.program_id(0) > 0)
        def _():
            dw_ref[...] += part

    row = pl.BlockSpec((tb, D), lambda i: (i, 0))
    one = pl.BlockSpec((1, D), lambda i: (0, 0))
    return pl.pallas_call(
        body, name=name, grid=(T // tb,), in_specs=[row, one, row, row], out_specs=[row, one],
        out_shape=[jax.ShapeDtypeStruct((T, D), F32), jax.ShapeDtypeStruct((1, D), F32)],
        compiler_params=_cp("arbitrary"))(h, w, dy, dres)


def _fgate_fwd(small, bias):
    T = small.shape[0]
    tb = _div(T, ROW_TILE, 8)

    def body(s_ref, b_ref, c_ref, carry):
        @pl.when(pl.program_id(0) == 0)
        def _():
            carry[...] = jnp.zeros_like(carry)

        lf = jax.nn.log_sigmoid(s_ref[...] + b_ref[...])
        r = lax.broadcasted_iota(jnp.int32, (tb, tb), 0)
        c = lax.broadcasted_iota(jnp.int32, (tb, tb), 1)
        tri = (r >= c).astype(F32)
        cs = jnp.dot(tri, lf, precision=HI, preferred_element_type=F32) + carry[...]
        c_ref[...] = cs
        carry[...] = cs[tb - 1:tb, :]

    return pl.pallas_call(
        body, name="fgate_fwd", grid=(T // tb,),
        in_specs=[pl.BlockSpec((tb, SMALL_W), lambda i: (i, 0)), pl.BlockSpec((1, SMALL_W), lambda i: (0, 0))],
        out_specs=pl.BlockSpec((tb, SMALL_W), lambda i: (i, 0)), out_shape=jax.ShapeDtypeStruct((T, SMALL_W), F32),
        scratch_shapes=[pltpu.VMEM((1, SMALL_W), F32)], compiler_params=_cp("arbitrary"))(small, bias)


def _fgate_bwd(small, bias, dc):
    T = small.shape[0]
    tb = _div(T, ROW_TILE, 8)
    nb = T // tb

    def body(s_ref, b_ref, dc_ref, df_ref, db_ref, carry):
        @pl.when(pl.program_id(0) == 0)
        def _():
            carry[...] = jnp.zeros_like(carry)

        r = lax.broadcasted_iota(jnp.int32, (tb, tb), 0)
        c = lax.broadcasted_iota(jnp.int32, (tb, tb), 1)
        tri = (r <= c).astype(F32)
        dlf = jnp.dot(tri, dc_ref[...], precision=HI, preferred_element_type=F32) + carry[...]
        carry[...] = dlf[0:1, :]
        df = dlf * _sig(-(s_ref[...] + b_ref[...]))
        df_ref[...] = df
        part = jnp.sum(df, axis=0, keepdims=True)

        @pl.when(pl.program_id(0) == 0)
        def _():
            db_ref[...] = part

        @pl.when(pl.program_id(0) > 0)
        def _():
            db_ref[...] += part

    rev = pl.BlockSpec((tb, SMALL_W), lambda i: (nb - 1 - i, 0))
    one = pl.BlockSpec((1, SMALL_W), lambda i: (0, 0))
    return pl.pallas_call(
        body, name="fgate_bwd", grid=(nb,), in_specs=[rev, one, rev], out_specs=[rev, one],
        out_shape=[jax.ShapeDtypeStruct((T, SMALL_W), F32), jax.ShapeDtypeStruct((1, SMALL_W), F32)],
        scratch_shapes=[pltpu.VMEM((1, SMALL_W), F32)], compiler_params=_cp("arbitrary"))(small, bias, dc)


def _att_rows(a8, tb):
    T = a8.shape[0]
    return a8.T.reshape(HEADS // 2, 2, T // tb, tb).transpose(0, 2, 1, 3)


def _att_cols(a8):
    return jnp.repeat(a8, HD, axis=1)


EXP_ZERO = -104.0
SKIP_SLACK = 2.0


def _att_stats(qkv, c_cols, ones_blk):
    T = qkv.shape[0]
    tb = _div(T, ATT_TILE)

    def body(q_ref, k_ref, c_ref, e_ref, o_ref):
        q = q_ref[...].astype(F32)
        k = k_ref[...].astype(F32)
        qn = jnp.max(jnp.dot(q * q, e_ref[...], precision=HI, preferred_element_type=F32), axis=0, keepdims=True)
        kn = jnp.max(jnp.dot(k * k, e_ref[...], precision=HI, preferred_element_type=F32), axis=0, keepdims=True)
        c = c_ref[...]
        o_ref[0] = jnp.concatenate([jnp.sqrt(qn), jnp.sqrt(kn), jnp.max(c, axis=0, keepdims=True),
                                    jnp.min(c, axis=0, keepdims=True), jnp.zeros((4, FW), F32)], axis=0)

    blk = lambda off: pl.BlockSpec((tb, FW), lambda i: (i, off))
    return pl.pallas_call(
        body, name="att_stats", grid=(T // tb,),
        in_specs=[blk(0), blk(1), blk(0), pl.BlockSpec((FW, FW), lambda i: (0, 0))],
        out_specs=pl.BlockSpec((1, 8, FW), lambda i: (i, 0, 0)), out_shape=jax.ShapeDtypeStruct((T // tb, 8, FW), F32),
        compiler_params=_cp("parallel"))(qkv, qkv, c_cols, ones_blk)


def _att_plan(stats):
    nb = stats.shape[0]
    st = stats[:, :4, ::HD]
    qmax, kmax, cmax, cmin = (st[:, r, :].T for r in range(4))
    bound = (HD ** -0.5) * 1.01 * qmax[:, :, None] * (kmax[:, None, :] + kmax[:, :, None]) \
        + cmax[:, :, None] - cmin[:, None, :] + SKIP_SLACK
    ii = lax.broadcasted_iota(jnp.int32, (nb, nb), 0)
    jj = lax.broadcasted_iota(jnp.int32, (nb, nb), 1)
    skip = (bound < EXP_ZERO) & (jj < ii)[None]
    live = (~skip & (jj <= ii)[None]).reshape(HEADS // 2, 2, nb, nb).any(axis=1)
    jfirst = jnp.argmax(live, axis=2).astype(jnp.int32)
    ilast = (nb - 1 - jnp.argmax(live[:, ::-1, :], axis=1)).astype(jnp.int32)
    return skip.astype(jnp.int32).reshape(-1), jfirst.reshape(-1), ilast.reshape(-1)


def _fox_fwd(qkv, c_cols, c_rows, plan):
    T = qkv.shape[0]
    tb = _div(T, ATT_TILE)
    nb = T // tb
    npair = HEADS // 2
    scale = HD ** -0.5
    dn = (((1,), (1,)), ((), ()))

    def body(skip_ref, jfirst_ref, ilast_ref, q_ref, k_ref, v_ref, cq_ref, ck_ref, o_ref, l_ref, m_scr, acc_scr):
        pr = pl.program_id(0)
        i = pl.program_id(1)
        q = q_ref[...]
        lane = lax.broadcasted_iota(jnp.int32, (1, 2 * HD), 1)
        sel0 = lane < HD
        zero = jnp.zeros_like(q)
        qh = (jnp.where(sel0, q, zero) * scale, jnp.where(sel0, zero, q) * scale)
        cq = (cq_ref[:, 0:1], cq_ref[:, HD:HD + 1])
        row = i * tb + lax.broadcasted_iota(jnp.int32, (tb, 1), 0)
        m_scr[...] = jnp.full(m_scr.shape, NEG, F32)
        acc_scr[...] = jnp.zeros(acc_scr.shape, F32)

        def tile(j, h, masked):
            start = pl.multiple_of(j * tb, tb)
            kb = k_ref[pl.ds(start, tb), :]
            vb = v_ref[pl.ds(start, tb), :]
            one = jnp.ones_like(vb)
            vh = jnp.where(sel0, vb, one) if h == 0 else jnp.where(sel0, one, vb)
            t = lax.dot_general(qh[h], kb, dn, preferred_element_type=F32) - ck_ref[0, j, h:h + 1, :]
            if masked:
                col = j * tb + lax.broadcasted_iota(jnp.int32, (1, tb), 1)
                mask = (col <= row) & (col >= FRONT)
                t = jnp.where(mask, t, NEG)
            m = m_scr[h]
            m_new = jnp.maximum(m, jnp.max(t, axis=-1, keepdims=True) + cq[h])
            p = jnp.exp(t + (cq[h] - m_new))
            if masked:
                p = jnp.where(mask, p, 0.0)
            m_scr[h] = m_new
            acc_scr[h] = acc_scr[h] * jnp.exp(m - m_new) + jnp.dot(p.astype(vb.dtype), vh, preferred_element_type=F32)

        def step(j, _):
            edge = (j == 0) | (j == i)
            for h in range(2):
                live = skip_ref[((2 * pr + h) * nb + i) * nb + j] == 0
                pl.when(live & edge)(functools.partial(tile, j, h, True))
                pl.when(live & jnp.logical_not(edge))(functools.partial(tile, j, h, False))
            return 0

        lax.fori_loop(jfirst_ref[pr * nb + i], i + 1, step, 0)
        outs, lses = [], []
        for h in range(2):
            l = acc_scr[h][:, (1 - h) * HD:(1 - h) * HD + 1]
            ok = l > 0.0
            ls = jnp.where(ok, l, 1.0)
            outs.append(jnp.where(ok, acc_scr[h] / ls, 0.0))
            lses.append(jnp.where(ok, m_scr[h] + jnp.log(ls), 0.0))
        o_ref[...] = jnp.where(sel0, outs[0], outs[1])
        l_ref[...] = jnp.where(sel0, lses[0], lses[1])

    blk = lambda off: pl.BlockSpec((tb, 2 * HD), lambda p, i, *_: (i, off + p))
    full = lambda off: pl.BlockSpec((T, 2 * HD), lambda p, i, *_: (0, off + p))
    rows = pl.BlockSpec((1, nb, 2, tb), lambda p, i, *_: (p, 0, 0, 0))
    return pl.pallas_call(
        body, name="fox_fwd",
        grid_spec=pltpu.PrefetchScalarGridSpec(
            num_scalar_prefetch=3, grid=(npair, nb),
            in_specs=[blk(0), full(npair), full(2 * npair), blk(0), rows], out_specs=[blk(0), blk(0)],
            scratch_shapes=[pltpu.VMEM((2, tb, 1), F32), pltpu.VMEM((2, tb, 2 * HD), F32)]),
        out_shape=[jax.ShapeDtypeStruct((T, FW), F32), jax.ShapeDtypeStruct((T, FW), F32)],
        compiler_params=_cp("parallel", "arbitrary"))(*plan, qkv, qkv, qkv, c_cols, c_rows)


def _fox_bwd_dq(qkv, do, c_cols, c_rows, lse, delta, plan):
    T = qkv.shape[0]
    tb = _div(T, ATT_TILE)
    nb = T // tb
    npair = HEADS // 2
    scale = HD ** -0.5
    dn = (((1,), (1,)), ((), ()))

    def body(skip_ref, jfirst_ref, ilast_ref, q_ref, k_ref, v_ref, do_ref, cq_ref, ck_ref, l_ref, dl_ref, dq_ref, dc_ref,
             dq_scr):
        pr = pl.program_id(0)
        i = pl.program_id(1)
        q = q_ref[...]
        do = do_ref[...]
        lane = lax.broadcasted_iota(jnp.int32, (1, 2 * HD), 1)
        sel0 = lane < HD
        qh = (jnp.where(sel0, q, jnp.zeros_like(q)) * scale, jnp.where(sel0, jnp.zeros_like(q), q) * scale)
        doh = (jnp.where(sel0, do, jnp.zeros_like(do)), jnp.where(sel0, jnp.zeros_like(do), do))
        ce = (cq_ref[:, 0:1] - l_ref[:, 0:1], cq_ref[:, HD:HD + 1] - l_ref[:, HD:HD + 1])
        dl = (dl_ref[:, 0:1], dl_ref[:, HD:HD + 1])
        row = i * tb + lax.broadcasted_iota(jnp.int32, (tb, 1), 0)
        dq_scr[...] = jnp.zeros(dq_scr.shape, F32)

        def tile(j, h, masked):
            start = pl.multiple_of(j * tb, tb)
            kb = k_ref[pl.ds(start, tb), :]
            vb = v_ref[pl.ds(start, tb), :]
            one = jnp.ones_like(kb)
            kh = jnp.where(sel0, kb, one) if h == 0 else jnp.where(sel0, one, kb)
            t = lax.dot_general(qh[h], kb, dn, preferred_element_type=F32) - ck_ref[0, j, h:h + 1, :]
            if masked:
                col = j * tb + lax.broadcasted_iota(jnp.int32, (1, tb), 1)
                mask = (col <= row) & (col >= FRONT)
                p = jnp.where(mask, jnp.exp(jnp.where(mask, t, NEG) + ce[h]), 0.0)
            else:
                p = jnp.exp(t + ce[h])
            dp = lax.dot_general(doh[h], vb, dn, preferred_element_type=F32)
            ds = p * (dp - dl[h])
            dq_scr[h] += jnp.dot(ds.astype(kb.dtype), kh, preferred_element_type=F32)

        def step(j, _):
            edge = (j == 0) | (j == i)
            for h in range(2):
                live = skip_ref[((2 * pr + h) * nb + i) * nb + j] == 0
                pl.when(live & edge)(functools.partial(tile, j, h, True))
                pl.when(live & jnp.logical_not(edge))(functools.partial(tile, j, h, False))
            return 0

        lax.fori_loop(jfirst_ref[pr * nb + i], i + 1, step, 0)
        dq_ref[...] = jnp.where(sel0, dq_scr[0], dq_scr[1]) * scale
        dc_ref[...] = jnp.where(sel0, dq_scr[0][:, HD:HD + 1], dq_scr[1][:, 0:1])

    blk = lambda off: pl.BlockSpec((tb, 2 * HD), lambda p, i, *_: (i, off + p))
    full = lambda off: pl.BlockSpec((T, 2 * HD), lambda p, i, *_: (0, off + p))
    rows = pl.BlockSpec((1, nb, 2, tb), lambda p, i, *_: (p, 0, 0, 0))
    return pl.pallas_call(
        body, name="fox_bwd_dq",
        grid_spec=pltpu.PrefetchScalarGridSpec(
            num_scalar_prefetch=3, grid=(npair, nb),
            in_specs=[blk(0), full(npair), full(2 * npair), blk(0), blk(0), rows, blk(0), blk(0)], out_specs=[blk(0), blk(0)],
            scratch_shapes=[pltpu.VMEM((2, tb, 2 * HD), F32)]),
        out_shape=[jax.ShapeDtypeStruct((T, FW), F32), jax.ShapeDtypeStruct((T, FW), F32)],
        compiler_params=_cp("parallel", "arbitrary"))(*plan, qkv, qkv, qkv, do, c_cols, c_rows, lse, delta)


def _fox_bwd_dkv(qkv, do, c_cols, ce_rows, delta_rows, plan):
    T = qkv.shape[0]
    tb = _div(T, ATT_TILE)
    nb = T // tb
    npair = HEADS // 2
    scale = HD ** -0.5
    dn = (((1,), (1,)), ((), ()))

    def body(skip_ref, jfirst_ref, ilast_ref, q_ref, k_ref, v_ref, do_ref, ck_ref, ce_ref, dl_ref,
             dk_ref, dv_ref, dc_ref, dk_scr, dv_scr):
        pr = pl.program_id(0)
        jb = pl.program_id(1)
        k = k_ref[...]
        v = v_ref[...]
        lane = lax.broadcasted_iota(jnp.int32, (1, 2 * HD), 1)
        sel0 = lane < HD
        kh = (jnp.where(sel0, k, jnp.zeros_like(k)) * scale, jnp.where(sel0, jnp.zeros_like(k), k) * scale)
        vh = (jnp.where(sel0, v, jnp.zeros_like(v)), jnp.where(sel0, jnp.zeros_like(v), v))
        ck = (ck_ref[:, 0:1], ck_ref[:, HD:HD + 1])
        kidx = jb * tb + lax.broadcasted_iota(jnp.int32, (tb, 1), 0)
        dk_scr[...] = jnp.zeros(dk_scr.shape, F32)
        dv_scr[...] = jnp.zeros(dv_scr.shape, F32)

        def tile(i, h, masked):
            start = pl.multiple_of(i * tb, tb)
            qb = q_ref[pl.ds(start, tb), :]
            dob = do_ref[pl.ds(start, tb), :]
            one = jnp.ones_like(qb)
            qh = jnp.where(sel0, qb, one) if h == 0 else jnp.where(sel0, one, qb)
            t = lax.dot_general(kh[h], qb, dn, preferred_element_type=F32) - ck[h]
            ce = ce_ref[0, i, h:h + 1, :]
            if masked:
                qidx = i * tb + lax.broadcasted_iota(jnp.int32, (1, tb), 1)
                mask = (kidx <= qidx) & (kidx >= FRONT)
                pt = jnp.where(mask, jnp.exp(jnp.where(mask, t, NEG) + ce), 0.0)
            else:
                pt = jnp.exp(t + ce)
            dv_scr[h] += jnp.dot(pt.astype(dob.dtype), dob, preferred_element_type=F32)
            dpt = lax.dot_general(vh[h], dob, dn, preferred_element_type=F32)
            dst = pt * (dpt - dl_ref[0, i, h:h + 1, :])
            dk_scr[h] += jnp.dot(dst.astype(qb.dtype), qh, preferred_element_type=F32)

        def step(i, _):
            edge = (i == jb) | (jb == 0)
            for h in range(2):
                live = skip_ref[((2 * pr + h) * nb + i) * nb + jb] == 0
                pl.when(live & edge)(functools.partial(tile, i, h, True))
                pl.when(live & jnp.logical_not(edge))(functools.partial(tile, i, h, False))
            return 0

        lax.fori_loop(jb, ilast_ref[pr * nb + jb] + 1, step, 0)
        dk_ref[...] = jnp.where(sel0, dk_scr[0], dk_scr[1]) * scale
        dv_ref[...] = jnp.where(sel0, dv_scr[0], dv_scr[1])
        dc_ref[...] = -jnp.where(sel0, dk_scr[0][:, HD:HD + 1], dk_scr[1][:, 0:1])

    blk = lambda off: pl.BlockSpec((tb, 2 * HD), lambda p, j, *_: (j, off + p))
    full = lambda off: pl.BlockSpec((T, 2 * HD), lambda p, j, *_: (0, off + p))
    rows = pl.BlockSpec((1, nb, 2, tb), lambda p, j, *_: (p, 0, 0, 0))
    return pl.pallas_call(
        body, name="fox_bwd_dkv",
        grid_spec=pltpu.PrefetchScalarGridSpec(
            num_scalar_prefetch=3, grid=(npair, nb),
            in_specs=[full(0), blk(npair), blk(2 * npair), full(0), blk(0), rows, rows],
            out_specs=[blk(0), blk(0), blk(0)],
            scratch_shapes=[pltpu.VMEM((2, tb, 2 * HD), F32), pltpu.VMEM((2, tb, 2 * HD), F32)]),
        out_shape=[jax.ShapeDtypeStruct((T, FW), F32)] * 3,
        compiler_params=_cp("parallel", "arbitrary"))(*plan, qkv, qkv, qkv, do, c_cols, ce_rows, delta_rows)


def _head_dot(a, b, ones_blk):
    T = a.shape[0]
    tb = _div(T, ROW_TILE, 8)

    def body(a_ref, b_ref, e_ref, o_ref):
        prod = a_ref[...].astype(F32) * b_ref[...].astype(F32)
        o_ref[...] = jnp.dot(prod, e_ref[...], precision=HI, preferred_element_type=F32)

    row = pl.BlockSpec((tb, FW), lambda i: (i, 0))
    return pl.pallas_call(
        body, name="head_dot", grid=(T // tb,), in_specs=[row, row, pl.BlockSpec((FW, FW), lambda i: (0, 0))],
        out_specs=row, out_shape=jax.ShapeDtypeStruct((T, FW), F32), compiler_params=_cp("parallel"))(a, b, ones_blk)


CONV_RC = 32
CONV_TC = 256


def _stage_prev(scr, x_ref, h_ref, first, tb):
    scr[0:8, :] = jnp.where(first, 0.0, h_ref[...])
    scr[8:8 + tb, :] = x_ref[...]


def _windows_prev(scr, r, kw):
    x = scr[pl.ds(pl.multiple_of(r * CONV_RC, CONV_RC), CONV_RC + 8), :]
    return [x[8:] if k == kw - 1 else pltpu.roll(x, kw - 1 - k, 0)[8:] for k in range(kw)]


def _fold8(a):
    return a.reshape(CONV_RC // 8, 8, a.shape[-1]).sum(axis=0)


def _halo_prev(tb, tc, off=0):
    return pl.BlockSpec((8, tc), lambda j, i: (jnp.maximum(i * (tb // 8) - 1, 0), j + off))


def _gconv_fwd(rest, w):
    T = rest.shape[0]
    C = 3 * FW
    kw = w.shape[0]
    tb, tc = _div(T, 1280, CONV_RC), CONV_TC

    def body(x_ref, h_ref, w_ref, o_ref, scr):
        _stage_prev(scr, x_ref, h_ref, pl.program_id(1) == 0, tb)
        wv = w_ref[...]

        def chunk(r, _):
            win = _windows_prev(scr, r, kw)
            u = sum(wv[k:k + 1, :] * win[k] for k in range(kw))
            o_ref[pl.ds(pl.multiple_of(r * CONV_RC, CONV_RC), CONV_RC), :] = u * _sig(u)
            return 0

        lax.fori_loop(0, tb // CONV_RC, chunk, 0)

    return pl.pallas_call(
        body, name="gconv_fwd", grid=(C // tc, T // tb),
        in_specs=[pl.BlockSpec((tb, tc), lambda j, i: (i, j)), _halo_prev(tb, tc), pl.BlockSpec((kw, tc), lambda j, i: (0, j))],
        out_specs=pl.BlockSpec((tb, tc), lambda j, i: (i, j)), out_shape=jax.ShapeDtypeStruct((T, C), F32),
        scratch_shapes=[pltpu.VMEM((tb + 8, tc), F32)], compiler_params=_cp("parallel", "arbitrary"))(rest, rest, w)


def _gconv_bwd_du(rest, w, dy):
    T = rest.shape[0]
    C = 3 * FW
    kw = w.shape[0]
    tb, tc = _div(T, 1280, CONV_RC), CONV_TC

    def body(x_ref, h_ref, w_ref, dy_ref, du_ref, dw_ref, scr):
        i = pl.program_id(1)
        _stage_prev(scr, x_ref, h_ref, i == 0, tb)
        wv = w_ref[...]

        def chunk(r, acc):
            rows = pl.ds(pl.multiple_of(r * CONV_RC, CONV_RC), CONV_RC)
            win = _windows_prev(scr, r, kw)
            u = sum(wv[k:k + 1, :] * win[k] for k in range(kw))
            sg = _sig(u)
            du = dy_ref[rows, :] * sg * (1.0 + u * (1.0 - sg))
            du_ref[rows, :] = du
            return tuple(acc[k] + _fold8(du * win[k]) for k in range(kw))

        acc = lax.fori_loop(0, tb // CONV_RC, chunk, tuple(jnp.zeros((8, tc), F32) for _ in range(kw)))
        part = jnp.concatenate([jnp.sum(a, axis=0, keepdims=True) for a in acc], axis=0)

        @pl.when(i == 0)
        def _():
            dw_ref[...] = part

        @pl.when(i > 0)
        def _():
            dw_ref[...] += part

    blk = pl.BlockSpec((tb, tc), lambda j, i: (i, j))
    wsp = pl.BlockSpec((kw, tc), lambda j, i: (0, j))
    return pl.pallas_call(
        body, name="gconv_bwd_du", grid=(C // tc, T // tb), in_specs=[blk, _halo_prev(tb, tc), wsp, blk],
        out_specs=[blk, wsp], out_shape=[jax.ShapeDtypeStruct((T, C), F32), jax.ShapeDtypeStruct((kw, C), F32)],
        scratch_shapes=[pltpu.VMEM((tb + 8, tc), F32)], compiler_params=_cp("parallel", "arbitrary"))(rest, rest, w, dy)


def _conv_bwd_dx(du, w, out_dtype, name):
    T, C = du.shape
    kw = w.shape[0]
    tb = _div(T, 1280, CONV_RC)
    tc = CONV_TC
    nb = T // tb

    def body(x_ref, h_ref, w_ref, o_ref, scr):
        scr[0:tb, :] = x_ref[...]
        scr[tb:tb + 8, :] = jnp.where(pl.program_id(1) == nb - 1, 0.0, h_ref[...])
        wv = w_ref[...]

        def chunk(r, _):
            start = pl.multiple_of(r * CONV_RC, CONV_RC)
            x = scr[pl.ds(start, CONV_RC + 8), :]
            acc = wv[kw - 1:kw, :] * x[:CONV_RC]
            for k in range(kw - 1):
                acc = acc + wv[k:k + 1, :] * pltpu.roll(x, CONV_RC + 8 - (kw - 1 - k), 0)[:CONV_RC]
            o_ref[pl.ds(start, CONV_RC), :] = acc.astype(o_ref.dtype)
            return 0

        lax.fori_loop(0, tb // CONV_RC, chunk, 0)

    halo = pl.BlockSpec((8, tc), lambda j, i: (jnp.minimum((i + 1) * (tb // 8), T // 8 - 1), j))
    return pl.pallas_call(
        body, name=name, grid=(C // tc, nb),
        in_specs=[pl.BlockSpec((tb, tc), lambda j, i: (i, j)), halo, pl.BlockSpec((kw, tc), lambda j, i: (0, j))],
        out_specs=pl.BlockSpec((tb, tc), lambda j, i: (i, j)), out_shape=jax.ShapeDtypeStruct((T, C), out_dtype),
        scratch_shapes=[pltpu.VMEM((tb + 8, tc), F32)], compiler_params=_cp("parallel", "arbitrary"))(du, du, w)


def _glu_fwd(up, w, b):
    T = up.shape[0]
    kw = w.shape[0]
    tb, tc = _div(T, 1280, CONV_RC), CONV_TC
    nc = DFF // tc

    def body(xg, hg, xu, hu, wg, wu, bg, bu, o_ref, sg, su):
        first = pl.program_id(1) == 0
        _stage_prev(sg, xg, hg, first, tb)
        _stage_prev(su, xu, hu, first, tb)
        wgv, wuv, bgv, buv = wg[...], wu[...], bg[...], bu[...]

        def chunk(r, _):
            wing, winu = _windows_prev(sg, r, kw), _windows_prev(su, r, kw)
            ug = bgv + sum(wgv[k:k + 1, :] * wing[k] for k in range(kw))
            uu = buv + sum(wuv[k:k + 1, :] * winu[k] for k in range(kw))
            o_ref[pl.ds(pl.multiple_of(r * CONV_RC, CONV_RC), CONV_RC), :] = (ug * _sig(ug) * uu).astype(o_ref.dtype)
            return 0

        lax.fori_loop(0, tb // CONV_RC, chunk, 0)

    blk = lambda off: pl.BlockSpec((tb, tc), lambda j, i: (i, j + off))
    wsp = lambda off: pl.BlockSpec((kw, tc), lambda j, i: (0, j + off))
    bsp = lambda off: pl.BlockSpec((1, tc), lambda j, i: (0, j + off))
    return pl.pallas_call(
        body, name="glu_fwd", grid=(nc, T // tb),
        in_specs=[blk(0), _halo_prev(tb, tc), blk(nc), _halo_prev(tb, tc, nc), wsp(0), wsp(nc), bsp(0), bsp(nc)],
        out_specs=blk(0), out_shape=jax.ShapeDtypeStruct((T, DFF), _BF),
        scratch_shapes=[pltpu.VMEM((tb + 8, tc), F32)] * 2, compiler_params=_cp("parallel", "arbitrary"))(
            up, up, up, up, w, w, b, b)


def _glu_bwd_du(up, w, b, df):
    T = up.shape[0]
    kw = w.shape[0]
    tb, tc = _div(T, 1280, CONV_RC), CONV_TC
    nc = DFF // tc

    def body(xg, hg, xu, hu, wg, wu, bg, bu, df_ref, dug_ref, duu_ref, dwg_ref, dwu_ref, dbg_ref, dbu_ref, sg, su):
        i = pl.program_id(1)
        _stage_prev(sg, xg, hg, i == 0, tb)
        _stage_prev(su, xu, hu, i == 0, tb)
        wgv, wuv, bgv, buv = wg[...], wu[...], bg[...], bu[...]

        def chunk(r, acc):
            rows = pl.ds(pl.multiple_of(r * CONV_RC, CONV_RC), CONV_RC)
            wing, winu = _windows_prev(sg, r, kw), _windows_prev(su, r, kw)
            ug = bgv + sum(wgv[k:k + 1, :] * wing[k] for k in range(kw))
            uu = buv + sum(wuv[k:k + 1, :] * winu[k] for k in range(kw))
            s = _sig(ug)
            df = df_ref[rows, :]
            dug = df * uu * s * (1.0 + ug * (1.0 - s))
            duu = df * ug * s
            dug_ref[rows, :] = dug
            duu_ref[rows, :] = duu
            new = [dug * wing[k] for k in range(kw)] + [duu * winu[k] for k in range(kw)] + [dug, duu]
            return tuple(a + _fold8(v) for a, v in zip(acc, new))

        acc = lax.fori_loop(0, tb // CONV_RC, chunk, tuple(jnp.zeros((8, tc), F32) for _ in range(2 * kw + 2)))
        col = [jnp.sum(a, axis=0, keepdims=True) for a in acc]
        parts = (jnp.concatenate(col[:kw], axis=0), jnp.concatenate(col[kw:2 * kw], axis=0), col[2 * kw], col[2 * kw + 1])
        accs = (dwg_ref, dwu_ref, dbg_ref, dbu_ref)

        @pl.when(i == 0)
        def _():
            for r, p in zip(accs, parts):
                r[...] = p

        @pl.when(i > 0)
        def _():
            for r, p in zip(accs, parts):
                r[...] += p

    blk = lambda off: pl.BlockSpec((tb, tc), lambda j, i: (i, j + off))
    wsp = lambda off: pl.BlockSpec((kw, tc), lambda j, i: (0, j + off))
    bsp = lambda off: pl.BlockSpec((1, tc), lambda j, i: (0, j + off))
    return pl.pallas_call(
        body, name="glu_bwd_du", grid=(nc, T // tb),
        in_specs=[blk(0), _halo_prev(tb, tc), blk(nc), _halo_prev(tb, tc, nc), wsp(0), wsp(nc), bsp(0), bsp(nc), blk(0)],
        out_specs=[blk(0), blk(0), wsp(0), wsp(0), bsp(0), bsp(0)],
        out_shape=[jax.ShapeDtypeStruct((T, DFF), F32)] * 2 + [jax.ShapeDtypeStruct((kw, DFF), F32)] * 2
        + [jax.ShapeDtypeStruct((1, DFF), F32)] * 2,
        scratch_shapes=[pltpu.VMEM((tb + 8, tc), F32)] * 2, compiler_params=_cp("parallel", "arbitrary"))(
            up, up, up, up, w, w, b, b, df)


def _mix_fwd(rest, gate_bias, y_fox, y_gdn):
    T = rest.shape[0]
    tb, tc = _div(T, ROW_TILE, 8), 512
    nc = D // tc
    og = OFF_GATES // tc

    def body(gf, gg, bf, bg, yf, yg, o_ref):
        o_ref[...] = (_sig(gf[...] + bf[...]) * yf[...] + _sig(gg[...] + bg[...]) * yg[...]).astype(o_ref.dtype)

    blk = lambda off: pl.BlockSpec((tb, tc), lambda i, j: (i, j + off))
    bsp = lambda off: pl.BlockSpec((1, tc), lambda i, j: (0, j + off))
    return pl.pallas_call(
        body, name="mix_fwd", grid=(T // tb, nc),
        in_specs=[blk(og), blk(og + nc), bsp(0), bsp(nc), blk(0), blk(0)], out_specs=blk(0),
        out_shape=jax.ShapeDtypeStruct((T, D), _BF), compiler_params=_cp("parallel", "parallel"))(
            rest, rest, gate_bias, gate_bias, y_fox, y_gdn)


def _mix_bwd(rest, gate_bias, y_fox, y_gdn, dmix):
    T = rest.shape[0]
    tb, tc = _div(T, ROW_TILE, 8), 512
    nc = D // tc
    og = OFF_GATES // tc

    def body(gf, gg, bf, bg, yf, yg, dm, dyf, dyg, dgf, dgg, dbf, dbg):
        i = pl.program_id(1)
        d = dm[...]
        sf = _sig(gf[...] + bf[...])
        sg = _sig(gg[...] + bg[...])
        dyf[...] = (d * sf).astype(dyf.dtype)
        dyg[...] = (d * sg).astype(dyg.dtype)
        a = d * yf[...] * sf * (1.0 - sf)
        b = d * yg[...] * sg * (1.0 - sg)
        dgf[...] = a
        dgg[...] = b
        pa = jnp.sum(a, axis=0, keepdims=True)
        pb = jnp.sum(b, axis=0, keepdims=True)

        @pl.when(i == 0)
        def _():
            dbf[...] = pa
            dbg[...] = pb

        @pl.when(i > 0)
        def _():
            dbf[...] += pa
            dbg[...] += pb

    blk = lambda off: pl.BlockSpec((tb, tc), lambda j, i: (i, j + off))
    bsp = lambda off: pl.BlockSpec((1, tc), lambda j, i: (0, j + off))
    return pl.pallas_call(
        body, name="mix_bwd", grid=(nc, T // tb),
        in_specs=[blk(og), blk(og + nc), bsp(0), bsp(nc), blk(0), blk(0), blk(0)],
        out_specs=[blk(0), blk(0), blk(0), blk(0), bsp(0), bsp(0)],
        out_shape=[jax.ShapeDtypeStruct((T, D), _BF)] * 2 + [jax.ShapeDtypeStruct((T, D), F32)] * 2
        + [jax.ShapeDtypeStruct((1, D), F32)] * 2,
        compiler_params=_cp("parallel", "arbitrary"))(rest, rest, gate_bias, gate_bias, y_fox, y_gdn, dmix)


def _loss_head(h2, w, target, n_valid):
    T = h2.shape[0]
    tb = _div(T, ROW_TILE, 8)

    def body(h_ref, w_ref, t_ref, dh_ref, loss_ref, dw_ref):
        i = pl.program_id(0)
        x = h_ref[...]
        r = lax.rsqrt(jnp.mean(x * x, axis=-1, keepdims=True) + EPS)
        xh = x * r
        row = i * tb + lax.broadcasted_iota(jnp.int32, (tb, 1), 0)
        valid = (row >= X0) & (row < X0 + n_valid)
        e = jnp.where(valid, xh * w_ref[...] - t_ref[...], 0.0)
        dy = e * (1.0 / D)
        g = dy * w_ref[...]
        dh_ref[...] = r * (g - xh * jnp.mean(xh * g, axis=-1, keepdims=True))
        lpart = 0.5 * jnp.sum(jnp.sum(e * e, axis=-1, keepdims=True) * (1.0 / D), axis=0, keepdims=True)
        wpart = jnp.sum(dy * xh, axis=0, keepdims=True)

        @pl.when(i == 0)
        def _():
            loss_ref[...] = lpart
            dw_ref[...] = wpart

        @pl.when(i > 0)
        def _():
            loss_ref[...] += lpart
            dw_ref[...] += wpart

    row = pl.BlockSpec((tb, D), lambda i: (i, 0))
    one = pl.BlockSpec((1, D), lambda i: (0, 0))
    return pl.pallas_call(
        body, name="loss_head", grid=(T // tb,), in_specs=[row, one, row],
        out_specs=[row, pl.BlockSpec((1, 1), lambda i: (0, 0)), one],
        out_shape=[jax.ShapeDtypeStruct((T, D), F32), jax.ShapeDtypeStruct((1, 1), F32), jax.ShapeDtypeStruct((1, D), F32)],
        compiler_params=_cp("arbitrary"))(h2, w, target)


def _bd_lo(a, b, ca, cb):
    return lax.dot_general(a.astype(_BF), b.astype(_BF), (((ca,), (cb,)), ((0,), (0,))), preferred_element_type=F32)


def _split2(a):
    hi = a.astype(_BF)
    return hi, (a - hi.astype(F32)).astype(_BF)


def _bd_hi(a, b, ca, cb, exact_a=False):
    dn = (((ca,), (cb,)), ((0,), (0,)))
    dot = lambda x, y: lax.dot_general(x, y, dn, preferred_element_type=F32)
    bh, bl = _split2(b)
    if exact_a:
        ah = a.astype(_BF)
        return dot(ah, bh) + dot(ah, bl)
    ah, al = _split2(a)
    return dot(ah, bh) + (dot(ah, bl) + dot(al, bh))


def _row_to_col(x):
    eye = lax.broadcasted_iota(jnp.int32, (1, CH, CH), 1) == lax.broadcasted_iota(jnp.int32, (1, CH, CH), 2)
    return jnp.sum(jnp.where(eye, jnp.broadcast_to(x, (HEADS, CH, CH)), 0.0), axis=2, keepdims=True)


def _col_to_row(x):
    eye = lax.broadcasted_iota(jnp.int32, (1, CH, CH), 1) == lax.broadcasted_iota(jnp.int32, (1, CH, CH), 2)
    return jnp.sum(jnp.where(eye, jnp.broadcast_to(x, (HEADS, CH, CH)), 0.0), axis=1, keepdims=True)


def _gdn_chunk(q, k, v, bpre, apre, alog, dtb):
    H = HEADS
    r = lax.broadcasted_iota(jnp.int32, (1, CH, CH), 1)
    c = lax.broadcasted_iota(jnp.int32, (1, CH, CH), 2)
    tril, strict = r >= c, r > c
    lb = jnp.broadcast_to(tril.astype(F32), (H, CH, CH))
    rq = lax.rsqrt(jnp.sum(q * q, axis=-1, keepdims=True) + EPS)
    rk = lax.rsqrt(jnp.sum(k * k, axis=-1, keepdims=True) + EPS)
    qh = q * rq
    qn = qh * (HD ** -0.5)
    kn = k * rk
    beta = _sig(bpre)
    x = apre + dtb
    ea = jnp.exp(alog)
    g = -ea * jax.nn.softplus(x)
    gb = jnp.broadcast_to(g, (H, CH, HD))
    gc = _bd_hi(lb, gb, 2, 1, True)
    dm = gc - _col_to_row(gc[:, :, 0:1])
    decay = jnp.where(tril, jnp.exp(jnp.where(tril, dm, 0.0)), 0.0)
    eg = jnp.exp(gc)
    gl = gc[:, CH - 1:CH, :]
    egl = jnp.exp(gl - gc)
    cd = jnp.exp(gl)
    kb = kn * beta
    vb = v * beta
    both = _bd_lo(_rows2(kb, qn), kn, 2, 2)
    kk, qk = both[:, :CH], both[:, CH:]
    pw = -jnp.where(strict, kk * decay, 0.0)
    tm = jnp.where(r == c, 1.0, 0.0) + pw
    pw = _bd_hi(pw, pw, 2, 1)
    for it in range(4):
        mul = _bd_hi if it < 2 else _bd_lo
        both = mul(_rows2(pw, tm), pw, 2, 1)
        pw, tm = both[:, :CH], tm + both[:, CH:]
    tm = tm + _bd_lo(tm, pw, 2, 1)
    kbg = kb * eg
    sol = _bd_hi(tm, _lanes2(vb, kbg), 2, 1)
    value, kcd = sol[:, :, :HD], sol[:, :, HD:]
    attn = jnp.where(tril, qk * decay, 0.0)
    return dict(tril=tril, strict=strict, lb=lb, rq=rq, rk=rk, qh=qh, qn=qn, kn=kn, beta=beta, x=x, ea=ea, g=g,
                decay=decay, eg=eg, egl=egl, cd=cd, kb=kb, vb=vb, kk=kk, tm=tm, kbg=kbg, value=value, kcd=kcd, qk=qk,
                attn=attn, qd=qn * eg, kt=kn * egl, sol=sol)


def _rows2(a, b):
    return jnp.concatenate([a, b], axis=1)


def _lanes2(a, b):
    return jnp.concatenate([a, b], axis=2)


GDN_CPS = 2


def _gdn_specs(T, rev):
    G = GDN_CPS
    ns = T // (G * CH)
    pos = (lambda n: ns - 1 - n) if rev else (lambda n: n)
    mat = pl.BlockSpec((HEADS, G * CH, HD), lambda n: (0, pos(n), 0))
    col = pl.BlockSpec((G, HEADS, 1, CH), lambda n: (pos(n), 0, 0, 0))
    sca = pl.BlockSpec((HEADS, 1, 1), lambda n: (0, 0, 0))
    nw = pl.BlockSpec((1, 1, HD), lambda n: (0, 0, 0))
    st = pl.BlockSpec((G, HEADS, HD, HD), lambda n: (pos(n), 0, 0, 0))
    tok = lambda width, off: pl.BlockSpec((G * CH, width), lambda n: (pos(n), off))
    return ns, mat, col, sca, nw, st, tok


def _split_heads(x):
    return [jnp.stack([x[:, (g * HEADS + h) * HD:(g * HEADS + h + 1) * HD] for h in range(HEADS)])
            for g in range(x.shape[1] // FW)]


def _store_heads(ref, rows, g, val):
    for h in range(HEADS):
        ref[rows, (g * HEADS + h) * HD:(g * HEADS + h + 1) * HD] = val[h]


def _gdn_fwd(conv, rest, bpre, apre, alog, dtb, nw):
    T = conv.shape[0]
    ns, mat, col, sca, nws, st, tok = _gdn_specs(T, False)
    rows = [slice(g * CH, (g + 1) * CH) for g in range(GDN_CPS)]

    def body(c_ref, z_ref, b_ref, a_ref, al_ref, dt_ref, nw_ref, o_ref, og_ref, st_ref, s_scr):
        @pl.when(pl.program_id(0) == 0)
        def _():
            s_scr[...] = jnp.zeros_like(s_scr)

        locs = [_gdn_chunk(*_split_heads(c_ref[rows[g], :]), _row_to_col(b_ref[g]), _row_to_col(a_ref[g]), al_ref[...],
                           dt_ref[...]) for g in range(GDN_CPS)]
        s = s_scr[...]
        for g, L in enumerate(locs):
            st_ref[g] = s
            both = _bd_lo(_rows2(L["kcd"], L["qd"]), s, 2, 1)
            v_new = L["value"] - both[:, :CH]
            o = both[:, CH:] + _bd_lo(L["attn"], v_new, 2, 1)
            s = s * L["cd"] + _bd_lo(L["kt"], v_new, 1, 1)
            o_ref[:, rows[g], :] = o
            zz, = _split_heads(z_ref[rows[g], :])
            rs = lax.rsqrt(jnp.mean(o * o, axis=-1, keepdims=True) + EPS)
            _store_heads(og_ref, rows[g], 0, o * rs * nw_ref[...] * zz * _sig(zz))
        s_scr[...] = s

    return pl.pallas_call(
        body, name="gdn_fwd", grid=(ns,), in_specs=[tok(3 * FW, 0), tok(FW, OFF_Z // FW), col, col, sca, sca, nws],
        out_specs=[mat, tok(FW, 0), st],
        out_shape=[jax.ShapeDtypeStruct((HEADS, T, HD), F32), jax.ShapeDtypeStruct((T, FW), F32),
                   jax.ShapeDtypeStruct((T // CH, HEADS, HD, HD), F32)],
        scratch_shapes=[pltpu.VMEM((HEADS, HD, HD), F32)], compiler_params=_cp("arbitrary"))(
            conv, rest, bpre, apre, alog, dtb, nw)


def _gdn_bwd(conv, rest, bpre, apre, alog, dtb, nw, states, o, dog):
    T = conv.shape[0]
    ns, mat, col, sca, nws, st, tok = _gdn_specs(T, True)
    rows = [slice(g * CH, (g + 1) * CH) for g in range(GDN_CPS)]

    def body(c_ref, z_ref, b_ref, a_ref, al_ref, dt_ref, nw_ref, st_ref, o_ref, dog_ref,
             dc_ref, dz_ref, db_ref, da_ref, dal_ref, ddt_ref, dnw_ref, ds_scr):
        @pl.when(pl.program_id(0) == 0)
        def _():
            ds_scr[...] = jnp.zeros_like(ds_scr)
            dal_ref[...] = jnp.zeros_like(dal_ref)
            ddt_ref[...] = jnp.zeros_like(ddt_ref)
            dnw_ref[...] = jnp.zeros_like(dnw_ref)

        splits = [_split_heads(c_ref[rows[g], :]) for g in range(GDN_CPS)]
        locs = [_gdn_chunk(*splits[g], _row_to_col(b_ref[g]), _row_to_col(a_ref[g]), al_ref[...], dt_ref[...])
                for g in range(GDN_CPS)]
        dsn = ds_scr[...]
        acc_al, acc_dt, acc_nw = 0.0, 0.0, 0.0
        for g in reversed(range(GDN_CPS)):
            L, vv = locs[g], splits[g][2]
            tril, strict, lb = L["tril"], L["strict"], L["lb"]
            qn, kn, kb, beta, decay, eg, egl, cd = L["qn"], L["kn"], L["kb"], L["beta"], L["decay"], L["eg"], L["egl"], L["cd"]
            value, kcd, attn, qd, kt, tm = L["value"], L["kcd"], L["attn"], L["qd"], L["kt"], L["tm"]
            s = st_ref[g]
            v_new = value - _bd_lo(kcd, s, 2, 1)
            oo = o_ref[:, rows[g], :]
            zz, = _split_heads(z_ref[rows[g], :])
            dog_, = _split_heads(dog_ref[rows[g], :])
            sz = _sig(zz)
            rs = lax.rsqrt(jnp.mean(oo * oo, axis=-1, keepdims=True) + EPS)
            oh = oo * rs
            _store_heads(dz_ref, rows[g], 0, dog_ * oh * nw_ref[...] * sz * (1.0 + zz * (1.0 - sz)))
            don = dog_ * zz * sz
            gdy = don * nw_ref[...]
            do = rs * (gdy - oh * jnp.mean(oh * gdy, axis=-1, keepdims=True))
            acc_nw = acc_nw + jnp.sum(don * oh, axis=(0, 1), keepdims=True)
            d_vnew = _bd_lo(attn, do, 1, 1) + _bd_lo(kt, dsn, 2, 1)
            both = _bd_lo(_rows2(do, d_vnew), s, 2, 2)
            d_qd, d_kcd = both[:, :CH], -both[:, CH:]
            d_attn = jnp.where(tril, _bd_lo(do, v_new, 2, 2), 0.0)
            d_kt = _bd_lo(v_new, dsn, 2, 2)
            d_cd = jnp.sum(s * dsn, axis=(1, 2), keepdims=True)
            dsn_next = cd * dsn + _bd_lo(_rows2(qd, kcd), _rows2(do, -d_vnew), 1, 1)
            dsol = _bd_hi(tm, _lanes2(d_vnew, d_kcd), 1, 1)
            d_vb, d_kbg = dsol[:, :, :HD], dsol[:, :, HD:]
            da = -jnp.where(strict, _bd_lo(dsol, L["sol"], 2, 2), 0.0)
            dkk = da * decay
            dqk = d_attn * decay
            d_decay = da * L["kk"] + d_attn * L["qk"]
            both = _bd_lo(_rows2(dkk, dqk), kn, 2, 1)
            d_kb = both[:, :CH] + d_kbg * eg
            d_qn = both[:, CH:] + d_qd * eg
            d_kn = _bd_lo(_rows2(dkk, dqk), _rows2(kb, qn), 1, 1) + d_kt * egl
            dd = d_decay * decay
            d_gc = jnp.sum(dd + d_qd * qd + d_kbg * L["kbg"] - d_kt * kt, axis=-1, keepdims=True) \
                - _row_to_col(jnp.sum(dd, axis=1, keepdims=True))
            d_gl = jnp.sum(d_kt * kt, axis=(1, 2), keepdims=True) + d_cd * cd[:, :, 0:1]
            last = lax.broadcasted_iota(jnp.int32, (1, CH, 1), 1) == CH - 1
            d_gc = d_gc + jnp.where(last, d_gl, 0.0)
            dg = _bd_hi(lb, jnp.broadcast_to(d_gc, (HEADS, CH, HD)), 1, 1, True)[:, :, 0:1]
            d_apre = -dg * L["ea"] * _sig(L["x"])
            da_ref[g] = _col_to_row(d_apre)
            acc_al = acc_al + jnp.sum(dg * L["g"], axis=1, keepdims=True)
            acc_dt = acc_dt + jnp.sum(d_apre, axis=1, keepdims=True)
            d_beta = jnp.sum(d_kb * kn + d_vb * vv, axis=-1, keepdims=True)
            db_ref[g] = _col_to_row(d_beta * beta * (1.0 - beta))
            d_kn = d_kn + d_kb * beta
            qh = L["qh"]
            _store_heads(dc_ref, rows[g], 0, (HD ** -0.5) * L["rq"] * (d_qn - qh * jnp.sum(qh * d_qn, axis=-1, keepdims=True)))
            _store_heads(dc_ref, rows[g], 1, L["rk"] * (d_kn - kn * jnp.sum(kn * d_kn, axis=-1, keepdims=True)))
            _store_heads(dc_ref, rows[g], 2, d_vb * beta)
            dsn = dsn_next
        ds_scr[...] = dsn
        dal_ref[...] += acc_al
        ddt_ref[...] += acc_dt
        dnw_ref[...] += acc_nw

    c3 = jax.ShapeDtypeStruct((T // CH, HEADS, 1, CH), F32)
    s3 = jax.ShapeDtypeStruct((HEADS, 1, 1), F32)
    return pl.pallas_call(
        body, name="gdn_bwd", grid=(ns,),
        in_specs=[tok(3 * FW, 0), tok(FW, OFF_Z // FW), col, col, sca, sca, nws, st, mat, tok(FW, 0)],
        out_specs=[tok(3 * FW, 0), tok(FW, 0), col, col, sca, sca, nws],
        out_shape=[jax.ShapeDtypeStruct((T, 3 * FW), F32), jax.ShapeDtypeStruct((T, FW), F32), c3, c3, s3, s3,
                   jax.ShapeDtypeStruct((1, 1, HD), F32)],
        scratch_shapes=[pltpu.VMEM((HEADS, HD, HD), F32)], compiler_params=_cp("arbitrary"))(
            conv, rest, bpre, apre, alog, dtb, nw, states, o, dog)


def _local_step(x, target, w):
    seq = x.shape[0]
    T = -(-(X0 + seq) // T_ALIGN) * T_ALIGN
    back = T - X0 - seq
    tb_att = _div(T, ATT_TILE)
    h0 = jnp.concatenate([jnp.zeros((FRONT, D), F32), w["meta"], x, jnp.zeros((back, D), F32)], axis=0)
    tgt = jnp.concatenate([jnp.zeros((X0, D), F32), target, jnp.zeros((back, D), F32)], axis=0)
    row = lambda v: v.reshape(1, -1)
    nmix, nffn, nfin = row(w["norm_mix"]), row(w["norm_ffn"]), row(w["norm_final"])
    gate_b = row(w["gate_bias"])
    fconv_b = row(w["ffn_conv_b"])
    bias128 = jnp.zeros((1, SMALL_W), F32).at[0, :HEADS].set(w["fgt_bias"])
    alog = w["a_log"].reshape(HEADS, 1, 1)
    dtb = w["dt_bias"].reshape(HEADS, 1, 1)
    gnw = w["gdn_norm"].reshape(1, 1, HD)

    a = _rms_fwd(h0, nmix, "rms_mix")
    pf = _mm(a, w["w_fox"], out_dtype=_BF, name="proj_fox")
    rest = _mm(a, w["w_rest"], name="proj_rest")
    small = rest[:, OFF_SMALL:]
    c8 = _fgate_fwd(small, bias128)[:, :HEADS]
    c_cols, c_rows = _att_cols(c8), _att_rows(c8, tb_att)
    ones_blk = jnp.asarray(np.kron(np.eye(HEADS, dtype=np.float32), np.ones((HD, HD), np.float32)))
    plan = _att_plan(_att_stats(pf, c_cols, ones_blk))
    o_fox, lse = _fox_fwd(pf, c_cols, c_rows, plan)
    conv = _gconv_fwd(rest, w["gdn_conv"])
    chunk_rows = lambda a8: a8.reshape(T // CH, CH, HEADS).transpose(0, 2, 1)[:, :, None, :]
    bpre = chunk_rows(small[:, HEADS:2 * HEADS])
    apre = chunk_rows(small[:, 2 * HEADS:3 * HEADS])
    o_raw, og, states = _gdn_fwd(conv, rest, bpre, apre, alog, dtb, gnw)
    y_fox = _mm(o_fox, w["w_bfox"], name="y_fox")
    y_gdn = _mm(og, w["w_bgdn"], name="y_gdn")
    mix = _mix_fwd(rest, gate_b, y_fox, y_gdn)
    h1 = _mm(mix, w["w_out"], resid=h0, name="out_proj")
    b = _rms_fwd(h1, nffn, "rms_ffn")
    up = _mm(b, w["w_up"], name="ffn_up")
    f = _glu_fwd(up, w["ffn_conv"], fconv_b)
    h2 = _mm(f, w["w_down"], resid=h1, name="ffn_down")
    dh2, loss, d_nfin = _loss_head(h2, nfin, tgt, seq)

    d_f = _mm(dh2, w["w_down"], tb=True, name="d_f")
    g_down = _mm_t(f, dh2, "g_down")
    dug, duu, dwg, dwu, dbg, dbu = _glu_bwd_du(up, w["ffn_conv"], fconv_b, d_f)
    dxg = _conv_bwd_dx(dug, w["ffn_conv"][:, :DFF], _BF, "fconv_dx_gate")
    dxu = _conv_bwd_dx(duu, w["ffn_conv"][:, DFF:], _BF, "fconv_dx_up")
    d_b = _mm(dxg, w["w_up"][:, :DFF], tb=True, name="d_b_gate")
    d_b = _mm(dxu, w["w_up"][:, DFF:], tb=True, resid=d_b, name="d_b_up")
    g_up = jnp.concatenate([_mm_t(b, dxg, "g_up_gate"), _mm_t(b, dxu, "g_up_up")], axis=1)
    dh1, d_nffn = _rms_bwd(h1, nffn, d_b, dh2, "rms_ffn_bwd")

    dmix = _mm(dh1, w["w_out"], tb=True, name="d_mix")
    g_out = _mm_t(mix, dh1, "g_out")
    dyf, dyg, dgf, dgg, dgbf, dgbg = _mix_bwd(rest, gate_b, y_fox, y_gdn, dmix)
    do_fox = _mm(dyf, w["w_bfox"], tb=True, out_dtype=_BF, name="d_o_fox")
    g_bfox = _mm_t(o_fox, dyf, "g_bfox")
    d_og = _mm(dyg, w["w_bgdn"], tb=True, name="d_o_gdn")
    g_bgdn = _mm_t(og, dyg, "g_bgdn")

    delta = _head_dot(do_fox, o_fox, ones_blk)
    dq, dcq = _fox_bwd_dq(pf, do_fox, c_cols, c_rows, lse, delta, plan)
    dk, dv, dck = _fox_bwd_dkv(pf, do_fox, c_cols, _att_rows(c8 - lse[:, ::HD], tb_att), _att_rows(delta[:, ::HD], tb_att), plan)
    dc = jnp.pad((dcq + dck)[:, ::HD], ((0, 0), (0, SMALL_W - HEADS)))
    dfp, d_fb = _fgate_bwd(small, bias128, dc)

    dconv, dz, dbp, dap, d_alog, d_dtb, d_gnw = _gdn_bwd(conv, rest, bpre, apre, alog, dtb, gnw, states, o_raw, d_og)
    du_g, g_gconv = _gconv_bwd_du(rest, w["gdn_conv"], dconv)
    dgx = _conv_bwd_dx(du_g, w["gdn_conv"], F32, "gconv_dx")
    token_rows = lambda a: a[:, :, 0, :].transpose(0, 2, 1).reshape(T, HEADS)
    dsmall = jnp.concatenate([dfp[:, :HEADS], token_rows(dbp), token_rows(dap),
                              jnp.zeros((T, SMALL_W - 3 * HEADS), F32)], axis=1)
    drest = jnp.concatenate([dgx, dz, dgf, dgg, dsmall], axis=1)
    dfox = jnp.concatenate([dq, dk, dv], axis=1)
    d_a = _mm(dfox, w["w_fox"], tb=True, name="d_a_fox")
    d_a = _mm(drest, w["w_rest"], tb=True, resid=d_a, name="d_a_rest")
    g_fox = _mm_t(a, dfox, "g_w_fox")
    g_rest = _mm_t(a, drest, "g_w_rest")
    dh0, d_nmix = _rms_bwd(h0, nmix, d_a, dh1, "rms_mix_bwd")

    sm = lambda lo: g_rest[:, OFF_SMALL + lo:OFF_SMALL + lo + HEADS]
    g_w_in = jnp.concatenate([g_fox, sm(0), g_rest[:, :3 * FW], g_rest[:, OFF_Z:OFF_Z + FW], sm(HEADS), sm(2 * HEADS),
                              g_rest[:, OFF_GATES:OFF_GATES + 2 * D]], axis=1)
    grads = dict(
        meta_tokens=dh0[FRONT:X0], w_in=g_w_in, fgt_bias=d_fb[0, :HEADS], gdn_conv_w=g_gconv,
        gdn_a_log=d_alog.reshape(HEADS), gdn_dt_bias=d_dtb.reshape(HEADS), gdn_norm_w=d_gnw.reshape(HD),
        gate_bias=jnp.concatenate([dgbf, dgbg], axis=1).reshape(2 * D), w_branch_fox=g_bfox, w_branch_gdn=g_bgdn,
        w_out=g_out, norm_mix_w=d_nmix.reshape(D), norm_ffn_w=d_nffn.reshape(D), ffn_w_up=g_up,
        ffn_conv_w=jnp.concatenate([dwg, dwu], axis=1), ffn_conv_b=jnp.concatenate([dbg, dbu], axis=1).reshape(2 * DFF),
        ffn_w_down=g_down, norm_final_w=d_nfin.reshape(D))
    return loss, dh0[X0:X0 + seq], grads


N_CHIPS = 4
PACK_W = 1024
PACK_ROW_ALIGN = 32
BIG = ("w_in", "w_branch_fox", "w_branch_gdn", "w_out", "ffn_w_up", "ffn_w_down")
WEIGHTS = (
    ("meta_tokens", (N_META, D), 1), ("w_in", (D, 3 * FW + HEADS + 4 * FW + 2 * HEADS + 2 * D), 1), ("fgt_bias", (1, HEADS), None),
    ("gdn_conv_w", (4, 3 * FW), 1), ("gdn_a_log", (1, HEADS), None), ("gdn_dt_bias", (1, HEADS), None),
    ("gdn_norm_w", (1, HD), None), ("gate_bias", (1, 2 * D), None), ("w_branch_fox", (FW, D), 1),
    ("w_branch_gdn", (FW, D), 1), ("w_out", (D, D), 0), ("norm_mix_w", (1, D), None), ("norm_ffn_w", (1, D), None),
    ("ffn_w_up", (D, 2 * DFF), 1), ("ffn_conv_w", (3, 2 * DFF), 1), ("ffn_conv_b", (1, 2 * DFF), None),
    ("ffn_w_down", (DFF, D), 0), ("norm_final_w", (1, D), None))
SPLIT_F32 = ("meta_tokens", "gdn_conv_w", "ffn_conv_w")


def _shard_shape(shape, axis):
    if axis is None:
        return shape
    return tuple(s // N_CHIPS if a == axis else s for a, s in enumerate(shape))


def _shard_of(full, axis, q):
    if axis is None:
        return full
    n = full.shape[axis] // N_CHIPS
    return lax.slice_in_dim(full, q * n, (q + 1) * n, axis=axis)


def _pack_rows(n_elems):
    rows = -(-n_elems // PACK_W)
    return -(-rows // PACK_ROW_ALIGN) * PACK_ROW_ALIGN


def _pack(pieces, dtype):
    flat = jnp.concatenate([p.reshape(-1).astype(dtype) for p in pieces])
    rows = _pack_rows(flat.shape[0])
    return jnp.pad(flat, (0, rows * PACK_W - flat.shape[0])).reshape(rows, PACK_W)


def _unpack(slab, shapes):
    flat = slab.reshape(-1)
    out, off = [], 0
    for s in shapes:
        n = int(np.prod(s))
        out.append(flat[off:off + n].reshape(s))
        off += n
    return out


HBM_SPEC = pl.BlockSpec(memory_space=pltpu.HBM)
MESH_ID = pl.DeviceIdType.MESH


def _scatter_chips(srcs, name):
    n = len(srcs)

    def body(*refs):
        src_refs, out_refs = refs[:n], refs[n:2 * n]
        send_sems, recv_sems, local_sems = refs[2 * n:]
        x, y, c = lax.axis_index("x"), lax.axis_index("y"), lax.axis_index("c")
        q = 2 * x + y
        peers = [(1 - x, y), (x, 1 - y), (1 - x, 1 - y)]

        def remote(a, k, src_slot, dst_slot):
            px, py = peers[k]
            return pltpu.make_async_remote_copy(
                src_ref=src_refs[a].at[src_slot], dst_ref=out_refs[a].at[dst_slot], send_sem=send_sems.at[3 * a + k],
                recv_sem=recv_sems.at[3 * a + k], device_id=(px, py, c), device_id_type=MESH_ID)

        mine = [pltpu.make_async_copy(src_refs[a].at[q], out_refs[a].at[q], local_sems.at[a]) for a in range(n)]
        for cp in mine:
            cp.start()
        sends = [remote(a, k, 2 * px + py, q) for a in range(n) for k, (px, py) in enumerate(peers)]
        for cp in sends:
            cp.start()
        for a in range(n):
            for k, (px, py) in enumerate(peers):
                remote(a, k, 0, 2 * px + py).wait_recv()
        for cp in sends:
            cp.wait_send()
        for cp in mine:
            cp.wait()

    out_shape = [jax.ShapeDtypeStruct((N_CHIPS,) + s.shape[-2:], s.dtype) for s in srcs]
    return pl.pallas_call(
        body, name=name, in_specs=[HBM_SPEC] * n, out_specs=[HBM_SPEC] * n, out_shape=out_shape,
        scratch_shapes=[pltpu.SemaphoreType.DMA((3 * n,)), pltpu.SemaphoreType.DMA((3 * n,)), pltpu.SemaphoreType.DMA((n,))],
    )(*srcs)


def _gather_chips(srcs, name):
    n = len(srcs)

    def body(*refs):
        src_refs, out_refs = refs[:n], refs[n:2 * n]
        ici_send, ici_recv, d2d_send, d2d_recv, local_sems = refs[2 * n:]
        x, y, c = lax.axis_index("x"), lax.axis_index("y"), lax.axis_index("c")
        q = 2 * x + y
        peers = [(1 - x, y), (x, 1 - y), (1 - x, 1 - y)]

        def half(a, which):
            r = srcs[a].shape[0] // 2
            return pl.ds(which * r, r)

        def ici(a, k, slot):
            px, py = peers[k]
            return pltpu.make_async_remote_copy(
                src_ref=src_refs[a].at[half(a, c)], dst_ref=out_refs[a].at[slot, half(a, c)], send_sem=ici_send.at[3 * a + k],
                recv_sem=ici_recv.at[3 * a + k], device_id=(px, py, c), device_id_type=MESH_ID)

        def d2d(a, k, which):
            px, py = peers[k]
            rows = out_refs[a].at[2 * px + py, half(a, which)]
            return pltpu.make_async_remote_copy(
                src_ref=rows, dst_ref=rows, send_sem=d2d_send.at[3 * a + k], recv_sem=d2d_recv.at[3 * a + k],
                device_id=(x, y, 1 - c), device_id_type=MESH_ID)

        mine = [pltpu.make_async_copy(src_refs[a], out_refs[a].at[q], local_sems.at[a]) for a in range(n)]
        for cp in mine:
            cp.start()
        sends = [ici(a, k, q) for a in range(n) for k in range(3)]
        for cp in sends:
            cp.start()
        passed = []
        for a in range(n):
            for k, (px, py) in enumerate(peers):
                ici(a, k, 2 * px + py).wait_recv()
                passed.append(d2d(a, k, c))
                passed[-1].start()
        for a in range(n):
            for k in range(3):
                d2d(a, k, 1 - c).wait_recv()
        for cp in sends + passed:
            cp.wait_send()
        for cp in mine:
            cp.wait()

    out_shape = [jax.ShapeDtypeStruct((N_CHIPS,) + s.shape, s.dtype) for s in srcs]
    sems = [pltpu.SemaphoreType.DMA((3 * n,))] * 4 + [pltpu.SemaphoreType.DMA((n,))]
    return pl.pallas_call(body, name=name, in_specs=[HBM_SPEC] * n, out_specs=[HBM_SPEC] * n, out_shape=out_shape,
                          scratch_shapes=sems)(*srcs)


def _sibling_swap(slabs, name):
    n = len(slabs)

    def body(*refs):
        src_refs, out_refs, send_sems, recv_sems = refs[:n], refs[n:2 * n], refs[2 * n], refs[2 * n + 1]
        x, y, c = lax.axis_index("x"), lax.axis_index("y"), lax.axis_index("c")
        cps = [pltpu.make_async_remote_copy(src_ref=src_refs[a], dst_ref=out_refs[a], send_sem=send_sems.at[a],
                                            recv_sem=recv_sems.at[a], device_id=(x, y, 1 - c), device_id_type=MESH_ID)
               for a in range(n)]
        for cp in cps:
            cp.start()
        for cp in cps:
            cp.wait_recv()
        for cp in cps:
            cp.wait_send()

    return pl.pallas_call(
        body, name=name, in_specs=[HBM_SPEC] * n, out_specs=[HBM_SPEC] * n,
        out_shape=[jax.ShapeDtypeStruct(s.shape, s.dtype) for s in slabs],
        scratch_shapes=[pltpu.SemaphoreType.DMA((n,)), pltpu.SemaphoreType.DMA((n,))])(*slabs)


def _sum_chips(r, name):
    rows, cols = r.shape[1:]
    tb = _div(rows, 256, 16)

    def body(r0, r1, r2, r3, o_ref):
        o_ref[...] = ((r0[0].astype(F32) + r1[0].astype(F32)) + r2[0].astype(F32)) + r3[0].astype(F32)

    spec = lambda j: pl.BlockSpec((1, tb, cols), lambda i: (j, i, 0))
    return pl.pallas_call(
        body, name=name, grid=(rows // tb,), in_specs=[spec(0), spec(1), spec(2), spec(3)],
        out_specs=pl.BlockSpec((tb, cols), lambda i: (i, 0)), out_shape=jax.ShapeDtypeStruct((rows, cols), F32),
        compiler_params=_cp("parallel"))(r, r, r, r)


def _adamw(w, m, v, p, q, name):
    rows, cols = w.shape
    tb = _div(rows, 256, 8)

    def body(w_ref, m_ref, v_ref, p_ref, q_ref, g_ref, d_ref, nm_ref, nv_ref):
        g = p_ref[...] + q_ref[...]
        m_new = B1 * m_ref[...] + (1.0 - B1) * g
        v_new = B2 * v_ref[...] + (1.0 - B2) * (g * g)
        g_ref[...] = g
        nm_ref[...] = m_new
        nv_ref[...] = v_new
        m_hat = m_new / (1.0 - B1 ** STEP)
        v_hat = v_new / (1.0 - B2 ** STEP)
        d_ref[...] = -LR * (m_hat / (jnp.sqrt(v_hat) + AEPS) + WD * w_ref[...])

    spec = pl.BlockSpec((tb, cols), lambda i: (i, 0))
    return pl.pallas_call(
        body, name=name, grid=(rows // tb,), in_specs=[spec] * 5, out_specs=[spec] * 4,
        out_shape=[jax.ShapeDtypeStruct((rows, cols), F32)] * 4, compiler_params=_cp("parallel"))(w, m, v, p, q)


def _split_w_in(w_in):
    o1 = 3 * FW
    o2 = o1 + HEADS
    o3 = o2 + 3 * FW
    o4 = o3 + FW
    o5 = o4 + HEADS
    o6 = o5 + HEADS
    pad = jnp.zeros((w_in.shape[0], SMALL_W - 3 * HEADS), w_in.dtype)
    rest = jnp.concatenate([w_in[:, o2:o3], w_in[:, o3:o4], w_in[:, o6:], w_in[:, o1:o2], w_in[:, o4:o5], w_in[:, o5:o6], pad],
                           axis=1)
    return w_in[:, :o1], rest


AXIS = {n: a for n, _, a in WEIGHTS}
SMALL = tuple(n for n, _, _ in WEIGHTS if n not in BIG)


def _by_chip(full, axis):
    rows, cols = full.shape
    if axis == 0:
        return full.reshape(N_CHIPS, rows // N_CHIPS, cols)
    return full.reshape(rows, N_CHIPS, cols // N_CHIPS).transpose(1, 0, 2)


def _from_chips(parts, axis):
    _, r, c = parts.shape
    if axis == 0:
        return parts.reshape(N_CHIPS * r, c)
    return parts.transpose(1, 0, 2).reshape(r, N_CHIPS * c)


def _gather_weights(shards):
    hi = {n: shards[n].astype(jnp.bfloat16) for n in SPLIT_F32}
    lo = [(shards[n] - hi[n].astype(F32)).astype(jnp.bfloat16) for n in SPLIT_F32]
    slab = _pack([hi[n] for n in SPLIT_F32] + lo, jnp.bfloat16)
    got = _gather_chips([shards[n].astype(jnp.bfloat16) for n in BIG] + [slab], "gather_weights")
    full = {n: _from_chips(g, AXIS[n]) for n, g in zip(BIG, got)}
    shapes = [shards[n].shape for n in SPLIT_F32] * 2
    per_chip = [_unpack(got[-1][j], shapes) for j in range(N_CHIPS)]
    for i, n in enumerate(SPLIT_F32):
        join = lambda off: jnp.concatenate([per_chip[j][off + i] for j in range(N_CHIPS)], axis=1).astype(F32)
        full[n] = join(0) + join(len(SPLIT_F32))
    return full


def kernel(x, meta_tokens, w_in, fgt_bias, gdn_conv_w, gdn_a_log, gdn_dt_bias, gdn_norm_w, gate_bias, w_branch_fox, w_branch_gdn, w_out, norm_mix_w, norm_ffn_w, ffn_w_up, ffn_conv_w, ffn_conv_b, ffn_w_down, norm_final_w, loss_target, m_meta_tokens, m_w_in, m_fgt_bias, m_gdn_conv_w, m_gdn_a_log, m_gdn_dt_bias, m_gdn_norm_w, m_gate_bias, m_w_branch_fox, m_w_branch_gdn, m_w_out, m_norm_mix_w, m_norm_ffn_w, m_ffn_w_up, m_ffn_conv_w, m_ffn_conv_b, m_ffn_w_down, m_norm_final_w, v_meta_tokens, v_w_in, v_fgt_bias, v_gdn_conv_w, v_gdn_a_log, v_gdn_dt_bias, v_gdn_norm_w, v_gate_bias, v_w_branch_fox, v_w_branch_gdn, v_w_out, v_norm_mix_w, v_norm_ffn_w, v_ffn_w_up, v_ffn_conv_w, v_ffn_conv_b, v_ffn_w_down, v_norm_final_w):
    weights = dict(meta_tokens=meta_tokens, w_in=w_in, fgt_bias=fgt_bias, gdn_conv_w=gdn_conv_w, gdn_a_log=gdn_a_log, gdn_dt_bias=gdn_dt_bias, gdn_norm_w=gdn_norm_w, gate_bias=gate_bias, w_branch_fox=w_branch_fox, w_branch_gdn=w_branch_gdn, w_out=w_out, norm_mix_w=norm_mix_w, norm_ffn_w=norm_ffn_w, ffn_w_up=ffn_w_up, ffn_conv_w=ffn_conv_w, ffn_conv_b=ffn_conv_b, ffn_w_down=ffn_w_down, norm_final_w=norm_final_w)
    m_in = dict(meta_tokens=m_meta_tokens, w_in=m_w_in, fgt_bias=m_fgt_bias, gdn_conv_w=m_gdn_conv_w, gdn_a_log=m_gdn_a_log, gdn_dt_bias=m_gdn_dt_bias, gdn_norm_w=m_gdn_norm_w, gate_bias=m_gate_bias, w_branch_fox=m_w_branch_fox, w_branch_gdn=m_w_branch_gdn, w_out=m_w_out, norm_mix_w=m_norm_mix_w, norm_ffn_w=m_norm_ffn_w, ffn_w_up=m_ffn_w_up, ffn_conv_w=m_ffn_conv_w, ffn_conv_b=m_ffn_conv_b, ffn_w_down=m_ffn_w_down, norm_final_w=m_norm_final_w)
    v_in = dict(meta_tokens=v_meta_tokens, w_in=v_w_in, fgt_bias=v_fgt_bias, gdn_conv_w=v_gdn_conv_w, gdn_a_log=v_gdn_a_log, gdn_dt_bias=v_gdn_dt_bias, gdn_norm_w=v_gdn_norm_w, gate_bias=v_gate_bias, w_branch_fox=v_w_branch_fox, w_branch_gdn=v_w_branch_gdn, w_out=v_w_out, norm_mix_w=v_norm_mix_w, norm_ffn_w=v_norm_ffn_w, ffn_w_up=v_ffn_w_up, ffn_conv_w=v_ffn_conv_w, ffn_conv_b=v_ffn_conv_b, ffn_w_down=v_ffn_w_down, norm_final_w=v_norm_final_w)
    shard2d = {n: _shard_shape(s, a) for n, s, a in WEIGHTS}
    as2d = lambda d: {n: d[n].reshape(shard2d[n]) for n, _, _ in WEIGHTS}
    w2, m2, v2 = as2d(weights), as2d(m_in), as2d(v_in)

    full = _gather_weights(w2)
    w_fox, w_rest = _split_w_in(full["w_in"])
    flat = lambda n: w2[n].reshape(-1)
    local_w = dict(
        meta=full["meta_tokens"], w_fox=w_fox, w_rest=w_rest, fgt_bias=flat("fgt_bias"), gdn_conv=full["gdn_conv_w"],
        a_log=flat("gdn_a_log"), dt_bias=flat("gdn_dt_bias"), gdn_norm=flat("gdn_norm_w"), gate_bias=flat("gate_bias"),
        w_bfox=full["w_branch_fox"], w_bgdn=full["w_branch_gdn"], w_out=full["w_out"], norm_mix=flat("norm_mix_w"),
        norm_ffn=flat("norm_ffn_w"), w_up=full["ffn_w_up"], ffn_conv=full["ffn_conv_w"], ffn_conv_b=flat("ffn_conv_b"),
        w_down=full["ffn_w_down"], norm_final=flat("norm_final_w"))

    loss, grad_x, grads = _local_step(x[0], loss_target[0], local_w)

    g2 = {n: grads[n].reshape(s) for n, s, _ in WEIGHTS}
    send = [_by_chip(g2[n].astype(jnp.bfloat16), AXIS[n]) for n in BIG]
    send.append(jnp.stack([_pack([_shard_of(g2[n], AXIS[n], j) for n in SMALL], F32) for j in range(N_CHIPS)]))
    parts = [_sum_chips(r, "sum_" + n) for r, n in zip(_scatter_chips(send, "scatter_grads"), BIG + ("small",))]
    others = _sibling_swap(parts, "swap_grads")
    slab = lambda d: _pack([d[n] for n in SMALL], F32)
    state = [(w2[n], m2[n], v2[n]) for n in BIG] + [(slab(w2), slab(m2), slab(v2))]
    outs = [_adamw(w, m, v, p, q, "adamw_" + n) for (w, m, v), p, q, n in zip(state, parts, others, BIG + ("small",))]
    small = [_unpack(o, [weights[n].shape for n in SMALL]) for o in outs[-1]]
    result = []
    for kind in range(4):
        by_name = {n: outs[i][kind].reshape(weights[n].shape) for i, n in enumerate(BIG)}
        by_name.update(zip(SMALL, small[kind]))
        result += [by_name[n] for n, _, _ in WEIGHTS]
    total = lax.psum(loss[0, 0], ("x", "y", "c"))
    return (total, grad_x[None], *result)
```

```python
import functools

import numpy as np
import jax
import jax.numpy as jnp
from jax import lax
from jax.experimental import pallas as pl
from jax.experimental.pallas import tpu as pltpu

F32 = jnp.float32
_BF = jnp.bfloat16
HI = lax.Precision.HIGHEST

D = 1024
N_META = 16
CH = 64
FRONT = CH - N_META
X0 = CH
HEADS = 8
HD = 64
FW = HEADS * HD
DFF = 2816
EPS = 1e-6
NEG = -1e30
T_ALIGN = 256
SMALL_W = 128
REST_W = 3 * FW + FW + 2 * D + SMALL_W
OFF_Z = 3 * FW
OFF_GATES = 4 * FW
OFF_SMALL = 4 * FW + 2 * D

LR, B1, B2, AEPS, WD, STEP = 0.001, 0.9, 0.999, 1e-08, 0.01, 10

VMEM_LIMIT = 56 * 1024 * 1024
ROW_TILE = 640
MM_TM, MM_TN, MM_TK = 1280, 512, 2816
ATT_TILE = 640


def _div(n, target, mult=128):
    if n <= target:
        return n
    best = None
    for d in range(mult, target + 1, mult):
        if n % d == 0:
            best = d
    assert best is not None, (n, target, mult)
    return best


def _cp(*sem):
    return pltpu.CompilerParams(dimension_semantics=sem, vmem_limit_bytes=VMEM_LIMIT)


def _sig(x):
    return 1.0 / (1.0 + jnp.exp(-x))


MM_VMEM_BUDGET = 40 * 1024 * 1024
MM_STEP_BYTES = 1 << 20


def _mm_tiles(m, n, k, sa, sb, so, has_resid):
    divs = lambda d, cap: [t for t in range(128, min(d, cap) + 1, 128) if d % t == 0] or [d]
    best = None
    for tm in divs(m, MM_TM * 2):
        for tn in divs(n, 4608):
            fixed = tm * tn * (4 + 2 * so + (8 if has_resid else 0))
            tks = [t for t in divs(k, MM_TK) if 2 * t * (tm * sa + tn * sb) + fixed <= MM_VMEM_BUDGET]
            if not tks:
                continue
            tk = tks[-1]
            steps = (m // tm) * (n // tn) * (k // tk)
            rmw = (k // tk - 1) * m * n * 4
            cost = (n // tn) * m * k * sa + (m // tm) * k * n * sb + steps * MM_STEP_BYTES + rmw
            if best is None or cost < best[0]:
                best = (cost, tm, tn, tk)
    assert best is not None, (m, n, k)
    return best[1:]


def _mm_t(a, b, name):
    return _mm(a, b, ta=True, name=name)


def _mm(a, b, *, ta=False, tb=False, out_dtype=F32, resid=None, name):
    K, M = a.shape if ta else a.shape[::-1]
    N = b.shape[0] if tb else b.shape[1]
    assert (b.shape[1] if tb else b.shape[0]) == K
    tm, tn, tk = _mm_tiles(M, N, K, a.dtype.itemsize, b.dtype.itemsize, jnp.dtype(out_dtype).itemsize, resid is not None)
    nk = K // tk
    dims = (((0 if ta else 1,), (1 if tb else 0,)), ((), ()))
    mxu = _BF

    def body(*refs):
        if resid is None:
            a_ref, b_ref, o_ref, acc = refs
            r_ref = None
        else:
            a_ref, b_ref, r_ref, o_ref, acc = refs
        k = pl.program_id(2)
        part = lax.dot_general(a_ref[...].astype(mxu), b_ref[...].astype(mxu), dims, preferred_element_type=F32)

        @pl.when(k == 0)
        def _():
            acc[...] = part

        @pl.when(k > 0)
        def _():
            acc[...] += part

        @pl.when(k == nk - 1)
        def _():
            r = acc[...]
            if r_ref is not None:
                r = r + r_ref[...]
            o_ref[...] = r.astype(o_ref.dtype)

    a_spec = pl.BlockSpec((tk, tm), lambda i, j, k: (k, i)) if ta else pl.BlockSpec((tm, tk), lambda i, j, k: (i, k))
    b_spec = pl.BlockSpec((tn, tk), lambda i, j, k: (j, k)) if tb else pl.BlockSpec((tk, tn), lambda i, j, k: (k, j))
    o_spec = pl.BlockSpec((tm, tn), lambda i, j, k: (i, j))
    in_specs = [a_spec, b_spec] + ([o_spec] if resid is not None else [])
    args = (a, b) + ((resid,) if resid is not None else ())
    return pl.pallas_call(
        body, name=name, grid=(M // tm, N // tn, nk), in_specs=in_specs, out_specs=o_spec,
        out_shape=jax.ShapeDtypeStruct((M, N), out_dtype), scratch_shapes=[pltpu.VMEM((tm, tn), F32)],
        compiler_params=_cp("parallel", "parallel", "arbitrary"))(*args)


def _rms_fwd(h, w, name):
    T = h.shape[0]
    tb = _div(T, ROW_TILE, 8)

    def body(h_ref, w_ref, o_ref):
        x = h_ref[...]
        r = lax.rsqrt(jnp.mean(x * x, axis=-1, keepdims=True) + EPS)
        o_ref[...] = (x * r * w_ref[...]).astype(o_ref.dtype)

    return pl.pallas_call(
        body, name=name, grid=(T // tb,),
        in_specs=[pl.BlockSpec((tb, D), lambda i: (i, 0)), pl.BlockSpec((1, D), lambda i: (0, 0))],
        out_specs=pl.BlockSpec((tb, D), lambda i: (i, 0)), out_shape=jax.ShapeDtypeStruct((T, D), _BF),
        compiler_params=_cp("parallel"))(h, w)


def _rms_bwd(h, w, dy, dres, name):
    T = h.shape[0]
    tb = _div(T, ROW_TILE, 8)

    def body(h_ref, w_ref, dy_ref, dr_ref, dh_ref, dw_ref):
        x = h_ref[...]
        r = lax.rsqrt(jnp.mean(x * x, axis=-1, keepdims=True) + EPS)
        xh = x * r
        dy = dy_ref[...]
        g = dy * w_ref[...]
        dh_ref[...] = dr_ref[...] + r * (g - xh * jnp.mean(xh * g, axis=-1, keepdims=True))
        part = jnp.sum(dy * xh, axis=0, keepdims=True)

        @pl.when(pl.program_id(0) == 0)
        def _():
            dw_ref[...] = part

        @pl.when(pl.program_id(0) > 0)
        def _():
            dw_ref[...] += part

    row = pl.BlockSpec((tb, D), lambda i: (i, 0))
    one = pl.BlockSpec((1, D), lambda i: (0, 0))
    return pl.pallas_call(
        body, name=name, grid=(T // tb,), in_specs=[row, one, row, row], out_specs=[row, one],
        out_shape=[jax.ShapeDtypeStruct((T, D), F32), jax.ShapeDtypeStruct((1, D), F32)],
        compiler_params=_cp("arbitrary"))(h, w, dy, dres)


def _fgate_fwd(small, bias):
    T = small.shape[0]
    tb = _div(T, ROW_TILE, 8)

    def body(s_ref, b_ref, c_ref, carry):
        @pl.when(pl.program_id(0) == 0)
        def _():
            carry[...] = jnp.zeros_like(carry)

        lf = jax.nn.log_sigmoid(s_ref[...] + b_ref[...])
        r = lax.broadcasted_iota(jnp.int32, (tb, tb), 0)
        c = lax.broadcasted_iota(jnp.int32, (tb, tb), 1)
        tri = (r >= c).astype(F32)
        cs = jnp.dot(tri, lf, precision=HI, preferred_element_type=F32) + carry[...]
        c_ref[...] = cs
        carry[...] = cs[tb - 1:tb, :]

    return pl.pallas_call(
        body, name="fgate_fwd", grid=(T // tb,),
        in_specs=[pl.BlockSpec((tb, SMALL_W), lambda i: (i, 0)), pl.BlockSpec((1, SMALL_W), lambda i: (0, 0))],
        out_specs=pl.BlockSpec((tb, SMALL_W), lambda i: (i, 0)), out_shape=jax.ShapeDtypeStruct((T, SMALL_W), F32),
        scratch_shapes=[pltpu.VMEM((1, SMALL_W), F32)], compiler_params=_cp("arbitrary"))(small, bias)


def _fgate_bwd(small, bias, dc):
    T = small.shape[0]
    tb = _div(T, ROW_TILE, 8)
    nb = T // tb

    def body(s_ref, b_ref, dc_ref, df_ref, db_ref, carry):
        @pl.when(pl.program_id(0) == 0)
        def _():
            carry[...] = jnp.zeros_like(carry)

        r = lax.broadcasted_iota(jnp.int32, (tb, tb), 0)
        c = lax.broadcasted_iota(jnp.int32, (tb, tb), 1)
        tri = (r <= c).astype(F32)
        dlf = jnp.dot(tri, dc_ref[...], precision=HI, preferred_element_type=F32) + carry[...]
        carry[...] = dlf[0:1, :]
        df = dlf * _sig(-(s_ref[...] + b_ref[...]))
        df_ref[...] = df
        part = jnp.sum(df, axis=0, keepdims=True)

        @pl.when(pl.program_id(0) == 0)
        def _():
            db_ref[...] = part

        @pl.when(pl.program_id(0) > 0)
        def _():
            db_ref[...] += part

    rev = pl.BlockSpec((tb, SMALL_W), lambda i: (nb - 1 - i, 0))
    one = pl.BlockSpec((1, SMALL_W), lambda i: (0, 0))
    return pl.pallas_call(
        body, name="fgate_bwd", grid=(nb,), in_specs=[rev, one, rev], out_specs=[rev, one],
        out_shape=[jax.ShapeDtypeStruct((T, SMALL_W), F32), jax.ShapeDtypeStruct((1, SMALL_W), F32)],
        scratch_shapes=[pltpu.VMEM((1, SMALL_W), F32)], compiler_params=_cp("arbitrary"))(small, bias, dc)


def _att_rows(a8, tb):
    T = a8.shape[0]
    return a8.T.reshape(HEADS // 2, 2, T // tb, tb).transpose(0, 2, 1, 3)


def _att_cols(a8):
    return jnp.repeat(a8, HD, axis=1)


EXP_ZERO = -104.0
SKIP_SLACK = 2.0
NORM_SLACK = 1.02


def _att_stats(qkv, c_cols, ones_blk):
    T = qkv.shape[0]
    tb = _div(T, ATT_TILE)

    def body(q_ref, k_ref, c_ref, e_ref, o_ref):
        q = q_ref[...].astype(F32)
        k = k_ref[...].astype(F32)
        e = e_ref[...].astype(_BF)
        qn = jnp.max(jnp.dot((q * q).astype(_BF), e, preferred_element_type=F32), axis=0, keepdims=True)
        kn = jnp.max(jnp.dot((k * k).astype(_BF), e, preferred_element_type=F32), axis=0, keepdims=True)
        c = c_ref[...]
        o_ref[0] = jnp.concatenate([jnp.sqrt(qn), jnp.sqrt(kn), jnp.max(c, axis=0, keepdims=True),
                                    jnp.min(c, axis=0, keepdims=True), jnp.zeros((4, FW), F32)], axis=0)

    blk = lambda off: pl.BlockSpec((tb, FW), lambda i: (i, off))
    return pl.pallas_call(
        body, name="att_stats", grid=(T // tb,),
        in_specs=[blk(0), blk(1), blk(0), pl.BlockSpec((FW, FW), lambda i: (0, 0))],
        out_specs=pl.BlockSpec((1, 8, FW), lambda i: (i, 0, 0)), out_shape=jax.ShapeDtypeStruct((T // tb, 8, FW), F32),
        compiler_params=_cp("parallel"))(qkv, qkv, c_cols, ones_blk)


def _att_plan(stats):
    nb = stats.shape[0]
    st = stats[:, :4, ::HD]
    qmax, kmax, cmax, cmin = (st[:, r, :].T for r in range(4))
    bound = (HD ** -0.5) * NORM_SLACK * qmax[:, :, None] * (kmax[:, None, :] + kmax[:, :, None]) \
        + cmax[:, :, None] - cmin[:, None, :] + SKIP_SLACK
    ii = lax.broadcasted_iota(jnp.int32, (nb, nb), 0)
    jj = lax.broadcasted_iota(jnp.int32, (nb, nb), 1)
    skip = (bound < EXP_ZERO) & (jj < ii)[None]
    live = (~skip & (jj <= ii)[None]).reshape(HEADS // 2, 2, nb, nb).any(axis=1)
    jfirst = jnp.argmax(live, axis=2).astype(jnp.int32)
    ilast = (nb - 1 - jnp.argmax(live[:, ::-1, :], axis=1)).astype(jnp.int32)
    return skip.astype(jnp.int32).reshape(-1), jfirst.reshape(-1), ilast.reshape(-1)


def _fox_fwd(qkv, c_cols, c_rows, plan):
    T = qkv.shape[0]
    tb = _div(T, ATT_TILE)
    nb = T // tb
    npair = HEADS // 2
    scale = HD ** -0.5
    dn = (((1,), (1,)), ((), ()))

    def body(skip_ref, jfirst_ref, ilast_ref, q_ref, k_ref, v_ref, cq_ref, ck_ref, o_ref, l_ref, m_scr, acc_scr):
        pr = pl.program_id(0)
        i = pl.program_id(1)
        q = q_ref[...]
        lane = lax.broadcasted_iota(jnp.int32, (1, 2 * HD), 1)
        sel0 = lane < HD
        zero = jnp.zeros_like(q)
        qh = (jnp.where(sel0, q, zero) * scale, jnp.where(sel0, zero, q) * scale)
        cq = (cq_ref[:, 0:1], cq_ref[:, HD:HD + 1])
        row = i * tb + lax.broadcasted_iota(jnp.int32, (tb, 1), 0)
        m_scr[...] = jnp.full(m_scr.shape, NEG, F32)
        acc_scr[...] = jnp.zeros(acc_scr.shape, F32)

        def tile(j, h, masked):
            start = pl.multiple_of(j * tb, tb)
            kb = k_ref[pl.ds(start, tb), :]
            vb = v_ref[pl.ds(start, tb), :]
            one = jnp.ones_like(vb)
            vh = jnp.where(sel0, vb, one) if h == 0 else jnp.where(sel0, one, vb)
            t = lax.dot_general(qh[h], kb, dn, preferred_element_type=F32) - ck_ref[0, j, h:h + 1, :]
            if masked:
                col = j * tb + lax.broadcasted_iota(jnp.int32, (1, tb), 1)
                mask = (col <= row) & (col >= FRONT)
                t = jnp.where(mask, t, NEG)
            m = m_scr[h]
            m_new = jnp.maximum(m, jnp.max(t, axis=-1, keepdims=True) + cq[h])
            p = jnp.exp(t + (cq[h] - m_new))
            if masked:
                p = jnp.where(mask, p, 0.0)
            m_scr[h] = m_new
            acc_scr[h] = acc_scr[h] * jnp.exp(m - m_new) + jnp.dot(p.astype(vb.dtype), vh, preferred_element_type=F32)

        def step(j, _):
            edge = (j == 0) | (j == i)
            for h in range(2):
                live = skip_ref[((2 * pr + h) * nb + i) * nb + j] == 0
                pl.when(live & edge)(functools.partial(tile, j, h, True))
                pl.when(live & jnp.logical_not(edge))(functools.partial(tile, j, h, False))
            return 0

        lax.fori_loop(jfirst_ref[pr * nb + i], i + 1, step, 0)
        outs, lses = [], []
        for h in range(2):
            l = acc_scr[h][:, (1 - h) * HD:(1 - h) * HD + 1]
            ok = l > 0.0
            ls = jnp.where(ok, l, 1.0)
            outs.append(jnp.where(ok, acc_scr[h] / ls, 0.0))
            lses.append(jnp.where(ok, m_scr[h] + jnp.log(ls), 0.0))
        o_ref[...] = jnp.where(sel0, outs[0], outs[1])
        l_ref[...] = jnp.where(sel0, lses[0], lses[1])

    blk = lambda off: pl.BlockSpec((tb, 2 * HD), lambda p, i, *_: (i, off + p))
    full = lambda off: pl.BlockSpec((T, 2 * HD), lambda p, i, *_: (0, off + p))
    rows = pl.BlockSpec((1, nb, 2, tb), lambda p, i, *_: (p, 0, 0, 0))
    return pl.pallas_call(
        body, name="fox_fwd",
        grid_spec=pltpu.PrefetchScalarGridSpec(
            num_scalar_prefetch=3, grid=(npair, nb),
            in_specs=[blk(0), full(npair), full(2 * npair), blk(0), rows], out_specs=[blk(0), blk(0)],
            scratch_shapes=[pltpu.VMEM((2, tb, 1), F32), pltpu.VMEM((2, tb, 2 * HD), F32)]),
        out_shape=[jax.ShapeDtypeStruct((T, FW), F32), jax.ShapeDtypeStruct((T, FW), F32)],
        compiler_params=_cp("parallel", "arbitrary"))(*plan, qkv, qkv, qkv, c_cols, c_rows)


def _fox_bwd_dq(qkv, do, c_cols, c_rows, lse, delta, plan):
    T = qkv.shape[0]
    tb = _div(T, ATT_TILE)
    nb = T // tb
    npair = HEADS // 2
    scale = HD ** -0.5
    dn = (((1,), (1,)), ((), ()))

    def body(skip_ref, jfirst_ref, ilast_ref, q_ref, k_ref, v_ref, do_ref, cq_ref, ck_ref, l_ref, dl_ref, dq_ref, dc_ref,
             dq_scr):
        pr = pl.program_id(0)
        i = pl.program_id(1)
        q = q_ref[...]
        do = do_ref[...]
        lane = lax.broadcasted_iota(jnp.int32, (1, 2 * HD), 1)
        sel0 = lane < HD
        qh = (jnp.where(sel0, q, jnp.zeros_like(q)) * scale, jnp.where(sel0, jnp.zeros_like(q), q) * scale)
        doh = (jnp.where(sel0, do, jnp.zeros_like(do)), jnp.where(sel0, jnp.zeros_like(do), do))
        ce = (cq_ref[:, 0:1] - l_ref[:, 0:1], cq_ref[:, HD:HD + 1] - l_ref[:, HD:HD + 1])
        dl = (dl_ref[:, 0:1], dl_ref[:, HD:HD + 1])
        row = i * tb + lax.broadcasted_iota(jnp.int32, (tb, 1), 0)
        dq_scr[...] = jnp.zeros(dq_scr.shape, F32)

        def tile(j, h, masked):
            start = pl.multiple_of(j * tb, tb)
            kb = k_ref[pl.ds(start, tb), :]
            vb = v_ref[pl.ds(start, tb), :]
            one = jnp.ones_like(kb)
            kh = jnp.where(sel0, kb, one) if h == 0 else jnp.where(sel0, one, kb)
            t = lax.dot_general(qh[h], kb, dn, preferred_element_type=F32) - ck_ref[0, j, h:h + 1, :]
            if masked:
                col = j * tb + lax.broadcasted_iota(jnp.int32, (1, tb), 1)
                mask = (col <= row) & (col >= FRONT)
                p = jnp.where(mask, jnp.exp(jnp.where(mask, t, NEG) + ce[h]), 0.0)
            else:
                p = jnp.exp(t + ce[h])
            dp = lax.dot_general(doh[h], vb, dn, preferred_element_type=F32)
            ds = p * (dp - dl[h])
            dq_scr[h] += jnp.dot(ds.astype(kb.dtype), kh, preferred_element_type=F32)

        def step(j, _):
            edge = (j == 0) | (j == i)
            for h in range(2):
                live = skip_ref[((2 * pr + h) * nb + i) * nb + j] == 0
                pl.when(live & edge)(functools.partial(tile, j, h, True))
                pl.when(live & jnp.logical_not(edge))(functools.partial(tile, j, h, False))
            return 0

        lax.fori_loop(jfirst_ref[pr * nb + i], i + 1, step, 0)
        dq_ref[...] = (jnp.where(sel0, dq_scr[0], dq_scr[1]) * scale).astype(dq_ref.dtype)
        dc_ref[...] = jnp.where(sel0, dq_scr[0][:, HD:HD + 1], dq_scr[1][:, 0:1])

    blk = lambda off: pl.BlockSpec((tb, 2 * HD), lambda p, i, *_: (i, off + p))
    full = lambda off: pl.BlockSpec((T, 2 * HD), lambda p, i, *_: (0, off + p))
    rows = pl.BlockSpec((1, nb, 2, tb), lambda p, i, *_: (p, 0, 0, 0))
    return pl.pallas_call(
        body, name="fox_bwd_dq",
        grid_spec=pltpu.PrefetchScalarGridSpec(
            num_scalar_prefetch=3, grid=(npair, nb),
            in_specs=[blk(0), full(npair), full(2 * npair), blk(0), blk(0), rows, blk(0), blk(0)], out_specs=[blk(0), blk(0)],
            scratch_shapes=[pltpu.VMEM((2, tb, 2 * HD), F32)]),
        out_shape=[jax.ShapeDtypeStruct((T, FW), _BF), jax.ShapeDtypeStruct((T, FW), F32)],
        compiler_params=_cp("parallel", "arbitrary"))(*plan, qkv, qkv, qkv, do, c_cols, c_rows, lse, delta)


def _fox_bwd_dkv(qkv, do, c_cols, ce_rows, delta_rows, plan):
    T = qkv.shape[0]
    tb = _div(T, ATT_TILE)
    nb = T // tb
    npair = HEADS // 2
    scale = HD ** -0.5
    dn = (((1,), (1,)), ((), ()))

    def body(skip_ref, jfirst_ref, ilast_ref, q_ref, k_ref, v_ref, do_ref, ck_ref, ce_ref, dl_ref,
             dk_ref, dv_ref, dc_ref, dk_scr, dv_scr):
        pr = pl.program_id(0)
        jb = pl.program_id(1)
        k = k_ref[...]
        v = v_ref[...]
        lane = lax.broadcasted_iota(jnp.int32, (1, 2 * HD), 1)
        sel0 = lane < HD
        kh = (jnp.where(sel0, k, jnp.zeros_like(k)) * scale, jnp.where(sel0, jnp.zeros_like(k), k) * scale)
        vh = (jnp.where(sel0, v, jnp.zeros_like(v)), jnp.where(sel0, jnp.zeros_like(v), v))
        ck = (ck_ref[:, 0:1], ck_ref[:, HD:HD + 1])
        kidx = jb * tb + lax.broadcasted_iota(jnp.int32, (tb, 1), 0)
        dk_scr[...] = jnp.zeros(dk_scr.shape, F32)
        dv_scr[...] = jnp.zeros(dv_scr.shape, F32)

        def tile(i, h, masked):
            start = pl.multiple_of(i * tb, tb)
            qb = q_ref[pl.ds(start, tb), :]
            dob = do_ref[pl.ds(start, tb), :]
            one = jnp.ones_like(qb)
            qh = jnp.where(sel0, qb, one) if h == 0 else jnp.where(sel0, one, qb)
            t = lax.dot_general(kh[h], qb, dn, preferred_element_type=F32) - ck[h]
            ce = ce_ref[0, i, h:h + 1, :]
            if masked:
                qidx = i * tb + lax.broadcasted_iota(jnp.int32, (1, tb), 1)
                mask = (kidx <= qidx) & (kidx >= FRONT)
                pt = jnp.where(mask, jnp.exp(jnp.where(mask, t, NEG) + ce), 0.0)
            else:
                pt = jnp.exp(t + ce)
            dv_scr[h] += jnp.dot(pt.astype(dob.dtype), dob, preferred_element_type=F32)
            dpt = lax.dot_general(vh[h], dob, dn, preferred_element_type=F32)
            dst = pt * (dpt - dl_ref[0, i, h:h + 1, :])
            dk_scr[h] += jnp.dot(dst.astype(qb.dtype), qh, preferred_element_type=F32)

        def step(i, _):
            edge = (i == jb) | (jb == 0)
            for h in range(2):
                live = skip_ref[((2 * pr + h) * nb + i) * nb + jb] == 0
                pl.when(live & edge)(functools.partial(tile, i, h, True))
                pl.when(live & jnp.logical_not(edge))(functools.partial(tile, i, h, False))
            return 0

        lax.fori_loop(jb, ilast_ref[pr * nb + jb] + 1, step, 0)
        dk_ref[...] = (jnp.where(sel0, dk_scr[0], dk_scr[1]) * scale).astype(dk_ref.dtype)
        dv_ref[...] = jnp.where(sel0, dv_scr[0], dv_scr[1]).astype(dv_ref.dtype)
        dc_ref[...] = -jnp.where(sel0, dk_scr[0][:, HD:HD + 1], dk_scr[1][:, 0:1])

    blk = lambda off: pl.BlockSpec((tb, 2 * HD), lambda p, j, *_: (j, off + p))
    full = lambda off: pl.BlockSpec((T, 2 * HD), lambda p, j, *_: (0, off + p))
    rows = pl.BlockSpec((1, nb, 2, tb), lambda p, j, *_: (p, 0, 0, 0))
    return pl.pallas_call(
        body, name="fox_bwd_dkv",
        grid_spec=pltpu.PrefetchScalarGridSpec(
            num_scalar_prefetch=3, grid=(npair, nb),
            in_specs=[full(0), blk(npair), blk(2 * npair), full(0), blk(0), rows, rows],
            out_specs=[blk(0), blk(0), blk(0)],
            scratch_shapes=[pltpu.VMEM((2, tb, 2 * HD), F32), pltpu.VMEM((2, tb, 2 * HD), F32)]),
        out_shape=[jax.ShapeDtypeStruct((T, FW), _BF)] * 2 + [jax.ShapeDtypeStruct((T, FW), F32)],
        compiler_params=_cp("parallel", "arbitrary"))(*plan, qkv, qkv, qkv, do, c_cols, ce_rows, delta_rows)


def _head_dot(a, b, ones_blk):
    T = a.shape[0]
    tb = _div(T, ROW_TILE, 8)

    def body(a_ref, b_ref, e_ref, o_ref):
        prod = a_ref[...].astype(F32) * b_ref[...].astype(F32)
        o_ref[...] = jnp.dot(prod, e_ref[...], precision=HI, preferred_element_type=F32)

    row = pl.BlockSpec((tb, FW), lambda i: (i, 0))
    return pl.pallas_call(
        body, name="head_dot", grid=(T // tb,), in_specs=[row, row, pl.BlockSpec((FW, FW), lambda i: (0, 0))],
        out_specs=row, out_shape=jax.ShapeDtypeStruct((T, FW), F32), compiler_params=_cp("parallel"))(a, b, ones_blk)


CONV_RC = 32
CONV_TC = 256


def _stage_prev(scr, x_ref, h_ref, first, tb):
    scr[0:8, :] = jnp.where(first, 0.0, h_ref[...])
    scr[8:8 + tb, :] = x_ref[...]


def _windows_prev(scr, r, kw):
    x = scr[pl.ds(pl.multiple_of(r * CONV_RC, CONV_RC), CONV_RC + 8), :]
    return [x[8:] if k == kw - 1 else pltpu.roll(x, kw - 1 - k, 0)[8:] for k in range(kw)]


def _fold8(a):
    return a.reshape(CONV_RC // 8, 8, a.shape[-1]).sum(axis=0)


def _halo_prev(tb, tc, off=0):
    return pl.BlockSpec((8, tc), lambda j, i: (jnp.maximum(i * (tb // 8) - 1, 0), j + off))


def _gconv_fwd(rest, w):
    T = rest.shape[0]
    C = 3 * FW
    kw = w.shape[0]
    tb, tc = _div(T, 1280, CONV_RC), CONV_TC

    def body(x_ref, h_ref, w_ref, o_ref, scr):
        _stage_prev(scr, x_ref, h_ref, pl.program_id(1) == 0, tb)
        wv = w_ref[...]

        def chunk(r, _):
            win = _windows_prev(scr, r, kw)
            u = sum(wv[k:k + 1, :] * win[k] for k in range(kw))
            o_ref[pl.ds(pl.multiple_of(r * CONV_RC, CONV_RC), CONV_RC), :] = u * _sig(u)
            return 0

        lax.fori_loop(0, tb // CONV_RC, chunk, 0)

    return pl.pallas_call(
        body, name="gconv_fwd", grid=(C // tc, T // tb),
        in_specs=[pl.BlockSpec((tb, tc), lambda j, i: (i, j)), _halo_prev(tb, tc), pl.BlockSpec((kw, tc), lambda j, i: (0, j))],
        out_specs=pl.BlockSpec((tb, tc), lambda j, i: (i, j)), out_shape=jax.ShapeDtypeStruct((T, C), F32),
        scratch_shapes=[pltpu.VMEM((tb + 8, tc), F32)], compiler_params=_cp("parallel", "arbitrary"))(rest, rest, w)


def _gconv_bwd_du(rest, w, dy):
    T = rest.shape[0]
    C = 3 * FW
    kw = w.shape[0]
    tb, tc = _div(T, 1280, CONV_RC), CONV_TC

    def body(x_ref, h_ref, w_ref, dy_ref, du_ref, dw_ref, scr):
        i = pl.program_id(1)
        _stage_prev(scr, x_ref, h_ref, i == 0, tb)
        wv = w_ref[...]

        def chunk(r, acc):
            rows = pl.ds(pl.multiple_of(r * CONV_RC, CONV_RC), CONV_RC)
            win = _windows_prev(scr, r, kw)
            u = sum(wv[k:k + 1, :] * win[k] for k in range(kw))
            sg = _sig(u)
            du = dy_ref[rows, :] * sg * (1.0 + u * (1.0 - sg))
            du_ref[rows, :] = du
            return tuple(acc[k] + _fold8(du * win[k]) for k in range(kw))

        acc = lax.fori_loop(0, tb // CONV_RC, chunk, tuple(jnp.zeros((8, tc), F32) for _ in range(kw)))
        part = jnp.concatenate([jnp.sum(a, axis=0, keepdims=True) for a in acc], axis=0)

        @pl.when(i == 0)
        def _():
            dw_ref[...] = part

        @pl.when(i > 0)
        def _():
            dw_ref[...] += part

    blk = pl.BlockSpec((tb, tc), lambda j, i: (i, j))
    wsp = pl.BlockSpec((kw, tc), lambda j, i: (0, j))
    return pl.pallas_call(
        body, name="gconv_bwd_du", grid=(C // tc, T // tb), in_specs=[blk, _halo_prev(tb, tc), wsp, blk],
        out_specs=[blk, wsp], out_shape=[jax.ShapeDtypeStruct((T, C), F32), jax.ShapeDtypeStruct((kw, C), F32)],
        scratch_shapes=[pltpu.VMEM((tb + 8, tc), F32)], compiler_params=_cp("parallel", "arbitrary"))(rest, rest, w, dy)


def _conv_bwd_dx(du, w, out_dtype, name):
    T, C = du.shape
    kw = w.shape[0]
    tb = _div(T, 1280, CONV_RC)
    tc = CONV_TC
    nb = T // tb

    def body(x_ref, h_ref, w_ref, o_ref, scr):
        scr[0:tb, :] = x_ref[...]
        scr[tb:tb + 8, :] = jnp.where(pl.program_id(1) == nb - 1, 0.0, h_ref[...])
        wv = w_ref[...]

        def chunk(r, _):
            start = pl.multiple_of(r * CONV_RC, CONV_RC)
            x = scr[pl.ds(start, CONV_RC + 8), :]
            acc = wv[kw - 1:kw, :] * x[:CONV_RC]
            for k in range(kw - 1):
                acc = acc + wv[k:k + 1, :] * pltpu.roll(x, CONV_RC + 8 - (kw - 1 - k), 0)[:CONV_RC]
            o_ref[pl.ds(start, CONV_RC), :] = acc.astype(o_ref.dtype)
            return 0

        lax.fori_loop(0, tb // CONV_RC, chunk, 0)

    halo = pl.BlockSpec((8, tc), lambda j, i: (jnp.minimum((i + 1) * (tb // 8), T // 8 - 1), j))
    return pl.pallas_call(
        body, name=name, grid=(C // tc, nb),
        in_specs=[pl.BlockSpec((tb, tc), lambda j, i: (i, j)), halo, pl.BlockSpec((kw, tc), lambda j, i: (0, j))],
        out_specs=pl.BlockSpec((tb, tc), lambda j, i: (i, j)), out_shape=jax.ShapeDtypeStruct((T, C), out_dtype),
        scratch_shapes=[pltpu.VMEM((tb + 8, tc), F32)], compiler_params=_cp("parallel", "arbitrary"))(du, du, w)


def _glu_fwd(up, w, b):
    T = up.shape[0]
    kw = w.shape[0]
    tb, tc = _div(T, 1280, CONV_RC), CONV_TC
    nc = DFF // tc

    def body(xg, hg, xu, hu, wg, wu, bg, bu, o_ref, sg, su):
        first = pl.program_id(1) == 0
        _stage_prev(sg, xg, hg, first, tb)
        _stage_prev(su, xu, hu, first, tb)
        wgv, wuv, bgv, buv = wg[...], wu[...], bg[...], bu[...]

        def chunk(r, _):
            wing, winu = _windows_prev(sg, r, kw), _windows_prev(su, r, kw)
            ug = bgv + sum(wgv[k:k + 1, :] * wing[k] for k in range(kw))
            uu = buv + sum(wuv[k:k + 1, :] * winu[k] for k in range(kw))
            o_ref[pl.ds(pl.multiple_of(r * CONV_RC, CONV_RC), CONV_RC), :] = (ug * _sig(ug) * uu).astype(o_ref.dtype)
            return 0

        lax.fori_loop(0, tb // CONV_RC, chunk, 0)

    blk = lambda off: pl.BlockSpec((tb, tc), lambda j, i: (i, j + off))
    wsp = lambda off: pl.BlockSpec((kw, tc), lambda j, i: (0, j + off))
    bsp = lambda off: pl.BlockSpec((1, tc), lambda j, i: (0, j + off))
    return pl.pallas_call(
        body, name="glu_fwd", grid=(nc, T // tb),
        in_specs=[blk(0), _halo_prev(tb, tc), blk(nc), _halo_prev(tb, tc, nc), wsp(0), wsp(nc), bsp(0), bsp(nc)],
        out_specs=blk(0), out_shape=jax.ShapeDtypeStruct((T, DFF), _BF),
        scratch_shapes=[pltpu.VMEM((tb + 8, tc), F32)] * 2, compiler_params=_cp("parallel", "arbitrary"))(
            up, up, up, up, w, w, b, b)


def _glu_bwd_du(up, w, b, df):
    T = up.shape[0]
    kw = w.shape[0]
    tb, tc = _div(T, 1280, CONV_RC), CONV_TC
    nc = DFF // tc

    def body(xg, hg, xu, hu, wg, wu, bg, bu, df_ref, dug_ref, duu_ref, dwg_ref, dwu_ref, dbg_ref, dbu_ref, sg, su):
        i = pl.program_id(1)
        _stage_prev(sg, xg, hg, i == 0, tb)
        _stage_prev(su, xu, hu, i == 0, tb)
        wgv, wuv, bgv, buv = wg[...], wu[...], bg[...], bu[...]

        def chunk(r, acc):
            rows = pl.ds(pl.multiple_of(r * CONV_RC, CONV_RC), CONV_RC)
            wing, winu = _windows_prev(sg, r, kw), _windows_prev(su, r, kw)
            ug = bgv + sum(wgv[k:k + 1, :] * wing[k] for k in range(kw))
            uu = buv + sum(wuv[k:k + 1, :] * winu[k] for k in range(kw))
            s = _sig(ug)
            df = df_ref[rows, :]
            dug = df * uu * s * (1.0 + ug * (1.0 - s))
            duu = df * ug * s
            dug_ref[rows, :] = dug
            duu_ref[rows, :] = duu
            new = [dug * wing[k] for k in range(kw)] + [duu * winu[k] for k in range(kw)] + [dug, duu]
            return tuple(a + _fold8(v) for a, v in zip(acc, new))

        acc = lax.fori_loop(0, tb // CONV_RC, chunk, tuple(jnp.zeros((8, tc), F32) for _ in range(2 * kw + 2)))
        col = [jnp.sum(a, axis=0, keepdims=True) for a in acc]
        parts = (jnp.concatenate(col[:kw], axis=0), jnp.concatenate(col[kw:2 * kw], axis=0), col[2 * kw], col[2 * kw + 1])
        accs = (dwg_ref, dwu_ref, dbg_ref, dbu_ref)

        @pl.when(i == 0)
        def _():
            for r, p in zip(accs, parts):
                r[...] = p

        @pl.when(i > 0)
        def _():
            for r, p in zip(accs, parts):
                r[...] += p

    blk = lambda off: pl.BlockSpec((tb, tc), lambda j, i: (i, j + off))
    wsp = lambda off: pl.BlockSpec((kw, tc), lambda j, i: (0, j + off))
    bsp = lambda off: pl.BlockSpec((1, tc), lambda j, i: (0, j + off))
    return pl.pallas_call(
        body, name="glu_bwd_du", grid=(nc, T // tb),
        in_specs=[blk(0), _halo_prev(tb, tc), blk(nc), _halo_prev(tb, tc, nc), wsp(0), wsp(nc), bsp(0), bsp(nc), blk(0)],
        out_specs=[blk(0), blk(0), wsp(0), wsp(0), bsp(0), bsp(0)],
        out_shape=[jax.ShapeDtypeStruct((T, DFF), F32)] * 2 + [jax.ShapeDtypeStruct((kw, DFF), F32)] * 2
        + [jax.ShapeDtypeStruct((1, DFF), F32)] * 2,
        scratch_shapes=[pltpu.VMEM((tb + 8, tc), F32)] * 2, compiler_params=_cp("parallel", "arbitrary"))(
            up, up, up, up, w, w, b, b, df)


def _mix_fwd(rest, gate_bias, y_fox, y_gdn):
    T = rest.shape[0]
    tb, tc = _div(T, ROW_TILE, 8), 512
    nc = D // tc
    og = OFF_GATES // tc

    def body(gf, gg, bf, bg, yf, yg, o_ref):
        o_ref[...] = (_sig(gf[...] + bf[...]) * yf[...] + _sig(gg[...] + bg[...]) * yg[...]).astype(o_ref.dtype)

    blk = lambda off: pl.BlockSpec((tb, tc), lambda i, j: (i, j + off))
    bsp = lambda off: pl.BlockSpec((1, tc), lambda i, j: (0, j + off))
    return pl.pallas_call(
        body, name="mix_fwd", grid=(T // tb, nc),
        in_specs=[blk(og), blk(og + nc), bsp(0), bsp(nc), blk(0), blk(0)], out_specs=blk(0),
        out_shape=jax.ShapeDtypeStruct((T, D), _BF), compiler_params=_cp("parallel", "parallel"))(
            rest, rest, gate_bias, gate_bias, y_fox, y_gdn)


def _mix_bwd(rest, gate_bias, y_fox, y_gdn, dmix):
    T = rest.shape[0]
    tb, tc = _div(T, ROW_TILE, 8), 512
    nc = D // tc
    og = OFF_GATES // tc

    def body(gf, gg, bf, bg, yf, yg, dm, dyf, dyg, dgf, dgg, dbf, dbg):
        i = pl.program_id(1)
        d = dm[...]
        sf = _sig(gf[...] + bf[...])
        sg = _sig(gg[...] + bg[...])
        dyf[...] = (d * sf).astype(dyf.dtype)
        dyg[...] = (d * sg).astype(dyg.dtype)
        a = d * yf[...] * sf * (1.0 - sf)
        b = d * yg[...] * sg * (1.0 - sg)
        dgf[...] = a.astype(dgf.dtype)
        dgg[...] = b.astype(dgg.dtype)
        pa = jnp.sum(a, axis=0, keepdims=True)
        pb = jnp.sum(b, axis=0, keepdims=True)

        @pl.when(i == 0)
        def _():
            dbf[...] = pa
            dbg[...] = pb

        @pl.when(i > 0)
        def _():
            dbf[...] += pa
            dbg[...] += pb

    blk = lambda off: pl.BlockSpec((tb, tc), lambda j, i: (i, j + off))
    bsp = lambda off: pl.BlockSpec((1, tc), lambda j, i: (0, j + off))
    return pl.pallas_call(
        body, name="mix_bwd", grid=(nc, T // tb),
        in_specs=[blk(og), blk(og + nc), bsp(0), bsp(nc), blk(0), blk(0), blk(0)],
        out_specs=[blk(0), blk(0), blk(0), blk(0), bsp(0), bsp(0)],
        out_shape=[jax.ShapeDtypeStruct((T, D), _BF)] * 4 + [jax.ShapeDtypeStruct((1, D), F32)] * 2,
        compiler_params=_cp("parallel", "arbitrary"))(rest, rest, gate_bias, gate_bias, y_fox, y_gdn, dmix)


def _loss_head(h2, w, target, n_valid):
    T = h2.shape[0]
    tb = _div(T, ROW_TILE, 8)

    def body(h_ref, w_ref, t_ref, dh_ref, loss_ref, dw_ref):
        i = pl.program_id(0)
        x = h_ref[...]
        r = lax.rsqrt(jnp.mean(x * x, axis=-1, keepdims=True) + EPS)
        xh = x * r
        row = i * tb + lax.broadcasted_iota(jnp.int32, (tb, 1), 0)
        valid = (row >= X0) & (row < X0 + n_valid)
        e = jnp.where(valid, xh * w_ref[...] - t_ref[...], 0.0)
        dy = e * (1.0 / D)
        g = dy * w_ref[...]
        dh_ref[...] = r * (g - xh * jnp.mean(xh * g, axis=-1, keepdims=True))
        lpart = 0.5 * jnp.sum(jnp.sum(e * e, axis=-1, keepdims=True) * (1.0 / D), axis=0, keepdims=True)
        wpart = jnp.sum(dy * xh, axis=0, keepdims=True)

        @pl.when(i == 0)
        def _():
            loss_ref[...] = lpart
            dw_ref[...] = wpart

        @pl.when(i > 0)
        def _():
            loss_ref[...] += lpart
            dw_ref[...] += wpart

    row = pl.BlockSpec((tb, D), lambda i: (i, 0))
    one = pl.BlockSpec((1, D), lambda i: (0, 0))
    return pl.pallas_call(
        body, name="loss_head", grid=(T // tb,), in_specs=[row, one, row],
        out_specs=[row, pl.BlockSpec((1, 1), lambda i: (0, 0)), one],
        out_shape=[jax.ShapeDtypeStruct((T, D), F32), jax.ShapeDtypeStruct((1, 1), F32), jax.ShapeDtypeStruct((1, D), F32)],
        compiler_params=_cp("arbitrary"))(h2, w, target)


def _bd_lo(a, b, ca, cb):
    return lax.dot_general(a.astype(_BF), b.astype(_BF), (((ca,), (cb,)), ((0,), (0,))), preferred_element_type=F32)


def _split2(a):
    hi = a.astype(_BF)
    return hi, (a - hi.astype(F32)).astype(_BF)


def _bd_hi(a, b, ca, cb, exact_a=False):
    dn = (((ca,), (cb,)), ((0,), (0,)))
    dot = lambda x, y: lax.dot_general(x, y, dn, preferred_element_type=F32)
    bh, bl = _split2(b)
    if exact_a:
        ah = a.astype(_BF)
        return dot(ah, bh) + dot(ah, bl)
    ah, al = _split2(a)
    return dot(ah, bh) + (dot(ah, bl) + dot(al, bh))


def _row_to_col(x):
    eye = lax.broadcasted_iota(jnp.int32, (1, CH, CH), 1) == lax.broadcasted_iota(jnp.int32, (1, CH, CH), 2)
    return jnp.sum(jnp.where(eye, jnp.broadcast_to(x, (HEADS, CH, CH)), 0.0), axis=2, keepdims=True)


def _col_to_row(x):
    eye = lax.broadcasted_iota(jnp.int32, (1, CH, CH), 1) == lax.broadcasted_iota(jnp.int32, (1, CH, CH), 2)
    return jnp.sum(jnp.where(eye, jnp.broadcast_to(x, (HEADS, CH, CH)), 0.0), axis=1, keepdims=True)


def _gdn_chunk(q, k, v, bpre, apre, alog, dtb):
    H = HEADS
    r = lax.broadcasted_iota(jnp.int32, (1, CH, CH), 1)
    c = lax.broadcasted_iota(jnp.int32, (1, CH, CH), 2)
    tril, strict = r >= c, r > c
    lb = jnp.broadcast_to(tril.astype(F32), (H, CH, CH))
    rq = lax.rsqrt(jnp.sum(q * q, axis=-1, keepdims=True) + EPS)
    rk = lax.rsqrt(jnp.sum(k * k, axis=-1, keepdims=True) + EPS)
    qh = q * rq
    qn = qh * (HD ** -0.5)
    kn = k * rk
    beta = _sig(bpre)
    x = apre + dtb
    ea = jnp.exp(alog)
    g = -ea * jax.nn.softplus(x)
    gb = jnp.broadcast_to(g, (H, CH, HD))
    gc = _bd_hi(lb, gb, 2, 1, True)
    dm = gc - _col_to_row(gc[:, :, 0:1])
    decay = jnp.where(tril, jnp.exp(jnp.where(tril, dm, 0.0)), 0.0)
    eg = jnp.exp(gc)
    gl = gc[:, CH - 1:CH, :]
    egl = jnp.exp(gl - gc)
    cd = jnp.exp(gl)
    kb = kn * beta
    vb = v * beta
    both = _bd_lo(_rows2(kb, qn), kn, 2, 2)
    kk, qk = both[:, :CH], both[:, CH:]
    pw = -jnp.where(strict, kk * decay, 0.0)
    tm = jnp.where(r == c, 1.0, 0.0) + pw
    pw = _bd_hi(pw, pw, 2, 1)
    for it in range(4):
        mul = _bd_hi if it < 2 else _bd_lo
        both = mul(_rows2(pw, tm), pw, 2, 1)
        pw, tm = both[:, :CH], tm + both[:, CH:]
    tm = tm + _bd_lo(tm, pw, 2, 1)
    kbg = kb * eg
    sol = _bd_hi(tm, _lanes2(vb, kbg), 2, 1)
    value, kcd = sol[:, :, :HD], sol[:, :, HD:]
    attn = jnp.where(tril, qk * decay, 0.0)
    return dict(tril=tril, strict=strict, lb=lb, rq=rq, rk=rk, qh=qh, qn=qn, kn=kn, beta=beta, x=x, ea=ea, g=g,
                decay=decay, eg=eg, egl=egl, cd=cd, kb=kb, vb=vb, kk=kk, tm=tm, kbg=kbg, value=value, kcd=kcd, qk=qk,
                attn=attn, qd=qn * eg, kt=kn * egl, sol=sol)


def _rows2(a, b):
    return jnp.concatenate([a, b], axis=1)


def _lanes2(a, b):
    return jnp.concatenate([a, b], axis=2)


GDN_CPS = 2


def _gdn_specs(T, rev):
    G = GDN_CPS
    ns = T // (G * CH)
    pos = (lambda n: ns - 1 - n) if rev else (lambda n: n)
    mat = pl.BlockSpec((HEADS, G * CH, HD), lambda n: (0, pos(n), 0))
    col = pl.BlockSpec((G, HEADS, 1, CH), lambda n: (pos(n), 0, 0, 0))
    sca = pl.BlockSpec((HEADS, 1, 1), lambda n: (0, 0, 0))
    nw = pl.BlockSpec((1, 1, HD), lambda n: (0, 0, 0))
    st = pl.BlockSpec((G, HEADS, HD, HD), lambda n: (pos(n), 0, 0, 0))
    tok = lambda width, off: pl.BlockSpec((G * CH, width), lambda n: (pos(n), off))
    return ns, mat, col, sca, nw, st, tok


def _split_heads(x):
    return [jnp.stack([x[:, (g * HEADS + h) * HD:(g * HEADS + h + 1) * HD] for h in range(HEADS)])
            for g in range(x.shape[1] // FW)]


def _store_heads(ref, rows, g, val):
    for h in range(HEADS):
        ref[rows, (g * HEADS + h) * HD:(g * HEADS + h + 1) * HD] = val[h]


def _gdn_fwd(conv, rest, bpre, apre, alog, dtb, nw):
    T = conv.shape[0]
    ns, mat, col, sca, nws, st, tok = _gdn_specs(T, False)
    rows = [slice(g * CH, (g + 1) * CH) for g in range(GDN_CPS)]

    def body(c_ref, z_ref, b_ref, a_ref, al_ref, dt_ref, nw_ref, o_ref, og_ref, st_ref, s_scr):
        @pl.when(pl.program_id(0) == 0)
        def _():
            s_scr[...] = jnp.zeros_like(s_scr)

        locs = [_gdn_chunk(*_split_heads(c_ref[rows[g], :]), _row_to_col(b_ref[g]), _row_to_col(a_ref[g]), al_ref[...],
                           dt_ref[...]) for g in range(GDN_CPS)]
        s = s_scr[...]
        for g, L in enumerate(locs):
            st_ref[g] = s
            both = _bd_lo(_rows2(L["kcd"], L["qd"]), s, 2, 1)
            v_new = L["value"] - both[:, :CH]
            o = both[:, CH:] + _bd_lo(L["attn"], v_new, 2, 1)
            s = s * L["cd"] + _bd_lo(L["kt"], v_new, 1, 1)
            o_ref[:, rows[g], :] = o
            zz, = _split_heads(z_ref[rows[g], :])
            rs = lax.rsqrt(jnp.mean(o * o, axis=-1, keepdims=True) + EPS)
            _store_heads(og_ref, rows[g], 0, o * rs * nw_ref[...] * zz * _sig(zz))
        s_scr[...] = s

    return pl.pallas_call(
        body, name="gdn_fwd", grid=(ns,), in_specs=[tok(3 * FW, 0), tok(FW, OFF_Z // FW), col, col, sca, sca, nws],
        out_specs=[mat, tok(FW, 0), st],
        out_shape=[jax.ShapeDtypeStruct((HEADS, T, HD), F32), jax.ShapeDtypeStruct((T, FW), F32),
                   jax.ShapeDtypeStruct((T // CH, HEADS, HD, HD), F32)],
        scratch_shapes=[pltpu.VMEM((HEADS, HD, HD), F32)], compiler_params=_cp("arbitrary"))(
            conv, rest, bpre, apre, alog, dtb, nw)


def _gdn_bwd(conv, rest, bpre, apre, alog, dtb, nw, states, o, dog):
    T = conv.shape[0]
    ns, mat, col, sca, nws, st, tok = _gdn_specs(T, True)
    rows = [slice(g * CH, (g + 1) * CH) for g in range(GDN_CPS)]

    def body(c_ref, z_ref, b_ref, a_ref, al_ref, dt_ref, nw_ref, st_ref, o_ref, dog_ref,
             dc_ref, dz_ref, db_ref, da_ref, dal_ref, ddt_ref, dnw_ref, ds_scr):
        @pl.when(pl.program_id(0) == 0)
        def _():
            ds_scr[...] = jnp.zeros_like(ds_scr)
            dal_ref[...] = jnp.zeros_like(dal_ref)
            ddt_ref[...] = jnp.zeros_like(ddt_ref)
            dnw_ref[...] = jnp.zeros_like(dnw_ref)

        splits = [_split_heads(c_ref[rows[g], :]) for g in range(GDN_CPS)]
        locs = [_gdn_chunk(*splits[g], _row_to_col(b_ref[g]), _row_to_col(a_ref[g]), al_ref[...], dt_ref[...])
                for g in range(GDN_CPS)]
        dsn = ds_scr[...]
        acc_al, acc_dt, acc_nw = 0.0, 0.0, 0.0
        for g in reversed(range(GDN_CPS)):
            L, vv = locs[g], splits[g][2]
            tril, strict, lb = L["tril"], L["strict"], L["lb"]
            qn, kn, kb, beta, decay, eg, egl, cd = L["qn"], L["kn"], L["kb"], L["beta"], L["decay"], L["eg"], L["egl"], L["cd"]
            value, kcd, attn, qd, kt, tm = L["value"], L["kcd"], L["attn"], L["qd"], L["kt"], L["tm"]
            s = st_ref[g]
            v_new = value - _bd_lo(kcd, s, 2, 1)
            oo = o_ref[:, rows[g], :]
            zz, = _split_heads(z_ref[rows[g], :])
            dog_, = _split_heads(dog_ref[rows[g], :])
            sz = _sig(zz)
            rs = lax.rsqrt(jnp.mean(oo * oo, axis=-1, keepdims=True) + EPS)
            oh = oo * rs
            _store_heads(dz_ref, rows[g], 0, dog_ * oh * nw_ref[...] * sz * (1.0 + zz * (1.0 - sz)))
            don = dog_ * zz * sz
            gdy = don * nw_ref[...]
            do = rs * (gdy - oh * jnp.mean(oh * gdy, axis=-1, keepdims=True))
            acc_nw = acc_nw + jnp.sum(don * oh, axis=(0, 1), keepdims=True)
            d_vnew = _bd_lo(attn, do, 1, 1) + _bd_lo(kt, dsn, 2, 1)
            both = _bd_lo(_rows2(do, d_vnew), s, 2, 2)
            d_qd, d_kcd = both[:, :CH], -both[:, CH:]
            d_attn = jnp.where(tril, _bd_lo(do, v_new, 2, 2), 0.0)
            d_kt = _bd_lo(v_new, dsn, 2, 2)
            d_cd = jnp.sum(s * dsn, axis=(1, 2), keepdims=True)
            dsn_next = cd * dsn + _bd_lo(_rows2(qd, kcd), _rows2(do, -d_vnew), 1, 1)
            dsol = _bd_hi(tm, _lanes2(d_vnew, d_kcd), 1, 1)
            d_vb, d_kbg = dsol[:, :, :HD], dsol[:, :, HD:]
            da = -jnp.where(strict, _bd_lo(dsol, L["sol"], 2, 2), 0.0)
            dkk = da * decay
            dqk = d_attn * decay
            d_decay = da * L["kk"] + d_attn * L["qk"]
            both = _bd_lo(_rows2(dkk, dqk), kn, 2, 1)
            d_kb = both[:, :CH] + d_kbg * eg
            d_qn = both[:, CH:] + d_qd * eg
            d_kn = _bd_lo(_rows2(dkk, dqk), _rows2(kb, qn), 1, 1) + d_kt * egl
            dd = d_decay * decay
            d_gc = jnp.sum(dd + d_qd * qd + d_kbg * L["kbg"] - d_kt * kt, axis=-1, keepdims=True) \
                - _row_to_col(jnp.sum(dd, axis=1, keepdims=True))
            d_gl = jnp.sum(d_kt * kt, axis=(1, 2), keepdims=True) + d_cd * cd[:, :, 0:1]
            last = lax.broadcasted_iota(jnp.int32, (1, CH, 1), 1) == CH - 1
            d_gc = d_gc + jnp.where(last, d_gl, 0.0)
            dg = _bd_hi(lb, jnp.broadcast_to(d_gc, (HEADS, CH, HD)), 1, 1, True)[:, :, 0:1]
            d_apre = -dg * L["ea"] * _sig(L["x"])
            da_ref[g] = _col_to_row(d_apre)
            acc_al = acc_al + jnp.sum(dg * L["g"], axis=1, keepdims=True)
            acc_dt = acc_dt + jnp.sum(d_apre, axis=1, keepdims=True)
            d_beta = jnp.sum(d_kb * kn + d_vb * vv, axis=-1, keepdims=True)
            db_ref[g] = _col_to_row(d_beta * beta * (1.0 - beta))
            d_kn = d_kn + d_kb * beta
            qh = L["qh"]
            _store_heads(dc_ref, rows[g], 0, (HD ** -0.5) * L["rq"] * (d_qn - qh * jnp.sum(qh * d_qn, axis=-1, keepdims=True)))
            _store_heads(dc_ref, rows[g], 1, L["rk"] * (d_kn - kn * jnp.sum(kn * d_kn, axis=-1, keepdims=True)))
            _store_heads(dc_ref, rows[g], 2, d_vb * beta)
            dsn = dsn_next
        ds_scr[...] = dsn
        dal_ref[...] += acc_al
        ddt_ref[...] += acc_dt
        dnw_ref[...] += acc_nw

    c3 = jax.ShapeDtypeStruct((T // CH, HEADS, 1, CH), F32)
    s3 = jax.ShapeDtypeStruct((HEADS, 1, 1), F32)
    return pl.pallas_call(
        body, name="gdn_bwd", grid=(ns,),
        in_specs=[tok(3 * FW, 0), tok(FW, OFF_Z // FW), col, col, sca, sca, nws, st, mat, tok(FW, 0)],
        out_specs=[tok(3 * FW, 0), tok(FW, 0), col, col, sca, sca, nws],
        out_shape=[jax.ShapeDtypeStruct((T, 3 * FW), F32), jax.ShapeDtypeStruct((T, FW), F32), c3, c3, s3, s3,
                   jax.ShapeDtypeStruct((1, 1, HD), F32)],
        scratch_shapes=[pltpu.VMEM((HEADS, HD, HD), F32)], compiler_params=_cp("arbitrary"))(
            conv, rest, bpre, apre, alog, dtb, nw, states, o, dog)


def _local_step(x, target, w):
    seq = x.shape[0]
    T = -(-(X0 + seq) // T_ALIGN) * T_ALIGN
    back = T - X0 - seq
    tb_att = _div(T, ATT_TILE)
    h0 = jnp.concatenate([jnp.zeros((FRONT, D), F32), w["meta"], x, jnp.zeros((back, D), F32)], axis=0)
    tgt = jnp.concatenate([jnp.zeros((X0, D), F32), target, jnp.zeros((back, D), F32)], axis=0)
    row = lambda v: v.reshape(1, -1)
    nmix, nffn, nfin = row(w["norm_mix"]), row(w["norm_ffn"]), row(w["norm_final"])
    gate_b = row(w["gate_bias"])
    fconv_b = row(w["ffn_conv_b"])
    bias128 = jnp.zeros((1, SMALL_W), F32).at[0, :HEADS].set(w["fgt_bias"])
    alog = w["a_log"].reshape(HEADS, 1, 1)
    dtb = w["dt_bias"].reshape(HEADS, 1, 1)
    gnw = w["gdn_norm"].reshape(1, 1, HD)

    a = _rms_fwd(h0, nmix, "rms_mix")
    pf = _mm(a, w["w_fox"], out_dtype=_BF, name="proj_fox")
    rest = _mm(a, w["w_rest"], name="proj_rest")
    small = rest[:, OFF_SMALL:]
    c8 = _fgate_fwd(small, bias128)[:, :HEADS]
    c_cols, c_rows = _att_cols(c8), _att_rows(c8, tb_att)
    ones_blk = jnp.asarray(np.kron(np.eye(HEADS, dtype=np.float32), np.ones((HD, HD), np.float32)))
    plan = _att_plan(_att_stats(pf, c_cols, ones_blk))
    o_fox, lse = _fox_fwd(pf, c_cols, c_rows, plan)
    conv = _gconv_fwd(rest, w["gdn_conv"])
    chunk_rows = lambda a8: a8.reshape(T // CH, CH, HEADS).transpose(0, 2, 1)[:, :, None, :]
    bpre = chunk_rows(small[:, HEADS:2 * HEADS])
    apre = chunk_rows(small[:, 2 * HEADS:3 * HEADS])
    o_raw, og, states = _gdn_fwd(conv, rest, bpre, apre, alog, dtb, gnw)
    y_fox = _mm(o_fox, w["w_bfox"], name="y_fox")
    y_gdn = _mm(og, w["w_bgdn"], name="y_gdn")
    mix = _mix_fwd(rest, gate_b, y_fox, y_gdn)
    h1 = _mm(mix, w["w_out"], resid=h0, name="out_proj")
    b = _rms_fwd(h1, nffn, "rms_ffn")
    up = _mm(b, w["w_up"], name="ffn_up")
    f = _glu_fwd(up, w["ffn_conv"], fconv_b)
    h2 = _mm(f, w["w_down"], resid=h1, name="ffn_down")
    dh2, loss, d_nfin = _loss_head(h2, nfin, tgt, seq)

    d_f = _mm(dh2, w["w_down"], tb=True, name="d_f")
    g_down = _mm_t(f, dh2, "g_down")
    dug, duu, dwg, dwu, dbg, dbu = _glu_bwd_du(up, w["ffn_conv"], fconv_b, d_f)
    dxg = _conv_bwd_dx(dug, w["ffn_conv"][:, :DFF], _BF, "fconv_dx_gate")
    dxu = _conv_bwd_dx(duu, w["ffn_conv"][:, DFF:], _BF, "fconv_dx_up")
    d_b = _mm(dxg, w["w_up"][:, :DFF], tb=True, name="d_b_gate")
    d_b = _mm(dxu, w["w_up"][:, DFF:], tb=True, resid=d_b, name="d_b_up")
    g_up = jnp.concatenate([_mm_t(b, dxg, "g_up_gate"), _mm_t(b, dxu, "g_up_up")], axis=1)
    dh1, d_nffn = _rms_bwd(h1, nffn, d_b, dh2, "rms_ffn_bwd")

    dmix = _mm(dh1, w["w_out"], tb=True, name="d_mix")
    g_out = _mm_t(mix, dh1, "g_out")
    dyf, dyg, dgf, dgg, dgbf, dgbg = _mix_bwd(rest, gate_b, y_fox, y_gdn, dmix)
    do_fox = _mm(dyf, w["w_bfox"], tb=True, out_dtype=_BF, name="d_o_fox")
    g_bfox = _mm_t(o_fox, dyf, "g_bfox")
    d_og = _mm(dyg, w["w_bgdn"], tb=True, name="d_o_gdn")
    g_bgdn = _mm_t(og, dyg, "g_bgdn")

    delta = _head_dot(do_fox, o_fox, ones_blk)
    dq, dcq = _fox_bwd_dq(pf, do_fox, c_cols, c_rows, lse, delta, plan)
    dk, dv, dck = _fox_bwd_dkv(pf, do_fox, c_cols, _att_rows(c8 - lse[:, ::HD], tb_att), _att_rows(delta[:, ::HD], tb_att), plan)
    dc = jnp.pad((dcq + dck)[:, ::HD], ((0, 0), (0, SMALL_W - HEADS)))
    dfp, d_fb = _fgate_bwd(small, bias128, dc)

    dconv, dz, dbp, dap, d_alog, d_dtb, d_gnw = _gdn_bwd(conv, rest, bpre, apre, alog, dtb, gnw, states, o_raw, d_og)
    du_g, g_gconv = _gconv_bwd_du(rest, w["gdn_conv"], dconv)
    dgx = _conv_bwd_dx(du_g, w["gdn_conv"], _BF, "gconv_dx")
    token_rows = lambda a: a[:, :, 0, :].transpose(0, 2, 1).reshape(T, HEADS)
    dsmall = jnp.concatenate([dfp[:, :HEADS], token_rows(dbp), token_rows(dap),
                              jnp.zeros((T, SMALL_W - 3 * HEADS), F32)], axis=1)
    drest = jnp.concatenate([dgx, dz.astype(_BF), dgf, dgg, dsmall.astype(_BF)], axis=1)
    dfox = jnp.concatenate([dq, dk, dv], axis=1)
    d_a = _mm(dfox, w["w_fox"], tb=True, name="d_a_fox")
    d_a = _mm(drest, w["w_rest"], tb=True, resid=d_a, name="d_a_rest")
    g_fox = _mm_t(a, dfox, "g_w_fox")
    g_rest = _mm_t(a, drest, "g_w_rest")
    dh0, d_nmix = _rms_bwd(h0, nmix, d_a, dh1, "rms_mix_bwd")

    sm = lambda lo: g_rest[:, OFF_SMALL + lo:OFF_SMALL + lo + HEADS]
    g_w_in = jnp.concatenate([g_fox, sm(0), g_rest[:, :3 * FW], g_rest[:, OFF_Z:OFF_Z + FW], sm(HEADS), sm(2 * HEADS),
                              g_rest[:, OFF_GATES:OFF_GATES + 2 * D]], axis=1)
    grads = dict(
        meta_tokens=dh0[FRONT:X0], w_in=g_w_in, fgt_bias=d_fb[0, :HEADS], gdn_conv_w=g_gconv,
        gdn_a_log=d_alog.reshape(HEADS), gdn_dt_bias=d_dtb.reshape(HEADS), gdn_norm_w=d_gnw.reshape(HD),
        gate_bias=jnp.concatenate([dgbf, dgbg], axis=1).reshape(2 * D), w_branch_fox=g_bfox, w_branch_gdn=g_bgdn,
        w_out=g_out, norm_mix_w=d_nmix.reshape(D), norm_ffn_w=d_nffn.reshape(D), ffn_w_up=g_up,
        ffn_conv_w=jnp.concatenate([dwg, dwu], axis=1), ffn_conv_b=jnp.concatenate([dbg, dbu], axis=1).reshape(2 * DFF),
        ffn_w_down=g_down, norm_final_w=d_nfin.reshape(D))
    return loss, dh0[X0:X0 + seq], grads


N_CHIPS = 4
PACK_W = 1024
PACK_ROW_ALIGN = 32
BIG = ("w_in", "w_branch_fox", "w_branch_gdn", "w_out", "ffn_w_up", "ffn_w_down")
WEIGHTS = (
    ("meta_tokens", (N_META, D), 1), ("w_in", (D, 3 * FW + HEADS + 4 * FW + 2 * HEADS + 2 * D), 1), ("fgt_bias", (1, HEADS), None),
    ("gdn_conv_w", (4, 3 * FW), 1), ("gdn_a_log", (1, HEADS), None), ("gdn_dt_bias", (1, HEADS), None),
    ("gdn_norm_w", (1, HD), None), ("gate_bias", (1, 2 * D), None), ("w_branch_fox", (FW, D), 1),
    ("w_branch_gdn", (FW, D), 1), ("w_out", (D, D), 0), ("norm_mix_w", (1, D), None), ("norm_ffn_w", (1, D), None),
    ("ffn_w_up", (D, 2 * DFF), 1), ("ffn_conv_w", (3, 2 * DFF), 1), ("ffn_conv_b", (1, 2 * DFF), None),
    ("ffn_w_down", (DFF, D), 0), ("norm_final_w", (1, D), None))
SPLIT_F32 = ("meta_tokens", "gdn_conv_w", "ffn_conv_w")


def _shard_shape(shape, axis):
    if axis is None:
        return shape
    return tuple(s // N_CHIPS if a == axis else s for a, s in enumerate(shape))


def _shard_of(full, axis, q):
    if axis is None:
        return full
    n = full.shape[axis] // N_CHIPS
    return lax.slice_in_dim(full, q * n, (q + 1) * n, axis=axis)


def _pack_rows(n_elems):
    rows = -(-n_elems // PACK_W)
    return -(-rows // PACK_ROW_ALIGN) * PACK_ROW_ALIGN


def _pack(pieces, dtype):
    flat = jnp.concatenate([p.reshape(-1).astype(dtype) for p in pieces])
    rows = _pack_rows(flat.shape[0])
    return jnp.pad(flat, (0, rows * PACK_W - flat.shape[0])).reshape(rows, PACK_W)


def _unpack(slab, shapes):
    flat = slab.reshape(-1)
    out, off = [], 0
    for s in shapes:
        n = int(np.prod(s))
        out.append(flat[off:off + n].reshape(s))
        off += n
    return out


HBM_SPEC = pl.BlockSpec(memory_space=pltpu.HBM)
MESH_ID = pl.DeviceIdType.MESH


def _scatter_chips(srcs, name):
    n = len(srcs)

    def body(*refs):
        src_refs, out_refs = refs[:n], refs[n:2 * n]
        send_sems, recv_sems, local_sems = refs[2 * n:]
        x, y, c = lax.axis_index("x"), lax.axis_index("y"), lax.axis_index("c")
        q = 2 * x + y
        peers = [(1 - x, y), (x, 1 - y), (1 - x, 1 - y)]

        def remote(a, k, src_slot, dst_slot):
            px, py = peers[k]
            return pltpu.make_async_remote_copy(
                src_ref=src_refs[a].at[src_slot], dst_ref=out_refs[a].at[dst_slot], send_sem=send_sems.at[3 * a + k],
                recv_sem=recv_sems.at[3 * a + k], device_id=(px, py, c), device_id_type=MESH_ID)

        mine = [pltpu.make_async_copy(src_refs[a].at[q], out_refs[a].at[q], local_sems.at[a]) for a in range(n)]
        for cp in mine:
            cp.start()
        sends = [remote(a, k, 2 * px + py, q) for a in range(n) for k, (px, py) in enumerate(peers)]
        for cp in sends:
            cp.start()
        for a in range(n):
            for k, (px, py) in enumerate(peers):
                remote(a, k, 0, 2 * px + py).wait_recv()
        for cp in sends:
            cp.wait_send()
        for cp in mine:
            cp.wait()

    out_shape = [jax.ShapeDtypeStruct((N_CHIPS,) + s.shape[-2:], s.dtype) for s in srcs]
    return pl.pallas_call(
        body, name=name, in_specs=[HBM_SPEC] * n, out_specs=[HBM_SPEC] * n, out_shape=out_shape,
        scratch_shapes=[pltpu.SemaphoreType.DMA((3 * n,)), pltpu.SemaphoreType.DMA((3 * n,)), pltpu.SemaphoreType.DMA((n,))],
    )(*srcs)


def _gather_chips(srcs, name):
    n = len(srcs)

    def body(*refs):
        src_refs, out_refs = refs[:n], refs[n:2 * n]
        ici_send, ici_recv, d2d_send, d2d_recv, local_sems = refs[2 * n:]
        x, y, c = lax.axis_index("x"), lax.axis_index("y"), lax.axis_index("c")
        q = 2 * x + y
        peers = [(1 - x, y), (x, 1 - y), (1 - x, 1 - y)]

        def half(a, which):
            r = srcs[a].shape[0] // 2
            return pl.ds(which * r, r)

        def ici(a, k, slot):
            px, py = peers[k]
            return pltpu.make_async_remote_copy(
                src_ref=src_refs[a].at[half(a, c)], dst_ref=out_refs[a].at[slot, half(a, c)], send_sem=ici_send.at[3 * a + k],
                recv_sem=ici_recv.at[3 * a + k], device_id=(px, py, c), device_id_type=MESH_ID)

        def d2d(a, k, which):
            px, py = peers[k]
            rows = out_refs[a].at[2 * px + py, half(a, which)]
            return pltpu.make_async_remote_copy(
                src_ref=rows, dst_ref=rows, send_sem=d2d_send.at[3 * a + k], recv_sem=d2d_recv.at[3 * a + k],
                device_id=(x, y, 1 - c), device_id_type=MESH_ID)

        mine = [pltpu.make_async_copy(src_refs[a], out_refs[a].at[q], local_sems.at[a]) for a in range(n)]
        for cp in mine:
            cp.start()
        sends = [ici(a, k, q) for a in range(n) for k in range(3)]
        for cp in sends:
            cp.start()
        passed = []
        for a in range(n):
            for k, (px, py) in enumerate(peers):
                ici(a, k, 2 * px + py).wait_recv()
                passed.append(d2d(a, k, c))
                passed[-1].start()
        for a in range(n):
            for k in range(3):
                d2d(a, k, 1 - c).wait_recv()
        for cp in sends + passed:
            cp.wait_send()
        for cp in mine:
            cp.wait()

    out_shape = [jax.ShapeDtypeStruct((N_CHIPS,) + s.shape, s.dtype) for s in srcs]
    sems = [pltpu.SemaphoreType.DMA((3 * n,))] * 4 + [pltpu.SemaphoreType.DMA((n,))]
    return pl.pallas_call(body, name=name, in_specs=[HBM_SPEC] * n, out_specs=[HBM_SPEC] * n, out_shape=out_shape,
                          scratch_shapes=sems)(*srcs)


def _sibling_swap(slabs, name):
    n = len(slabs)

    def body(*refs):
        src_refs, out_refs, send_sems, recv_sems = refs[:n], refs[n:2 * n], refs[2 * n], refs[2 * n + 1]
        x, y, c = lax.axis_index("x"), lax.axis_index("y"), lax.axis_index("c")
        cps = [pltpu.make_async_remote_copy(src_ref=src_refs[a], dst_ref=out_refs[a], send_sem=send_sems.at[a],
                                            recv_sem=recv_sems.at[a], device_id=(x, y, 1 - c), device_id_type=MESH_ID)
               for a in range(n)]
        for cp in cps:
            cp.start()
        for cp in cps:
            cp.wait_recv()
        for cp in cps:
            cp.wait_send()

    return pl.pallas_call(
        body, name=name, in_specs=[HBM_SPEC] * n, out_specs=[HBM_SPEC] * n,
        out_shape=[jax.ShapeDtypeStruct(s.shape, s.dtype) for s in slabs],
        scratch_shapes=[pltpu.SemaphoreType.DMA((n,)), pltpu.SemaphoreType.DMA((n,))])(*slabs)


def _sum_chips(r, name):
    rows, cols = r.shape[1:]
    tb = _div(rows, 256, 16)

    def body(r0, r1, r2, r3, o_ref):
        o_ref[...] = ((r0[0].astype(F32) + r1[0].astype(F32)) + r2[0].astype(F32)) + r3[0].astype(F32)

    spec = lambda j: pl.BlockSpec((1, tb, cols), lambda i: (j, i, 0))
    return pl.pallas_call(
        body, name=name, grid=(rows // tb,), in_specs=[spec(0), spec(1), spec(2), spec(3)],
        out_specs=pl.BlockSpec((tb, cols), lambda i: (i, 0)), out_shape=jax.ShapeDtypeStruct((rows, cols), F32),
        compiler_params=_cp("parallel"))(r, r, r, r)


def _adamw(w, m, v, p, q, name):
    rows, cols = w.shape
    tb = _div(rows, 256, 8)

    def body(w_ref, m_ref, v_ref, p_ref, q_ref, g_ref, d_ref, nm_ref, nv_ref):
        g = p_ref[...] + q_ref[...]
        m_new = B1 * m_ref[...] + (1.0 - B1) * g
        v_new = B2 * v_ref[...] + (1.0 - B2) * (g * g)
        g_ref[...] = g
        nm_ref[...] = m_new
        nv_ref[...] = v_new
        m_hat = m_new / (1.0 - B1 ** STEP)
        v_hat = v_new / (1.0 - B2 ** STEP)
        d_ref[...] = -LR * (m_hat / (jnp.sqrt(v_hat) + AEPS) + WD * w_ref[...])

    spec = pl.BlockSpec((tb, cols), lambda i: (i, 0))
    return pl.pallas_call(
        body, name=name, grid=(rows // tb,), in_specs=[spec] * 5, out_specs=[spec] * 4,
        out_shape=[jax.ShapeDtypeStruct((rows, cols), F32)] * 4, compiler_params=_cp("parallel"))(w, m, v, p, q)


def _split_w_in(w_in):
    o1 = 3 * FW
    o2 = o1 + HEADS
    o3 = o2 + 3 * FW
    o4 = o3 + FW
    o5 = o4 + HEADS
    o6 = o5 + HEADS
    pad = jnp.zeros((w_in.shape[0], SMALL_W - 3 * HEADS), w_in.dtype)
    rest = jnp.concatenate([w_in[:, o2:o3], w_in[:, o3:o4], w_in[:, o6:], w_in[:, o1:o2], w_in[:, o4:o5], w_in[:, o5:o6], pad],
                           axis=1)
    return w_in[:, :o1], rest


AXIS = {n: a for n, _, a in WEIGHTS}
SMALL = tuple(n for n, _, _ in WEIGHTS if n not in BIG)


def _by_chip(full, axis):
    rows, cols = full.shape
    if axis == 0:
        return full.reshape(N_CHIPS, rows // N_CHIPS, cols)
    return full.reshape(rows, N_CHIPS, cols // N_CHIPS).transpose(1, 0, 2)


def _from_chips(parts, axis):
    _, r, c = parts.shape
    if axis == 0:
        return parts.reshape(N_CHIPS * r, c)
    return parts.transpose(1, 0, 2).reshape(r, N_CHIPS * c)


def _gather_weights(shards):
    hi = {n: shards[n].astype(jnp.bfloat16) for n in SPLIT_F32}
    lo = [(shards[n] - hi[n].astype(F32)).astype(jnp.bfloat16) for n in SPLIT_F32]
    slab = _pack([hi[n] for n in SPLIT_F32] + lo, jnp.bfloat16)
    got = _gather_chips([shards[n].astype(jnp.bfloat16) for n in BIG] + [slab], "gather_weights")
    full = {n: _from_chips(g, AXIS[n]) for n, g in zip(BIG, got)}
    shapes = [shards[n].shape for n in SPLIT_F32] * 2
    per_chip = [_unpack(got[-1][j], shapes) for j in range(N_CHIPS)]
    for i, n in enumerate(SPLIT_F32):
        join = lambda off: jnp.concatenate([per_chip[j][off + i] for j in range(N_CHIPS)], axis=1).astype(F32)
        full[n] = join(0) + join(len(SPLIT_F32))
    return full


def kernel(x, meta_tokens, w_in, fgt_bias, gdn_conv_w, gdn_a_log, gdn_dt_bias, gdn_norm_w, gate_bias, w_branch_fox, w_branch_gdn, w_out, norm_mix_w, norm_ffn_w, ffn_w_up, ffn_conv_w, ffn_conv_b, ffn_w_down, norm_final_w, loss_target, m_meta_tokens, m_w_in, m_fgt_bias, m_gdn_conv_w, m_gdn_a_log, m_gdn_dt_bias, m_gdn_norm_w, m_gate_bias, m_w_branch_fox, m_w_branch_gdn, m_w_out, m_norm_mix_w, m_norm_ffn_w, m_ffn_w_up, m_ffn_conv_w, m_ffn_conv_b, m_ffn_w_down, m_norm_final_w, v_meta_tokens, v_w_in, v_fgt_bias, v_gdn_conv_w, v_gdn_a_log, v_gdn_dt_bias, v_gdn_norm_w, v_gate_bias, v_w_branch_fox, v_w_branch_gdn, v_w_out, v_norm_mix_w, v_norm_ffn_w, v_ffn_w_up, v_ffn_conv_w, v_ffn_conv_b, v_ffn_w_down, v_norm_final_w):
    weights = dict(meta_tokens=meta_tokens, w_in=w_in, fgt_bias=fgt_bias, gdn_conv_w=gdn_conv_w, gdn_a_log=gdn_a_log, gdn_dt_bias=gdn_dt_bias, gdn_norm_w=gdn_norm_w, gate_bias=gate_bias, w_branch_fox=w_branch_fox, w_branch_gdn=w_branch_gdn, w_out=w_out, norm_mix_w=norm_mix_w, norm_ffn_w=norm_ffn_w, ffn_w_up=ffn_w_up, ffn_conv_w=ffn_conv_w, ffn_conv_b=ffn_conv_b, ffn_w_down=ffn_w_down, norm_final_w=norm_final_w)
    m_in = dict(meta_tokens=m_meta_tokens, w_in=m_w_in, fgt_bias=m_fgt_bias, gdn_conv_w=m_gdn_conv_w, gdn_a_log=m_gdn_a_log, gdn_dt_bias=m_gdn_dt_bias, gdn_norm_w=m_gdn_norm_w, gate_bias=m_gate_bias, w_branch_fox=m_w_branch_fox, w_branch_gdn=m_w_branch_gdn, w_out=m_w_out, norm_mix_w=m_norm_mix_w, norm_ffn_w=m_norm_ffn_w, ffn_w_up=m_ffn_w_up, ffn_conv_w=m_ffn_conv_w, ffn_conv_b=m_ffn_conv_b, ffn_w_down=m_ffn_w_down, norm_final_w=m_norm_final_w)
    v_in = dict(meta_tokens=v_meta_tokens, w_in=v_w_in, fgt_bias=v_fgt_bias, gdn_conv_w=v_gdn_conv_w, gdn_a_log=v_gdn_a_log, gdn_dt_bias=v_gdn_dt_bias, gdn_norm_w=v_gdn_norm_w, gate_bias=v_gate_bias, w_branch_fox=v_w_branch_fox, w_branch_gdn=v_w_branch_gdn, w_out=v_w_out, norm_mix_w=v_norm_mix_w, norm_ffn_w=v_norm_ffn_w, ffn_w_up=v_ffn_w_up, ffn_conv_w=v_ffn_conv_w, ffn_conv_b=v_ffn_conv_b, ffn_w_down=v_ffn_w_down, norm_final_w=v_norm_final_w)
    shard2d = {n: _shard_shape(s, a) for n, s, a in WEIGHTS}
    as2d = lambda d: {n: d[n].reshape(shard2d[n]) for n, _, _ in WEIGHTS}
    w2, m2, v2 = as2d(weights), as2d(m_in), as2d(v_in)

    full = _gather_weights(w2)
    w_fox, w_rest = _split_w_in(full["w_in"])
    flat = lambda n: w2[n].reshape(-1)
    local_w = dict(
        meta=full["meta_tokens"], w_fox=w_fox, w_rest=w_rest, fgt_bias=flat("fgt_bias"), gdn_conv=full["gdn_conv_w"],
        a_log=flat("gdn_a_log"), dt_bias=flat("gdn_dt_bias"), gdn_norm=flat("gdn_norm_w"), gate_bias=flat("gate_bias"),
        w_bfox=full["w_branch_fox"], w_bgdn=full["w_branch_gdn"], w_out=full["w_out"], norm_mix=flat("norm_mix_w"),
        norm_ffn=flat("norm_ffn_w"), w_up=full["ffn_w_up"], ffn_conv=full["ffn_conv_w"], ffn_conv_b=flat("ffn_conv_b"),
        w_down=full["ffn_w_down"], norm_final=flat("norm_final_w"))

    loss, grad_x, grads = _local_step(x[0], loss_target[0], local_w)

    g2 = {n: grads[n].reshape(s) for n, s, _ in WEIGHTS}
    send = [_by_chip(g2[n].astype(jnp.bfloat16), AXIS[n]) for n in BIG]
    send.append(jnp.stack([_pack([_shard_of(g2[n], AXIS[n], j) for n in SMALL], F32) for j in range(N_CHIPS)]))
    parts = [_sum_chips(r, "sum_" + n) for r, n in zip(_scatter_chips(send, "scatter_grads"), BIG + ("small",))]
    others = _sibling_swap(parts, "swap_grads")
    slab = lambda d: _pack([d[n] for n in SMALL], F32)
    state = [(w2[n], m2[n], v2[n]) for n in BIG] + [(slab(w2), slab(m2), slab(v2))]
    outs = [_adamw(w, m, v, p, q, "adamw_" + n) for (w, m, v), p, q, n in zip(state, parts, others, BIG + ("small",))]
    small = [_unpack(o, [weights[n].shape for n in SMALL]) for o in outs[-1]]
    result = []
    for kind in range(4):
        by_name = {n: outs[i][kind].reshape(weights[n].shape) for i, n in enumerate(BIG)}
        by_name.update(zip(SMALL, small[kind]))
        result += [by_name[n] for n, _, _ in WEIGHTS]
    total = lax.psum(loss[0, 0], ("x", "y", "c"))
    return (total, grad_x[None], *result)
```

```python
import functools

import numpy as np
import jax
import jax.numpy as jnp
from jax import lax
from jax.experimental import pallas as pl
from jax.experimental.pallas import tpu as pltpu

F32 = jnp.float32
_BF = jnp.bfloat16
HI = lax.Precision.HIGHEST

D = 1024
N_META = 16
CH = 64
FRONT = CH - N_META
X0 = CH
HEADS = 8
HD = 64
FW = HEADS * HD
DFF = 2816
EPS = 1e-6
NEG = -1e30
T_ALIGN = 256
SMALL_W = 128
REST_W = 3 * FW + FW + 2 * D + SMALL_W
OFF_Z = 3 * FW
OFF_GATES = 4 * FW
OFF_SMALL = 4 * FW + 2 * D

LR, B1, B2, AEPS, WD, STEP = 0.001, 0.9, 0.999, 1e-08, 0.01, 10

VMEM_LIMIT = 56 * 1024 * 1024
ROW_TILE = 640
MM_TM, MM_TN, MM_TK = 1280, 512, 2816
ATT_TILE = 640


def _div(n, target, mult=128):
    if n <= target:
        return n
    best = None
    for d in range(mult, target + 1, mult):
        if n % d == 0:
            best = d
    assert best is not None, (n, target, mult)
    return best


def _cp(*sem):
    return pltpu.CompilerParams(dimension_semantics=sem, vmem_limit_bytes=VMEM_LIMIT)


def _sig(x):
    return 1.0 / (1.0 + jnp.exp(-x))


MM_VMEM_BUDGET = 40 * 1024 * 1024
MM_STEP_BYTES = 1 << 20


def _mm_tiles(m, n, k, sa, sb, so, has_resid):
    divs = lambda d, cap: [t for t in range(128, min(d, cap) + 1, 128) if d % t == 0] or [d]
    best = None
    for tm in divs(m, MM_TM * 2):
        for tn in divs(n, 4608):
            fixed = tm * tn * (4 + 2 * so + (8 if has_resid else 0))
            tks = [t for t in divs(k, MM_TK) if 2 * t * (tm * sa + tn * sb) + fixed <= MM_VMEM_BUDGET]
            if not tks:
                continue
            tk = tks[-1]
            steps = (m // tm) * (n // tn) * (k // tk)
            rmw = (k // tk - 1) * m * n * 4
            cost = (n // tn) * m * k * sa + (m // tm) * k * n * sb + steps * MM_STEP_BYTES + rmw
            if best is None or cost < best[0]:
                best = (cost, tm, tn, tk)
    assert best is not None, (m, n, k)
    return best[1:]


def _mm_t(a, b, name):
    return _mm(a, b, ta=True, name=name)


def _mm(a, b, *, ta=False, tb=False, out_dtype=F32, resid=None, name):
    K, M = a.shape if ta else a.shape[::-1]
    N = b.shape[0] if tb else b.shape[1]
    assert (b.shape[1] if tb else b.shape[0]) == K
    tm, tn, tk = _mm_tiles(M, N, K, a.dtype.itemsize, b.dtype.itemsize, jnp.dtype(out_dtype).itemsize, resid is not None)
    nk = K // tk
    dims = (((0 if ta else 1,), (1 if tb else 0,)), ((), ()))
    mxu = _BF

    def body(*refs):
        if resid is None:
            a_ref, b_ref, o_ref, acc = refs
            r_ref = None
        else:
            a_ref, b_ref, r_ref, o_ref, acc = refs
        k = pl.program_id(2)
        part = lax.dot_general(a_ref[...].astype(mxu), b_ref[...].astype(mxu), dims, preferred_element_type=F32)

        @pl.when(k == 0)
        def _():
            acc[...] = part

        @pl.when(k > 0)
        def _():
            acc[...] += part

        @pl.when(k == nk - 1)
        def _():
            r = acc[...]
            if r_ref is not None:
                r = r + r_ref[...]
            o_ref[...] = r.astype(o_ref.dtype)

    a_spec = pl.BlockSpec((tk, tm), lambda i, j, k: (k, i)) if ta else pl.BlockSpec((tm, tk), lambda i, j, k: (i, k))
    b_spec = pl.BlockSpec((tn, tk), lambda i, j, k: (j, k)) if tb else pl.BlockSpec((tk, tn), lambda i, j, k: (k, j))
    o_spec = pl.BlockSpec((tm, tn), lambda i, j, k: (i, j))
    in_specs = [a_spec, b_spec] + ([o_spec] if resid is not None else [])
    args = (a, b) + ((resid,) if resid is not None else ())
    return pl.pallas_call(
        body, name=name, grid=(M // tm, N // tn, nk), in_specs=in_specs, out_specs=o_spec,
        out_shape=jax.ShapeDtypeStruct((M, N), out_dtype), scratch_shapes=[pltpu.VMEM((tm, tn), F32)],
        compiler_params=_cp("parallel", "parallel", "arbitrary"))(*args)


def _rms_fwd(h, w, name):
    T = h.shape[0]
    tb = _div(T, ROW_TILE, 8)

    def body(h_ref, w_ref, o_ref):
        x = h_ref[...]
        r = lax.rsqrt(jnp.mean(x * x, axis=-1, keepdims=True) + EPS)
        o_ref[...] = (x * r * w_ref[...]).astype(o_ref.dtype)

    return pl.pallas_call(
        body, name=name, grid=(T // tb,),
        in_specs=[pl.BlockSpec((tb, D), lambda i: (i, 0)), pl.BlockSpec((1, D), lambda i: (0, 0))],
        out_specs=pl.BlockSpec((tb, D), lambda i: (i, 0)), out_shape=jax.ShapeDtypeStruct((T, D), _BF),
        compiler_params=_cp("parallel"))(h, w)


def _rms_bwd(h, w, dy, dres, name):
    T = h.shape[0]
    tb = _div(T, ROW_TILE, 8)

    def body(h_ref, w_ref, dy_ref, dr_ref, dh_ref, dw_ref):
        x = h_ref[...]
        r = lax.rsqrt(jnp.mean(x * x, axis=-1, keepdims=True) + EPS)
        xh = x * r
        dy = dy_ref[...]
        g = dy * w_ref[...]
        dh_ref[...] = dr_ref[...] + r * (g - xh * jnp.mean(xh * g, axis=-1, keepdims=True))
        part = jnp.sum(dy * xh, axis=0, keepdims=True)

        @pl.when(pl.program_id(0) == 0)
        def _():
            dw_ref[...] = part

        @pl.when(pl.program_id(0) > 0)
        def _():
            dw_ref[...] += part

    row = pl.BlockSpec((tb, D), lambda i: (i, 0))
    one = pl.BlockSpec((1, D), lambda i: (0, 0))
    return pl.pallas_call(
        body, name=name, grid=(T // tb,), in_specs=[row, one, row, row], out_specs=[row, one],
        out_shape=[jax.ShapeDtypeStruct((T, D), F32), jax.ShapeDtypeStruct((1, D), F32)],
        compiler_params=_cp("arbitrary"))(h, w, dy, dres)


def _fgate_fwd(small, bias):
    T = small.shape[0]
    tb = _div(T, ROW_TILE, 8)

    def body(s_ref, b_ref, c_ref, carry):
        @pl.when(pl.program_id(0) == 0)
        def _():
            carry[...] = jnp.zeros_like(carry)

        lf = jax.nn.log_sigmoid(s_ref[...] + b_ref[...])
        r = lax.broadcasted_iota(jnp.int32, (tb, tb), 0)
        c = lax.broadcasted_iota(jnp.int32, (tb, tb), 1)
        tri = (r >= c).astype(F32)
        cs = jnp.dot(tri, lf, precision=HI, preferred_element_type=F32) + carry[...]
        c_ref[...] = cs
        carry[...] = cs[tb - 1:tb, :]

    return pl.pallas_call(
        body, name="fgate_fwd", grid=(T // tb,),
        in_specs=[pl.BlockSpec((tb, SMALL_W), lambda i: (i, 0)), pl.BlockSpec((1, SMALL_W), lambda i: (0, 0))],
        out_specs=pl.BlockSpec((tb, SMALL_W), lambda i: (i, 0)), out_shape=jax.ShapeDtypeStruct((T, SMALL_W), F32),
        scratch_shapes=[pltpu.VMEM((1, SMALL_W), F32)], compiler_params=_cp("arbitrary"))(small, bias)


def _fgate_bwd(small, bias, dc):
    T = small.shape[0]
    tb = _div(T, ROW_TILE, 8)
    nb = T // tb

    def body(s_ref, b_ref, dc_ref, df_ref, db_ref, carry):
        @pl.when(pl.program_id(0) == 0)
        def _():
            carry[...] = jnp.zeros_like(carry)

        r = lax.broadcasted_iota(jnp.int32, (tb, tb), 0)
        c = lax.broadcasted_iota(jnp.int32, (tb, tb), 1)
        tri = (r <= c).astype(F32)
        dlf = jnp.dot(tri, dc_ref[...], precision=HI, preferred_element_type=F32) + carry[...]
        carry[...] = dlf[0:1, :]
        df = dlf * _sig(-(s_ref[...] + b_ref[...]))
        df_ref[...] = df
        part = jnp.sum(df, axis=0, keepdims=True)

        @pl.when(pl.program_id(0) == 0)
        def _():
            db_ref[...] = part

        @pl.when(pl.program_id(0) > 0)
        def _():
            db_ref[...] += part

    rev = pl.BlockSpec((tb, SMALL_W), lambda i: (nb - 1 - i, 0))
    one = pl.BlockSpec((1, SMALL_W), lambda i: (0, 0))
    return pl.pallas_call(
        body, name="fgate_bwd", grid=(nb,), in_specs=[rev, one, rev], out_specs=[rev, one],
        out_shape=[jax.ShapeDtypeStruct((T, SMALL_W), F32), jax.ShapeDtypeStruct((1, SMALL_W), F32)],
        scratch_shapes=[pltpu.VMEM((1, SMALL_W), F32)], compiler_params=_cp("arbitrary"))(small, bias, dc)


def _att_rows(a8, tb):
    T = a8.shape[0]
    return a8.T.reshape(HEADS // 2, 2, T // tb, tb).transpose(0, 2, 1, 3)


def _att_cols(a8):
    return jnp.repeat(a8, HD, axis=1)


def _pair_cols(tb):
    return pl.BlockSpec((1, tb, 2), lambda p, i, *_: (p, i, 0))


def _from_pairs(a):
    return a.transpose(1, 0, 2).reshape(a.shape[1], HEADS)


EXP_ZERO = -104.0
SKIP_SLACK = 2.0
NORM_SLACK = 1.02


def _att_stats(qkv, c_cols, ones_blk):
    T = qkv.shape[0]
    tb = _div(T, ATT_TILE)

    def body(q_ref, k_ref, c_ref, e_ref, o_ref):
        q = q_ref[...].astype(F32)
        k = k_ref[...].astype(F32)
        e = e_ref[...].astype(_BF)
        qn = jnp.max(jnp.dot((q * q).astype(_BF), e, preferred_element_type=F32), axis=0, keepdims=True)
        kn = jnp.max(jnp.dot((k * k).astype(_BF), e, preferred_element_type=F32), axis=0, keepdims=True)
        c = c_ref[...]
        o_ref[0] = jnp.concatenate([jnp.sqrt(qn), jnp.sqrt(kn), jnp.max(c, axis=0, keepdims=True),
                                    jnp.min(c, axis=0, keepdims=True), jnp.zeros((4, FW), F32)], axis=0)

    blk = lambda off: pl.BlockSpec((tb, FW), lambda i: (i, off))
    return pl.pallas_call(
        body, name="att_stats", grid=(T // tb,),
        in_specs=[blk(0), blk(1), blk(0), pl.BlockSpec((FW, FW), lambda i: (0, 0))],
        out_specs=pl.BlockSpec((1, 8, FW), lambda i: (i, 0, 0)), out_shape=jax.ShapeDtypeStruct((T // tb, 8, FW), F32),
        compiler_params=_cp("parallel"))(qkv, qkv, c_cols, ones_blk)


def _att_plan(stats):
    nb = stats.shape[0]
    st = stats[:, :4, ::HD]
    qmax, kmax, cmax, cmin = (st[:, r, :].T for r in range(4))
    bound = (HD ** -0.5) * NORM_SLACK * qmax[:, :, None] * (kmax[:, None, :] + kmax[:, :, None]) \
        + cmax[:, :, None] - cmin[:, None, :] + SKIP_SLACK
    ii = lax.broadcasted_iota(jnp.int32, (nb, nb), 0)
    jj = lax.broadcasted_iota(jnp.int32, (nb, nb), 1)
    skip = (bound < EXP_ZERO) & (jj < ii)[None]
    live = (~skip & (jj <= ii)[None]).reshape(HEADS // 2, 2, nb, nb).any(axis=1)
    jfirst = jnp.argmax(live, axis=2).astype(jnp.int32)
    ilast = (nb - 1 - jnp.argmax(live[:, ::-1, :], axis=1)).astype(jnp.int32)
    return skip.astype(jnp.int32).reshape(-1), jfirst.reshape(-1), ilast.reshape(-1)


def _fox_fwd(qkv, c_cols, c_rows, plan):
    T = qkv.shape[0]
    tb = _div(T, ATT_TILE)
    nb = T // tb
    npair = HEADS // 2
    scale = HD ** -0.5
    dn = (((1,), (1,)), ((), ()))

    def body(skip_ref, jfirst_ref, ilast_ref, q_ref, k_ref, v_ref, cq_ref, ck_ref, o_ref, l_ref, l2_ref, m_scr, acc_scr):
        pr = pl.program_id(0)
        i = pl.program_id(1)
        q = q_ref[...]
        lane = lax.broadcasted_iota(jnp.int32, (1, 2 * HD), 1)
        sel0 = lane < HD
        zero = jnp.zeros_like(q)
        qh = (jnp.where(sel0, q, zero) * scale, jnp.where(sel0, zero, q) * scale)
        cq = (cq_ref[:, 0:1], cq_ref[:, HD:HD + 1])
        row = i * tb + lax.broadcasted_iota(jnp.int32, (tb, 1), 0)
        m_scr[...] = jnp.full(m_scr.shape, NEG, F32)
        acc_scr[...] = jnp.zeros(acc_scr.shape, F32)

        def tile(j, h, masked):
            start = pl.multiple_of(j * tb, tb)
            kb = k_ref[pl.ds(start, tb), :]
            vb = v_ref[pl.ds(start, tb), :]
            one = jnp.ones_like(vb)
            vh = jnp.where(sel0, vb, one) if h == 0 else jnp.where(sel0, one, vb)
            t = lax.dot_general(qh[h], kb, dn, preferred_element_type=F32) - ck_ref[0, j, h:h + 1, :]
            if masked:
                col = j * tb + lax.broadcasted_iota(jnp.int32, (1, tb), 1)
                mask = (col <= row) & (col >= FRONT)
                t = jnp.where(mask, t, NEG)
            m = m_scr[h]
            m_new = jnp.maximum(m, jnp.max(t, axis=-1, keepdims=True) + cq[h])
            p = jnp.exp(t + (cq[h] - m_new))
            if masked:
                p = jnp.where(mask, p, 0.0)
            m_scr[h] = m_new
            acc_scr[h] = acc_scr[h] * jnp.exp(m - m_new) + jnp.dot(p.astype(vb.dtype), vh, preferred_element_type=F32)

        def step(j, _):
            edge = (j == 0) | (j == i)
            for h in range(2):
                live = skip_ref[((2 * pr + h) * nb + i) * nb + j] == 0
                pl.when(live & edge)(functools.partial(tile, j, h, True))
                pl.when(live & jnp.logical_not(edge))(functools.partial(tile, j, h, False))
            return 0

        lax.fori_loop(jfirst_ref[pr * nb + i], i + 1, step, 0)
        outs, lses = [], []
        for h in range(2):
            l = acc_scr[h][:, (1 - h) * HD:(1 - h) * HD + 1]
            ok = l > 0.0
            ls = jnp.where(ok, l, 1.0)
            outs.append(jnp.where(ok, acc_scr[h] / ls, 0.0))
            lses.append(jnp.where(ok, m_scr[h] + jnp.log(ls), 0.0))
        o_ref[...] = jnp.where(sel0, outs[0], outs[1])
        l_ref[...] = jnp.where(sel0, lses[0], lses[1])
        l2_ref[0, :, 0:1] = lses[0]
        l2_ref[0, :, 1:2] = lses[1]

    blk = lambda off: pl.BlockSpec((tb, 2 * HD), lambda p, i, *_: (i, off + p))
    full = lambda off: pl.BlockSpec((T, 2 * HD), lambda p, i, *_: (0, off + p))
    rows = pl.BlockSpec((1, nb, 2, tb), lambda p, i, *_: (p, 0, 0, 0))
    return pl.pallas_call(
        body, name="fox_fwd",
        grid_spec=pltpu.PrefetchScalarGridSpec(
            num_scalar_prefetch=3, grid=(npair, nb),
            in_specs=[blk(0), full(npair), full(2 * npair), blk(0), rows], out_specs=[blk(0), blk(0), _pair_cols(tb)],
            scratch_shapes=[pltpu.VMEM((2, tb, 1), F32), pltpu.VMEM((2, tb, 2 * HD), F32)]),
        out_shape=[jax.ShapeDtypeStruct((T, FW), F32), jax.ShapeDtypeStruct((T, FW), F32),
                   jax.ShapeDtypeStruct((npair, T, 2), F32)],
        compiler_params=_cp("parallel", "arbitrary"))(*plan, qkv, qkv, qkv, c_cols, c_rows)


def _fox_bwd_dq(qkv, do, c_cols, c_rows, lse, delta, plan):
    T = qkv.shape[0]
    tb = _div(T, ATT_TILE)
    nb = T // tb
    npair = HEADS // 2
    scale = HD ** -0.5
    dn = (((1,), (1,)), ((), ()))

    def body(skip_ref, jfirst_ref, ilast_ref, q_ref, k_ref, v_ref, do_ref, cq_ref, ck_ref, l_ref, dl_ref, dq_ref, dc_ref,
             dq_scr):
        pr = pl.program_id(0)
        i = pl.program_id(1)
        q = q_ref[...]
        do = do_ref[...]
        lane = lax.broadcasted_iota(jnp.int32, (1, 2 * HD), 1)
        sel0 = lane < HD
        qh = (jnp.where(sel0, q, jnp.zeros_like(q)) * scale, jnp.where(sel0, jnp.zeros_like(q), q) * scale)
        doh = (jnp.where(sel0, do, jnp.zeros_like(do)), jnp.where(sel0, jnp.zeros_like(do), do))
        ce = (cq_ref[:, 0:1] - l_ref[:, 0:1], cq_ref[:, HD:HD + 1] - l_ref[:, HD:HD + 1])
        dl = (dl_ref[:, 0:1], dl_ref[:, HD:HD + 1])
        row = i * tb + lax.broadcasted_iota(jnp.int32, (tb, 1), 0)
        dq_scr[...] = jnp.zeros(dq_scr.shape, F32)

        def tile(j, h, masked):
            start = pl.multiple_of(j * tb, tb)
            kb = k_ref[pl.ds(start, tb), :]
            vb = v_ref[pl.ds(start, tb), :]
            one = jnp.ones_like(kb)
            kh = jnp.where(sel0, kb, one) if h == 0 else jnp.where(sel0, one, kb)
            t = lax.dot_general(qh[h], kb, dn, preferred_element_type=F32) - ck_ref[0, j, h:h + 1, :]
            if masked:
                col = j * tb + lax.broadcasted_iota(jnp.int32, (1, tb), 1)
                mask = (col <= row) & (col >= FRONT)
                p = jnp.where(mask, jnp.exp(jnp.where(mask, t, NEG) + ce[h]), 0.0)
            else:
                p = jnp.exp(t + ce[h])
            dp = lax.dot_general(doh[h], vb, dn, preferred_element_type=F32)
            ds = p * (dp - dl[h])
            dq_scr[h] += jnp.dot(ds.astype(kb.dtype), kh, preferred_element_type=F32)

        def step(j, _):
            edge = (j == 0) | (j == i)
            for h in range(2):
                live = skip_ref[((2 * pr + h) * nb + i) * nb + j] == 0
                pl.when(live & edge)(functools.partial(tile, j, h, True))
                pl.when(live & jnp.logical_not(edge))(functools.partial(tile, j, h, False))
            return 0

        lax.fori_loop(jfirst_ref[pr * nb + i], i + 1, step, 0)
        dq_ref[...] = (jnp.where(sel0, dq_scr[0], dq_scr[1]) * scale).astype(dq_ref.dtype)
        dc_ref[0, :, 0:1] = dq_scr[0][:, HD:HD + 1]
        dc_ref[0, :, 1:2] = dq_scr[1][:, 0:1]

    blk = lambda off: pl.BlockSpec((tb, 2 * HD), lambda p, i, *_: (i, off + p))
    full = lambda off: pl.BlockSpec((T, 2 * HD), lambda p, i, *_: (0, off + p))
    rows = pl.BlockSpec((1, nb, 2, tb), lambda p, i, *_: (p, 0, 0, 0))
    return pl.pallas_call(
        body, name="fox_bwd_dq",
        grid_spec=pltpu.PrefetchScalarGridSpec(
            num_scalar_prefetch=3, grid=(npair, nb),
            in_specs=[blk(0), full(npair), full(2 * npair), blk(0), blk(0), rows, blk(0), blk(0)],
            out_specs=[blk(0), _pair_cols(tb)],
            scratch_shapes=[pltpu.VMEM((2, tb, 2 * HD), F32)]),
        out_shape=[jax.ShapeDtypeStruct((T, FW), _BF), jax.ShapeDtypeStruct((npair, T, 2), F32)],
        compiler_params=_cp("parallel", "arbitrary"))(*plan, qkv, qkv, qkv, do, c_cols, c_rows, lse, delta)


def _fox_bwd_dkv(qkv, do, c_cols, ce_rows, delta_rows, plan):
    T = qkv.shape[0]
    tb = _div(T, ATT_TILE)
    nb = T // tb
    npair = HEADS // 2
    scale = HD ** -0.5
    dn = (((1,), (1,)), ((), ()))

    def body(skip_ref, jfirst_ref, ilast_ref, q_ref, k_ref, v_ref, do_ref, ck_ref, ce_ref, dl_ref,
             dk_ref, dv_ref, dc_ref, dk_scr, dv_scr):
        pr = pl.program_id(0)
        jb = pl.program_id(1)
        k = k_ref[...]
        v = v_ref[...]
        lane = lax.broadcasted_iota(jnp.int32, (1, 2 * HD), 1)
        sel0 = lane < HD
        kh = (jnp.where(sel0, k, jnp.zeros_like(k)) * scale, jnp.where(sel0, jnp.zeros_like(k), k) * scale)
        vh = (jnp.where(sel0, v, jnp.zeros_like(v)), jnp.where(sel0, jnp.zeros_like(v), v))
        ck = (ck_ref[:, 0:1], ck_ref[:, HD:HD + 1])
        kidx = jb * tb + lax.broadcasted_iota(jnp.int32, (tb, 1), 0)
        dk_scr[...] = jnp.zeros(dk_scr.shape, F32)
        dv_scr[...] = jnp.zeros(dv_scr.shape, F32)

        def tile(i, h, masked):
            start = pl.multiple_of(i * tb, tb)
            qb = q_ref[pl.ds(start, tb), :]
            dob = do_ref[pl.ds(start, tb), :]
            one = jnp.ones_like(qb)
            qh = jnp.where(sel0, qb, one) if h == 0 else jnp.where(sel0, one, qb)
            t = lax.dot_general(kh[h], qb, dn, preferred_element_type=F32) - ck[h]
            ce = ce_ref[0, i, h:h + 1, :]
            if masked:
                qidx = i * tb + lax.broadcasted_iota(jnp.int32, (1, tb), 1)
                mask = (kidx <= qidx) & (kidx >= FRONT)
                pt = jnp.where(mask, jnp.exp(jnp.where(mask, t, NEG) + ce), 0.0)
            else:
                pt = jnp.exp(t + ce)
            dv_scr[h] += jnp.dot(pt.astype(dob.dtype), dob, preferred_element_type=F32)
            dpt = lax.dot_general(vh[h], dob, dn, preferred_element_type=F32)
            dst = pt * (dpt - dl_ref[0, i, h:h + 1, :])
            dk_scr[h] += jnp.dot(dst.astype(qb.dtype), qh, preferred_element_type=F32)

        def step(i, _):
            edge = (i == jb) | (jb == 0)
            for h in range(2):
                live = skip_ref[((2 * pr + h) * nb + i) * nb + jb] == 0
                pl.when(live & edge)(functools.partial(tile, i, h, True))
                pl.when(live & jnp.logical_not(edge))(functools.partial(tile, i, h, False))
            return 0

        lax.fori_loop(jb, ilast_ref[pr * nb + jb] + 1, step, 0)
        dk_ref[...] = (jnp.where(sel0, dk_scr[0], dk_scr[1]) * scale).astype(dk_ref.dtype)
        dv_ref[...] = jnp.where(sel0, dv_scr[0], dv_scr[1]).astype(dv_ref.dtype)
        dc_ref[0, :, 0:1] = -dk_scr[0][:, HD:HD + 1]
        dc_ref[0, :, 1:2] = -dk_scr[1][:, 0:1]

    blk = lambda off: pl.BlockSpec((tb, 2 * HD), lambda p, j, *_: (j, off + p))
    full = lambda off: pl.BlockSpec((T, 2 * HD), lambda p, j, *_: (0, off + p))
    rows = pl.BlockSpec((1, nb, 2, tb), lambda p, j, *_: (p, 0, 0, 0))
    return pl.pallas_call(
        body, name="fox_bwd_dkv",
        grid_spec=pltpu.PrefetchScalarGridSpec(
            num_scalar_prefetch=3, grid=(npair, nb),
            in_specs=[full(0), blk(npair), blk(2 * npair), full(0), blk(0), rows, rows],
            out_specs=[blk(0), blk(0), _pair_cols(tb)],
            scratch_shapes=[pltpu.VMEM((2, tb, 2 * HD), F32), pltpu.VMEM((2, tb, 2 * HD), F32)]),
        out_shape=[jax.ShapeDtypeStruct((T, FW), _BF)] * 2 + [jax.ShapeDtypeStruct((npair, T, 2), F32)],
        compiler_params=_cp("parallel", "arbitrary"))(*plan, qkv, qkv, qkv, do, c_cols, ce_rows, delta_rows)


def _head_dot(a, b, ones_blk, ones_col):
    T = a.shape[0]
    tb = _div(T, ROW_TILE, 8)

    def body(a_ref, b_ref, e_ref, c_ref, o_ref, o8_ref):
        prod = a_ref[...].astype(F32) * b_ref[...].astype(F32)
        o_ref[...] = jnp.dot(prod, e_ref[...], precision=HI, preferred_element_type=F32)
        o8_ref[...] = jnp.dot(prod, c_ref[...], precision=HI, preferred_element_type=F32)

    row = pl.BlockSpec((tb, FW), lambda i: (i, 0))
    return pl.pallas_call(
        body, name="head_dot", grid=(T // tb,),
        in_specs=[row, row, pl.BlockSpec((FW, FW), lambda i: (0, 0)), pl.BlockSpec((FW, SMALL_W), lambda i: (0, 0))],
        out_specs=[row, pl.BlockSpec((tb, SMALL_W), lambda i: (i, 0))],
        out_shape=[jax.ShapeDtypeStruct((T, FW), F32), jax.ShapeDtypeStruct((T, SMALL_W), F32)],
        compiler_params=_cp("parallel"))(a, b, ones_blk, ones_col)


CONV_RC = 32
CONV_TC = 256


def _stage_prev(scr, x_ref, h_ref, first, tb):
    scr[0:8, :] = jnp.where(first, 0.0, h_ref[...])
    scr[8:8 + tb, :] = x_ref[...]


def _windows_prev(scr, r, kw):
    x = scr[pl.ds(pl.multiple_of(r * CONV_RC, CONV_RC), CONV_RC + 8), :]
    return [x[8:] if k == kw - 1 else pltpu.roll(x, kw - 1 - k, 0)[8:] for k in range(kw)]


def _fold8(a):
    return a.reshape(CONV_RC // 8, 8, a.shape[-1]).sum(axis=0)


def _halo_prev(tb, tc, off=0):
    return pl.BlockSpec((8, tc), lambda j, i: (jnp.maximum(i * (tb // 8) - 1, 0), j + off))


def _gconv_fwd(rest, w):
    T = rest.shape[0]
    C = 3 * FW
    kw = w.shape[0]
    tb, tc = _div(T, 1280, CONV_RC), CONV_TC

    def body(x_ref, h_ref, w_ref, o_ref, scr):
        _stage_prev(scr, x_ref, h_ref, pl.program_id(1) == 0, tb)
        wv = w_ref[...]

        def chunk(r, _):
            win = _windows_prev(scr, r, kw)
            u = sum(wv[k:k + 1, :] * win[k] for k in range(kw))
            o_ref[pl.ds(pl.multiple_of(r * CONV_RC, CONV_RC), CONV_RC), :] = u * _sig(u)
            return 0

        lax.fori_loop(0, tb // CONV_RC, chunk, 0)

    return pl.pallas_call(
        body, name="gconv_fwd", grid=(C // tc, T // tb),
        in_specs=[pl.BlockSpec((tb, tc), lambda j, i: (i, j)), _halo_prev(tb, tc), pl.BlockSpec((kw, tc), lambda j, i: (0, j))],
        out_specs=pl.BlockSpec((tb, tc), lambda j, i: (i, j)), out_shape=jax.ShapeDtypeStruct((T, C), F32),
        scratch_shapes=[pltpu.VMEM((tb + 8, tc), F32)], compiler_params=_cp("parallel", "arbitrary"))(rest, rest, w)


def _gconv_bwd_du(rest, w, dy):
    T = rest.shape[0]
    C = 3 * FW
    kw = w.shape[0]
    tb, tc = _div(T, 1280, CONV_RC), CONV_TC

    def body(x_ref, h_ref, w_ref, dy_ref, du_ref, dw_ref, scr):
        i = pl.program_id(1)
        _stage_prev(scr, x_ref, h_ref, i == 0, tb)
        wv = w_ref[...]

        def chunk(r, acc):
            rows = pl.ds(pl.multiple_of(r * CONV_RC, CONV_RC), CONV_RC)
            win = _windows_prev(scr, r, kw)
            u = sum(wv[k:k + 1, :] * win[k] for k in range(kw))
            sg = _sig(u)
            du = dy_ref[rows, :] * sg * (1.0 + u * (1.0 - sg))
            du_ref[rows, :] = du
            return tuple(acc[k] + _fold8(du * win[k]) for k in range(kw))

        acc = lax.fori_loop(0, tb // CONV_RC, chunk, tuple(jnp.zeros((8, tc), F32) for _ in range(kw)))
        part = jnp.concatenate([jnp.sum(a, axis=0, keepdims=True) for a in acc], axis=0)

        @pl.when(i == 0)
        def _():
            dw_ref[...] = part

        @pl.when(i > 0)
        def _():
            dw_ref[...] += part

    blk = pl.BlockSpec((tb, tc), lambda j, i: (i, j))
    wsp = pl.BlockSpec((kw, tc), lambda j, i: (0, j))
    return pl.pallas_call(
        body, name="gconv_bwd_du", grid=(C // tc, T // tb), in_specs=[blk, _halo_prev(tb, tc), wsp, blk],
        out_specs=[blk, wsp], out_shape=[jax.ShapeDtypeStruct((T, C), F32), jax.ShapeDtypeStruct((kw, C), F32)],
        scratch_shapes=[pltpu.VMEM((tb + 8, tc), F32)], compiler_params=_cp("parallel", "arbitrary"))(rest, rest, w, dy)


def _conv_bwd_dx(du, w, out_dtype, name):
    T, C = du.shape
    kw = w.shape[0]
    tb = _div(T, 1280, CONV_RC)
    tc = CONV_TC
    nb = T // tb

    def body(x_ref, h_ref, w_ref, o_ref, scr):
        scr[0:tb, :] = x_ref[...]
        scr[tb:tb + 8, :] = jnp.where(pl.program_id(1) == nb - 1, 0.0, h_ref[...])
        wv = w_ref[...]

        def chunk(r, _):
            start = pl.multiple_of(r * CONV_RC, CONV_RC)
            x = scr[pl.ds(start, CONV_RC + 8), :]
            acc = wv[kw - 1:kw, :] * x[:CONV_RC]
            for k in range(kw - 1):
                acc = acc + wv[k:k + 1, :] * pltpu.roll(x, CONV_RC + 8 - (kw - 1 - k), 0)[:CONV_RC]
            o_ref[pl.ds(start, CONV_RC), :] = acc.astype(o_ref.dtype)
            return 0

        lax.fori_loop(0, tb // CONV_RC, chunk, 0)

    halo = pl.BlockSpec((8, tc), lambda j, i: (jnp.minimum((i + 1) * (tb // 8), T // 8 - 1), j))
    return pl.pallas_call(
        body, name=name, grid=(C // tc, nb),
        in_specs=[pl.BlockSpec((tb, tc), lambda j, i: (i, j)), halo, pl.BlockSpec((kw, tc), lambda j, i: (0, j))],
        out_specs=pl.BlockSpec((tb, tc), lambda j, i: (i, j)), out_shape=jax.ShapeDtypeStruct((T, C), out_dtype),
        scratch_shapes=[pltpu.VMEM((tb + 8, tc), F32)], compiler_params=_cp("parallel", "arbitrary"))(du, du, w)


def _glu_fwd(up, w, b):
    T = up.shape[0]
    kw = w.shape[0]
    tb, tc = _div(T, 1280, CONV_RC), CONV_TC
    nc = DFF // tc

    def body(xg, hg, xu, hu, wg, wu, bg, bu, o_ref, sg, su):
        first = pl.program_id(1) == 0
        _stage_prev(sg, xg, hg, first, tb)
        _stage_prev(su, xu, hu, first, tb)
        wgv, wuv, bgv, buv = wg[...], wu[...], bg[...], bu[...]

        def chunk(r, _):
            wing, winu = _windows_prev(sg, r, kw), _windows_prev(su, r, kw)
            ug = bgv + sum(wgv[k:k + 1, :] * wing[k] for k in range(kw))
            uu = buv + sum(wuv[k:k + 1, :] * winu[k] for k in range(kw))
            o_ref[pl.ds(pl.multiple_of(r * CONV_RC, CONV_RC), CONV_RC), :] = (ug * _sig(ug) * uu).astype(o_ref.dtype)
            return 0

        lax.fori_loop(0, tb // CONV_RC, chunk, 0)

    blk = lambda off: pl.BlockSpec((tb, tc), lambda j, i: (i, j + off))
    wsp = lambda off: pl.BlockSpec((kw, tc), lambda j, i: (0, j + off))
    bsp = lambda off: pl.BlockSpec((1, tc), lambda j, i: (0, j + off))
    return pl.pallas_call(
        body, name="glu_fwd", grid=(nc, T // tb),
        in_specs=[blk(0), _halo_prev(tb, tc), blk(nc), _halo_prev(tb, tc, nc), wsp(0), wsp(nc), bsp(0), bsp(nc)],
        out_specs=blk(0), out_shape=jax.ShapeDtypeStruct((T, DFF), _BF),
        scratch_shapes=[pltpu.VMEM((tb + 8, tc), F32)] * 2, compiler_params=_cp("parallel", "arbitrary"))(
            up, up, up, up, w, w, b, b)


def _glu_bwd_du(up, w, b, df):
    T = up.shape[0]
    kw = w.shape[0]
    tb, tc = _div(T, 1280, CONV_RC), CONV_TC
    nc = DFF // tc

    def body(xg, hg, xu, hu, wg, wu, bg, bu, df_ref, dug_ref, duu_ref, dwg_ref, dwu_ref, dbg_ref, dbu_ref, sg, su):
        i = pl.program_id(1)
        _stage_prev(sg, xg, hg, i == 0, tb)
        _stage_prev(su, xu, hu, i == 0, tb)
        wgv, wuv, bgv, buv = wg[...], wu[...], bg[...], bu[...]

        def chunk(r, acc):
            rows = pl.ds(pl.multiple_of(r * CONV_RC, CONV_RC), CONV_RC)
            wing, winu = _windows_prev(sg, r, kw), _windows_prev(su, r, kw)
            ug = bgv + sum(wgv[k:k + 1, :] * wing[k] for k in range(kw))
            uu = buv + sum(wuv[k:k + 1, :] * winu[k] for k in range(kw))
            s = _sig(ug)
            df = df_ref[rows, :]
            dug = df * uu * s * (1.0 + ug * (1.0 - s))
            duu = df * ug * s
            dug_ref[rows, :] = dug
            duu_ref[rows, :] = duu
            new = [dug * wing[k] for k in range(kw)] + [duu * winu[k] for k in range(kw)] + [dug, duu]
            return tuple(a + _fold8(v) for a, v in zip(acc, new))

        acc = lax.fori_loop(0, tb // CONV_RC, chunk, tuple(jnp.zeros((8, tc), F32) for _ in range(2 * kw + 2)))
        col = [jnp.sum(a, axis=0, keepdims=True) for a in acc]
        parts = (jnp.concatenate(col[:kw], axis=0), jnp.concatenate(col[kw:2 * kw], axis=0), col[2 * kw], col[2 * kw + 1])
        accs = (dwg_ref, dwu_ref, dbg_ref, dbu_ref)

        @pl.when(i == 0)
        def _():
            for r, p in zip(accs, parts):
                r[...] = p

        @pl.when(i > 0)
        def _():
            for r, p in zip(accs, parts):
                r[...] += p

    blk = lambda off: pl.BlockSpec((tb, tc), lambda j, i: (i, j + off))
    wsp = lambda off: pl.BlockSpec((kw, tc), lambda j, i: (0, j + off))
    bsp = lambda off: pl.BlockSpec((1, tc), lambda j, i: (0, j + off))
    return pl.pallas_call(
        body, name="glu_bwd_du", grid=(nc, T // tb),
        in_specs=[blk(0), _halo_prev(tb, tc), blk(nc), _halo_prev(tb, tc, nc), wsp(0), wsp(nc), bsp(0), bsp(nc), blk(0)],
        out_specs=[blk(0), blk(0), wsp(0), wsp(0), bsp(0), bsp(0)],
        out_shape=[jax.ShapeDtypeStruct((T, DFF), F32)] * 2 + [jax.ShapeDtypeStruct((kw, DFF), F32)] * 2
        + [jax.ShapeDtypeStruct((1, DFF), F32)] * 2,
        scratch_shapes=[pltpu.VMEM((tb + 8, tc), F32)] * 2, compiler_params=_cp("parallel", "arbitrary"))(
            up, up, up, up, w, w, b, b, df)


def _mix_fwd(rest, gate_bias, y_fox, y_gdn):
    T = rest.shape[0]
    tb, tc = _div(T, ROW_TILE, 8), 512
    nc = D // tc
    og = OFF_GATES // tc

    def body(gf, gg, bf, bg, yf, yg, o_ref):
        o_ref[...] = (_sig(gf[...] + bf[...]) * yf[...] + _sig(gg[...] + bg[...]) * yg[...]).astype(o_ref.dtype)

    blk = lambda off: pl.BlockSpec((tb, tc), lambda i, j: (i, j + off))
    bsp = lambda off: pl.BlockSpec((1, tc), lambda i, j: (0, j + off))
    return pl.pallas_call(
        body, name="mix_fwd", grid=(T // tb, nc),
        in_specs=[blk(og), blk(og + nc), bsp(0), bsp(nc), blk(0), blk(0)], out_specs=blk(0),
        out_shape=jax.ShapeDtypeStruct((T, D), _BF), compiler_params=_cp("parallel", "parallel"))(
            rest, rest, gate_bias, gate_bias, y_fox, y_gdn)


def _mix_bwd(rest, gate_bias, y_fox, y_gdn, dmix):
    T = rest.shape[0]
    tb, tc = _div(T, ROW_TILE, 8), 512
    nc = D // tc
    og = OFF_GATES // tc

    def body(gf, gg, bf, bg, yf, yg, dm, dyf, dyg, dgf, dgg, dbf, dbg):
        i = pl.program_id(1)
        d = dm[...]
        sf = _sig(gf[...] + bf[...])
        sg = _sig(gg[...] + bg[...])
        dyf[...] = (d * sf).astype(dyf.dtype)
        dyg[...] = (d * sg).astype(dyg.dtype)
        a = d * yf[...] * sf * (1.0 - sf)
        b = d * yg[...] * sg * (1.0 - sg)
        dgf[...] = a.astype(dgf.dtype)
        dgg[...] = b.astype(dgg.dtype)
        pa = jnp.sum(a, axis=0, keepdims=True)
        pb = jnp.sum(b, axis=0, keepdims=True)

        @pl.when(i == 0)
        def _():
            dbf[...] = pa
            dbg[...] = pb

        @pl.when(i > 0)
        def _():
            dbf[...] += pa
            dbg[...] += pb

    blk = lambda off: pl.BlockSpec((tb, tc), lambda j, i: (i, j + off))
    bsp = lambda off: pl.BlockSpec((1, tc), lambda j, i: (0, j + off))
    return pl.pallas_call(
        body, name="mix_bwd", grid=(nc, T // tb),
        in_specs=[blk(og), blk(og + nc), bsp(0), bsp(nc), blk(0), blk(0), blk(0)],
        out_specs=[blk(0), blk(0), blk(0), blk(0), bsp(0), bsp(0)],
        out_shape=[jax.ShapeDtypeStruct((T, D), _BF)] * 4 + [jax.ShapeDtypeStruct((1, D), F32)] * 2,
        compiler_params=_cp("parallel", "arbitrary"))(rest, rest, gate_bias, gate_bias, y_fox, y_gdn, dmix)


def _loss_head(h2, w, target, n_valid):
    T = h2.shape[0]
    tb = _div(T, ROW_TILE, 8)

    def body(h_ref, w_ref, t_ref, dh_ref, loss_ref, dw_ref):
        i = pl.program_id(0)
        x = h_ref[...]
        r = lax.rsqrt(jnp.mean(x * x, axis=-1, keepdims=True) + EPS)
        xh = x * r
        row = i * tb + lax.broadcasted_iota(jnp.int32, (tb, 1), 0)
        valid = (row >= X0) & (row < X0 + n_valid)
        e = jnp.where(valid, xh * w_ref[...] - t_ref[...], 0.0)
        dy = e * (1.0 / D)
        g = dy * w_ref[...]
        dh_ref[...] = r * (g - xh * jnp.mean(xh * g, axis=-1, keepdims=True))
        lpart = 0.5 * jnp.sum(jnp.sum(e * e, axis=-1, keepdims=True) * (1.0 / D), axis=0, keepdims=True)
        wpart = jnp.sum(dy * xh, axis=0, keepdims=True)

        @pl.when(i == 0)
        def _():
            loss_ref[...] = lpart
            dw_ref[...] = wpart

        @pl.when(i > 0)
        def _():
            loss_ref[...] += lpart
            dw_ref[...] += wpart

    row = pl.BlockSpec((tb, D), lambda i: (i, 0))
    one = pl.BlockSpec((1, D), lambda i: (0, 0))
    return pl.pallas_call(
        body, name="loss_head", grid=(T // tb,), in_specs=[row, one, row],
        out_specs=[row, pl.BlockSpec((1, 1), lambda i: (0, 0)), one],
        out_shape=[jax.ShapeDtypeStruct((T, D), F32), jax.ShapeDtypeStruct((1, 1), F32), jax.ShapeDtypeStruct((1, D), F32)],
        compiler_params=_cp("arbitrary"))(h2, w, target)


def _bd_lo(a, b, ca, cb):
    return lax.dot_general(a.astype(_BF), b.astype(_BF), (((ca,), (cb,)), ((0,), (0,))), preferred_element_type=F32)


def _split2(a):
    hi = a.astype(_BF)
    return hi, (a - hi.astype(F32)).astype(_BF)


def _bd_hi(a, b, ca, cb, exact_a=False):
    dn = (((ca,), (cb,)), ((0,), (0,)))
    dot = lambda x, y: lax.dot_general(x, y, dn, preferred_element_type=F32)
    bh, bl = _split2(b)
    if exact_a:
        ah = a.astype(_BF)
        return dot(ah, bh) + dot(ah, bl)
    ah, al = _split2(a)
    return dot(ah, bh) + (dot(ah, bl) + dot(al, bh))


def _row_to_col(x):
    eye = lax.broadcasted_iota(jnp.int32, (1, CH, CH), 1) == lax.broadcasted_iota(jnp.int32, (1, CH, CH), 2)
    return jnp.sum(jnp.where(eye, jnp.broadcast_to(x, (HEADS, CH, CH)), 0.0), axis=2, keepdims=True)


def _col_to_row(x):
    eye = lax.broadcasted_iota(jnp.int32, (1, CH, CH), 1) == lax.broadcasted_iota(jnp.int32, (1, CH, CH), 2)
    return jnp.sum(jnp.where(eye, jnp.broadcast_to(x, (HEADS, CH, CH)), 0.0), axis=1, keepdims=True)


def _gdn_chunk(q, k, v, bpre, apre, alog, dtb):
    H = HEADS
    r = lax.broadcasted_iota(jnp.int32, (1, CH, CH), 1)
    c = lax.broadcasted_iota(jnp.int32, (1, CH, CH), 2)
    tril, strict = r >= c, r > c
    lb = jnp.broadcast_to(tril.astype(F32), (H, CH, CH))
    rq = lax.rsqrt(jnp.sum(q * q, axis=-1, keepdims=True) + EPS)
    rk = lax.rsqrt(jnp.sum(k * k, axis=-1, keepdims=True) + EPS)
    qh = q * rq
    qn = qh * (HD ** -0.5)
    kn = k * rk
    beta = _sig(bpre)
    x = apre + dtb
    ea = jnp.exp(alog)
    g = -ea * jax.nn.softplus(x)
    gb = jnp.broadcast_to(g, (H, CH, HD))
    gc = _bd_hi(lb, gb, 2, 1, True)
    dm = gc - _col_to_row(gc[:, :, 0:1])
    decay = jnp.where(tril, jnp.exp(jnp.where(tril, dm, 0.0)), 0.0)
    eg = jnp.exp(gc)
    gl = gc[:, CH - 1:CH, :]
    egl = jnp.exp(gl - gc)
    cd = jnp.exp(gl)
    kb = kn * beta
    vb = v * beta
    both = _bd_lo(_rows2(kb, qn), kn, 2, 2)
    kk, qk = both[:, :CH], both[:, CH:]
    pw = -jnp.where(strict, kk * decay, 0.0)
    tm = jnp.where(r == c, 1.0, 0.0) + pw
    pw = _bd_hi(pw, pw, 2, 1)
    for it in range(4):
        mul = _bd_hi if it < 2 else _bd_lo
        both = mul(_rows2(pw, tm), pw, 2, 1)
        pw, tm = both[:, :CH], tm + both[:, CH:]
    tm = tm + _bd_lo(tm, pw, 2, 1)
    kbg = kb * eg
    sol = _bd_hi(tm, _lanes2(vb, kbg), 2, 1)
    value, kcd = sol[:, :, :HD], sol[:, :, HD:]
    attn = jnp.where(tril, qk * decay, 0.0)
    return dict(tril=tril, strict=strict, lb=lb, rq=rq, rk=rk, qh=qh, qn=qn, kn=kn, beta=beta, x=x, ea=ea, g=g,
                decay=decay, eg=eg, egl=egl, cd=cd, kb=kb, vb=vb, kk=kk, tm=tm, kbg=kbg, value=value, kcd=kcd, qk=qk,
                attn=attn, qd=qn * eg, kt=kn * egl, sol=sol)


def _rows2(a, b):
    return jnp.concatenate([a, b], axis=1)


def _lanes2(a, b):
    return jnp.concatenate([a, b], axis=2)


GDN_CPS = 2


def _gdn_specs(T, rev):
    G = GDN_CPS
    ns = T // (G * CH)
    pos = (lambda n: ns - 1 - n) if rev else (lambda n: n)
    mat = pl.BlockSpec((HEADS, G * CH, HD), lambda n: (0, pos(n), 0))
    col = pl.BlockSpec((G, HEADS, 1, CH), lambda n: (pos(n), 0, 0, 0))
    sca = pl.BlockSpec((HEADS, 1, 1), lambda n: (0, 0, 0))
    nw = pl.BlockSpec((1, 1, HD), lambda n: (0, 0, 0))
    st = pl.BlockSpec((G, HEADS, HD, HD), lambda n: (pos(n), 0, 0, 0))
    tok = lambda width, off: pl.BlockSpec((G * CH, width), lambda n: (pos(n), off))
    return ns, mat, col, sca, nw, st, tok


def _split_heads(x):
    return [jnp.stack([x[:, (g * HEADS + h) * HD:(g * HEADS + h + 1) * HD] for h in range(HEADS)])
            for g in range(x.shape[1] // FW)]


def _store_heads(ref, rows, g, val):
    for h in range(HEADS):
        ref[rows, (g * HEADS + h) * HD:(g * HEADS + h + 1) * HD] = val[h]


def _gdn_fwd(conv, rest, bpre, apre, alog, dtb, nw):
    T = conv.shape[0]
    ns, mat, col, sca, nws, st, tok = _gdn_specs(T, False)
    rows = [slice(g * CH, (g + 1) * CH) for g in range(GDN_CPS)]

    def body(c_ref, z_ref, b_ref, a_ref, al_ref, dt_ref, nw_ref, o_ref, og_ref, st_ref, s_scr):
        @pl.when(pl.program_id(0) == 0)
        def _():
            s_scr[...] = jnp.zeros_like(s_scr)

        locs = [_gdn_chunk(*_split_heads(c_ref[rows[g], :]), _row_to_col(b_ref[g]), _row_to_col(a_ref[g]), al_ref[...],
                           dt_ref[...]) for g in range(GDN_CPS)]
        s = s_scr[...]
        for g, L in enumerate(locs):
            st_ref[g] = s
            both = _bd_lo(_rows2(L["kcd"], L["qd"]), s, 2, 1)
            v_new = L["value"] - both[:, :CH]
            o = both[:, CH:] + _bd_lo(L["attn"], v_new, 2, 1)
            s = s * L["cd"] + _bd_lo(L["kt"], v_new, 1, 1)
            o_ref[:, rows[g], :] = o
            zz, = _split_heads(z_ref[rows[g], :])
            rs = lax.rsqrt(jnp.mean(o * o, axis=-1, keepdims=True) + EPS)
            _store_heads(og_ref, rows[g], 0, o * rs * nw_ref[...] * zz * _sig(zz))
        s_scr[...] = s

    return pl.pallas_call(
        body, name="gdn_fwd", grid=(ns,), in_specs=[tok(3 * FW, 0), tok(FW, OFF_Z // FW), col, col, sca, sca, nws],
        out_specs=[mat, tok(FW, 0), st],
        out_shape=[jax.ShapeDtypeStruct((HEADS, T, HD), F32), jax.ShapeDtypeStruct((T, FW), F32),
                   jax.ShapeDtypeStruct((T // CH, HEADS, HD, HD), F32)],
        scratch_shapes=[pltpu.VMEM((HEADS, HD, HD), F32)], compiler_params=_cp("arbitrary"))(
            conv, rest, bpre, apre, alog, dtb, nw)


def _gdn_bwd(conv, rest, bpre, apre, alog, dtb, nw, states, o, dog):
    T = conv.shape[0]
    ns, mat, col, sca, nws, st, tok = _gdn_specs(T, True)
    rows = [slice(g * CH, (g + 1) * CH) for g in range(GDN_CPS)]

    def body(c_ref, z_ref, b_ref, a_ref, al_ref, dt_ref, nw_ref, st_ref, o_ref, dog_ref,
             dc_ref, dz_ref, db_ref, da_ref, dal_ref, ddt_ref, dnw_ref, ds_scr):
        @pl.when(pl.program_id(0) == 0)
        def _():
            ds_scr[...] = jnp.zeros_like(ds_scr)
            dal_ref[...] = jnp.zeros_like(dal_ref)
            ddt_ref[...] = jnp.zeros_like(ddt_ref)
            dnw_ref[...] = jnp.zeros_like(dnw_ref)

        splits = [_split_heads(c_ref[rows[g], :]) for g in range(GDN_CPS)]
        locs = [_gdn_chunk(*splits[g], _row_to_col(b_ref[g]), _row_to_col(a_ref[g]), al_ref[...], dt_ref[...])
                for g in range(GDN_CPS)]
        dsn = ds_scr[...]
        acc_al, acc_dt, acc_nw = 0.0, 0.0, 0.0
        for g in reversed(range(GDN_CPS)):
            L, vv = locs[g], splits[g][2]
            tril, strict, lb = L["tril"], L["strict"], L["lb"]
            qn, kn, kb, beta, decay, eg, egl, cd = L["qn"], L["kn"], L["kb"], L["beta"], L["decay"], L["eg"], L["egl"], L["cd"]
            value, kcd, attn, qd, kt, tm = L["value"], L["kcd"], L["attn"], L["qd"], L["kt"], L["tm"]
            s = st_ref[g]
            v_new = value - _bd_lo(kcd, s, 2, 1)
            oo = o_ref[:, rows[g], :]
            zz, = _split_heads(z_ref[rows[g], :])
            dog_, = _split_heads(dog_ref[rows[g], :])
            sz = _sig(zz)
            rs = lax.rsqrt(jnp.mean(oo * oo, axis=-1, keepdims=True) + EPS)
            oh = oo * rs
            _store_heads(dz_ref, rows[g], 0, dog_ * oh * nw_ref[...] * sz * (1.0 + zz * (1.0 - sz)))
            don = dog_ * zz * sz
            gdy = don * nw_ref[...]
            do = rs * (gdy - oh * jnp.mean(oh * gdy, axis=-1, keepdims=True))
            acc_nw = acc_nw + jnp.sum(don * oh, axis=(0, 1), keepdims=True)
            d_vnew = _bd_lo(attn, do, 1, 1) + _bd_lo(kt, dsn, 2, 1)
            both = _bd_lo(_rows2(do, d_vnew), s, 2, 2)
            d_qd, d_kcd = both[:, :CH], -both[:, CH:]
            d_attn = jnp.where(tril, _bd_lo(do, v_new, 2, 2), 0.0)
            d_kt = _bd_lo(v_new, dsn, 2, 2)
            d_cd = jnp.sum(s * dsn, axis=(1, 2), keepdims=True)
            dsn_next = cd * dsn + _bd_lo(_rows2(qd, kcd), _rows2(do, -d_vnew), 1, 1)
            dsol = _bd_hi(tm, _lanes2(d_vnew, d_kcd), 1, 1)
            d_vb, d_kbg = dsol[:, :, :HD], dsol[:, :, HD:]
            da = -jnp.where(strict, _bd_lo(dsol, L["sol"], 2, 2), 0.0)
            dkk = da * decay
            dqk = d_attn * decay
            d_decay = da * L["kk"] + d_attn * L["qk"]
            both = _bd_lo(_rows2(dkk, dqk), kn, 2, 1)
            d_kb = both[:, :CH] + d_kbg * eg
            d_qn = both[:, CH:] + d_qd * eg
            d_kn = _bd_lo(_rows2(dkk, dqk), _rows2(kb, qn), 1, 1) + d_kt * egl
            dd = d_decay * decay
            d_gc = jnp.sum(dd + d_qd * qd + d_kbg * L["kbg"] - d_kt * kt, axis=-1, keepdims=True) \
                - _row_to_col(jnp.sum(dd, axis=1, keepdims=True))
            d_gl = jnp.sum(d_kt * kt, axis=(1, 2), keepdims=True) + d_cd * cd[:, :, 0:1]
            last = lax.broadcasted_iota(jnp.int32, (1, CH, 1), 1) == CH - 1
            d_gc = d_gc + jnp.where(last, d_gl, 0.0)
            dg = _bd_hi(lb, jnp.broadcast_to(d_gc, (HEADS, CH, HD)), 1, 1, True)[:, :, 0:1]
            d_apre = -dg * L["ea"] * _sig(L["x"])
            da_ref[g] = _col_to_row(d_apre)
            acc_al = acc_al + jnp.sum(dg * L["g"], axis=1, keepdims=True)
            acc_dt = acc_dt + jnp.sum(d_apre, axis=1, keepdims=True)
            d_beta = jnp.sum(d_kb * kn + d_vb * vv, axis=-1, keepdims=True)
            db_ref[g] = _col_to_row(d_beta * beta * (1.0 - beta))
            d_kn = d_kn + d_kb * beta
            qh = L["qh"]
            _store_heads(dc_ref, rows[g], 0, (HD ** -0.5) * L["rq"] * (d_qn - qh * jnp.sum(qh * d_qn, axis=-1, keepdims=True)))
            _store_heads(dc_ref, rows[g], 1, L["rk"] * (d_kn - kn * jnp.sum(kn * d_kn, axis=-1, keepdims=True)))
            _store_heads(dc_ref, rows[g], 2, d_vb * beta)
            dsn = dsn_next
        ds_scr[...] = dsn
        dal_ref[...] += acc_al
        ddt_ref[...] += acc_dt
        dnw_ref[...] += acc_nw

    c3 = jax.ShapeDtypeStruct((T // CH, HEADS, 1, CH), F32)
    s3 = jax.ShapeDtypeStruct((HEADS, 1, 1), F32)
    return pl.pallas_call(
        body, name="gdn_bwd", grid=(ns,),
        in_specs=[tok(3 * FW, 0), tok(FW, OFF_Z // FW), col, col, sca, sca, nws, st, mat, tok(FW, 0)],
        out_specs=[tok(3 * FW, 0), tok(FW, 0), col, col, sca, sca, nws],
        out_shape=[jax.ShapeDtypeStruct((T, 3 * FW), F32), jax.ShapeDtypeStruct((T, FW), F32), c3, c3, s3, s3,
                   jax.ShapeDtypeStruct((1, 1, HD), F32)],
        scratch_shapes=[pltpu.VMEM((HEADS, HD, HD), F32)], compiler_params=_cp("arbitrary"))(
            conv, rest, bpre, apre, alog, dtb, nw, states, o, dog)


def _local_step(x, target, w):
    seq = x.shape[0]
    T = -(-(X0 + seq) // T_ALIGN) * T_ALIGN
    back = T - X0 - seq
    tb_att = _div(T, ATT_TILE)
    h0 = jnp.concatenate([jnp.zeros((FRONT, D), F32), w["meta"], x, jnp.zeros((back, D), F32)], axis=0)
    tgt = jnp.concatenate([jnp.zeros((X0, D), F32), target, jnp.zeros((back, D), F32)], axis=0)
    row = lambda v: v.reshape(1, -1)
    nmix, nffn, nfin = row(w["norm_mix"]), row(w["norm_ffn"]), row(w["norm_final"])
    gate_b = row(w["gate_bias"])
    fconv_b = row(w["ffn_conv_b"])
    bias128 = jnp.zeros((1, SMALL_W), F32).at[0, :HEADS].set(w["fgt_bias"])
    alog = w["a_log"].reshape(HEADS, 1, 1)
    dtb = w["dt_bias"].reshape(HEADS, 1, 1)
    gnw = w["gdn_norm"].reshape(1, 1, HD)

    a = _rms_fwd(h0, nmix, "rms_mix")
    pf = _mm(a, w["w_fox"], out_dtype=_BF, name="proj_fox")
    rest = _mm(a, w["w_rest"], name="proj_rest")
    small = rest[:, OFF_SMALL:]
    c8 = _fgate_fwd(small, bias128)[:, :HEADS]
    c_cols, c_rows = _att_cols(c8), _att_rows(c8, tb_att)
    ones_blk = jnp.asarray(np.kron(np.eye(HEADS, dtype=np.float32), np.ones((HD, HD), np.float32)))
    plan = _att_plan(_att_stats(pf, c_cols, ones_blk))
    o_fox, lse, lse2 = _fox_fwd(pf, c_cols, c_rows, plan)
    conv = _gconv_fwd(rest, w["gdn_conv"])
    chunk_rows = lambda a8: a8.reshape(T // CH, CH, HEADS).transpose(0, 2, 1)[:, :, None, :]
    bpre = chunk_rows(small[:, HEADS:2 * HEADS])
    apre = chunk_rows(small[:, 2 * HEADS:3 * HEADS])
    o_raw, og, states = _gdn_fwd(conv, rest, bpre, apre, alog, dtb, gnw)
    y_fox = _mm(o_fox, w["w_bfox"], name="y_fox")
    y_gdn = _mm(og, w["w_bgdn"], name="y_gdn")
    mix = _mix_fwd(rest, gate_b, y_fox, y_gdn)
    h1 = _mm(mix, w["w_out"], resid=h0, name="out_proj")
    b = _rms_fwd(h1, nffn, "rms_ffn")
    up = _mm(b, w["w_up"], name="ffn_up")
    f = _glu_fwd(up, w["ffn_conv"], fconv_b)
    h2 = _mm(f, w["w_down"], resid=h1, name="ffn_down")
    dh2, loss, d_nfin = _loss_head(h2, nfin, tgt, seq)

    d_f = _mm(dh2, w["w_down"], tb=True, name="d_f")
    g_down = _mm_t(f, dh2, "g_down")
    dug, duu, dwg, dwu, dbg, dbu = _glu_bwd_du(up, w["ffn_conv"], fconv_b, d_f)
    dxg = _conv_bwd_dx(dug, w["ffn_conv"][:, :DFF], _BF, "fconv_dx_gate")
    dxu = _conv_bwd_dx(duu, w["ffn_conv"][:, DFF:], _BF, "fconv_dx_up")
    d_b = _mm(dxg, w["w_up"][:, :DFF], tb=True, name="d_b_gate")
    d_b = _mm(dxu, w["w_up"][:, DFF:], tb=True, resid=d_b, name="d_b_up")
    g_up = jnp.concatenate([_mm_t(b, dxg, "g_up_gate"), _mm_t(b, dxu, "g_up_up")], axis=1)
    dh1, d_nffn = _rms_bwd(h1, nffn, d_b, dh2, "rms_ffn_bwd")

    dmix = _mm(dh1, w["w_out"], tb=True, name="d_mix")
    g_out = _mm_t(mix, dh1, "g_out")
    dyf, dyg, dgf, dgg, dgbf, dgbg = _mix_bwd(rest, gate_b, y_fox, y_gdn, dmix)
    do_fox = _mm(dyf, w["w_bfox"], tb=True, out_dtype=_BF, name="d_o_fox")
    g_bfox = _mm_t(o_fox, dyf, "g_bfox")
    d_og = _mm(dyg, w["w_bgdn"], tb=True, name="d_o_gdn")
    g_bgdn = _mm_t(og, dyg, "g_bgdn")

    ones_col = jnp.asarray(np.kron(np.eye(HEADS, SMALL_W, dtype=np.float32), np.ones((HD, 1), np.float32)))
    delta, delta8 = _head_dot(do_fox, o_fox, ones_blk, ones_col)
    dq, dcq = _fox_bwd_dq(pf, do_fox, c_cols, c_rows, lse, delta, plan)
    dk, dv, dck = _fox_bwd_dkv(pf, do_fox, c_cols, _att_rows(c8 - _from_pairs(lse2), tb_att),
                               _att_rows(delta8[:, :HEADS], tb_att), plan)
    dc = jnp.pad(_from_pairs(dcq + dck), ((0, 0), (0, SMALL_W - HEADS)))
    dfp, d_fb = _fgate_bwd(small, bias128, dc)

    dconv, dz, dbp, dap, d_alog, d_dtb, d_gnw = _gdn_bwd(conv, rest, bpre, apre, alog, dtb, gnw, states, o_raw, d_og)
    du_g, g_gconv = _gconv_bwd_du(rest, w["gdn_conv"], dconv)
    dgx = _conv_bwd_dx(du_g, w["gdn_conv"], _BF, "gconv_dx")
    token_rows = lambda a: a[:, :, 0, :].transpose(0, 2, 1).reshape(T, HEADS)
    dsmall = jnp.concatenate([dfp[:, :HEADS], token_rows(dbp), token_rows(dap),
                              jnp.zeros((T, SMALL_W - 3 * HEADS), F32)], axis=1)
    drest = jnp.concatenate([dgx, dz.astype(_BF), dgf, dgg, dsmall.astype(_BF)], axis=1)
    dfox = jnp.concatenate([dq, dk, dv], axis=1)
    d_a = _mm(dfox, w["w_fox"], tb=True, name="d_a_fox")
    d_a = _mm(drest, w["w_rest"], tb=True, resid=d_a, name="d_a_rest")
    g_fox = _mm_t(a, dfox, "g_w_fox")
    g_rest = _mm_t(a, drest, "g_w_rest")
    dh0, d_nmix = _rms_bwd(h0, nmix, d_a, dh1, "rms_mix_bwd")

    sm = lambda lo: g_rest[:, OFF_SMALL + lo:OFF_SMALL + lo + HEADS]
    g_w_in = jnp.concatenate([g_fox, sm(0), g_rest[:, :3 * FW], g_rest[:, OFF_Z:OFF_Z + FW], sm(HEADS), sm(2 * HEADS),
                              g_rest[:, OFF_GATES:OFF_GATES + 2 * D]], axis=1)
    grads = dict(
        meta_tokens=dh0[FRONT:X0], w_in=g_w_in, fgt_bias=d_fb[0, :HEADS], gdn_conv_w=g_gconv,
        gdn_a_log=d_alog.reshape(HEADS), gdn_dt_bias=d_dtb.reshape(HEADS), gdn_norm_w=d_gnw.reshape(HD),
        gate_bias=jnp.concatenate([dgbf, dgbg], axis=1).reshape(2 * D), w_branch_fox=g_bfox, w_branch_gdn=g_bgdn,
        w_out=g_out, norm_mix_w=d_nmix.reshape(D), norm_ffn_w=d_nffn.reshape(D), ffn_w_up=g_up,
        ffn_conv_w=jnp.concatenate([dwg, dwu], axis=1), ffn_conv_b=jnp.concatenate([dbg, dbu], axis=1).reshape(2 * DFF),
        ffn_w_down=g_down, norm_final_w=d_nfin.reshape(D))
    return loss, dh0[X0:X0 + seq], grads


N_CHIPS = 4
PACK_W = 1024
PACK_ROW_ALIGN = 32
BIG = ("w_in", "w_branch_fox", "w_branch_gdn", "w_out", "ffn_w_up", "ffn_w_down")
WEIGHTS = (
    ("meta_tokens", (N_META, D), 1), ("w_in", (D, 3 * FW + HEADS + 4 * FW + 2 * HEADS + 2 * D), 1), ("fgt_bias", (1, HEADS), None),
    ("gdn_conv_w", (4, 3 * FW), 1), ("gdn_a_log", (1, HEADS), None), ("gdn_dt_bias", (1, HEADS), None),
    ("gdn_norm_w", (1, HD), None), ("gate_bias", (1, 2 * D), None), ("w_branch_fox", (FW, D), 1),
    ("w_branch_gdn", (FW, D), 1), ("w_out", (D, D), 0), ("norm_mix_w", (1, D), None), ("norm_ffn_w", (1, D), None),
    ("ffn_w_up", (D, 2 * DFF), 1), ("ffn_conv_w", (3, 2 * DFF), 1), ("ffn_conv_b", (1, 2 * DFF), None),
    ("ffn_w_down", (DFF, D), 0), ("norm_final_w", (1, D), None))
SPLIT_F32 = ("meta_tokens", "gdn_conv_w", "ffn_conv_w")


def _shard_shape(shape, axis):
    if axis is None:
        return shape
    return tuple(s // N_CHIPS if a == axis else s for a, s in enumerate(shape))


def _shard_of(full, axis, q):
    if axis is None:
        return full
    n = full.shape[axis] // N_CHIPS
    return lax.slice_in_dim(full, q * n, (q + 1) * n, axis=axis)


def _pack_rows(n_elems):
    rows = -(-n_elems // PACK_W)
    return -(-rows // PACK_ROW_ALIGN) * PACK_ROW_ALIGN


def _pack(pieces, dtype):
    flat = jnp.concatenate([p.reshape(-1).astype(dtype) for p in pieces])
    rows = _pack_rows(flat.shape[0])
    return jnp.pad(flat, (0, rows * PACK_W - flat.shape[0])).reshape(rows, PACK_W)


def _unpack(slab, shapes):
    flat = slab.reshape(-1)
    out, off = [], 0
    for s in shapes:
        n = int(np.prod(s))
        out.append(flat[off:off + n].reshape(s))
        off += n
    return out


HBM_SPEC = pl.BlockSpec(memory_space=pltpu.HBM)
MESH_ID = pl.DeviceIdType.MESH


def _scatter_chips(srcs, name):
    n = len(srcs)

    def body(*refs):
        src_refs, out_refs = refs[:n], refs[n:2 * n]
        send_sems, recv_sems, local_sems = refs[2 * n:]
        x, y, c = lax.axis_index("x"), lax.axis_index("y"), lax.axis_index("c")
        q = 2 * x + y
        peers = [(1 - x, y), (x, 1 - y), (1 - x, 1 - y)]

        def remote(a, k, src_slot, dst_slot):
            px, py = peers[k]
            return pltpu.make_async_remote_copy(
                src_ref=src_refs[a].at[src_slot], dst_ref=out_refs[a].at[dst_slot], send_sem=send_sems.at[3 * a + k],
                recv_sem=recv_sems.at[3 * a + k], device_id=(px, py, c), device_id_type=MESH_ID)

        mine = [pltpu.make_async_copy(src_refs[a].at[q], out_refs[a].at[q], local_sems.at[a]) for a in range(n)]
        for cp in mine:
            cp.start()
        sends = [remote(a, k, 2 * px + py, q) for a in range(n) for k, (px, py) in enumerate(peers)]
        for cp in sends:
            cp.start()
        for a in range(n):
            for k, (px, py) in enumerate(peers):
                remote(a, k, 0, 2 * px + py).wait_recv()
        for cp in sends:
            cp.wait_send()
        for cp in mine:
            cp.wait()

    out_shape = [jax.ShapeDtypeStruct((N_CHIPS,) + s.shape[-2:], s.dtype) for s in srcs]
    return pl.pallas_call(
        body, name=name, in_specs=[HBM_SPEC] * n, out_specs=[HBM_SPEC] * n, out_shape=out_shape,
        scratch_shapes=[pltpu.SemaphoreType.DMA((3 * n,)), pltpu.SemaphoreType.DMA((3 * n,)), pltpu.SemaphoreType.DMA((n,))],
    )(*srcs)


def _gather_chips(srcs, name):
    n = len(srcs)

    def body(*refs):
        src_refs, out_refs = refs[:n], refs[n:2 * n]
        ici_send, ici_recv, d2d_send, d2d_recv, local_sems = refs[2 * n:]
        x, y, c = lax.axis_index("x"), lax.axis_index("y"), lax.axis_index("c")
        q = 2 * x + y
        peers = [(1 - x, y), (x, 1 - y), (1 - x, 1 - y)]

        def half(a, which):
            r = srcs[a].shape[0] // 2
            return pl.ds(which * r, r)

        def ici(a, k, slot):
            px, py = peers[k]
            return pltpu.make_async_remote_copy(
                src_ref=src_refs[a].at[half(a, c)], dst_ref=out_refs[a].at[slot, half(a, c)], send_sem=ici_send.at[3 * a + k],
                recv_sem=ici_recv.at[3 * a + k], device_id=(px, py, c), device_id_type=MESH_ID)

        def d2d(a, k, which):
            px, py = peers[k]
            rows = out_refs[a].at[2 * px + py, half(a, which)]
            return pltpu.make_async_remote_copy(
                src_ref=rows, dst_ref=rows, send_sem=d2d_send.at[3 * a + k], recv_sem=d2d_recv.at[3 * a + k],
                device_id=(x, y, 1 - c), device_id_type=MESH_ID)

        mine = [pltpu.make_async_copy(src_refs[a], out_refs[a].at[q], local_sems.at[a]) for a in range(n)]
        for cp in mine:
            cp.start()
        sends = [ici(a, k, q) for a in range(n) for k in range(3)]
        for cp in sends:
            cp.start()
        passed = []
        for a in range(n):
            for k, (px, py) in enumerate(peers):
                ici(a, k, 2 * px + py).wait_recv()
                passed.append(d2d(a, k, c))
                passed[-1].start()
        for a in range(n):
            for k in range(3):
                d2d(a, k, 1 - c).wait_recv()
        for cp in sends + passed:
            cp.wait_send()
        for cp in mine:
            cp.wait()

    out_shape = [jax.ShapeDtypeStruct((N_CHIPS,) + s.shape, s.dtype) for s in srcs]
    sems = [pltpu.SemaphoreType.DMA((3 * n,))] * 4 + [pltpu.SemaphoreType.DMA((n,))]
    return pl.pallas_call(body, name=name, in_specs=[HBM_SPEC] * n, out_specs=[HBM_SPEC] * n, out_shape=out_shape,
                          scratch_shapes=sems)(*srcs)


def _sibling_swap(slabs, name):
    n = len(slabs)

    def body(*refs):
        src_refs, out_refs, send_sems, recv_sems = refs[:n], refs[n:2 * n], refs[2 * n], refs[2 * n + 1]
        x, y, c = lax.axis_index("x"), lax.axis_index("y"), lax.axis_index("c")
        cps = [pltpu.make_async_remote_copy(src_ref=src_refs[a], dst_ref=out_refs[a], send_sem=send_sems.at[a],
                                            recv_sem=recv_sems.at[a], device_id=(x, y, 1 - c), device_id_type=MESH_ID)
               for a in range(n)]
        for cp in cps:
            cp.start()
        for cp in cps:
            cp.wait_recv()
        for cp in cps:
            cp.wait_send()

    return pl.pallas_call(
        body, name=name, in_specs=[HBM_SPEC] * n, out_specs=[HBM_SPEC] * n,
        out_shape=[jax.ShapeDtypeStruct(s.shape, s.dtype) for s in slabs],
        scratch_shapes=[pltpu.SemaphoreType.DMA((n,)), pltpu.SemaphoreType.DMA((n,))])(*slabs)


def _sum_chips(r, name):
    rows, cols = r.shape[1:]
    tb = _div(rows, 256, 16)

    def body(r0, r1, r2, r3, o_ref):
        o_ref[...] = ((r0[0].astype(F32) + r1[0].astype(F32)) + r2[0].astype(F32)) + r3[0].astype(F32)

    spec = lambda j: pl.BlockSpec((1, tb, cols), lambda i: (j, i, 0))
    return pl.pallas_call(
        body, name=name, grid=(rows // tb,), in_specs=[spec(0), spec(1), spec(2), spec(3)],
        out_specs=pl.BlockSpec((tb, cols), lambda i: (i, 0)), out_shape=jax.ShapeDtypeStruct((rows, cols), F32),
        compiler_params=_cp("parallel"))(r, r, r, r)


def _adamw(w, m, v, p, q, name):
    rows, cols = w.shape
    tb = _div(rows, 256, 8)

    def body(w_ref, m_ref, v_ref, p_ref, q_ref, g_ref, d_ref, nm_ref, nv_ref):
        g = p_ref[...] + q_ref[...]
        m_new = B1 * m_ref[...] + (1.0 - B1) * g
        v_new = B2 * v_ref[...] + (1.0 - B2) * (g * g)
        g_ref[...] = g
        nm_ref[...] = m_new
        nv_ref[...] = v_new
        m_hat = m_new / (1.0 - B1 ** STEP)
        v_hat = v_new / (1.0 - B2 ** STEP)
        d_ref[...] = -LR * (m_hat / (jnp.sqrt(v_hat) + AEPS) + WD * w_ref[...])

    spec = pl.BlockSpec((tb, cols), lambda i: (i, 0))
    return pl.pallas_call(
        body, name=name, grid=(rows // tb,), in_specs=[spec] * 5, out_specs=[spec] * 4,
        out_shape=[jax.ShapeDtypeStruct((rows, cols), F32)] * 4, compiler_params=_cp("parallel"))(w, m, v, p, q)


def _split_w_in(w_in):
    o1 = 3 * FW
    o2 = o1 + HEADS
    o3 = o2 + 3 * FW
    o4 = o3 + FW
    o5 = o4 + HEADS
    o6 = o5 + HEADS
    pad = jnp.zeros((w_in.shape[0], SMALL_W - 3 * HEADS), w_in.dtype)
    rest = jnp.concatenate([w_in[:, o2:o3], w_in[:, o3:o4], w_in[:, o6:], w_in[:, o1:o2], w_in[:, o4:o5], w_in[:, o5:o6], pad],
                           axis=1)
    return w_in[:, :o1], rest


AXIS = {n: a for n, _, a in WEIGHTS}
SMALL = tuple(n for n, _, _ in WEIGHTS if n not in BIG)


def _by_chip(full, axis):
    rows, cols = full.shape
    if axis == 0:
        return full.reshape(N_CHIPS, rows // N_CHIPS, cols)
    return full.reshape(rows, N_CHIPS, cols // N_CHIPS).transpose(1, 0, 2)


def _from_chips(parts, axis):
    _, r, c = parts.shape
    if axis == 0:
        return parts.reshape(N_CHIPS * r, c)
    return parts.transpose(1, 0, 2).reshape(r, N_CHIPS * c)


def _gather_weights(shards):
    hi = {n: shards[n].astype(jnp.bfloat16) for n in SPLIT_F32}
    lo = [(shards[n] - hi[n].astype(F32)).astype(jnp.bfloat16) for n in SPLIT_F32]
    slab = _pack([hi[n] for n in SPLIT_F32] + lo, jnp.bfloat16)
    got = _gather_chips([shards[n].astype(jnp.bfloat16) for n in BIG] + [slab], "gather_weights")
    full = {n: _from_chips(g, AXIS[n]) for n, g in zip(BIG, got)}
    shapes = [shards[n].shape for n in SPLIT_F32] * 2
    per_chip = [_unpack(got[-1][j], shapes) for j in range(N_CHIPS)]
    for i, n in enumerate(SPLIT_F32):
        join = lambda off: jnp.concatenate([per_chip[j][off + i] for j in range(N_CHIPS)], axis=1).astype(F32)
        full[n] = join(0) + join(len(SPLIT_F32))
    return full


def kernel(x, meta_tokens, w_in, fgt_bias, gdn_conv_w, gdn_a_log, gdn_dt_bias, gdn_norm_w, gate_bias, w_branch_fox, w_branch_gdn, w_out, norm_mix_w, norm_ffn_w, ffn_w_up, ffn_conv_w, ffn_conv_b, ffn_w_down, norm_final_w, loss_target, m_meta_tokens, m_w_in, m_fgt_bias, m_gdn_conv_w, m_gdn_a_log, m_gdn_dt_bias, m_gdn_norm_w, m_gate_bias, m_w_branch_fox, m_w_branch_gdn, m_w_out, m_norm_mix_w, m_norm_ffn_w, m_ffn_w_up, m_ffn_conv_w, m_ffn_conv_b, m_ffn_w_down, m_norm_final_w, v_meta_tokens, v_w_in, v_fgt_bias, v_gdn_conv_w, v_gdn_a_log, v_gdn_dt_bias, v_gdn_norm_w, v_gate_bias, v_w_branch_fox, v_w_branch_gdn, v_w_out, v_norm_mix_w, v_norm_ffn_w, v_ffn_w_up, v_ffn_conv_w, v_ffn_conv_b, v_ffn_w_down, v_norm_final_w):
    weights = dict(meta_tokens=meta_tokens, w_in=w_in, fgt_bias=fgt_bias, gdn_conv_w=gdn_conv_w, gdn_a_log=gdn_a_log, gdn_dt_bias=gdn_dt_bias, gdn_norm_w=gdn_norm_w, gate_bias=gate_bias, w_branch_fox=w_branch_fox, w_branch_gdn=w_branch_gdn, w_out=w_out, norm_mix_w=norm_mix_w, norm_ffn_w=norm_ffn_w, ffn_w_up=ffn_w_up, ffn_conv_w=ffn_conv_w, ffn_conv_b=ffn_conv_b, ffn_w_down=ffn_w_down, norm_final_w=norm_final_w)
    m_in = dict(meta_tokens=m_meta_tokens, w_in=m_w_in, fgt_bias=m_fgt_bias, gdn_conv_w=m_gdn_conv_w, gdn_a_log=m_gdn_a_log, gdn_dt_bias=m_gdn_dt_bias, gdn_norm_w=m_gdn_norm_w, gate_bias=m_gate_bias, w_branch_fox=m_w_branch_fox, w_branch_gdn=m_w_branch_gdn, w_out=m_w_out, norm_mix_w=m_norm_mix_w, norm_ffn_w=m_norm_ffn_w, ffn_w_up=m_ffn_w_up, ffn_conv_w=m_ffn_conv_w, ffn_conv_b=m_ffn_conv_b, ffn_w_down=m_ffn_w_down, norm_final_w=m_norm_final_w)
    v_in = dict(meta_tokens=v_meta_tokens, w_in=v_w_in, fgt_bias=v_fgt_bias, gdn_conv_w=v_gdn_conv_w, gdn_a_log=v_gdn_a_log, gdn_dt_bias=v_gdn_dt_bias, gdn_norm_w=v_gdn_norm_w, gate_bias=v_gate_bias, w_branch_fox=v_w_branch_fox, w_branch_gdn=v_w_branch_gdn, w_out=v_w_out, norm_mix_w=v_norm_mix_w, norm_ffn_w=v_norm_ffn_w, ffn_w_up=v_ffn_w_up, ffn_conv_w=v_ffn_conv_w, ffn_conv_b=v_ffn_conv_b, ffn_w_down=v_ffn_w_down, norm_final_w=v_norm_final_w)
    shard2d = {n: _shard_shape(s, a) for n, s, a in WEIGHTS}
    as2d = lambda d: {n: d[n].reshape(shard2d[n]) for n, _, _ in WEIGHTS}
    w2, m2, v2 = as2d(weights), as2d(m_in), as2d(v_in)

    full = _gather_weights(w2)
    w_fox, w_rest = _split_w_in(full["w_in"])
    flat = lambda n: w2[n].reshape(-1)
    local_w = dict(
        meta=full["meta_tokens"], w_fox=w_fox, w_rest=w_rest, fgt_bias=flat("fgt_bias"), gdn_conv=full["gdn_conv_w"],
        a_log=flat("gdn_a_log"), dt_bias=flat("gdn_dt_bias"), gdn_norm=flat("gdn_norm_w"), gate_bias=flat("gate_bias"),
        w_bfox=full["w_branch_fox"], w_bgdn=full["w_branch_gdn"], w_out=full["w_out"], norm_mix=flat("norm_mix_w"),
        norm_ffn=flat("norm_ffn_w"), w_up=full["ffn_w_up"], ffn_conv=full["ffn_conv_w"], ffn_conv_b=flat("ffn_conv_b"),
        w_down=full["ffn_w_down"], norm_final=flat("norm_final_w"))

    loss, grad_x, grads = _local_step(x[0], loss_target[0], local_w)

    g2 = {n: grads[n].reshape(s) for n, s, _ in WEIGHTS}
    send = [_by_chip(g2[n].astype(jnp.bfloat16), AXIS[n]) for n in BIG]
    send.append(jnp.stack([_pack([_shard_of(g2[n], AXIS[n], j) for n in SMALL], F32) for j in range(N_CHIPS)]))
    parts = [_sum_chips(r, "sum_" + n) for r, n in zip(_scatter_chips(send, "scatter_grads"), BIG + ("small",))]
    others = _sibling_swap(parts, "swap_grads")
    slab = lambda d: _pack([d[n] for n in SMALL], F32)
    state = [(w2[n], m2[n], v2[n]) for n in BIG] + [(slab(w2), slab(m2), slab(v2))]
    outs = [_adamw(w, m, v, p, q, "adamw_" + n) for (w, m, v), p, q, n in zip(state, parts, others, BIG + ("small",))]
    small = [_unpack(o, [weights[n].shape for n in SMALL]) for o in outs[-1]]
    result = []
    for kind in range(4):
        by_name = {n: outs[i][kind].reshape(weights[n].shape) for i, n in enumerate(BIG)}
        by_name.update(zip(SMALL, small[kind]))
        result += [by_name[n] for n, _, _ in WEIGHTS]
    total = lax.psum(loss[0, 0], ("x", "y", "c"))
    return (total, grad_x[None], *result)
```

```python
import functools

import numpy as np
import jax
import jax.numpy as jnp
from jax import lax
from jax.experimental import pallas as pl
from jax.experimental.pallas import tpu as pltpu

F32 = jnp.float32
_BF = jnp.bfloat16
HI = lax.Precision.HIGHEST

D = 1024
N_META = 16
CH = 64
FRONT = CH - N_META
X0 = CH
HEADS = 8
HD = 64
FW = HEADS * HD
DFF = 2816
EPS = 1e-6
NEG = -1e30
T_ALIGN = 256
SMALL_W = 128
REST_W = 3 * FW + FW + 2 * D + SMALL_W
OFF_Z = 3 * FW
OFF_GATES = 4 * FW
OFF_SMALL = 4 * FW + 2 * D

LR, B1, B2, AEPS, WD, STEP = 0.001, 0.9, 0.999, 1e-08, 0.01, 10

VMEM_LIMIT = 56 * 1024 * 1024
ROW_TILE = 640
MM_TM, MM_TN, MM_TK = 1280, 512, 2816
ATT_TILE = 640


def _div(n, target, mult=128):
    if n <= target:
        return n
    best = None
    for d in range(mult, target + 1, mult):
        if n % d == 0:
            best = d
    assert best is not None, (n, target, mult)
    return best


def _cp(*sem):
    return pltpu.CompilerParams(dimension_semantics=sem, vmem_limit_bytes=VMEM_LIMIT)


def _sig(x):
    return 1.0 / (1.0 + jnp.exp(-x))


MM_VMEM_BUDGET = 40 * 1024 * 1024
MM_STEP_BYTES = 1 << 20


def _mm_tiles(m, n, k, sa, sb, so, has_resid):
    divs = lambda d, cap: [t for t in range(128, min(d, cap) + 1, 128) if d % t == 0] or [d]
    best = None
    for tm in divs(m, MM_TM * 2):
        for tn in divs(n, 4608):
            fixed = tm * tn * (4 + 2 * so + (8 if has_resid else 0))
            tks = [t for t in divs(k, MM_TK) if 2 * t * (tm * sa + tn * sb) + fixed <= MM_VMEM_BUDGET]
            if not tks:
                continue
            tk = tks[-1]
            steps = (m // tm) * (n // tn) * (k // tk)
            rmw = (k // tk - 1) * m * n * 4
            cost = (n // tn) * m * k * sa + (m // tm) * k * n * sb + steps * MM_STEP_BYTES + rmw
            if best is None or cost < best[0]:
                best = (cost, tm, tn, tk)
    assert best is not None, (m, n, k)
    return best[1:]


def _mm_t(a, b, name):
    return _mm(a, b, ta=True, name=name)


def _mm(a, b, *, ta=False, tb=False, out_dtype=F32, resid=None, name):
    K, M = a.shape if ta else a.shape[::-1]
    N = b.shape[0] if tb else b.shape[1]
    assert (b.shape[1] if tb else b.shape[0]) == K
    tm, tn, tk = _mm_tiles(M, N, K, a.dtype.itemsize, b.dtype.itemsize, jnp.dtype(out_dtype).itemsize, resid is not None)
    nk = K // tk
    dims = (((0 if ta else 1,), (1 if tb else 0,)), ((), ()))
    mxu = _BF

    def body(*refs):
        if resid is None:
            a_ref, b_ref, o_ref, acc = refs
            r_ref = None
        else:
            a_ref, b_ref, r_ref, o_ref, acc = refs
        k = pl.program_id(2)
        part = lax.dot_general(a_ref[...].astype(mxu), b_ref[...].astype(mxu), dims, preferred_element_type=F32)

        @pl.when(k == 0)
        def _():
            acc[...] = part

        @pl.when(k > 0)
        def _():
            acc[...] += part

        @pl.when(k == nk - 1)
        def _():
            r = acc[...]
            if r_ref is not None:
                r = r + r_ref[...]
            o_ref[...] = r.astype(o_ref.dtype)

    a_spec = pl.BlockSpec((tk, tm), lambda i, j, k: (k, i)) if ta else pl.BlockSpec((tm, tk), lambda i, j, k: (i, k))
    b_spec = pl.BlockSpec((tn, tk), lambda i, j, k: (j, k)) if tb else pl.BlockSpec((tk, tn), lambda i, j, k: (k, j))
    o_spec = pl.BlockSpec((tm, tn), lambda i, j, k: (i, j))
    in_specs = [a_spec, b_spec] + ([o_spec] if resid is not None else [])
    args = (a, b) + ((resid,) if resid is not None else ())
    return pl.pallas_call(
        body, name=name, grid=(M // tm, N // tn, nk), in_specs=in_specs, out_specs=o_spec,
        out_shape=jax.ShapeDtypeStruct((M, N), out_dtype), scratch_shapes=[pltpu.VMEM((tm, tn), F32)],
        compiler_params=_cp("parallel", "parallel", "arbitrary"))(*args)


def _rms_fwd(h, w, name):
    T = h.shape[0]
    tb = _div(T, ROW_TILE, 8)

    def body(h_ref, w_ref, o_ref):
        x = h_ref[...]
        r = lax.rsqrt(jnp.mean(x * x, axis=-1, keepdims=True) + EPS)
        o_ref[...] = (x * r * w_ref[...]).astype(o_ref.dtype)

    return pl.pallas_call(
        body, name=name, grid=(T // tb,),
        in_specs=[pl.BlockSpec((tb, D), lambda i: (i, 0)), pl.BlockSpec((1, D), lambda i: (0, 0))],
        out_specs=pl.BlockSpec((tb, D), lambda i: (i, 0)), out_shape=jax.ShapeDtypeStruct((T, D), _BF),
        compiler_params=_cp("parallel"))(h, w)


def _rms_bwd(h, w, dy, dres, name):
    T = h.shape[0]
    tb = _div(T, ROW_TILE, 8)

    def body(h_ref, w_ref, dy_ref, dr_ref, dh_ref, dw_ref):
        x = h_ref[...]
        r = lax.rsqrt(jnp.mean(x * x, axis=-1, keepdims=True) + EPS)
        xh = x * r
        dy = dy_ref[...]
        g = dy * w_ref[...]
        dh_ref[...] = dr_ref[...] + r * (g - xh * jnp.mean(xh * g, axis=-1, keepdims=True))
        part = jnp.sum(dy * xh, axis=0, keepdims=True)

        @pl.when(pl.program_id(0) == 0)
        def _():
            dw_ref[...] = part

        @pl.when(pl.program_id(0) > 0)
        def _():
            dw_ref[...] += part

    row = pl.BlockSpec((tb, D), lambda i: (i, 0))
    one = pl.BlockSpec((1, D), lambda i: (0, 0))
    return pl.pallas_call(
        body, name=name, grid=(T // tb,), in_specs=[row, one, row, row], out_specs=[row, one],
        out_shape=[jax.ShapeDtypeStruct((T, D), F32), jax.ShapeDtypeStruct((1, D), F32)],
        compiler_params=_cp("arbitrary"))(h, w, dy, dres)


def _fgate_fwd(small, bias):
    T = small.shape[0]
    tb = _div(T, ROW_TILE, 8)

    def body(s_ref, b_ref, c_ref, carry):
        @pl.when(pl.program_id(0) == 0)
        def _():
            carry[...] = jnp.zeros_like(carry)

        lf = jax.nn.log_sigmoid(s_ref[...] + b_ref[...])
        r = lax.broadcasted_iota(jnp.int32, (tb, tb), 0)
        c = lax.broadcasted_iota(jnp.int32, (tb, tb), 1)
        tri = (r >= c).astype(F32)
        cs = jnp.dot(tri, lf, precision=HI, preferred_element_type=F32) + carry[...]
        c_ref[...] = cs
        carry[...] = cs[tb - 1:tb, :]

    return pl.pallas_call(
        body, name="fgate_fwd", grid=(T // tb,),
        in_specs=[pl.BlockSpec((tb, SMALL_W), lambda i: (i, 0)), pl.BlockSpec((1, SMALL_W), lambda i: (0, 0))],
        out_specs=pl.BlockSpec((tb, SMALL_W), lambda i: (i, 0)), out_shape=jax.ShapeDtypeStruct((T, SMALL_W), F32),
        scratch_shapes=[pltpu.VMEM((1, SMALL_W), F32)], compiler_params=_cp("arbitrary"))(small, bias)


def _fgate_bwd(small, bias, dc):
    T = small.shape[0]
    tb = _div(T, ROW_TILE, 8)
    nb = T // tb

    def body(s_ref, b_ref, dc_ref, df_ref, db_ref, carry):
        @pl.when(pl.program_id(0) == 0)
        def _():
            carry[...] = jnp.zeros_like(carry)

        r = lax.broadcasted_iota(jnp.int32, (tb, tb), 0)
        c = lax.broadcasted_iota(jnp.int32, (tb, tb), 1)
        tri = (r <= c).astype(F32)
        dlf = jnp.dot(tri, dc_ref[...], precision=HI, preferred_element_type=F32) + carry[...]
        carry[...] = dlf[0:1, :]
        df = dlf * _sig(-(s_ref[...] + b_ref[...]))
        df_ref[...] = df
        part = jnp.sum(df, axis=0, keepdims=True)

        @pl.when(pl.program_id(0) == 0)
        def _():
            db_ref[...] = part

        @pl.when(pl.program_id(0) > 0)
        def _():
            db_ref[...] += part

    rev = pl.BlockSpec((tb, SMALL_W), lambda i: (nb - 1 - i, 0))
    one = pl.BlockSpec((1, SMALL_W), lambda i: (0, 0))
    return pl.pallas_call(
        body, name="fgate_bwd", grid=(nb,), in_specs=[rev, one, rev], out_specs=[rev, one],
        out_shape=[jax.ShapeDtypeStruct((T, SMALL_W), F32), jax.ShapeDtypeStruct((1, SMALL_W), F32)],
        scratch_shapes=[pltpu.VMEM((1, SMALL_W), F32)], compiler_params=_cp("arbitrary"))(small, bias, dc)


def _att_rows(a8, tb):
    T = a8.shape[0]
    return a8.T.reshape(HEADS // 2, 2, T // tb, tb).transpose(0, 2, 1, 3)


def _att_cols(a8):
    return jnp.repeat(a8, HD, axis=1)


def _pair_cols(tb):
    return pl.BlockSpec((1, tb, 2), lambda p, i, *_: (p, i, 0))


def _from_pairs(a):
    return a.transpose(1, 0, 2).reshape(a.shape[1], HEADS)


EXP_ZERO = -104.0
SKIP_SLACK = 2.0
NORM_SLACK = 1.02


def _att_stats(qkv, c_cols, ones_blk):
    T = qkv.shape[0]
    tb = _div(T, ATT_TILE)

    def body(q_ref, k_ref, c_ref, e_ref, o_ref):
        q = q_ref[...].astype(F32)
        k = k_ref[...].astype(F32)
        e = e_ref[...].astype(_BF)
        qn = jnp.max(jnp.dot((q * q).astype(_BF), e, preferred_element_type=F32), axis=0, keepdims=True)
        kn = jnp.max(jnp.dot((k * k).astype(_BF), e, preferred_element_type=F32), axis=0, keepdims=True)
        c = c_ref[...]
        o_ref[0] = jnp.concatenate([jnp.sqrt(qn), jnp.sqrt(kn), jnp.max(c, axis=0, keepdims=True),
                                    jnp.min(c, axis=0, keepdims=True), jnp.zeros((4, FW), F32)], axis=0)

    blk = lambda off: pl.BlockSpec((tb, FW), lambda i: (i, off))
    return pl.pallas_call(
        body, name="att_stats", grid=(T // tb,),
        in_specs=[blk(0), blk(1), blk(0), pl.BlockSpec((FW, FW), lambda i: (0, 0))],
        out_specs=pl.BlockSpec((1, 8, FW), lambda i: (i, 0, 0)), out_shape=jax.ShapeDtypeStruct((T // tb, 8, FW), F32),
        compiler_params=_cp("parallel"))(qkv, qkv, c_cols, ones_blk)


def _att_plan(stats):
    nb = stats.shape[0]
    st = stats[:, :4, ::HD]
    qmax, kmax, cmax, cmin = (st[:, r, :].T for r in range(4))
    bound = (HD ** -0.5) * NORM_SLACK * qmax[:, :, None] * (kmax[:, None, :] + kmax[:, :, None]) \
        + cmax[:, :, None] - cmin[:, None, :] + SKIP_SLACK
    ii = lax.broadcasted_iota(jnp.int32, (nb, nb), 0)
    jj = lax.broadcasted_iota(jnp.int32, (nb, nb), 1)
    skip = (bound < EXP_ZERO) & (jj < ii)[None]
    live = (~skip & (jj <= ii)[None]).reshape(HEADS // 2, 2, nb, nb).any(axis=1)
    jfirst = jnp.argmax(live, axis=2).astype(jnp.int32)
    ilast = (nb - 1 - jnp.argmax(live[:, ::-1, :], axis=1)).astype(jnp.int32)
    return skip.astype(jnp.int32).reshape(-1), jfirst.reshape(-1), ilast.reshape(-1)


def _fox_fwd(qkv, c_cols, c_rows, plan):
    T = qkv.shape[0]
    tb = _div(T, ATT_TILE)
    nb = T // tb
    npair = HEADS // 2
    scale = HD ** -0.5
    dn = (((1,), (1,)), ((), ()))

    def body(skip_ref, jfirst_ref, ilast_ref, q_ref, k_ref, v_ref, cq_ref, ck_ref, o_ref, l_ref, l2_ref, m_scr, acc_scr):
        pr = pl.program_id(0)
        i = pl.program_id(1)
        q = q_ref[...]
        lane = lax.broadcasted_iota(jnp.int32, (1, 2 * HD), 1)
        sel0 = lane < HD
        zero = jnp.zeros_like(q)
        qh = (jnp.where(sel0, q, zero) * scale, jnp.where(sel0, zero, q) * scale)
        cq = (cq_ref[:, 0:1], cq_ref[:, HD:HD + 1])
        row = i * tb + lax.broadcasted_iota(jnp.int32, (tb, 1), 0)
        m_scr[...] = jnp.full(m_scr.shape, NEG, F32)
        acc_scr[...] = jnp.zeros(acc_scr.shape, F32)

        def tile(j, h, masked):
            start = pl.multiple_of(j * tb, tb)
            kb = k_ref[pl.ds(start, tb), :]
            vb = v_ref[pl.ds(start, tb), :]
            one = jnp.ones_like(vb)
            vh = jnp.where(sel0, vb, one) if h == 0 else jnp.where(sel0, one, vb)
            t = lax.dot_general(qh[h], kb, dn, preferred_element_type=F32) - ck_ref[0, j, h:h + 1, :]
            if masked:
                col = j * tb + lax.broadcasted_iota(jnp.int32, (1, tb), 1)
                mask = (col <= row) & (col >= FRONT)
                t = jnp.where(mask, t, NEG)
            m = m_scr[h]
            m_new = jnp.maximum(m, jnp.max(t, axis=-1, keepdims=True) + cq[h])
            p = jnp.exp(t + (cq[h] - m_new))
            if masked:
                p = jnp.where(mask, p, 0.0)
            m_scr[h] = m_new
            acc_scr[h] = acc_scr[h] * jnp.exp(m - m_new) + jnp.dot(p.astype(vb.dtype), vh, preferred_element_type=F32)

        def step(j, _):
            edge = (j == 0) | (j == i)
            for h in range(2):
                live = skip_ref[((2 * pr + h) * nb + i) * nb + j] == 0
                pl.when(live & edge)(functools.partial(tile, j, h, True))
                pl.when(live & jnp.logical_not(edge))(functools.partial(tile, j, h, False))
            return 0

        lax.fori_loop(jfirst_ref[pr * nb + i], i + 1, step, 0)
        outs, lses = [], []
        for h in range(2):
            l = acc_scr[h][:, (1 - h) * HD:(1 - h) * HD + 1]
            ok = l > 0.0
            ls = jnp.where(ok, l, 1.0)
            outs.append(jnp.where(ok, acc_scr[h] / ls, 0.0))
            lses.append(jnp.where(ok, m_scr[h] + jnp.log(ls), 0.0))
        o_ref[...] = jnp.where(sel0, outs[0], outs[1])
        l_ref[...] = jnp.where(sel0, lses[0], lses[1])
        l2_ref[0, :, 0:1] = lses[0]
        l2_ref[0, :, 1:2] = lses[1]

    blk = lambda off: pl.BlockSpec((tb, 2 * HD), lambda p, i, *_: (i, off + p))
    full = lambda off: pl.BlockSpec((T, 2 * HD), lambda p, i, *_: (0, off + p))
    rows = pl.BlockSpec((1, nb, 2, tb), lambda p, i, *_: (p, 0, 0, 0))
    return pl.pallas_call(
        body, name="fox_fwd",
        grid_spec=pltpu.PrefetchScalarGridSpec(
            num_scalar_prefetch=3, grid=(npair, nb),
            in_specs=[blk(0), full(npair), full(2 * npair), blk(0), rows], out_specs=[blk(0), blk(0), _pair_cols(tb)],
            scratch_shapes=[pltpu.VMEM((2, tb, 1), F32), pltpu.VMEM((2, tb, 2 * HD), F32)]),
        out_shape=[jax.ShapeDtypeStruct((T, FW), F32), jax.ShapeDtypeStruct((T, FW), F32),
                   jax.ShapeDtypeStruct((npair, T, 2), F32)],
        compiler_params=_cp("parallel", "arbitrary"))(*plan, qkv, qkv, qkv, c_cols, c_rows)


def _fox_bwd_dq(qkv, do, c_cols, c_rows, lse, delta, plan):
    T = qkv.shape[0]
    tb = _div(T, ATT_TILE)
    nb = T // tb
    npair = HEADS // 2
    scale = HD ** -0.5
    dn = (((1,), (1,)), ((), ()))

    def body(skip_ref, jfirst_ref, ilast_ref, q_ref, k_ref, v_ref, do_ref, cq_ref, ck_ref, l_ref, dl_ref, dq_ref, dc_ref,
             dq_scr):
        pr = pl.program_id(0)
        i = pl.program_id(1)
        q = q_ref[...]
        do = do_ref[...]
        lane = lax.broadcasted_iota(jnp.int32, (1, 2 * HD), 1)
        sel0 = lane < HD
        qh = (jnp.where(sel0, q, jnp.zeros_like(q)) * scale, jnp.where(sel0, jnp.zeros_like(q), q) * scale)
        doh = (jnp.where(sel0, do, jnp.zeros_like(do)), jnp.where(sel0, jnp.zeros_like(do), do))
        ce = (cq_ref[:, 0:1] - l_ref[:, 0:1], cq_ref[:, HD:HD + 1] - l_ref[:, HD:HD + 1])
        dl = (dl_ref[:, 0:1], dl_ref[:, HD:HD + 1])
        row = i * tb + lax.broadcasted_iota(jnp.int32, (tb, 1), 0)
        dq_scr[...] = jnp.zeros(dq_scr.shape, F32)

        def tile(j, h, masked):
            start = pl.multiple_of(j * tb, tb)
            kb = k_ref[pl.ds(start, tb), :]
            vb = v_ref[pl.ds(start, tb), :]
            one = jnp.ones_like(kb)
            kh = jnp.where(sel0, kb, one) if h == 0 else jnp.where(sel0, one, kb)
            t = lax.dot_general(qh[h], kb, dn, preferred_element_type=F32) - ck_ref[0, j, h:h + 1, :]
            if masked:
                col = j * tb + lax.broadcasted_iota(jnp.int32, (1, tb), 1)
                mask = (col <= row) & (col >= FRONT)
                p = jnp.where(mask, jnp.exp(jnp.where(mask, t, NEG) + ce[h]), 0.0)
            else:
                p = jnp.exp(t + ce[h])
            dp = lax.dot_general(doh[h], vb, dn, preferred_element_type=F32)
            ds = p * (dp - dl[h])
            dq_scr[h] += jnp.dot(ds.astype(kb.dtype), kh, preferred_element_type=F32)

        def step(j, _):
            edge = (j == 0) | (j == i)
            for h in range(2):
                live = skip_ref[((2 * pr + h) * nb + i) * nb + j] == 0
                pl.when(live & edge)(functools.partial(tile, j, h, True))
                pl.when(live & jnp.logical_not(edge))(functools.partial(tile, j, h, False))
            return 0

        lax.fori_loop(jfirst_ref[pr * nb + i], i + 1, step, 0)
        dq_ref[...] = (jnp.where(sel0, dq_scr[0], dq_scr[1]) * scale).astype(dq_ref.dtype)
        dc_ref[0, :, 0:1] = dq_scr[0][:, HD:HD + 1]
        dc_ref[0, :, 1:2] = dq_scr[1][:, 0:1]

    blk = lambda off: pl.BlockSpec((tb, 2 * HD), lambda p, i, *_: (i, off + p))
    full = lambda off: pl.BlockSpec((T, 2 * HD), lambda p, i, *_: (0, off + p))
    rows = pl.BlockSpec((1, nb, 2, tb), lambda p, i, *_: (p, 0, 0, 0))
    return pl.pallas_call(
        body, name="fox_bwd_dq",
        grid_spec=pltpu.PrefetchScalarGridSpec(
            num_scalar_prefetch=3, grid=(npair, nb),
            in_specs=[blk(0), full(npair), full(2 * npair), blk(0), blk(0), rows, blk(0), blk(0)],
            out_specs=[blk(0), _pair_cols(tb)],
            scratch_shapes=[pltpu.VMEM((2, tb, 2 * HD), F32)]),
        out_shape=[jax.ShapeDtypeStruct((T, FW), _BF), jax.ShapeDtypeStruct((npair, T, 2), F32)],
        compiler_params=_cp("parallel", "arbitrary"))(*plan, qkv, qkv, qkv, do, c_cols, c_rows, lse, delta)


def _fox_bwd_dkv(qkv, do, c_cols, ce_rows, delta_rows, plan):
    T = qkv.shape[0]
    tb = _div(T, ATT_TILE)
    nb = T // tb
    npair = HEADS // 2
    scale = HD ** -0.5
    dn = (((1,), (1,)), ((), ()))

    def body(skip_ref, jfirst_ref, ilast_ref, q_ref, k_ref, v_ref, do_ref, ck_ref, ce_ref, dl_ref,
             dk_ref, dv_ref, dc_ref, dk_scr, dv_scr):
        pr = pl.program_id(0)
        jb = pl.program_id(1)
        k = k_ref[...]
        v = v_ref[...]
        lane = lax.broadcasted_iota(jnp.int32, (1, 2 * HD), 1)
        sel0 = lane < HD
        kh = (jnp.where(sel0, k, jnp.zeros_like(k)) * scale, jnp.where(sel0, jnp.zeros_like(k), k) * scale)
        vh = (jnp.where(sel0, v, jnp.zeros_like(v)), jnp.where(sel0, jnp.zeros_like(v), v))
        ck = (ck_ref[:, 0:1], ck_ref[:, HD:HD + 1])
        kidx = jb * tb + lax.broadcasted_iota(jnp.int32, (tb, 1), 0)
        dk_scr[...] = jnp.zeros(dk_scr.shape, F32)
        dv_scr[...] = jnp.zeros(dv_scr.shape, F32)

        def tile(i, h, masked):
            start = pl.multiple_of(i * tb, tb)
            qb = q_ref[pl.ds(start, tb), :]
            dob = do_ref[pl.ds(start, tb), :]
            one = jnp.ones_like(qb)
            qh = jnp.where(sel0, qb, one) if h == 0 else jnp.where(sel0, one, qb)
            t = lax.dot_general(kh[h], qb, dn, preferred_element_type=F32) - ck[h]
            ce = ce_ref[0, i, h:h + 1, :]
            if masked:
                qidx = i * tb + lax.broadcasted_iota(jnp.int32, (1, tb), 1)
                mask = (kidx <= qidx) & (kidx >= FRONT)
                pt = jnp.where(mask, jnp.exp(jnp.where(mask, t, NEG) + ce), 0.0)
            else:
                pt = jnp.exp(t + ce)
            dv_scr[h] += jnp.dot(pt.astype(dob.dtype), dob, preferred_element_type=F32)
            dpt = lax.dot_general(vh[h], dob, dn, preferred_element_type=F32)
            dst = pt * (dpt - dl_ref[0, i, h:h + 1, :])
            dk_scr[h] += jnp.dot(dst.astype(qb.dtype), qh, preferred_element_type=F32)

        def step(i, _):
            edge = (i == jb) | (jb == 0)
            for h in range(2):
                live = skip_ref[((2 * pr + h) * nb + i) * nb + jb] == 0
                pl.when(live & edge)(functools.partial(tile, i, h, True))
                pl.when(live & jnp.logical_not(edge))(functools.partial(tile, i, h, False))
            return 0

        lax.fori_loop(jb, ilast_ref[pr * nb + jb] + 1, step, 0)
        dk_ref[...] = (jnp.where(sel0, dk_scr[0], dk_scr[1]) * scale).astype(dk_ref.dtype)
        dv_ref[...] = jnp.where(sel0, dv_scr[0], dv_scr[1]).astype(dv_ref.dtype)
        dc_ref[0, :, 0:1] = -dk_scr[0][:, HD:HD + 1]
        dc_ref[0, :, 1:2] = -dk_scr[1][:, 0:1]

    blk = lambda off: pl.BlockSpec((tb, 2 * HD), lambda p, j, *_: (j, off + p))
    full = lambda off: pl.BlockSpec((T, 2 * HD), lambda p, j, *_: (0, off + p))
    rows = pl.BlockSpec((1, nb, 2, tb), lambda p, j, *_: (p, 0, 0, 0))
    return pl.pallas_call(
        body, name="fox_bwd_dkv",
        grid_spec=pltpu.PrefetchScalarGridSpec(
            num_scalar_prefetch=3, grid=(npair, nb),
            in_specs=[full(0), blk(npair), blk(2 * npair), full(0), blk(0), rows, rows],
            out_specs=[blk(0), blk(0), _pair_cols(tb)],
            scratch_shapes=[pltpu.VMEM((2, tb, 2 * HD), F32), pltpu.VMEM((2, tb, 2 * HD), F32)]),
        out_shape=[jax.ShapeDtypeStruct((T, FW), _BF)] * 2 + [jax.ShapeDtypeStruct((npair, T, 2), F32)],
        compiler_params=_cp("parallel", "arbitrary"))(*plan, qkv, qkv, qkv, do, c_cols, ce_rows, delta_rows)


def _head_dot(a, b, ones_blk, ones_col):
    T = a.shape[0]
    tb = _div(T, ROW_TILE, 8)

    def body(a_ref, b_ref, e_ref, c_ref, o_ref, o8_ref):
        prod = a_ref[...].astype(F32) * b_ref[...].astype(F32)
        o_ref[...] = jnp.dot(prod, e_ref[...], precision=HI, preferred_element_type=F32)
        o8_ref[...] = jnp.dot(prod, c_ref[...], precision=HI, preferred_element_type=F32)

    row = pl.BlockSpec((tb, FW), lambda i: (i, 0))
    return pl.pallas_call(
        body, name="head_dot", grid=(T // tb,),
        in_specs=[row, row, pl.BlockSpec((FW, FW), lambda i: (0, 0)), pl.BlockSpec((FW, SMALL_W), lambda i: (0, 0))],
        out_specs=[row, pl.BlockSpec((tb, SMALL_W), lambda i: (i, 0))],
        out_shape=[jax.ShapeDtypeStruct((T, FW), F32), jax.ShapeDtypeStruct((T, SMALL_W), F32)],
        compiler_params=_cp("parallel"))(a, b, ones_blk, ones_col)


CONV_RC = 32
CONV_TC = 256


def _stage_prev(scr, x_ref, h_ref, first, tb):
    scr[0:8, :] = jnp.where(first, 0.0, h_ref[...])
    scr[8:8 + tb, :] = x_ref[...]


def _windows_prev(scr, r, kw):
    x = scr[pl.ds(pl.multiple_of(r * CONV_RC, CONV_RC), CONV_RC + 8), :]
    return [x[8:] if k == kw - 1 else pltpu.roll(x, kw - 1 - k, 0)[8:] for k in range(kw)]


def _fold8(a):
    return a.reshape(CONV_RC // 8, 8, a.shape[-1]).sum(axis=0)


def _halo_prev(tb, tc, off=0):
    return pl.BlockSpec((8, tc), lambda j, i: (jnp.maximum(i * (tb // 8) - 1, 0), j + off))


def _gconv_fwd(rest, w):
    T = rest.shape[0]
    C = 3 * FW
    kw = w.shape[0]
    tb, tc = _div(T, 1280, CONV_RC), CONV_TC

    def body(x_ref, h_ref, w_ref, o_ref, scr):
        _stage_prev(scr, x_ref, h_ref, pl.program_id(1) == 0, tb)
        wv = w_ref[...]

        def chunk(r, _):
            win = _windows_prev(scr, r, kw)
            u = sum(wv[k:k + 1, :] * win[k] for k in range(kw))
            o_ref[pl.ds(pl.multiple_of(r * CONV_RC, CONV_RC), CONV_RC), :] = u * _sig(u)
            return 0

        lax.fori_loop(0, tb // CONV_RC, chunk, 0)

    return pl.pallas_call(
        body, name="gconv_fwd", grid=(C // tc, T // tb),
        in_specs=[pl.BlockSpec((tb, tc), lambda j, i: (i, j)), _halo_prev(tb, tc), pl.BlockSpec((kw, tc), lambda j, i: (0, j))],
        out_specs=pl.BlockSpec((tb, tc), lambda j, i: (i, j)), out_shape=jax.ShapeDtypeStruct((T, C), F32),
        scratch_shapes=[pltpu.VMEM((tb + 8, tc), F32)], compiler_params=_cp("parallel", "arbitrary"))(rest, rest, w)


def _gconv_bwd_du(rest, w, dy):
    T = rest.shape[0]
    C = 3 * FW
    kw = w.shape[0]
    tb, tc = _div(T, 1280, CONV_RC), CONV_TC

    def body(x_ref, h_ref, w_ref, dy_ref, du_ref, dw_ref, scr):
        i = pl.program_id(1)
        _stage_prev(scr, x_ref, h_ref, i == 0, tb)
        wv = w_ref[...]

        def chunk(r, acc):
            rows = pl.ds(pl.multiple_of(r * CONV_RC, CONV_RC), CONV_RC)
            win = _windows_prev(scr, r, kw)
            u = sum(wv[k:k + 1, :] * win[k] for k in range(kw))
            sg = _sig(u)
            du = dy_ref[rows, :] * sg * (1.0 + u * (1.0 - sg))
            du_ref[rows, :] = du
            return tuple(acc[k] + _fold8(du * win[k]) for k in range(kw))

        acc = lax.fori_loop(0, tb // CONV_RC, chunk, tuple(jnp.zeros((8, tc), F32) for _ in range(kw)))
        part = jnp.concatenate([jnp.sum(a, axis=0, keepdims=True) for a in acc], axis=0)

        @pl.when(i == 0)
        def _():
            dw_ref[...] = part

        @pl.when(i > 0)
        def _():
            dw_ref[...] += part

    blk = pl.BlockSpec((tb, tc), lambda j, i: (i, j))
    wsp = pl.BlockSpec((kw, tc), lambda j, i: (0, j))
    return pl.pallas_call(
        body, name="gconv_bwd_du", grid=(C // tc, T // tb), in_specs=[blk, _halo_prev(tb, tc), wsp, blk],
        out_specs=[blk, wsp], out_shape=[jax.ShapeDtypeStruct((T, C), F32), jax.ShapeDtypeStruct((kw, C), F32)],
        scratch_shapes=[pltpu.VMEM((tb + 8, tc), F32)], compiler_params=_cp("parallel", "arbitrary"))(rest, rest, w, dy)


def _conv_bwd_dx(du, w, out_dtype, name):
    T, C = du.shape
    kw = w.shape[0]
    tb = _div(T, 1280, CONV_RC)
    tc = CONV_TC
    nb = T // tb

    def body(x_ref, h_ref, w_ref, o_ref, scr):
        scr[0:tb, :] = x_ref[...]
        scr[tb:tb + 8, :] = jnp.where(pl.program_id(1) == nb - 1, 0.0, h_ref[...])
        wv = w_ref[...]

        def chunk(r, _):
            start = pl.multiple_of(r * CONV_RC, CONV_RC)
            x = scr[pl.ds(start, CONV_RC + 8), :]
            acc = wv[kw - 1:kw, :] * x[:CONV_RC]
            for k in range(kw - 1):
                acc = acc + wv[k:k + 1, :] * pltpu.roll(x, CONV_RC + 8 - (kw - 1 - k), 0)[:CONV_RC]
            o_ref[pl.ds(start, CONV_RC), :] = acc.astype(o_ref.dtype)
            return 0

        lax.fori_loop(0, tb // CONV_RC, chunk, 0)

    halo = pl.BlockSpec((8, tc), lambda j, i: (jnp.minimum((i + 1) * (tb // 8), T // 8 - 1), j))
    return pl.pallas_call(
        body, name=name, grid=(C // tc, nb),
        in_specs=[pl.BlockSpec((tb, tc), lambda j, i: (i, j)), halo, pl.BlockSpec((kw, tc), lambda j, i: (0, j))],
        out_specs=pl.BlockSpec((tb, tc), lambda j, i: (i, j)), out_shape=jax.ShapeDtypeStruct((T, C), out_dtype),
        scratch_shapes=[pltpu.VMEM((tb + 8, tc), F32)], compiler_params=_cp("parallel", "arbitrary"))(du, du, w)


def _glu_fwd(up, w, b):
    T = up.shape[0]
    kw = w.shape[0]
    tb, tc = _div(T, 1280, CONV_RC), CONV_TC
    nc = DFF // tc

    def body(xg, hg, xu, hu, wg, wu, bg, bu, o_ref, sg, su):
        first = pl.program_id(1) == 0
        _stage_prev(sg, xg, hg, first, tb)
        _stage_prev(su, xu, hu, first, tb)
        wgv, wuv, bgv, buv = wg[...], wu[...], bg[...], bu[...]

        def chunk(r, _):
            wing, winu = _windows_prev(sg, r, kw), _windows_prev(su, r, kw)
            ug = bgv + sum(wgv[k:k + 1, :] * wing[k] for k in range(kw))
            uu = buv + sum(wuv[k:k + 1, :] * winu[k] for k in range(kw))
            o_ref[pl.ds(pl.multiple_of(r * CONV_RC, CONV_RC), CONV_RC), :] = (ug * _sig(ug) * uu).astype(o_ref.dtype)
            return 0

        lax.fori_loop(0, tb // CONV_RC, chunk, 0)

    blk = lambda off: pl.BlockSpec((tb, tc), lambda j, i: (i, j + off))
    wsp = lambda off: pl.BlockSpec((kw, tc), lambda j, i: (0, j + off))
    bsp = lambda off: pl.BlockSpec((1, tc), lambda j, i: (0, j + off))
    return pl.pallas_call(
        body, name="glu_fwd", grid=(nc, T // tb),
        in_specs=[blk(0), _halo_prev(tb, tc), blk(nc), _halo_prev(tb, tc, nc), wsp(0), wsp(nc), bsp(0), bsp(nc)],
        out_specs=blk(0), out_shape=jax.ShapeDtypeStruct((T, DFF), _BF),
        scratch_shapes=[pltpu.VMEM((tb + 8, tc), F32)] * 2, compiler_params=_cp("parallel", "arbitrary"))(
            up, up, up, up, w, w, b, b)


def _glu_bwd_du(up, w, b, df):
    T = up.shape[0]
    kw = w.shape[0]
    tb, tc = _div(T, 1280, CONV_RC), CONV_TC
    nc = DFF // tc

    def body(xg, hg, xu, hu, wg, wu, bg, bu, df_ref, dug_ref, duu_ref, dwg_ref, dwu_ref, dbg_ref, dbu_ref, sg, su):
        i = pl.program_id(1)
        _stage_prev(sg, xg, hg, i == 0, tb)
        _stage_prev(su, xu, hu, i == 0, tb)
        wgv, wuv, bgv, buv = wg[...], wu[...], bg[...], bu[...]

        def chunk(r, acc):
            rows = pl.ds(pl.multiple_of(r * CONV_RC, CONV_RC), CONV_RC)
            wing, winu = _windows_prev(sg, r, kw), _windows_prev(su, r, kw)
            ug = bgv + sum(wgv[k:k + 1, :] * wing[k] for k in range(kw))
            uu = buv + sum(wuv[k:k + 1, :] * winu[k] for k in range(kw))
            s = _sig(ug)
            df = df_ref[rows, :]
            dug = df * uu * s * (1.0 + ug * (1.0 - s))
            duu = df * ug * s
            dug_ref[rows, :] = dug
            duu_ref[rows, :] = duu
            new = [dug * wing[k] for k in range(kw)] + [duu * winu[k] for k in range(kw)] + [dug, duu]
            return tuple(a + _fold8(v) for a, v in zip(acc, new))

        acc = lax.fori_loop(0, tb // CONV_RC, chunk, tuple(jnp.zeros((8, tc), F32) for _ in range(2 * kw + 2)))
        col = [jnp.sum(a, axis=0, keepdims=True) for a in acc]
        parts = (jnp.concatenate(col[:kw], axis=0), jnp.concatenate(col[kw:2 * kw], axis=0), col[2 * kw], col[2 * kw + 1])
        accs = (dwg_ref, dwu_ref, dbg_ref, dbu_ref)

        @pl.when(i == 0)
        def _():
            for r, p in zip(accs, parts):
                r[...] = p

        @pl.when(i > 0)
        def _():
            for r, p in zip(accs, parts):
                r[...] += p

    blk = lambda off: pl.BlockSpec((tb, tc), lambda j, i: (i, j + off))
    wsp = lambda off: pl.BlockSpec((kw, tc), lambda j, i: (0, j + off))
    bsp = lambda off: pl.BlockSpec((1, tc), lambda j, i: (0, j + off))
    return pl.pallas_call(
        body, name="glu_bwd_du", grid=(nc, T // tb),
        in_specs=[blk(0), _halo_prev(tb, tc), blk(nc), _halo_prev(tb, tc, nc), wsp(0), wsp(nc), bsp(0), bsp(nc), blk(0)],
        out_specs=[blk(0), blk(0), wsp(0), wsp(0), bsp(0), bsp(0)],
        out_shape=[jax.ShapeDtypeStruct((T, DFF), F32)] * 2 + [jax.ShapeDtypeStruct((kw, DFF), F32)] * 2
        + [jax.ShapeDtypeStruct((1, DFF), F32)] * 2,
        scratch_shapes=[pltpu.VMEM((tb + 8, tc), F32)] * 2, compiler_params=_cp("parallel", "arbitrary"))(
            up, up, up, up, w, w, b, b, df)


def _mix_fwd(rest, gate_bias, y_fox, y_gdn):
    T = rest.shape[0]
    tb, tc = _div(T, ROW_TILE, 8), 512
    nc = D // tc
    og = OFF_GATES // tc

    def body(gf, gg, bf, bg, yf, yg, o_ref):
        o_ref[...] = (_sig(gf[...] + bf[...]) * yf[...] + _sig(gg[...] + bg[...]) * yg[...]).astype(o_ref.dtype)

    blk = lambda off: pl.BlockSpec((tb, tc), lambda i, j: (i, j + off))
    bsp = lambda off: pl.BlockSpec((1, tc), lambda i, j: (0, j + off))
    return pl.pallas_call(
        body, name="mix_fwd", grid=(T // tb, nc),
        in_specs=[blk(og), blk(og + nc), bsp(0), bsp(nc), blk(0), blk(0)], out_specs=blk(0),
        out_shape=jax.ShapeDtypeStruct((T, D), _BF), compiler_params=_cp("parallel", "parallel"))(
            rest, rest, gate_bias, gate_bias, y_fox, y_gdn)


def _mix_bwd(rest, gate_bias, y_fox, y_gdn, dmix):
    T = rest.shape[0]
    tb, tc = _div(T, ROW_TILE, 8), 512
    nc = D // tc
    og = OFF_GATES // tc

    def body(gf, gg, bf, bg, yf, yg, dm, dyf, dyg, dgf, dgg, dbf, dbg):
        i = pl.program_id(1)
        d = dm[...]
        sf = _sig(gf[...] + bf[...])
        sg = _sig(gg[...] + bg[...])
        dyf[...] = (d * sf).astype(dyf.dtype)
        dyg[...] = (d * sg).astype(dyg.dtype)
        a = d * yf[...] * sf * (1.0 - sf)
        b = d * yg[...] * sg * (1.0 - sg)
        dgf[...] = a.astype(dgf.dtype)
        dgg[...] = b.astype(dgg.dtype)
        pa = jnp.sum(a, axis=0, keepdims=True)
        pb = jnp.sum(b, axis=0, keepdims=True)

        @pl.when(i == 0)
        def _():
            dbf[...] = pa
            dbg[...] = pb

        @pl.when(i > 0)
        def _():
            dbf[...] += pa
            dbg[...] += pb

    blk = lambda off: pl.BlockSpec((tb, tc), lambda j, i: (i, j + off))
    bsp = lambda off: pl.BlockSpec((1, tc), lambda j, i: (0, j + off))
    return pl.pallas_call(
        body, name="mix_bwd", grid=(nc, T // tb),
        in_specs=[blk(og), blk(og + nc), bsp(0), bsp(nc), blk(0), blk(0), blk(0)],
        out_specs=[blk(0), blk(0), blk(0), blk(0), bsp(0), bsp(0)],
        out_shape=[jax.ShapeDtypeStruct((T, D), _BF)] * 4 + [jax.ShapeDtypeStruct((1, D), F32)] * 2,
        compiler_params=_cp("parallel", "arbitrary"))(rest, rest, gate_bias, gate_bias, y_fox, y_gdn, dmix)


def _loss_head(h2, w, target, n_valid):
    T = h2.shape[0]
    tb = _div(T, ROW_TILE, 8)

    def body(h_ref, w_ref, t_ref, dh_ref, loss_ref, dw_ref):
        i = pl.program_id(0)
        x = h_ref[...]
        r = lax.rsqrt(jnp.mean(x * x, axis=-1, keepdims=True) + EPS)
        xh = x * r
        row = i * tb + lax.broadcasted_iota(jnp.int32, (tb, 1), 0)
        valid = (row >= X0) & (row < X0 + n_valid)
        e = jnp.where(valid, xh * w_ref[...] - t_ref[...], 0.0)
        dy = e * (1.0 / D)
        g = dy * w_ref[...]
        dh_ref[...] = r * (g - xh * jnp.mean(xh * g, axis=-1, keepdims=True))
        lpart = 0.5 * jnp.sum(jnp.sum(e * e, axis=-1, keepdims=True) * (1.0 / D), axis=0, keepdims=True)
        wpart = jnp.sum(dy * xh, axis=0, keepdims=True)

        @pl.when(i == 0)
        def _():
            loss_ref[...] = lpart
            dw_ref[...] = wpart

        @pl.when(i > 0)
        def _():
            loss_ref[...] += lpart
            dw_ref[...] += wpart

    row = pl.BlockSpec((tb, D), lambda i: (i, 0))
    one = pl.BlockSpec((1, D), lambda i: (0, 0))
    return pl.pallas_call(
        body, name="loss_head", grid=(T // tb,), in_specs=[row, one, row],
        out_specs=[row, pl.BlockSpec((1, 1), lambda i: (0, 0)), one],
        out_shape=[jax.ShapeDtypeStruct((T, D), F32), jax.ShapeDtypeStruct((1, 1), F32), jax.ShapeDtypeStruct((1, D), F32)],
        compiler_params=_cp("arbitrary"))(h2, w, target)


def _bd_lo(a, b, ca, cb):
    return lax.dot_general(a.astype(_BF), b.astype(_BF), (((ca,), (cb,)), ((0,), (0,))), preferred_element_type=F32)


def _split2(a):
    hi = a.astype(_BF)
    return hi, (a - hi.astype(F32)).astype(_BF)


def _bd_hi(a, b, ca, cb, exact_a=False):
    dn = (((ca,), (cb,)), ((0,), (0,)))
    dot = lambda x, y: lax.dot_general(x, y, dn, preferred_element_type=F32)
    bh, bl = _split2(b)
    if exact_a:
        ah = a.astype(_BF)
        return dot(ah, bh) + dot(ah, bl)
    ah, al = _split2(a)
    return dot(ah, bh) + (dot(ah, bl) + dot(al, bh))


def _row_to_col(x):
    eye = lax.broadcasted_iota(jnp.int32, (1, CH, CH), 1) == lax.broadcasted_iota(jnp.int32, (1, CH, CH), 2)
    return jnp.sum(jnp.where(eye, jnp.broadcast_to(x, (HEADS, CH, CH)), 0.0), axis=2, keepdims=True)


def _col_to_row(x):
    eye = lax.broadcasted_iota(jnp.int32, (1, CH, CH), 1) == lax.broadcasted_iota(jnp.int32, (1, CH, CH), 2)
    return jnp.sum(jnp.where(eye, jnp.broadcast_to(x, (HEADS, CH, CH)), 0.0), axis=1, keepdims=True)


def _gdn_chunk(q, k, v, bpre, apre, alog, dtb):
    H = HEADS
    r = lax.broadcasted_iota(jnp.int32, (1, CH, CH), 1)
    c = lax.broadcasted_iota(jnp.int32, (1, CH, CH), 2)
    tril, strict = r >= c, r > c
    lb = jnp.broadcast_to(tril.astype(F32), (H, CH, CH))
    rq = lax.rsqrt(jnp.sum(q * q, axis=-1, keepdims=True) + EPS)
    rk = lax.rsqrt(jnp.sum(k * k, axis=-1, keepdims=True) + EPS)
    qh = q * rq
    qn = qh * (HD ** -0.5)
    kn = k * rk
    beta = _sig(bpre)
    x = apre + dtb
    ea = jnp.exp(alog)
    g = -ea * jax.nn.softplus(x)
    gb = jnp.broadcast_to(g, (H, CH, HD))
    gc = _bd_hi(lb, gb, 2, 1, True)
    dm = gc - _col_to_row(gc[:, :, 0:1])
    decay = jnp.where(tril, jnp.exp(jnp.where(tril, dm, 0.0)), 0.0)
    eg = jnp.exp(gc)
    gl = gc[:, CH - 1:CH, :]
    egl = jnp.exp(gl - gc)
    cd = jnp.exp(gl)
    kb = kn * beta
    vb = v * beta
    both = _bd_lo(_rows2(kb, qn), kn, 2, 2)
    kk, qk = both[:, :CH], both[:, CH:]
    pw = -jnp.where(strict, kk * decay, 0.0)
    tm = jnp.where(r == c, 1.0, 0.0) + pw
    pw = _bd_hi(pw, pw, 2, 1)
    for it in range(4):
        mul = _bd_hi if it < 2 else _bd_lo
        both = mul(_rows2(pw, tm), pw, 2, 1)
        pw, tm = both[:, :CH], tm + both[:, CH:]
    tm = tm + _bd_lo(tm, pw, 2, 1)
    kbg = kb * eg
    sol = _bd_hi(tm, _lanes2(vb, kbg), 2, 1)
    value, kcd = sol[:, :, :HD], sol[:, :, HD:]
    attn = jnp.where(tril, qk * decay, 0.0)
    return dict(tril=tril, strict=strict, lb=lb, rq=rq, rk=rk, qh=qh, qn=qn, kn=kn, beta=beta, x=x, ea=ea, g=g,
                decay=decay, eg=eg, egl=egl, cd=cd, kb=kb, vb=vb, kk=kk, tm=tm, kbg=kbg, value=value, kcd=kcd, qk=qk,
                attn=attn, qd=qn * eg, kt=kn * egl, sol=sol)


def _rows2(a, b):
    return jnp.concatenate([a, b], axis=1)


def _lanes2(a, b):
    return jnp.concatenate([a, b], axis=2)


GDN_CPS = 2


def _gdn_specs(T, rev):
    G = GDN_CPS
    ns = T // (G * CH)
    pos = (lambda n: ns - 1 - n) if rev else (lambda n: n)
    mat = pl.BlockSpec((HEADS, G * CH, HD), lambda n: (0, pos(n), 0))
    col = pl.BlockSpec((G, HEADS, 1, CH), lambda n: (pos(n), 0, 0, 0))
    sca = pl.BlockSpec((HEADS, 1, 1), lambda n: (0, 0, 0))
    nw = pl.BlockSpec((1, 1, HD), lambda n: (0, 0, 0))
    st = pl.BlockSpec((G, HEADS, HD, HD), lambda n: (pos(n), 0, 0, 0))
    tok = lambda width, off: pl.BlockSpec((G * CH, width), lambda n: (pos(n), off))
    return ns, mat, col, sca, nw, st, tok


def _split_heads(x):
    return [jnp.stack([x[:, (g * HEADS + h) * HD:(g * HEADS + h + 1) * HD] for h in range(HEADS)])
            for g in range(x.shape[1] // FW)]


def _store_heads(ref, rows, g, val):
    for h in range(HEADS):
        ref[rows, (g * HEADS + h) * HD:(g * HEADS + h + 1) * HD] = val[h]


def _gdn_fwd(conv, rest, bpre, apre, alog, dtb, nw, gather=()):
    T = conv.shape[0]
    ns, mat, col, sca, nws, st, tok = _gdn_specs(T, False)
    rows = [slice(g * CH, (g + 1) * CH) for g in range(GDN_CPS)]
    ng = len(gather)

    def body(*refs):
        c_ref, z_ref, b_ref, a_ref, al_ref, dt_ref, nw_ref = refs[:7]
        o_ref, og_ref, st_ref = refs[7 + ng:10 + ng]
        s_scr = refs[10 + 2 * ng]
        comm = (refs[7:7 + ng], refs[10 + ng:10 + 2 * ng], refs[11 + 2 * ng:])

        @pl.when(pl.program_id(0) == 0)
        def _():
            s_scr[...] = jnp.zeros_like(s_scr)
            if ng:
                _gather_phase(*comm, True)

        locs = [_gdn_chunk(*_split_heads(c_ref[rows[g], :]), _row_to_col(b_ref[g]), _row_to_col(a_ref[g]), al_ref[...],
                           dt_ref[...]) for g in range(GDN_CPS)]
        s = s_scr[...]
        for g, L in enumerate(locs):
            st_ref[g] = s
            both = _bd_lo(_rows2(L["kcd"], L["qd"]), s, 2, 1)
            v_new = L["value"] - both[:, :CH]
            o = both[:, CH:] + _bd_lo(L["attn"], v_new, 2, 1)
            s = s * L["cd"] + _bd_lo(L["kt"], v_new, 1, 1)
            o_ref[:, rows[g], :] = o
            zz, = _split_heads(z_ref[rows[g], :])
            rs = lax.rsqrt(jnp.mean(o * o, axis=-1, keepdims=True) + EPS)
            _store_heads(og_ref, rows[g], 0, o * rs * nw_ref[...] * zz * _sig(zz))
        s_scr[...] = s
        if ng:
            pl.when(pl.program_id(0) == ns - 1)(functools.partial(_gather_phase, *comm, False))

    return pl.pallas_call(
        body, name="gdn_fwd", grid=(ns,),
        in_specs=[tok(3 * FW, 0), tok(FW, OFF_Z // FW), col, col, sca, sca, nws] + [HBM_SPEC] * ng,
        out_specs=[mat, tok(FW, 0), st] + [HBM_SPEC] * ng,
        out_shape=[jax.ShapeDtypeStruct((HEADS, T, HD), F32), jax.ShapeDtypeStruct((T, FW), F32),
                   jax.ShapeDtypeStruct((T // CH, HEADS, HD, HD), F32)] + _gather_shapes(gather),
        scratch_shapes=[pltpu.VMEM((HEADS, HD, HD), F32)] + (_gather_sems(ng) if ng else []),
        compiler_params=_cp("arbitrary"))(conv, rest, bpre, apre, alog, dtb, nw, *gather)


def _gdn_bwd(conv, rest, bpre, apre, alog, dtb, nw, states, o, dog, scatter=()):
    T = conv.shape[0]
    ns, mat, col, sca, nws, st, tok = _gdn_specs(T, True)
    rows = [slice(g * CH, (g + 1) * CH) for g in range(GDN_CPS)]
    ng = len(scatter)

    def body(*refs):
        c_ref, z_ref, b_ref, a_ref, al_ref, dt_ref, nw_ref, st_ref, o_ref, dog_ref = refs[:10]
        dc_ref, dz_ref, db_ref, da_ref, dal_ref, ddt_ref, dnw_ref = refs[10 + ng:17 + ng]
        ds_scr = refs[17 + 2 * ng]
        comm = (refs[10:10 + ng], refs[17 + ng:17 + 2 * ng], refs[18 + 2 * ng:])

        @pl.when(pl.program_id(0) == 0)
        def _():
            ds_scr[...] = jnp.zeros_like(ds_scr)
            dal_ref[...] = jnp.zeros_like(dal_ref)
            ddt_ref[...] = jnp.zeros_like(ddt_ref)
            dnw_ref[...] = jnp.zeros_like(dnw_ref)
            if ng:
                _scatter_phase(*comm, True)

        splits = [_split_heads(c_ref[rows[g], :]) for g in range(GDN_CPS)]
        locs = [_gdn_chunk(*splits[g], _row_to_col(b_ref[g]), _row_to_col(a_ref[g]), al_ref[...], dt_ref[...])
                for g in range(GDN_CPS)]
        dsn = ds_scr[...]
        acc_al, acc_dt, acc_nw = 0.0, 0.0, 0.0
        for g in reversed(range(GDN_CPS)):
            L, vv = locs[g], splits[g][2]
            tril, strict, lb = L["tril"], L["strict"], L["lb"]
            qn, kn, kb, beta, decay, eg, egl, cd = L["qn"], L["kn"], L["kb"], L["beta"], L["decay"], L["eg"], L["egl"], L["cd"]
            value, kcd, attn, qd, kt, tm = L["value"], L["kcd"], L["attn"], L["qd"], L["kt"], L["tm"]
            s = st_ref[g]
            v_new = value - _bd_lo(kcd, s, 2, 1)
            oo = o_ref[:, rows[g], :]
            zz, = _split_heads(z_ref[rows[g], :])
            dog_, = _split_heads(dog_ref[rows[g], :])
            sz = _sig(zz)
            rs = lax.rsqrt(jnp.mean(oo * oo, axis=-1, keepdims=True) + EPS)
            oh = oo * rs
            _store_heads(dz_ref, rows[g], 0, dog_ * oh * nw_ref[...] * sz * (1.0 + zz * (1.0 - sz)))
            don = dog_ * zz * sz
            gdy = don * nw_ref[...]
            do = rs * (gdy - oh * jnp.mean(oh * gdy, axis=-1, keepdims=True))
            acc_nw = acc_nw + jnp.sum(don * oh, axis=(0, 1), keepdims=True)
            d_vnew = _bd_lo(attn, do, 1, 1) + _bd_lo(kt, dsn, 2, 1)
            both = _bd_lo(_rows2(do, d_vnew), s, 2, 2)
            d_qd, d_kcd = both[:, :CH], -both[:, CH:]
            d_attn = jnp.where(tril, _bd_lo(do, v_new, 2, 2), 0.0)
            d_kt = _bd_lo(v_new, dsn, 2, 2)
            d_cd = jnp.sum(s * dsn, axis=(1, 2), keepdims=True)
            dsn_next = cd * dsn + _bd_lo(_rows2(qd, kcd), _rows2(do, -d_vnew), 1, 1)
            dsol = _bd_hi(tm, _lanes2(d_vnew, d_kcd), 1, 1)
            d_vb, d_kbg = dsol[:, :, :HD], dsol[:, :, HD:]
            da = -jnp.where(strict, _bd_lo(dsol, L["sol"], 2, 2), 0.0)
            dkk = da * decay
            dqk = d_attn * decay
            d_decay = da * L["kk"] + d_attn * L["qk"]
            both = _bd_lo(_rows2(dkk, dqk), kn, 2, 1)
            d_kb = both[:, :CH] + d_kbg * eg
            d_qn = both[:, CH:] + d_qd * eg
            d_kn = _bd_lo(_rows2(dkk, dqk), _rows2(kb, qn), 1, 1) + d_kt * egl
            dd = d_decay * decay
            d_gc = jnp.sum(dd + d_qd * qd + d_kbg * L["kbg"] - d_kt * kt, axis=-1, keepdims=True) \
                - _row_to_col(jnp.sum(dd, axis=1, keepdims=True))
            d_gl = jnp.sum(d_kt * kt, axis=(1, 2), keepdims=True) + d_cd * cd[:, :, 0:1]
            last = lax.broadcasted_iota(jnp.int32, (1, CH, 1), 1) == CH - 1
            d_gc = d_gc + jnp.where(last, d_gl, 0.0)
            dg = _bd_hi(lb, jnp.broadcast_to(d_gc, (HEADS, CH, HD)), 1, 1, True)[:, :, 0:1]
            d_apre = -dg * L["ea"] * _sig(L["x"])
            da_ref[g] = _col_to_row(d_apre)
            acc_al = acc_al + jnp.sum(dg * L["g"], axis=1, keepdims=True)
            acc_dt = acc_dt + jnp.sum(d_apre, axis=1, keepdims=True)
            d_beta = jnp.sum(d_kb * kn + d_vb * vv, axis=-1, keepdims=True)
            db_ref[g] = _col_to_row(d_beta * beta * (1.0 - beta))
            d_kn = d_kn + d_kb * beta
            qh = L["qh"]
            _store_heads(dc_ref, rows[g], 0, (HD ** -0.5) * L["rq"] * (d_qn - qh * jnp.sum(qh * d_qn, axis=-1, keepdims=True)))
            _store_heads(dc_ref, rows[g], 1, L["rk"] * (d_kn - kn * jnp.sum(kn * d_kn, axis=-1, keepdims=True)))
            _store_heads(dc_ref, rows[g], 2, d_vb * beta)
            dsn = dsn_next
        ds_scr[...] = dsn
        dal_ref[...] += acc_al
        ddt_ref[...] += acc_dt
        dnw_ref[...] += acc_nw
        if ng:
            pl.when(pl.program_id(0) == ns - 1)(functools.partial(_scatter_phase, *comm, False))

    c3 = jax.ShapeDtypeStruct((T // CH, HEADS, 1, CH), F32)
    s3 = jax.ShapeDtypeStruct((HEADS, 1, 1), F32)
    return pl.pallas_call(
        body, name="gdn_bwd", grid=(ns,),
        in_specs=[tok(3 * FW, 0), tok(FW, OFF_Z // FW), col, col, sca, sca, nws, st, mat, tok(FW, 0)] + [HBM_SPEC] * ng,
        out_specs=[tok(3 * FW, 0), tok(FW, 0), col, col, sca, sca, nws] + [HBM_SPEC] * ng,
        out_shape=[jax.ShapeDtypeStruct((T, 3 * FW), F32), jax.ShapeDtypeStruct((T, FW), F32), c3, c3, s3, s3,
                   jax.ShapeDtypeStruct((1, 1, HD), F32)] + _scatter_shapes(scatter),
        scratch_shapes=[pltpu.VMEM((HEADS, HD, HD), F32)] + (_scatter_sems(ng) if ng else []),
        compiler_params=_cp("arbitrary"))(conv, rest, bpre, apre, alog, dtb, nw, states, o, dog, *scatter)


def _local_step(x, target, w, ffn_shards=None):
    seq = x.shape[0]
    T = -(-(X0 + seq) // T_ALIGN) * T_ALIGN
    back = T - X0 - seq
    tb_att = _div(T, ATT_TILE)
    h0 = jnp.concatenate([jnp.zeros((FRONT, D), F32), w["meta"], x, jnp.zeros((back, D), F32)], axis=0)
    tgt = jnp.concatenate([jnp.zeros((X0, D), F32), target, jnp.zeros((back, D), F32)], axis=0)
    row = lambda v: v.reshape(1, -1)
    nmix, nffn, nfin = row(w["norm_mix"]), row(w["norm_ffn"]), row(w["norm_final"])
    gate_b = row(w["gate_bias"])
    fconv_b = row(w["ffn_conv_b"])
    bias128 = jnp.zeros((1, SMALL_W), F32).at[0, :HEADS].set(w["fgt_bias"])
    alog = w["a_log"].reshape(HEADS, 1, 1)
    dtb = w["dt_bias"].reshape(HEADS, 1, 1)
    gnw = w["gdn_norm"].reshape(1, 1, HD)

    a = _rms_fwd(h0, nmix, "rms_mix")
    pf = _mm(a, w["w_fox"], out_dtype=_BF, name="proj_fox")
    rest = _mm(a, w["w_rest"], name="proj_rest")
    small = rest[:, OFF_SMALL:]
    c8 = _fgate_fwd(small, bias128)[:, :HEADS]
    c_cols, c_rows = _att_cols(c8), _att_rows(c8, tb_att)
    ones_blk = jnp.asarray(np.kron(np.eye(HEADS, dtype=np.float32), np.ones((HD, HD), np.float32)))
    plan = _att_plan(_att_stats(pf, c_cols, ones_blk))
    o_fox, lse, lse2 = _fox_fwd(pf, c_cols, c_rows, plan)
    conv = _gconv_fwd(rest, w["gdn_conv"])
    chunk_rows = lambda a8: a8.reshape(T // CH, CH, HEADS).transpose(0, 2, 1)[:, :, None, :]
    bpre = chunk_rows(small[:, HEADS:2 * HEADS])
    apre = chunk_rows(small[:, 2 * HEADS:3 * HEADS])
    if ffn_shards is None:
        o_raw, og, states = _gdn_fwd(conv, rest, bpre, apre, alog, dtb, gnw)
        w_up, w_down = w["w_up"], w["w_down"]
    else:
        o_raw, og, states, got_up, got_down = _gdn_fwd(conv, rest, bpre, apre, alog, dtb, gnw, gather=ffn_shards)
        w_up, w_down = _from_chips(got_up, 1), _from_chips(got_down, 0)
    y_fox = _mm(o_fox, w["w_bfox"], name="y_fox")
    y_gdn = _mm(og, w["w_bgdn"], name="y_gdn")
    mix = _mix_fwd(rest, gate_b, y_fox, y_gdn)
    h1 = _mm(mix, w["w_out"], resid=h0, name="out_proj")
    b = _rms_fwd(h1, nffn, "rms_ffn")
    up = _mm(b, w_up, name="ffn_up")
    f = _glu_fwd(up, w["ffn_conv"], fconv_b)
    h2 = _mm(f, w_down, resid=h1, name="ffn_down")
    dh2, loss, d_nfin = _loss_head(h2, nfin, tgt, seq)

    d_f = _mm(dh2, w_down, tb=True, name="d_f")
    g_down = _mm_t(f, dh2, "g_down")
    dug, duu, dwg, dwu, dbg, dbu = _glu_bwd_du(up, w["ffn_conv"], fconv_b, d_f)
    dxg = _conv_bwd_dx(dug, w["ffn_conv"][:, :DFF], _BF, "fconv_dx_gate")
    dxu = _conv_bwd_dx(duu, w["ffn_conv"][:, DFF:], _BF, "fconv_dx_up")
    d_b = _mm(dxg, w_up[:, :DFF], tb=True, name="d_b_gate")
    d_b = _mm(dxu, w_up[:, DFF:], tb=True, resid=d_b, name="d_b_up")
    g_up = jnp.concatenate([_mm_t(b, dxg, "g_up_gate"), _mm_t(b, dxu, "g_up_up")], axis=1)
    dh1, d_nffn = _rms_bwd(h1, nffn, d_b, dh2, "rms_ffn_bwd")

    dmix = _mm(dh1, w["w_out"], tb=True, name="d_mix")
    g_out = _mm_t(mix, dh1, "g_out")
    dyf, dyg, dgf, dgg, dgbf, dgbg = _mix_bwd(rest, gate_b, y_fox, y_gdn, dmix)
    do_fox = _mm(dyf, w["w_bfox"], tb=True, out_dtype=_BF, name="d_o_fox")
    g_bfox = _mm_t(o_fox, dyf, "g_bfox")
    d_og = _mm(dyg, w["w_bgdn"], tb=True, name="d_o_gdn")
    g_bgdn = _mm_t(og, dyg, "g_bgdn")

    ones_col = jnp.asarray(np.kron(np.eye(HEADS, SMALL_W, dtype=np.float32), np.ones((HD, 1), np.float32)))
    delta, delta8 = _head_dot(do_fox, o_fox, ones_blk, ones_col)
    dq, dcq = _fox_bwd_dq(pf, do_fox, c_cols, c_rows, lse, delta, plan)
    dk, dv, dck = _fox_bwd_dkv(pf, do_fox, c_cols, _att_rows(c8 - _from_pairs(lse2), tb_att),
                               _att_rows(delta8[:, :HEADS], tb_att), plan)
    dc = jnp.pad(_from_pairs(dcq + dck), ((0, 0), (0, SMALL_W - HEADS)))
    dfp, d_fb = _fgate_bwd(small, bias128, dc)

    if ffn_shards is None:
        ffn_recv = None
        dconv, dz, dbp, dap, d_alog, d_dtb, d_gnw = _gdn_bwd(conv, rest, bpre, apre, alog, dtb, gnw, states, o_raw, d_og)
    else:
        send = [_by_chip(g_up.astype(jnp.bfloat16), 1), _by_chip(g_down.astype(jnp.bfloat16), 0)]
        dconv, dz, dbp, dap, d_alog, d_dtb, d_gnw, *ffn_recv = _gdn_bwd(
            conv, rest, bpre, apre, alog, dtb, gnw, states, o_raw, d_og, scatter=send)
    du_g, g_gconv = _gconv_bwd_du(rest, w["gdn_conv"], dconv)
    dgx = _conv_bwd_dx(du_g, w["gdn_conv"], _BF, "gconv_dx")
    token_rows = lambda a: a[:, :, 0, :].transpose(0, 2, 1).reshape(T, HEADS)
    dsmall = jnp.concatenate([dfp[:, :HEADS], token_rows(dbp), token_rows(dap),
                              jnp.zeros((T, SMALL_W - 3 * HEADS), F32)], axis=1)
    drest = jnp.concatenate([dgx, dz.astype(_BF), dgf, dgg, dsmall.astype(_BF)], axis=1)
    dfox = jnp.concatenate([dq, dk, dv], axis=1)
    d_a = _mm(dfox, w["w_fox"], tb=True, name="d_a_fox")
    d_a = _mm(drest, w["w_rest"], tb=True, resid=d_a, name="d_a_rest")
    g_fox = _mm_t(a, dfox, "g_w_fox")
    g_rest = _mm_t(a, drest, "g_w_rest")
    dh0, d_nmix = _rms_bwd(h0, nmix, d_a, dh1, "rms_mix_bwd")

    sm = lambda lo: g_rest[:, OFF_SMALL + lo:OFF_SMALL + lo + HEADS]
    g_w_in = jnp.concatenate([g_fox, sm(0), g_rest[:, :3 * FW], g_rest[:, OFF_Z:OFF_Z + FW], sm(HEADS), sm(2 * HEADS),
                              g_rest[:, OFF_GATES:OFF_GATES + 2 * D]], axis=1)
    grads = dict(
        meta_tokens=dh0[FRONT:X0], w_in=g_w_in, fgt_bias=d_fb[0, :HEADS], gdn_conv_w=g_gconv,
        gdn_a_log=d_alog.reshape(HEADS), gdn_dt_bias=d_dtb.reshape(HEADS), gdn_norm_w=d_gnw.reshape(HD),
        gate_bias=jnp.concatenate([dgbf, dgbg], axis=1).reshape(2 * D), w_branch_fox=g_bfox, w_branch_gdn=g_bgdn,
        w_out=g_out, norm_mix_w=d_nmix.reshape(D), norm_ffn_w=d_nffn.reshape(D), ffn_w_up=g_up,
        ffn_conv_w=jnp.concatenate([dwg, dwu], axis=1), ffn_conv_b=jnp.concatenate([dbg, dbu], axis=1).reshape(2 * DFF),
        ffn_w_down=g_down, norm_final_w=d_nfin.reshape(D), ffn_recv=ffn_recv)
    return loss, dh0[X0:X0 + seq], grads


N_CHIPS = 4
PACK_W = 1024
PACK_ROW_ALIGN = 32
BIG_EARLY = ("w_in", "w_branch_fox", "w_branch_gdn", "w_out")
BIG_LATE = ("ffn_w_up", "ffn_w_down")
BIG = BIG_EARLY + BIG_LATE
WEIGHTS = (
    ("meta_tokens", (N_META, D), 1), ("w_in", (D, 3 * FW + HEADS + 4 * FW + 2 * HEADS + 2 * D), 1), ("fgt_bias", (1, HEADS), None),
    ("gdn_conv_w", (4, 3 * FW), 1), ("gdn_a_log", (1, HEADS), None), ("gdn_dt_bias", (1, HEADS), None),
    ("gdn_norm_w", (1, HD), None), ("gate_bias", (1, 2 * D), None), ("w_branch_fox", (FW, D), 1),
    ("w_branch_gdn", (FW, D), 1), ("w_out", (D, D), 0), ("norm_mix_w", (1, D), None), ("norm_ffn_w", (1, D), None),
    ("ffn_w_up", (D, 2 * DFF), 1), ("ffn_conv_w", (3, 2 * DFF), 1), ("ffn_conv_b", (1, 2 * DFF), None),
    ("ffn_w_down", (DFF, D), 0), ("norm_final_w", (1, D), None))
SPLIT_F32 = ("meta_tokens", "gdn_conv_w", "ffn_conv_w")


def _shard_shape(shape, axis):
    if axis is None:
        return shape
    return tuple(s // N_CHIPS if a == axis else s for a, s in enumerate(shape))


def _shard_of(full, axis, q):
    if axis is None:
        return full
    n = full.shape[axis] // N_CHIPS
    return lax.slice_in_dim(full, q * n, (q + 1) * n, axis=axis)


def _pack_rows(n_elems):
    rows = -(-n_elems // PACK_W)
    return -(-rows // PACK_ROW_ALIGN) * PACK_ROW_ALIGN


def _pack(pieces, dtype):
    flat = jnp.concatenate([p.reshape(-1).astype(dtype) for p in pieces])
    rows = _pack_rows(flat.shape[0])
    return jnp.pad(flat, (0, rows * PACK_W - flat.shape[0])).reshape(rows, PACK_W)


def _unpack(slab, shapes):
    flat = slab.reshape(-1)
    out, off = [], 0
    for s in shapes:
        n = int(np.prod(s))
        out.append(flat[off:off + n].reshape(s))
        off += n
    return out


HBM_SPEC = pl.BlockSpec(memory_space=pltpu.HBM)
MESH_ID = pl.DeviceIdType.MESH


def _scatter_chips(srcs, name):
    n = len(srcs)

    def body(*refs):
        _scatter_phase(refs[:n], refs[n:2 * n], refs[2 * n:], True)
        _scatter_phase(refs[:n], refs[n:2 * n], refs[2 * n:], False)

    return pl.pallas_call(body, name=name, in_specs=[HBM_SPEC] * n, out_specs=[HBM_SPEC] * n,
                          out_shape=_scatter_shapes(srcs), scratch_shapes=_scatter_sems(n))(*srcs)


def _scatter_shapes(srcs):
    return [jax.ShapeDtypeStruct(s.shape, s.dtype) for s in srcs]


def _scatter_sems(n):
    return [pltpu.SemaphoreType.DMA((3 * n,)), pltpu.SemaphoreType.DMA((3 * n,)), pltpu.SemaphoreType.DMA((n,))]


def _scatter_phase(src_refs, out_refs, sems, issue):
    n = len(src_refs)
    send_sems, recv_sems, local_sems = sems
    x, y, c = lax.axis_index("x"), lax.axis_index("y"), lax.axis_index("c")
    q = 2 * x + y
    peers = [(1 - x, y), (x, 1 - y), (1 - x, 1 - y)]

    def remote(a, k, src_slot, dst_slot):
        px, py = peers[k]
        return pltpu.make_async_remote_copy(
            src_ref=src_refs[a].at[src_slot], dst_ref=out_refs[a].at[dst_slot], send_sem=send_sems.at[3 * a + k],
            recv_sem=recv_sems.at[3 * a + k], device_id=(px, py, c), device_id_type=MESH_ID)

    mine = [pltpu.make_async_copy(src_refs[a].at[q], out_refs[a].at[q], local_sems.at[a]) for a in range(n)]
    sends = [remote(a, k, 2 * px + py, q) for a in range(n) for k, (px, py) in enumerate(peers)]
    if issue:
        for cp in mine + sends:
            cp.start()
        return
    for a in range(n):
        for k, (px, py) in enumerate(peers):
            remote(a, k, 0, 2 * px + py).wait_recv()
    for cp in sends:
        cp.wait_send()
    for cp in mine:
        cp.wait()


def _gather_chips(srcs, name):
    n = len(srcs)

    def body(*refs):
        _gather_phase(refs[:n], refs[n:2 * n], refs[2 * n:], True)
        _gather_phase(refs[:n], refs[n:2 * n], refs[2 * n:], False)

    return pl.pallas_call(body, name=name, in_specs=[HBM_SPEC] * n, out_specs=[HBM_SPEC] * n,
                          out_shape=_gather_shapes(srcs), scratch_shapes=_gather_sems(n))(*srcs)


def _gather_shapes(srcs):
    return [jax.ShapeDtypeStruct((N_CHIPS,) + s.shape, s.dtype) for s in srcs]


def _gather_sems(n):
    return [pltpu.SemaphoreType.DMA((3 * n,))] * 4 + [pltpu.SemaphoreType.DMA((n,))]


def _gather_phase(src_refs, out_refs, sems, issue):
    n = len(src_refs)
    ici_send, ici_recv, d2d_send, d2d_recv, local_sems = sems
    x, y, c = lax.axis_index("x"), lax.axis_index("y"), lax.axis_index("c")
    q = 2 * x + y
    peers = [(1 - x, y), (x, 1 - y), (1 - x, 1 - y)]

    def half(a, which):
        r = src_refs[a].shape[0] // 2
        return pl.ds(which * r, r)

    def ici(a, k, slot):
        px, py = peers[k]
        return pltpu.make_async_remote_copy(
            src_ref=src_refs[a].at[half(a, c)], dst_ref=out_refs[a].at[slot, half(a, c)], send_sem=ici_send.at[3 * a + k],
            recv_sem=ici_recv.at[3 * a + k], device_id=(px, py, c), device_id_type=MESH_ID)

    def d2d(a, k, which):
        px, py = peers[k]
        rows = out_refs[a].at[2 * px + py, half(a, which)]
        return pltpu.make_async_remote_copy(
            src_ref=rows, dst_ref=rows, send_sem=d2d_send.at[3 * a + k], recv_sem=d2d_recv.at[3 * a + k],
            device_id=(x, y, 1 - c), device_id_type=MESH_ID)

    mine = [pltpu.make_async_copy(src_refs[a], out_refs[a].at[q], local_sems.at[a]) for a in range(n)]
    sends = [ici(a, k, q) for a in range(n) for k in range(3)]
    if issue:
        for cp in mine + sends:
            cp.start()
        return
    passed = []
    for a in range(n):
        for k, (px, py) in enumerate(peers):
            ici(a, k, 2 * px + py).wait_recv()
            passed.append(d2d(a, k, c))
            passed[-1].start()
    for a in range(n):
        for k in range(3):
            d2d(a, k, 1 - c).wait_recv()
    for cp in sends + passed:
        cp.wait_send()
    for cp in mine:
        cp.wait()


def _sibling_swap(slabs, name):
    n = len(slabs)

    def body(*refs):
        src_refs, out_refs, send_sems, recv_sems = refs[:n], refs[n:2 * n], refs[2 * n], refs[2 * n + 1]
        x, y, c = lax.axis_index("x"), lax.axis_index("y"), lax.axis_index("c")
        cps = [pltpu.make_async_remote_copy(src_ref=src_refs[a], dst_ref=out_refs[a], send_sem=send_sems.at[a],
                                            recv_sem=recv_sems.at[a], device_id=(x, y, 1 - c), device_id_type=MESH_ID)
               for a in range(n)]
        for cp in cps:
            cp.start()
        for cp in cps:
            cp.wait_recv()
        for cp in cps:
            cp.wait_send()

    return pl.pallas_call(
        body, name=name, in_specs=[HBM_SPEC] * n, out_specs=[HBM_SPEC] * n,
        out_shape=[jax.ShapeDtypeStruct(s.shape, s.dtype) for s in slabs],
        scratch_shapes=[pltpu.SemaphoreType.DMA((n,)), pltpu.SemaphoreType.DMA((n,))])(*slabs)


def _sum_chips(r, name):
    rows, cols = r.shape[1:]
    tb = _div(rows, 256, 16)

    def body(r0, r1, r2, r3, o_ref):
        o_ref[...] = ((r0[0].astype(F32) + r1[0].astype(F32)) + r2[0].astype(F32)) + r3[0].astype(F32)

    spec = lambda j: pl.BlockSpec((1, tb, cols), lambda i: (j, i, 0))
    return pl.pallas_call(
        body, name=name, grid=(rows // tb,), in_specs=[spec(0), spec(1), spec(2), spec(3)],
        out_specs=pl.BlockSpec((tb, cols), lambda i: (i, 0)), out_shape=jax.ShapeDtypeStruct((rows, cols), F32),
        compiler_params=_cp("parallel"))(r, r, r, r)


def _adamw(w, m, v, p, q, name):
    rows, cols = w.shape
    tb = _div(rows, 256, 8)

    def body(w_ref, m_ref, v_ref, p_ref, q_ref, g_ref, d_ref, nm_ref, nv_ref):
        g = p_ref[...] + q_ref[...]
        m_new = B1 * m_ref[...] + (1.0 - B1) * g
        v_new = B2 * v_ref[...] + (1.0 - B2) * (g * g)
        g_ref[...] = g
        nm_ref[...] = m_new
        nv_ref[...] = v_new
        m_hat = m_new / (1.0 - B1 ** STEP)
        v_hat = v_new / (1.0 - B2 ** STEP)
        d_ref[...] = -LR * (m_hat / (jnp.sqrt(v_hat) + AEPS) + WD * w_ref[...])

    spec = pl.BlockSpec((tb, cols), lambda i: (i, 0))
    return pl.pallas_call(
        body, name=name, grid=(rows // tb,), in_specs=[spec] * 5, out_specs=[spec] * 4,
        out_shape=[jax.ShapeDtypeStruct((rows, cols), F32)] * 4, compiler_params=_cp("parallel"))(w, m, v, p, q)


def _split_w_in(w_in):
    o1 = 3 * FW
    o2 = o1 + HEADS
    o3 = o2 + 3 * FW
    o4 = o3 + FW
    o5 = o4 + HEADS
    o6 = o5 + HEADS
    pad = jnp.zeros((w_in.shape[0], SMALL_W - 3 * HEADS), w_in.dtype)
    rest = jnp.concatenate([w_in[:, o2:o3], w_in[:, o3:o4], w_in[:, o6:], w_in[:, o1:o2], w_in[:, o4:o5], w_in[:, o5:o6], pad],
                           axis=1)
    return w_in[:, :o1], rest


AXIS = {n: a for n, _, a in WEIGHTS}
SMALL = tuple(n for n, _, _ in WEIGHTS if n not in BIG)


def _by_chip(full, axis):
    rows, cols = full.shape
    if axis == 0:
        return full.reshape(N_CHIPS, rows // N_CHIPS, cols)
    return full.reshape(rows, N_CHIPS, cols // N_CHIPS).transpose(1, 0, 2)


def _from_chips(parts, axis):
    _, r, c = parts.shape
    if axis == 0:
        return parts.reshape(N_CHIPS * r, c)
    return parts.transpose(1, 0, 2).reshape(r, N_CHIPS * c)


def _gather_weights(shards):
    hi = {n: shards[n].astype(jnp.bfloat16) for n in SPLIT_F32}
    lo = [(shards[n] - hi[n].astype(F32)).astype(jnp.bfloat16) for n in SPLIT_F32]
    slab = _pack([hi[n] for n in SPLIT_F32] + lo, jnp.bfloat16)
    got = _gather_chips([shards[n].astype(jnp.bfloat16) for n in BIG_EARLY] + [slab], "gather_weights")
    full = {n: _from_chips(g, AXIS[n]) for n, g in zip(BIG_EARLY, got)}
    shapes = [shards[n].shape for n in SPLIT_F32] * 2
    per_chip = [_unpack(got[-1][j], shapes) for j in range(N_CHIPS)]
    for i, n in enumerate(SPLIT_F32):
        join = lambda off: jnp.concatenate([per_chip[j][off + i] for j in range(N_CHIPS)], axis=1).astype(F32)
        full[n] = join(0) + join(len(SPLIT_F32))
    return full


def kernel(x, meta_tokens, w_in, fgt_bias, gdn_conv_w, gdn_a_log, gdn_dt_bias, gdn_norm_w, gate_bias, w_branch_fox, w_branch_gdn, w_out, norm_mix_w, norm_ffn_w, ffn_w_up, ffn_conv_w, ffn_conv_b, ffn_w_down, norm_final_w, loss_target, m_meta_tokens, m_w_in, m_fgt_bias, m_gdn_conv_w, m_gdn_a_log, m_gdn_dt_bias, m_gdn_norm_w, m_gate_bias, m_w_branch_fox, m_w_branch_gdn, m_w_out, m_norm_mix_w, m_norm_ffn_w, m_ffn_w_up, m_ffn_conv_w, m_ffn_conv_b, m_ffn_w_down, m_norm_final_w, v_meta_tokens, v_w_in, v_fgt_bias, v_gdn_conv_w, v_gdn_a_log, v_gdn_dt_bias, v_gdn_norm_w, v_gate_bias, v_w_branch_fox, v_w_branch_gdn, v_w_out, v_norm_mix_w, v_norm_ffn_w, v_ffn_w_up, v_ffn_conv_w, v_ffn_conv_b, v_ffn_w_down, v_norm_final_w):
    weights = dict(meta_tokens=meta_tokens, w_in=w_in, fgt_bias=fgt_bias, gdn_conv_w=gdn_conv_w, gdn_a_log=gdn_a_log, gdn_dt_bias=gdn_dt_bias, gdn_norm_w=gdn_norm_w, gate_bias=gate_bias, w_branch_fox=w_branch_fox, w_branch_gdn=w_branch_gdn, w_out=w_out, norm_mix_w=norm_mix_w, norm_ffn_w=norm_ffn_w, ffn_w_up=ffn_w_up, ffn_conv_w=ffn_conv_w, ffn_conv_b=ffn_conv_b, ffn_w_down=ffn_w_down, norm_final_w=norm_final_w)
    m_in = dict(meta_tokens=m_meta_tokens, w_in=m_w_in, fgt_bias=m_fgt_bias, gdn_conv_w=m_gdn_conv_w, gdn_a_log=m_gdn_a_log, gdn_dt_bias=m_gdn_dt_bias, gdn_norm_w=m_gdn_norm_w, gate_bias=m_gate_bias, w_branch_fox=m_w_branch_fox, w_branch_gdn=m_w_branch_gdn, w_out=m_w_out, norm_mix_w=m_norm_mix_w, norm_ffn_w=m_norm_ffn_w, ffn_w_up=m_ffn_w_up, ffn_conv_w=m_ffn_conv_w, ffn_conv_b=m_ffn_conv_b, ffn_w_down=m_ffn_w_down, norm_final_w=m_norm_final_w)
    v_in = dict(meta_tokens=v_meta_tokens, w_in=v_w_in, fgt_bias=v_fgt_bias, gdn_conv_w=v_gdn_conv_w, gdn_a_log=v_gdn_a_log, gdn_dt_bias=v_gdn_dt_bias, gdn_norm_w=v_gdn_norm_w, gate_bias=v_gate_bias, w_branch_fox=v_w_branch_fox, w_branch_gdn=v_w_branch_gdn, w_out=v_w_out, norm_mix_w=v_norm_mix_w, norm_ffn_w=v_norm_ffn_w, ffn_w_up=v_ffn_w_up, ffn_conv_w=v_ffn_conv_w, ffn_conv_b=v_ffn_conv_b, ffn_w_down=v_ffn_w_down, norm_final_w=v_norm_final_w)
    shard2d = {n: _shard_shape(s, a) for n, s, a in WEIGHTS}
    as2d = lambda d: {n: d[n].reshape(shard2d[n]) for n, _, _ in WEIGHTS}
    w2, m2, v2 = as2d(weights), as2d(m_in), as2d(v_in)

    full = _gather_weights(w2)
    w_fox, w_rest = _split_w_in(full["w_in"])
    flat = lambda n: w2[n].reshape(-1)
    local_w = dict(
        meta=full["meta_tokens"], w_fox=w_fox, w_rest=w_rest, fgt_bias=flat("fgt_bias"), gdn_conv=full["gdn_conv_w"],
        a_log=flat("gdn_a_log"), dt_bias=flat("gdn_dt_bias"), gdn_norm=flat("gdn_norm_w"), gate_bias=flat("gate_bias"),
        w_bfox=full["w_branch_fox"], w_bgdn=full["w_branch_gdn"], w_out=full["w_out"], norm_mix=flat("norm_mix_w"),
        norm_ffn=flat("norm_ffn_w"), ffn_conv=full["ffn_conv_w"], ffn_conv_b=flat("ffn_conv_b"),
        norm_final=flat("norm_final_w"))

    loss, grad_x, grads = _local_step(x[0], loss_target[0], local_w, [w2[n].astype(jnp.bfloat16) for n in BIG_LATE])

    g2 = {n: grads[n].reshape(s) for n, s, _ in WEIGHTS}
    send = [_by_chip(g2[n].astype(jnp.bfloat16), AXIS[n]) for n in BIG_EARLY]
    send.append(jnp.stack([_pack([_shard_of(g2[n], AXIS[n], j) for n in SMALL], F32) for j in range(N_CHIPS)]))
    recv = list(_scatter_chips(send, "scatter_grads"))
    recv = recv[:-1] + list(grads["ffn_recv"]) + recv[-1:]
    parts = [_sum_chips(r, "sum_" + n) for r, n in zip(recv, BIG + ("small",))]
    others = _sibling_swap(parts, "swap_grads")
    slab = lambda d: _pack([d[n] for n in SMALL], F32)
    state = [(w2[n], m2[n], v2[n]) for n in BIG] + [(slab(w2), slab(m2), slab(v2))]
    outs = [_adamw(w, m, v, p, q, "adamw_" + n) for (w, m, v), p, q, n in zip(state, parts, others, BIG + ("small",))]
    small = [_unpack(o, [weights[n].shape for n in SMALL]) for o in outs[-1]]
    result = []
    for kind in range(4):
        by_name = {n: outs[i][kind].reshape(weights[n].shape) for i, n in enumerate(BIG)}
        by_name.update(zip(SMALL, small[kind]))
        result += [by_name[n] for n, _, _ in WEIGHTS]
    total = lax.psum(loss[0, 0], ("x", "y", "c"))
    return (total, grad_x[None], *result)
```

```python
import functools

import numpy as np
import jax
import jax.numpy as jnp
from jax import lax
from jax.experimental import pallas as pl
from jax.experimental.pallas import tpu as pltpu

F32 = jnp.float32
_BF = jnp.bfloat16
HI = lax.Precision.HIGHEST

D = 1024
N_META = 16
CH = 64
FRONT = CH - N_META
X0 = CH
HEADS = 8
HD = 64
FW = HEADS * HD
DFF = 2816
EPS = 1e-6
NEG = -1e30
T_ALIGN = 256
SMALL_W = 128
REST_W = 3 * FW + FW + 2 * D + SMALL_W
OFF_Z = 3 * FW
OFF_GATES = 4 * FW
OFF_SMALL = 4 * FW + 2 * D

LR, B1, B2, AEPS, WD, STEP = 0.001, 0.9, 0.999, 1e-08, 0.01, 10

VMEM_LIMIT = 56 * 1024 * 1024
ROW_TILE = 640
MM_TM, MM_TN, MM_TK = 1280, 512, 2816
ATT_TILE = 640


def _div(n, target, mult=128):
    if n <= target:
        return n
    best = None
    for d in range(mult, target + 1, mult):
        if n % d == 0:
            best = d
    assert best is not None, (n, target, mult)
    return best


def _cp(*sem):
    return pltpu.CompilerParams(dimension_semantics=sem, vmem_limit_bytes=VMEM_LIMIT)


def _sig(x):
    return 1.0 / (1.0 + jnp.exp(-x))


MM_VMEM_BUDGET = 40 * 1024 * 1024
MM_STEP_BYTES = 1 << 20


def _mm_tiles(m, n, k, sa, sb, so, has_resid):
    divs = lambda d, cap: [t for t in range(128, min(d, cap) + 1, 128) if d % t == 0] or [d]
    best = None
    for tm in divs(m, MM_TM * 2):
        for tn in divs(n, 4608):
            fixed = tm * tn * (4 + 2 * so + (8 if has_resid else 0))
            tks = [t for t in divs(k, MM_TK) if 2 * t * (tm * sa + tn * sb) + fixed <= MM_VMEM_BUDGET]
            if not tks:
                continue
            tk = tks[-1]
            steps = (m // tm) * (n // tn) * (k // tk)
            rmw = (k // tk - 1) * m * n * 4
            cost = (n // tn) * m * k * sa + (m // tm) * k * n * sb + steps * MM_STEP_BYTES + rmw
            if best is None or cost < best[0]:
                best = (cost, tm, tn, tk)
    assert best is not None, (m, n, k)
    return best[1:]


def _mm_t(a, b, name):
    return _mm(a, b, ta=True, name=name)


def _mm(a, b, *, ta=False, tb=False, out_dtype=F32, resid=None, name):
    K, M = a.shape if ta else a.shape[::-1]
    N = b.shape[0] if tb else b.shape[1]
    assert (b.shape[1] if tb else b.shape[0]) == K
    tm, tn, tk = _mm_tiles(M, N, K, a.dtype.itemsize, b.dtype.itemsize, jnp.dtype(out_dtype).itemsize, resid is not None)
    nk = K // tk
    dims = (((0 if ta else 1,), (1 if tb else 0,)), ((), ()))
    mxu = _BF

    def body(*refs):
        if resid is None:
            a_ref, b_ref, o_ref, acc = refs
            r_ref = None
        else:
            a_ref, b_ref, r_ref, o_ref, acc = refs
        k = pl.program_id(2)
        part = lax.dot_general(a_ref[...].astype(mxu), b_ref[...].astype(mxu), dims, preferred_element_type=F32)

        @pl.when(k == 0)
        def _():
            acc[...] = part

        @pl.when(k > 0)
        def _():
            acc[...] += part

        @pl.when(k == nk - 1)
        def _():
            r = acc[...]
            if r_ref is not None:
                r = r + r_ref[...]
            o_ref[...] = r.astype(o_ref.dtype)

    a_spec = pl.BlockSpec((tk, tm), lambda i, j, k: (k, i)) if ta else pl.BlockSpec((tm, tk), lambda i, j, k: (i, k))
    b_spec = pl.BlockSpec((tn, tk), lambda i, j, k: (j, k)) if tb else pl.BlockSpec((tk, tn), lambda i, j, k: (k, j))
    o_spec = pl.BlockSpec((tm, tn), lambda i, j, k: (i, j))
    in_specs = [a_spec, b_spec] + ([o_spec] if resid is not None else [])
    args = (a, b) + ((resid,) if resid is not None else ())
    return pl.pallas_call(
        body, name=name, grid=(M // tm, N // tn, nk), in_specs=in_specs, out_specs=o_spec,
        out_shape=jax.ShapeDtypeStruct((M, N), out_dtype), scratch_shapes=[pltpu.VMEM((tm, tn), F32)],
        compiler_params=_cp("parallel", "parallel", "arbitrary"))(*args)


def _rms_fwd(h, w, name):
    T = h.shape[0]
    tb = _div(T, ROW_TILE, 8)

    def body(h_ref, w_ref, o_ref):
        x = h_ref[...]
        r = lax.rsqrt(jnp.mean(x * x, axis=-1, keepdims=True) + EPS)
        o_ref[...] = (x * r * w_ref[...]).astype(o_ref.dtype)

    return pl.pallas_call(
        body, name=name, grid=(T // tb,),
        in_specs=[pl.BlockSpec((tb, D), lambda i: (i, 0)), pl.BlockSpec((1, D), lambda i: (0, 0))],
        out_specs=pl.BlockSpec((tb, D), lambda i: (i, 0)), out_shape=jax.ShapeDtypeStruct((T, D), _BF),
        compiler_params=_cp("parallel"))(h, w)


def _rms_bwd(h, w, dy, dres, name):
    T = h.shape[0]
    tb = _div(T, ROW_TILE, 8)

    def body(h_ref, w_ref, dy_ref, dr_ref, dh_ref, dw_ref):
        x = h_ref[...]
        r = lax.rsqrt(jnp.mean(x * x, axis=-1, keepdims=True) + EPS)
        xh = x * r
        dy = dy_ref[...]
        g = dy * w_ref[...]
        dh_ref[...] = dr_ref[...] + r * (g - xh * jnp.mean(xh * g, axis=-1, keepdims=True))
        part = jnp.sum(dy * xh, axis=0, keepdims=True)

        @pl.when(pl.program_id(0) == 0)
        def _():
            dw_ref[...] = part

        @pl.when(pl.program_id(0) > 0)
        def _():
            dw_ref[...] += part

    row = pl.BlockSpec((tb, D), lambda i: (i, 0))
    one = pl.BlockSpec((1, D), lambda i: (0, 0))
    return pl.pallas_call(
        body, name=name, grid=(T // tb,), in_specs=[row, one, row, row], out_specs=[row, one],
        out_shape=[jax.ShapeDtypeStruct((T, D), F32), jax.ShapeDtypeStruct((1, D), F32)],
        compiler_params=_cp("arbitrary"))(h, w, dy, dres)


def _fgate_fwd(small, bias):
    T = small.shape[0]
    tb = _div(T, ROW_TILE, 8)

    def body(s_ref, b_ref, c_ref, carry):
        @pl.when(pl.program_id(0) == 0)
        def _():
            carry[...] = jnp.zeros_like(carry)

        lf = jax.nn.log_sigmoid(s_ref[...] + b_ref[...])
        r = lax.broadcasted_iota(jnp.int32, (tb, tb), 0)
        c = lax.broadcasted_iota(jnp.int32, (tb, tb), 1)
        tri = (r >= c).astype(F32)
        cs = jnp.dot(tri, lf, precision=HI, preferred_element_type=F32) + carry[...]
        c_ref[...] = cs
        carry[...] = cs[tb - 1:tb, :]

    return pl.pallas_call(
        body, name="fgate_fwd", grid=(T // tb,),
        in_specs=[pl.BlockSpec((tb, SMALL_W), lambda i: (i, 0)), pl.BlockSpec((1, SMALL_W), lambda i: (0, 0))],
        out_specs=pl.BlockSpec((tb, SMALL_W), lambda i: (i, 0)), out_shape=jax.ShapeDtypeStruct((T, SMALL_W), F32),
        scratch_shapes=[pltpu.VMEM((1, SMALL_W), F32)], compiler_params=_cp("arbitrary"))(small, bias)


def _fgate_bwd(small, bias, dc):
    T = small.shape[0]
    tb = _div(T, ROW_TILE, 8)
    nb = T // tb

    def body(s_ref, b_ref, dc_ref, df_ref, db_ref, carry):
        @pl.when(pl.program_id(0) == 0)
        def _():
            carry[...] = jnp.zeros_like(carry)

        r = lax.broadcasted_iota(jnp.int32, (tb, tb), 0)
        c = lax.broadcasted_iota(jnp.int32, (tb, tb), 1)
        tri = (r <= c).astype(F32)
        dlf = jnp.dot(tri, dc_ref[...], precision=HI, preferred_element_type=F32) + carry[...]
        carry[...] = dlf[0:1, :]
        df = dlf * _sig(-(s_ref[...] + b_ref[...]))
        df_ref[...] = df
        part = jnp.sum(df, axis=0, keepdims=True)

        @pl.when(pl.program_id(0) == 0)
        def _():
            db_ref[...] = part

        @pl.when(pl.program_id(0) > 0)
        def _():
            db_ref[...] += part

    rev = pl.BlockSpec((tb, SMALL_W), lambda i: (nb - 1 - i, 0))
    one = pl.BlockSpec((1, SMALL_W), lambda i: (0, 0))
    return pl.pallas_call(
        body, name="fgate_bwd", grid=(nb,), in_specs=[rev, one, rev], out_specs=[rev, one],
        out_shape=[jax.ShapeDtypeStruct((T, SMALL_W), F32), jax.ShapeDtypeStruct((1, SMALL_W), F32)],
        scratch_shapes=[pltpu.VMEM((1, SMALL_W), F32)], compiler_params=_cp("arbitrary"))(small, bias, dc)


def _att_rows(a8, tb):
    T = a8.shape[0]
    return a8.T.reshape(HEADS // 2, 2, T // tb, tb).transpose(0, 2, 1, 3)


def _att_cols(a8):
    return jnp.repeat(a8, HD, axis=1)


def _pair_cols(tb):
    return pl.BlockSpec((1, tb, 2), lambda p, i, *_: (p, i, 0))


def _from_pairs(a):
    return a.transpose(1, 0, 2).reshape(a.shape[1], HEADS)


EXP_ZERO = -104.0
SKIP_SLACK = 2.0
NORM_SLACK = 1.02


def _att_stats(qkv, c_cols, ones_blk):
    T = qkv.shape[0]
    tb = _div(T, ATT_TILE)

    def body(q_ref, k_ref, c_ref, e_ref, o_ref):
        q = q_ref[...].astype(F32)
        k = k_ref[...].astype(F32)
        e = e_ref[...].astype(_BF)
        qn = jnp.max(jnp.dot((q * q).astype(_BF), e, preferred_element_type=F32), axis=0, keepdims=True)
        kn = jnp.max(jnp.dot((k * k).astype(_BF), e, preferred_element_type=F32), axis=0, keepdims=True)
        c = c_ref[...]
        o_ref[0] = jnp.concatenate([jnp.sqrt(qn), jnp.sqrt(kn), jnp.max(c, axis=0, keepdims=True),
                                    jnp.min(c, axis=0, keepdims=True), jnp.zeros((4, FW), F32)], axis=0)

    blk = lambda off: pl.BlockSpec((tb, FW), lambda i: (i, off))
    return pl.pallas_call(
        body, name="att_stats", grid=(T // tb,),
        in_specs=[blk(0), blk(1), blk(0), pl.BlockSpec((FW, FW), lambda i: (0, 0))],
        out_specs=pl.BlockSpec((1, 8, FW), lambda i: (i, 0, 0)), out_shape=jax.ShapeDtypeStruct((T // tb, 8, FW), F32),
        compiler_params=_cp("parallel"))(qkv, qkv, c_cols, ones_blk)


def _att_plan(stats):
    nb = stats.shape[0]
    st = stats[:, :4, ::HD]
    qmax, kmax, cmax, cmin = (st[:, r, :].T for r in range(4))
    bound = (HD ** -0.5) * NORM_SLACK * qmax[:, :, None] * (kmax[:, None, :] + kmax[:, :, None]) \
        + cmax[:, :, None] - cmin[:, None, :] + SKIP_SLACK
    ii = lax.broadcasted_iota(jnp.int32, (nb, nb), 0)
    jj = lax.broadcasted_iota(jnp.int32, (nb, nb), 1)
    skip = (bound < EXP_ZERO) & (jj < ii)[None]
    live = (~skip & (jj <= ii)[None]).reshape(HEADS // 2, 2, nb, nb).any(axis=1)
    jfirst = jnp.argmax(live, axis=2).astype(jnp.int32)
    ilast = (nb - 1 - jnp.argmax(live[:, ::-1, :], axis=1)).astype(jnp.int32)
    return skip.astype(jnp.int32).reshape(-1), jfirst.reshape(-1), ilast.reshape(-1)


def _fox_fwd(qkv, c_cols, c_rows, plan):
    T = qkv.shape[0]
    tb = _div(T, ATT_TILE)
    nb = T // tb
    npair = HEADS // 2
    scale = HD ** -0.5
    dn = (((1,), (1,)), ((), ()))

    def body(skip_ref, jfirst_ref, ilast_ref, q_ref, k_ref, v_ref, cq_ref, ck_ref, o_ref, l_ref, l2_ref, m_scr, acc_scr):
        pr = pl.program_id(0)
        i = pl.program_id(1)
        q = q_ref[...]
        lane = lax.broadcasted_iota(jnp.int32, (1, 2 * HD), 1)
        sel0 = lane < HD
        zero = jnp.zeros_like(q)
        qh = (jnp.where(sel0, q, zero) * scale, jnp.where(sel0, zero, q) * scale)
        cq = (cq_ref[:, 0:1], cq_ref[:, HD:HD + 1])
        row = i * tb + lax.broadcasted_iota(jnp.int32, (tb, 1), 0)
        m_scr[...] = jnp.full(m_scr.shape, NEG, F32)
        acc_scr[...] = jnp.zeros(acc_scr.shape, F32)

        def tile(j, h, masked):
            start = pl.multiple_of(j * tb, tb)
            kb = k_ref[pl.ds(start, tb), :]
            vb = v_ref[pl.ds(start, tb), :]
            one = jnp.ones_like(vb)
            vh = jnp.where(sel0, vb, one) if h == 0 else jnp.where(sel0, one, vb)
            t = lax.dot_general(qh[h], kb, dn, preferred_element_type=F32) - ck_ref[0, j, h:h + 1, :]
            if masked:
                col = j * tb + lax.broadcasted_iota(jnp.int32, (1, tb), 1)
                mask = (col <= row) & (col >= FRONT)
                t = jnp.where(mask, t, NEG)
            m = m_scr[h]
            m_new = jnp.maximum(m, jnp.max(t, axis=-1, keepdims=True) + cq[h])
            p = jnp.exp(t + (cq[h] - m_new))
            if masked:
                p = jnp.where(mask, p, 0.0)
            m_scr[h] = m_new
            acc_scr[h] = acc_scr[h] * jnp.exp(m - m_new) + jnp.dot(p.astype(vb.dtype), vh, preferred_element_type=F32)

        def step(j, _):
            edge = (j == 0) | (j == i)
            for h in range(2):
                live = skip_ref[((2 * pr + h) * nb + i) * nb + j] == 0
                pl.when(live & edge)(functools.partial(tile, j, h, True))
                pl.when(live & jnp.logical_not(edge))(functools.partial(tile, j, h, False))
            return 0

        lax.fori_loop(jfirst_ref[pr * nb + i], i + 1, step, 0)
        outs, lses = [], []
        for h in range(2):
            l = acc_scr[h][:, (1 - h) * HD:(1 - h) * HD + 1]
            ok = l > 0.0
            ls = jnp.where(ok, l, 1.0)
            outs.append(jnp.where(ok, acc_scr[h] / ls, 0.0))
            lses.append(jnp.where(ok, m_scr[h] + jnp.log(ls), 0.0))
        o_ref[...] = jnp.where(sel0, outs[0], outs[1])
        l_ref[...] = jnp.where(sel0, lses[0], lses[1])
        l2_ref[0, :, 0:1] = lses[0]
        l2_ref[0, :, 1:2] = lses[1]

    blk = lambda off: pl.BlockSpec((tb, 2 * HD), lambda p, i, *_: (i, off + p))
    full = lambda off: pl.BlockSpec((T, 2 * HD), lambda p, i, *_: (0, off + p))
    rows = pl.BlockSpec((1, nb, 2, tb), lambda p, i, *_: (p, 0, 0, 0))
    return pl.pallas_call(
        body, name="fox_fwd",
        grid_spec=pltpu.PrefetchScalarGridSpec(
            num_scalar_prefetch=3, grid=(npair, nb),
            in_specs=[blk(0), full(npair), full(2 * npair), blk(0), rows], out_specs=[blk(0), blk(0), _pair_cols(tb)],
            scratch_shapes=[pltpu.VMEM((2, tb, 1), F32), pltpu.VMEM((2, tb, 2 * HD), F32)]),
        out_shape=[jax.ShapeDtypeStruct((T, FW), F32), jax.ShapeDtypeStruct((T, FW), F32),
                   jax.ShapeDtypeStruct((npair, T, 2), F32)],
        compiler_params=_cp("parallel", "arbitrary"))(*plan, qkv, qkv, qkv, c_cols, c_rows)


def _fox_bwd_dq(qkv, do, c_cols, c_rows, lse, delta, plan):
    T = qkv.shape[0]
    tb = _div(T, ATT_TILE)
    nb = T // tb
    npair = HEADS // 2
    scale = HD ** -0.5
    dn = (((1,), (1,)), ((), ()))

    def body(skip_ref, jfirst_ref, ilast_ref, q_ref, k_ref, v_ref, do_ref, cq_ref, ck_ref, l_ref, dl_ref, dq_ref, dc_ref,
             dq_scr):
        pr = pl.program_id(0)
        i = pl.program_id(1)
        q = q_ref[...]
        do = do_ref[...]
        lane = lax.broadcasted_iota(jnp.int32, (1, 2 * HD), 1)
        sel0 = lane < HD
        qh = (jnp.where(sel0, q, jnp.zeros_like(q)) * scale, jnp.where(sel0, jnp.zeros_like(q), q) * scale)
        doh = (jnp.where(sel0, do, jnp.zeros_like(do)), jnp.where(sel0, jnp.zeros_like(do), do))
        ce = (cq_ref[:, 0:1] - l_ref[:, 0:1], cq_ref[:, HD:HD + 1] - l_ref[:, HD:HD + 1])
        dl = (dl_ref[:, 0:1], dl_ref[:, HD:HD + 1])
        row = i * tb + lax.broadcasted_iota(jnp.int32, (tb, 1), 0)
        dq_scr[...] = jnp.zeros(dq_scr.shape, F32)

        def tile(j, h, masked):
            start = pl.multiple_of(j * tb, tb)
            kb = k_ref[pl.ds(start, tb), :]
            vb = v_ref[pl.ds(start, tb), :]
            one = jnp.ones_like(kb)
            kh = jnp.where(sel0, kb, one) if h == 0 else jnp.where(sel0, one, kb)
            t = lax.dot_general(qh[h], kb, dn, preferred_element_type=F32) - ck_ref[0, j, h:h + 1, :]
            if masked:
                col = j * tb + lax.broadcasted_iota(jnp.int32, (1, tb), 1)
                mask = (col <= row) & (col >= FRONT)
                p = jnp.where(mask, jnp.exp(jnp.where(mask, t, NEG) + ce[h]), 0.0)
            else:
                p = jnp.exp(t + ce[h])
            dp = lax.dot_general(doh[h], vb, dn, preferred_element_type=F32)
            ds = p * (dp - dl[h])
            dq_scr[h] += jnp.dot(ds.astype(kb.dtype), kh, preferred_element_type=F32)

        def step(j, _):
            edge = (j == 0) | (j == i)
            for h in range(2):
                live = skip_ref[((2 * pr + h) * nb + i) * nb + j] == 0
                pl.when(live & edge)(functools.partial(tile, j, h, True))
                pl.when(live & jnp.logical_not(edge))(functools.partial(tile, j, h, False))
            return 0

        lax.fori_loop(jfirst_ref[pr * nb + i], i + 1, step, 0)
        dq_ref[...] = (jnp.where(sel0, dq_scr[0], dq_scr[1]) * scale).astype(dq_ref.dtype)
        dc_ref[0, :, 0:1] = dq_scr[0][:, HD:HD + 1]
        dc_ref[0, :, 1:2] = dq_scr[1][:, 0:1]

    blk = lambda off: pl.BlockSpec((tb, 2 * HD), lambda p, i, *_: (i, off + p))
    full = lambda off: pl.BlockSpec((T, 2 * HD), lambda p, i, *_: (0, off + p))
    rows = pl.BlockSpec((1, nb, 2, tb), lambda p, i, *_: (p, 0, 0, 0))
    return pl.pallas_call(
        body, name="fox_bwd_dq",
        grid_spec=pltpu.PrefetchScalarGridSpec(
            num_scalar_prefetch=3, grid=(npair, nb),
            in_specs=[blk(0), full(npair), full(2 * npair), blk(0), blk(0), rows, blk(0), blk(0)],
            out_specs=[blk(0), _pair_cols(tb)],
            scratch_shapes=[pltpu.VMEM((2, tb, 2 * HD), F32)]),
        out_shape=[jax.ShapeDtypeStruct((T, FW), _BF), jax.ShapeDtypeStruct((npair, T, 2), F32)],
        compiler_params=_cp("parallel", "arbitrary"))(*plan, qkv, qkv, qkv, do, c_cols, c_rows, lse, delta)


def _fox_bwd_dkv(qkv, do, c_cols, ce_rows, delta_rows, plan):
    T = qkv.shape[0]
    tb = _div(T, ATT_TILE)
    nb = T // tb
    npair = HEADS // 2
    scale = HD ** -0.5
    dn = (((1,), (1,)), ((), ()))

    def body(skip_ref, jfirst_ref, ilast_ref, q_ref, k_ref, v_ref, do_ref, ck_ref, ce_ref, dl_ref,
             dk_ref, dv_ref, dc_ref, dk_scr, dv_scr):
        pr = pl.program_id(0)
        jb = pl.program_id(1)
        k = k_ref[...]
        v = v_ref[...]
        lane = lax.broadcasted_iota(jnp.int32, (1, 2 * HD), 1)
        sel0 = lane < HD
        kh = (jnp.where(sel0, k, jnp.zeros_like(k)) * scale, jnp.where(sel0, jnp.zeros_like(k), k) * scale)
        vh = (jnp.where(sel0, v, jnp.zeros_like(v)), jnp.where(sel0, jnp.zeros_like(v), v))
        ck = (ck_ref[:, 0:1], ck_ref[:, HD:HD + 1])
        kidx = jb * tb + lax.broadcasted_iota(jnp.int32, (tb, 1), 0)
        dk_scr[...] = jnp.zeros(dk_scr.shape, F32)
        dv_scr[...] = jnp.zeros(dv_scr.shape, F32)

        def tile(i, h, masked):
            start = pl.multiple_of(i * tb, tb)
            qb = q_ref[pl.ds(start, tb), :]
            dob = do_ref[pl.ds(start, tb), :]
            one = jnp.ones_like(qb)
            qh = jnp.where(sel0, qb, one) if h == 0 else jnp.where(sel0, one, qb)
            t = lax.dot_general(kh[h], qb, dn, preferred_element_type=F32) - ck[h]
            ce = ce_ref[0, i, h:h + 1, :]
            if masked:
                qidx = i * tb + lax.broadcasted_iota(jnp.int32, (1, tb), 1)
                mask = (kidx <= qidx) & (kidx >= FRONT)
                pt = jnp.where(mask, jnp.exp(jnp.where(mask, t, NEG) + ce), 0.0)
            else:
                pt = jnp.exp(t + ce)
            dv_scr[h] += jnp.dot(pt.astype(dob.dtype), dob, preferred_element_type=F32)
            dpt = lax.dot_general(vh[h], dob, dn, preferred_element_type=F32)
            dst = pt * (dpt - dl_ref[0, i, h:h + 1, :])
            dk_scr[h] += jnp.dot(dst.astype(qb.dtype), qh, preferred_element_type=F32)

        def step(i, _):
            edge = (i == jb) | (jb == 0)
            for h in range(2):
                live = skip_ref[((2 * pr + h) * nb + i) * nb + jb] == 0
                pl.when(live & edge)(functools.partial(tile, i, h, True))
                pl.when(live & jnp.logical_not(edge))(functools.partial(tile, i, h, False))
            return 0

        lax.fori_loop(jb, ilast_ref[pr * nb + jb] + 1, step, 0)
        dk_ref[...] = (jnp.where(sel0, dk_scr[0], dk_scr[1]) * scale).astype(dk_ref.dtype)
        dv_ref[...] = jnp.where(sel0, dv_scr[0], dv_scr[1]).astype(dv_ref.dtype)
        dc_ref[0, :, 0:1] = -dk_scr[0][:, HD:HD + 1]
        dc_ref[0, :, 1:2] = -dk_scr[1][:, 0:1]

    blk = lambda off: pl.BlockSpec((tb, 2 * HD), lambda p, j, *_: (j, off + p))
    full = lambda off: pl.BlockSpec((T, 2 * HD), lambda p, j, *_: (0, off + p))
    rows = pl.BlockSpec((1, nb, 2, tb), lambda p, j, *_: (p, 0, 0, 0))
    return pl.pallas_call(
        body, name="fox_bwd_dkv",
        grid_spec=pltpu.PrefetchScalarGridSpec(
            num_scalar_prefetch=3, grid=(npair, nb),
            in_specs=[full(0), blk(npair), blk(2 * npair), full(0), blk(0), rows, rows],
            out_specs=[blk(0), blk(0), _pair_cols(tb)],
            scratch_shapes=[pltpu.VMEM((2, tb, 2 * HD), F32), pltpu.VMEM((2, tb, 2 * HD), F32)]),
        out_shape=[jax.ShapeDtypeStruct((T, FW), _BF)] * 2 + [jax.ShapeDtypeStruct((npair, T, 2), F32)],
        compiler_params=_cp("parallel", "arbitrary"))(*plan, qkv, qkv, qkv, do, c_cols, ce_rows, delta_rows)


def _head_dot(a, b, ones_blk, ones_col):
    T = a.shape[0]
    tb = _div(T, ROW_TILE, 8)

    def body(a_ref, b_ref, e_ref, c_ref, o_ref, o8_ref):
        prod = a_ref[...].astype(F32) * b_ref[...].astype(F32)
        o_ref[...] = jnp.dot(prod, e_ref[...], precision=HI, preferred_element_type=F32)
        o8_ref[...] = jnp.dot(prod, c_ref[...], precision=HI, preferred_element_type=F32)

    row = pl.BlockSpec((tb, FW), lambda i: (i, 0))
    return pl.pallas_call(
        body, name="head_dot", grid=(T // tb,),
        in_specs=[row, row, pl.BlockSpec((FW, FW), lambda i: (0, 0)), pl.BlockSpec((FW, SMALL_W), lambda i: (0, 0))],
        out_specs=[row, pl.BlockSpec((tb, SMALL_W), lambda i: (i, 0))],
        out_shape=[jax.ShapeDtypeStruct((T, FW), F32), jax.ShapeDtypeStruct((T, SMALL_W), F32)],
        compiler_params=_cp("parallel"))(a, b, ones_blk, ones_col)


CONV_RC = 32
CONV_TC = 256


def _stage_prev(scr, x_ref, h_ref, first, tb):
    scr[0:8, :] = jnp.where(first, 0.0, h_ref[...])
    scr[8:8 + tb, :] = x_ref[...]


def _windows_prev(scr, r, kw):
    x = scr[pl.ds(pl.multiple_of(r * CONV_RC, CONV_RC), CONV_RC + 8), :]
    return [x[8:] if k == kw - 1 else pltpu.roll(x, kw - 1 - k, 0)[8:] for k in range(kw)]


def _fold8(a):
    return a.reshape(CONV_RC // 8, 8, a.shape[-1]).sum(axis=0)


def _halo_prev(tb, tc, off=0):
    return pl.BlockSpec((8, tc), lambda j, i: (jnp.maximum(i * (tb // 8) - 1, 0), j + off))


def _gconv_fwd(rest, w):
    T = rest.shape[0]
    C = 3 * FW
    kw = w.shape[0]
    tb, tc = _div(T, 1280, CONV_RC), CONV_TC

    def body(x_ref, h_ref, w_ref, o_ref, scr):
        _stage_prev(scr, x_ref, h_ref, pl.program_id(1) == 0, tb)
        wv = w_ref[...]

        def chunk(r, _):
            win = _windows_prev(scr, r, kw)
            u = sum(wv[k:k + 1, :] * win[k] for k in range(kw))
            o_ref[pl.ds(pl.multiple_of(r * CONV_RC, CONV_RC), CONV_RC), :] = u * _sig(u)
            return 0

        lax.fori_loop(0, tb // CONV_RC, chunk, 0)

    return pl.pallas_call(
        body, name="gconv_fwd", grid=(C // tc, T // tb),
        in_specs=[pl.BlockSpec((tb, tc), lambda j, i: (i, j)), _halo_prev(tb, tc), pl.BlockSpec((kw, tc), lambda j, i: (0, j))],
        out_specs=pl.BlockSpec((tb, tc), lambda j, i: (i, j)), out_shape=jax.ShapeDtypeStruct((T, C), F32),
        scratch_shapes=[pltpu.VMEM((tb + 8, tc), F32)], compiler_params=_cp("parallel", "arbitrary"))(rest, rest, w)


def _gconv_bwd_du(rest, w, dy):
    T = rest.shape[0]
    C = 3 * FW
    kw = w.shape[0]
    tb, tc = _div(T, 1280, CONV_RC), CONV_TC

    def body(x_ref, h_ref, w_ref, dy_ref, du_ref, dw_ref, scr):
        i = pl.program_id(1)
        _stage_prev(scr, x_ref, h_ref, i == 0, tb)
        wv = w_ref[...]

        def chunk(r, acc):
            rows = pl.ds(pl.multiple_of(r * CONV_RC, CONV_RC), CONV_RC)
            win = _windows_prev(scr, r, kw)
            u = sum(wv[k:k + 1, :] * win[k] for k in range(kw))
            sg = _sig(u)
            du = dy_ref[rows, :] * sg * (1.0 + u * (1.0 - sg))
            du_ref[rows, :] = du
            return tuple(acc[k] + _fold8(du * win[k]) for k in range(kw))

        acc = lax.fori_loop(0, tb // CONV_RC, chunk, tuple(jnp.zeros((8, tc), F32) for _ in range(kw)))
        part = jnp.concatenate([jnp.sum(a, axis=0, keepdims=True) for a in acc], axis=0)

        @pl.when(i == 0)
        def _():
            dw_ref[...] = part

        @pl.when(i > 0)
        def _():
            dw_ref[...] += part

    blk = pl.BlockSpec((tb, tc), lambda j, i: (i, j))
    wsp = pl.BlockSpec((kw, tc), lambda j, i: (0, j))
    return pl.pallas_call(
        body, name="gconv_bwd_du", grid=(C // tc, T // tb), in_specs=[blk, _halo_prev(tb, tc), wsp, blk],
        out_specs=[blk, wsp], out_shape=[jax.ShapeDtypeStruct((T, C), F32), jax.ShapeDtypeStruct((kw, C), F32)],
        scratch_shapes=[pltpu.VMEM((tb + 8, tc), F32)], compiler_params=_cp("parallel", "arbitrary"))(rest, rest, w, dy)


def _conv_bwd_dx(du, w, out_dtype, name):
    T, C = du.shape
    kw = w.shape[0]
    tb = _div(T, 1280, CONV_RC)
    tc = CONV_TC
    nb = T // tb

    def body(x_ref, h_ref, w_ref, o_ref, scr):
        scr[0:tb, :] = x_ref[...]
        scr[tb:tb + 8, :] = jnp.where(pl.program_id(1) == nb - 1, 0.0, h_ref[...])
        wv = w_ref[...]

        def chunk(r, _):
            start = pl.multiple_of(r * CONV_RC, CONV_RC)
            x = scr[pl.ds(start, CONV_RC + 8), :]
            acc = wv[kw - 1:kw, :] * x[:CONV_RC]
            for k in range(kw - 1):
                acc = acc + wv[k:k + 1, :] * pltpu.roll(x, CONV_RC + 8 - (kw - 1 - k), 0)[:CONV_RC]
            o_ref[pl.ds(start, CONV_RC), :] = acc.astype(o_ref.dtype)
            return 0

        lax.fori_loop(0, tb // CONV_RC, chunk, 0)

    halo = pl.BlockSpec((8, tc), lambda j, i: (jnp.minimum((i + 1) * (tb // 8), T // 8 - 1), j))
    return pl.pallas_call(
        body, name=name, grid=(C // tc, nb),
        in_specs=[pl.BlockSpec((tb, tc), lambda j, i: (i, j)), halo, pl.BlockSpec((kw, tc), lambda j, i: (0, j))],
        out_specs=pl.BlockSpec((tb, tc), lambda j, i: (i, j)), out_shape=jax.ShapeDtypeStruct((T, C), out_dtype),
        scratch_shapes=[pltpu.VMEM((tb + 8, tc), F32)], compiler_params=_cp("parallel", "arbitrary"))(du, du, w)


def _glu_fwd(up, w, b):
    T = up.shape[0]
    kw = w.shape[0]
    tb, tc = _div(T, 1280, CONV_RC), CONV_TC
    nc = DFF // tc

    def body(xg, hg, xu, hu, wg, wu, bg, bu, o_ref, sg, su):
        first = pl.program_id(1) == 0
        _stage_prev(sg, xg, hg, first, tb)
        _stage_prev(su, xu, hu, first, tb)
        wgv, wuv, bgv, buv = wg[...], wu[...], bg[...], bu[...]

        def chunk(r, _):
            wing, winu = _windows_prev(sg, r, kw), _windows_prev(su, r, kw)
            ug = bgv + sum(wgv[k:k + 1, :] * wing[k] for k in range(kw))
            uu = buv + sum(wuv[k:k + 1, :] * winu[k] for k in range(kw))
            o_ref[pl.ds(pl.multiple_of(r * CONV_RC, CONV_RC), CONV_RC), :] = (ug * _sig(ug) * uu).astype(o_ref.dtype)
            return 0

        lax.fori_loop(0, tb // CONV_RC, chunk, 0)

    blk = lambda off: pl.BlockSpec((tb, tc), lambda j, i: (i, j + off))
    wsp = lambda off: pl.BlockSpec((kw, tc), lambda j, i: (0, j + off))
    bsp = lambda off: pl.BlockSpec((1, tc), lambda j, i: (0, j + off))
    return pl.pallas_call(
        body, name="glu_fwd", grid=(nc, T // tb),
        in_specs=[blk(0), _halo_prev(tb, tc), blk(nc), _halo_prev(tb, tc, nc), wsp(0), wsp(nc), bsp(0), bsp(nc)],
        out_specs=blk(0), out_shape=jax.ShapeDtypeStruct((T, DFF), _BF),
        scratch_shapes=[pltpu.VMEM((tb + 8, tc), F32)] * 2, compiler_params=_cp("parallel", "arbitrary"))(
            up, up, up, up, w, w, b, b)


def _glu_bwd_du(up, w, b, df):
    T = up.shape[0]
    kw = w.shape[0]
    tb, tc = _div(T, 1280, CONV_RC), CONV_TC
    nc = DFF // tc

    def body(xg, hg, xu, hu, wg, wu, bg, bu, df_ref, dug_ref, duu_ref, dwg_ref, dwu_ref, dbg_ref, dbu_ref, sg, su):
        i = pl.program_id(1)
        _stage_prev(sg, xg, hg, i == 0, tb)
        _stage_prev(su, xu, hu, i == 0, tb)
        wgv, wuv, bgv, buv = wg[...], wu[...], bg[...], bu[...]

        def chunk(r, acc):
            rows = pl.ds(pl.multiple_of(r * CONV_RC, CONV_RC), CONV_RC)
            wing, winu = _windows_prev(sg, r, kw), _windows_prev(su, r, kw)
            ug = bgv + sum(wgv[k:k + 1, :] * wing[k] for k in range(kw))
            uu = buv + sum(wuv[k:k + 1, :] * winu[k] for k in range(kw))
            s = _sig(ug)
            df = df_ref[rows, :]
            dug = df * uu * s * (1.0 + ug * (1.0 - s))
            duu = df * ug * s
            dug_ref[rows, :] = dug
            duu_ref[rows, :] = duu
            new = [dug * wing[k] for k in range(kw)] + [duu * winu[k] for k in range(kw)] + [dug, duu]
            return tuple(a + _fold8(v) for a, v in zip(acc, new))

        acc = lax.fori_loop(0, tb // CONV_RC, chunk, tuple(jnp.zeros((8, tc), F32) for _ in range(2 * kw + 2)))
        col = [jnp.sum(a, axis=0, keepdims=True) for a in acc]
        parts = (jnp.concatenate(col[:kw], axis=0), jnp.concatenate(col[kw:2 * kw], axis=0), col[2 * kw], col[2 * kw + 1])
        accs = (dwg_ref, dwu_ref, dbg_ref, dbu_ref)

        @pl.when(i == 0)
        def _():
            for r, p in zip(accs, parts):
                r[...] = p

        @pl.when(i > 0)
        def _():
            for r, p in zip(accs, parts):
                r[...] += p

    blk = lambda off: pl.BlockSpec((tb, tc), lambda j, i: (i, j + off))
    wsp = lambda off: pl.BlockSpec((kw, tc), lambda j, i: (0, j + off))
    bsp = lambda off: pl.BlockSpec((1, tc), lambda j, i: (0, j + off))
    return pl.pallas_call(
        body, name="glu_bwd_du", grid=(nc, T // tb),
        in_specs=[blk(0), _halo_prev(tb, tc), blk(nc), _halo_prev(tb, tc, nc), wsp(0), wsp(nc), bsp(0), bsp(nc), blk(0)],
        out_specs=[blk(0), blk(0), wsp(0), wsp(0), bsp(0), bsp(0)],
        out_shape=[jax.ShapeDtypeStruct((T, DFF), F32)] * 2 + [jax.ShapeDtypeStruct((kw, DFF), F32)] * 2
        + [jax.ShapeDtypeStruct((1, DFF), F32)] * 2,
        scratch_shapes=[pltpu.VMEM((tb + 8, tc), F32)] * 2, compiler_params=_cp("parallel", "arbitrary"))(
            up, up, up, up, w, w, b, b, df)


def _mix_fwd(rest, gate_bias, y_fox, y_gdn):
    T = rest.shape[0]
    tb, tc = _div(T, ROW_TILE, 8), 512
    nc = D // tc
    og = OFF_GATES // tc

    def body(gf, gg, bf, bg, yf, yg, o_ref):
        o_ref[...] = (_sig(gf[...] + bf[...]) * yf[...] + _sig(gg[...] + bg[...]) * yg[...]).astype(o_ref.dtype)

    blk = lambda off: pl.BlockSpec((tb, tc), lambda i, j: (i, j + off))
    bsp = lambda off: pl.BlockSpec((1, tc), lambda i, j: (0, j + off))
    return pl.pallas_call(
        body, name="mix_fwd", grid=(T // tb, nc),
        in_specs=[blk(og), blk(og + nc), bsp(0), bsp(nc), blk(0), blk(0)], out_specs=blk(0),
        out_shape=jax.ShapeDtypeStruct((T, D), _BF), compiler_params=_cp("parallel", "parallel"))(
            rest, rest, gate_bias, gate_bias, y_fox, y_gdn)


def _mix_bwd(rest, gate_bias, y_fox, y_gdn, dmix):
    T = rest.shape[0]
    tb, tc = _div(T, ROW_TILE, 8), 512
    nc = D // tc
    og = OFF_GATES // tc

    def body(gf, gg, bf, bg, yf, yg, dm, dyf, dyg, dgf, dgg, dbf, dbg):
        i = pl.program_id(1)
        d = dm[...]
        sf = _sig(gf[...] + bf[...])
        sg = _sig(gg[...] + bg[...])
        dyf[...] = (d * sf).astype(dyf.dtype)
        dyg[...] = (d * sg).astype(dyg.dtype)
        a = d * yf[...] * sf * (1.0 - sf)
        b = d * yg[...] * sg * (1.0 - sg)
        dgf[...] = a.astype(dgf.dtype)
        dgg[...] = b.astype(dgg.dtype)
        pa = jnp.sum(a, axis=0, keepdims=True)
        pb = jnp.sum(b, axis=0, keepdims=True)

        @pl.when(i == 0)
        def _():
            dbf[...] = pa
            dbg[...] = pb

        @pl.when(i > 0)
        def _():
            dbf[...] += pa
            dbg[...] += pb

    blk = lambda off: pl.BlockSpec((tb, tc), lambda j, i: (i, j + off))
    bsp = lambda off: pl.BlockSpec((1, tc), lambda j, i: (0, j + off))
    return pl.pallas_call(
        body, name="mix_bwd", grid=(nc, T // tb),
        in_specs=[blk(og), blk(og + nc), bsp(0), bsp(nc), blk(0), blk(0), blk(0)],
        out_specs=[blk(0), blk(0), blk(0), blk(0), bsp(0), bsp(0)],
        out_shape=[jax.ShapeDtypeStruct((T, D), _BF)] * 4 + [jax.ShapeDtypeStruct((1, D), F32)] * 2,
        compiler_params=_cp("parallel", "arbitrary"))(rest, rest, gate_bias, gate_bias, y_fox, y_gdn, dmix)


def _loss_head(h2, w, target, n_valid):
    T = h2.shape[0]
    tb = _div(T, ROW_TILE, 8)

    def body(h_ref, w_ref, t_ref, dh_ref, loss_ref, dw_ref):
        i = pl.program_id(0)
        x = h_ref[...]
        r = lax.rsqrt(jnp.mean(x * x, axis=-1, keepdims=True) + EPS)
        xh = x * r
        row = i * tb + lax.broadcasted_iota(jnp.int32, (tb, 1), 0)
        valid = (row >= X0) & (row < X0 + n_valid)
        e = jnp.where(valid, xh * w_ref[...] - t_ref[...], 0.0)
        dy = e * (1.0 / D)
        g = dy * w_ref[...]
        dh_ref[...] = r * (g - xh * jnp.mean(xh * g, axis=-1, keepdims=True))
        lpart = 0.5 * jnp.sum(jnp.sum(e * e, axis=-1, keepdims=True) * (1.0 / D), axis=0, keepdims=True)
        wpart = jnp.sum(dy * xh, axis=0, keepdims=True)

        @pl.when(i == 0)
        def _():
            loss_ref[...] = lpart
            dw_ref[...] = wpart

        @pl.when(i > 0)
        def _():
            loss_ref[...] += lpart
            dw_ref[...] += wpart

    row = pl.BlockSpec((tb, D), lambda i: (i, 0))
    one = pl.BlockSpec((1, D), lambda i: (0, 0))
    return pl.pallas_call(
        body, name="loss_head", grid=(T // tb,), in_specs=[row, one, row],
        out_specs=[row, pl.BlockSpec((1, 1), lambda i: (0, 0)), one],
        out_shape=[jax.ShapeDtypeStruct((T, D), F32), jax.ShapeDtypeStruct((1, 1), F32), jax.ShapeDtypeStruct((1, D), F32)],
        compiler_params=_cp("arbitrary"))(h2, w, target)


def _bd_lo(a, b, ca, cb):
    return lax.dot_general(a.astype(_BF), b.astype(_BF), (((ca,), (cb,)), ((0,), (0,))), preferred_element_type=F32)


def _split2(a):
    hi = a.astype(_BF)
    return hi, (a - hi.astype(F32)).astype(_BF)


def _bd_hi(a, b, ca, cb, exact_a=False):
    dn = (((ca,), (cb,)), ((0,), (0,)))
    dot = lambda x, y: lax.dot_general(x, y, dn, preferred_element_type=F32)
    bh, bl = _split2(b)
    if exact_a:
        ah = a.astype(_BF)
        return dot(ah, bh) + dot(ah, bl)
    ah, al = _split2(a)
    return dot(ah, bh) + (dot(ah, bl) + dot(al, bh))


def _row_to_col(x):
    eye = lax.broadcasted_iota(jnp.int32, (1, CH, CH), 1) == lax.broadcasted_iota(jnp.int32, (1, CH, CH), 2)
    return jnp.sum(jnp.where(eye, jnp.broadcast_to(x, (HEADS, CH, CH)), 0.0), axis=2, keepdims=True)


def _col_to_row(x):
    eye = lax.broadcasted_iota(jnp.int32, (1, CH, CH), 1) == lax.broadcasted_iota(jnp.int32, (1, CH, CH), 2)
    return jnp.sum(jnp.where(eye, jnp.broadcast_to(x, (HEADS, CH, CH)), 0.0), axis=1, keepdims=True)


def _gdn_chunk(q, k, v, bpre, apre, alog, dtb):
    H = HEADS
    r = lax.broadcasted_iota(jnp.int32, (1, CH, CH), 1)
    c = lax.broadcasted_iota(jnp.int32, (1, CH, CH), 2)
    tril, strict = r >= c, r > c
    lb = jnp.broadcast_to(tril.astype(F32), (H, CH, CH))
    rq = lax.rsqrt(jnp.sum(q * q, axis=-1, keepdims=True) + EPS)
    rk = lax.rsqrt(jnp.sum(k * k, axis=-1, keepdims=True) + EPS)
    qh = q * rq
    qn = qh * (HD ** -0.5)
    kn = k * rk
    beta = _sig(bpre)
    x = apre + dtb
    ea = jnp.exp(alog)
    g = -ea * jax.nn.softplus(x)
    gb = jnp.broadcast_to(g, (H, CH, HD))
    gc = _bd_hi(lb, gb, 2, 1, True)
    dm = gc - _col_to_row(gc[:, :, 0:1])
    decay = jnp.where(tril, jnp.exp(jnp.where(tril, dm, 0.0)), 0.0)
    eg = jnp.exp(gc)
    gl = gc[:, CH - 1:CH, :]
    egl = jnp.exp(gl - gc)
    cd = jnp.exp(gl)
    kb = kn * beta
    vb = v * beta
    both = _bd_lo(_rows2(kb, qn), kn, 2, 2)
    kk, qk = both[:, :CH], both[:, CH:]
    pw = -jnp.where(strict, kk * decay, 0.0)
    tm = jnp.where(r == c, 1.0, 0.0) + pw
    pw = _bd_hi(pw, pw, 2, 1)
    for it in range(4):
        mul = _bd_hi if it < 2 else _bd_lo
        both = mul(_rows2(pw, tm), pw, 2, 1)
        pw, tm = both[:, :CH], tm + both[:, CH:]
    tm = tm + _bd_lo(tm, pw, 2, 1)
    kbg = kb * eg
    sol = _bd_hi(tm, _lanes2(vb, kbg), 2, 1)
    value, kcd = sol[:, :, :HD], sol[:, :, HD:]
    attn = jnp.where(tril, qk * decay, 0.0)
    return dict(tril=tril, strict=strict, lb=lb, rq=rq, rk=rk, qh=qh, qn=qn, kn=kn, beta=beta, x=x, ea=ea, g=g,
                decay=decay, eg=eg, egl=egl, cd=cd, kb=kb, vb=vb, kk=kk, tm=tm, kbg=kbg, value=value, kcd=kcd, qk=qk,
                attn=attn, qd=qn * eg, kt=kn * egl, sol=sol)


def _rows2(a, b):
    return jnp.concatenate([a, b], axis=1)


def _lanes2(a, b):
    return jnp.concatenate([a, b], axis=2)


GDN_CPS = 2


def _gdn_specs(T, rev):
    G = GDN_CPS
    ns = T // (G * CH)
    pos = (lambda n: ns - 1 - n) if rev else (lambda n: n)
    mat = pl.BlockSpec((HEADS, G * CH, HD), lambda n: (0, pos(n), 0))
    col = pl.BlockSpec((G, HEADS, 1, CH), lambda n: (pos(n), 0, 0, 0))
    sca = pl.BlockSpec((HEADS, 1, 1), lambda n: (0, 0, 0))
    nw = pl.BlockSpec((1, 1, HD), lambda n: (0, 0, 0))
    st = pl.BlockSpec((G, HEADS, HD, HD), lambda n: (pos(n), 0, 0, 0))
    tok = lambda width, off: pl.BlockSpec((G * CH, width), lambda n: (pos(n), off))
    return ns, mat, col, sca, nw, st, tok


def _split_heads(x):
    return [jnp.stack([x[:, (g * HEADS + h) * HD:(g * HEADS + h + 1) * HD] for h in range(HEADS)])
            for g in range(x.shape[1] // FW)]


def _store_heads(ref, rows, g, val):
    for h in range(HEADS):
        ref[rows, (g * HEADS + h) * HD:(g * HEADS + h + 1) * HD] = val[h]


def _gdn_fwd(conv, rest, bpre, apre, alog, dtb, nw, gather=()):
    T = conv.shape[0]
    ns, mat, col, sca, nws, st, tok = _gdn_specs(T, False)
    rows = [slice(g * CH, (g + 1) * CH) for g in range(GDN_CPS)]
    ng = len(gather)

    def body(*refs):
        c_ref, z_ref, b_ref, a_ref, al_ref, dt_ref, nw_ref = refs[:7]
        o_ref, og_ref, st_ref = refs[7 + ng:10 + ng]
        s_scr = refs[10 + 2 * ng]
        comm = (refs[7:7 + ng], refs[10 + ng:10 + 2 * ng], refs[11 + 2 * ng:])

        @pl.when(pl.program_id(0) == 0)
        def _():
            s_scr[...] = jnp.zeros_like(s_scr)
            if ng:
                _gather_phase(*comm, True)

        locs = [_gdn_chunk(*_split_heads(c_ref[rows[g], :]), _row_to_col(b_ref[g]), _row_to_col(a_ref[g]), al_ref[...],
                           dt_ref[...]) for g in range(GDN_CPS)]
        s = s_scr[...]
        for g, L in enumerate(locs):
            st_ref[g] = s
            both = _bd_lo(_rows2(L["kcd"], L["qd"]), s, 2, 1)
            v_new = L["value"] - both[:, :CH]
            o = both[:, CH:] + _bd_lo(L["attn"], v_new, 2, 1)
            s = s * L["cd"] + _bd_lo(L["kt"], v_new, 1, 1)
            o_ref[:, rows[g], :] = o
            zz, = _split_heads(z_ref[rows[g], :])
            rs = lax.rsqrt(jnp.mean(o * o, axis=-1, keepdims=True) + EPS)
            _store_heads(og_ref, rows[g], 0, o * rs * nw_ref[...] * zz * _sig(zz))
        s_scr[...] = s
        if ng:
            pl.when(pl.program_id(0) == ns - 1)(functools.partial(_gather_phase, *comm, False))

    return pl.pallas_call(
        body, name="gdn_fwd", grid=(ns,),
        in_specs=[tok(3 * FW, 0), tok(FW, OFF_Z // FW), col, col, sca, sca, nws] + [HBM_SPEC] * ng,
        out_specs=[mat, tok(FW, 0), st] + [HBM_SPEC] * ng,
        out_shape=[jax.ShapeDtypeStruct((HEADS, T, HD), F32), jax.ShapeDtypeStruct((T, FW), F32),
                   jax.ShapeDtypeStruct((T // CH, HEADS, HD, HD), F32)] + _gather_shapes(gather),
        scratch_shapes=[pltpu.VMEM((HEADS, HD, HD), F32)] + (_gather_sems(ng) if ng else []),
        compiler_params=_cp("arbitrary"))(conv, rest, bpre, apre, alog, dtb, nw, *gather)


def _gdn_bwd(conv, rest, bpre, apre, alog, dtb, nw, states, o, dog, scatter=()):
    T = conv.shape[0]
    ns, mat, col, sca, nws, st, tok = _gdn_specs(T, True)
    rows = [slice(g * CH, (g + 1) * CH) for g in range(GDN_CPS)]
    ng = len(scatter)

    def body(*refs):
        c_ref, z_ref, b_ref, a_ref, al_ref, dt_ref, nw_ref, st_ref, o_ref, dog_ref = refs[:10]
        dc_ref, dz_ref, db_ref, da_ref, dal_ref, ddt_ref, dnw_ref = refs[10 + ng:17 + ng]
        ds_scr = refs[17 + 2 * ng]
        comm = (refs[10:10 + ng], refs[17 + ng:17 + 2 * ng], refs[18 + 2 * ng:])

        @pl.when(pl.program_id(0) == 0)
        def _():
            ds_scr[...] = jnp.zeros_like(ds_scr)
            dal_ref[...] = jnp.zeros_like(dal_ref)
            ddt_ref[...] = jnp.zeros_like(ddt_ref)
            dnw_ref[...] = jnp.zeros_like(dnw_ref)
            if ng:
                _scatter_phase(*comm, True)

        splits = [_split_heads(c_ref[rows[g], :]) for g in range(GDN_CPS)]
        locs = [_gdn_chunk(*splits[g], _row_to_col(b_ref[g]), _row_to_col(a_ref[g]), al_ref[...], dt_ref[...])
                for g in range(GDN_CPS)]
        dsn = ds_scr[...]
        acc_al, acc_dt, acc_nw = 0.0, 0.0, 0.0
        for g in reversed(range(GDN_CPS)):
            L, vv = locs[g], splits[g][2]
            tril, strict, lb = L["tril"], L["strict"], L["lb"]
            qn, kn, kb, beta, decay, eg, egl, cd = L["qn"], L["kn"], L["kb"], L["beta"], L["decay"], L["eg"], L["egl"], L["cd"]
            value, kcd, attn, qd, kt, tm = L["value"], L["kcd"], L["attn"], L["qd"], L["kt"], L["tm"]
            s = st_ref[g]
            v_new = value - _bd_lo(kcd, s, 2, 1)
            oo = o_ref[:, rows[g], :]
            zz, = _split_heads(z_ref[rows[g], :])
            dog_, = _split_heads(dog_ref[rows[g], :])
            sz = _sig(zz)
            rs = lax.rsqrt(jnp.mean(oo * oo, axis=-1, keepdims=True) + EPS)
            oh = oo * rs
            _store_heads(dz_ref, rows[g], 0, dog_ * oh * nw_ref[...] * sz * (1.0 + zz * (1.0 - sz)))
            don = dog_ * zz * sz
            gdy = don * nw_ref[...]
            do = rs * (gdy - oh * jnp.mean(oh * gdy, axis=-1, keepdims=True))
            acc_nw = acc_nw + jnp.sum(don * oh, axis=(0, 1), keepdims=True)
            d_vnew = _bd_lo(attn, do, 1, 1) + _bd_lo(kt, dsn, 2, 1)
            both = _bd_lo(_rows2(do, d_vnew), s, 2, 2)
            d_qd, d_kcd = both[:, :CH], -both[:, CH:]
            d_attn = jnp.where(tril, _bd_lo(do, v_new, 2, 2), 0.0)
            d_kt = _bd_lo(v_new, dsn, 2, 2)
            d_cd = jnp.sum(s * dsn, axis=(1, 2), keepdims=True)
            dsn_next = cd * dsn + _bd_lo(_rows2(qd, kcd), _rows2(do, -d_vnew), 1, 1)
            dsol = _bd_hi(tm, _lanes2(d_vnew, d_kcd), 1, 1)
            d_vb, d_kbg = dsol[:, :, :HD], dsol[:, :, HD:]
            da = -jnp.where(strict, _bd_lo(dsol, L["sol"], 2, 2), 0.0)
            dkk = da * decay
            dqk = d_attn * decay
            d_decay = da * L["kk"] + d_attn * L["qk"]
            both = _bd_lo(_rows2(dkk, dqk), kn, 2, 1)
            d_kb = both[:, :CH] + d_kbg * eg
            d_qn = both[:, CH:] + d_qd * eg
            d_kn = _bd_lo(_rows2(dkk, dqk), _rows2(kb, qn), 1, 1) + d_kt * egl
            dd = d_decay * decay
            d_gc = jnp.sum(dd + d_qd * qd + d_kbg * L["kbg"] - d_kt * kt, axis=-1, keepdims=True) \
                - _row_to_col(jnp.sum(dd, axis=1, keepdims=True))
            d_gl = jnp.sum(d_kt * kt, axis=(1, 2), keepdims=True) + d_cd * cd[:, :, 0:1]
            last = lax.broadcasted_iota(jnp.int32, (1, CH, 1), 1) == CH - 1
            d_gc = d_gc + jnp.where(last, d_gl, 0.0)
            dg = _bd_hi(lb, jnp.broadcast_to(d_gc, (HEADS, CH, HD)), 1, 1, True)[:, :, 0:1]
            d_apre = -dg * L["ea"] * _sig(L["x"])
            da_ref[g] = _col_to_row(d_apre)
            acc_al = acc_al + jnp.sum(dg * L["g"], axis=1, keepdims=True)
            acc_dt = acc_dt + jnp.sum(d_apre, axis=1, keepdims=True)
            d_beta = jnp.sum(d_kb * kn + d_vb * vv, axis=-1, keepdims=True)
            db_ref[g] = _col_to_row(d_beta * beta * (1.0 - beta))
            d_kn = d_kn + d_kb * beta
            qh = L["qh"]
            _store_heads(dc_ref, rows[g], 0, (HD ** -0.5) * L["rq"] * (d_qn - qh * jnp.sum(qh * d_qn, axis=-1, keepdims=True)))
            _store_heads(dc_ref, rows[g], 1, L["rk"] * (d_kn - kn * jnp.sum(kn * d_kn, axis=-1, keepdims=True)))
            _store_heads(dc_ref, rows[g], 2, d_vb * beta)
            dsn = dsn_next
        ds_scr[...] = dsn
        dal_ref[...] += acc_al
        ddt_ref[...] += acc_dt
        dnw_ref[...] += acc_nw
        if ng:
            pl.when(pl.program_id(0) == ns - 1)(functools.partial(_scatter_phase, *comm, False))

    c3 = jax.ShapeDtypeStruct((T // CH, HEADS, 1, CH), F32)
    s3 = jax.ShapeDtypeStruct((HEADS, 1, 1), F32)
    return pl.pallas_call(
        body, name="gdn_bwd", grid=(ns,),
        in_specs=[tok(3 * FW, 0), tok(FW, OFF_Z // FW), col, col, sca, sca, nws, st, mat, tok(FW, 0)] + [HBM_SPEC] * ng,
        out_specs=[tok(3 * FW, 0), tok(FW, 0), col, col, sca, sca, nws] + [HBM_SPEC] * ng,
        out_shape=[jax.ShapeDtypeStruct((T, 3 * FW), F32), jax.ShapeDtypeStruct((T, FW), F32), c3, c3, s3, s3,
                   jax.ShapeDtypeStruct((1, 1, HD), F32)] + _scatter_shapes(scatter),
        scratch_shapes=[pltpu.VMEM((HEADS, HD, HD), F32)] + (_scatter_sems(ng) if ng else []),
        compiler_params=_cp("arbitrary"))(conv, rest, bpre, apre, alog, dtb, nw, states, o, dog, *scatter)


LATE_KEYS = {"w_branch_fox": "w_bfox", "w_branch_gdn": "w_bgdn", "w_out": "w_out", "ffn_w_up": "w_up", "ffn_w_down": "w_down"}


def _local_step(x, target, w, late=None):
    seq = x.shape[0]
    T = -(-(X0 + seq) // T_ALIGN) * T_ALIGN
    back = T - X0 - seq
    tb_att = _div(T, ATT_TILE)
    h0 = jnp.concatenate([jnp.zeros((FRONT, D), F32), w["meta"], x, jnp.zeros((back, D), F32)], axis=0)
    tgt = jnp.concatenate([jnp.zeros((X0, D), F32), target, jnp.zeros((back, D), F32)], axis=0)
    row = lambda v: v.reshape(1, -1)
    nmix, nffn, nfin = row(w["norm_mix"]), row(w["norm_ffn"]), row(w["norm_final"])
    gate_b = row(w["gate_bias"])
    fconv_b = row(w["ffn_conv_b"])
    bias128 = jnp.zeros((1, SMALL_W), F32).at[0, :HEADS].set(w["fgt_bias"])
    alog = w["a_log"].reshape(HEADS, 1, 1)
    dtb = w["dt_bias"].reshape(HEADS, 1, 1)
    gnw = w["gdn_norm"].reshape(1, 1, HD)

    a = _rms_fwd(h0, nmix, "rms_mix")
    pf = _mm(a, w["w_fox"], out_dtype=_BF, name="proj_fox")
    rest = _mm(a, w["w_rest"], name="proj_rest")
    small = rest[:, OFF_SMALL:]
    c8 = _fgate_fwd(small, bias128)[:, :HEADS]
    c_cols, c_rows = _att_cols(c8), _att_rows(c8, tb_att)
    ones_blk = jnp.asarray(np.kron(np.eye(HEADS, dtype=np.float32), np.ones((HD, HD), np.float32)))
    plan = _att_plan(_att_stats(pf, c_cols, ones_blk))
    o_fox, lse, lse2 = _fox_fwd(pf, c_cols, c_rows, plan)
    conv = _gconv_fwd(rest, w["gdn_conv"])
    chunk_rows = lambda a8: a8.reshape(T // CH, CH, HEADS).transpose(0, 2, 1)[:, :, None, :]
    bpre = chunk_rows(small[:, HEADS:2 * HEADS])
    apre = chunk_rows(small[:, 2 * HEADS:3 * HEADS])
    if late is None:
        o_raw, og, states = _gdn_fwd(conv, rest, bpre, apre, alog, dtb, gnw)
    else:
        o_raw, og, states, *got = _gdn_fwd(conv, rest, bpre, apre, alog, dtb, gnw, gather=list(late.values()))
        w = dict(w, **{LATE_KEYS[n]: _from_chips(g, AXIS[n]) for n, g in zip(late, got)})
    w_up, w_down = w["w_up"], w["w_down"]
    y_fox = _mm(o_fox, w["w_bfox"], name="y_fox")
    y_gdn = _mm(og, w["w_bgdn"], name="y_gdn")
    mix = _mix_fwd(rest, gate_b, y_fox, y_gdn)
    h1 = _mm(mix, w["w_out"], resid=h0, name="out_proj")
    b = _rms_fwd(h1, nffn, "rms_ffn")
    up = _mm(b, w_up, name="ffn_up")
    f = _glu_fwd(up, w["ffn_conv"], fconv_b)
    h2 = _mm(f, w_down, resid=h1, name="ffn_down")
    dh2, loss, d_nfin = _loss_head(h2, nfin, tgt, seq)

    d_f = _mm(dh2, w_down, tb=True, name="d_f")
    g_down = _mm_t(f, dh2, "g_down")
    dug, duu, dwg, dwu, dbg, dbu = _glu_bwd_du(up, w["ffn_conv"], fconv_b, d_f)
    dxg = _conv_bwd_dx(dug, w["ffn_conv"][:, :DFF], _BF, "fconv_dx_gate")
    dxu = _conv_bwd_dx(duu, w["ffn_conv"][:, DFF:], _BF, "fconv_dx_up")
    d_b = _mm(dxg, w_up[:, :DFF], tb=True, name="d_b_gate")
    d_b = _mm(dxu, w_up[:, DFF:], tb=True, resid=d_b, name="d_b_up")
    g_up = jnp.concatenate([_mm_t(b, dxg, "g_up_gate"), _mm_t(b, dxu, "g_up_up")], axis=1)
    dh1, d_nffn = _rms_bwd(h1, nffn, d_b, dh2, "rms_ffn_bwd")

    dmix = _mm(dh1, w["w_out"], tb=True, name="d_mix")
    g_out = _mm_t(mix, dh1, "g_out")
    dyf, dyg, dgf, dgg, dgbf, dgbg = _mix_bwd(rest, gate_b, y_fox, y_gdn, dmix)
    do_fox = _mm(dyf, w["w_bfox"], tb=True, out_dtype=_BF, name="d_o_fox")
    g_bfox = _mm_t(o_fox, dyf, "g_bfox")
    d_og = _mm(dyg, w["w_bgdn"], tb=True, name="d_o_gdn")
    g_bgdn = _mm_t(og, dyg, "g_bgdn")

    ones_col = jnp.asarray(np.kron(np.eye(HEADS, SMALL_W, dtype=np.float32), np.ones((HD, 1), np.float32)))
    delta, delta8 = _head_dot(do_fox, o_fox, ones_blk, ones_col)
    dq, dcq = _fox_bwd_dq(pf, do_fox, c_cols, c_rows, lse, delta, plan)
    dk, dv, dck = _fox_bwd_dkv(pf, do_fox, c_cols, _att_rows(c8 - _from_pairs(lse2), tb_att),
                               _att_rows(delta8[:, :HEADS], tb_att), plan)
    dc = jnp.pad(_from_pairs(dcq + dck), ((0, 0), (0, SMALL_W - HEADS)))
    dfp, d_fb = _fgate_bwd(small, bias128, dc)

    if late is None:
        late_recv = None
        dconv, dz, dbp, dap, d_alog, d_dtb, d_gnw = _gdn_bwd(conv, rest, bpre, apre, alog, dtb, gnw, states, o_raw, d_og)
    else:
        ready = dict(w_branch_fox=g_bfox, w_branch_gdn=g_bgdn, w_out=g_out, ffn_w_up=g_up, ffn_w_down=g_down)
        send = [_by_chip(ready[n].astype(jnp.bfloat16), AXIS[n]) for n in late]
        dconv, dz, dbp, dap, d_alog, d_dtb, d_gnw, *late_recv = _gdn_bwd(
            conv, rest, bpre, apre, alog, dtb, gnw, states, o_raw, d_og, scatter=send)
    du_g, g_gconv = _gconv_bwd_du(rest, w["gdn_conv"], dconv)
    dgx = _conv_bwd_dx(du_g, w["gdn_conv"], _BF, "gconv_dx")
    token_rows = lambda a: a[:, :, 0, :].transpose(0, 2, 1).reshape(T, HEADS)
    dsmall = jnp.concatenate([dfp[:, :HEADS], token_rows(dbp), token_rows(dap),
                              jnp.zeros((T, SMALL_W - 3 * HEADS), F32)], axis=1)
    drest = jnp.concatenate([dgx, dz.astype(_BF), dgf, dgg, dsmall.astype(_BF)], axis=1)
    dfox = jnp.concatenate([dq, dk, dv], axis=1)
    d_a = _mm(dfox, w["w_fox"], tb=True, name="d_a_fox")
    d_a = _mm(drest, w["w_rest"], tb=True, resid=d_a, name="d_a_rest")
    g_fox = _mm_t(a, dfox, "g_w_fox")
    g_rest = _mm_t(a, drest, "g_w_rest")
    dh0, d_nmix = _rms_bwd(h0, nmix, d_a, dh1, "rms_mix_bwd")

    sm = lambda lo: g_rest[:, OFF_SMALL + lo:OFF_SMALL + lo + HEADS]
    g_w_in = jnp.concatenate([g_fox, sm(0), g_rest[:, :3 * FW], g_rest[:, OFF_Z:OFF_Z + FW], sm(HEADS), sm(2 * HEADS),
                              g_rest[:, OFF_GATES:OFF_GATES + 2 * D]], axis=1)
    grads = dict(
        meta_tokens=dh0[FRONT:X0], w_in=g_w_in, fgt_bias=d_fb[0, :HEADS], gdn_conv_w=g_gconv,
        gdn_a_log=d_alog.reshape(HEADS), gdn_dt_bias=d_dtb.reshape(HEADS), gdn_norm_w=d_gnw.reshape(HD),
        gate_bias=jnp.concatenate([dgbf, dgbg], axis=1).reshape(2 * D), w_branch_fox=g_bfox, w_branch_gdn=g_bgdn,
        w_out=g_out, norm_mix_w=d_nmix.reshape(D), norm_ffn_w=d_nffn.reshape(D), ffn_w_up=g_up,
        ffn_conv_w=jnp.concatenate([dwg, dwu], axis=1), ffn_conv_b=jnp.concatenate([dbg, dbu], axis=1).reshape(2 * DFF),
        ffn_w_down=g_down, norm_final_w=d_nfin.reshape(D), late_recv=late_recv)
    return loss, dh0[X0:X0 + seq], grads


N_CHIPS = 4
PACK_W = 1024
PACK_ROW_ALIGN = 32
BIG_EARLY = ("w_in",)
BIG_LATE = ("w_branch_fox", "w_branch_gdn", "w_out", "ffn_w_up", "ffn_w_down")
BIG = BIG_EARLY + BIG_LATE
WEIGHTS = (
    ("meta_tokens", (N_META, D), 1), ("w_in", (D, 3 * FW + HEADS + 4 * FW + 2 * HEADS + 2 * D), 1), ("fgt_bias", (1, HEADS), None),
    ("gdn_conv_w", (4, 3 * FW), 1), ("gdn_a_log", (1, HEADS), None), ("gdn_dt_bias", (1, HEADS), None),
    ("gdn_norm_w", (1, HD), None), ("gate_bias", (1, 2 * D), None), ("w_branch_fox", (FW, D), 1),
    ("w_branch_gdn", (FW, D), 1), ("w_out", (D, D), 0), ("norm_mix_w", (1, D), None), ("norm_ffn_w", (1, D), None),
    ("ffn_w_up", (D, 2 * DFF), 1), ("ffn_conv_w", (3, 2 * DFF), 1), ("ffn_conv_b", (1, 2 * DFF), None),
    ("ffn_w_down", (DFF, D), 0), ("norm_final_w", (1, D), None))
SPLIT_F32 = ("meta_tokens", "gdn_conv_w", "ffn_conv_w")


def _shard_shape(shape, axis):
    if axis is None:
        return shape
    return tuple(s // N_CHIPS if a == axis else s for a, s in enumerate(shape))


def _shard_of(full, axis, q):
    if axis is None:
        return full
    n = full.shape[axis] // N_CHIPS
    return lax.slice_in_dim(full, q * n, (q + 1) * n, axis=axis)


def _pack_rows(n_elems):
    rows = -(-n_elems // PACK_W)
    return -(-rows // PACK_ROW_ALIGN) * PACK_ROW_ALIGN


def _pack(pieces, dtype):
    flat = jnp.concatenate([p.reshape(-1).astype(dtype) for p in pieces])
    rows = _pack_rows(flat.shape[0])
    return jnp.pad(flat, (0, rows * PACK_W - flat.shape[0])).reshape(rows, PACK_W)


def _unpack(slab, shapes):
    flat = slab.reshape(-1)
    out, off = [], 0
    for s in shapes:
        n = int(np.prod(s))
        out.append(flat[off:off + n].reshape(s))
        off += n
    return out


HBM_SPEC = pl.BlockSpec(memory_space=pltpu.HBM)
MESH_ID = pl.DeviceIdType.MESH


def _scatter_chips(srcs, name):
    n = len(srcs)

    def body(*refs):
        _scatter_phase(refs[:n], refs[n:2 * n], refs[2 * n:], True)
        _scatter_phase(refs[:n], refs[n:2 * n], refs[2 * n:], False)

    return pl.pallas_call(body, name=name, in_specs=[HBM_SPEC] * n, out_specs=[HBM_SPEC] * n,
                          out_shape=_scatter_shapes(srcs), scratch_shapes=_scatter_sems(n))(*srcs)


def _scatter_shapes(srcs):
    return [jax.ShapeDtypeStruct(s.shape, s.dtype) for s in srcs]


def _scatter_sems(n):
    return [pltpu.SemaphoreType.DMA((3 * n,)), pltpu.SemaphoreType.DMA((3 * n,)), pltpu.SemaphoreType.DMA((n,))]


def _scatter_phase(src_refs, out_refs, sems, issue):
    n = len(src_refs)
    send_sems, recv_sems, local_sems = sems
    x, y, c = lax.axis_index("x"), lax.axis_index("y"), lax.axis_index("c")
    q = 2 * x + y
    peers = [(1 - x, y), (x, 1 - y), (1 - x, 1 - y)]

    def remote(a, k, src_slot, dst_slot):
        px, py = peers[k]
        return pltpu.make_async_remote_copy(
            src_ref=src_refs[a].at[src_slot], dst_ref=out_refs[a].at[dst_slot], send_sem=send_sems.at[3 * a + k],
            recv_sem=recv_sems.at[3 * a + k], device_id=(px, py, c), device_id_type=MESH_ID)

    mine = [pltpu.make_async_copy(src_refs[a].at[q], out_refs[a].at[q], local_sems.at[a]) for a in range(n)]
    sends = [remote(a, k, 2 * px + py, q) for a in range(n) for k, (px, py) in enumerate(peers)]
    if issue:
        for cp in mine + sends:
            cp.start()
        return
    for a in range(n):
        for k, (px, py) in enumerate(peers):
            remote(a, k, 0, 2 * px + py).wait_recv()
    for cp in sends:
        cp.wait_send()
    for cp in mine:
        cp.wait()


def _gather_chips(srcs, name):
    n = len(srcs)

    def body(*refs):
        _gather_phase(refs[:n], refs[n:2 * n], refs[2 * n:], True)
        _gather_phase(refs[:n], refs[n:2 * n], refs[2 * n:], False)

    return pl.pallas_call(body, name=name, in_specs=[HBM_SPEC] * n, out_specs=[HBM_SPEC] * n,
                          out_shape=_gather_shapes(srcs), scratch_shapes=_gather_sems(n))(*srcs)


def _gather_shapes(srcs):
    return [jax.ShapeDtypeStruct((N_CHIPS,) + s.shape, s.dtype) for s in srcs]


def _gather_sems(n):
    return [pltpu.SemaphoreType.DMA((3 * n,))] * 4 + [pltpu.SemaphoreType.DMA((n,))]


def _gather_phase(src_refs, out_refs, sems, issue):
    n = len(src_refs)
    ici_send, ici_recv, d2d_send, d2d_recv, local_sems = sems
    x, y, c = lax.axis_index("x"), lax.axis_index("y"), lax.axis_index("c")
    q = 2 * x + y
    peers = [(1 - x, y), (x, 1 - y), (1 - x, 1 - y)]

    def half(a, which):
        r = src_refs[a].shape[0] // 2
        return pl.ds(which * r, r)

    def ici(a, k, slot):
        px, py = peers[k]
        return pltpu.make_async_remote_copy(
            src_ref=src_refs[a].at[half(a, c)], dst_ref=out_refs[a].at[slot, half(a, c)], send_sem=ici_send.at[3 * a + k],
            recv_sem=ici_recv.at[3 * a + k], device_id=(px, py, c), device_id_type=MESH_ID)

    def d2d(a, k, which):
        px, py = peers[k]
        rows = out_refs[a].at[2 * px + py, half(a, which)]
        return pltpu.make_async_remote_copy(
            src_ref=rows, dst_ref=rows, send_sem=d2d_send.at[3 * a + k], recv_sem=d2d_recv.at[3 * a + k],
            device_id=(x, y, 1 - c), device_id_type=MESH_ID)

    mine = [pltpu.make_async_copy(src_refs[a], out_refs[a].at[q], local_sems.at[a]) for a in range(n)]
    sends = [ici(a, k, q) for a in range(n) for k in range(3)]
    if issue:
        for cp in mine + sends:
            cp.start()
        return
    passed = []
    for a in range(n):
        for k, (px, py) in enumerate(peers):
            ici(a, k, 2 * px + py).wait_recv()
            passed.append(d2d(a, k, c))
            passed[-1].start()
    for a in range(n):
        for k in range(3):
            d2d(a, k, 1 - c).wait_recv()
    for cp in sends + passed:
        cp.wait_send()
    for cp in mine:
        cp.wait()


def _sibling_swap(slabs, name):
    n = len(slabs)

    def body(*refs):
        src_refs, out_refs, send_sems, recv_sems = refs[:n], refs[n:2 * n], refs[2 * n], refs[2 * n + 1]
        x, y, c = lax.axis_index("x"), lax.axis_index("y"), lax.axis_index("c")
        cps = [pltpu.make_async_remote_copy(src_ref=src_refs[a], dst_ref=out_refs[a], send_sem=send_sems.at[a],
                                            recv_sem=recv_sems.at[a], device_id=(x, y, 1 - c), device_id_type=MESH_ID)
               for a in range(n)]
        for cp in cps:
            cp.start()
        for cp in cps:
            cp.wait_recv()
        for cp in cps:
            cp.wait_send()

    return pl.pallas_call(
        body, name=name, in_specs=[HBM_SPEC] * n, out_specs=[HBM_SPEC] * n,
        out_shape=[jax.ShapeDtypeStruct(s.shape, s.dtype) for s in slabs],
        scratch_shapes=[pltpu.SemaphoreType.DMA((n,)), pltpu.SemaphoreType.DMA((n,))])(*slabs)


def _sum_chips(r, name):
    rows, cols = r.shape[1:]
    tb = _div(rows, 256, 16)

    def body(r0, r1, r2, r3, o_ref):
        o_ref[...] = ((r0[0].astype(F32) + r1[0].astype(F32)) + r2[0].astype(F32)) + r3[0].astype(F32)

    spec = lambda j: pl.BlockSpec((1, tb, cols), lambda i: (j, i, 0))
    return pl.pallas_call(
        body, name=name, grid=(rows // tb,), in_specs=[spec(0), spec(1), spec(2), spec(3)],
        out_specs=pl.BlockSpec((tb, cols), lambda i: (i, 0)), out_shape=jax.ShapeDtypeStruct((rows, cols), F32),
        compiler_params=_cp("parallel"))(r, r, r, r)


def _adamw(w, m, v, p, q, name):
    rows, cols = w.shape
    tb = _div(rows, 256, 8)

    def body(w_ref, m_ref, v_ref, p_ref, q_ref, g_ref, d_ref, nm_ref, nv_ref):
        g = p_ref[...] + q_ref[...]
        m_new = B1 * m_ref[...] + (1.0 - B1) * g
        v_new = B2 * v_ref[...] + (1.0 - B2) * (g * g)
        g_ref[...] = g
        nm_ref[...] = m_new
        nv_ref[...] = v_new
        m_hat = m_new / (1.0 - B1 ** STEP)
        v_hat = v_new / (1.0 - B2 ** STEP)
        d_ref[...] = -LR * (m_hat / (jnp.sqrt(v_hat) + AEPS) + WD * w_ref[...])

    spec = pl.BlockSpec((tb, cols), lambda i: (i, 0))
    return pl.pallas_call(
        body, name=name, grid=(rows // tb,), in_specs=[spec] * 5, out_specs=[spec] * 4,
        out_shape=[jax.ShapeDtypeStruct((rows, cols), F32)] * 4, compiler_params=_cp("parallel"))(w, m, v, p, q)


def _split_w_in(w_in):
    o1 = 3 * FW
    o2 = o1 + HEADS
    o3 = o2 + 3 * FW
    o4 = o3 + FW
    o5 = o4 + HEADS
    o6 = o5 + HEADS
    pad = jnp.zeros((w_in.shape[0], SMALL_W - 3 * HEADS), w_in.dtype)
    rest = jnp.concatenate([w_in[:, o2:o3], w_in[:, o3:o4], w_in[:, o6:], w_in[:, o1:o2], w_in[:, o4:o5], w_in[:, o5:o6], pad],
                           axis=1)
    return w_in[:, :o1], rest


AXIS = {n: a for n, _, a in WEIGHTS}
SMALL = tuple(n for n, _, _ in WEIGHTS if n not in BIG)


def _by_chip(full, axis):
    rows, cols = full.shape
    if axis == 0:
        return full.reshape(N_CHIPS, rows // N_CHIPS, cols)
    return full.reshape(rows, N_CHIPS, cols // N_CHIPS).transpose(1, 0, 2)


def _from_chips(parts, axis):
    _, r, c = parts.shape
    if axis == 0:
        return parts.reshape(N_CHIPS * r, c)
    return parts.transpose(1, 0, 2).reshape(r, N_CHIPS * c)


def _gather_weights(shards):
    hi = {n: shards[n].astype(jnp.bfloat16) for n in SPLIT_F32}
    lo = [(shards[n] - hi[n].astype(F32)).astype(jnp.bfloat16) for n in SPLIT_F32]
    slab = _pack([hi[n] for n in SPLIT_F32] + lo, jnp.bfloat16)
    got = _gather_chips([shards[n].astype(jnp.bfloat16) for n in BIG_EARLY] + [slab], "gather_weights")
    full = {n: _from_chips(g, AXIS[n]) for n, g in zip(BIG_EARLY, got)}
    shapes = [shards[n].shape for n in SPLIT_F32] * 2
    per_chip = [_unpack(got[-1][j], shapes) for j in range(N_CHIPS)]
    for i, n in enumerate(SPLIT_F32):
        join = lambda off: jnp.concatenate([per_chip[j][off + i] for j in range(N_CHIPS)], axis=1).astype(F32)
        full[n] = join(0) + join(len(SPLIT_F32))
    return full


def kernel(x, meta_tokens, w_in, fgt_bias, gdn_conv_w, gdn_a_log, gdn_dt_bias, gdn_norm_w, gate_bias, w_branch_fox, w_branch_gdn, w_out, norm_mix_w, norm_ffn_w, ffn_w_up, ffn_conv_w, ffn_conv_b, ffn_w_down, norm_final_w, loss_target, m_meta_tokens, m_w_in, m_fgt_bias, m_gdn_conv_w, m_gdn_a_log, m_gdn_dt_bias, m_gdn_norm_w, m_gate_bias, m_w_branch_fox, m_w_branch_gdn, m_w_out, m_norm_mix_w, m_norm_ffn_w, m_ffn_w_up, m_ffn_conv_w, m_ffn_conv_b, m_ffn_w_down, m_norm_final_w, v_meta_tokens, v_w_in, v_fgt_bias, v_gdn_conv_w, v_gdn_a_log, v_gdn_dt_bias, v_gdn_norm_w, v_gate_bias, v_w_branch_fox, v_w_branch_gdn, v_w_out, v_norm_mix_w, v_norm_ffn_w, v_ffn_w_up, v_ffn_conv_w, v_ffn_conv_b, v_ffn_w_down, v_norm_final_w):
    weights = dict(meta_tokens=meta_tokens, w_in=w_in, fgt_bias=fgt_bias, gdn_conv_w=gdn_conv_w, gdn_a_log=gdn_a_log, gdn_dt_bias=gdn_dt_bias, gdn_norm_w=gdn_norm_w, gate_bias=gate_bias, w_branch_fox=w_branch_fox, w_branch_gdn=w_branch_gdn, w_out=w_out, norm_mix_w=norm_mix_w, norm_ffn_w=norm_ffn_w, ffn_w_up=ffn_w_up, ffn_conv_w=ffn_conv_w, ffn_conv_b=ffn_conv_b, ffn_w_down=ffn_w_down, norm_final_w=norm_final_w)
    m_in = dict(meta_tokens=m_meta_tokens, w_in=m_w_in, fgt_bias=m_fgt_bias, gdn_conv_w=m_gdn_conv_w, gdn_a_log=m_gdn_a_log, gdn_dt_bias=m_gdn_dt_bias, gdn_norm_w=m_gdn_norm_w, gate_bias=m_gate_bias, w_branch_fox=m_w_branch_fox, w_branch_gdn=m_w_branch_gdn, w_out=m_w_out, norm_mix_w=m_norm_mix_w, norm_ffn_w=m_norm_ffn_w, ffn_w_up=m_ffn_w_up, ffn_conv_w=m_ffn_conv_w, ffn_conv_b=m_ffn_conv_b, ffn_w_down=m_ffn_w_down, norm_final_w=m_norm_final_w)
    v_in = dict(meta_tokens=v_meta_tokens, w_in=v_w_in, fgt_bias=v_fgt_bias, gdn_conv_w=v_gdn_conv_w, gdn_a_log=v_gdn_a_log, gdn_dt_bias=v_gdn_dt_bias, gdn_norm_w=v_gdn_norm_w, gate_bias=v_gate_bias, w_branch_fox=v_w_branch_fox, w_branch_gdn=v_w_branch_gdn, w_out=v_w_out, norm_mix_w=v_norm_mix_w, norm_ffn_w=v_norm_ffn_w, ffn_w_up=v_ffn_w_up, ffn_conv_w=v_ffn_conv_w, ffn_conv_b=v_ffn_conv_b, ffn_w_down=v_ffn_w_down, norm_final_w=v_norm_final_w)
    shard2d = {n: _shard_shape(s, a) for n, s, a in WEIGHTS}
    as2d = lambda d: {n: d[n].reshape(shard2d[n]) for n, _, _ in WEIGHTS}
    w2, m2, v2 = as2d(weights), as2d(m_in), as2d(v_in)

    full = _gather_weights(w2)
    w_fox, w_rest = _split_w_in(full["w_in"])
    flat = lambda n: w2[n].reshape(-1)
    local_w = dict(
        meta=full["meta_tokens"], w_fox=w_fox, w_rest=w_rest, fgt_bias=flat("fgt_bias"), gdn_conv=full["gdn_conv_w"],
        a_log=flat("gdn_a_log"), dt_bias=flat("gdn_dt_bias"), gdn_norm=flat("gdn_norm_w"), gate_bias=flat("gate_bias"),
        norm_mix=flat("norm_mix_w"), norm_ffn=flat("norm_ffn_w"), ffn_conv=full["ffn_conv_w"], ffn_conv_b=flat("ffn_conv_b"),
        norm_final=flat("norm_final_w"))

    loss, grad_x, grads = _local_step(x[0], loss_target[0], local_w, {n: w2[n].astype(jnp.bfloat16) for n in BIG_LATE})

    g2 = {n: grads[n].reshape(s) for n, s, _ in WEIGHTS}
    send = [_by_chip(g2[n].astype(jnp.bfloat16), AXIS[n]) for n in BIG_EARLY]
    send.append(jnp.stack([_pack([_shard_of(g2[n], AXIS[n], j) for n in SMALL], F32) for j in range(N_CHIPS)]))
    recv = list(_scatter_chips(send, "scatter_grads"))
    recv = recv[:-1] + list(grads["late_recv"]) + recv[-1:]
    parts = [_sum_chips(r, "sum_" + n) for r, n in zip(recv, BIG + ("small",))]
    others = _sibling_swap(parts, "swap_grads")
    slab = lambda d: _pack([d[n] for n in SMALL], F32)
    state = [(w2[n], m2[n], v2[n]) for n in BIG] + [(slab(w2), slab(m2), slab(v2))]
    outs = [_adamw(w, m, v, p, q, "adamw_" + n) for (w, m, v), p, q, n in zip(state, parts, others, BIG + ("small",))]
    small = [_unpack(o, [weights[n].shape for n in SMALL]) for o in outs[-1]]
    result = []
    for kind in range(4):
        by_name = {n: outs[i][kind].reshape(weights[n].shape) for i, n in enumerate(BIG)}
        by_name.update(zip(SMALL, small[kind]))
        result += [by_name[n] for n, _, _ in WEIGHTS]
    total = lax.psum(loss[0, 0], ("x", "y", "c"))
    return (total, grad_x[None], *result)
```

```python
import functools

import numpy as np
import jax
import jax.numpy as jnp
from jax import lax
from jax.experimental import pallas as pl
from jax.experimental.pallas import tpu as pltpu

F32 = jnp.float32
_BF = jnp.bfloat16
HI = lax.Precision.HIGHEST

D = 1024
N_META = 16
CH = 64
FRONT = CH - N_META
X0 = CH
HEADS = 8
HD = 64
FW = HEADS * HD
DFF = 2816
EPS = 1e-6
NEG = -1e30
T_ALIGN = 256
SMALL_W = 128
REST_W = 3 * FW + FW + 2 * D + SMALL_W
OFF_Z = 3 * FW
OFF_GATES = 4 * FW
OFF_SMALL = 4 * FW + 2 * D

LR, B1, B2, AEPS, WD, STEP = 0.001, 0.9, 0.999, 1e-08, 0.01, 10

VMEM_LIMIT = 56 * 1024 * 1024
ROW_TILE = 640
MM_TM, MM_TN, MM_TK = 1280, 512, 2816
ATT_TILE = 640


def _div(n, target, mult=128):
    if n <= target:
        return n
    best = None
    for d in range(mult, target + 1, mult):
        if n % d == 0:
            best = d
    assert best is not None, (n, target, mult)
    return best


def _cp(*sem):
    return pltpu.CompilerParams(dimension_semantics=sem, vmem_limit_bytes=VMEM_LIMIT)


def _sig(x):
    return 1.0 / (1.0 + jnp.exp(-x))


MM_VMEM_BUDGET = 40 * 1024 * 1024
MM_STEP_BYTES = 1 << 20


def _mm_tiles(m, n, k, sa, sb, so, has_resid):
    divs = lambda d, cap: [t for t in range(128, min(d, cap) + 1, 128) if d % t == 0] or [d]
    best = None
    for tm in divs(m, MM_TM * 2):
        for tn in divs(n, 4608):
            fixed = tm * tn * (4 + 2 * so + (8 if has_resid else 0))
            tks = [t for t in divs(k, MM_TK) if 2 * t * (tm * sa + tn * sb) + fixed <= MM_VMEM_BUDGET]
            if not tks:
                continue
            tk = tks[-1]
            steps = (m // tm) * (n // tn) * (k // tk)
            rmw = (k // tk - 1) * m * n * 4
            cost = (n // tn) * m * k * sa + (m // tm) * k * n * sb + steps * MM_STEP_BYTES + rmw
            if best is None or cost < best[0]:
                best = (cost, tm, tn, tk)
    assert best is not None, (m, n, k)
    return best[1:]


def _mm_t(a, b, name):
    return _mm(a, b, ta=True, name=name)


def _mm(a, b, *, ta=False, tb=False, out_dtype=F32, resid=None, name, scatter=()):
    K, M = a.shape if ta else a.shape[::-1]
    N = b.shape[0] if tb else b.shape[1]
    assert (b.shape[1] if tb else b.shape[0]) == K
    tm, tn, tk = _mm_tiles(M, N, K, a.dtype.itemsize, b.dtype.itemsize, jnp.dtype(out_dtype).itemsize, resid is not None)
    nk = K // tk
    grid = (M // tm, N // tn, nk)
    dims = (((0 if ta else 1,), (1 if tb else 0,)), ((), ()))
    mxu = _BF
    nr = 1 if resid is not None else 0
    ng = len(scatter)

    def body(*refs):
        a_ref, b_ref = refs[:2]
        r_ref = refs[2] if nr else None
        o_ref = refs[2 + nr + ng]
        acc = refs[3 + nr + 2 * ng]
        comm = (refs[2 + nr:2 + nr + ng], refs[3 + nr + ng:3 + nr + 2 * ng], refs[4 + nr + 2 * ng:])
        k = pl.program_id(2)
        if ng:
            step = (pl.program_id(0) * grid[1] + pl.program_id(1)) * grid[2] + k
            pl.when(step == 0)(functools.partial(_scatter_phase, *comm, True))
        part = lax.dot_general(a_ref[...].astype(mxu), b_ref[...].astype(mxu), dims, preferred_element_type=F32)

        @pl.when(k == 0)
        def _():
            acc[...] = part

        @pl.when(k > 0)
        def _():
            acc[...] += part

        @pl.when(k == nk - 1)
        def _():
            r = acc[...]
            if r_ref is not None:
                r = r + r_ref[...]
            o_ref[...] = r.astype(o_ref.dtype)

        if ng:
            pl.when(step == grid[0] * grid[1] * grid[2] - 1)(functools.partial(_scatter_phase, *comm, False))

    a_spec = pl.BlockSpec((tk, tm), lambda i, j, k: (k, i)) if ta else pl.BlockSpec((tm, tk), lambda i, j, k: (i, k))
    b_spec = pl.BlockSpec((tn, tk), lambda i, j, k: (j, k)) if tb else pl.BlockSpec((tk, tn), lambda i, j, k: (k, j))
    o_spec = pl.BlockSpec((tm, tn), lambda i, j, k: (i, j))
    in_specs = [a_spec, b_spec] + ([o_spec] if nr else []) + [HBM_SPEC] * ng
    args = (a, b) + ((resid,) if nr else ()) + tuple(scatter)
    out = jax.ShapeDtypeStruct((M, N), out_dtype)
    sem = ("arbitrary",) * 3 if ng else ("parallel", "parallel", "arbitrary")
    return pl.pallas_call(
        body, name=name, grid=grid, in_specs=in_specs, out_specs=[o_spec] + [HBM_SPEC] * ng if ng else o_spec,
        out_shape=[out] + _scatter_shapes(scatter) if ng else out,
        scratch_shapes=[pltpu.VMEM((tm, tn), F32)] + (_scatter_sems(ng) if ng else []),
        compiler_params=_cp(*sem))(*args)


def _rms_fwd(h, w, name):
    T = h.shape[0]
    tb = _div(T, ROW_TILE, 8)

    def body(h_ref, w_ref, o_ref):
        x = h_ref[...]
        r = lax.rsqrt(jnp.mean(x * x, axis=-1, keepdims=True) + EPS)
        o_ref[...] = (x * r * w_ref[...]).astype(o_ref.dtype)

    return pl.pallas_call(
        body, name=name, grid=(T // tb,),
        in_specs=[pl.BlockSpec((tb, D), lambda i: (i, 0)), pl.BlockSpec((1, D), lambda i: (0, 0))],
        out_specs=pl.BlockSpec((tb, D), lambda i: (i, 0)), out_shape=jax.ShapeDtypeStruct((T, D), _BF),
        compiler_params=_cp("parallel"))(h, w)


def _rms_bwd(h, w, dy, dres, name):
    T = h.shape[0]
    tb = _div(T, ROW_TILE, 8)

    def body(h_ref, w_ref, dy_ref, dr_ref, dh_ref, dw_ref):
        x = h_ref[...]
        r = lax.rsqrt(jnp.mean(x * x, axis=-1, keepdims=True) + EPS)
        xh = x * r
        dy = dy_ref[...]
        g = dy * w_ref[...]
        dh_ref[...] = dr_ref[...] + r * (g - xh * jnp.mean(xh * g, axis=-1, keepdims=True))
        part = jnp.sum(dy * xh, axis=0, keepdims=True)

        @pl.when(pl.program_id(0) == 0)
        def _():
            dw_ref[...] = part

        @pl.when(pl.program_id(0) > 0)
        def _():
            dw_ref[...] += part

    row = pl.BlockSpec((tb, D), lambda i: (i, 0))
    one = pl.BlockSpec((1, D), lambda i: (0, 0))
    return pl.pallas_call(
        body, name=name, grid=(T // tb,), in_specs=[row, one, row, row], out_specs=[row, one],
        out_shape=[jax.ShapeDtypeStruct((T, D), F32), jax.ShapeDtypeStruct((1, D), F32)],
        compiler_params=_cp("arbitrary"))(h, w, dy, dres)


def _fgate_fwd(small, bias):
    T = small.shape[0]
    tb = _div(T, ROW_TILE, 8)

    def body(s_ref, b_ref, c_ref, carry):
        @pl.when(pl.program_id(0) == 0)
        def _():
            carry[...] = jnp.zeros_like(carry)

        lf = jax.nn.log_sigmoid(s_ref[...] + b_ref[...])
        r = lax.broadcasted_iota(jnp.int32, (tb, tb), 0)
        c = lax.broadcasted_iota(jnp.int32, (tb, tb), 1)
        tri = (r >= c).astype(F32)
        cs = jnp.dot(tri, lf, precision=HI, preferred_element_type=F32) + carry[...]
        c_ref[...] = cs
        carry[...] = cs[tb - 1:tb, :]

    return pl.pallas_call(
        body, name="fgate_fwd", grid=(T // tb,),
        in_specs=[pl.BlockSpec((tb, SMALL_W), lambda i: (i, 0)), pl.BlockSpec((1, SMALL_W), lambda i: (0, 0))],
        out_specs=pl.BlockSpec((tb, SMALL_W), lambda i: (i, 0)), out_shape=jax.ShapeDtypeStruct((T, SMALL_W), F32),
        scratch_shapes=[pltpu.VMEM((1, SMALL_W), F32)], compiler_params=_cp("arbitrary"))(small, bias)


def _fgate_bwd(small, bias, dc):
    T = small.shape[0]
    tb = _div(T, ROW_TILE, 8)
    nb = T // tb

    def body(s_ref, b_ref, dc_ref, df_ref, db_ref, carry):
        @pl.when(pl.program_id(0) == 0)
        def _():
            carry[...] = jnp.zeros_like(carry)

        r = lax.broadcasted_iota(jnp.int32, (tb, tb), 0)
        c = lax.broadcasted_iota(jnp.int32, (tb, tb), 1)
        tri = (r <= c).astype(F32)
        dlf = jnp.dot(tri, dc_ref[...], precision=HI, preferred_element_type=F32) + carry[...]
        carry[...] = dlf[0:1, :]
        df = dlf * _sig(-(s_ref[...] + b_ref[...]))
        df_ref[...] = df
        part = jnp.sum(df, axis=0, keepdims=True)

        @pl.when(pl.program_id(0) == 0)
        def _():
            db_ref[...] = part

        @pl.when(pl.program_id(0) > 0)
        def _():
            db_ref[...] += part

    rev = pl.BlockSpec((tb, SMALL_W), lambda i: (nb - 1 - i, 0))
    one = pl.BlockSpec((1, SMALL_W), lambda i: (0, 0))
    return pl.pallas_call(
        body, name="fgate_bwd", grid=(nb,), in_specs=[rev, one, rev], out_specs=[rev, one],
        out_shape=[jax.ShapeDtypeStruct((T, SMALL_W), F32), jax.ShapeDtypeStruct((1, SMALL_W), F32)],
        scratch_shapes=[pltpu.VMEM((1, SMALL_W), F32)], compiler_params=_cp("arbitrary"))(small, bias, dc)


def _att_rows(a8, tb):
    T = a8.shape[0]
    return a8.T.reshape(HEADS // 2, 2, T // tb, tb).transpose(0, 2, 1, 3)


def _att_cols(a8):
    return jnp.repeat(a8, HD, axis=1)


def _pair_cols(tb):
    return pl.BlockSpec((1, tb, 2), lambda p, i, *_: (p, i, 0))


def _from_pairs(a):
    return a.transpose(1, 0, 2).reshape(a.shape[1], HEADS)


EXP_ZERO = -104.0
SKIP_SLACK = 2.0
NORM_SLACK = 1.02


def _att_stats(qkv, c_cols, ones_blk):
    T = qkv.shape[0]
    tb = _div(T, ATT_TILE)

    def body(q_ref, k_ref, c_ref, e_ref, o_ref):
        q = q_ref[...].astype(F32)
        k = k_ref[...].astype(F32)
        e = e_ref[...].astype(_BF)
        qn = jnp.max(jnp.dot((q * q).astype(_BF), e, preferred_element_type=F32), axis=0, keepdims=True)
        kn = jnp.max(jnp.dot((k * k).astype(_BF), e, preferred_element_type=F32), axis=0, keepdims=True)
        c = c_ref[...]
        o_ref[0] = jnp.concatenate([jnp.sqrt(qn), jnp.sqrt(kn), jnp.max(c, axis=0, keepdims=True),
                                    jnp.min(c, axis=0, keepdims=True), jnp.zeros((4, FW), F32)], axis=0)

    blk = lambda off: pl.BlockSpec((tb, FW), lambda i: (i, off))
    return pl.pallas_call(
        body, name="att_stats", grid=(T // tb,),
        in_specs=[blk(0), blk(1), blk(0), pl.BlockSpec((FW, FW), lambda i: (0, 0))],
        out_specs=pl.BlockSpec((1, 8, FW), lambda i: (i, 0, 0)), out_shape=jax.ShapeDtypeStruct((T // tb, 8, FW), F32),
        compiler_params=_cp("parallel"))(qkv, qkv, c_cols, ones_blk)


def _att_plan(stats):
    nb = stats.shape[0]
    st = stats[:, :4, ::HD]
    qmax, kmax, cmax, cmin = (st[:, r, :].T for r in range(4))
    bound = (HD ** -0.5) * NORM_SLACK * qmax[:, :, None] * (kmax[:, None, :] + kmax[:, :, None]) \
        + cmax[:, :, None] - cmin[:, None, :] + SKIP_SLACK
    ii = lax.broadcasted_iota(jnp.int32, (nb, nb), 0)
    jj = lax.broadcasted_iota(jnp.int32, (nb, nb), 1)
    skip = (bound < EXP_ZERO) & (jj < ii)[None]
    live = (~skip & (jj <= ii)[None]).reshape(HEADS // 2, 2, nb, nb).any(axis=1)
    jfirst = jnp.argmax(live, axis=2).astype(jnp.int32)
    ilast = (nb - 1 - jnp.argmax(live[:, ::-1, :], axis=1)).astype(jnp.int32)
    return skip.astype(jnp.int32).reshape(-1), jfirst.reshape(-1), ilast.reshape(-1)


def _fox_fwd(qkv, c_cols, c_rows, plan):
    T = qkv.shape[0]
    tb = _div(T, ATT_TILE)
    nb = T // tb
    npair = HEADS // 2
    scale = HD ** -0.5
    dn = (((1,), (1,)), ((), ()))

    def body(skip_ref, jfirst_ref, ilast_ref, q_ref, k_ref, v_ref, cq_ref, ck_ref, o_ref, l_ref, l2_ref, m_scr, acc_scr):
        pr = pl.program_id(0)
        i = pl.program_id(1)
        q = q_ref[...]
        lane = lax.broadcasted_iota(jnp.int32, (1, 2 * HD), 1)
        sel0 = lane < HD
        zero = jnp.zeros_like(q)
        qh = (jnp.where(sel0, q, zero) * scale, jnp.where(sel0, zero, q) * scale)
        cq = (cq_ref[:, 0:1], cq_ref[:, HD:HD + 1])
        row = i * tb + lax.broadcasted_iota(jnp.int32, (tb, 1), 0)
        m_scr[...] = jnp.full(m_scr.shape, NEG, F32)
        acc_scr[...] = jnp.zeros(acc_scr.shape, F32)

        def tile(j, h, masked):
            start = pl.multiple_of(j * tb, tb)
            kb = k_ref[pl.ds(start, tb), :]
            vb = v_ref[pl.ds(start, tb), :]
            one = jnp.ones_like(vb)
            vh = jnp.where(sel0, vb, one) if h == 0 else jnp.where(sel0, one, vb)
            t = lax.dot_general(qh[h], kb, dn, preferred_element_type=F32) - ck_ref[0, j, h:h + 1, :]
            if masked:
                col = j * tb + lax.broadcasted_iota(jnp.int32, (1, tb), 1)
                mask = (col <= row) & (col >= FRONT)
                t = jnp.where(mask, t, NEG)
            m = m_scr[h]
            m_new = jnp.maximum(m, jnp.max(t, axis=-1, keepdims=True) + cq[h])
            p = jnp.exp(t + (cq[h] - m_new))
            if masked:
                p = jnp.where(mask, p, 0.0)
            m_scr[h] = m_new
            acc_scr[h] = acc_scr[h] * jnp.exp(m - m_new) + jnp.dot(p.astype(vb.dtype), vh, preferred_element_type=F32)

        def step(j, _):
            edge = (j == 0) | (j == i)
            for h in range(2):
                live = skip_ref[((2 * pr + h) * nb + i) * nb + j] == 0
                pl.when(live & edge)(functools.partial(tile, j, h, True))
                pl.when(live & jnp.logical_not(edge))(functools.partial(tile, j, h, False))
            return 0

        lax.fori_loop(jfirst_ref[pr * nb + i], i + 1, step, 0)
        outs, lses = [], []
        for h in range(2):
            l = acc_scr[h][:, (1 - h) * HD:(1 - h) * HD + 1]
            ok = l > 0.0
            ls = jnp.where(ok, l, 1.0)
            outs.append(jnp.where(ok, acc_scr[h] / ls, 0.0))
            lses.append(jnp.where(ok, m_scr[h] + jnp.log(ls), 0.0))
        o_ref[...] = jnp.where(sel0, outs[0], outs[1])
        l_ref[...] = jnp.where(sel0, lses[0], lses[1])
        l2_ref[0, :, 0:1] = lses[0]
        l2_ref[0, :, 1:2] = lses[1]

    blk = lambda off: pl.BlockSpec((tb, 2 * HD), lambda p, i, *_: (i, off + p))
    full = lambda off: pl.BlockSpec((T, 2 * HD), lambda p, i, *_: (0, off + p))
    rows = pl.BlockSpec((1, nb, 2, tb), lambda p, i, *_: (p, 0, 0, 0))
    return pl.pallas_call(
        body, name="fox_fwd",
        grid_spec=pltpu.PrefetchScalarGridSpec(
            num_scalar_prefetch=3, grid=(npair, nb),
            in_specs=[blk(0), full(npair), full(2 * npair), blk(0), rows], out_specs=[blk(0), blk(0), _pair_cols(tb)],
            scratch_shapes=[pltpu.VMEM((2, tb, 1), F32), pltpu.VMEM((2, tb, 2 * HD), F32)]),
        out_shape=[jax.ShapeDtypeStruct((T, FW), F32), jax.ShapeDtypeStruct((T, FW), F32),
                   jax.ShapeDtypeStruct((npair, T, 2), F32)],
        compiler_params=_cp("parallel", "arbitrary"))(*plan, qkv, qkv, qkv, c_cols, c_rows)


def _fox_bwd_dq(qkv, do, c_cols, c_rows, lse, delta, plan):
    T = qkv.shape[0]
    tb = _div(T, ATT_TILE)
    nb = T // tb
    npair = HEADS // 2
    scale = HD ** -0.5
    dn = (((1,), (1,)), ((), ()))

    def body(skip_ref, jfirst_ref, ilast_ref, q_ref, k_ref, v_ref, do_ref, cq_ref, ck_ref, l_ref, dl_ref, dq_ref, dc_ref,
             dq_scr):
        pr = pl.program_id(0)
        i = pl.program_id(1)
        q = q_ref[...]
        do = do_ref[...]
        lane = lax.broadcasted_iota(jnp.int32, (1, 2 * HD), 1)
        sel0 = lane < HD
        qh = (jnp.where(sel0, q, jnp.zeros_like(q)) * scale, jnp.where(sel0, jnp.zeros_like(q), q) * scale)
        doh = (jnp.where(sel0, do, jnp.zeros_like(do)), jnp.where(sel0, jnp.zeros_like(do), do))
        ce = (cq_ref[:, 0:1] - l_ref[:, 0:1], cq_ref[:, HD:HD + 1] - l_ref[:, HD:HD + 1])
        dl = (dl_ref[:, 0:1], dl_ref[:, HD:HD + 1])
        row = i * tb + lax.broadcasted_iota(jnp.int32, (tb, 1), 0)
        dq_scr[...] = jnp.zeros(dq_scr.shape, F32)

        def tile(j, h, masked):
            start = pl.multiple_of(j * tb, tb)
            kb = k_ref[pl.ds(start, tb), :]
            vb = v_ref[pl.ds(start, tb), :]
            one = jnp.ones_like(kb)
            kh = jnp.where(sel0, kb, one) if h == 0 else jnp.where(sel0, one, kb)
            t = lax.dot_general(qh[h], kb, dn, preferred_element_type=F32) - ck_ref[0, j, h:h + 1, :]
            if masked:
                col = j * tb + lax.broadcasted_iota(jnp.int32, (1, tb), 1)
                mask = (col <= row) & (col >= FRONT)
                p = jnp.where(mask, jnp.exp(jnp.where(mask, t, NEG) + ce[h]), 0.0)
            else:
                p = jnp.exp(t + ce[h])
            dp = lax.dot_general(doh[h], vb, dn, preferred_element_type=F32)
            ds = p * (dp - dl[h])
            dq_scr[h] += jnp.dot(ds.astype(kb.dtype), kh, preferred_element_type=F32)

        def step(j, _):
            edge = (j == 0) | (j == i)
            for h in range(2):
                live = skip_ref[((2 * pr + h) * nb + i) * nb + j] == 0
                pl.when(live & edge)(functools.partial(tile, j, h, True))
                pl.when(live & jnp.logical_not(edge))(functools.partial(tile, j, h, False))
            return 0

        lax.fori_loop(jfirst_ref[pr * nb + i], i + 1, step, 0)
        dq_ref[...] = (jnp.where(sel0, dq_scr[0], dq_scr[1]) * scale).astype(dq_ref.dtype)
        dc_ref[0, :, 0:1] = dq_scr[0][:, HD:HD + 1]
        dc_ref[0, :, 1:2] = dq_scr[1][:, 0:1]

    blk = lambda off: pl.BlockSpec((tb, 2 * HD), lambda p, i, *_: (i, off + p))
    full = lambda off: pl.BlockSpec((T, 2 * HD), lambda p, i, *_: (0, off + p))
    rows = pl.BlockSpec((1, nb, 2, tb), lambda p, i, *_: (p, 0, 0, 0))
    return pl.pallas_call(
        body, name="fox_bwd_dq",
        grid_spec=pltpu.PrefetchScalarGridSpec(
            num_scalar_prefetch=3, grid=(npair, nb),
            in_specs=[blk(0), full(npair), full(2 * npair), blk(0), blk(0), rows, blk(0), blk(0)],
            out_specs=[blk(0), _pair_cols(tb)],
            scratch_shapes=[pltpu.VMEM((2, tb, 2 * HD), F32)]),
        out_shape=[jax.ShapeDtypeStruct((T, FW), _BF), jax.ShapeDtypeStruct((npair, T, 2), F32)],
        compiler_params=_cp("parallel", "arbitrary"))(*plan, qkv, qkv, qkv, do, c_cols, c_rows, lse, delta)


def _fox_bwd_dkv(qkv, do, c_cols, ce_rows, delta_rows, plan):
    T = qkv.shape[0]
    tb = _div(T, ATT_TILE)
    nb = T // tb
    npair = HEADS // 2
    scale = HD ** -0.5
    dn = (((1,), (1,)), ((), ()))

    def body(skip_ref, jfirst_ref, ilast_ref, q_ref, k_ref, v_ref, do_ref, ck_ref, ce_ref, dl_ref,
             dk_ref, dv_ref, dc_ref, dk_scr, dv_scr):
        pr = pl.program_id(0)
        jb = pl.program_id(1)
        k = k_ref[...]
        v = v_ref[...]
        lane = lax.broadcasted_iota(jnp.int32, (1, 2 * HD), 1)
        sel0 = lane < HD
        kh = (jnp.where(sel0, k, jnp.zeros_like(k)) * scale, jnp.where(sel0, jnp.zeros_like(k), k) * scale)
        vh = (jnp.where(sel0, v, jnp.zeros_like(v)), jnp.where(sel0, jnp.zeros_like(v), v))
        ck = (ck_ref[:, 0:1], ck_ref[:, HD:HD + 1])
        kidx = jb * tb + lax.broadcasted_iota(jnp.int32, (tb, 1), 0)
        dk_scr[...] = jnp.zeros(dk_scr.shape, F32)
        dv_scr[...] = jnp.zeros(dv_scr.shape, F32)

        def tile(i, h, masked):
            start = pl.multiple_of(i * tb, tb)
            qb = q_ref[pl.ds(start, tb), :]
            dob = do_ref[pl.ds(start, tb), :]
            one = jnp.ones_like(qb)
            qh = jnp.where(sel0, qb, one) if h == 0 else jnp.where(sel0, one, qb)
            t = lax.dot_general(kh[h], qb, dn, preferred_element_type=F32) - ck[h]
            ce = ce_ref[0, i, h:h + 1, :]
            if masked:
                qidx = i * tb + lax.broadcasted_iota(jnp.int32, (1, tb), 1)
                mask = (kidx <= qidx) & (kidx >= FRONT)
                pt = jnp.where(mask, jnp.exp(jnp.where(mask, t, NEG) + ce), 0.0)
            else:
                pt = jnp.exp(t + ce)
            dv_scr[h] += jnp.dot(pt.astype(dob.dtype), dob, preferred_element_type=F32)
            dpt = lax.dot_general(vh[h], dob, dn, preferred_element_type=F32)
            dst = pt * (dpt - dl_ref[0, i, h:h + 1, :])
            dk_scr[h] += jnp.dot(dst.astype(qb.dtype), qh, preferred_element_type=F32)

        def step(i, _):
            edge = (i == jb) | (jb == 0)
            for h in range(2):
                live = skip_ref[((2 * pr + h) * nb + i) * nb + jb] == 0
                pl.when(live & edge)(functools.partial(tile, i, h, True))
                pl.when(live & jnp.logical_not(edge))(functools.partial(tile, i, h, False))
            return 0

        lax.fori_loop(jb, ilast_ref[pr * nb + jb] + 1, step, 0)
        dk_ref[...] = (jnp.where(sel0, dk_scr[0], dk_scr[1]) * scale).astype(dk_ref.dtype)
        dv_ref[...] = jnp.where(sel0, dv_scr[0], dv_scr[1]).astype(dv_ref.dtype)
        dc_ref[0, :, 0:1] = -dk_scr[0][:, HD:HD + 1]
        dc_ref[0, :, 1:2] = -dk_scr[1][:, 0:1]

    blk = lambda off: pl.BlockSpec((tb, 2 * HD), lambda p, j, *_: (j, off + p))
    full = lambda off: pl.BlockSpec((T, 2 * HD), lambda p, j, *_: (0, off + p))
    rows = pl.BlockSpec((1, nb, 2, tb), lambda p, j, *_: (p, 0, 0, 0))
    return pl.pallas_call(
        body, name="fox_bwd_dkv",
        grid_spec=pltpu.PrefetchScalarGridSpec(
            num_scalar_prefetch=3, grid=(npair, nb),
            in_specs=[full(0), blk(npair), blk(2 * npair), full(0), blk(0), rows, rows],
            out_specs=[blk(0), blk(0), _pair_cols(tb)],
            scratch_shapes=[pltpu.VMEM((2, tb, 2 * HD), F32), pltpu.VMEM((2, tb, 2 * HD), F32)]),
        out_shape=[jax.ShapeDtypeStruct((T, FW), _BF)] * 2 + [jax.ShapeDtypeStruct((npair, T, 2), F32)],
        compiler_params=_cp("parallel", "arbitrary"))(*plan, qkv, qkv, qkv, do, c_cols, ce_rows, delta_rows)


def _head_dot(a, b, ones_blk, ones_col):
    T = a.shape[0]
    tb = _div(T, ROW_TILE, 8)

    def body(a_ref, b_ref, e_ref, c_ref, o_ref, o8_ref):
        prod = a_ref[...].astype(F32) * b_ref[...].astype(F32)
        o_ref[...] = jnp.dot(prod, e_ref[...], precision=HI, preferred_element_type=F32)
        o8_ref[...] = jnp.dot(prod, c_ref[...], precision=HI, preferred_element_type=F32)

    row = pl.BlockSpec((tb, FW), lambda i: (i, 0))
    return pl.pallas_call(
        body, name="head_dot", grid=(T // tb,),
        in_specs=[row, row, pl.BlockSpec((FW, FW), lambda i: (0, 0)), pl.BlockSpec((FW, SMALL_W), lambda i: (0, 0))],
        out_specs=[row, pl.BlockSpec((tb, SMALL_W), lambda i: (i, 0))],
        out_shape=[jax.ShapeDtypeStruct((T, FW), F32), jax.ShapeDtypeStruct((T, SMALL_W), F32)],
        compiler_params=_cp("parallel"))(a, b, ones_blk, ones_col)


CONV_RC = 32
CONV_TC = 256


def _stage_prev(scr, x_ref, h_ref, first, tb):
    scr[0:8, :] = jnp.where(first, 0.0, h_ref[...])
    scr[8:8 + tb, :] = x_ref[...]


def _windows_prev(scr, r, kw):
    x = scr[pl.ds(pl.multiple_of(r * CONV_RC, CONV_RC), CONV_RC + 8), :]
    return [x[8:] if k == kw - 1 else pltpu.roll(x, kw - 1 - k, 0)[8:] for k in range(kw)]


def _fold8(a):
    return a.reshape(CONV_RC // 8, 8, a.shape[-1]).sum(axis=0)


def _halo_prev(tb, tc, off=0):
    return pl.BlockSpec((8, tc), lambda j, i: (jnp.maximum(i * (tb // 8) - 1, 0), j + off))


def _gconv_fwd(rest, w):
    T = rest.shape[0]
    C = 3 * FW
    kw = w.shape[0]
    tb, tc = _div(T, 1280, CONV_RC), CONV_TC

    def body(x_ref, h_ref, w_ref, o_ref, scr):
        _stage_prev(scr, x_ref, h_ref, pl.program_id(1) == 0, tb)
        wv = w_ref[...]

        def chunk(r, _):
            win = _windows_prev(scr, r, kw)
            u = sum(wv[k:k + 1, :] * win[k] for k in range(kw))
            o_ref[pl.ds(pl.multiple_of(r * CONV_RC, CONV_RC), CONV_RC), :] = u * _sig(u)
            return 0

        lax.fori_loop(0, tb // CONV_RC, chunk, 0)

    return pl.pallas_call(
        body, name="gconv_fwd", grid=(C // tc, T // tb),
        in_specs=[pl.BlockSpec((tb, tc), lambda j, i: (i, j)), _halo_prev(tb, tc), pl.BlockSpec((kw, tc), lambda j, i: (0, j))],
        out_specs=pl.BlockSpec((tb, tc), lambda j, i: (i, j)), out_shape=jax.ShapeDtypeStruct((T, C), F32),
        scratch_shapes=[pltpu.VMEM((tb + 8, tc), F32)], compiler_params=_cp("parallel", "arbitrary"))(rest, rest, w)


def _gconv_bwd_du(rest, w, dy):
    T = rest.shape[0]
    C = 3 * FW
    kw = w.shape[0]
    tb, tc = _div(T, 1280, CONV_RC), CONV_TC

    def body(x_ref, h_ref, w_ref, dy_ref, du_ref, dw_ref, scr):
        i = pl.program_id(1)
        _stage_prev(scr, x_ref, h_ref, i == 0, tb)
        wv = w_ref[...]

        def chunk(r, acc):
            rows = pl.ds(pl.multiple_of(r * CONV_RC, CONV_RC), CONV_RC)
            win = _windows_prev(scr, r, kw)
            u = sum(wv[k:k + 1, :] * win[k] for k in range(kw))
            sg = _sig(u)
            du = dy_ref[rows, :] * sg * (1.0 + u * (1.0 - sg))
            du_ref[rows, :] = du
            return tuple(acc[k] + _fold8(du * win[k]) for k in range(kw))

        acc = lax.fori_loop(0, tb // CONV_RC, chunk, tuple(jnp.zeros((8, tc), F32) for _ in range(kw)))
        part = jnp.concatenate([jnp.sum(a, axis=0, keepdims=True) for a in acc], axis=0)

        @pl.when(i == 0)
        def _():
            dw_ref[...] = part

        @pl.when(i > 0)
        def _():
            dw_ref[...] += part

    blk = pl.BlockSpec((tb, tc), lambda j, i: (i, j))
    wsp = pl.BlockSpec((kw, tc), lambda j, i: (0, j))
    return pl.pallas_call(
        body, name="gconv_bwd_du", grid=(C // tc, T // tb), in_specs=[blk, _halo_prev(tb, tc), wsp, blk],
        out_specs=[blk, wsp], out_shape=[jax.ShapeDtypeStruct((T, C), F32), jax.ShapeDtypeStruct((kw, C), F32)],
        scratch_shapes=[pltpu.VMEM((tb + 8, tc), F32)], compiler_params=_cp("parallel", "arbitrary"))(rest, rest, w, dy)


def _conv_bwd_dx(du, w, out_dtype, name):
    T, C = du.shape
    kw = w.shape[0]
    tb = _div(T, 1280, CONV_RC)
    tc = CONV_TC
    nb = T // tb

    def body(x_ref, h_ref, w_ref, o_ref, scr):
        scr[0:tb, :] = x_ref[...]
        scr[tb:tb + 8, :] = jnp.where(pl.program_id(1) == nb - 1, 0.0, h_ref[...])
        wv = w_ref[...]

        def chunk(r, _):
            start = pl.multiple_of(r * CONV_RC, CONV_RC)
            x = scr[pl.ds(start, CONV_RC + 8), :]
            acc = wv[kw - 1:kw, :] * x[:CONV_RC]
            for k in range(kw - 1):
                acc = acc + wv[k:k + 1, :] * pltpu.roll(x, CONV_RC + 8 - (kw - 1 - k), 0)[:CONV_RC]
            o_ref[pl.ds(start, CONV_RC), :] = acc.astype(o_ref.dtype)
            return 0

        lax.fori_loop(0, tb // CONV_RC, chunk, 0)

    halo = pl.BlockSpec((8, tc), lambda j, i: (jnp.minimum((i + 1) * (tb // 8), T // 8 - 1), j))
    return pl.pallas_call(
        body, name=name, grid=(C // tc, nb),
        in_specs=[pl.BlockSpec((tb, tc), lambda j, i: (i, j)), halo, pl.BlockSpec((kw, tc), lambda j, i: (0, j))],
        out_specs=pl.BlockSpec((tb, tc), lambda j, i: (i, j)), out_shape=jax.ShapeDtypeStruct((T, C), out_dtype),
        scratch_shapes=[pltpu.VMEM((tb + 8, tc), F32)], compiler_params=_cp("parallel", "arbitrary"))(du, du, w)


def _glu_fwd(up, w, b):
    T = up.shape[0]
    kw = w.shape[0]
    tb, tc = _div(T, 1280, CONV_RC), CONV_TC
    nc = DFF // tc

    def body(xg, hg, xu, hu, wg, wu, bg, bu, o_ref, sg, su):
        first = pl.program_id(1) == 0
        _stage_prev(sg, xg, hg, first, tb)
        _stage_prev(su, xu, hu, first, tb)
        wgv, wuv, bgv, buv = wg[...], wu[...], bg[...], bu[...]

        def chunk(r, _):
            wing, winu = _windows_prev(sg, r, kw), _windows_prev(su, r, kw)
            ug = bgv + sum(wgv[k:k + 1, :] * wing[k] for k in range(kw))
            uu = buv + sum(wuv[k:k + 1, :] * winu[k] for k in range(kw))
            o_ref[pl.ds(pl.multiple_of(r * CONV_RC, CONV_RC), CONV_RC), :] = (ug * _sig(ug) * uu).astype(o_ref.dtype)
            return 0

        lax.fori_loop(0, tb // CONV_RC, chunk, 0)

    blk = lambda off: pl.BlockSpec((tb, tc), lambda j, i: (i, j + off))
    wsp = lambda off: pl.BlockSpec((kw, tc), lambda j, i: (0, j + off))
    bsp = lambda off: pl.BlockSpec((1, tc), lambda j, i: (0, j + off))
    return pl.pallas_call(
        body, name="glu_fwd", grid=(nc, T // tb),
        in_specs=[blk(0), _halo_prev(tb, tc), blk(nc), _halo_prev(tb, tc, nc), wsp(0), wsp(nc), bsp(0), bsp(nc)],
        out_specs=blk(0), out_shape=jax.ShapeDtypeStruct((T, DFF), _BF),
        scratch_shapes=[pltpu.VMEM((tb + 8, tc), F32)] * 2, compiler_params=_cp("parallel", "arbitrary"))(
            up, up, up, up, w, w, b, b)


def _glu_bwd_du(up, w, b, df):
    T = up.shape[0]
    kw = w.shape[0]
    tb, tc = _div(T, 1280, CONV_RC), CONV_TC
    nc = DFF // tc

    def body(xg, hg, xu, hu, wg, wu, bg, bu, df_ref, dug_ref, duu_ref, dwg_ref, dwu_ref, dbg_ref, dbu_ref, sg, su):
        i = pl.program_id(1)
        _stage_prev(sg, xg, hg, i == 0, tb)
        _stage_prev(su, xu, hu, i == 0, tb)
        wgv, wuv, bgv, buv = wg[...], wu[...], bg[...], bu[...]

        def chunk(r, acc):
            rows = pl.ds(pl.multiple_of(r * CONV_RC, CONV_RC), CONV_RC)
            wing, winu = _windows_prev(sg, r, kw), _windows_prev(su, r, kw)
            ug = bgv + sum(wgv[k:k + 1, :] * wing[k] for k in range(kw))
            uu = buv + sum(wuv[k:k + 1, :] * winu[k] for k in range(kw))
            s = _sig(ug)
            df = df_ref[rows, :]
            dug = df * uu * s * (1.0 + ug * (1.0 - s))
            duu = df * ug * s
            dug_ref[rows, :] = dug
            duu_ref[rows, :] = duu
            new = [dug * wing[k] for k in range(kw)] + [duu * winu[k] for k in range(kw)] + [dug, duu]
            return tuple(a + _fold8(v) for a, v in zip(acc, new))

        acc = lax.fori_loop(0, tb // CONV_RC, chunk, tuple(jnp.zeros((8, tc), F32) for _ in range(2 * kw + 2)))
        col = [jnp.sum(a, axis=0, keepdims=True) for a in acc]
        parts = (jnp.concatenate(col[:kw], axis=0), jnp.concatenate(col[kw:2 * kw], axis=0), col[2 * kw], col[2 * kw + 1])
        accs = (dwg_ref, dwu_ref, dbg_ref, dbu_ref)

        @pl.when(i == 0)
        def _():
            for r, p in zip(accs, parts):
                r[...] = p

        @pl.when(i > 0)
        def _():
            for r, p in zip(accs, parts):
                r[...] += p

    blk = lambda off: pl.BlockSpec((tb, tc), lambda j, i: (i, j + off))
    wsp = lambda off: pl.BlockSpec((kw, tc), lambda j, i: (0, j + off))
    bsp = lambda off: pl.BlockSpec((1, tc), lambda j, i: (0, j + off))
    return pl.pallas_call(
        body, name="glu_bwd_du", grid=(nc, T // tb),
        in_specs=[blk(0), _halo_prev(tb, tc), blk(nc), _halo_prev(tb, tc, nc), wsp(0), wsp(nc), bsp(0), bsp(nc), blk(0)],
        out_specs=[blk(0), blk(0), wsp(0), wsp(0), bsp(0), bsp(0)],
        out_shape=[jax.ShapeDtypeStruct((T, DFF), F32)] * 2 + [jax.ShapeDtypeStruct((kw, DFF), F32)] * 2
        + [jax.ShapeDtypeStruct((1, DFF), F32)] * 2,
        scratch_shapes=[pltpu.VMEM((tb + 8, tc), F32)] * 2, compiler_params=_cp("parallel", "arbitrary"))(
            up, up, up, up, w, w, b, b, df)


def _mix_fwd(rest, gate_bias, y_fox, y_gdn):
    T = rest.shape[0]
    tb, tc = _div(T, ROW_TILE, 8), 512
    nc = D // tc
    og = OFF_GATES // tc

    def body(gf, gg, bf, bg, yf, yg, o_ref):
        o_ref[...] = (_sig(gf[...] + bf[...]) * yf[...] + _sig(gg[...] + bg[...]) * yg[...]).astype(o_ref.dtype)

    blk = lambda off: pl.BlockSpec((tb, tc), lambda i, j: (i, j + off))
    bsp = lambda off: pl.BlockSpec((1, tc), lambda i, j: (0, j + off))
    return pl.pallas_call(
        body, name="mix_fwd", grid=(T // tb, nc),
        in_specs=[blk(og), blk(og + nc), bsp(0), bsp(nc), blk(0), blk(0)], out_specs=blk(0),
        out_shape=jax.ShapeDtypeStruct((T, D), _BF), compiler_params=_cp("parallel", "parallel"))(
            rest, rest, gate_bias, gate_bias, y_fox, y_gdn)


def _mix_bwd(rest, gate_bias, y_fox, y_gdn, dmix):
    T = rest.shape[0]
    tb, tc = _div(T, ROW_TILE, 8), 512
    nc = D // tc
    og = OFF_GATES // tc

    def body(gf, gg, bf, bg, yf, yg, dm, dyf, dyg, dgf, dgg, dbf, dbg):
        i = pl.program_id(1)
        d = dm[...]
        sf = _sig(gf[...] + bf[...])
        sg = _sig(gg[...] + bg[...])
        dyf[...] = (d * sf).astype(dyf.dtype)
        dyg[...] = (d * sg).astype(dyg.dtype)
        a = d * yf[...] * sf * (1.0 - sf)
        b = d * yg[...] * sg * (1.0 - sg)
        dgf[...] = a.astype(dgf.dtype)
        dgg[...] = b.astype(dgg.dtype)
        pa = jnp.sum(a, axis=0, keepdims=True)
        pb = jnp.sum(b, axis=0, keepdims=True)

        @pl.when(i == 0)
        def _():
            dbf[...] = pa
            dbg[...] = pb

        @pl.when(i > 0)
        def _():
            dbf[...] += pa
            dbg[...] += pb

    blk = lambda off: pl.BlockSpec((tb, tc), lambda j, i: (i, j + off))
    bsp = lambda off: pl.BlockSpec((1, tc), lambda j, i: (0, j + off))
    return pl.pallas_call(
        body, name="mix_bwd", grid=(nc, T // tb),
        in_specs=[blk(og), blk(og + nc), bsp(0), bsp(nc), blk(0), blk(0), blk(0)],
        out_specs=[blk(0), blk(0), blk(0), blk(0), bsp(0), bsp(0)],
        out_shape=[jax.ShapeDtypeStruct((T, D), _BF)] * 4 + [jax.ShapeDtypeStruct((1, D), F32)] * 2,
        compiler_params=_cp("parallel", "arbitrary"))(rest, rest, gate_bias, gate_bias, y_fox, y_gdn, dmix)


def _loss_head(h2, w, target, n_valid):
    T = h2.shape[0]
    tb = _div(T, ROW_TILE, 8)

    def body(h_ref, w_ref, t_ref, dh_ref, loss_ref, dw_ref):
        i = pl.program_id(0)
        x = h_ref[...]
        r = lax.rsqrt(jnp.mean(x * x, axis=-1, keepdims=True) + EPS)
        xh = x * r
        row = i * tb + lax.broadcasted_iota(jnp.int32, (tb, 1), 0)
        valid = (row >= X0) & (row < X0 + n_valid)
        e = jnp.where(valid, xh * w_ref[...] - t_ref[...], 0.0)
        dy = e * (1.0 / D)
        g = dy * w_ref[...]
        dh_ref[...] = r * (g - xh * jnp.mean(xh * g, axis=-1, keepdims=True))
        lpart = 0.5 * jnp.sum(jnp.sum(e * e, axis=-1, keepdims=True) * (1.0 / D), axis=0, keepdims=True)
        wpart = jnp.sum(dy * xh, axis=0, keepdims=True)

        @pl.when(i == 0)
        def _():
            loss_ref[...] = lpart
            dw_ref[...] = wpart

        @pl.when(i > 0)
        def _():
            loss_ref[...] += lpart
            dw_ref[...] += wpart

    row = pl.BlockSpec((tb, D), lambda i: (i, 0))
    one = pl.BlockSpec((1, D), lambda i: (0, 0))
    return pl.pallas_call(
        body, name="loss_head", grid=(T // tb,), in_specs=[row, one, row],
        out_specs=[row, pl.BlockSpec((1, 1), lambda i: (0, 0)), one],
        out_shape=[jax.ShapeDtypeStruct((T, D), F32), jax.ShapeDtypeStruct((1, 1), F32), jax.ShapeDtypeStruct((1, D), F32)],
        compiler_params=_cp("arbitrary"))(h2, w, target)


def _bd_lo(a, b, ca, cb):
    return lax.dot_general(a.astype(_BF), b.astype(_BF), (((ca,), (cb,)), ((0,), (0,))), preferred_element_type=F32)


def _split2(a):
    hi = a.astype(_BF)
    return hi, (a - hi.astype(F32)).astype(_BF)


def _bd_hi(a, b, ca, cb, exact_a=False):
    dn = (((ca,), (cb,)), ((0,), (0,)))
    dot = lambda x, y: lax.dot_general(x, y, dn, preferred_element_type=F32)
    bh, bl = _split2(b)
    if exact_a:
        ah = a.astype(_BF)
        return dot(ah, bh) + dot(ah, bl)
    ah, al = _split2(a)
    return dot(ah, bh) + (dot(ah, bl) + dot(al, bh))


def _row_to_col(x):
    eye = lax.broadcasted_iota(jnp.int32, (1, CH, CH), 1) == lax.broadcasted_iota(jnp.int32, (1, CH, CH), 2)
    return jnp.sum(jnp.where(eye, jnp.broadcast_to(x, (HEADS, CH, CH)), 0.0), axis=2, keepdims=True)


def _col_to_row(x):
    eye = lax.broadcasted_iota(jnp.int32, (1, CH, CH), 1) == lax.broadcasted_iota(jnp.int32, (1, CH, CH), 2)
    return jnp.sum(jnp.where(eye, jnp.broadcast_to(x, (HEADS, CH, CH)), 0.0), axis=1, keepdims=True)


def _gdn_chunk(q, k, v, bpre, apre, alog, dtb):
    H = HEADS
    r = lax.broadcasted_iota(jnp.int32, (1, CH, CH), 1)
    c = lax.broadcasted_iota(jnp.int32, (1, CH, CH), 2)
    tril, strict = r >= c, r > c
    lb = jnp.broadcast_to(tril.astype(F32), (H, CH, CH))
    rq = lax.rsqrt(jnp.sum(q * q, axis=-1, keepdims=True) + EPS)
    rk = lax.rsqrt(jnp.sum(k * k, axis=-1, keepdims=True) + EPS)
    qh = q * rq
    qn = qh * (HD ** -0.5)
    kn = k * rk
    beta = _sig(bpre)
    x = apre + dtb
    ea = jnp.exp(alog)
    g = -ea * jax.nn.softplus(x)
    gb = jnp.broadcast_to(g, (H, CH, HD))
    gc = _bd_hi(lb, gb, 2, 1, True)
    dm = gc - _col_to_row(gc[:, :, 0:1])
    decay = jnp.where(tril, jnp.exp(jnp.where(tril, dm, 0.0)), 0.0)
    eg = jnp.exp(gc)
    gl = gc[:, CH - 1:CH, :]
    egl = jnp.exp(gl - gc)
    cd = jnp.exp(gl)
    kb = kn * beta
    vb = v * beta
    both = _bd_lo(_rows2(kb, qn), kn, 2, 2)
    kk, qk = both[:, :CH], both[:, CH:]
    pw = -jnp.where(strict, kk * decay, 0.0)
    tm = jnp.where(r == c, 1.0, 0.0) + pw
    pw = _bd_hi(pw, pw, 2, 1)
    for it in range(4):
        mul = _bd_hi if it < 2 else _bd_lo
        both = mul(_rows2(pw, tm), pw, 2, 1)
        pw, tm = both[:, :CH], tm + both[:, CH:]
    tm = tm + _bd_lo(tm, pw, 2, 1)
    kbg = kb * eg
    sol = _bd_hi(tm, _lanes2(vb, kbg), 2, 1)
    value, kcd = sol[:, :, :HD], sol[:, :, HD:]
    attn = jnp.where(tril, qk * decay, 0.0)
    return dict(tril=tril, strict=strict, lb=lb, rq=rq, rk=rk, qh=qh, qn=qn, kn=kn, beta=beta, x=x, ea=ea, g=g,
                decay=decay, eg=eg, egl=egl, cd=cd, kb=kb, vb=vb, kk=kk, tm=tm, kbg=kbg, value=value, kcd=kcd, qk=qk,
                attn=attn, qd=qn * eg, kt=kn * egl, sol=sol)


def _rows2(a, b):
    return jnp.concatenate([a, b], axis=1)


def _lanes2(a, b):
    return jnp.concatenate([a, b], axis=2)


GDN_CPS = 2


def _gdn_specs(T, rev):
    G = GDN_CPS
    ns = T // (G * CH)
    pos = (lambda n: ns - 1 - n) if rev else (lambda n: n)
    mat = pl.BlockSpec((HEADS, G * CH, HD), lambda n: (0, pos(n), 0))
    col = pl.BlockSpec((G, HEADS, 1, CH), lambda n: (pos(n), 0, 0, 0))
    sca = pl.BlockSpec((HEADS, 1, 1), lambda n: (0, 0, 0))
    nw = pl.BlockSpec((1, 1, HD), lambda n: (0, 0, 0))
    st = pl.BlockSpec((G, HEADS, HD, HD), lambda n: (pos(n), 0, 0, 0))
    tok = lambda width, off: pl.BlockSpec((G * CH, width), lambda n: (pos(n), off))
    return ns, mat, col, sca, nw, st, tok


def _split_heads(x):
    return [jnp.stack([x[:, (g * HEADS + h) * HD:(g * HEADS + h + 1) * HD] for h in range(HEADS)])
            for g in range(x.shape[1] // FW)]


def _store_heads(ref, rows, g, val):
    for h in range(HEADS):
        ref[rows, (g * HEADS + h) * HD:(g * HEADS + h + 1) * HD] = val[h]


def _gdn_fwd(conv, rest, bpre, apre, alog, dtb, nw, gather=()):
    T = conv.shape[0]
    ns, mat, col, sca, nws, st, tok = _gdn_specs(T, False)
    rows = [slice(g * CH, (g + 1) * CH) for g in range(GDN_CPS)]
    ng = len(gather)

    def body(*refs):
        c_ref, z_ref, b_ref, a_ref, al_ref, dt_ref, nw_ref = refs[:7]
        o_ref, og_ref, st_ref = refs[7 + ng:10 + ng]
        s_scr = refs[10 + 2 * ng]
        comm = (refs[7:7 + ng], refs[10 + ng:10 + 2 * ng], refs[11 + 2 * ng:])

        @pl.when(pl.program_id(0) == 0)
        def _():
            s_scr[...] = jnp.zeros_like(s_scr)
            if ng:
                _gather_phase(*comm, True)

        locs = [_gdn_chunk(*_split_heads(c_ref[rows[g], :]), _row_to_col(b_ref[g]), _row_to_col(a_ref[g]), al_ref[...],
                           dt_ref[...]) for g in range(GDN_CPS)]
        s = s_scr[...]
        for g, L in enumerate(locs):
            st_ref[g] = s
            both = _bd_lo(_rows2(L["kcd"], L["qd"]), s, 2, 1)
            v_new = L["value"] - both[:, :CH]
            o = both[:, CH:] + _bd_lo(L["attn"], v_new, 2, 1)
            s = s * L["cd"] + _bd_lo(L["kt"], v_new, 1, 1)
            o_ref[:, rows[g], :] = o
            zz, = _split_heads(z_ref[rows[g], :])
            rs = lax.rsqrt(jnp.mean(o * o, axis=-1, keepdims=True) + EPS)
            _store_heads(og_ref, rows[g], 0, o * rs * nw_ref[...] * zz * _sig(zz))
        s_scr[...] = s
        if ng:
            pl.when(pl.program_id(0) == ns - 1)(functools.partial(_gather_phase, *comm, False))

    return pl.pallas_call(
        body, name="gdn_fwd", grid=(ns,),
        in_specs=[tok(3 * FW, 0), tok(FW, OFF_Z // FW), col, col, sca, sca, nws] + [HBM_SPEC] * ng,
        out_specs=[mat, tok(FW, 0), st] + [HBM_SPEC] * ng,
        out_shape=[jax.ShapeDtypeStruct((HEADS, T, HD), F32), jax.ShapeDtypeStruct((T, FW), F32),
                   jax.ShapeDtypeStruct((T // CH, HEADS, HD, HD), F32)] + _gather_shapes(gather),
        scratch_shapes=[pltpu.VMEM((HEADS, HD, HD), F32)] + (_gather_sems(ng) if ng else []),
        compiler_params=_cp("arbitrary"))(conv, rest, bpre, apre, alog, dtb, nw, *gather)


def _gdn_bwd(conv, rest, bpre, apre, alog, dtb, nw, states, o, dog, scatter=()):
    T = conv.shape[0]
    ns, mat, col, sca, nws, st, tok = _gdn_specs(T, True)
    rows = [slice(g * CH, (g + 1) * CH) for g in range(GDN_CPS)]
    ng = len(scatter)

    def body(*refs):
        c_ref, z_ref, b_ref, a_ref, al_ref, dt_ref, nw_ref, st_ref, o_ref, dog_ref = refs[:10]
        dc_ref, dz_ref, db_ref, da_ref, dal_ref, ddt_ref, dnw_ref = refs[10 + ng:17 + ng]
        ds_scr = refs[17 + 2 * ng]
        comm = (refs[10:10 + ng], refs[17 + ng:17 + 2 * ng], refs[18 + 2 * ng:])

        @pl.when(pl.program_id(0) == 0)
        def _():
            ds_scr[...] = jnp.zeros_like(ds_scr)
            dal_ref[...] = jnp.zeros_like(dal_ref)
            ddt_ref[...] = jnp.zeros_like(ddt_ref)
            dnw_ref[...] = jnp.zeros_like(dnw_ref)
            if ng:
                _scatter_phase(*comm, True)

        splits = [_split_heads(c_ref[rows[g], :]) for g in range(GDN_CPS)]
        locs = [_gdn_chunk(*splits[g], _row_to_col(b_ref[g]), _row_to_col(a_ref[g]), al_ref[...], dt_ref[...])
                for g in range(GDN_CPS)]
        dsn = ds_scr[...]
        acc_al, acc_dt, acc_nw = 0.0, 0.0, 0.0
        for g in reversed(range(GDN_CPS)):
            L, vv = locs[g], splits[g][2]
            tril, strict, lb = L["tril"], L["strict"], L["lb"]
            qn, kn, kb, beta, decay, eg, egl, cd = L["qn"], L["kn"], L["kb"], L["beta"], L["decay"], L["eg"], L["egl"], L["cd"]
            value, kcd, attn, qd, kt, tm = L["value"], L["kcd"], L["attn"], L["qd"], L["kt"], L["tm"]
            s = st_ref[g]
            v_new = value - _bd_lo(kcd, s, 2, 1)
            oo = o_ref[:, rows[g], :]
            zz, = _split_heads(z_ref[rows[g], :])
            dog_, = _split_heads(dog_ref[rows[g], :])
            sz = _sig(zz)
            rs = lax.rsqrt(jnp.mean(oo * oo, axis=-1, keepdims=True) + EPS)
            oh = oo * rs
            _store_heads(dz_ref, rows[g], 0, dog_ * oh * nw_ref[...] * sz * (1.0 + zz * (1.0 - sz)))
            don = dog_ * zz * sz
            gdy = don * nw_ref[...]
            do = rs * (gdy - oh * jnp.mean(oh * gdy, axis=-1, keepdims=True))
            acc_nw = acc_nw + jnp.sum(don * oh, axis=(0, 1), keepdims=True)
            d_vnew = _bd_lo(attn, do, 1, 1) + _bd_lo(kt, dsn, 2, 1)
            both = _bd_lo(_rows2(do, d_vnew), s, 2, 2)
            d_qd, d_kcd = both[:, :CH], -both[:, CH:]
            d_attn = jnp.where(tril, _bd_lo(do, v_new, 2, 2), 0.0)
            d_kt = _bd_lo(v_new, dsn, 2, 2)
            d_cd = jnp.sum(s * dsn, axis=(1, 2), keepdims=True)
            dsn_next = cd * dsn + _bd_lo(_rows2(qd, kcd), _rows2(do, -d_vnew), 1, 1)
            dsol = _bd_hi(tm, _lanes2(d_vnew, d_kcd), 1, 1)
            d_vb, d_kbg = dsol[:, :, :HD], dsol[:, :, HD:]
            da = -jnp.where(strict, _bd_lo(dsol, L["sol"], 2, 2), 0.0)
            dkk = da * decay
            dqk = d_attn * decay
            d_decay = da * L["kk"] + d_attn * L["qk"]
            both = _bd_lo(_rows2(dkk, dqk), kn, 2, 1)
            d_kb = both[:, :CH] + d_kbg * eg
            d_qn = both[:, CH:] + d_qd * eg
            d_kn = _bd_lo(_rows2(dkk, dqk), _rows2(kb, qn), 1, 1) + d_kt * egl
            dd = d_decay * decay
            d_gc = jnp.sum(dd + d_qd * qd + d_kbg * L["kbg"] - d_kt * kt, axis=-1, keepdims=True) \
                - _row_to_col(jnp.sum(dd, axis=1, keepdims=True))
            d_gl = jnp.sum(d_kt * kt, axis=(1, 2), keepdims=True) + d_cd * cd[:, :, 0:1]
            last = lax.broadcasted_iota(jnp.int32, (1, CH, 1), 1) == CH - 1
            d_gc = d_gc + jnp.where(last, d_gl, 0.0)
            dg = _bd_hi(lb, jnp.broadcast_to(d_gc, (HEADS, CH, HD)), 1, 1, True)[:, :, 0:1]
            d_apre = -dg * L["ea"] * _sig(L["x"])
            da_ref[g] = _col_to_row(d_apre)
            acc_al = acc_al + jnp.sum(dg * L["g"], axis=1, keepdims=True)
            acc_dt = acc_dt + jnp.sum(d_apre, axis=1, keepdims=True)
            d_beta = jnp.sum(d_kb * kn + d_vb * vv, axis=-1, keepdims=True)
            db_ref[g] = _col_to_row(d_beta * beta * (1.0 - beta))
            d_kn = d_kn + d_kb * beta
            qh = L["qh"]
            _store_heads(dc_ref, rows[g], 0, (HD ** -0.5) * L["rq"] * (d_qn - qh * jnp.sum(qh * d_qn, axis=-1, keepdims=True)))
            _store_heads(dc_ref, rows[g], 1, L["rk"] * (d_kn - kn * jnp.sum(kn * d_kn, axis=-1, keepdims=True)))
            _store_heads(dc_ref, rows[g], 2, d_vb * beta)
            dsn = dsn_next
        ds_scr[...] = dsn
        dal_ref[...] += acc_al
        ddt_ref[...] += acc_dt
        dnw_ref[...] += acc_nw
        if ng:
            pl.when(pl.program_id(0) == ns - 1)(functools.partial(_scatter_phase, *comm, False))

    c3 = jax.ShapeDtypeStruct((T // CH, HEADS, 1, CH), F32)
    s3 = jax.ShapeDtypeStruct((HEADS, 1, 1), F32)
    return pl.pallas_call(
        body, name="gdn_bwd", grid=(ns,),
        in_specs=[tok(3 * FW, 0), tok(FW, OFF_Z // FW), col, col, sca, sca, nws, st, mat, tok(FW, 0)] + [HBM_SPEC] * ng,
        out_specs=[tok(3 * FW, 0), tok(FW, 0), col, col, sca, sca, nws] + [HBM_SPEC] * ng,
        out_shape=[jax.ShapeDtypeStruct((T, 3 * FW), F32), jax.ShapeDtypeStruct((T, FW), F32), c3, c3, s3, s3,
                   jax.ShapeDtypeStruct((1, 1, HD), F32)] + _scatter_shapes(scatter),
        scratch_shapes=[pltpu.VMEM((HEADS, HD, HD), F32)] + (_scatter_sems(ng) if ng else []),
        compiler_params=_cp("arbitrary"))(conv, rest, bpre, apre, alog, dtb, nw, states, o, dog, *scatter)


LATE_KEYS = {"w_branch_fox": "w_bfox", "w_branch_gdn": "w_bgdn", "w_out": "w_out", "ffn_w_up": "w_up", "ffn_w_down": "w_down"}


def _local_step(x, target, w, late=None):
    seq = x.shape[0]
    T = -(-(X0 + seq) // T_ALIGN) * T_ALIGN
    back = T - X0 - seq
    tb_att = _div(T, ATT_TILE)
    h0 = jnp.concatenate([jnp.zeros((FRONT, D), F32), w["meta"], x, jnp.zeros((back, D), F32)], axis=0)
    tgt = jnp.concatenate([jnp.zeros((X0, D), F32), target, jnp.zeros((back, D), F32)], axis=0)
    row = lambda v: v.reshape(1, -1)
    nmix, nffn, nfin = row(w["norm_mix"]), row(w["norm_ffn"]), row(w["norm_final"])
    gate_b = row(w["gate_bias"])
    fconv_b = row(w["ffn_conv_b"])
    bias128 = jnp.zeros((1, SMALL_W), F32).at[0, :HEADS].set(w["fgt_bias"])
    alog = w["a_log"].reshape(HEADS, 1, 1)
    dtb = w["dt_bias"].reshape(HEADS, 1, 1)
    gnw = w["gdn_norm"].reshape(1, 1, HD)

    a = _rms_fwd(h0, nmix, "rms_mix")
    pf = _mm(a, w["w_fox"], out_dtype=_BF, name="proj_fox")
    rest = _mm(a, w["w_rest"], name="proj_rest")
    small = rest[:, OFF_SMALL:]
    c8 = _fgate_fwd(small, bias128)[:, :HEADS]
    c_cols, c_rows = _att_cols(c8), _att_rows(c8, tb_att)
    ones_blk = jnp.asarray(np.kron(np.eye(HEADS, dtype=np.float32), np.ones((HD, HD), np.float32)))
    plan = _att_plan(_att_stats(pf, c_cols, ones_blk))
    o_fox, lse, lse2 = _fox_fwd(pf, c_cols, c_rows, plan)
    conv = _gconv_fwd(rest, w["gdn_conv"])
    chunk_rows = lambda a8: a8.reshape(T // CH, CH, HEADS).transpose(0, 2, 1)[:, :, None, :]
    bpre = chunk_rows(small[:, HEADS:2 * HEADS])
    apre = chunk_rows(small[:, 2 * HEADS:3 * HEADS])
    if late is None:
        o_raw, og, states = _gdn_fwd(conv, rest, bpre, apre, alog, dtb, gnw)
    else:
        o_raw, og, states, *got = _gdn_fwd(conv, rest, bpre, apre, alog, dtb, gnw, gather=list(late.values()))
        w = dict(w, **{LATE_KEYS[n]: _from_chips(g, AXIS[n]) for n, g in zip(late, got)})
    w_up, w_down = w["w_up"], w["w_down"]
    y_fox = _mm(o_fox, w["w_bfox"], name="y_fox")
    y_gdn = _mm(og, w["w_bgdn"], name="y_gdn")
    mix = _mix_fwd(rest, gate_b, y_fox, y_gdn)
    h1 = _mm(mix, w["w_out"], resid=h0, name="out_proj")
    b = _rms_fwd(h1, nffn, "rms_ffn")
    up = _mm(b, w_up, name="ffn_up")
    f = _glu_fwd(up, w["ffn_conv"], fconv_b)
    h2 = _mm(f, w_down, resid=h1, name="ffn_down")
    dh2, loss, d_nfin = _loss_head(h2, nfin, tgt, seq)

    d_f = _mm(dh2, w_down, tb=True, name="d_f")
    g_down = _mm_t(f, dh2, "g_down")
    dug, duu, dwg, dwu, dbg, dbu = _glu_bwd_du(up, w["ffn_conv"], fconv_b, d_f)
    dxg = _conv_bwd_dx(dug, w["ffn_conv"][:, :DFF], _BF, "fconv_dx_gate")
    dxu = _conv_bwd_dx(duu, w["ffn_conv"][:, DFF:], _BF, "fconv_dx_up")
    d_b = _mm(dxg, w_up[:, :DFF], tb=True, name="d_b_gate")
    d_b = _mm(dxu, w_up[:, DFF:], tb=True, resid=d_b, name="d_b_up")
    g_up = jnp.concatenate([_mm_t(b, dxg, "g_up_gate"), _mm_t(b, dxu, "g_up_up")], axis=1)
    dh1, d_nffn = _rms_bwd(h1, nffn, d_b, dh2, "rms_ffn_bwd")

    dmix = _mm(dh1, w["w_out"], tb=True, name="d_mix")
    g_out = _mm_t(mix, dh1, "g_out")
    dyf, dyg, dgf, dgg, dgbf, dgbg = _mix_bwd(rest, gate_b, y_fox, y_gdn, dmix)
    do_fox = _mm(dyf, w["w_bfox"], tb=True, out_dtype=_BF, name="d_o_fox")
    g_bfox = _mm_t(o_fox, dyf, "g_bfox")
    d_og = _mm(dyg, w["w_bgdn"], tb=True, name="d_o_gdn")
    g_bgdn = _mm_t(og, dyg, "g_bgdn")

    ones_col = jnp.asarray(np.kron(np.eye(HEADS, SMALL_W, dtype=np.float32), np.ones((HD, 1), np.float32)))
    delta, delta8 = _head_dot(do_fox, o_fox, ones_blk, ones_col)
    dq, dcq = _fox_bwd_dq(pf, do_fox, c_cols, c_rows, lse, delta, plan)
    dk, dv, dck = _fox_bwd_dkv(pf, do_fox, c_cols, _att_rows(c8 - _from_pairs(lse2), tb_att),
                               _att_rows(delta8[:, :HEADS], tb_att), plan)
    dc = jnp.pad(_from_pairs(dcq + dck), ((0, 0), (0, SMALL_W - HEADS)))
    dfp, d_fb = _fgate_bwd(small, bias128, dc)

    if late is None:
        late_recv = None
        dconv, dz, dbp, dap, d_alog, d_dtb, d_gnw = _gdn_bwd(conv, rest, bpre, apre, alog, dtb, gnw, states, o_raw, d_og)
    else:
        ready = dict(w_branch_fox=g_bfox, w_branch_gdn=g_bgdn, w_out=g_out, ffn_w_up=g_up, ffn_w_down=g_down)
        send = [_by_chip(ready[n].astype(jnp.bfloat16), AXIS[n]) for n in late]
        dconv, dz, dbp, dap, d_alog, d_dtb, d_gnw, *late_recv = _gdn_bwd(
            conv, rest, bpre, apre, alog, dtb, gnw, states, o_raw, d_og, scatter=send)
    du_g, g_gconv = _gconv_bwd_du(rest, w["gdn_conv"], dconv)
    dgx = _conv_bwd_dx(du_g, w["gdn_conv"], _BF, "gconv_dx")
    token_rows = lambda a: a[:, :, 0, :].transpose(0, 2, 1).reshape(T, HEADS)
    dsmall = jnp.concatenate([dfp[:, :HEADS], token_rows(dbp), token_rows(dap),
                              jnp.zeros((T, SMALL_W - 3 * HEADS), F32)], axis=1)
    drest = jnp.concatenate([dgx, dz.astype(_BF), dgf, dgg, dsmall.astype(_BF)], axis=1)
    dfox = jnp.concatenate([dq, dk, dv], axis=1)
    g_fox = _mm_t(a, dfox, "g_w_fox")
    g_rest = _mm_t(a, drest, "g_w_rest")
    sm = lambda lo: g_rest[:, OFF_SMALL + lo:OFF_SMALL + lo + HEADS]
    g_w_in = jnp.concatenate([g_fox, sm(0), g_rest[:, :3 * FW], g_rest[:, OFF_Z:OFF_Z + FW], sm(HEADS), sm(2 * HEADS),
                              g_rest[:, OFF_GATES:OFF_GATES + 2 * D]], axis=1)
    d_a = _mm(dfox, w["w_fox"], tb=True, name="d_a_fox")
    if late is None:
        d_a = _mm(drest, w["w_rest"], tb=True, resid=d_a, name="d_a_rest")
    else:
        d_a, w_in_recv = _mm(drest, w["w_rest"], tb=True, resid=d_a, name="d_a_rest",
                             scatter=[_by_chip(g_w_in.astype(jnp.bfloat16), AXIS["w_in"])])
        late_recv = [w_in_recv] + late_recv
    dh0, d_nmix = _rms_bwd(h0, nmix, d_a, dh1, "rms_mix_bwd")

    grads = dict(
        meta_tokens=dh0[FRONT:X0], w_in=g_w_in, fgt_bias=d_fb[0, :HEADS], gdn_conv_w=g_gconv,
        gdn_a_log=d_alog.reshape(HEADS), gdn_dt_bias=d_dtb.reshape(HEADS), gdn_norm_w=d_gnw.reshape(HD),
        gate_bias=jnp.concatenate([dgbf, dgbg], axis=1).reshape(2 * D), w_branch_fox=g_bfox, w_branch_gdn=g_bgdn,
        w_out=g_out, norm_mix_w=d_nmix.reshape(D), norm_ffn_w=d_nffn.reshape(D), ffn_w_up=g_up,
        ffn_conv_w=jnp.concatenate([dwg, dwu], axis=1), ffn_conv_b=jnp.concatenate([dbg, dbu], axis=1).reshape(2 * DFF),
        ffn_w_down=g_down, norm_final_w=d_nfin.reshape(D), late_recv=late_recv)
    return loss, dh0[X0:X0 + seq], grads


N_CHIPS = 4
PACK_W = 1024
PACK_ROW_ALIGN = 32
BIG_EARLY = ("w_in",)
BIG_LATE = ("w_branch_fox", "w_branch_gdn", "w_out", "ffn_w_up", "ffn_w_down")
BIG = BIG_EARLY + BIG_LATE
WEIGHTS = (
    ("meta_tokens", (N_META, D), 1), ("w_in", (D, 3 * FW + HEADS + 4 * FW + 2 * HEADS + 2 * D), 1), ("fgt_bias", (1, HEADS), None),
    ("gdn_conv_w", (4, 3 * FW), 1), ("gdn_a_log", (1, HEADS), None), ("gdn_dt_bias", (1, HEADS), None),
    ("gdn_norm_w", (1, HD), None), ("gate_bias", (1, 2 * D), None), ("w_branch_fox", (FW, D), 1),
    ("w_branch_gdn", (FW, D), 1), ("w_out", (D, D), 0), ("norm_mix_w", (1, D), None), ("norm_ffn_w", (1, D), None),
    ("ffn_w_up", (D, 2 * DFF), 1), ("ffn_conv_w", (3, 2 * DFF), 1), ("ffn_conv_b", (1, 2 * DFF), None),
    ("ffn_w_down", (DFF, D), 0), ("norm_final_w", (1, D), None))
SPLIT_F32 = ("meta_tokens", "gdn_conv_w", "ffn_conv_w")


def _shard_shape(shape, axis):
    if axis is None:
        return shape
    return tuple(s // N_CHIPS if a == axis else s for a, s in enumerate(shape))


def _shard_of(full, axis, q):
    if axis is None:
        return full
    n = full.shape[axis] // N_CHIPS
    return lax.slice_in_dim(full, q * n, (q + 1) * n, axis=axis)


def _pack_rows(n_elems):
    rows = -(-n_elems // PACK_W)
    return -(-rows // PACK_ROW_ALIGN) * PACK_ROW_ALIGN


def _pack(pieces, dtype):
    flat = jnp.concatenate([p.reshape(-1).astype(dtype) for p in pieces])
    rows = _pack_rows(flat.shape[0])
    return jnp.pad(flat, (0, rows * PACK_W - flat.shape[0])).reshape(rows, PACK_W)


def _unpack(slab, shapes):
    flat = slab.reshape(-1)
    out, off = [], 0
    for s in shapes:
        n = int(np.prod(s))
        out.append(flat[off:off + n].reshape(s))
        off += n
    return out


HBM_SPEC = pl.BlockSpec(memory_space=pltpu.HBM)
MESH_ID = pl.DeviceIdType.MESH


def _scatter_chips(srcs, name):
    n = len(srcs)

    def body(*refs):
        _scatter_phase(refs[:n], refs[n:2 * n], refs[2 * n:], True)
        _scatter_phase(refs[:n], refs[n:2 * n], refs[2 * n:], False)

    return pl.pallas_call(body, name=name, in_specs=[HBM_SPEC] * n, out_specs=[HBM_SPEC] * n,
                          out_shape=_scatter_shapes(srcs), scratch_shapes=_scatter_sems(n))(*srcs)


def _scatter_shapes(srcs):
    return [jax.ShapeDtypeStruct(s.shape, s.dtype) for s in srcs]


def _scatter_sems(n):
    return [pltpu.SemaphoreType.DMA((3 * n,)), pltpu.SemaphoreType.DMA((3 * n,)), pltpu.SemaphoreType.DMA((n,))]


def _scatter_phase(src_refs, out_refs, sems, issue):
    n = len(src_refs)
    send_sems, recv_sems, local_sems = sems
    x, y, c = lax.axis_index("x"), lax.axis_index("y"), lax.axis_index("c")
    q = 2 * x + y
    peers = [(1 - x, y), (x, 1 - y), (1 - x, 1 - y)]

    def remote(a, k, src_slot, dst_slot):
        px, py = peers[k]
        return pltpu.make_async_remote_copy(
            src_ref=src_refs[a].at[src_slot], dst_ref=out_refs[a].at[dst_slot], send_sem=send_sems.at[3 * a + k],
            recv_sem=recv_sems.at[3 * a + k], device_id=(px, py, c), device_id_type=MESH_ID)

    mine = [pltpu.make_async_copy(src_refs[a].at[q], out_refs[a].at[q], local_sems.at[a]) for a in range(n)]
    sends = [remote(a, k, 2 * px + py, q) for a in range(n) for k, (px, py) in enumerate(peers)]
    if issue:
        for cp in mine + sends:
            cp.start()
        return
    for a in range(n):
        for k, (px, py) in enumerate(peers):
            remote(a, k, 0, 2 * px + py).wait_recv()
    for cp in sends:
        cp.wait_send()
    for cp in mine:
        cp.wait()


def _gather_chips(srcs, name):
    n = len(srcs)

    def body(*refs):
        _gather_phase(refs[:n], refs[n:2 * n], refs[2 * n:], True)
        _gather_phase(refs[:n], refs[n:2 * n], refs[2 * n:], False)

    return pl.pallas_call(body, name=name, in_specs=[HBM_SPEC] * n, out_specs=[HBM_SPEC] * n,
                          out_shape=_gather_shapes(srcs), scratch_shapes=_gather_sems(n))(*srcs)


def _gather_shapes(srcs):
    return [jax.ShapeDtypeStruct((N_CHIPS,) + s.shape, s.dtype) for s in srcs]


def _gather_sems(n):
    return [pltpu.SemaphoreType.DMA((3 * n,))] * 4 + [pltpu.SemaphoreType.DMA((n,))]


def _gather_phase(src_refs, out_refs, sems, issue):
    n = len(src_refs)
    ici_send, ici_recv, d2d_send, d2d_recv, local_sems = sems
    x, y, c = lax.axis_index("x"), lax.axis_index("y"), lax.axis_index("c")
    q = 2 * x + y
    peers = [(1 - x, y), (x, 1 - y), (1 - x, 1 - y)]

    def half(a, which):
        r = src_refs[a].shape[0] // 2
        return pl.ds(which * r, r)

    def ici(a, k, slot):
        px, py = peers[k]
        return pltpu.make_async_remote_copy(
            src_ref=src_refs[a].at[half(a, c)], dst_ref=out_refs[a].at[slot, half(a, c)], send_sem=ici_send.at[3 * a + k],
            recv_sem=ici_recv.at[3 * a + k], device_id=(px, py, c), device_id_type=MESH_ID)

    def d2d(a, k, which):
        px, py = peers[k]
        rows = out_refs[a].at[2 * px + py, half(a, which)]
        return pltpu.make_async_remote_copy(
            src_ref=rows, dst_ref=rows, send_sem=d2d_send.at[3 * a + k], recv_sem=d2d_recv.at[3 * a + k],
            device_id=(x, y, 1 - c), device_id_type=MESH_ID)

    mine = [pltpu.make_async_copy(src_refs[a], out_refs[a].at[q], local_sems.at[a]) for a in range(n)]
    sends = [ici(a, k, q) for a in range(n) for k in range(3)]
    if issue:
        for cp in mine + sends:
            cp.start()
        return
    passed = []
    for a in range(n):
        for k, (px, py) in enumerate(peers):
            ici(a, k, 2 * px + py).wait_recv()
            passed.append(d2d(a, k, c))
            passed[-1].start()
    for a in range(n):
        for k in range(3):
            d2d(a, k, 1 - c).wait_recv()
    for cp in sends + passed:
        cp.wait_send()
    for cp in mine:
        cp.wait()


def _sibling_swap(slabs, name):
    n = len(slabs)

    def body(*refs):
        src_refs, out_refs, send_sems, recv_sems = refs[:n], refs[n:2 * n], refs[2 * n], refs[2 * n + 1]
        x, y, c = lax.axis_index("x"), lax.axis_index("y"), lax.axis_index("c")
        cps = [pltpu.make_async_remote_copy(src_ref=src_refs[a], dst_ref=out_refs[a], send_sem=send_sems.at[a],
                                            recv_sem=recv_sems.at[a], device_id=(x, y, 1 - c), device_id_type=MESH_ID)
               for a in range(n)]
        for cp in cps:
            cp.start()
        for cp in cps:
            cp.wait_recv()
        for cp in cps:
            cp.wait_send()

    return pl.pallas_call(
        body, name=name, in_specs=[HBM_SPEC] * n, out_specs=[HBM_SPEC] * n,
        out_shape=[jax.ShapeDtypeStruct(s.shape, s.dtype) for s in slabs],
        scratch_shapes=[pltpu.SemaphoreType.DMA((n,)), pltpu.SemaphoreType.DMA((n,))])(*slabs)


def _sum_chips(r, name):
    rows, cols = r.shape[1:]
    tb = _div(rows, 256, 16)

    def body(r0, r1, r2, r3, o_ref):
        o_ref[...] = ((r0[0].astype(F32) + r1[0].astype(F32)) + r2[0].astype(F32)) + r3[0].astype(F32)

    spec = lambda j: pl.BlockSpec((1, tb, cols), lambda i: (j, i, 0))
    return pl.pallas_call(
        body, name=name, grid=(rows // tb,), in_specs=[spec(0), spec(1), spec(2), spec(3)],
        out_specs=pl.BlockSpec((tb, cols), lambda i: (i, 0)), out_shape=jax.ShapeDtypeStruct((rows, cols), F32),
        compiler_params=_cp("parallel"))(r, r, r, r)


def _adamw(w, m, v, p, q, name):
    rows, cols = w.shape
    tb = _div(rows, 256, 8)

    def body(w_ref, m_ref, v_ref, p_ref, q_ref, g_ref, d_ref, nm_ref, nv_ref):
        g = p_ref[...] + q_ref[...]
        m_new = B1 * m_ref[...] + (1.0 - B1) * g
        v_new = B2 * v_ref[...] + (1.0 - B2) * (g * g)
        g_ref[...] = g
        nm_ref[...] = m_new
        nv_ref[...] = v_new
        m_hat = m_new / (1.0 - B1 ** STEP)
        v_hat = v_new / (1.0 - B2 ** STEP)
        d_ref[...] = -LR * (m_hat / (jnp.sqrt(v_hat) + AEPS) + WD * w_ref[...])

    spec = pl.BlockSpec((tb, cols), lambda i: (i, 0))
    return pl.pallas_call(
        body, name=name, grid=(rows // tb,), in_specs=[spec] * 5, out_specs=[spec] * 4,
        out_shape=[jax.ShapeDtypeStruct((rows, cols), F32)] * 4, compiler_params=_cp("parallel"))(w, m, v, p, q)


def _split_w_in(w_in):
    o1 = 3 * FW
    o2 = o1 + HEADS
    o3 = o2 + 3 * FW
    o4 = o3 + FW
    o5 = o4 + HEADS
    o6 = o5 + HEADS
    pad = jnp.zeros((w_in.shape[0], SMALL_W - 3 * HEADS), w_in.dtype)
    rest = jnp.concatenate([w_in[:, o2:o3], w_in[:, o3:o4], w_in[:, o6:], w_in[:, o1:o2], w_in[:, o4:o5], w_in[:, o5:o6], pad],
                           axis=1)
    return w_in[:, :o1], rest


AXIS = {n: a for n, _, a in WEIGHTS}
SMALL = tuple(n for n, _, _ in WEIGHTS if n not in BIG)


def _by_chip(full, axis):
    rows, cols = full.shape
    if axis == 0:
        return full.reshape(N_CHIPS, rows // N_CHIPS, cols)
    return full.reshape(rows, N_CHIPS, cols // N_CHIPS).transpose(1, 0, 2)


def _from_chips(parts, axis):
    _, r, c = parts.shape
    if axis == 0:
        return parts.reshape(N_CHIPS * r, c)
    return parts.transpose(1, 0, 2).reshape(r, N_CHIPS * c)


def _gather_weights(shards):
    hi = {n: shards[n].astype(jnp.bfloat16) for n in SPLIT_F32}
    lo = [(shards[n] - hi[n].astype(F32)).astype(jnp.bfloat16) for n in SPLIT_F32]
    slab = _pack([hi[n] for n in SPLIT_F32] + lo, jnp.bfloat16)
    got = _gather_chips([shards[n].astype(jnp.bfloat16) for n in BIG_EARLY] + [slab], "gather_weights")
    full = {n: _from_chips(g, AXIS[n]) for n, g in zip(BIG_EARLY, got)}
    shapes = [shards[n].shape for n in SPLIT_F32] * 2
    per_chip = [_unpack(got[-1][j], shapes) for j in range(N_CHIPS)]
    for i, n in enumerate(SPLIT_F32):
        join = lambda off: jnp.concatenate([per_chip[j][off + i] for j in range(N_CHIPS)], axis=1).astype(F32)
        full[n] = join(0) + join(len(SPLIT_F32))
    return full


def kernel(x, meta_tokens, w_in, fgt_bias, gdn_conv_w, gdn_a_log, gdn_dt_bias, gdn_norm_w, gate_bias, w_branch_fox, w_branch_gdn, w_out, norm_mix_w, norm_ffn_w, ffn_w_up, ffn_conv_w, ffn_conv_b, ffn_w_down, norm_final_w, loss_target, m_meta_tokens, m_w_in, m_fgt_bias, m_gdn_conv_w, m_gdn_a_log, m_gdn_dt_bias, m_gdn_norm_w, m_gate_bias, m_w_branch_fox, m_w_branch_gdn, m_w_out, m_norm_mix_w, m_norm_ffn_w, m_ffn_w_up, m_ffn_conv_w, m_ffn_conv_b, m_ffn_w_down, m_norm_final_w, v_meta_tokens, v_w_in, v_fgt_bias, v_gdn_conv_w, v_gdn_a_log, v_gdn_dt_bias, v_gdn_norm_w, v_gate_bias, v_w_branch_fox, v_w_branch_gdn, v_w_out, v_norm_mix_w, v_norm_ffn_w, v_ffn_w_up, v_ffn_conv_w, v_ffn_conv_b, v_ffn_w_down, v_norm_final_w):
    weights = dict(meta_tokens=meta_tokens, w_in=w_in, fgt_bias=fgt_bias, gdn_conv_w=gdn_conv_w, gdn_a_log=gdn_a_log, gdn_dt_bias=gdn_dt_bias, gdn_norm_w=gdn_norm_w, gate_bias=gate_bias, w_branch_fox=w_branch_fox, w_branch_gdn=w_branch_gdn, w_out=w_out, norm_mix_w=norm_mix_w, norm_ffn_w=norm_ffn_w, ffn_w_up=ffn_w_up, ffn_conv_w=ffn_conv_w, ffn_conv_b=ffn_conv_b, ffn_w_down=ffn_w_down, norm_final_w=norm_final_w)
    m_in = dict(meta_tokens=m_meta_tokens, w_in=m_w_in, fgt_bias=m_fgt_bias, gdn_conv_w=m_gdn_conv_w, gdn_a_log=m_gdn_a_log, gdn_dt_bias=m_gdn_dt_bias, gdn_norm_w=m_gdn_norm_w, gate_bias=m_gate_bias, w_branch_fox=m_w_branch_fox, w_branch_gdn=m_w_branch_gdn, w_out=m_w_out, norm_mix_w=m_norm_mix_w, norm_ffn_w=m_norm_ffn_w, ffn_w_up=m_ffn_w_up, ffn_conv_w=m_ffn_conv_w, ffn_conv_b=m_ffn_conv_b, ffn_w_down=m_ffn_w_down, norm_final_w=m_norm_final_w)
    v_in = dict(meta_tokens=v_meta_tokens, w_in=v_w_in, fgt_bias=v_fgt_bias, gdn_conv_w=v_gdn_conv_w, gdn_a_log=v_gdn_a_log, gdn_dt_bias=v_gdn_dt_bias, gdn_norm_w=v_gdn_norm_w, gate_bias=v_gate_bias, w_branch_fox=v_w_branch_fox, w_branch_gdn=v_w_branch_gdn, w_out=v_w_out, norm_mix_w=v_norm_mix_w, norm_ffn_w=v_norm_ffn_w, ffn_w_up=v_ffn_w_up, ffn_conv_w=v_ffn_conv_w, ffn_conv_b=v_ffn_conv_b, ffn_w_down=v_ffn_w_down, norm_final_w=v_norm_final_w)
    shard2d = {n: _shard_shape(s, a) for n, s, a in WEIGHTS}
    as2d = lambda d: {n: d[n].reshape(shard2d[n]) for n, _, _ in WEIGHTS}
    w2, m2, v2 = as2d(weights), as2d(m_in), as2d(v_in)

    full = _gather_weights(w2)
    w_fox, w_rest = _split_w_in(full["w_in"])
    flat = lambda n: w2[n].reshape(-1)
    local_w = dict(
        meta=full["meta_tokens"], w_fox=w_fox, w_rest=w_rest, fgt_bias=flat("fgt_bias"), gdn_conv=full["gdn_conv_w"],
        a_log=flat("gdn_a_log"), dt_bias=flat("gdn_dt_bias"), gdn_norm=flat("gdn_norm_w"), gate_bias=flat("gate_bias"),
        norm_mix=flat("norm_mix_w"), norm_ffn=flat("norm_ffn_w"), ffn_conv=full["ffn_conv_w"], ffn_conv_b=flat("ffn_conv_b"),
        norm_final=flat("norm_final_w"))

    loss, grad_x, grads = _local_step(x[0], loss_target[0], local_w, {n: w2[n].astype(jnp.bfloat16) for n in BIG_LATE})

    g2 = {n: grads[n].reshape(s) for n, s, _ in WEIGHTS}
    small_send = jnp.stack([_pack([_shard_of(g2[n], AXIS[n], j) for n in SMALL], F32) for j in range(N_CHIPS)])
    recv = list(grads["late_recv"]) + list(_scatter_chips([small_send], "scatter_grads"))
    parts = [_sum_chips(r, "sum_" + n) for r, n in zip(recv, BIG + ("small",))]
    others = _sibling_swap(parts, "swap_grads")
    slab = lambda d: _pack([d[n] for n in SMALL], F32)
    state = [(w2[n], m2[n], v2[n]) for n in BIG] + [(slab(w2), slab(m2), slab(v2))]
    outs = [_adamw(w, m, v, p, q, "adamw_" + n) for (w, m, v), p, q, n in zip(state, parts, others, BIG + ("small",))]
    small = [_unpack(o, [weights[n].shape for n in SMALL]) for o in outs[-1]]
    result = []
    for kind in range(4):
        by_name = {n: outs[i][kind].reshape(weights[n].shape) for i, n in enumerate(BIG)}
        by_name.update(zip(SMALL, small[kind]))
        result += [by_name[n] for n, _, _ in WEIGHTS]
    total = lax.psum(loss[0, 0], ("x", "y", "c"))
    return (total, grad_x[None], *result)
```

```python
import functools

import numpy as np
import jax
import jax.numpy as jnp
from jax import lax
from jax.experimental import pallas as pl
from jax.experimental.pallas import tpu as pltpu

F32 = jnp.float32
_BF = jnp.bfloat16
HI = lax.Precision.HIGHEST

D = 1024
N_META = 16
CH = 64
FRONT = CH - N_META
X0 = CH
HEADS = 8
HD = 64
FW = HEADS * HD
DFF = 2816
EPS = 1e-6
NEG = -1e30
T_ALIGN = 256
SMALL_W = 128
REST_W = 3 * FW + FW + 2 * D + SMALL_W
OFF_Z = 3 * FW
OFF_GATES = 4 * FW
OFF_SMALL = 4 * FW + 2 * D

LR, B1, B2, AEPS, WD, STEP = 0.001, 0.9, 0.999, 1e-08, 0.01, 10

VMEM_LIMIT = 56 * 1024 * 1024
ROW_TILE = 640
MM_TM, MM_TN, MM_TK = 1280, 512, 2816
ATT_TILE = 640


def _div(n, target, mult=128):
    if n <= target:
        return n
    best = None
    for d in range(mult, target + 1, mult):
        if n % d == 0:
            best = d
    assert best is not None, (n, target, mult)
    return best


def _cp(*sem):
    return pltpu.CompilerParams(dimension_semantics=sem, vmem_limit_bytes=VMEM_LIMIT)


def _sig(x):
    return 0.5 * jnp.tanh(0.5 * x) + 0.5


MM_VMEM_BUDGET = 40 * 1024 * 1024
MM_STEP_BYTES = 1 << 20


def _mm_tiles(m, n, k, sa, sb, so, has_resid):
    divs = lambda d, cap: [t for t in range(128, min(d, cap) + 1, 128) if d % t == 0] or [d]
    best = None
    for tm in divs(m, MM_TM * 2):
        for tn in divs(n, 4608):
            fixed = tm * tn * (4 + 2 * so + (8 if has_resid else 0))
            tks = [t for t in divs(k, MM_TK) if 2 * t * (tm * sa + tn * sb) + fixed <= MM_VMEM_BUDGET]
            if not tks:
                continue
            tk = tks[-1]
            steps = (m // tm) * (n // tn) * (k // tk)
            rmw = (k // tk - 1) * m * n * 4
            cost = (n // tn) * m * k * sa + (m // tm) * k * n * sb + steps * MM_STEP_BYTES + rmw
            if best is None or cost < best[0]:
                best = (cost, tm, tn, tk)
    assert best is not None, (m, n, k)
    return best[1:]


def _mm_t(a, b, name):
    return _mm(a, b, ta=True, name=name)


def _mm(a, b, *, ta=False, tb=False, out_dtype=F32, resid=None, name, scatter=()):
    K, M = a.shape if ta else a.shape[::-1]
    N = b.shape[0] if tb else b.shape[1]
    assert (b.shape[1] if tb else b.shape[0]) == K
    tm, tn, tk = _mm_tiles(M, N, K, a.dtype.itemsize, b.dtype.itemsize, jnp.dtype(out_dtype).itemsize, resid is not None)
    nk = K // tk
    grid = (M // tm, N // tn, nk)
    dims = (((0 if ta else 1,), (1 if tb else 0,)), ((), ()))
    mxu = _BF
    nr = 1 if resid is not None else 0
    ng = len(scatter)

    def body(*refs):
        a_ref, b_ref = refs[:2]
        r_ref = refs[2] if nr else None
        o_ref = refs[2 + nr + ng]
        acc = refs[3 + nr + 2 * ng]
        comm = (refs[2 + nr:2 + nr + ng], refs[3 + nr + ng:3 + nr + 2 * ng], refs[4 + nr + 2 * ng:])
        k = pl.program_id(2)
        if ng:
            step = (pl.program_id(0) * grid[1] + pl.program_id(1)) * grid[2] + k
            pl.when(step == 0)(functools.partial(_scatter_phase, *comm, True))
        part = lax.dot_general(a_ref[...].astype(mxu), b_ref[...].astype(mxu), dims, preferred_element_type=F32)

        @pl.when(k == 0)
        def _():
            acc[...] = part

        @pl.when(k > 0)
        def _():
            acc[...] += part

        @pl.when(k == nk - 1)
        def _():
            r = acc[...]
            if r_ref is not None:
                r = r + r_ref[...]
            o_ref[...] = r.astype(o_ref.dtype)

        if ng:
            pl.when(step == grid[0] * grid[1] * grid[2] - 1)(functools.partial(_scatter_phase, *comm, False))

    a_spec = pl.BlockSpec((tk, tm), lambda i, j, k: (k, i)) if ta else pl.BlockSpec((tm, tk), lambda i, j, k: (i, k))
    b_spec = pl.BlockSpec((tn, tk), lambda i, j, k: (j, k)) if tb else pl.BlockSpec((tk, tn), lambda i, j, k: (k, j))
    o_spec = pl.BlockSpec((tm, tn), lambda i, j, k: (i, j))
    in_specs = [a_spec, b_spec] + ([o_spec] if nr else []) + [HBM_SPEC] * ng
    args = (a, b) + ((resid,) if nr else ()) + tuple(scatter)
    out = jax.ShapeDtypeStruct((M, N), out_dtype)
    sem = ("arbitrary",) * 3 if ng else ("parallel", "parallel", "arbitrary")
    return pl.pallas_call(
        body, name=name, grid=grid, in_specs=in_specs, out_specs=[o_spec] + [HBM_SPEC] * ng if ng else o_spec,
        out_shape=[out] + _scatter_shapes(scatter) if ng else out,
        scratch_shapes=[pltpu.VMEM((tm, tn), F32)] + (_scatter_sems(ng) if ng else []),
        compiler_params=_cp(*sem))(*args)


def _rms_fwd(h, w, name):
    T = h.shape[0]
    tb = _div(T, ROW_TILE, 8)

    def body(h_ref, w_ref, o_ref):
        x = h_ref[...]
        r = lax.rsqrt(jnp.mean(x * x, axis=-1, keepdims=True) + EPS)
        o_ref[...] = (x * r * w_ref[...]).astype(o_ref.dtype)

    return pl.pallas_call(
        body, name=name, grid=(T // tb,),
        in_specs=[pl.BlockSpec((tb, D), lambda i: (i, 0)), pl.BlockSpec((1, D), lambda i: (0, 0))],
        out_specs=pl.BlockSpec((tb, D), lambda i: (i, 0)), out_shape=jax.ShapeDtypeStruct((T, D), _BF),
        compiler_params=_cp("parallel"))(h, w)


def _rms_bwd(h, w, dy, dres, name):
    T = h.shape[0]
    tb = _div(T, ROW_TILE, 8)

    def body(h_ref, w_ref, dy_ref, dr_ref, dh_ref, dw_ref):
        x = h_ref[...]
        r = lax.rsqrt(jnp.mean(x * x, axis=-1, keepdims=True) + EPS)
        xh = x * r
        dy = dy_ref[...]
        g = dy * w_ref[...]
        dh_ref[...] = dr_ref[...] + r * (g - xh * jnp.mean(xh * g, axis=-1, keepdims=True))
        part = jnp.sum(dy * xh, axis=0, keepdims=True)

        @pl.when(pl.program_id(0) == 0)
        def _():
            dw_ref[...] = part

        @pl.when(pl.program_id(0) > 0)
        def _():
            dw_ref[...] += part

    row = pl.BlockSpec((tb, D), lambda i: (i, 0))
    one = pl.BlockSpec((1, D), lambda i: (0, 0))
    return pl.pallas_call(
        body, name=name, grid=(T // tb,), in_specs=[row, one, row, row], out_specs=[row, one],
        out_shape=[jax.ShapeDtypeStruct((T, D), F32), jax.ShapeDtypeStruct((1, D), F32)],
        compiler_params=_cp("arbitrary"))(h, w, dy, dres)


def _fgate_fwd(small, bias):
    T = small.shape[0]
    tb = _div(T, ROW_TILE, 8)

    def body(s_ref, b_ref, c_ref, carry):
        @pl.when(pl.program_id(0) == 0)
        def _():
            carry[...] = jnp.zeros_like(carry)

        lf = jax.nn.log_sigmoid(s_ref[...] + b_ref[...])
        r = lax.broadcasted_iota(jnp.int32, (tb, tb), 0)
        c = lax.broadcasted_iota(jnp.int32, (tb, tb), 1)
        tri = (r >= c).astype(F32)
        cs = jnp.dot(tri, lf, precision=HI, preferred_element_type=F32) + carry[...]
        c_ref[...] = cs
        carry[...] = cs[tb - 1:tb, :]

    return pl.pallas_call(
        body, name="fgate_fwd", grid=(T // tb,),
        in_specs=[pl.BlockSpec((tb, SMALL_W), lambda i: (i, 0)), pl.BlockSpec((1, SMALL_W), lambda i: (0, 0))],
        out_specs=pl.BlockSpec((tb, SMALL_W), lambda i: (i, 0)), out_shape=jax.ShapeDtypeStruct((T, SMALL_W), F32),
        scratch_shapes=[pltpu.VMEM((1, SMALL_W), F32)], compiler_params=_cp("arbitrary"))(small, bias)


def _fgate_bwd(small, bias, dc):
    T = small.shape[0]
    tb = _div(T, ROW_TILE, 8)
    nb = T // tb

    def body(s_ref, b_ref, dc_ref, df_ref, db_ref, carry):
        @pl.when(pl.program_id(0) == 0)
        def _():
            carry[...] = jnp.zeros_like(carry)

        r = lax.broadcasted_iota(jnp.int32, (tb, tb), 0)
        c = lax.broadcasted_iota(jnp.int32, (tb, tb), 1)
        tri = (r <= c).astype(F32)
        dlf = jnp.dot(tri, dc_ref[...], precision=HI, preferred_element_type=F32) + carry[...]
        carry[...] = dlf[0:1, :]
        df = dlf * _sig(-(s_ref[...] + b_ref[...]))
        df_ref[...] = df
        part = jnp.sum(df, axis=0, keepdims=True)

        @pl.when(pl.program_id(0) == 0)
        def _():
            db_ref[...] = part

        @pl.when(pl.program_id(0) > 0)
        def _():
            db_ref[...] += part

    rev = pl.BlockSpec((tb, SMALL_W), lambda i: (nb - 1 - i, 0))
    one = pl.BlockSpec((1, SMALL_W), lambda i: (0, 0))
    return pl.pallas_call(
        body, name="fgate_bwd", grid=(nb,), in_specs=[rev, one, rev], out_specs=[rev, one],
        out_shape=[jax.ShapeDtypeStruct((T, SMALL_W), F32), jax.ShapeDtypeStruct((1, SMALL_W), F32)],
        scratch_shapes=[pltpu.VMEM((1, SMALL_W), F32)], compiler_params=_cp("arbitrary"))(small, bias, dc)


def _att_rows(a8, tb):
    T = a8.shape[0]
    return a8.T.reshape(HEADS // 2, 2, T // tb, tb).transpose(0, 2, 1, 3)


def _att_cols(a8):
    return jnp.repeat(a8, HD, axis=1)


def _pair_cols(tb):
    return pl.BlockSpec((1, tb, 2), lambda p, i, *_: (p, i, 0))


def _from_pairs(a):
    return a.transpose(1, 0, 2).reshape(a.shape[1], HEADS)


EXP_ZERO = -104.0
SKIP_SLACK = 2.0
NORM_SLACK = 1.02


def _att_stats(qkv, c_cols, ones_blk):
    T = qkv.shape[0]
    tb = _div(T, ATT_TILE)

    def body(q_ref, k_ref, c_ref, e_ref, o_ref):
        q = q_ref[...].astype(F32)
        k = k_ref[...].astype(F32)
        e = e_ref[...].astype(_BF)
        qn = jnp.max(jnp.dot((q * q).astype(_BF), e, preferred_element_type=F32), axis=0, keepdims=True)
        kn = jnp.max(jnp.dot((k * k).astype(_BF), e, preferred_element_type=F32), axis=0, keepdims=True)
        c = c_ref[...]
        o_ref[0] = jnp.concatenate([jnp.sqrt(qn), jnp.sqrt(kn), jnp.max(c, axis=0, keepdims=True),
                                    jnp.min(c, axis=0, keepdims=True), jnp.zeros((4, FW), F32)], axis=0)

    blk = lambda off: pl.BlockSpec((tb, FW), lambda i: (i, off))
    return pl.pallas_call(
        body, name="att_stats", grid=(T // tb,),
        in_specs=[blk(0), blk(1), blk(0), pl.BlockSpec((FW, FW), lambda i: (0, 0))],
        out_specs=pl.BlockSpec((1, 8, FW), lambda i: (i, 0, 0)), out_shape=jax.ShapeDtypeStruct((T // tb, 8, FW), F32),
        compiler_params=_cp("parallel"))(qkv, qkv, c_cols, ones_blk)


def _att_plan(stats):
    nb = stats.shape[0]
    st = stats[:, :4, ::HD]
    qmax, kmax, cmax, cmin = (st[:, r, :].T for r in range(4))
    bound = (HD ** -0.5) * NORM_SLACK * qmax[:, :, None] * (kmax[:, None, :] + kmax[:, :, None]) \
        + cmax[:, :, None] - cmin[:, None, :] + SKIP_SLACK
    ii = lax.broadcasted_iota(jnp.int32, (nb, nb), 0)
    jj = lax.broadcasted_iota(jnp.int32, (nb, nb), 1)
    skip = (bound < EXP_ZERO) & (jj < ii)[None]
    live = (~skip & (jj <= ii)[None]).reshape(HEADS // 2, 2, nb, nb).any(axis=1)
    jfirst = jnp.argmax(live, axis=2).astype(jnp.int32)
    ilast = (nb - 1 - jnp.argmax(live[:, ::-1, :], axis=1)).astype(jnp.int32)
    return skip.astype(jnp.int32).reshape(-1), jfirst.reshape(-1), ilast.reshape(-1)


def _fox_fwd(qkv, c_cols, c_rows, plan):
    T = qkv.shape[0]
    tb = _div(T, ATT_TILE)
    nb = T // tb
    npair = HEADS // 2
    scale = HD ** -0.5
    dn = (((1,), (1,)), ((), ()))

    def body(skip_ref, jfirst_ref, ilast_ref, q_ref, k_ref, v_ref, cq_ref, ck_ref, o_ref, l_ref, l2_ref, m_scr, acc_scr):
        pr = pl.program_id(0)
        i = pl.program_id(1)
        q = q_ref[...]
        lane = lax.broadcasted_iota(jnp.int32, (1, 2 * HD), 1)
        sel0 = lane < HD
        zero = jnp.zeros_like(q)
        qh = (jnp.where(sel0, q, zero) * scale, jnp.where(sel0, zero, q) * scale)
        cq = (cq_ref[:, 0:1], cq_ref[:, HD:HD + 1])
        row = i * tb + lax.broadcasted_iota(jnp.int32, (tb, 1), 0)
        m_scr[...] = jnp.full(m_scr.shape, NEG, F32)
        acc_scr[...] = jnp.zeros(acc_scr.shape, F32)

        def tile(j, h, masked):
            start = pl.multiple_of(j * tb, tb)
            kb = k_ref[pl.ds(start, tb), :]
            vb = v_ref[pl.ds(start, tb), :]
            one = jnp.ones_like(vb)
            vh = jnp.where(sel0, vb, one) if h == 0 else jnp.where(sel0, one, vb)
            t = lax.dot_general(qh[h], kb, dn, preferred_element_type=F32) - ck_ref[0, j, h:h + 1, :]
            if masked:
                col = j * tb + lax.broadcasted_iota(jnp.int32, (1, tb), 1)
                mask = (col <= row) & (col >= FRONT)
                t = jnp.where(mask, t, NEG)
            m = m_scr[h]
            m_new = jnp.maximum(m, jnp.max(t, axis=-1, keepdims=True) + cq[h])
            p = jnp.exp(t + (cq[h] - m_new))
            if masked:
                p = jnp.where(mask, p, 0.0)
            m_scr[h] = m_new
            acc_scr[h] = acc_scr[h] * jnp.exp(m - m_new) + jnp.dot(p.astype(vb.dtype), vh, preferred_element_type=F32)

        def step(j, _):
            edge = (j == 0) | (j == i)
            for h in range(2):
                live = skip_ref[((2 * pr + h) * nb + i) * nb + j] == 0
                pl.when(live & edge)(functools.partial(tile, j, h, True))
                pl.when(live & jnp.logical_not(edge))(functools.partial(tile, j, h, False))
            return 0

        lax.fori_loop(jfirst_ref[pr * nb + i], i + 1, step, 0)
        outs, lses = [], []
        for h in range(2):
            l = acc_scr[h][:, (1 - h) * HD:(1 - h) * HD + 1]
            ok = l > 0.0
            ls = jnp.where(ok, l, 1.0)
            outs.append(jnp.where(ok, acc_scr[h] / ls, 0.0))
            lses.append(jnp.where(ok, m_scr[h] + jnp.log(ls), 0.0))
        o_ref[...] = jnp.where(sel0, outs[0], outs[1])
        l_ref[...] = jnp.where(sel0, lses[0], lses[1])
        l2_ref[0, :, 0:1] = lses[0]
        l2_ref[0, :, 1:2] = lses[1]

    blk = lambda off: pl.BlockSpec((tb, 2 * HD), lambda p, i, *_: (i, off + p))
    full = lambda off: pl.BlockSpec((T, 2 * HD), lambda p, i, *_: (0, off + p))
    rows = pl.BlockSpec((1, nb, 2, tb), lambda p, i, *_: (p, 0, 0, 0))
    return pl.pallas_call(
        body, name="fox_fwd",
        grid_spec=pltpu.PrefetchScalarGridSpec(
            num_scalar_prefetch=3, grid=(npair, nb),
            in_specs=[blk(0), full(npair), full(2 * npair), blk(0), rows], out_specs=[blk(0), blk(0), _pair_cols(tb)],
            scratch_shapes=[pltpu.VMEM((2, tb, 1), F32), pltpu.VMEM((2, tb, 2 * HD), F32)]),
        out_shape=[jax.ShapeDtypeStruct((T, FW), F32), jax.ShapeDtypeStruct((T, FW), F32),
                   jax.ShapeDtypeStruct((npair, T, 2), F32)],
        compiler_params=_cp("parallel", "arbitrary"))(*plan, qkv, qkv, qkv, c_cols, c_rows)


def _fox_bwd_dq(qkv, do, c_cols, c_rows, lse, delta, plan):
    T = qkv.shape[0]
    tb = _div(T, ATT_TILE)
    nb = T // tb
    npair = HEADS // 2
    scale = HD ** -0.5
    dn = (((1,), (1,)), ((), ()))

    def body(skip_ref, jfirst_ref, ilast_ref, q_ref, k_ref, v_ref, do_ref, cq_ref, ck_ref, l_ref, dl_ref, dq_ref, dc_ref,
             dq_scr):
        pr = pl.program_id(0)
        i = pl.program_id(1)
        q = q_ref[...]
        do = do_ref[...]
        lane = lax.broadcasted_iota(jnp.int32, (1, 2 * HD), 1)
        sel0 = lane < HD
        qh = (jnp.where(sel0, q, jnp.zeros_like(q)) * scale, jnp.where(sel0, jnp.zeros_like(q), q) * scale)
        doh = (jnp.where(sel0, do, jnp.zeros_like(do)), jnp.where(sel0, jnp.zeros_like(do), do))
        ce = (cq_ref[:, 0:1] - l_ref[:, 0:1], cq_ref[:, HD:HD + 1] - l_ref[:, HD:HD + 1])
        dl = (dl_ref[:, 0:1], dl_ref[:, HD:HD + 1])
        row = i * tb + lax.broadcasted_iota(jnp.int32, (tb, 1), 0)
        dq_scr[...] = jnp.zeros(dq_scr.shape, F32)

        def tile(j, h, masked):
            start = pl.multiple_of(j * tb, tb)
            kb = k_ref[pl.ds(start, tb), :]
            vb = v_ref[pl.ds(start, tb), :]
            one = jnp.ones_like(kb)
            kh = jnp.where(sel0, kb, one) if h == 0 else jnp.where(sel0, one, kb)
            t = lax.dot_general(qh[h], kb, dn, preferred_element_type=F32) - ck_ref[0, j, h:h + 1, :]
            if masked:
                col = j * tb + lax.broadcasted_iota(jnp.int32, (1, tb), 1)
                mask = (col <= row) & (col >= FRONT)
                p = jnp.where(mask, jnp.exp(jnp.where(mask, t, NEG) + ce[h]), 0.0)
            else:
                p = jnp.exp(t + ce[h])
            dp = lax.dot_general(doh[h], vb, dn, preferred_element_type=F32)
            ds = p * (dp - dl[h])
            dq_scr[h] += jnp.dot(ds.astype(kb.dtype), kh, preferred_element_type=F32)

        def step(j, _):
            edge = (j == 0) | (j == i)
            for h in range(2):
                live = skip_ref[((2 * pr + h) * nb + i) * nb + j] == 0
                pl.when(live & edge)(functools.partial(tile, j, h, True))
                pl.when(live & jnp.logical_not(edge))(functools.partial(tile, j, h, False))
            return 0

        lax.fori_loop(jfirst_ref[pr * nb + i], i + 1, step, 0)
        dq_ref[...] = (jnp.where(sel0, dq_scr[0], dq_scr[1]) * scale).astype(dq_ref.dtype)
        dc_ref[0, :, 0:1] = dq_scr[0][:, HD:HD + 1]
        dc_ref[0, :, 1:2] = dq_scr[1][:, 0:1]

    blk = lambda off: pl.BlockSpec((tb, 2 * HD), lambda p, i, *_: (i, off + p))
    full = lambda off: pl.BlockSpec((T, 2 * HD), lambda p, i, *_: (0, off + p))
    rows = pl.BlockSpec((1, nb, 2, tb), lambda p, i, *_: (p, 0, 0, 0))
    return pl.pallas_call(
        body, name="fox_bwd_dq",
        grid_spec=pltpu.PrefetchScalarGridSpec(
            num_scalar_prefetch=3, grid=(npair, nb),
            in_specs=[blk(0), full(npair), full(2 * npair), blk(0), blk(0), rows, blk(0), blk(0)],
            out_specs=[blk(0), _pair_cols(tb)],
            scratch_shapes=[pltpu.VMEM((2, tb, 2 * HD), F32)]),
        out_shape=[jax.ShapeDtypeStruct((T, FW), _BF), jax.ShapeDtypeStruct((npair, T, 2), F32)],
        compiler_params=_cp("parallel", "arbitrary"))(*plan, qkv, qkv, qkv, do, c_cols, c_rows, lse, delta)


def _fox_bwd_dkv(qkv, do, c_cols, ce_rows, delta_rows, plan):
    T = qkv.shape[0]
    tb = _div(T, ATT_TILE)
    nb = T // tb
    npair = HEADS // 2
    scale = HD ** -0.5
    dn = (((1,), (1,)), ((), ()))

    def body(skip_ref, jfirst_ref, ilast_ref, q_ref, k_ref, v_ref, do_ref, ck_ref, ce_ref, dl_ref,
             dk_ref, dv_ref, dc_ref, dk_scr, dv_scr):
        pr = pl.program_id(0)
        jb = pl.program_id(1)
        k = k_ref[...]
        v = v_ref[...]
        lane = lax.broadcasted_iota(jnp.int32, (1, 2 * HD), 1)
        sel0 = lane < HD
        kh = (jnp.where(sel0, k, jnp.zeros_like(k)) * scale, jnp.where(sel0, jnp.zeros_like(k), k) * scale)
        vh = (jnp.where(sel0, v, jnp.zeros_like(v)), jnp.where(sel0, jnp.zeros_like(v), v))
        ck = (ck_ref[:, 0:1], ck_ref[:, HD:HD + 1])
        kidx = jb * tb + lax.broadcasted_iota(jnp.int32, (tb, 1), 0)
        dk_scr[...] = jnp.zeros(dk_scr.shape, F32)
        dv_scr[...] = jnp.zeros(dv_scr.shape, F32)

        def tile(i, h, masked):
            start = pl.multiple_of(i * tb, tb)
            qb = q_ref[pl.ds(start, tb), :]
            dob = do_ref[pl.ds(start, tb), :]
            one = jnp.ones_like(qb)
            qh = jnp.where(sel0, qb, one) if h == 0 else jnp.where(sel0, one, qb)
            t = lax.dot_general(kh[h], qb, dn, preferred_element_type=F32) - ck[h]
            ce = ce_ref[0, i, h:h + 1, :]
            if masked:
                qidx = i * tb + lax.broadcasted_iota(jnp.int32, (1, tb), 1)
                mask = (kidx <= qidx) & (kidx >= FRONT)
                pt = jnp.where(mask, jnp.exp(jnp.where(mask, t, NEG) + ce), 0.0)
            else:
                pt = jnp.exp(t + ce)
            dv_scr[h] += jnp.dot(pt.astype(dob.dtype), dob, preferred_element_type=F32)
            dpt = lax.dot_general(vh[h], dob, dn, preferred_element_type=F32)
            dst = pt * (dpt - dl_ref[0, i, h:h + 1, :])
            dk_scr[h] += jnp.dot(dst.astype(qb.dtype), qh, preferred_element_type=F32)

        def step(i, _):
            edge = (i == jb) | (jb == 0)
            for h in range(2):
                live = skip_ref[((2 * pr + h) * nb + i) * nb + jb] == 0
                pl.when(live & edge)(functools.partial(tile, i, h, True))
                pl.when(live & jnp.logical_not(edge))(functools.partial(tile, i, h, False))
            return 0

        lax.fori_loop(jb, ilast_ref[pr * nb + jb] + 1, step, 0)
        dk_ref[...] = (jnp.where(sel0, dk_scr[0], dk_scr[1]) * scale).astype(dk_ref.dtype)
        dv_ref[...] = jnp.where(sel0, dv_scr[0], dv_scr[1]).astype(dv_ref.dtype)
        dc_ref[0, :, 0:1] = -dk_scr[0][:, HD:HD + 1]
        dc_ref[0, :, 1:2] = -dk_scr[1][:, 0:1]

    blk = lambda off: pl.BlockSpec((tb, 2 * HD), lambda p, j, *_: (j, off + p))
    full = lambda off: pl.BlockSpec((T, 2 * HD), lambda p, j, *_: (0, off + p))
    rows = pl.BlockSpec((1, nb, 2, tb), lambda p, j, *_: (p, 0, 0, 0))
    return pl.pallas_call(
        body, name="fox_bwd_dkv",
        grid_spec=pltpu.PrefetchScalarGridSpec(
            num_scalar_prefetch=3, grid=(npair, nb),
            in_specs=[full(0), blk(npair), blk(2 * npair), full(0), blk(0), rows, rows],
            out_specs=[blk(0), blk(0), _pair_cols(tb)],
            scratch_shapes=[pltpu.VMEM((2, tb, 2 * HD), F32), pltpu.VMEM((2, tb, 2 * HD), F32)]),
        out_shape=[jax.ShapeDtypeStruct((T, FW), _BF)] * 2 + [jax.ShapeDtypeStruct((npair, T, 2), F32)],
        compiler_params=_cp("parallel", "arbitrary"))(*plan, qkv, qkv, qkv, do, c_cols, ce_rows, delta_rows)


def _head_dot(a, b, ones_blk, ones_col):
    T = a.shape[0]
    tb = _div(T, ROW_TILE, 8)

    def body(a_ref, b_ref, e_ref, c_ref, o_ref, o8_ref):
        prod = a_ref[...].astype(F32) * b_ref[...].astype(F32)
        o_ref[...] = jnp.dot(prod, e_ref[...], precision=HI, preferred_element_type=F32)
        o8_ref[...] = jnp.dot(prod, c_ref[...], precision=HI, preferred_element_type=F32)

    row = pl.BlockSpec((tb, FW), lambda i: (i, 0))
    return pl.pallas_call(
        body, name="head_dot", grid=(T // tb,),
        in_specs=[row, row, pl.BlockSpec((FW, FW), lambda i: (0, 0)), pl.BlockSpec((FW, SMALL_W), lambda i: (0, 0))],
        out_specs=[row, pl.BlockSpec((tb, SMALL_W), lambda i: (i, 0))],
        out_shape=[jax.ShapeDtypeStruct((T, FW), F32), jax.ShapeDtypeStruct((T, SMALL_W), F32)],
        compiler_params=_cp("parallel"))(a, b, ones_blk, ones_col)


CONV_RC = 32
CONV_TC = 256


def _stage_prev(scr, x_ref, h_ref, first, tb):
    scr[0:8, :] = jnp.where(first, 0.0, h_ref[...])
    scr[8:8 + tb, :] = x_ref[...]


def _windows_prev(scr, r, kw):
    x = scr[pl.ds(pl.multiple_of(r * CONV_RC, CONV_RC), CONV_RC + 8), :]
    return [x[8:] if k == kw - 1 else pltpu.roll(x, kw - 1 - k, 0)[8:] for k in range(kw)]


def _fold8(a):
    return a.reshape(CONV_RC // 8, 8, a.shape[-1]).sum(axis=0)


def _halo_prev(tb, tc, off=0):
    return pl.BlockSpec((8, tc), lambda j, i: (jnp.maximum(i * (tb // 8) - 1, 0), j + off))


def _gconv_fwd(rest, w):
    T = rest.shape[0]
    C = 3 * FW
    kw = w.shape[0]
    tb, tc = _div(T, 1280, CONV_RC), CONV_TC

    def body(x_ref, h_ref, w_ref, o_ref, scr):
        _stage_prev(scr, x_ref, h_ref, pl.program_id(1) == 0, tb)
        wv = w_ref[...]

        def chunk(r, _):
            win = _windows_prev(scr, r, kw)
            u = sum(wv[k:k + 1, :] * win[k] for k in range(kw))
            o_ref[pl.ds(pl.multiple_of(r * CONV_RC, CONV_RC), CONV_RC), :] = u * _sig(u)
            return 0

        lax.fori_loop(0, tb // CONV_RC, chunk, 0)

    return pl.pallas_call(
        body, name="gconv_fwd", grid=(C // tc, T // tb),
        in_specs=[pl.BlockSpec((tb, tc), lambda j, i: (i, j)), _halo_prev(tb, tc), pl.BlockSpec((kw, tc), lambda j, i: (0, j))],
        out_specs=pl.BlockSpec((tb, tc), lambda j, i: (i, j)), out_shape=jax.ShapeDtypeStruct((T, C), F32),
        scratch_shapes=[pltpu.VMEM((tb + 8, tc), F32)], compiler_params=_cp("parallel", "arbitrary"))(rest, rest, w)


def _gconv_bwd_du(rest, w, dy):
    T = rest.shape[0]
    C = 3 * FW
    kw = w.shape[0]
    tb, tc = _div(T, 1280, CONV_RC), CONV_TC

    def body(x_ref, h_ref, w_ref, dy_ref, du_ref, dw_ref, scr):
        i = pl.program_id(1)
        _stage_prev(scr, x_ref, h_ref, i == 0, tb)
        wv = w_ref[...]

        def chunk(r, acc):
            rows = pl.ds(pl.multiple_of(r * CONV_RC, CONV_RC), CONV_RC)
            win = _windows_prev(scr, r, kw)
            u = sum(wv[k:k + 1, :] * win[k] for k in range(kw))
            sg = _sig(u)
            du = dy_ref[rows, :] * sg * (1.0 + u * (1.0 - sg))
            du_ref[rows, :] = du
            return tuple(acc[k] + _fold8(du * win[k]) for k in range(kw))

        acc = lax.fori_loop(0, tb // CONV_RC, chunk, tuple(jnp.zeros((8, tc), F32) for _ in range(kw)))
        part = jnp.concatenate([jnp.sum(a, axis=0, keepdims=True) for a in acc], axis=0)

        @pl.when(i == 0)
        def _():
            dw_ref[...] = part

        @pl.when(i > 0)
        def _():
            dw_ref[...] += part

    blk = pl.BlockSpec((tb, tc), lambda j, i: (i, j))
    wsp = pl.BlockSpec((kw, tc), lambda j, i: (0, j))
    return pl.pallas_call(
        body, name="gconv_bwd_du", grid=(C // tc, T // tb), in_specs=[blk, _halo_prev(tb, tc), wsp, blk],
        out_specs=[blk, wsp], out_shape=[jax.ShapeDtypeStruct((T, C), F32), jax.ShapeDtypeStruct((kw, C), F32)],
        scratch_shapes=[pltpu.VMEM((tb + 8, tc), F32)], compiler_params=_cp("parallel", "arbitrary"))(rest, rest, w, dy)


def _conv_bwd_dx(du, w, out_dtype, name):
    T, C = du.shape
    kw = w.shape[0]
    tb = _div(T, 1280, CONV_RC)
    tc = CONV_TC
    nb = T // tb

    def body(x_ref, h_ref, w_ref, o_ref, scr):
        scr[0:tb, :] = x_ref[...]
        scr[tb:tb + 8, :] = jnp.where(pl.program_id(1) == nb - 1, 0.0, h_ref[...])
        wv = w_ref[...]

        def chunk(r, _):
            start = pl.multiple_of(r * CONV_RC, CONV_RC)
            x = scr[pl.ds(start, CONV_RC + 8), :]
            acc = wv[kw - 1:kw, :] * x[:CONV_RC]
            for k in range(kw - 1):
                acc = acc + wv[k:k + 1, :] * pltpu.roll(x, CONV_RC + 8 - (kw - 1 - k), 0)[:CONV_RC]
            o_ref[pl.ds(start, CONV_RC), :] = acc.astype(o_ref.dtype)
            return 0

        lax.fori_loop(0, tb // CONV_RC, chunk, 0)

    halo = pl.BlockSpec((8, tc), lambda j, i: (jnp.minimum((i + 1) * (tb // 8), T // 8 - 1), j))
    return pl.pallas_call(
        body, name=name, grid=(C // tc, nb),
        in_specs=[pl.BlockSpec((tb, tc), lambda j, i: (i, j)), halo, pl.BlockSpec((kw, tc), lambda j, i: (0, j))],
        out_specs=pl.BlockSpec((tb, tc), lambda j, i: (i, j)), out_shape=jax.ShapeDtypeStruct((T, C), out_dtype),
        scratch_shapes=[pltpu.VMEM((tb + 8, tc), F32)], compiler_params=_cp("parallel", "arbitrary"))(du, du, w)


def _glu_fwd(up, w, b):
    T = up.shape[0]
    kw = w.shape[0]
    tb, tc = _div(T, 1280, CONV_RC), CONV_TC
    nc = DFF // tc

    def body(xg, hg, xu, hu, wg, wu, bg, bu, o_ref, sg, su):
        first = pl.program_id(1) == 0
        _stage_prev(sg, xg, hg, first, tb)
        _stage_prev(su, xu, hu, first, tb)
        wgv, wuv, bgv, buv = wg[...], wu[...], bg[...], bu[...]

        def chunk(r, _):
            wing, winu = _windows_prev(sg, r, kw), _windows_prev(su, r, kw)
            ug = bgv + sum(wgv[k:k + 1, :] * wing[k] for k in range(kw))
            uu = buv + sum(wuv[k:k + 1, :] * winu[k] for k in range(kw))
            o_ref[pl.ds(pl.multiple_of(r * CONV_RC, CONV_RC), CONV_RC), :] = (ug * _sig(ug) * uu).astype(o_ref.dtype)
            return 0

        lax.fori_loop(0, tb // CONV_RC, chunk, 0)

    blk = lambda off: pl.BlockSpec((tb, tc), lambda j, i: (i, j + off))
    wsp = lambda off: pl.BlockSpec((kw, tc), lambda j, i: (0, j + off))
    bsp = lambda off: pl.BlockSpec((1, tc), lambda j, i: (0, j + off))
    return pl.pallas_call(
        body, name="glu_fwd", grid=(nc, T // tb),
        in_specs=[blk(0), _halo_prev(tb, tc), blk(nc), _halo_prev(tb, tc, nc), wsp(0), wsp(nc), bsp(0), bsp(nc)],
        out_specs=blk(0), out_shape=jax.ShapeDtypeStruct((T, DFF), _BF),
        scratch_shapes=[pltpu.VMEM((tb + 8, tc), F32)] * 2, compiler_params=_cp("parallel", "arbitrary"))(
            up, up, up, up, w, w, b, b)


def _glu_bwd_du(up, w, b, df):
    T = up.shape[0]
    kw = w.shape[0]
    tb, tc = _div(T, 1280, CONV_RC), CONV_TC
    nc = DFF // tc

    def body(xg, hg, xu, hu, wg, wu, bg, bu, df_ref, dug_ref, duu_ref, dwg_ref, dwu_ref, dbg_ref, dbu_ref, sg, su):
        i = pl.program_id(1)
        _stage_prev(sg, xg, hg, i == 0, tb)
        _stage_prev(su, xu, hu, i == 0, tb)
        wgv, wuv, bgv, buv = wg[...], wu[...], bg[...], bu[...]

        def chunk(r, acc):
            rows = pl.ds(pl.multiple_of(r * CONV_RC, CONV_RC), CONV_RC)
            wing, winu = _windows_prev(sg, r, kw), _windows_prev(su, r, kw)
            ug = bgv + sum(wgv[k:k + 1, :] * wing[k] for k in range(kw))
            uu = buv + sum(wuv[k:k + 1, :] * winu[k] for k in range(kw))
            s = _sig(ug)
            df = df_ref[rows, :]
            dug = df * uu * s * (1.0 + ug * (1.0 - s))
            duu = df * ug * s
            dug_ref[rows, :] = dug
            duu_ref[rows, :] = duu
            new = [dug * wing[k] for k in range(kw)] + [duu * winu[k] for k in range(kw)] + [dug, duu]
            return tuple(a + _fold8(v) for a, v in zip(acc, new))

        acc = lax.fori_loop(0, tb // CONV_RC, chunk, tuple(jnp.zeros((8, tc), F32) for _ in range(2 * kw + 2)))
        col = [jnp.sum(a, axis=0, keepdims=True) for a in acc]
        parts = (jnp.concatenate(col[:kw], axis=0), jnp.concatenate(col[kw:2 * kw], axis=0), col[2 * kw], col[2 * kw + 1])
        accs = (dwg_ref, dwu_ref, dbg_ref, dbu_ref)

        @pl.when(i == 0)
        def _():
            for r, p in zip(accs, parts):
                r[...] = p

        @pl.when(i > 0)
        def _():
            for r, p in zip(accs, parts):
                r[...] += p

    blk = lambda off: pl.BlockSpec((tb, tc), lambda j, i: (i, j + off))
    wsp = lambda off: pl.BlockSpec((kw, tc), lambda j, i: (0, j + off))
    bsp = lambda off: pl.BlockSpec((1, tc), lambda j, i: (0, j + off))
    return pl.pallas_call(
        body, name="glu_bwd_du", grid=(nc, T // tb),
        in_specs=[blk(0), _halo_prev(tb, tc), blk(nc), _halo_prev(tb, tc, nc), wsp(0), wsp(nc), bsp(0), bsp(nc), blk(0)],
        out_specs=[blk(0), blk(0), wsp(0), wsp(0), bsp(0), bsp(0)],
        out_shape=[jax.ShapeDtypeStruct((T, DFF), F32)] * 2 + [jax.ShapeDtypeStruct((kw, DFF), F32)] * 2
        + [jax.ShapeDtypeStruct((1, DFF), F32)] * 2,
        scratch_shapes=[pltpu.VMEM((tb + 8, tc), F32)] * 2, compiler_params=_cp("parallel", "arbitrary"))(
            up, up, up, up, w, w, b, b, df)


def _mix_fwd(rest, gate_bias, y_fox, y_gdn):
    T = rest.shape[0]
    tb, tc = _div(T, ROW_TILE, 8), 512
    nc = D // tc
    og = OFF_GATES // tc

    def body(gf, gg, bf, bg, yf, yg, o_ref):
        o_ref[...] = (_sig(gf[...] + bf[...]) * yf[...] + _sig(gg[...] + bg[...]) * yg[...]).astype(o_ref.dtype)

    blk = lambda off: pl.BlockSpec((tb, tc), lambda i, j: (i, j + off))
    bsp = lambda off: pl.BlockSpec((1, tc), lambda i, j: (0, j + off))
    return pl.pallas_call(
        body, name="mix_fwd", grid=(T // tb, nc),
        in_specs=[blk(og), blk(og + nc), bsp(0), bsp(nc), blk(0), blk(0)], out_specs=blk(0),
        out_shape=jax.ShapeDtypeStruct((T, D), _BF), compiler_params=_cp("parallel", "parallel"))(
            rest, rest, gate_bias, gate_bias, y_fox, y_gdn)


def _mix_bwd(rest, gate_bias, y_fox, y_gdn, dmix):
    T = rest.shape[0]
    tb, tc = _div(T, ROW_TILE, 8), 512
    nc = D // tc
    og = OFF_GATES // tc

    def body(gf, gg, bf, bg, yf, yg, dm, dyf, dyg, dgf, dgg, dbf, dbg):
        i = pl.program_id(1)
        d = dm[...]
        sf = _sig(gf[...] + bf[...])
        sg = _sig(gg[...] + bg[...])
        dyf[...] = (d * sf).astype(dyf.dtype)
        dyg[...] = (d * sg).astype(dyg.dtype)
        a = d * yf[...] * sf * (1.0 - sf)
        b = d * yg[...] * sg * (1.0 - sg)
        dgf[...] = a.astype(dgf.dtype)
        dgg[...] = b.astype(dgg.dtype)
        pa = jnp.sum(a, axis=0, keepdims=True)
        pb = jnp.sum(b, axis=0, keepdims=True)

        @pl.when(i == 0)
        def _():
            dbf[...] = pa
            dbg[...] = pb

        @pl.when(i > 0)
        def _():
            dbf[...] += pa
            dbg[...] += pb

    blk = lambda off: pl.BlockSpec((tb, tc), lambda j, i: (i, j + off))
    bsp = lambda off: pl.BlockSpec((1, tc), lambda j, i: (0, j + off))
    return pl.pallas_call(
        body, name="mix_bwd", grid=(nc, T // tb),
        in_specs=[blk(og), blk(og + nc), bsp(0), bsp(nc), blk(0), blk(0), blk(0)],
        out_specs=[blk(0), blk(0), blk(0), blk(0), bsp(0), bsp(0)],
        out_shape=[jax.ShapeDtypeStruct((T, D), _BF)] * 4 + [jax.ShapeDtypeStruct((1, D), F32)] * 2,
        compiler_params=_cp("parallel", "arbitrary"))(rest, rest, gate_bias, gate_bias, y_fox, y_gdn, dmix)


def _loss_head(h2, w, target, n_valid):
    T = h2.shape[0]
    tb = _div(T, ROW_TILE, 8)

    def body(h_ref, w_ref, t_ref, dh_ref, loss_ref, dw_ref):
        i = pl.program_id(0)
        x = h_ref[...]
        r = lax.rsqrt(jnp.mean(x * x, axis=-1, keepdims=True) + EPS)
        xh = x * r
        row = i * tb + lax.broadcasted_iota(jnp.int32, (tb, 1), 0)
        valid = (row >= X0) & (row < X0 + n_valid)
        e = jnp.where(valid, xh * w_ref[...] - t_ref[...], 0.0)
        dy = e * (1.0 / D)
        g = dy * w_ref[...]
        dh_ref[...] = r * (g - xh * jnp.mean(xh * g, axis=-1, keepdims=True))
        lpart = 0.5 * jnp.sum(jnp.sum(e * e, axis=-1, keepdims=True) * (1.0 / D), axis=0, keepdims=True)
        wpart = jnp.sum(dy * xh, axis=0, keepdims=True)

        @pl.when(i == 0)
        def _():
            loss_ref[...] = lpart
            dw_ref[...] = wpart

        @pl.when(i > 0)
        def _():
            loss_ref[...] += lpart
            dw_ref[...] += wpart

    row = pl.BlockSpec((tb, D), lambda i: (i, 0))
    one = pl.BlockSpec((1, D), lambda i: (0, 0))
    return pl.pallas_call(
        body, name="loss_head", grid=(T // tb,), in_specs=[row, one, row],
        out_specs=[row, pl.BlockSpec((1, 1), lambda i: (0, 0)), one],
        out_shape=[jax.ShapeDtypeStruct((T, D), F32), jax.ShapeDtypeStruct((1, 1), F32), jax.ShapeDtypeStruct((1, D), F32)],
        compiler_params=_cp("arbitrary"))(h2, w, target)


def _bd_lo(a, b, ca, cb):
    return lax.dot_general(a.astype(_BF), b.astype(_BF), (((ca,), (cb,)), ((0,), (0,))), preferred_element_type=F32)


def _split2(a):
    hi = a.astype(_BF)
    return hi, (a - hi.astype(F32)).astype(_BF)


def _bd_hi(a, b, ca, cb, exact_a=False):
    dn = (((ca,), (cb,)), ((0,), (0,)))
    dot = lambda x, y: lax.dot_general(x, y, dn, preferred_element_type=F32)
    bh, bl = _split2(b)
    if exact_a:
        ah = a.astype(_BF)
        return dot(ah, bh) + dot(ah, bl)
    ah, al = _split2(a)
    return dot(ah, bh) + (dot(ah, bl) + dot(al, bh))


def _row_to_col(x):
    eye = lax.broadcasted_iota(jnp.int32, (1, CH, CH), 1) == lax.broadcasted_iota(jnp.int32, (1, CH, CH), 2)
    return jnp.sum(jnp.where(eye, jnp.broadcast_to(x, (HEADS, CH, CH)), 0.0), axis=2, keepdims=True)


def _col_to_row(x):
    eye = lax.broadcasted_iota(jnp.int32, (1, CH, CH), 1) == lax.broadcasted_iota(jnp.int32, (1, CH, CH), 2)
    return jnp.sum(jnp.where(eye, jnp.broadcast_to(x, (HEADS, CH, CH)), 0.0), axis=1, keepdims=True)


def _gdn_chunk(q, k, v, bpre, apre, alog, dtb):
    H = HEADS
    r = lax.broadcasted_iota(jnp.int32, (1, CH, CH), 1)
    c = lax.broadcasted_iota(jnp.int32, (1, CH, CH), 2)
    tril, strict = r >= c, r > c
    lb = jnp.broadcast_to(tril.astype(F32), (H, CH, CH))
    rq = lax.rsqrt(jnp.sum(q * q, axis=-1, keepdims=True) + EPS)
    rk = lax.rsqrt(jnp.sum(k * k, axis=-1, keepdims=True) + EPS)
    qh = q * rq
    qn = qh * (HD ** -0.5)
    kn = k * rk
    beta = _sig(bpre)
    x = apre + dtb
    ea = jnp.exp(alog)
    g = -ea * jax.nn.softplus(x)
    gb = jnp.broadcast_to(g, (H, CH, HD))
    gc = _bd_hi(lb, gb, 2, 1, True)
    dm = gc - _col_to_row(gc[:, :, 0:1])
    decay = jnp.where(tril, jnp.exp(jnp.where(tril, dm, 0.0)), 0.0)
    eg = jnp.exp(gc)
    gl = gc[:, CH - 1:CH, :]
    egl = jnp.exp(gl - gc)
    cd = jnp.exp(gl)
    kb = kn * beta
    vb = v * beta
    both = _bd_lo(_rows2(kb, qn), kn, 2, 2)
    kk, qk = both[:, :CH], both[:, CH:]
    pw = -jnp.where(strict, kk * decay, 0.0)
    tm = jnp.where(r == c, 1.0, 0.0) + pw
    pw = _bd_hi(pw, pw, 2, 1)
    for it in range(4):
        mul = _bd_hi if it < 2 else _bd_lo
        both = mul(_rows2(pw, tm), pw, 2, 1)
        pw, tm = both[:, :CH], tm + both[:, CH:]
    tm = tm + _bd_lo(tm, pw, 2, 1)
    kbg = kb * eg
    sol = _bd_hi(tm, _lanes2(vb, kbg), 2, 1)
    value, kcd = sol[:, :, :HD], sol[:, :, HD:]
    attn = jnp.where(tril, qk * decay, 0.0)
    return dict(tril=tril, strict=strict, lb=lb, rq=rq, rk=rk, qh=qh, qn=qn, kn=kn, beta=beta, x=x, ea=ea, g=g,
                decay=decay, eg=eg, egl=egl, cd=cd, kb=kb, vb=vb, kk=kk, tm=tm, kbg=kbg, value=value, kcd=kcd, qk=qk,
                attn=attn, qd=qn * eg, kt=kn * egl, sol=sol)


def _rows2(a, b):
    return jnp.concatenate([a, b], axis=1)


def _lanes2(a, b):
    return jnp.concatenate([a, b], axis=2)


GDN_CPS = 2
GDN_CPS_FWD = 4


def _gdn_specs(T, rev):
    G = GDN_CPS if rev else GDN_CPS_FWD
    ns = T // (G * CH)
    pos = (lambda n: ns - 1 - n) if rev else (lambda n: n)
    mat = pl.BlockSpec((HEADS, G * CH, HD), lambda n: (0, pos(n), 0))
    col = pl.BlockSpec((G, HEADS, 1, CH), lambda n: (pos(n), 0, 0, 0))
    sca = pl.BlockSpec((HEADS, 1, 1), lambda n: (0, 0, 0))
    nw = pl.BlockSpec((1, 1, HD), lambda n: (0, 0, 0))
    st = pl.BlockSpec((G, HEADS, HD, HD), lambda n: (pos(n), 0, 0, 0))
    tok = lambda width, off: pl.BlockSpec((G * CH, width), lambda n: (pos(n), off))
    return ns, mat, col, sca, nw, st, tok


def _split_heads(x):
    return [jnp.stack([x[:, (g * HEADS + h) * HD:(g * HEADS + h + 1) * HD] for h in range(HEADS)])
            for g in range(x.shape[1] // FW)]


def _store_heads(ref, rows, g, val):
    for h in range(HEADS):
        ref[rows, (g * HEADS + h) * HD:(g * HEADS + h + 1) * HD] = val[h]


def _gdn_fwd(conv, rest, bpre, apre, alog, dtb, nw, gather=()):
    T = conv.shape[0]
    ns, mat, col, sca, nws, st, tok = _gdn_specs(T, False)
    rows = [slice(g * CH, (g + 1) * CH) for g in range(GDN_CPS_FWD)]
    ng = len(gather)

    def body(*refs):
        c_ref, z_ref, b_ref, a_ref, al_ref, dt_ref, nw_ref = refs[:7]
        o_ref, og_ref, st_ref = refs[7 + ng:10 + ng]
        s_scr = refs[10 + 2 * ng]
        comm = (refs[7:7 + ng], refs[10 + ng:10 + 2 * ng], refs[11 + 2 * ng:])

        @pl.when(pl.program_id(0) == 0)
        def _():
            s_scr[...] = jnp.zeros_like(s_scr)
            if ng:
                _gather_phase(*comm, True)

        locs = [_gdn_chunk(*_split_heads(c_ref[rows[g], :]), _row_to_col(b_ref[g]), _row_to_col(a_ref[g]), al_ref[...],
                           dt_ref[...]) for g in range(GDN_CPS_FWD)]
        s = s_scr[...]
        for g, L in enumerate(locs):
            st_ref[g] = s
            both = _bd_lo(_rows2(L["kcd"], L["qd"]), s, 2, 1)
            v_new = L["value"] - both[:, :CH]
            o = both[:, CH:] + _bd_lo(L["attn"], v_new, 2, 1)
            s = s * L["cd"] + _bd_lo(L["kt"], v_new, 1, 1)
            o_ref[:, rows[g], :] = o
            zz, = _split_heads(z_ref[rows[g], :])
            rs = lax.rsqrt(jnp.mean(o * o, axis=-1, keepdims=True) + EPS)
            _store_heads(og_ref, rows[g], 0, o * rs * nw_ref[...] * zz * _sig(zz))
        s_scr[...] = s
        if ng:
            pl.when(pl.program_id(0) == ns - 1)(functools.partial(_gather_phase, *comm, False))

    return pl.pallas_call(
        body, name="gdn_fwd", grid=(ns,),
        in_specs=[tok(3 * FW, 0), tok(FW, OFF_Z // FW), col, col, sca, sca, nws] + [HBM_SPEC] * ng,
        out_specs=[mat, tok(FW, 0), st] + [HBM_SPEC] * ng,
        out_shape=[jax.ShapeDtypeStruct((HEADS, T, HD), F32), jax.ShapeDtypeStruct((T, FW), F32),
                   jax.ShapeDtypeStruct((T // CH, HEADS, HD, HD), F32)] + _gather_shapes(gather),
        scratch_shapes=[pltpu.VMEM((HEADS, HD, HD), F32)] + (_gather_sems(ng) if ng else []),
        compiler_params=_cp("arbitrary"))(conv, rest, bpre, apre, alog, dtb, nw, *gather)


def _gdn_bwd(conv, rest, bpre, apre, alog, dtb, nw, states, o, dog, scatter=()):
    T = conv.shape[0]
    ns, mat, col, sca, nws, st, tok = _gdn_specs(T, True)
    rows = [slice(g * CH, (g + 1) * CH) for g in range(GDN_CPS)]
    ng = len(scatter)

    def body(*refs):
        c_ref, z_ref, b_ref, a_ref, al_ref, dt_ref, nw_ref, st_ref, o_ref, dog_ref = refs[:10]
        dc_ref, dz_ref, db_ref, da_ref, dal_ref, ddt_ref, dnw_ref = refs[10 + ng:17 + ng]
        ds_scr = refs[17 + 2 * ng]
        comm = (refs[10:10 + ng], refs[17 + ng:17 + 2 * ng], refs[18 + 2 * ng:])

        @pl.when(pl.program_id(0) == 0)
        def _():
            ds_scr[...] = jnp.zeros_like(ds_scr)
            dal_ref[...] = jnp.zeros_like(dal_ref)
            ddt_ref[...] = jnp.zeros_like(ddt_ref)
            dnw_ref[...] = jnp.zeros_like(dnw_ref)
            if ng:
                _scatter_phase(*comm, True)

        splits = [_split_heads(c_ref[rows[g], :]) for g in range(GDN_CPS)]
        locs = [_gdn_chunk(*splits[g], _row_to_col(b_ref[g]), _row_to_col(a_ref[g]), al_ref[...], dt_ref[...])
                for g in range(GDN_CPS)]
        dsn = ds_scr[...]
        acc_al, acc_dt, acc_nw = 0.0, 0.0, 0.0
        for g in reversed(range(GDN_CPS)):
            L, vv = locs[g], splits[g][2]
            tril, strict, lb = L["tril"], L["strict"], L["lb"]
            qn, kn, kb, beta, decay, eg, egl, cd = L["qn"], L["kn"], L["kb"], L["beta"], L["decay"], L["eg"], L["egl"], L["cd"]
            value, kcd, attn, qd, kt, tm = L["value"], L["kcd"], L["attn"], L["qd"], L["kt"], L["tm"]
            s = st_ref[g]
            v_new = value - _bd_lo(kcd, s, 2, 1)
            oo = o_ref[:, rows[g], :]
            zz, = _split_heads(z_ref[rows[g], :])
            dog_, = _split_heads(dog_ref[rows[g], :])
            sz = _sig(zz)
            rs = lax.rsqrt(jnp.mean(oo * oo, axis=-1, keepdims=True) + EPS)
            oh = oo * rs
            _store_heads(dz_ref, rows[g], 0, dog_ * oh * nw_ref[...] * sz * (1.0 + zz * (1.0 - sz)))
            don = dog_ * zz * sz
            gdy = don * nw_ref[...]
            do = rs * (gdy - oh * jnp.mean(oh * gdy, axis=-1, keepdims=True))
            acc_nw = acc_nw + jnp.sum(don * oh, axis=(0, 1), keepdims=True)
            d_vnew = _bd_lo(attn, do, 1, 1) + _bd_lo(kt, dsn, 2, 1)
            both = _bd_lo(_rows2(do, d_vnew), s, 2, 2)
            d_qd, d_kcd = both[:, :CH], -both[:, CH:]
            d_attn = jnp.where(tril, _bd_lo(do, v_new, 2, 2), 0.0)
            d_kt = _bd_lo(v_new, dsn, 2, 2)
            d_cd = jnp.sum(s * dsn, axis=(1, 2), keepdims=True)
            dsn_next = cd * dsn + _bd_lo(_rows2(qd, kcd), _rows2(do, -d_vnew), 1, 1)
            dsol = _bd_hi(tm, _lanes2(d_vnew, d_kcd), 1, 1)
            d_vb, d_kbg = dsol[:, :, :HD], dsol[:, :, HD:]
            da = -jnp.where(strict, _bd_lo(dsol, L["sol"], 2, 2), 0.0)
            dkk = da * decay
            dqk = d_attn * decay
            d_decay = da * L["kk"] + d_attn * L["qk"]
            both = _bd_lo(_rows2(dkk, dqk), kn, 2, 1)
            d_kb = both[:, :CH] + d_kbg * eg
            d_qn = both[:, CH:] + d_qd * eg
            d_kn = _bd_lo(_rows2(dkk, dqk), _rows2(kb, qn), 1, 1) + d_kt * egl
            dd = d_decay * decay
            d_gc = jnp.sum(dd + d_qd * qd + d_kbg * L["kbg"] - d_kt * kt, axis=-1, keepdims=True) \
                - _row_to_col(jnp.sum(dd, axis=1, keepdims=True))
            d_gl = jnp.sum(d_kt * kt, axis=(1, 2), keepdims=True) + d_cd * cd[:, :, 0:1]
            last = lax.broadcasted_iota(jnp.int32, (1, CH, 1), 1) == CH - 1
            d_gc = d_gc + jnp.where(last, d_gl, 0.0)
            dg = _bd_hi(lb, jnp.broadcast_to(d_gc, (HEADS, CH, HD)), 1, 1, True)[:, :, 0:1]
            d_apre = -dg * L["ea"] * _sig(L["x"])
            da_ref[g] = _col_to_row(d_apre)
            acc_al = acc_al + jnp.sum(dg * L["g"], axis=1, keepdims=True)
            acc_dt = acc_dt + jnp.sum(d_apre, axis=1, keepdims=True)
            d_beta = jnp.sum(d_kb * kn + d_vb * vv, axis=-1, keepdims=True)
            db_ref[g] = _col_to_row(d_beta * beta * (1.0 - beta))
            d_kn = d_kn + d_kb * beta
            qh = L["qh"]
            _store_heads(dc_ref, rows[g], 0, (HD ** -0.5) * L["rq"] * (d_qn - qh * jnp.sum(qh * d_qn, axis=-1, keepdims=True)))
            _store_heads(dc_ref, rows[g], 1, L["rk"] * (d_kn - kn * jnp.sum(kn * d_kn, axis=-1, keepdims=True)))
            _store_heads(dc_ref, rows[g], 2, d_vb * beta)
            dsn = dsn_next
        ds_scr[...] = dsn
        dal_ref[...] += acc_al
        ddt_ref[...] += acc_dt
        dnw_ref[...] += acc_nw
        if ng:
            pl.when(pl.program_id(0) == ns - 1)(functools.partial(_scatter_phase, *comm, False))

    c3 = jax.ShapeDtypeStruct((T // CH, HEADS, 1, CH), F32)
    s3 = jax.ShapeDtypeStruct((HEADS, 1, 1), F32)
    return pl.pallas_call(
        body, name="gdn_bwd", grid=(ns,),
        in_specs=[tok(3 * FW, 0), tok(FW, OFF_Z // FW), col, col, sca, sca, nws, st, mat, tok(FW, 0)] + [HBM_SPEC] * ng,
        out_specs=[tok(3 * FW, 0), tok(FW, 0), col, col, sca, sca, nws] + [HBM_SPEC] * ng,
        out_shape=[jax.ShapeDtypeStruct((T, 3 * FW), F32), jax.ShapeDtypeStruct((T, FW), F32), c3, c3, s3, s3,
                   jax.ShapeDtypeStruct((1, 1, HD), F32)] + _scatter_shapes(scatter),
        scratch_shapes=[pltpu.VMEM((HEADS, HD, HD), F32)] + (_scatter_sems(ng) if ng else []),
        compiler_params=_cp("arbitrary"))(conv, rest, bpre, apre, alog, dtb, nw, states, o, dog, *scatter)


LATE_KEYS = {"w_branch_fox": "w_bfox", "w_branch_gdn": "w_bgdn", "w_out": "w_out", "ffn_w_up": "w_up", "ffn_w_down": "w_down"}


def _local_step(x, target, w, late=None):
    seq = x.shape[0]
    T = -(-(X0 + seq) // T_ALIGN) * T_ALIGN
    back = T - X0 - seq
    tb_att = _div(T, ATT_TILE)
    h0 = jnp.concatenate([jnp.zeros((FRONT, D), F32), w["meta"], x, jnp.zeros((back, D), F32)], axis=0)
    tgt = jnp.concatenate([jnp.zeros((X0, D), F32), target, jnp.zeros((back, D), F32)], axis=0)
    row = lambda v: v.reshape(1, -1)
    nmix, nffn, nfin = row(w["norm_mix"]), row(w["norm_ffn"]), row(w["norm_final"])
    gate_b = row(w["gate_bias"])
    fconv_b = row(w["ffn_conv_b"])
    bias128 = jnp.zeros((1, SMALL_W), F32).at[0, :HEADS].set(w["fgt_bias"])
    alog = w["a_log"].reshape(HEADS, 1, 1)
    dtb = w["dt_bias"].reshape(HEADS, 1, 1)
    gnw = w["gdn_norm"].reshape(1, 1, HD)

    a = _rms_fwd(h0, nmix, "rms_mix")
    pf = _mm(a, w["w_fox"], out_dtype=_BF, name="proj_fox")
    rest = _mm(a, w["w_rest"], name="proj_rest")
    small = rest[:, OFF_SMALL:]
    c8 = _fgate_fwd(small, bias128)[:, :HEADS]
    c_cols, c_rows = _att_cols(c8), _att_rows(c8, tb_att)
    ones_blk = jnp.asarray(np.kron(np.eye(HEADS, dtype=np.float32), np.ones((HD, HD), np.float32)))
    plan = _att_plan(_att_stats(pf, c_cols, ones_blk))
    o_fox, lse, lse2 = _fox_fwd(pf, c_cols, c_rows, plan)
    conv = _gconv_fwd(rest, w["gdn_conv"])
    chunk_rows = lambda a8: a8.reshape(T // CH, CH, HEADS).transpose(0, 2, 1)[:, :, None, :]
    bpre = chunk_rows(small[:, HEADS:2 * HEADS])
    apre = chunk_rows(small[:, 2 * HEADS:3 * HEADS])
    if late is None:
        o_raw, og, states = _gdn_fwd(conv, rest, bpre, apre, alog, dtb, gnw)
    else:
        o_raw, og, states, *got = _gdn_fwd(conv, rest, bpre, apre, alog, dtb, gnw, gather=list(late.values()))
        w = dict(w, **{LATE_KEYS[n]: _from_chips(g, AXIS[n]) for n, g in zip(late, got)})
    w_up, w_down = w["w_up"], w["w_down"]
    y_fox = _mm(o_fox, w["w_bfox"], name="y_fox")
    y_gdn = _mm(og, w["w_bgdn"], name="y_gdn")
    mix = _mix_fwd(rest, gate_b, y_fox, y_gdn)
    h1 = _mm(mix, w["w_out"], resid=h0, name="out_proj")
    b = _rms_fwd(h1, nffn, "rms_ffn")
    up = _mm(b, w_up, name="ffn_up")
    f = _glu_fwd(up, w["ffn_conv"], fconv_b)
    h2 = _mm(f, w_down, resid=h1, name="ffn_down")
    dh2, loss, d_nfin = _loss_head(h2, nfin, tgt, seq)

    d_f = _mm(dh2, w_down, tb=True, name="d_f")
    g_down = _mm_t(f, dh2, "g_down")
    dug, duu, dwg, dwu, dbg, dbu = _glu_bwd_du(up, w["ffn_conv"], fconv_b, d_f)
    dxg = _conv_bwd_dx(dug, w["ffn_conv"][:, :DFF], _BF, "fconv_dx_gate")
    dxu = _conv_bwd_dx(duu, w["ffn_conv"][:, DFF:], _BF, "fconv_dx_up")
    d_b = _mm(dxg, w_up[:, :DFF], tb=True, name="d_b_gate")
    d_b = _mm(dxu, w_up[:, DFF:], tb=True, resid=d_b, name="d_b_up")
    g_up = jnp.concatenate([_mm_t(b, dxg, "g_up_gate"), _mm_t(b, dxu, "g_up_up")], axis=1)
    dh1, d_nffn = _rms_bwd(h1, nffn, d_b, dh2, "rms_ffn_bwd")

    dmix = _mm(dh1, w["w_out"], tb=True, name="d_mix")
    g_out = _mm_t(mix, dh1, "g_out")
    dyf, dyg, dgf, dgg, dgbf, dgbg = _mix_bwd(rest, gate_b, y_fox, y_gdn, dmix)
    do_fox = _mm(dyf, w["w_bfox"], tb=True, out_dtype=_BF, name="d_o_fox")
    g_bfox = _mm_t(o_fox, dyf, "g_bfox")
    d_og = _mm(dyg, w["w_bgdn"], tb=True, name="d_o_gdn")
    g_bgdn = _mm_t(og, dyg, "g_bgdn")

    ones_col = jnp.asarray(np.kron(np.eye(HEADS, SMALL_W, dtype=np.float32), np.ones((HD, 1), np.float32)))
    delta, delta8 = _head_dot(do_fox, o_fox, ones_blk, ones_col)
    dq, dcq = _fox_bwd_dq(pf, do_fox, c_cols, c_rows, lse, delta, plan)
    dk, dv, dck = _fox_bwd_dkv(pf, do_fox, c_cols, _att_rows(c8 - _from_pairs(lse2), tb_att),
                               _att_rows(delta8[:, :HEADS], tb_att), plan)
    dc = jnp.pad(_from_pairs(dcq + dck), ((0, 0), (0, SMALL_W - HEADS)))
    dfp, d_fb = _fgate_bwd(small, bias128, dc)

    if late is None:
        late_recv = None
        dconv, dz, dbp, dap, d_alog, d_dtb, d_gnw = _gdn_bwd(conv, rest, bpre, apre, alog, dtb, gnw, states, o_raw, d_og)
    else:
        ready = dict(w_branch_fox=g_bfox, w_branch_gdn=g_bgdn, w_out=g_out, ffn_w_up=g_up, ffn_w_down=g_down)
        send = [_by_chip(ready[n].astype(jnp.bfloat16), AXIS[n]) for n in late]
        dconv, dz, dbp, dap, d_alog, d_dtb, d_gnw, *late_recv = _gdn_bwd(
            conv, rest, bpre, apre, alog, dtb, gnw, states, o_raw, d_og, scatter=send)
    du_g, g_gconv = _gconv_bwd_du(rest, w["gdn_conv"], dconv)
    dgx = _conv_bwd_dx(du_g, w["gdn_conv"], _BF, "gconv_dx")
    token_rows = lambda a: a[:, :, 0, :].transpose(0, 2, 1).reshape(T, HEADS)
    dsmall = jnp.concatenate([dfp[:, :HEADS], token_rows(dbp), token_rows(dap),
                              jnp.zeros((T, SMALL_W - 3 * HEADS), F32)], axis=1)
    drest = jnp.concatenate([dgx, dz.astype(_BF), dgf, dgg, dsmall.astype(_BF)], axis=1)
    dfox = jnp.concatenate([dq, dk, dv], axis=1)
    g_fox = _mm_t(a, dfox, "g_w_fox")
    g_rest = _mm_t(a, drest, "g_w_rest")
    sm = lambda lo: g_rest[:, OFF_SMALL + lo:OFF_SMALL + lo + HEADS]
    g_w_in = jnp.concatenate([g_fox, sm(0), g_rest[:, :3 * FW], g_rest[:, OFF_Z:OFF_Z + FW], sm(HEADS), sm(2 * HEADS),
                              g_rest[:, OFF_GATES:OFF_GATES + 2 * D]], axis=1)
    d_a = _mm(dfox, w["w_fox"], tb=True, name="d_a_fox")
    if late is None:
        d_a = _mm(drest, w["w_rest"], tb=True, resid=d_a, name="d_a_rest")
    else:
        d_a, w_in_recv = _mm(drest, w["w_rest"], tb=True, resid=d_a, name="d_a_rest",
                             scatter=[_by_chip(g_w_in.astype(jnp.bfloat16), AXIS["w_in"])])
        late_recv = [w_in_recv] + late_recv
    dh0, d_nmix = _rms_bwd(h0, nmix, d_a, dh1, "rms_mix_bwd")

    grads = dict(
        meta_tokens=dh0[FRONT:X0], w_in=g_w_in, fgt_bias=d_fb[0, :HEADS], gdn_conv_w=g_gconv,
        gdn_a_log=d_alog.reshape(HEADS), gdn_dt_bias=d_dtb.reshape(HEADS), gdn_norm_w=d_gnw.reshape(HD),
        gate_bias=jnp.concatenate([dgbf, dgbg], axis=1).reshape(2 * D), w_branch_fox=g_bfox, w_branch_gdn=g_bgdn,
        w_out=g_out, norm_mix_w=d_nmix.reshape(D), norm_ffn_w=d_nffn.reshape(D), ffn_w_up=g_up,
        ffn_conv_w=jnp.concatenate([dwg, dwu], axis=1), ffn_conv_b=jnp.concatenate([dbg, dbu], axis=1).reshape(2 * DFF),
        ffn_w_down=g_down, norm_final_w=d_nfin.reshape(D), late_recv=late_recv)
    return loss, dh0[X0:X0 + seq], grads


N_CHIPS = 4
PACK_W = 1024
PACK_ROW_ALIGN = 32
BIG_EARLY = ("w_in",)
BIG_LATE = ("w_branch_fox", "w_branch_gdn", "w_out", "ffn_w_up", "ffn_w_down")
BIG = BIG_EARLY + BIG_LATE
WEIGHTS = (
    ("meta_tokens", (N_META, D), 1), ("w_in", (D, 3 * FW + HEADS + 4 * FW + 2 * HEADS + 2 * D), 1), ("fgt_bias", (1, HEADS), None),
    ("gdn_conv_w", (4, 3 * FW), 1), ("gdn_a_log", (1, HEADS), None), ("gdn_dt_bias", (1, HEADS), None),
    ("gdn_norm_w", (1, HD), None), ("gate_bias", (1, 2 * D), None), ("w_branch_fox", (FW, D), 1),
    ("w_branch_gdn", (FW, D), 1), ("w_out", (D, D), 0), ("norm_mix_w", (1, D), None), ("norm_ffn_w", (1, D), None),
    ("ffn_w_up", (D, 2 * DFF), 1), ("ffn_conv_w", (3, 2 * DFF), 1), ("ffn_conv_b", (1, 2 * DFF), None),
    ("ffn_w_down", (DFF, D), 0), ("norm_final_w", (1, D), None))
SPLIT_F32 = ("meta_tokens", "gdn_conv_w", "ffn_conv_w")


def _shard_shape(shape, axis):
    if axis is None:
        return shape
    return tuple(s // N_CHIPS if a == axis else s for a, s in enumerate(shape))


def _shard_of(full, axis, q):
    if axis is None:
        return full
    n = full.shape[axis] // N_CHIPS
    return lax.slice_in_dim(full, q * n, (q + 1) * n, axis=axis)


def _pack_rows(n_elems):
    rows = -(-n_elems // PACK_W)
    return -(-rows // PACK_ROW_ALIGN) * PACK_ROW_ALIGN


def _pack(pieces, dtype):
    flat = jnp.concatenate([p.reshape(-1).astype(dtype) for p in pieces])
    rows = _pack_rows(flat.shape[0])
    return jnp.pad(flat, (0, rows * PACK_W - flat.shape[0])).reshape(rows, PACK_W)


def _unpack(slab, shapes):
    flat = slab.reshape(-1)
    out, off = [], 0
    for s in shapes:
        n = int(np.prod(s))
        out.append(flat[off:off + n].reshape(s))
        off += n
    return out


HBM_SPEC = pl.BlockSpec(memory_space=pltpu.HBM)
MESH_ID = pl.DeviceIdType.MESH


def _scatter_chips(srcs, name):
    n = len(srcs)

    def body(*refs):
        _scatter_phase(refs[:n], refs[n:2 * n], refs[2 * n:], True)
        _scatter_phase(refs[:n], refs[n:2 * n], refs[2 * n:], False)

    return pl.pallas_call(body, name=name, in_specs=[HBM_SPEC] * n, out_specs=[HBM_SPEC] * n,
                          out_shape=_scatter_shapes(srcs), scratch_shapes=_scatter_sems(n))(*srcs)


def _scatter_shapes(srcs):
    return [jax.ShapeDtypeStruct(s.shape, s.dtype) for s in srcs]


def _scatter_sems(n):
    return [pltpu.SemaphoreType.DMA((3 * n,)), pltpu.SemaphoreType.DMA((3 * n,)), pltpu.SemaphoreType.DMA((n,))]


def _scatter_phase(src_refs, out_refs, sems, issue):
    n = len(src_refs)
    send_sems, recv_sems, local_sems = sems
    x, y, c = lax.axis_index("x"), lax.axis_index("y"), lax.axis_index("c")
    q = 2 * x + y
    peers = [(1 - x, y), (x, 1 - y), (1 - x, 1 - y)]

    def remote(a, k, src_slot, dst_slot):
        px, py = peers[k]
        return pltpu.make_async_remote_copy(
            src_ref=src_refs[a].at[src_slot], dst_ref=out_refs[a].at[dst_slot], send_sem=send_sems.at[3 * a + k],
            recv_sem=recv_sems.at[3 * a + k], device_id=(px, py, c), device_id_type=MESH_ID)

    mine = [pltpu.make_async_copy(src_refs[a].at[q], out_refs[a].at[q], local_sems.at[a]) for a in range(n)]
    sends = [remote(a, k, 2 * px + py, q) for a in range(n) for k, (px, py) in enumerate(peers)]
    if issue:
        for cp in mine + sends:
            cp.start()
        return
    for a in range(n):
        for k, (px, py) in enumerate(peers):
            remote(a, k, 0, 2 * px + py).wait_recv()
    for cp in sends:
        cp.wait_send()
    for cp in mine:
        cp.wait()


def _gather_chips(srcs, name):
    n = len(srcs)

    def body(*refs):
        _gather_phase(refs[:n], refs[n:2 * n], refs[2 * n:], True)
        _gather_phase(refs[:n], refs[n:2 * n], refs[2 * n:], False)

    return pl.pallas_call(body, name=name, in_specs=[HBM_SPEC] * n, out_specs=[HBM_SPEC] * n,
                          out_shape=_gather_shapes(srcs), scratch_shapes=_gather_sems(n))(*srcs)


def _gather_shapes(srcs):
    return [jax.ShapeDtypeStruct((N_CHIPS,) + s.shape, s.dtype) for s in srcs]


def _gather_sems(n):
    return [pltpu.SemaphoreType.DMA((3 * n,))] * 4 + [pltpu.SemaphoreType.DMA((n,))]


def _gather_phase(src_refs, out_refs, sems, issue):
    n = len(src_refs)
    ici_send, ici_recv, d2d_send, d2d_recv, local_sems = sems
    x, y, c = lax.axis_index("x"), lax.axis_index("y"), lax.axis_index("c")
    q = 2 * x + y
    peers = [(1 - x, y), (x, 1 - y), (1 - x, 1 - y)]

    def half(a, which):
        r = src_refs[a].shape[0] // 2
        return pl.ds(which * r, r)

    def ici(a, k, slot):
        px, py = peers[k]
        return pltpu.make_async_remote_copy(
            src_ref=src_refs[a].at[half(a, c)], dst_ref=out_refs[a].at[slot, half(a, c)], send_sem=ici_send.at[3 * a + k],
            recv_sem=ici_recv.at[3 * a + k], device_id=(px, py, c), device_id_type=MESH_ID)

    def d2d(a, k, which):
        px, py = peers[k]
        rows = out_refs[a].at[2 * px + py, half(a, which)]
        return pltpu.make_async_remote_copy(
            src_ref=rows, dst_ref=rows, send_sem=d2d_send.at[3 * a + k], recv_sem=d2d_recv.at[3 * a + k],
            device_id=(x, y, 1 - c), device_id_type=MESH_ID)

    mine = [pltpu.make_async_copy(src_refs[a], out_refs[a].at[q], local_sems.at[a]) for a in range(n)]
    sends = [ici(a, k, q) for a in range(n) for k in range(3)]
    if issue:
        for cp in mine + sends:
            cp.start()
        return
    passed = []
    for a in range(n):
        for k, (px, py) in enumerate(peers):
            ici(a, k, 2 * px + py).wait_recv()
            passed.append(d2d(a, k, c))
            passed[-1].start()
    for a in range(n):
        for k in range(3):
            d2d(a, k, 1 - c).wait_recv()
    for cp in sends + passed:
        cp.wait_send()
    for cp in mine:
        cp.wait()


def _sibling_swap(slabs, name):
    n = len(slabs)

    def body(*refs):
        src_refs, out_refs, send_sems, recv_sems = refs[:n], refs[n:2 * n], refs[2 * n], refs[2 * n + 1]
        x, y, c = lax.axis_index("x"), lax.axis_index("y"), lax.axis_index("c")
        cps = [pltpu.make_async_remote_copy(src_ref=src_refs[a], dst_ref=out_refs[a], send_sem=send_sems.at[a],
                                            recv_sem=recv_sems.at[a], device_id=(x, y, 1 - c), device_id_type=MESH_ID)
               for a in range(n)]
        for cp in cps:
            cp.start()
        for cp in cps:
            cp.wait_recv()
        for cp in cps:
            cp.wait_send()

    return pl.pallas_call(
        body, name=name, in_specs=[HBM_SPEC] * n, out_specs=[HBM_SPEC] * n,
        out_shape=[jax.ShapeDtypeStruct(s.shape, s.dtype) for s in slabs],
        scratch_shapes=[pltpu.SemaphoreType.DMA((n,)), pltpu.SemaphoreType.DMA((n,))])(*slabs)


def _sum_chips(r, name):
    rows, cols = r.shape[1:]
    tb = _div(rows, 256, 16)

    def body(r0, r1, r2, r3, o_ref):
        o_ref[...] = ((r0[0].astype(F32) + r1[0].astype(F32)) + r2[0].astype(F32)) + r3[0].astype(F32)

    spec = lambda j: pl.BlockSpec((1, tb, cols), lambda i: (j, i, 0))
    return pl.pallas_call(
        body, name=name, grid=(rows // tb,), in_specs=[spec(0), spec(1), spec(2), spec(3)],
        out_specs=pl.BlockSpec((tb, cols), lambda i: (i, 0)), out_shape=jax.ShapeDtypeStruct((rows, cols), F32),
        compiler_params=_cp("parallel"))(r, r, r, r)


def _adamw(w, m, v, p, q, name):
    rows, cols = w.shape
    tb = _div(rows, 256, 8)

    def body(w_ref, m_ref, v_ref, p_ref, q_ref, g_ref, d_ref, nm_ref, nv_ref):
        g = p_ref[...] + q_ref[...]
        m_new = B1 * m_ref[...] + (1.0 - B1) * g
        v_new = B2 * v_ref[...] + (1.0 - B2) * (g * g)
        g_ref[...] = g
        nm_ref[...] = m_new
        nv_ref[...] = v_new
        m_hat = m_new / (1.0 - B1 ** STEP)
        v_hat = v_new / (1.0 - B2 ** STEP)
        d_ref[...] = -LR * (m_hat / (jnp.sqrt(v_hat) + AEPS) + WD * w_ref[...])

    spec = pl.BlockSpec((tb, cols), lambda i: (i, 0))
    return pl.pallas_call(
        body, name=name, grid=(rows // tb,), in_specs=[spec] * 5, out_specs=[spec] * 4,
        out_shape=[jax.ShapeDtypeStruct((rows, cols), F32)] * 4, compiler_params=_cp("parallel"))(w, m, v, p, q)


def _split_w_in(w_in):
    o1 = 3 * FW
    o2 = o1 + HEADS
    o3 = o2 + 3 * FW
    o4 = o3 + FW
    o5 = o4 + HEADS
    o6 = o5 + HEADS
    pad = jnp.zeros((w_in.shape[0], SMALL_W - 3 * HEADS), w_in.dtype)
    rest = jnp.concatenate([w_in[:, o2:o3], w_in[:, o3:o4], w_in[:, o6:], w_in[:, o1:o2], w_in[:, o4:o5], w_in[:, o5:o6], pad],
                           axis=1)
    return w_in[:, :o1], rest


AXIS = {n: a for n, _, a in WEIGHTS}
SMALL = tuple(n for n, _, _ in WEIGHTS if n not in BIG)


def _by_chip(full, axis):
    rows, cols = full.shape
    if axis == 0:
        return full.reshape(N_CHIPS, rows // N_CHIPS, cols)
    return full.reshape(rows, N_CHIPS, cols // N_CHIPS).transpose(1, 0, 2)


def _from_chips(parts, axis):
    _, r, c = parts.shape
    if axis == 0:
        return parts.reshape(N_CHIPS * r, c)
    return parts.transpose(1, 0, 2).reshape(r, N_CHIPS * c)


def _gather_weights(shards):
    hi = {n: shards[n].astype(jnp.bfloat16) for n in SPLIT_F32}
    lo = [(shards[n] - hi[n].astype(F32)).astype(jnp.bfloat16) for n in SPLIT_F32]
    slab = _pack([hi[n] for n in SPLIT_F32] + lo, jnp.bfloat16)
    got = _gather_chips([shards[n].astype(jnp.bfloat16) for n in BIG_EARLY] + [slab], "gather_weights")
    full = {n: _from_chips(g, AXIS[n]) for n, g in zip(BIG_EARLY, got)}
    shapes = [shards[n].shape for n in SPLIT_F32] * 2
    per_chip = [_unpack(got[-1][j], shapes) for j in range(N_CHIPS)]
    for i, n in enumerate(SPLIT_F32):
        join = lambda off: jnp.concatenate([per_chip[j][off + i] for j in range(N_CHIPS)], axis=1).astype(F32)
        full[n] = join(0) + join(len(SPLIT_F32))
    return full


def kernel(x, meta_tokens, w_in, fgt_bias, gdn_conv_w, gdn_a_log, gdn_dt_bias, gdn_norm_w, gate_bias, w_branch_fox, w_branch_gdn, w_out, norm_mix_w, norm_ffn_w, ffn_w_up, ffn_conv_w, ffn_conv_b, ffn_w_down, norm_final_w, loss_target, m_meta_tokens, m_w_in, m_fgt_bias, m_gdn_conv_w, m_gdn_a_log, m_gdn_dt_bias, m_gdn_norm_w, m_gate_bias, m_w_branch_fox, m_w_branch_gdn, m_w_out, m_norm_mix_w, m_norm_ffn_w, m_ffn_w_up, m_ffn_conv_w, m_ffn_conv_b, m_ffn_w_down, m_norm_final_w, v_meta_tokens, v_w_in, v_fgt_bias, v_gdn_conv_w, v_gdn_a_log, v_gdn_dt_bias, v_gdn_norm_w, v_gate_bias, v_w_branch_fox, v_w_branch_gdn, v_w_out, v_norm_mix_w, v_norm_ffn_w, v_ffn_w_up, v_ffn_conv_w, v_ffn_conv_b, v_ffn_w_down, v_norm_final_w):
    weights = dict(meta_tokens=meta_tokens, w_in=w_in, fgt_bias=fgt_bias, gdn_conv_w=gdn_conv_w, gdn_a_log=gdn_a_log, gdn_dt_bias=gdn_dt_bias, gdn_norm_w=gdn_norm_w, gate_bias=gate_bias, w_branch_fox=w_branch_fox, w_branch_gdn=w_branch_gdn, w_out=w_out, norm_mix_w=norm_mix_w, norm_ffn_w=norm_ffn_w, ffn_w_up=ffn_w_up, ffn_conv_w=ffn_conv_w, ffn_conv_b=ffn_conv_b, ffn_w_down=ffn_w_down, norm_final_w=norm_final_w)
    m_in = dict(meta_tokens=m_meta_tokens, w_in=m_w_in, fgt_bias=m_fgt_bias, gdn_conv_w=m_gdn_conv_w, gdn_a_log=m_gdn_a_log, gdn_dt_bias=m_gdn_dt_bias, gdn_norm_w=m_gdn_norm_w, gate_bias=m_gate_bias, w_branch_fox=m_w_branch_fox, w_branch_gdn=m_w_branch_gdn, w_out=m_w_out, norm_mix_w=m_norm_mix_w, norm_ffn_w=m_norm_ffn_w, ffn_w_up=m_ffn_w_up, ffn_conv_w=m_ffn_conv_w, ffn_conv_b=m_ffn_conv_b, ffn_w_down=m_ffn_w_down, norm_final_w=m_norm_final_w)
    v_in = dict(meta_tokens=v_meta_tokens, w_in=v_w_in, fgt_bias=v_fgt_bias, gdn_conv_w=v_gdn_conv_w, gdn_a_log=v_gdn_a_log, gdn_dt_bias=v_gdn_dt_bias, gdn_norm_w=v_gdn_norm_w, gate_bias=v_gate_bias, w_branch_fox=v_w_branch_fox, w_branch_gdn=v_w_branch_gdn, w_out=v_w_out, norm_mix_w=v_norm_mix_w, norm_ffn_w=v_norm_ffn_w, ffn_w_up=v_ffn_w_up, ffn_conv_w=v_ffn_conv_w, ffn_conv_b=v_ffn_conv_b, ffn_w_down=v_ffn_w_down, norm_final_w=v_norm_final_w)
    shard2d = {n: _shard_shape(s, a) for n, s, a in WEIGHTS}
    as2d = lambda d: {n: d[n].reshape(shard2d[n]) for n, _, _ in WEIGHTS}
    w2, m2, v2 = as2d(weights), as2d(m_in), as2d(v_in)

    full = _gather_weights(w2)
    w_fox, w_rest = _split_w_in(full["w_in"])
    flat = lambda n: w2[n].reshape(-1)
    local_w = dict(
        meta=full["meta_tokens"], w_fox=w_fox, w_rest=w_rest, fgt_bias=flat("fgt_bias"), gdn_conv=full["gdn_conv_w"],
        a_log=flat("gdn_a_log"), dt_bias=flat("gdn_dt_bias"), gdn_norm=flat("gdn_norm_w"), gate_bias=flat("gate_bias"),
        norm_mix=flat("norm_mix_w"), norm_ffn=flat("norm_ffn_w"), ffn_conv=full["ffn_conv_w"], ffn_conv_b=flat("ffn_conv_b"),
        norm_final=flat("norm_final_w"))

    loss, grad_x, grads = _local_step(x[0], loss_target[0], local_w, {n: w2[n].astype(jnp.bfloat16) for n in BIG_LATE})

    g2 = {n: grads[n].reshape(s) for n, s, _ in WEIGHTS}
    small_send = jnp.stack([_pack([_shard_of(g2[n], AXIS[n], j) for n in SMALL], F32) for j in range(N_CHIPS)])
    recv = list(grads["late_recv"]) + list(_scatter_chips([small_send], "scatter_grads"))
    parts = [_sum_chips(r, "sum_" + n) for r, n in zip(recv, BIG + ("small",))]
    others = _sibling_swap(parts, "swap_grads")
    slab = lambda d: _pack([d[n] for n in SMALL], F32)
    state = [(w2[n], m2[n], v2[n]) for n in BIG] + [(slab(w2), slab(m2), slab(v2))]
    outs = [_adamw(w, m, v, p, q, "adamw_" + n) for (w, m, v), p, q, n in zip(state, parts, others, BIG + ("small",))]
    small = [_unpack(o, [weights[n].shape for n in SMALL]) for o in outs[-1]]
    result = []
    for kind in range(4):
        by_name = {n: outs[i][kind].reshape(weights[n].shape) for i, n in enumerate(BIG)}
        by_name.update(zip(SMALL, small[kind]))
        result += [by_name[n] for n, _, _ in WEIGHTS]
    total = lax.psum(loss[0, 0], ("x", "y", "c"))
    return (total, grad_x[None], *result)
```

```python
import functools

import numpy as np
import jax
import jax.numpy as jnp
from jax import lax
from jax.experimental import pallas as pl
from jax.experimental.pallas import tpu as pltpu

F32 = jnp.float32
_BF = jnp.bfloat16
HI = lax.Precision.HIGHEST

D = 1024
N_META = 16
CH = 64
FRONT = CH - N_META
X0 = CH
HEADS = 8
HD = 64
FW = HEADS * HD
DFF = 2816
EPS = 1e-6
NEG = -1e30
T_ALIGN = 256
SMALL_W = 128
REST_W = 3 * FW + FW + 2 * D + SMALL_W
OFF_Z = 3 * FW
OFF_GATES = 4 * FW
OFF_SMALL = 4 * FW + 2 * D

LR, B1, B2, AEPS, WD, STEP = 0.001, 0.9, 0.999, 1e-08, 0.01, 10

VMEM_LIMIT = 56 * 1024 * 1024
ROW_TILE = 640
MM_TM, MM_TN, MM_TK = 1280, 512, 2816
ATT_TILE = 640


def _div(n, target, mult=128):
    if n <= target:
        return n
    best = None
    for d in range(mult, target + 1, mult):
        if n % d == 0:
            best = d
    assert best is not None, (n, target, mult)
    return best


def _cp(*sem):
    return pltpu.CompilerParams(dimension_semantics=sem, vmem_limit_bytes=VMEM_LIMIT)


def _sig(x):
    return 0.5 * jnp.tanh(0.5 * x) + 0.5


MM_VMEM_BUDGET = 40 * 1024 * 1024
MM_STEP_BYTES = 1 << 20


def _mm_tiles(m, n, k, sa, sb, so, has_resid):
    divs = lambda d, cap: [t for t in range(128, min(d, cap) + 1, 128) if d % t == 0] or [d]
    best = None
    for tm in divs(m, MM_TM * 2):
        for tn in divs(n, 4608):
            fixed = tm * tn * (4 + 2 * so + (8 if has_resid else 0))
            tks = [t for t in divs(k, MM_TK) if 2 * t * (tm * sa + tn * sb) + fixed <= MM_VMEM_BUDGET]
            if not tks:
                continue
            tk = tks[-1]
            steps = (m // tm) * (n // tn) * (k // tk)
            rmw = (k // tk - 1) * m * n * 4
            cost = (n // tn) * m * k * sa + (m // tm) * k * n * sb + steps * MM_STEP_BYTES + rmw
            if best is None or cost < best[0]:
                best = (cost, tm, tn, tk)
    assert best is not None, (m, n, k)
    return best[1:]


def _mm_t(a, b, name):
    return _mm(a, b, ta=True, name=name)


def _mm(a, b, *, ta=False, tb=False, out_dtype=F32, resid=None, name, scatter=()):
    K, M = a.shape if ta else a.shape[::-1]
    N = b.shape[0] if tb else b.shape[1]
    assert (b.shape[1] if tb else b.shape[0]) == K
    tm, tn, tk = _mm_tiles(M, N, K, a.dtype.itemsize, b.dtype.itemsize, jnp.dtype(out_dtype).itemsize, resid is not None)
    nk = K // tk
    grid = (M // tm, N // tn, nk)
    dims = (((0 if ta else 1,), (1 if tb else 0,)), ((), ()))
    mxu = _BF
    nr = 1 if resid is not None else 0
    ng = len(scatter)

    def body(*refs):
        a_ref, b_ref = refs[:2]
        r_ref = refs[2] if nr else None
        o_ref = refs[2 + nr + ng]
        acc = refs[3 + nr + 2 * ng]
        comm = (refs[2 + nr:2 + nr + ng], refs[3 + nr + ng:3 + nr + 2 * ng], refs[4 + nr + 2 * ng:])
        k = pl.program_id(2)
        if ng:
            step = (pl.program_id(0) * grid[1] + pl.program_id(1)) * grid[2] + k
            pl.when(step == 0)(functools.partial(_scatter_phase, *comm, True))
        part = lax.dot_general(a_ref[...].astype(mxu), b_ref[...].astype(mxu), dims, preferred_element_type=F32)

        @pl.when(k == 0)
        def _():
            acc[...] = part

        @pl.when(k > 0)
        def _():
            acc[...] += part

        @pl.when(k == nk - 1)
        def _():
            r = acc[...]
            if r_ref is not None:
                r = r + r_ref[...]
            o_ref[...] = r.astype(o_ref.dtype)

        if ng:
            pl.when(step == grid[0] * grid[1] * grid[2] - 1)(functools.partial(_scatter_phase, *comm, False))

    a_spec = pl.BlockSpec((tk, tm), lambda i, j, k: (k, i)) if ta else pl.BlockSpec((tm, tk), lambda i, j, k: (i, k))
    b_spec = pl.BlockSpec((tn, tk), lambda i, j, k: (j, k)) if tb else pl.BlockSpec((tk, tn), lambda i, j, k: (k, j))
    o_spec = pl.BlockSpec((tm, tn), lambda i, j, k: (i, j))
    in_specs = [a_spec, b_spec] + ([o_spec] if nr else []) + [HBM_SPEC] * ng
    args = (a, b) + ((resid,) if nr else ()) + tuple(scatter)
    out = jax.ShapeDtypeStruct((M, N), out_dtype)
    sem = ("arbitrary",) * 3 if ng else ("parallel", "parallel", "arbitrary")
    return pl.pallas_call(
        body, name=name, grid=grid, in_specs=in_specs, out_specs=[o_spec] + [HBM_SPEC] * ng if ng else o_spec,
        out_shape=[out] + _scatter_shapes(scatter) if ng else out,
        scratch_shapes=[pltpu.VMEM((tm, tn), F32)] + (_scatter_sems(ng) if ng else []),
        compiler_params=_cp(*sem))(*args)


def _rms_fwd(h, w, name):
    T = h.shape[0]
    tb = _div(T, ROW_TILE, 8)

    def body(h_ref, w_ref, o_ref):
        x = h_ref[...]
        r = lax.rsqrt(jnp.mean(x * x, axis=-1, keepdims=True) + EPS)
        o_ref[...] = (x * r * w_ref[...]).astype(o_ref.dtype)

    return pl.pallas_call(
        body, name=name, grid=(T // tb,),
        in_specs=[pl.BlockSpec((tb, D), lambda i: (i, 0)), pl.BlockSpec((1, D), lambda i: (0, 0))],
        out_specs=pl.BlockSpec((tb, D), lambda i: (i, 0)), out_shape=jax.ShapeDtypeStruct((T, D), _BF),
        compiler_params=_cp("parallel"))(h, w)


def _rms_bwd(h, w, dy, dres, name):
    T = h.shape[0]
    tb = _div(T, ROW_TILE, 8)

    def body(h_ref, w_ref, dy_ref, dr_ref, dh_ref, dw_ref):
        x = h_ref[...]
        r = lax.rsqrt(jnp.mean(x * x, axis=-1, keepdims=True) + EPS)
        xh = x * r
        dy = dy_ref[...]
        g = dy * w_ref[...]
        dh_ref[...] = dr_ref[...] + r * (g - xh * jnp.mean(xh * g, axis=-1, keepdims=True))
        part = jnp.sum(dy * xh, axis=0, keepdims=True)

        @pl.when(pl.program_id(0) == 0)
        def _():
            dw_ref[...] = part

        @pl.when(pl.program_id(0) > 0)
        def _():
            dw_ref[...] += part

    row = pl.BlockSpec((tb, D), lambda i: (i, 0))
    one = pl.BlockSpec((1, D), lambda i: (0, 0))
    return pl.pallas_call(
        body, name=name, grid=(T // tb,), in_specs=[row, one, row, row], out_specs=[row, one],
        out_shape=[jax.ShapeDtypeStruct((T, D), F32), jax.ShapeDtypeStruct((1, D), F32)],
        compiler_params=_cp("arbitrary"))(h, w, dy, dres)


def _fgate_fwd(small, bias):
    T = small.shape[0]
    tb = _div(T, ROW_TILE, 8)

    def body(s_ref, b_ref, c_ref, carry):
        @pl.when(pl.program_id(0) == 0)
        def _():
            carry[...] = jnp.zeros_like(carry)

        lf = jax.nn.log_sigmoid(s_ref[...] + b_ref[...])
        r = lax.broadcasted_iota(jnp.int32, (tb, tb), 0)
        c = lax.broadcasted_iota(jnp.int32, (tb, tb), 1)
        tri = (r >= c).astype(F32)
        cs = jnp.dot(tri, lf, precision=HI, preferred_element_type=F32) + carry[...]
        c_ref[...] = cs
        carry[...] = cs[tb - 1:tb, :]

    return pl.pallas_call(
        body, name="fgate_fwd", grid=(T // tb,),
        in_specs=[pl.BlockSpec((tb, SMALL_W), lambda i: (i, 0)), pl.BlockSpec((1, SMALL_W), lambda i: (0, 0))],
        out_specs=pl.BlockSpec((tb, SMALL_W), lambda i: (i, 0)), out_shape=jax.ShapeDtypeStruct((T, SMALL_W), F32),
        scratch_shapes=[pltpu.VMEM((1, SMALL_W), F32)], compiler_params=_cp("arbitrary"))(small, bias)


def _fgate_bwd(small, bias, dc):
    T = small.shape[0]
    tb = _div(T, ROW_TILE, 8)
    nb = T // tb

    def body(s_ref, b_ref, dc_ref, df_ref, db_ref, carry):
        @pl.when(pl.program_id(0) == 0)
        def _():
            carry[...] = jnp.zeros_like(carry)

        r = lax.broadcasted_iota(jnp.int32, (tb, tb), 0)
        c = lax.broadcasted_iota(jnp.int32, (tb, tb), 1)
        tri = (r <= c).astype(F32)
        dlf = jnp.dot(tri, dc_ref[...], precision=HI, preferred_element_type=F32) + carry[...]
        carry[...] = dlf[0:1, :]
        df = dlf * _sig(-(s_ref[...] + b_ref[...]))
        df_ref[...] = df
        part = jnp.sum(df, axis=0, keepdims=True)

        @pl.when(pl.program_id(0) == 0)
        def _():
            db_ref[...] = part

        @pl.when(pl.program_id(0) > 0)
        def _():
            db_ref[...] += part

    rev = pl.BlockSpec((tb, SMALL_W), lambda i: (nb - 1 - i, 0))
    one = pl.BlockSpec((1, SMALL_W), lambda i: (0, 0))
    return pl.pallas_call(
        body, name="fgate_bwd", grid=(nb,), in_specs=[rev, one, rev], out_specs=[rev, one],
        out_shape=[jax.ShapeDtypeStruct((T, SMALL_W), F32), jax.ShapeDtypeStruct((1, SMALL_W), F32)],
        scratch_shapes=[pltpu.VMEM((1, SMALL_W), F32)], compiler_params=_cp("arbitrary"))(small, bias, dc)


def _att_rows(a8, tb):
    T = a8.shape[0]
    return a8.T.reshape(HEADS // 2, 2, T // tb, tb).transpose(0, 2, 1, 3)


def _att_cols(a8):
    return jnp.repeat(a8, HD, axis=1)


def _pair_cols(tb):
    return pl.BlockSpec((1, tb, 2), lambda p, i, *_: (p, i, 0))


def _from_pairs(a):
    return a.transpose(1, 0, 2).reshape(a.shape[1], HEADS)


EXP_ZERO = -104.0
SKIP_SLACK = 2.0
NORM_SLACK = 1.02


def _att_stats(qkv, c_cols, ones_blk):
    T = qkv.shape[0]
    tb = _div(T, ATT_TILE)

    def body(q_ref, k_ref, c_ref, e_ref, o_ref):
        q = q_ref[...].astype(F32)
        k = k_ref[...].astype(F32)
        e = e_ref[...].astype(_BF)
        qn = jnp.max(jnp.dot((q * q).astype(_BF), e, preferred_element_type=F32), axis=0, keepdims=True)
        kn = jnp.max(jnp.dot((k * k).astype(_BF), e, preferred_element_type=F32), axis=0, keepdims=True)
        c = c_ref[...]
        o_ref[0] = jnp.concatenate([jnp.sqrt(qn), jnp.sqrt(kn), jnp.max(c, axis=0, keepdims=True),
                                    jnp.min(c, axis=0, keepdims=True), jnp.zeros((4, FW), F32)], axis=0)

    blk = lambda off: pl.BlockSpec((tb, FW), lambda i: (i, off))
    return pl.pallas_call(
        body, name="att_stats", grid=(T // tb,),
        in_specs=[blk(0), blk(1), blk(0), pl.BlockSpec((FW, FW), lambda i: (0, 0))],
        out_specs=pl.BlockSpec((1, 8, FW), lambda i: (i, 0, 0)), out_shape=jax.ShapeDtypeStruct((T // tb, 8, FW), F32),
        compiler_params=_cp("parallel"))(qkv, qkv, c_cols, ones_blk)


def _att_plan(stats):
    nb = stats.shape[0]
    st = stats[:, :4, ::HD]
    qmax, kmax, cmax, cmin = (st[:, r, :].T for r in range(4))
    bound = (HD ** -0.5) * NORM_SLACK * qmax[:, :, None] * (kmax[:, None, :] + kmax[:, :, None]) \
        + cmax[:, :, None] - cmin[:, None, :] + SKIP_SLACK
    ii = lax.broadcasted_iota(jnp.int32, (nb, nb), 0)
    jj = lax.broadcasted_iota(jnp.int32, (nb, nb), 1)
    skip = (bound < EXP_ZERO) & (jj < ii)[None]
    live = (~skip & (jj <= ii)[None]).reshape(HEADS // 2, 2, nb, nb).any(axis=1)
    jfirst = jnp.argmax(live, axis=2).astype(jnp.int32)
    ilast = (nb - 1 - jnp.argmax(live[:, ::-1, :], axis=1)).astype(jnp.int32)
    return skip.astype(jnp.int32).reshape(-1), jfirst.reshape(-1), ilast.reshape(-1)


def _fox_fwd(qkv, c_cols, c_rows, plan):
    T = qkv.shape[0]
    tb = _div(T, ATT_TILE)
    nb = T // tb
    npair = HEADS // 2
    scale = HD ** -0.5
    dn = (((1,), (1,)), ((), ()))

    def body(skip_ref, jfirst_ref, ilast_ref, q_ref, k_ref, v_ref, cq_ref, ck_ref, o_ref, l_ref, l2_ref, m_scr, acc_scr):
        pr = pl.program_id(0)
        i = pl.program_id(1)
        q = q_ref[...]
        lane = lax.broadcasted_iota(jnp.int32, (1, 2 * HD), 1)
        sel0 = lane < HD
        zero = jnp.zeros_like(q)
        qh = (jnp.where(sel0, q, zero) * scale, jnp.where(sel0, zero, q) * scale)
        cq = (cq_ref[:, 0:1], cq_ref[:, HD:HD + 1])
        row = i * tb + lax.broadcasted_iota(jnp.int32, (tb, 1), 0)
        m_scr[...] = jnp.full(m_scr.shape, NEG, F32)
        acc_scr[...] = jnp.zeros(acc_scr.shape, F32)

        def tile(j, h, masked):
            start = pl.multiple_of(j * tb, tb)
            kb = k_ref[pl.ds(start, tb), :]
            vb = v_ref[pl.ds(start, tb), :]
            one = jnp.ones_like(vb)
            vh = jnp.where(sel0, vb, one) if h == 0 else jnp.where(sel0, one, vb)
            t = lax.dot_general(qh[h], kb, dn, preferred_element_type=F32) - ck_ref[0, j, h:h + 1, :]
            if masked:
                col = j * tb + lax.broadcasted_iota(jnp.int32, (1, tb), 1)
                mask = (col <= row) & (col >= FRONT)
                t = jnp.where(mask, t, NEG)
            m = m_scr[h]
            m_new = jnp.maximum(m, jnp.max(t, axis=-1, keepdims=True) + cq[h])
            p = jnp.exp(t + (cq[h] - m_new))
            if masked:
                p = jnp.where(mask, p, 0.0)
            m_scr[h] = m_new
            acc_scr[h] = acc_scr[h] * jnp.exp(m - m_new) + jnp.dot(p.astype(vb.dtype), vh, preferred_element_type=F32)

        def step(j, _):
            edge = (j == 0) | (j == i)
            for h in range(2):
                live = skip_ref[((2 * pr + h) * nb + i) * nb + j] == 0
                pl.when(live & edge)(functools.partial(tile, j, h, True))
                pl.when(live & jnp.logical_not(edge))(functools.partial(tile, j, h, False))
            return 0

        lax.fori_loop(jfirst_ref[pr * nb + i], i + 1, step, 0)
        outs, lses = [], []
        for h in range(2):
            l = acc_scr[h][:, (1 - h) * HD:(1 - h) * HD + 1]
            ok = l > 0.0
            ls = jnp.where(ok, l, 1.0)
            outs.append(jnp.where(ok, acc_scr[h] / ls, 0.0))
            lses.append(jnp.where(ok, m_scr[h] + jnp.log(ls), 0.0))
        o_ref[...] = jnp.where(sel0, outs[0], outs[1])
        l_ref[...] = jnp.where(sel0, lses[0], lses[1])
        l2_ref[0, :, 0:1] = lses[0]
        l2_ref[0, :, 1:2] = lses[1]

    blk = lambda off: pl.BlockSpec((tb, 2 * HD), lambda p, i, *_: (i, off + p))
    full = lambda off: pl.BlockSpec((T, 2 * HD), lambda p, i, *_: (0, off + p))
    rows = pl.BlockSpec((1, nb, 2, tb), lambda p, i, *_: (p, 0, 0, 0))
    return pl.pallas_call(
        body, name="fox_fwd",
        grid_spec=pltpu.PrefetchScalarGridSpec(
            num_scalar_prefetch=3, grid=(npair, nb),
            in_specs=[blk(0), full(npair), full(2 * npair), blk(0), rows], out_specs=[blk(0), blk(0), _pair_cols(tb)],
            scratch_shapes=[pltpu.VMEM((2, tb, 1), F32), pltpu.VMEM((2, tb, 2 * HD), F32)]),
        out_shape=[jax.ShapeDtypeStruct((T, FW), F32), jax.ShapeDtypeStruct((T, FW), F32),
                   jax.ShapeDtypeStruct((npair, T, 2), F32)],
        compiler_params=_cp("parallel", "arbitrary"))(*plan, qkv, qkv, qkv, c_cols, c_rows)


def _fox_bwd_dq(qkv, do, c_cols, c_rows, lse, delta, plan):
    T = qkv.shape[0]
    tb = _div(T, ATT_TILE)
    nb = T // tb
    npair = HEADS // 2
    scale = HD ** -0.5
    dn = (((1,), (1,)), ((), ()))

    def body(skip_ref, jfirst_ref, ilast_ref, q_ref, k_ref, v_ref, do_ref, cq_ref, ck_ref, l_ref, dl_ref, dq_ref, dc_ref,
             dq_scr):
        pr = pl.program_id(0)
        i = pl.program_id(1)
        q = q_ref[...]
        do = do_ref[...]
        lane = lax.broadcasted_iota(jnp.int32, (1, 2 * HD), 1)
        sel0 = lane < HD
        qh = (jnp.where(sel0, q, jnp.zeros_like(q)) * scale, jnp.where(sel0, jnp.zeros_like(q), q) * scale)
        doh = (jnp.where(sel0, do, jnp.zeros_like(do)), jnp.where(sel0, jnp.zeros_like(do), do))
        ce = (cq_ref[:, 0:1] - l_ref[:, 0:1], cq_ref[:, HD:HD + 1] - l_ref[:, HD:HD + 1])
        dl = (dl_ref[:, 0:1], dl_ref[:, HD:HD + 1])
        row = i * tb + lax.broadcasted_iota(jnp.int32, (tb, 1), 0)
        dq_scr[...] = jnp.zeros(dq_scr.shape, F32)

        def tile(j, h, masked):
            start = pl.multiple_of(j * tb, tb)
            kb = k_ref[pl.ds(start, tb), :]
            vb = v_ref[pl.ds(start, tb), :]
            one = jnp.ones_like(kb)
            kh = jnp.where(sel0, kb, one) if h == 0 else jnp.where(sel0, one, kb)
            t = lax.dot_general(qh[h], kb, dn, preferred_element_type=F32) - ck_ref[0, j, h:h + 1, :]
            if masked:
                col = j * tb + lax.broadcasted_iota(jnp.int32, (1, tb), 1)
                mask = (col <= row) & (col >= FRONT)
                p = jnp.where(mask, jnp.exp(jnp.where(mask, t, NEG) + ce[h]), 0.0)
            else:
                p = jnp.exp(t + ce[h])
            dp = lax.dot_general(doh[h], vb, dn, preferred_element_type=F32)
            ds = p * (dp - dl[h])
            dq_scr[h] += jnp.dot(ds.astype(kb.dtype), kh, preferred_element_type=F32)

        def step(j, _):
            edge = (j == 0) | (j == i)
            for h in range(2):
                live = skip_ref[((2 * pr + h) * nb + i) * nb + j] == 0
                pl.when(live & edge)(functools.partial(tile, j, h, True))
                pl.when(live & jnp.logical_not(edge))(functools.partial(tile, j, h, False))
            return 0

        lax.fori_loop(jfirst_ref[pr * nb + i], i + 1, step, 0)
        dq_ref[...] = (jnp.where(sel0, dq_scr[0], dq_scr[1]) * scale).astype(dq_ref.dtype)
        dc_ref[0, :, 0:1] = dq_scr[0][:, HD:HD + 1]
        dc_ref[0, :, 1:2] = dq_scr[1][:, 0:1]

    blk = lambda off: pl.BlockSpec((tb, 2 * HD), lambda p, i, *_: (i, off + p))
    full = lambda off: pl.BlockSpec((T, 2 * HD), lambda p, i, *_: (0, off + p))
    rows = pl.BlockSpec((1, nb, 2, tb), lambda p, i, *_: (p, 0, 0, 0))
    return pl.pallas_call(
        body, name="fox_bwd_dq",
        grid_spec=pltpu.PrefetchScalarGridSpec(
            num_scalar_prefetch=3, grid=(npair, nb),
            in_specs=[blk(0), full(npair), full(2 * npair), blk(0), blk(0), rows, blk(0), blk(0)],
            out_specs=[blk(0), _pair_cols(tb)],
            scratch_shapes=[pltpu.VMEM((2, tb, 2 * HD), F32)]),
        out_shape=[jax.ShapeDtypeStruct((T, FW), _BF), jax.ShapeDtypeStruct((npair, T, 2), F32)],
        compiler_params=_cp("parallel", "arbitrary"))(*plan, qkv, qkv, qkv, do, c_cols, c_rows, lse, delta)


def _fox_bwd_dkv(qkv, do, c_cols, ce_rows, delta_rows, plan):
    T = qkv.shape[0]
    tb = _div(T, ATT_TILE)
    nb = T // tb
    npair = HEADS // 2
    scale = HD ** -0.5
    dn = (((1,), (1,)), ((), ()))

    def body(skip_ref, jfirst_ref, ilast_ref, q_ref, k_ref, v_ref, do_ref, ck_ref, ce_ref, dl_ref,
             dk_ref, dv_ref, dc_ref, dk_scr, dv_scr):
        pr = pl.program_id(0)
        jb = pl.program_id(1)
        k = k_ref[...]
        v = v_ref[...]
        lane = lax.broadcasted_iota(jnp.int32, (1, 2 * HD), 1)
        sel0 = lane < HD
        kh = (jnp.where(sel0, k, jnp.zeros_like(k)) * scale, jnp.where(sel0, jnp.zeros_like(k), k) * scale)
        vh = (jnp.where(sel0, v, jnp.zeros_like(v)), jnp.where(sel0, jnp.zeros_like(v), v))
        ck = (ck_ref[:, 0:1], ck_ref[:, HD:HD + 1])
        kidx = jb * tb + lax.broadcasted_iota(jnp.int32, (tb, 1), 0)
        dk_scr[...] = jnp.zeros(dk_scr.shape, F32)
        dv_scr[...] = jnp.zeros(dv_scr.shape, F32)

        def tile(i, h, masked):
            start = pl.multiple_of(i * tb, tb)
            qb = q_ref[pl.ds(start, tb), :]
            dob = do_ref[pl.ds(start, tb), :]
            one = jnp.ones_like(qb)
            qh = jnp.where(sel0, qb, one) if h == 0 else jnp.where(sel0, one, qb)
            t = lax.dot_general(kh[h], qb, dn, preferred_element_type=F32) - ck[h]
            ce = ce_ref[0, i, h:h + 1, :]
            if masked:
                qidx = i * tb + lax.broadcasted_iota(jnp.int32, (1, tb), 1)
                mask = (kidx <= qidx) & (kidx >= FRONT)
                pt = jnp.where(mask, jnp.exp(jnp.where(mask, t, NEG) + ce), 0.0)
            else:
                pt = jnp.exp(t + ce)
            dv_scr[h] += jnp.dot(pt.astype(dob.dtype), dob, preferred_element_type=F32)
            dpt = lax.dot_general(vh[h], dob, dn, preferred_element_type=F32)
            dst = pt * (dpt - dl_ref[0, i, h:h + 1, :])
            dk_scr[h] += jnp.dot(dst.astype(qb.dtype), qh, preferred_element_type=F32)

        def step(i, _):
            edge = (i == jb) | (jb == 0)
            for h in range(2):
                live = skip_ref[((2 * pr + h) * nb + i) * nb + jb] == 0
                pl.when(live & edge)(functools.partial(tile, i, h, True))
                pl.when(live & jnp.logical_not(edge))(functools.partial(tile, i, h, False))
            return 0

        lax.fori_loop(jb, ilast_ref[pr * nb + jb] + 1, step, 0)
        dk_ref[...] = (jnp.where(sel0, dk_scr[0], dk_scr[1]) * scale).astype(dk_ref.dtype)
        dv_ref[...] = jnp.where(sel0, dv_scr[0], dv_scr[1]).astype(dv_ref.dtype)
        dc_ref[0, :, 0:1] = -dk_scr[0][:, HD:HD + 1]
        dc_ref[0, :, 1:2] = -dk_scr[1][:, 0:1]

    blk = lambda off: pl.BlockSpec((tb, 2 * HD), lambda p, j, *_: (j, off + p))
    full = lambda off: pl.BlockSpec((T, 2 * HD), lambda p, j, *_: (0, off + p))
    rows = pl.BlockSpec((1, nb, 2, tb), lambda p, j, *_: (p, 0, 0, 0))
    return pl.pallas_call(
        body, name="fox_bwd_dkv",
        grid_spec=pltpu.PrefetchScalarGridSpec(
            num_scalar_prefetch=3, grid=(npair, nb),
            in_specs=[full(0), blk(npair), blk(2 * npair), full(0), blk(0), rows, rows],
            out_specs=[blk(0), blk(0), _pair_cols(tb)],
            scratch_shapes=[pltpu.VMEM((2, tb, 2 * HD), F32), pltpu.VMEM((2, tb, 2 * HD), F32)]),
        out_shape=[jax.ShapeDtypeStruct((T, FW), _BF)] * 2 + [jax.ShapeDtypeStruct((npair, T, 2), F32)],
        compiler_params=_cp("parallel", "arbitrary"))(*plan, qkv, qkv, qkv, do, c_cols, ce_rows, delta_rows)


def _head_dot(a, b, ones_blk, ones_col):
    T = a.shape[0]
    tb = _div(T, ROW_TILE, 8)

    def body(a_ref, b_ref, e_ref, c_ref, o_ref, o8_ref):
        prod = a_ref[...].astype(F32) * b_ref[...].astype(F32)
        o_ref[...] = jnp.dot(prod, e_ref[...], precision=HI, preferred_element_type=F32)
        o8_ref[...] = jnp.dot(prod, c_ref[...], precision=HI, preferred_element_type=F32)

    row = pl.BlockSpec((tb, FW), lambda i: (i, 0))
    return pl.pallas_call(
        body, name="head_dot", grid=(T // tb,),
        in_specs=[row, row, pl.BlockSpec((FW, FW), lambda i: (0, 0)), pl.BlockSpec((FW, SMALL_W), lambda i: (0, 0))],
        out_specs=[row, pl.BlockSpec((tb, SMALL_W), lambda i: (i, 0))],
        out_shape=[jax.ShapeDtypeStruct((T, FW), F32), jax.ShapeDtypeStruct((T, SMALL_W), F32)],
        compiler_params=_cp("parallel"))(a, b, ones_blk, ones_col)


CONV_RC = 32
CONV_TC = 256


def _rc(r):
    return pl.ds(r * CONV_RC if isinstance(r, int) else pl.multiple_of(r * CONV_RC, CONV_RC), CONV_RC)


def _prev_rows(x_ref, h_ref, first):
    def at(r):
        if isinstance(r, int):
            assert r == 0
            return jnp.concatenate([jnp.where(first, 0.0, h_ref[...]), x_ref[0:CONV_RC, :]], axis=0)
        return x_ref[pl.ds(pl.multiple_of(r * CONV_RC - 8, 8), CONV_RC + 8), :]
    return at


def _windows_prev(x, kw):
    return [x[8:] if k == kw - 1 else pltpu.roll(x, kw - 1 - k, 0)[8:] for k in range(kw)]


def _fold8(a):
    return a.reshape(CONV_RC // 8, 8, a.shape[-1]).sum(axis=0)


def _halo_prev(tb, tc, off=0):
    return pl.BlockSpec((8, tc), lambda j, i: (jnp.maximum(i * (tb // 8) - 1, 0), j + off))


def _gconv_fwd(rest, w):
    T = rest.shape[0]
    C = 3 * FW
    kw = w.shape[0]
    tb, tc = _div(T, 1280, CONV_RC), CONV_TC

    def body(x_ref, h_ref, w_ref, o_ref):
        at = _prev_rows(x_ref, h_ref, pl.program_id(1) == 0)
        wv = w_ref[...]

        def chunk(r, _):
            win = _windows_prev(at(r), kw)
            u = sum(wv[k:k + 1, :] * win[k] for k in range(kw))
            o_ref[_rc(r), :] = u * _sig(u)
            return 0

        chunk(0, 0)
        lax.fori_loop(1, tb // CONV_RC, chunk, 0)

    return pl.pallas_call(
        body, name="gconv_fwd", grid=(C // tc, T // tb),
        in_specs=[pl.BlockSpec((tb, tc), lambda j, i: (i, j)), _halo_prev(tb, tc), pl.BlockSpec((kw, tc), lambda j, i: (0, j))],
        out_specs=pl.BlockSpec((tb, tc), lambda j, i: (i, j)), out_shape=jax.ShapeDtypeStruct((T, C), F32),
        compiler_params=_cp("parallel", "arbitrary"))(rest, rest, w)


def _gconv_bwd_du(rest, w, dy):
    T = rest.shape[0]
    C = 3 * FW
    kw = w.shape[0]
    tb, tc = _div(T, 1280, CONV_RC), CONV_TC

    def body(x_ref, h_ref, w_ref, dy_ref, du_ref, dw_ref):
        i = pl.program_id(1)
        at = _prev_rows(x_ref, h_ref, i == 0)
        wv = w_ref[...]

        def chunk(r, acc):
            rows = _rc(r)
            win = _windows_prev(at(r), kw)
            u = sum(wv[k:k + 1, :] * win[k] for k in range(kw))
            sg = _sig(u)
            du = dy_ref[rows, :] * sg * (1.0 + u * (1.0 - sg))
            du_ref[rows, :] = du
            return tuple(acc[k] + _fold8(du * win[k]) for k in range(kw))

        acc = chunk(0, tuple(jnp.zeros((8, tc), F32) for _ in range(kw)))
        acc = lax.fori_loop(1, tb // CONV_RC, chunk, acc)
        part = jnp.concatenate([jnp.sum(a, axis=0, keepdims=True) for a in acc], axis=0)

        @pl.when(i == 0)
        def _():
            dw_ref[...] = part

        @pl.when(i > 0)
        def _():
            dw_ref[...] += part

    blk = pl.BlockSpec((tb, tc), lambda j, i: (i, j))
    wsp = pl.BlockSpec((kw, tc), lambda j, i: (0, j))
    return pl.pallas_call(
        body, name="gconv_bwd_du", grid=(C // tc, T // tb), in_specs=[blk, _halo_prev(tb, tc), wsp, blk],
        out_specs=[blk, wsp], out_shape=[jax.ShapeDtypeStruct((T, C), F32), jax.ShapeDtypeStruct((kw, C), F32)],
        compiler_params=_cp("parallel", "arbitrary"))(rest, rest, w, dy)


def _conv_bwd_dx(du, w, out_dtype, name):
    T, C = du.shape
    kw = w.shape[0]
    tb = _div(T, 1280, CONV_RC)
    tc = CONV_TC
    nb = T // tb

    last = tb // CONV_RC - 1

    def body(x_ref, h_ref, w_ref, o_ref):
        wv = w_ref[...]

        def chunk(r, _):
            if isinstance(r, int):
                tail = jnp.where(pl.program_id(1) == nb - 1, 0.0, h_ref[...])
                x = jnp.concatenate([x_ref[r * CONV_RC:(r + 1) * CONV_RC, :], tail], axis=0)
            else:
                x = x_ref[pl.ds(pl.multiple_of(r * CONV_RC, CONV_RC), CONV_RC + 8), :]
            acc = wv[kw - 1:kw, :] * x[:CONV_RC]
            for k in range(kw - 1):
                acc = acc + wv[k:k + 1, :] * pltpu.roll(x, CONV_RC + 8 - (kw - 1 - k), 0)[:CONV_RC]
            o_ref[_rc(r), :] = acc.astype(o_ref.dtype)
            return 0

        lax.fori_loop(0, last, chunk, 0)
        chunk(last, 0)

    halo = pl.BlockSpec((8, tc), lambda j, i: (jnp.minimum((i + 1) * (tb // 8), T // 8 - 1), j))
    return pl.pallas_call(
        body, name=name, grid=(C // tc, nb),
        in_specs=[pl.BlockSpec((tb, tc), lambda j, i: (i, j)), halo, pl.BlockSpec((kw, tc), lambda j, i: (0, j))],
        out_specs=pl.BlockSpec((tb, tc), lambda j, i: (i, j)), out_shape=jax.ShapeDtypeStruct((T, C), out_dtype),
        compiler_params=_cp("parallel", "arbitrary"))(du, du, w)


def _glu_fwd(up, w, b):
    T = up.shape[0]
    kw = w.shape[0]
    tb, tc = _div(T, 1280, CONV_RC), CONV_TC
    nc = DFF // tc

    def body(xg, hg, xu, hu, wg, wu, bg, bu, o_ref):
        first = pl.program_id(1) == 0
        atg, atu = _prev_rows(xg, hg, first), _prev_rows(xu, hu, first)
        wgv, wuv, bgv, buv = wg[...], wu[...], bg[...], bu[...]

        def chunk(r, _):
            wing, winu = _windows_prev(atg(r), kw), _windows_prev(atu(r), kw)
            ug = bgv + sum(wgv[k:k + 1, :] * wing[k] for k in range(kw))
            uu = buv + sum(wuv[k:k + 1, :] * winu[k] for k in range(kw))
            o_ref[_rc(r), :] = (ug * _sig(ug) * uu).astype(o_ref.dtype)
            return 0

        chunk(0, 0)
        lax.fori_loop(1, tb // CONV_RC, chunk, 0)

    blk = lambda off: pl.BlockSpec((tb, tc), lambda j, i: (i, j + off))
    wsp = lambda off: pl.BlockSpec((kw, tc), lambda j, i: (0, j + off))
    bsp = lambda off: pl.BlockSpec((1, tc), lambda j, i: (0, j + off))
    return pl.pallas_call(
        body, name="glu_fwd", grid=(nc, T // tb),
        in_specs=[blk(0), _halo_prev(tb, tc), blk(nc), _halo_prev(tb, tc, nc), wsp(0), wsp(nc), bsp(0), bsp(nc)],
        out_specs=blk(0), out_shape=jax.ShapeDtypeStruct((T, DFF), _BF),
        compiler_params=_cp("parallel", "arbitrary"))(up, up, up, up, w, w, b, b)


def _glu_bwd_du(up, w, b, df):
    T = up.shape[0]
    kw = w.shape[0]
    tb, tc = _div(T, 1280, CONV_RC), CONV_TC
    nc = DFF // tc

    def body(xg, hg, xu, hu, wg, wu, bg, bu, df_ref, dug_ref, duu_ref, dwg_ref, dwu_ref, dbg_ref, dbu_ref):
        i = pl.program_id(1)
        atg, atu = _prev_rows(xg, hg, i == 0), _prev_rows(xu, hu, i == 0)
        wgv, wuv, bgv, buv = wg[...], wu[...], bg[...], bu[...]

        def chunk(r, acc):
            rows = _rc(r)
            wing, winu = _windows_prev(atg(r), kw), _windows_prev(atu(r), kw)
            ug = bgv + sum(wgv[k:k + 1, :] * wing[k] for k in range(kw))
            uu = buv + sum(wuv[k:k + 1, :] * winu[k] for k in range(kw))
            s = _sig(ug)
            df = df_ref[rows, :]
            dug = df * uu * s * (1.0 + ug * (1.0 - s))
            duu = df * ug * s
            dug_ref[rows, :] = dug
            duu_ref[rows, :] = duu
            new = [dug * wing[k] for k in range(kw)] + [duu * winu[k] for k in range(kw)] + [dug, duu]
            return tuple(a + _fold8(v) for a, v in zip(acc, new))

        acc = chunk(0, tuple(jnp.zeros((8, tc), F32) for _ in range(2 * kw + 2)))
        acc = lax.fori_loop(1, tb // CONV_RC, chunk, acc)
        col = [jnp.sum(a, axis=0, keepdims=True) for a in acc]
        parts = (jnp.concatenate(col[:kw], axis=0), jnp.concatenate(col[kw:2 * kw], axis=0), col[2 * kw], col[2 * kw + 1])
        accs = (dwg_ref, dwu_ref, dbg_ref, dbu_ref)

        @pl.when(i == 0)
        def _():
            for r, p in zip(accs, parts):
                r[...] = p

        @pl.when(i > 0)
        def _():
            for r, p in zip(accs, parts):
                r[...] += p

    blk = lambda off: pl.BlockSpec((tb, tc), lambda j, i: (i, j + off))
    wsp = lambda off: pl.BlockSpec((kw, tc), lambda j, i: (0, j + off))
    bsp = lambda off: pl.BlockSpec((1, tc), lambda j, i: (0, j + off))
    return pl.pallas_call(
        body, name="glu_bwd_du", grid=(nc, T // tb),
        in_specs=[blk(0), _halo_prev(tb, tc), blk(nc), _halo_prev(tb, tc, nc), wsp(0), wsp(nc), bsp(0), bsp(nc), blk(0)],
        out_specs=[blk(0), blk(0), wsp(0), wsp(0), bsp(0), bsp(0)],
        out_shape=[jax.ShapeDtypeStruct((T, DFF), F32)] * 2 + [jax.ShapeDtypeStruct((kw, DFF), F32)] * 2
        + [jax.ShapeDtypeStruct((1, DFF), F32)] * 2,
        compiler_params=_cp("parallel", "arbitrary"))(up, up, up, up, w, w, b, b, df)


def _mix_fwd(rest, gate_bias, y_fox, y_gdn):
    T = rest.shape[0]
    tb, tc = _div(T, ROW_TILE, 8), 512
    nc = D // tc
    og = OFF_GATES // tc

    def body(gf, gg, bf, bg, yf, yg, o_ref):
        o_ref[...] = (_sig(gf[...] + bf[...]) * yf[...] + _sig(gg[...] + bg[...]) * yg[...]).astype(o_ref.dtype)

    blk = lambda off: pl.BlockSpec((tb, tc), lambda i, j: (i, j + off))
    bsp = lambda off: pl.BlockSpec((1, tc), lambda i, j: (0, j + off))
    return pl.pallas_call(
        body, name="mix_fwd", grid=(T // tb, nc),
        in_specs=[blk(og), blk(og + nc), bsp(0), bsp(nc), blk(0), blk(0)], out_specs=blk(0),
        out_shape=jax.ShapeDtypeStruct((T, D), _BF), compiler_params=_cp("parallel", "parallel"))(
            rest, rest, gate_bias, gate_bias, y_fox, y_gdn)


def _mix_bwd(rest, gate_bias, y_fox, y_gdn, dmix):
    T = rest.shape[0]
    tb, tc = _div(T, ROW_TILE, 8), 512
    nc = D // tc
    og = OFF_GATES // tc

    def body(gf, gg, bf, bg, yf, yg, dm, dyf, dyg, dgf, dgg, dbf, dbg):
        i = pl.program_id(1)
        d = dm[...]
        sf = _sig(gf[...] + bf[...])
        sg = _sig(gg[...] + bg[...])
        dyf[...] = (d * sf).astype(dyf.dtype)
        dyg[...] = (d * sg).astype(dyg.dtype)
        a = d * yf[...] * sf * (1.0 - sf)
        b = d * yg[...] * sg * (1.0 - sg)
        dgf[...] = a.astype(dgf.dtype)
        dgg[...] = b.astype(dgg.dtype)
        pa = jnp.sum(a, axis=0, keepdims=True)
        pb = jnp.sum(b, axis=0, keepdims=True)

        @pl.when(i == 0)
        def _():
            dbf[...] = pa
            dbg[...] = pb

        @pl.when(i > 0)
        def _():
            dbf[...] += pa
            dbg[...] += pb

    blk = lambda off: pl.BlockSpec((tb, tc), lambda j, i: (i, j + off))
    bsp = lambda off: pl.BlockSpec((1, tc), lambda j, i: (0, j + off))
    return pl.pallas_call(
        body, name="mix_bwd", grid=(nc, T // tb),
        in_specs=[blk(og), blk(og + nc), bsp(0), bsp(nc), blk(0), blk(0), blk(0)],
        out_specs=[blk(0), blk(0), blk(0), blk(0), bsp(0), bsp(0)],
        out_shape=[jax.ShapeDtypeStruct((T, D), _BF)] * 4 + [jax.ShapeDtypeStruct((1, D), F32)] * 2,
        compiler_params=_cp("parallel", "arbitrary"))(rest, rest, gate_bias, gate_bias, y_fox, y_gdn, dmix)


def _loss_head(h2, w, target, n_valid):
    T = h2.shape[0]
    tb = _div(T, ROW_TILE, 8)

    def body(h_ref, w_ref, t_ref, dh_ref, loss_ref, dw_ref):
        i = pl.program_id(0)
        x = h_ref[...]
        r = lax.rsqrt(jnp.mean(x * x, axis=-1, keepdims=True) + EPS)
        xh = x * r
        row = i * tb + lax.broadcasted_iota(jnp.int32, (tb, 1), 0)
        valid = (row >= X0) & (row < X0 + n_valid)
        e = jnp.where(valid, xh * w_ref[...] - t_ref[...], 0.0)
        dy = e * (1.0 / D)
        g = dy * w_ref[...]
        dh_ref[...] = r * (g - xh * jnp.mean(xh * g, axis=-1, keepdims=True))
        lpart = 0.5 * jnp.sum(jnp.sum(e * e, axis=-1, keepdims=True) * (1.0 / D), axis=0, keepdims=True)
        wpart = jnp.sum(dy * xh, axis=0, keepdims=True)

        @pl.when(i == 0)
        def _():
            loss_ref[...] = lpart
            dw_ref[...] = wpart

        @pl.when(i > 0)
        def _():
            loss_ref[...] += lpart
            dw_ref[...] += wpart

    row = pl.BlockSpec((tb, D), lambda i: (i, 0))
    one = pl.BlockSpec((1, D), lambda i: (0, 0))
    return pl.pallas_call(
        body, name="loss_head", grid=(T // tb,), in_specs=[row, one, row],
        out_specs=[row, pl.BlockSpec((1, 1), lambda i: (0, 0)), one],
        out_shape=[jax.ShapeDtypeStruct((T, D), F32), jax.ShapeDtypeStruct((1, 1), F32), jax.ShapeDtypeStruct((1, D), F32)],
        compiler_params=_cp("arbitrary"))(h2, w, target)


def _bd_lo(a, b, ca, cb):
    return lax.dot_general(a.astype(_BF), b.astype(_BF), (((ca,), (cb,)), ((0,), (0,))), preferred_element_type=F32)


def _split2(a):
    hi = a.astype(_BF)
    return hi, (a - hi.astype(F32)).astype(_BF)


def _bd_hi(a, b, ca, cb, exact_a=False):
    dn = (((ca,), (cb,)), ((0,), (0,)))
    dot = lambda x, y: lax.dot_general(x, y, dn, preferred_element_type=F32)
    bh, bl = _split2(b)
    if exact_a:
        ah = a.astype(_BF)
        return dot(ah, bh) + dot(ah, bl)
    ah, al = _split2(a)
    return dot(ah, bh) + (dot(ah, bl) + dot(al, bh))


def _row_to_col(x):
    eye = lax.broadcasted_iota(jnp.int32, (1, CH, CH), 1) == lax.broadcasted_iota(jnp.int32, (1, CH, CH), 2)
    return jnp.sum(jnp.where(eye, jnp.broadcast_to(x, (HEADS, CH, CH)), 0.0), axis=2, keepdims=True)


def _col_to_row(x):
    eye = lax.broadcasted_iota(jnp.int32, (1, CH, CH), 1) == lax.broadcasted_iota(jnp.int32, (1, CH, CH), 2)
    return jnp.sum(jnp.where(eye, jnp.broadcast_to(x, (HEADS, CH, CH)), 0.0), axis=1, keepdims=True)


def _gdn_chunk(q, k, v, bpre, apre, alog, dtb):
    H = HEADS
    r = lax.broadcasted_iota(jnp.int32, (1, CH, CH), 1)
    c = lax.broadcasted_iota(jnp.int32, (1, CH, CH), 2)
    tril, strict = r >= c, r > c
    lb = jnp.broadcast_to(tril.astype(F32), (H, CH, CH))
    rq = lax.rsqrt(jnp.sum(q * q, axis=-1, keepdims=True) + EPS)
    rk = lax.rsqrt(jnp.sum(k * k, axis=-1, keepdims=True) + EPS)
    qh = q * rq
    qn = qh * (HD ** -0.5)
    kn = k * rk
    beta = _sig(bpre)
    x = apre + dtb
    ea = jnp.exp(alog)
    g = -ea * jax.nn.softplus(x)
    gb = jnp.broadcast_to(g, (H, CH, HD))
    gc = _bd_hi(lb, gb, 2, 1, True)
    dm = gc - _col_to_row(gc[:, :, 0:1])
    decay = jnp.where(tril, jnp.exp(jnp.where(tril, dm, 0.0)), 0.0)
    eg = jnp.exp(gc)
    gl = gc[:, CH - 1:CH, :]
    egl = jnp.exp(gl - gc)
    cd = jnp.exp(gl)
    kb = kn * beta
    vb = v * beta
    both = _bd_lo(_rows2(kb, qn), kn, 2, 2)
    kk, qk = both[:, :CH], both[:, CH:]
    pw = -jnp.where(strict, kk * decay, 0.0)
    tm = jnp.where(r == c, 1.0, 0.0) + pw
    pw = _bd_hi(pw, pw, 2, 1)
    for it in range(4):
        mul = _bd_hi if it < 2 else _bd_lo
        both = mul(_rows2(pw, tm), pw, 2, 1)
        pw, tm = both[:, :CH], tm + both[:, CH:]
    tm = tm + _bd_lo(tm, pw, 2, 1)
    kbg = kb * eg
    sol = _bd_hi(tm, _lanes2(vb, kbg), 2, 1)
    value, kcd = sol[:, :, :HD], sol[:, :, HD:]
    attn = jnp.where(tril, qk * decay, 0.0)
    return dict(tril=tril, strict=strict, lb=lb, rq=rq, rk=rk, qh=qh, qn=qn, kn=kn, beta=beta, x=x, ea=ea, g=g,
                decay=decay, eg=eg, egl=egl, cd=cd, kb=kb, vb=vb, kk=kk, tm=tm, kbg=kbg, value=value, kcd=kcd, qk=qk,
                attn=attn, qd=qn * eg, kt=kn * egl, sol=sol)


def _rows2(a, b):
    return jnp.concatenate([a, b], axis=1)


def _lanes2(a, b):
    return jnp.concatenate([a, b], axis=2)


GDN_CPS = 2
GDN_CPS_FWD = 4


def _gdn_specs(T, rev):
    G = GDN_CPS if rev else GDN_CPS_FWD
    ns = T // (G * CH)
    pos = (lambda n: ns - 1 - n) if rev else (lambda n: n)
    mat = pl.BlockSpec((HEADS, G * CH, HD), lambda n: (0, pos(n), 0))
    col = pl.BlockSpec((G, HEADS, 1, CH), lambda n: (pos(n), 0, 0, 0))
    sca = pl.BlockSpec((HEADS, 1, 1), lambda n: (0, 0, 0))
    nw = pl.BlockSpec((1, 1, HD), lambda n: (0, 0, 0))
    st = pl.BlockSpec((G, HEADS, HD, HD), lambda n: (pos(n), 0, 0, 0))
    tok = lambda width, off: pl.BlockSpec((G * CH, width), lambda n: (pos(n), off))
    return ns, mat, col, sca, nw, st, tok


def _split_heads(x):
    return [jnp.stack([x[:, (g * HEADS + h) * HD:(g * HEADS + h + 1) * HD] for h in range(HEADS)])
            for g in range(x.shape[1] // FW)]


def _store_heads(ref, rows, g, val):
    for h in range(HEADS):
        ref[rows, (g * HEADS + h) * HD:(g * HEADS + h + 1) * HD] = val[h]


def _gdn_fwd(conv, rest, bpre, apre, alog, dtb, nw, gather=()):
    T = conv.shape[0]
    ns, mat, col, sca, nws, st, tok = _gdn_specs(T, False)
    rows = [slice(g * CH, (g + 1) * CH) for g in range(GDN_CPS_FWD)]
    ng = len(gather)

    def body(*refs):
        c_ref, z_ref, b_ref, a_ref, al_ref, dt_ref, nw_ref = refs[:7]
        o_ref, og_ref, st_ref = refs[7 + ng:10 + ng]
        s_scr = refs[10 + 2 * ng]
        comm = (refs[7:7 + ng], refs[10 + ng:10 + 2 * ng], refs[11 + 2 * ng:])

        @pl.when(pl.program_id(0) == 0)
        def _():
            s_scr[...] = jnp.zeros_like(s_scr)
            if ng:
                _gather_phase(*comm, True)

        locs = [_gdn_chunk(*_split_heads(c_ref[rows[g], :]), _row_to_col(b_ref[g]), _row_to_col(a_ref[g]), al_ref[...],
                           dt_ref[...]) for g in range(GDN_CPS_FWD)]
        s = s_scr[...]
        for g, L in enumerate(locs):
            st_ref[g] = s
            both = _bd_lo(_rows2(L["kcd"], L["qd"]), s, 2, 1)
            v_new = L["value"] - both[:, :CH]
            o = both[:, CH:] + _bd_lo(L["attn"], v_new, 2, 1)
            s = s * L["cd"] + _bd_lo(L["kt"], v_new, 1, 1)
            o_ref[:, rows[g], :] = o
            zz, = _split_heads(z_ref[rows[g], :])
            rs = lax.rsqrt(jnp.mean(o * o, axis=-1, keepdims=True) + EPS)
            _store_heads(og_ref, rows[g], 0, o * rs * nw_ref[...] * zz * _sig(zz))
        s_scr[...] = s
        if ng:
            pl.when(pl.program_id(0) == ns - 1)(functools.partial(_gather_phase, *comm, False))

    return pl.pallas_call(
        body, name="gdn_fwd", grid=(ns,),
        in_specs=[tok(3 * FW, 0), tok(FW, OFF_Z // FW), col, col, sca, sca, nws] + [HBM_SPEC] * ng,
        out_specs=[mat, tok(FW, 0), st] + [HBM_SPEC] * ng,
        out_shape=[jax.ShapeDtypeStruct((HEADS, T, HD), F32), jax.ShapeDtypeStruct((T, FW), F32),
                   jax.ShapeDtypeStruct((T // CH, HEADS, HD, HD), F32)] + _gather_shapes(gather),
        scratch_shapes=[pltpu.VMEM((HEADS, HD, HD), F32)] + (_gather_sems(ng) if ng else []),
        compiler_params=_cp("arbitrary"))(conv, rest, bpre, apre, alog, dtb, nw, *gather)


def _gdn_bwd(conv, rest, bpre, apre, alog, dtb, nw, states, o, dog, scatter=()):
    T = conv.shape[0]
    ns, mat, col, sca, nws, st, tok = _gdn_specs(T, True)
    rows = [slice(g * CH, (g + 1) * CH) for g in range(GDN_CPS)]
    ng = len(scatter)

    def body(*refs):
        c_ref, z_ref, b_ref, a_ref, al_ref, dt_ref, nw_ref, st_ref, o_ref, dog_ref = refs[:10]
        dc_ref, dz_ref, db_ref, da_ref, dal_ref, ddt_ref, dnw_ref = refs[10 + ng:17 + ng]
        ds_scr = refs[17 + 2 * ng]
        comm = (refs[10:10 + ng], refs[17 + ng:17 + 2 * ng], refs[18 + 2 * ng:])

        @pl.when(pl.program_id(0) == 0)
        def _():
            ds_scr[...] = jnp.zeros_like(ds_scr)
            dal_ref[...] = jnp.zeros_like(dal_ref)
            ddt_ref[...] = jnp.zeros_like(ddt_ref)
            dnw_ref[...] = jnp.zeros_like(dnw_ref)
            if ng:
                _scatter_phase(*comm, True)

        splits = [_split_heads(c_ref[rows[g], :]) for g in range(GDN_CPS)]
        locs = [_gdn_chunk(*splits[g], _row_to_col(b_ref[g]), _row_to_col(a_ref[g]), al_ref[...], dt_ref[...])
                for g in range(GDN_CPS)]
        dsn = ds_scr[...]
        acc_al, acc_dt, acc_nw = 0.0, 0.0, 0.0
        for g in reversed(range(GDN_CPS)):
            L, vv = locs[g], splits[g][2]
            tril, strict, lb = L["tril"], L["strict"], L["lb"]
            qn, kn, kb, beta, decay, eg, egl, cd = L["qn"], L["kn"], L["kb"], L["beta"], L["decay"], L["eg"], L["egl"], L["cd"]
            value, kcd, attn, qd, kt, tm = L["value"], L["kcd"], L["attn"], L["qd"], L["kt"], L["tm"]
            s = st_ref[g]
            v_new = value - _bd_lo(kcd, s, 2, 1)
            oo = o_ref[:, rows[g], :]
            zz, = _split_heads(z_ref[rows[g], :])
            dog_, = _split_heads(dog_ref[rows[g], :])
            sz = _sig(zz)
            rs = lax.rsqrt(jnp.mean(oo * oo, axis=-1, keepdims=True) + EPS)
            oh = oo * rs
            _store_heads(dz_ref, rows[g], 0, dog_ * oh * nw_ref[...] * sz * (1.0 + zz * (1.0 - sz)))
            don = dog_ * zz * sz
            gdy = don * nw_ref[...]
            do = rs * (gdy - oh * jnp.mean(oh * gdy, axis=-1, keepdims=True))
            acc_nw = acc_nw + jnp.sum(don * oh, axis=(0, 1), keepdims=True)
            d_vnew = _bd_lo(attn, do, 1, 1) + _bd_lo(kt, dsn, 2, 1)
            both = _bd_lo(_rows2(do, d_vnew), s, 2, 2)
            d_qd, d_kcd = both[:, :CH], -both[:, CH:]
            d_attn = jnp.where(tril, _bd_lo(do, v_new, 2, 2), 0.0)
            d_kt = _bd_lo(v_new, dsn, 2, 2)
            d_cd = jnp.sum(s * dsn, axis=(1, 2), keepdims=True)
            dsn_next = cd * dsn + _bd_lo(_rows2(qd, kcd), _rows2(do, -d_vnew), 1, 1)
            dsol = _bd_hi(tm, _lanes2(d_vnew, d_kcd), 1, 1)
            d_vb, d_kbg = dsol[:, :, :HD], dsol[:, :, HD:]
            da = -jnp.where(strict, _bd_lo(dsol, L["sol"], 2, 2), 0.0)
            dkk = da * decay
            dqk = d_attn * decay
            d_decay = da * L["kk"] + d_attn * L["qk"]
            both = _bd_lo(_rows2(dkk, dqk), kn, 2, 1)
            d_kb = both[:, :CH] + d_kbg * eg
            d_qn = both[:, CH:] + d_qd * eg
            d_kn = _bd_lo(_rows2(dkk, dqk), _rows2(kb, qn), 1, 1) + d_kt * egl
            dd = d_decay * decay
            d_gc = jnp.sum(dd + d_qd * qd + d_kbg * L["kbg"] - d_kt * kt, axis=-1, keepdims=True) \
                - _row_to_col(jnp.sum(dd, axis=1, keepdims=True))
            d_gl = jnp.sum(d_kt * kt, axis=(1, 2), keepdims=True) + d_cd * cd[:, :, 0:1]
            last = lax.broadcasted_iota(jnp.int32, (1, CH, 1), 1) == CH - 1
            d_gc = d_gc + jnp.where(last, d_gl, 0.0)
            dg = _bd_hi(lb, jnp.broadcast_to(d_gc, (HEADS, CH, HD)), 1, 1, True)[:, :, 0:1]
            d_apre = -dg * L["ea"] * _sig(L["x"])
            da_ref[g] = _col_to_row(d_apre)
            acc_al = acc_al + jnp.sum(dg * L["g"], axis=1, keepdims=True)
            acc_dt = acc_dt + jnp.sum(d_apre, axis=1, keepdims=True)
            d_beta = jnp.sum(d_kb * kn + d_vb * vv, axis=-1, keepdims=True)
            db_ref[g] = _col_to_row(d_beta * beta * (1.0 - beta))
            d_kn = d_kn + d_kb * beta
            qh = L["qh"]
            _store_heads(dc_ref, rows[g], 0, (HD ** -0.5) * L["rq"] * (d_qn - qh * jnp.sum(qh * d_qn, axis=-1, keepdims=True)))
            _store_heads(dc_ref, rows[g], 1, L["rk"] * (d_kn - kn * jnp.sum(kn * d_kn, axis=-1, keepdims=True)))
            _store_heads(dc_ref, rows[g], 2, d_vb * beta)
            dsn = dsn_next
        ds_scr[...] = dsn
        dal_ref[...] += acc_al
        ddt_ref[...] += acc_dt
        dnw_ref[...] += acc_nw
        if ng:
            pl.when(pl.program_id(0) == ns - 1)(functools.partial(_scatter_phase, *comm, False))

    c3 = jax.ShapeDtypeStruct((T // CH, HEADS, 1, CH), F32)
    s3 = jax.ShapeDtypeStruct((HEADS, 1, 1), F32)
    return pl.pallas_call(
        body, name="gdn_bwd", grid=(ns,),
        in_specs=[tok(3 * FW, 0), tok(FW, OFF_Z // FW), col, col, sca, sca, nws, st, mat, tok(FW, 0)] + [HBM_SPEC] * ng,
        out_specs=[tok(3 * FW, 0), tok(FW, 0), col, col, sca, sca, nws] + [HBM_SPEC] * ng,
        out_shape=[jax.ShapeDtypeStruct((T, 3 * FW), F32), jax.ShapeDtypeStruct((T, FW), F32), c3, c3, s3, s3,
                   jax.ShapeDtypeStruct((1, 1, HD), F32)] + _scatter_shapes(scatter),
        scratch_shapes=[pltpu.VMEM((HEADS, HD, HD), F32)] + (_scatter_sems(ng) if ng else []),
        compiler_params=_cp("arbitrary"))(conv, rest, bpre, apre, alog, dtb, nw, states, o, dog, *scatter)


LATE_KEYS = {"w_branch_fox": "w_bfox", "w_branch_gdn": "w_bgdn", "w_out": "w_out", "ffn_w_up": "w_up", "ffn_w_down": "w_down"}


def _local_step(x, target, w, late=None):
    seq = x.shape[0]
    T = -(-(X0 + seq) // T_ALIGN) * T_ALIGN
    back = T - X0 - seq
    tb_att = _div(T, ATT_TILE)
    h0 = jnp.concatenate([jnp.zeros((FRONT, D), F32), w["meta"], x, jnp.zeros((back, D), F32)], axis=0)
    tgt = jnp.concatenate([jnp.zeros((X0, D), F32), target, jnp.zeros((back, D), F32)], axis=0)
    row = lambda v: v.reshape(1, -1)
    nmix, nffn, nfin = row(w["norm_mix"]), row(w["norm_ffn"]), row(w["norm_final"])
    gate_b = row(w["gate_bias"])
    fconv_b = row(w["ffn_conv_b"])
    bias128 = jnp.zeros((1, SMALL_W), F32).at[0, :HEADS].set(w["fgt_bias"])
    alog = w["a_log"].reshape(HEADS, 1, 1)
    dtb = w["dt_bias"].reshape(HEADS, 1, 1)
    gnw = w["gdn_norm"].reshape(1, 1, HD)

    a = _rms_fwd(h0, nmix, "rms_mix")
    pf = _mm(a, w["w_fox"], out_dtype=_BF, name="proj_fox")
    rest = _mm(a, w["w_rest"], name="proj_rest")
    small = rest[:, OFF_SMALL:]
    c8 = _fgate_fwd(small, bias128)[:, :HEADS]
    c_cols, c_rows = _att_cols(c8), _att_rows(c8, tb_att)
    ones_blk = jnp.asarray(np.kron(np.eye(HEADS, dtype=np.float32), np.ones((HD, HD), np.float32)))
    plan = _att_plan(_att_stats(pf, c_cols, ones_blk))
    o_fox, lse, lse2 = _fox_fwd(pf, c_cols, c_rows, plan)
    conv = _gconv_fwd(rest, w["gdn_conv"])
    chunk_rows = lambda a8: a8.reshape(T // CH, CH, HEADS).transpose(0, 2, 1)[:, :, None, :]
    bpre = chunk_rows(small[:, HEADS:2 * HEADS])
    apre = chunk_rows(small[:, 2 * HEADS:3 * HEADS])
    if late is None:
        o_raw, og, states = _gdn_fwd(conv, rest, bpre, apre, alog, dtb, gnw)
    else:
        o_raw, og, states, *got = _gdn_fwd(conv, rest, bpre, apre, alog, dtb, gnw, gather=list(late.values()))
        w = dict(w, **{LATE_KEYS[n]: _from_chips(g, AXIS[n]) for n, g in zip(late, got)})
    w_up, w_down = w["w_up"], w["w_down"]
    y_fox = _mm(o_fox, w["w_bfox"], name="y_fox")
    y_gdn = _mm(og, w["w_bgdn"], name="y_gdn")
    mix = _mix_fwd(rest, gate_b, y_fox, y_gdn)
    h1 = _mm(mix, w["w_out"], resid=h0, name="out_proj")
    b = _rms_fwd(h1, nffn, "rms_ffn")
    up = _mm(b, w_up, name="ffn_up")
    f = _glu_fwd(up, w["ffn_conv"], fconv_b)
    h2 = _mm(f, w_down, resid=h1, name="ffn_down")
    dh2, loss, d_nfin = _loss_head(h2, nfin, tgt, seq)

    d_f = _mm(dh2, w_down, tb=True, name="d_f")
    g_down = _mm_t(f, dh2, "g_down")
    dug, duu, dwg, dwu, dbg, dbu = _glu_bwd_du(up, w["ffn_conv"], fconv_b, d_f)
    dxg = _conv_bwd_dx(dug, w["ffn_conv"][:, :DFF], _BF, "fconv_dx_gate")
    dxu = _conv_bwd_dx(duu, w["ffn_conv"][:, DFF:], _BF, "fconv_dx_up")
    d_b = _mm(dxg, w_up[:, :DFF], tb=True, name="d_b_gate")
    d_b = _mm(dxu, w_up[:, DFF:], tb=True, resid=d_b, name="d_b_up")
    g_up = jnp.concatenate([_mm_t(b, dxg, "g_up_gate"), _mm_t(b, dxu, "g_up_up")], axis=1)
    dh1, d_nffn = _rms_bwd(h1, nffn, d_b, dh2, "rms_ffn_bwd")

    dmix = _mm(dh1, w["w_out"], tb=True, name="d_mix")
    g_out = _mm_t(mix, dh1, "g_out")
    dyf, dyg, dgf, dgg, dgbf, dgbg = _mix_bwd(rest, gate_b, y_fox, y_gdn, dmix)
    do_fox = _mm(dyf, w["w_bfox"], tb=True, out_dtype=_BF, name="d_o_fox")
    g_bfox = _mm_t(o_fox, dyf, "g_bfox")
    d_og = _mm(dyg, w["w_bgdn"], tb=True, name="d_o_gdn")
    g_bgdn = _mm_t(og, dyg, "g_bgdn")

    ones_col = jnp.asarray(np.kron(np.eye(HEADS, SMALL_W, dtype=np.float32), np.ones((HD, 1), np.float32)))
    delta, delta8 = _head_dot(do_fox, o_fox, ones_blk, ones_col)
    dq, dcq = _fox_bwd_dq(pf, do_fox, c_cols, c_rows, lse, delta, plan)
    dk, dv, dck = _fox_bwd_dkv(pf, do_fox, c_cols, _att_rows(c8 - _from_pairs(lse2), tb_att),
                               _att_rows(delta8[:, :HEADS], tb_att), plan)
    dc = jnp.pad(_from_pairs(dcq + dck), ((0, 0), (0, SMALL_W - HEADS)))
    dfp, d_fb = _fgate_bwd(small, bias128, dc)

    if late is None:
        late_recv = None
        dconv, dz, dbp, dap, d_alog, d_dtb, d_gnw = _gdn_bwd(conv, rest, bpre, apre, alog, dtb, gnw, states, o_raw, d_og)
    else:
        ready = dict(w_branch_fox=g_bfox, w_branch_gdn=g_bgdn, w_out=g_out, ffn_w_up=g_up, ffn_w_down=g_down)
        send = [_by_chip(ready[n].astype(jnp.bfloat16), AXIS[n]) for n in late]
        dconv, dz, dbp, dap, d_alog, d_dtb, d_gnw, *late_recv = _gdn_bwd(
            conv, rest, bpre, apre, alog, dtb, gnw, states, o_raw, d_og, scatter=send)
    du_g, g_gconv = _gconv_bwd_du(rest, w["gdn_conv"], dconv)
    dgx = _conv_bwd_dx(du_g, w["gdn_conv"], _BF, "gconv_dx")
    token_rows = lambda a: a[:, :, 0, :].transpose(0, 2, 1).reshape(T, HEADS)
    dsmall = jnp.concatenate([dfp[:, :HEADS], token_rows(dbp), token_rows(dap),
                              jnp.zeros((T, SMALL_W - 3 * HEADS), F32)], axis=1)
    drest = jnp.concatenate([dgx, dz.astype(_BF), dgf, dgg, dsmall.astype(_BF)], axis=1)
    dfox = jnp.concatenate([dq, dk, dv], axis=1)
    g_fox = _mm_t(a, dfox, "g_w_fox")
    g_rest = _mm_t(a, drest, "g_w_rest")
    sm = lambda lo: g_rest[:, OFF_SMALL + lo:OFF_SMALL + lo + HEADS]
    g_w_in = jnp.concatenate([g_fox, sm(0), g_rest[:, :3 * FW], g_rest[:, OFF_Z:OFF_Z + FW], sm(HEADS), sm(2 * HEADS),
                              g_rest[:, OFF_GATES:OFF_GATES + 2 * D]], axis=1)
    d_a = _mm(dfox, w["w_fox"], tb=True, name="d_a_fox")
    if late is None:
        d_a = _mm(drest, w["w_rest"], tb=True, resid=d_a, name="d_a_rest")
    else:
        d_a, w_in_recv = _mm(drest, w["w_rest"], tb=True, resid=d_a, name="d_a_rest",
                             scatter=[_by_chip(g_w_in.astype(jnp.bfloat16), AXIS["w_in"])])
        late_recv = [w_in_recv] + late_recv
    dh0, d_nmix = _rms_bwd(h0, nmix, d_a, dh1, "rms_mix_bwd")

    grads = dict(
        meta_tokens=dh0[FRONT:X0], w_in=g_w_in, fgt_bias=d_fb[0, :HEADS], gdn_conv_w=g_gconv,
        gdn_a_log=d_alog.reshape(HEADS), gdn_dt_bias=d_dtb.reshape(HEADS), gdn_norm_w=d_gnw.reshape(HD),
        gate_bias=jnp.concatenate([dgbf, dgbg], axis=1).reshape(2 * D), w_branch_fox=g_bfox, w_branch_gdn=g_bgdn,
        w_out=g_out, norm_mix_w=d_nmix.reshape(D), norm_ffn_w=d_nffn.reshape(D), ffn_w_up=g_up,
        ffn_conv_w=jnp.concatenate([dwg, dwu], axis=1), ffn_conv_b=jnp.concatenate([dbg, dbu], axis=1).reshape(2 * DFF),
        ffn_w_down=g_down, norm_final_w=d_nfin.reshape(D), late_recv=late_recv)
    return loss, dh0[X0:X0 + seq], grads


N_CHIPS = 4
PACK_W = 1024
PACK_ROW_ALIGN = 32
BIG_EARLY = ("w_in",)
BIG_LATE = ("w_branch_fox", "w_branch_gdn", "w_out", "ffn_w_up", "ffn_w_down")
BIG = BIG_EARLY + BIG_LATE
WEIGHTS = (
    ("meta_tokens", (N_META, D), 1), ("w_in", (D, 3 * FW + HEADS + 4 * FW + 2 * HEADS + 2 * D), 1), ("fgt_bias", (1, HEADS), None),
    ("gdn_conv_w", (4, 3 * FW), 1), ("gdn_a_log", (1, HEADS), None), ("gdn_dt_bias", (1, HEADS), None),
    ("gdn_norm_w", (1, HD), None), ("gate_bias", (1, 2 * D), None), ("w_branch_fox", (FW, D), 1),
    ("w_branch_gdn", (FW, D), 1), ("w_out", (D, D), 0), ("norm_mix_w", (1, D), None), ("norm_ffn_w", (1, D), None),
    ("ffn_w_up", (D, 2 * DFF), 1), ("ffn_conv_w", (3, 2 * DFF), 1), ("ffn_conv_b", (1, 2 * DFF), None),
    ("ffn_w_down", (DFF, D), 0), ("norm_final_w", (1, D), None))
SPLIT_F32 = ("meta_tokens", "gdn_conv_w", "ffn_conv_w")


def _shard_shape(shape, axis):
    if axis is None:
        return shape
    return tuple(s // N_CHIPS if a == axis else s for a, s in enumerate(shape))


def _shard_of(full, axis, q):
    if axis is None:
        return full
    n = full.shape[axis] // N_CHIPS
    return lax.slice_in_dim(full, q * n, (q + 1) * n, axis=axis)


def _pack_rows(n_elems):
    rows = -(-n_elems // PACK_W)
    return -(-rows // PACK_ROW_ALIGN) * PACK_ROW_ALIGN


def _pack(pieces, dtype):
    flat = jnp.concatenate([p.reshape(-1).astype(dtype) for p in pieces])
    rows = _pack_rows(flat.shape[0])
    return jnp.pad(flat, (0, rows * PACK_W - flat.shape[0])).reshape(rows, PACK_W)


def _unpack(slab, shapes):
    flat = slab.reshape(-1)
    out, off = [], 0
    for s in shapes:
        n = int(np.prod(s))
        out.append(flat[off:off + n].reshape(s))
        off += n
    return out


HBM_SPEC = pl.BlockSpec(memory_space=pltpu.HBM)
MESH_ID = pl.DeviceIdType.MESH


def _scatter_chips(srcs, name):
    n = len(srcs)

    def body(*refs):
        _scatter_phase(refs[:n], refs[n:2 * n], refs[2 * n:], True)
        _scatter_phase(refs[:n], refs[n:2 * n], refs[2 * n:], False)

    return pl.pallas_call(body, name=name, in_specs=[HBM_SPEC] * n, out_specs=[HBM_SPEC] * n,
                          out_shape=_scatter_shapes(srcs), scratch_shapes=_scatter_sems(n))(*srcs)


def _scatter_shapes(srcs):
    return [jax.ShapeDtypeStruct(s.shape, s.dtype) for s in srcs]


def _scatter_sems(n):
    return [pltpu.SemaphoreType.DMA((3 * n,)), pltpu.SemaphoreType.DMA((3 * n,)), pltpu.SemaphoreType.DMA((n,))]


def _scatter_phase(src_refs, out_refs, sems, issue):
    n = len(src_refs)
    send_sems, recv_sems, local_sems = sems
    x, y, c = lax.axis_index("x"), lax.axis_index("y"), lax.axis_index("c")
    q = 2 * x + y
    peers = [(1 - x, y), (x, 1 - y), (1 - x, 1 - y)]

    def remote(a, k, src_slot, dst_slot):
        px, py = peers[k]
        return pltpu.make_async_remote_copy(
            src_ref=src_refs[a].at[src_slot], dst_ref=out_refs[a].at[dst_slot], send_sem=send_sems.at[3 * a + k],
            recv_sem=recv_sems.at[3 * a + k], device_id=(px, py, c), device_id_type=MESH_ID)

    mine = [pltpu.make_async_copy(src_refs[a].at[q], out_refs[a].at[q], local_sems.at[a]) for a in range(n)]
    sends = [remote(a, k, 2 * px + py, q) for a in range(n) for k, (px, py) in enumerate(peers)]
    if issue:
        for cp in mine + sends:
            cp.start()
        return
    for a in range(n):
        for k, (px, py) in enumerate(peers):
            remote(a, k, 0, 2 * px + py).wait_recv()
    for cp in sends:
        cp.wait_send()
    for cp in mine:
        cp.wait()


def _gather_chips(srcs, name):
    n = len(srcs)

    def body(*refs):
        _gather_phase(refs[:n], refs[n:2 * n], refs[2 * n:], True)
        _gather_phase(refs[:n], refs[n:2 * n], refs[2 * n:], False)

    return pl.pallas_call(body, name=name, in_specs=[HBM_SPEC] * n, out_specs=[HBM_SPEC] * n,
                          out_shape=_gather_shapes(srcs), scratch_shapes=_gather_sems(n))(*srcs)


def _gather_shapes(srcs):
    return [jax.ShapeDtypeStruct((N_CHIPS,) + s.shape, s.dtype) for s in srcs]


def _gather_sems(n):
    return [pltpu.SemaphoreType.DMA((3 * n,))] * 4 + [pltpu.SemaphoreType.DMA((n,))]


def _gather_phase(src_refs, out_refs, sems, issue):
    n = len(src_refs)
    ici_send, ici_recv, d2d_send, d2d_recv, local_sems = sems
    x, y, c = lax.axis_index("x"), lax.axis_index("y"), lax.axis_index("c")
    q = 2 * x + y
    peers = [(1 - x, y), (x, 1 - y), (1 - x, 1 - y)]

    def half(a, which):
        r = src_refs[a].shape[0] // 2
        return pl.ds(which * r, r)

    def ici(a, k, slot):
        px, py = peers[k]
        return pltpu.make_async_remote_copy(
            src_ref=src_refs[a].at[half(a, c)], dst_ref=out_refs[a].at[slot, half(a, c)], send_sem=ici_send.at[3 * a + k],
            recv_sem=ici_recv.at[3 * a + k], device_id=(px, py, c), device_id_type=MESH_ID)

    def d2d(a, k, which):
        px, py = peers[k]
        rows = out_refs[a].at[2 * px + py, half(a, which)]
        return pltpu.make_async_remote_copy(
            src_ref=rows, dst_ref=rows, send_sem=d2d_send.at[3 * a + k], recv_sem=d2d_recv.at[3 * a + k],
            device_id=(x, y, 1 - c), device_id_type=MESH_ID)

    mine = [pltpu.make_async_copy(src_refs[a], out_refs[a].at[q], local_sems.at[a]) for a in range(n)]
    sends = [ici(a, k, q) for a in range(n) for k in range(3)]
    if issue:
        for cp in mine + sends:
            cp.start()
        return
    passed = []
    for a in range(n):
        for k, (px, py) in enumerate(peers):
            ici(a, k, 2 * px + py).wait_recv()
            passed.append(d2d(a, k, c))
            passed[-1].start()
    for a in range(n):
        for k in range(3):
            d2d(a, k, 1 - c).wait_recv()
    for cp in sends + passed:
        cp.wait_send()
    for cp in mine:
        cp.wait()


def _sibling_swap(slabs, name):
    n = len(slabs)

    def body(*refs):
        src_refs, out_refs, send_sems, recv_sems = refs[:n], refs[n:2 * n], refs[2 * n], refs[2 * n + 1]
        x, y, c = lax.axis_index("x"), lax.axis_index("y"), lax.axis_index("c")
        cps = [pltpu.make_async_remote_copy(src_ref=src_refs[a], dst_ref=out_refs[a], send_sem=send_sems.at[a],
                                            recv_sem=recv_sems.at[a], device_id=(x, y, 1 - c), device_id_type=MESH_ID)
               for a in range(n)]
        for cp in cps:
            cp.start()
        for cp in cps:
            cp.wait_recv()
        for cp in cps:
            cp.wait_send()

    return pl.pallas_call(
        body, name=name, in_specs=[HBM_SPEC] * n, out_specs=[HBM_SPEC] * n,
        out_shape=[jax.ShapeDtypeStruct(s.shape, s.dtype) for s in slabs],
        scratch_shapes=[pltpu.SemaphoreType.DMA((n,)), pltpu.SemaphoreType.DMA((n,))])(*slabs)


def _sum_chips(r, name):
    rows, cols = r.shape[1:]
    tb = _div(rows, 256, 16)

    def body(r0, r1, r2, r3, o_ref):
        o_ref[...] = ((r0[0].astype(F32) + r1[0].astype(F32)) + r2[0].astype(F32)) + r3[0].astype(F32)

    spec = lambda j: pl.BlockSpec((1, tb, cols), lambda i: (j, i, 0))
    return pl.pallas_call(
        body, name=name, grid=(rows // tb,), in_specs=[spec(0), spec(1), spec(2), spec(3)],
        out_specs=pl.BlockSpec((tb, cols), lambda i: (i, 0)), out_shape=jax.ShapeDtypeStruct((rows, cols), F32),
        compiler_params=_cp("parallel"))(r, r, r, r)


def _adamw(w, m, v, p, q, name):
    rows, cols = w.shape
    tb = _div(rows, 256, 8)

    def body(w_ref, m_ref, v_ref, p_ref, q_ref, g_ref, d_ref, nm_ref, nv_ref):
        g = p_ref[...] + q_ref[...]
        m_new = B1 * m_ref[...] + (1.0 - B1) * g
        v_new = B2 * v_ref[...] + (1.0 - B2) * (g * g)
        g_ref[...] = g
        nm_ref[...] = m_new
        nv_ref[...] = v_new
        m_hat = m_new / (1.0 - B1 ** STEP)
        v_hat = v_new / (1.0 - B2 ** STEP)
        d_ref[...] = -LR * (m_hat / (jnp.sqrt(v_hat) + AEPS) + WD * w_ref[...])

    spec = pl.BlockSpec((tb, cols), lambda i: (i, 0))
    return pl.pallas_call(
        body, name=name, grid=(rows // tb,), in_specs=[spec] * 5, out_specs=[spec] * 4,
        out_shape=[jax.ShapeDtypeStruct((rows, cols), F32)] * 4, compiler_params=_cp("parallel"))(w, m, v, p, q)


def _split_w_in(w_in):
    o1 = 3 * FW
    o2 = o1 + HEADS
    o3 = o2 + 3 * FW
    o4 = o3 + FW
    o5 = o4 + HEADS
    o6 = o5 + HEADS
    pad = jnp.zeros((w_in.shape[0], SMALL_W - 3 * HEADS), w_in.dtype)
    rest = jnp.concatenate([w_in[:, o2:o3], w_in[:, o3:o4], w_in[:, o6:], w_in[:, o1:o2], w_in[:, o4:o5], w_in[:, o5:o6], pad],
                           axis=1)
    return w_in[:, :o1], rest


AXIS = {n: a for n, _, a in WEIGHTS}
SMALL = tuple(n for n, _, _ in WEIGHTS if n not in BIG)


def _by_chip(full, axis):
    rows, cols = full.shape
    if axis == 0:
        return full.reshape(N_CHIPS, rows // N_CHIPS, cols)
    return full.reshape(rows, N_CHIPS, cols // N_CHIPS).transpose(1, 0, 2)


def _from_chips(parts, axis):
    _, r, c = parts.shape
    if axis == 0:
        return parts.reshape(N_CHIPS * r, c)
    return parts.transpose(1, 0, 2).reshape(r, N_CHIPS * c)


def _gather_weights(shards):
    hi = {n: shards[n].astype(jnp.bfloat16) for n in SPLIT_F32}
    lo = [(shards[n] - hi[n].astype(F32)).astype(jnp.bfloat16) for n in SPLIT_F32]
    slab = _pack([hi[n] for n in SPLIT_F32] + lo, jnp.bfloat16)
    got = _gather_chips([shards[n].astype(jnp.bfloat16) for n in BIG_EARLY] + [slab], "gather_weights")
    full = {n: _from_chips(g, AXIS[n]) for n, g in zip(BIG_EARLY, got)}
    shapes = [shards[n].shape for n in SPLIT_F32] * 2
    per_chip = [_unpack(got[-1][j], shapes) for j in range(N_CHIPS)]
    for i, n in enumerate(SPLIT_F32):
        join = lambda off: jnp.concatenate([per_chip[j][off + i] for j in range(N_CHIPS)], axis=1).astype(F32)
        full[n] = join(0) + join(len(SPLIT_F32))
    return full


def kernel(x, meta_tokens, w_in, fgt_bias, gdn_conv_w, gdn_a_log, gdn_dt_bias, gdn_norm_w, gate_bias, w_branch_fox, w_branch_gdn, w_out, norm_mix_w, norm_ffn_w, ffn_w_up, ffn_conv_w, ffn_conv_b, ffn_w_down, norm_final_w, loss_target, m_meta_tokens, m_w_in, m_fgt_bias, m_gdn_conv_w, m_gdn_a_log, m_gdn_dt_bias, m_gdn_norm_w, m_gate_bias, m_w_branch_fox, m_w_branch_gdn, m_w_out, m_norm_mix_w, m_norm_ffn_w, m_ffn_w_up, m_ffn_conv_w, m_ffn_conv_b, m_ffn_w_down, m_norm_final_w, v_meta_tokens, v_w_in, v_fgt_bias, v_gdn_conv_w, v_gdn_a_log, v_gdn_dt_bias, v_gdn_norm_w, v_gate_bias, v_w_branch_fox, v_w_branch_gdn, v_w_out, v_norm_mix_w, v_norm_ffn_w, v_ffn_w_up, v_ffn_conv_w, v_ffn_conv_b, v_ffn_w_down, v_norm_final_w):
    weights = dict(meta_tokens=meta_tokens, w_in=w_in, fgt_bias=fgt_bias, gdn_conv_w=gdn_conv_w, gdn_a_log=gdn_a_log, gdn_dt_bias=gdn_dt_bias, gdn_norm_w=gdn_norm_w, gate_bias=gate_bias, w_branch_fox=w_branch_fox, w_branch_gdn=w_branch_gdn, w_out=w_out, norm_mix_w=norm_mix_w, norm_ffn_w=norm_ffn_w, ffn_w_up=ffn_w_up, ffn_conv_w=ffn_conv_w, ffn_conv_b=ffn_conv_b, ffn_w_down=ffn_w_down, norm_final_w=norm_final_w)
    m_in = dict(meta_tokens=m_meta_tokens, w_in=m_w_in, fgt_bias=m_fgt_bias, gdn_conv_w=m_gdn_conv_w, gdn_a_log=m_gdn_a_log, gdn_dt_bias=m_gdn_dt_bias, gdn_norm_w=m_gdn_norm_w, gate_bias=m_gate_bias, w_branch_fox=m_w_branch_fox, w_branch_gdn=m_w_branch_gdn, w_out=m_w_out, norm_mix_w=m_norm_mix_w, norm_ffn_w=m_norm_ffn_w, ffn_w_up=m_ffn_w_up, ffn_conv_w=m_ffn_conv_w, ffn_conv_b=m_ffn_conv_b, ffn_w_down=m_ffn_w_down, norm_final_w=m_norm_final_w)
    v_in = dict(meta_tokens=v_meta_tokens, w_in=v_w_in, fgt_bias=v_fgt_bias, gdn_conv_w=v_gdn_conv_w, gdn_a_log=v_gdn_a_log, gdn_dt_bias=v_gdn_dt_bias, gdn_norm_w=v_gdn_norm_w, gate_bias=v_gate_bias, w_branch_fox=v_w_branch_fox, w_branch_gdn=v_w_branch_gdn, w_out=v_w_out, norm_mix_w=v_norm_mix_w, norm_ffn_w=v_norm_ffn_w, ffn_w_up=v_ffn_w_up, ffn_conv_w=v_ffn_conv_w, ffn_conv_b=v_ffn_conv_b, ffn_w_down=v_ffn_w_down, norm_final_w=v_norm_final_w)
    shard2d = {n: _shard_shape(s, a) for n, s, a in WEIGHTS}
    as2d = lambda d: {n: d[n].reshape(shard2d[n]) for n, _, _ in WEIGHTS}
    w2, m2, v2 = as2d(weights), as2d(m_in), as2d(v_in)

    full = _gather_weights(w2)
    w_fox, w_rest = _split_w_in(full["w_in"])
    flat = lambda n: w2[n].reshape(-1)
    local_w = dict(
        meta=full["meta_tokens"], w_fox=w_fox, w_rest=w_rest, fgt_bias=flat("fgt_bias"), gdn_conv=full["gdn_conv_w"],
        a_log=flat("gdn_a_log"), dt_bias=flat("gdn_dt_bias"), gdn_norm=flat("gdn_norm_w"), gate_bias=flat("gate_bias"),
        norm_mix=flat("norm_mix_w"), norm_ffn=flat("norm_ffn_w"), ffn_conv=full["ffn_conv_w"], ffn_conv_b=flat("ffn_conv_b"),
        norm_final=flat("norm_final_w"))

    loss, grad_x, grads = _local_step(x[0], loss_target[0], local_w, {n: w2[n].astype(jnp.bfloat16) for n in BIG_LATE})

    g2 = {n: grads[n].reshape(s) for n, s, _ in WEIGHTS}
    small_send = jnp.stack([_pack([_shard_of(g2[n], AXIS[n], j) for n in SMALL], F32) for j in range(N_CHIPS)])
    recv = list(grads["late_recv"]) + list(_scatter_chips([small_send], "scatter_grads"))
    parts = [_sum_chips(r, "sum_" + n) for r, n in zip(recv, BIG + ("small",))]
    others = _sibling_swap(parts, "swap_grads")
    slab = lambda d: _pack([d[n] for n in SMALL], F32)
    state = [(w2[n], m2[n], v2[n]) for n in BIG] + [(slab(w2), slab(m2), slab(v2))]
    outs = [_adamw(w, m, v, p, q, "adamw_" + n) for (w, m, v), p, q, n in zip(state, parts, others, BIG + ("small",))]
    small = [_unpack(o, [weights[n].shape for n in SMALL]) for o in outs[-1]]
    result = []
    for kind in range(4):
        by_name = {n: outs[i][kind].reshape(weights[n].shape) for i, n in enumerate(BIG)}
        by_name.update(zip(SMALL, small[kind]))
        result += [by_name[n] for n, _, _ in WEIGHTS]
    total = lax.psum(loss[0, 0], ("x", "y", "c"))
    return (total, grad_x[None], *result)
```

```python
import functools

import numpy as np
import jax
import jax.numpy as jnp
from jax import lax
from jax.experimental import pallas as pl
from jax.experimental.pallas import tpu as pltpu

F32 = jnp.float32
_BF = jnp.bfloat16
HI = lax.Precision.HIGHEST

D = 1024
N_META = 16
CH = 64
FRONT = CH - N_META
X0 = CH
HEADS = 8
HD = 64
FW = HEADS * HD
DFF = 2816
EPS = 1e-6
NEG = -1e30
T_ALIGN = 256
SMALL_W = 128
REST_W = 3 * FW + FW + 2 * D + SMALL_W
OFF_Z = 3 * FW
OFF_GATES = 4 * FW
OFF_SMALL = 4 * FW + 2 * D

LR, B1, B2, AEPS, WD, STEP = 0.001, 0.9, 0.999, 1e-08, 0.01, 10

VMEM_LIMIT = 56 * 1024 * 1024
ROW_TILE = 640
MM_TM, MM_TN, MM_TK = 1280, 512, 2816
ATT_TILE = 640


def _div(n, target, mult=128):
    if n <= target:
        return n
    best = None
    for d in range(mult, target + 1, mult):
        if n % d == 0:
            best = d
    assert best is not None, (n, target, mult)
    return best


def _cp(*sem):
    return pltpu.CompilerParams(dimension_semantics=sem, vmem_limit_bytes=VMEM_LIMIT)


def _sig(x):
    return 0.5 * jnp.tanh(0.5 * x) + 0.5


MM_VMEM_BUDGET = 40 * 1024 * 1024
MM_STEP_BYTES = 1 << 20


def _mm_tiles(m, n, k, sa, sb, so, has_resid):
    divs = lambda d, cap: [t for t in range(128, min(d, cap) + 1, 128) if d % t == 0] or [d]
    best = None
    for tm in divs(m, MM_TM * 2):
        for tn in divs(n, 4608):
            fixed = tm * tn * (4 + 2 * so + (8 if has_resid else 0))
            tks = [t for t in divs(k, MM_TK) if 2 * t * (tm * sa + tn * sb) + fixed <= MM_VMEM_BUDGET]
            if not tks:
                continue
            tk = tks[-1]
            steps = (m // tm) * (n // tn) * (k // tk)
            rmw = (k // tk - 1) * m * n * 4
            cost = (n // tn) * m * k * sa + (m // tm) * k * n * sb + steps * MM_STEP_BYTES + rmw
            if best is None or cost < best[0]:
                best = (cost, tm, tn, tk)
    assert best is not None, (m, n, k)
    return best[1:]


def _mm_t(a, b, name):
    return _mm(a, b, ta=True, name=name)


def _mm(a, b, *, ta=False, tb=False, out_dtype=F32, resid=None, name, scatter=()):
    K, M = a.shape if ta else a.shape[::-1]
    N = b.shape[0] if tb else b.shape[1]
    assert (b.shape[1] if tb else b.shape[0]) == K
    tm, tn, tk = _mm_tiles(M, N, K, a.dtype.itemsize, b.dtype.itemsize, jnp.dtype(out_dtype).itemsize, resid is not None)
    nk = K // tk
    grid = (M // tm, N // tn, nk)
    dims = (((0 if ta else 1,), (1 if tb else 0,)), ((), ()))
    mxu = _BF
    nr = 1 if resid is not None else 0
    ng = len(scatter)

    def body(*refs):
        a_ref, b_ref = refs[:2]
        r_ref = refs[2] if nr else None
        o_ref = refs[2 + nr + ng]
        acc = refs[3 + nr + 2 * ng]
        comm = (refs[2 + nr:2 + nr + ng], refs[3 + nr + ng:3 + nr + 2 * ng], refs[4 + nr + 2 * ng:])
        k = pl.program_id(2)
        if ng:
            step = (pl.program_id(0) * grid[1] + pl.program_id(1)) * grid[2] + k
            pl.when(step == 0)(functools.partial(_scatter_phase, *comm, True))
        part = lax.dot_general(a_ref[...].astype(mxu), b_ref[...].astype(mxu), dims, preferred_element_type=F32)

        @pl.when(k == 0)
        def _():
            acc[...] = part

        @pl.when(k > 0)
        def _():
            acc[...] += part

        @pl.when(k == nk - 1)
        def _():
            r = acc[...]
            if r_ref is not None:
                r = r + r_ref[...]
            o_ref[...] = r.astype(o_ref.dtype)

        if ng:
            pl.when(step == grid[0] * grid[1] * grid[2] - 1)(functools.partial(_scatter_phase, *comm, False))

    a_spec = pl.BlockSpec((tk, tm), lambda i, j, k: (k, i)) if ta else pl.BlockSpec((tm, tk), lambda i, j, k: (i, k))
    b_spec = pl.BlockSpec((tn, tk), lambda i, j, k: (j, k)) if tb else pl.BlockSpec((tk, tn), lambda i, j, k: (k, j))
    o_spec = pl.BlockSpec((tm, tn), lambda i, j, k: (i, j))
    in_specs = [a_spec, b_spec] + ([o_spec] if nr else []) + [HBM_SPEC] * ng
    args = (a, b) + ((resid,) if nr else ()) + tuple(scatter)
    out = jax.ShapeDtypeStruct((M, N), out_dtype)
    sem = ("arbitrary",) * 3 if ng else ("parallel", "parallel", "arbitrary")
    return pl.pallas_call(
        body, name=name, grid=grid, in_specs=in_specs, out_specs=[o_spec] + [HBM_SPEC] * ng if ng else o_spec,
        out_shape=[out] + _scatter_shapes(scatter) if ng else out,
        scratch_shapes=[pltpu.VMEM((tm, tn), F32)] + (_scatter_sems(ng) if ng else []),
        compiler_params=_cp(*sem))(*args)


def _rms_fwd(h, w, name):
    T = h.shape[0]
    tb = _div(T, ROW_TILE, 8)

    def body(h_ref, w_ref, o_ref):
        x = h_ref[...]
        r = lax.rsqrt(jnp.mean(x * x, axis=-1, keepdims=True) + EPS)
        o_ref[...] = (x * r * w_ref[...]).astype(o_ref.dtype)

    return pl.pallas_call(
        body, name=name, grid=(T // tb,),
        in_specs=[pl.BlockSpec((tb, D), lambda i: (i, 0)), pl.BlockSpec((1, D), lambda i: (0, 0))],
        out_specs=pl.BlockSpec((tb, D), lambda i: (i, 0)), out_shape=jax.ShapeDtypeStruct((T, D), _BF),
        compiler_params=_cp("parallel"))(h, w)


def _rms_bwd(h, w, dy, dres, name):
    T = h.shape[0]
    tb = _div(T, ROW_TILE, 8)

    def body(h_ref, w_ref, dy_ref, dr_ref, dh_ref, dw_ref):
        x = h_ref[...]
        r = lax.rsqrt(jnp.mean(x * x, axis=-1, keepdims=True) + EPS)
        xh = x * r
        dy = dy_ref[...]
        g = dy * w_ref[...]
        dh_ref[...] = dr_ref[...] + r * (g - xh * jnp.mean(xh * g, axis=-1, keepdims=True))
        part = jnp.sum(dy * xh, axis=0, keepdims=True)

        @pl.when(pl.program_id(0) == 0)
        def _():
            dw_ref[...] = part

        @pl.when(pl.program_id(0) > 0)
        def _():
            dw_ref[...] += part

    row = pl.BlockSpec((tb, D), lambda i: (i, 0))
    one = pl.BlockSpec((1, D), lambda i: (0, 0))
    return pl.pallas_call(
        body, name=name, grid=(T // tb,), in_specs=[row, one, row, row], out_specs=[row, one],
        out_shape=[jax.ShapeDtypeStruct((T, D), F32), jax.ShapeDtypeStruct((1, D), F32)],
        compiler_params=_cp("arbitrary"))(h, w, dy, dres)


def _fgate_fwd(small, bias):
    T = small.shape[0]
    tb = _div(T, ROW_TILE, 8)

    def body(s_ref, b_ref, c_ref, carry):
        @pl.when(pl.program_id(0) == 0)
        def _():
            carry[...] = jnp.zeros_like(carry)

        lf = jax.nn.log_sigmoid(s_ref[...] + b_ref[...])
        r = lax.broadcasted_iota(jnp.int32, (tb, tb), 0)
        c = lax.broadcasted_iota(jnp.int32, (tb, tb), 1)
        tri = (r >= c).astype(F32)
        cs = jnp.dot(tri, lf, precision=HI, preferred_element_type=F32) + carry[...]
        c_ref[...] = cs
        carry[...] = cs[tb - 1:tb, :]

    return pl.pallas_call(
        body, name="fgate_fwd", grid=(T // tb,),
        in_specs=[pl.BlockSpec((tb, SMALL_W), lambda i: (i, 0)), pl.BlockSpec((1, SMALL_W), lambda i: (0, 0))],
        out_specs=pl.BlockSpec((tb, SMALL_W), lambda i: (i, 0)), out_shape=jax.ShapeDtypeStruct((T, SMALL_W), F32),
        scratch_shapes=[pltpu.VMEM((1, SMALL_W), F32)], compiler_params=_cp("arbitrary"))(small, bias)


def _fgate_bwd(small, bias, dc):
    T = small.shape[0]
    tb = _div(T, ROW_TILE, 8)
    nb = T // tb

    def body(s_ref, b_ref, dc_ref, df_ref, db_ref, carry):
        @pl.when(pl.program_id(0) == 0)
        def _():
            carry[...] = jnp.zeros_like(carry)

        r = lax.broadcasted_iota(jnp.int32, (tb, tb), 0)
        c = lax.broadcasted_iota(jnp.int32, (tb, tb), 1)
        tri = (r <= c).astype(F32)
        dlf = jnp.dot(tri, dc_ref[...], precision=HI, preferred_element_type=F32) + carry[...]
        carry[...] = dlf[0:1, :]
        df = dlf * _sig(-(s_ref[...] + b_ref[...]))
        df_ref[...] = df
        part = jnp.sum(df, axis=0, keepdims=True)

        @pl.when(pl.program_id(0) == 0)
        def _():
            db_ref[...] = part

        @pl.when(pl.program_id(0) > 0)
        def _():
            db_ref[...] += part

    rev = pl.BlockSpec((tb, SMALL_W), lambda i: (nb - 1 - i, 0))
    one = pl.BlockSpec((1, SMALL_W), lambda i: (0, 0))
    return pl.pallas_call(
        body, name="fgate_bwd", grid=(nb,), in_specs=[rev, one, rev], out_specs=[rev, one],
        out_shape=[jax.ShapeDtypeStruct((T, SMALL_W), F32), jax.ShapeDtypeStruct((1, SMALL_W), F32)],
        scratch_shapes=[pltpu.VMEM((1, SMALL_W), F32)], compiler_params=_cp("arbitrary"))(small, bias, dc)


def _att_rows(a8, tb):
    T = a8.shape[0]
    return a8.T.reshape(HEADS // 2, 2, T // tb, tb).transpose(0, 2, 1, 3)


def _att_cols(a8):
    return jnp.repeat(a8, HD, axis=1)


def _pair_cols(tb):
    return pl.BlockSpec((1, tb, 2), lambda p, i, *_: (p, i, 0))


def _from_pairs(a):
    return a.transpose(1, 0, 2).reshape(a.shape[1], HEADS)


EXP_ZERO = -104.0
SKIP_SLACK = 2.0
NORM_SLACK = 1.02


def _att_stats(qkv, c_cols, ones_blk):
    T = qkv.shape[0]
    tb = _div(T, ATT_TILE)

    def body(q_ref, k_ref, c_ref, e_ref, o_ref):
        q = q_ref[...].astype(F32)
        k = k_ref[...].astype(F32)
        e = e_ref[...].astype(_BF)
        qn = jnp.max(jnp.dot((q * q).astype(_BF), e, preferred_element_type=F32), axis=0, keepdims=True)
        kn = jnp.max(jnp.dot((k * k).astype(_BF), e, preferred_element_type=F32), axis=0, keepdims=True)
        c = c_ref[...]
        o_ref[0] = jnp.concatenate([jnp.sqrt(qn), jnp.sqrt(kn), jnp.max(c, axis=0, keepdims=True),
                                    jnp.min(c, axis=0, keepdims=True), jnp.zeros((4, FW), F32)], axis=0)

    blk = lambda off: pl.BlockSpec((tb, FW), lambda i: (i, off))
    return pl.pallas_call(
        body, name="att_stats", grid=(T // tb,),
        in_specs=[blk(0), blk(1), blk(0), pl.BlockSpec((FW, FW), lambda i: (0, 0))],
        out_specs=pl.BlockSpec((1, 8, FW), lambda i: (i, 0, 0)), out_shape=jax.ShapeDtypeStruct((T // tb, 8, FW), F32),
        compiler_params=_cp("parallel"))(qkv, qkv, c_cols, ones_blk)


def _att_plan(stats):
    nb = stats.shape[0]
    st = stats[:, :4, ::HD]
    qmax, kmax, cmax, cmin = (st[:, r, :].T for r in range(4))
    bound = (HD ** -0.5) * NORM_SLACK * qmax[:, :, None] * (kmax[:, None, :] + kmax[:, :, None]) \
        + cmax[:, :, None] - cmin[:, None, :] + SKIP_SLACK
    ii = lax.broadcasted_iota(jnp.int32, (nb, nb), 0)
    jj = lax.broadcasted_iota(jnp.int32, (nb, nb), 1)
    skip = (bound < EXP_ZERO) & (jj < ii)[None]
    live = (~skip & (jj <= ii)[None]).reshape(HEADS // 2, 2, nb, nb).any(axis=1)
    jfirst = jnp.argmax(live, axis=2).astype(jnp.int32)
    ilast = (nb - 1 - jnp.argmax(live[:, ::-1, :], axis=1)).astype(jnp.int32)
    return skip.astype(jnp.int32).reshape(-1), jfirst.reshape(-1), ilast.reshape(-1)


def _fox_fwd(qkv, c_cols, c_rows, plan):
    T = qkv.shape[0]
    tb = _div(T, ATT_TILE)
    nb = T // tb
    npair = HEADS // 2
    scale = HD ** -0.5
    dn = (((1,), (1,)), ((), ()))

    def body(skip_ref, jfirst_ref, ilast_ref, q_ref, k_ref, v_ref, cq_ref, ck_ref, o_ref, l_ref, l2_ref, m_scr, acc_scr):
        pr = pl.program_id(0)
        i = pl.program_id(1)
        q = q_ref[...]
        lane = lax.broadcasted_iota(jnp.int32, (1, 2 * HD), 1)
        sel0 = lane < HD
        zero = jnp.zeros_like(q)
        qh = (jnp.where(sel0, q, zero) * scale, jnp.where(sel0, zero, q) * scale)
        cq = (cq_ref[:, 0:1], cq_ref[:, HD:HD + 1])
        row = i * tb + lax.broadcasted_iota(jnp.int32, (tb, 1), 0)
        m_scr[...] = jnp.full(m_scr.shape, NEG, F32)
        acc_scr[...] = jnp.zeros(acc_scr.shape, F32)

        def tile(j, h, masked):
            start = pl.multiple_of(j * tb, tb)
            kb = k_ref[pl.ds(start, tb), :]
            vb = v_ref[pl.ds(start, tb), :]
            one = jnp.ones_like(vb)
            vh = jnp.where(sel0, vb, one) if h == 0 else jnp.where(sel0, one, vb)
            t = lax.dot_general(qh[h], kb, dn, preferred_element_type=F32) - ck_ref[0, j, h:h + 1, :]
            if masked:
                col = j * tb + lax.broadcasted_iota(jnp.int32, (1, tb), 1)
                mask = (col <= row) & (col >= FRONT)
                t = jnp.where(mask, t, NEG)
            m = m_scr[h]
            m_new = jnp.maximum(m, jnp.max(t, axis=-1, keepdims=True) + cq[h])
            p = jnp.exp(t + (cq[h] - m_new))
            if masked:
                p = jnp.where(mask, p, 0.0)
            m_scr[h] = m_new
            acc_scr[h] = acc_scr[h] * jnp.exp(m - m_new) + jnp.dot(p.astype(vb.dtype), vh, preferred_element_type=F32)

        def step(j, _):
            edge = (j == 0) | (j == i)
            for h in range(2):
                live = skip_ref[((2 * pr + h) * nb + i) * nb + j] == 0
                pl.when(live & edge)(functools.partial(tile, j, h, True))
                pl.when(live & jnp.logical_not(edge))(functools.partial(tile, j, h, False))
            return 0

        lax.fori_loop(jfirst_ref[pr * nb + i], i + 1, step, 0)
        outs, lses = [], []
        for h in range(2):
            l = acc_scr[h][:, (1 - h) * HD:(1 - h) * HD + 1]
            ok = l > 0.0
            ls = jnp.where(ok, l, 1.0)
            outs.append(jnp.where(ok, acc_scr[h] / ls, 0.0))
            lses.append(jnp.where(ok, m_scr[h] + jnp.log(ls), 0.0))
        o_ref[...] = jnp.where(sel0, outs[0], outs[1])
        l_ref[...] = jnp.where(sel0, lses[0], lses[1])
        l2_ref[0, :, 0:1] = lses[0]
        l2_ref[0, :, 1:2] = lses[1]

    blk = lambda off: pl.BlockSpec((tb, 2 * HD), lambda p, i, *_: (i, off + p))
    full = lambda off: pl.BlockSpec((T, 2 * HD), lambda p, i, *_: (0, off + p))
    rows = pl.BlockSpec((1, nb, 2, tb), lambda p, i, *_: (p, 0, 0, 0))
    return pl.pallas_call(
        body, name="fox_fwd",
        grid_spec=pltpu.PrefetchScalarGridSpec(
            num_scalar_prefetch=3, grid=(npair, nb),
            in_specs=[blk(0), full(npair), full(2 * npair), blk(0), rows], out_specs=[blk(0), blk(0), _pair_cols(tb)],
            scratch_shapes=[pltpu.VMEM((2, tb, 1), F32), pltpu.VMEM((2, tb, 2 * HD), F32)]),
        out_shape=[jax.ShapeDtypeStruct((T, FW), F32), jax.ShapeDtypeStruct((T, FW), F32),
                   jax.ShapeDtypeStruct((npair, T, 2), F32)],
        compiler_params=_cp("parallel", "arbitrary"))(*plan, qkv, qkv, qkv, c_cols, c_rows)


def _fox_bwd_dq(qkv, do, c_cols, c_rows, lse, o, plan):
    T = qkv.shape[0]
    tb = _div(T, ATT_TILE)
    nb = T // tb
    npair = HEADS // 2
    scale = HD ** -0.5
    dn = (((1,), (1,)), ((), ()))

    def body(skip_ref, jfirst_ref, ilast_ref, q_ref, k_ref, v_ref, do_ref, cq_ref, ck_ref, l_ref, o_ref, dq_ref, dc_ref,
             d2_ref, dq_scr):
        pr = pl.program_id(0)
        i = pl.program_id(1)
        q = q_ref[...]
        do = do_ref[...]
        lane = lax.broadcasted_iota(jnp.int32, (1, 2 * HD), 1)
        sel0 = lane < HD
        qh = (jnp.where(sel0, q, jnp.zeros_like(q)) * scale, jnp.where(sel0, jnp.zeros_like(q), q) * scale)
        doh = (jnp.where(sel0, do, jnp.zeros_like(do)), jnp.where(sel0, jnp.zeros_like(do), do))
        ce = (cq_ref[:, 0:1] - l_ref[:, 0:1], cq_ref[:, HD:HD + 1] - l_ref[:, HD:HD + 1])
        prod = do.astype(F32) * o_ref[...]
        dl = (jnp.sum(jnp.where(sel0, prod, 0.0), axis=-1, keepdims=True),
              jnp.sum(jnp.where(sel0, 0.0, prod), axis=-1, keepdims=True))
        d2_ref[0, :, 0:1] = dl[0]
        d2_ref[0, :, 1:2] = dl[1]
        row = i * tb + lax.broadcasted_iota(jnp.int32, (tb, 1), 0)
        dq_scr[...] = jnp.zeros(dq_scr.shape, F32)

        def tile(j, h, masked):
            start = pl.multiple_of(j * tb, tb)
            kb = k_ref[pl.ds(start, tb), :]
            vb = v_ref[pl.ds(start, tb), :]
            one = jnp.ones_like(kb)
            kh = jnp.where(sel0, kb, one) if h == 0 else jnp.where(sel0, one, kb)
            t = lax.dot_general(qh[h], kb, dn, preferred_element_type=F32) - ck_ref[0, j, h:h + 1, :]
            if masked:
                col = j * tb + lax.broadcasted_iota(jnp.int32, (1, tb), 1)
                mask = (col <= row) & (col >= FRONT)
                p = jnp.where(mask, jnp.exp(jnp.where(mask, t, NEG) + ce[h]), 0.0)
            else:
                p = jnp.exp(t + ce[h])
            dp = lax.dot_general(doh[h], vb, dn, preferred_element_type=F32)
            ds = p * (dp - dl[h])
            dq_scr[h] += jnp.dot(ds.astype(kb.dtype), kh, preferred_element_type=F32)

        def step(j, _):
            edge = (j == 0) | (j == i)
            for h in range(2):
                live = skip_ref[((2 * pr + h) * nb + i) * nb + j] == 0
                pl.when(live & edge)(functools.partial(tile, j, h, True))
                pl.when(live & jnp.logical_not(edge))(functools.partial(tile, j, h, False))
            return 0

        lax.fori_loop(jfirst_ref[pr * nb + i], i + 1, step, 0)
        dq_ref[...] = (jnp.where(sel0, dq_scr[0], dq_scr[1]) * scale).astype(dq_ref.dtype)
        dc_ref[0, :, 0:1] = dq_scr[0][:, HD:HD + 1]
        dc_ref[0, :, 1:2] = dq_scr[1][:, 0:1]

    blk = lambda off: pl.BlockSpec((tb, 2 * HD), lambda p, i, *_: (i, off + p))
    full = lambda off: pl.BlockSpec((T, 2 * HD), lambda p, i, *_: (0, off + p))
    rows = pl.BlockSpec((1, nb, 2, tb), lambda p, i, *_: (p, 0, 0, 0))
    return pl.pallas_call(
        body, name="fox_bwd_dq",
        grid_spec=pltpu.PrefetchScalarGridSpec(
            num_scalar_prefetch=3, grid=(npair, nb),
            in_specs=[blk(0), full(npair), full(2 * npair), blk(0), blk(0), rows, blk(0), blk(0)],
            out_specs=[blk(0), _pair_cols(tb), _pair_cols(tb)],
            scratch_shapes=[pltpu.VMEM((2, tb, 2 * HD), F32)]),
        out_shape=[jax.ShapeDtypeStruct((T, FW), _BF)] + [jax.ShapeDtypeStruct((npair, T, 2), F32)] * 2,
        compiler_params=_cp("parallel", "arbitrary"))(*plan, qkv, qkv, qkv, do, c_cols, c_rows, lse, o)


def _fox_bwd_dkv(qkv, do, c_cols, ce_rows, delta_rows, plan):
    T = qkv.shape[0]
    tb = _div(T, ATT_TILE)
    nb = T // tb
    npair = HEADS // 2
    scale = HD ** -0.5
    dn = (((1,), (1,)), ((), ()))

    def body(skip_ref, jfirst_ref, ilast_ref, q_ref, k_ref, v_ref, do_ref, ck_ref, ce_ref, dl_ref,
             dk_ref, dv_ref, dc_ref, dk_scr, dv_scr):
        pr = pl.program_id(0)
        jb = pl.program_id(1)
        k = k_ref[...]
        v = v_ref[...]
        lane = lax.broadcasted_iota(jnp.int32, (1, 2 * HD), 1)
        sel0 = lane < HD
        kh = (jnp.where(sel0, k, jnp.zeros_like(k)) * scale, jnp.where(sel0, jnp.zeros_like(k), k) * scale)
        vh = (jnp.where(sel0, v, jnp.zeros_like(v)), jnp.where(sel0, jnp.zeros_like(v), v))
        ck = (ck_ref[:, 0:1], ck_ref[:, HD:HD + 1])
        kidx = jb * tb + lax.broadcasted_iota(jnp.int32, (tb, 1), 0)
        dk_scr[...] = jnp.zeros(dk_scr.shape, F32)
        dv_scr[...] = jnp.zeros(dv_scr.shape, F32)

        def tile(i, h, masked):
            start = pl.multiple_of(i * tb, tb)
            qb = q_ref[pl.ds(start, tb), :]
            dob = do_ref[pl.ds(start, tb), :]
            one = jnp.ones_like(qb)
            qh = jnp.where(sel0, qb, one) if h == 0 else jnp.where(sel0, one, qb)
            t = lax.dot_general(kh[h], qb, dn, preferred_element_type=F32) - ck[h]
            ce = ce_ref[0, i, h:h + 1, :]
            if masked:
                qidx = i * tb + lax.broadcasted_iota(jnp.int32, (1, tb), 1)
                mask = (kidx <= qidx) & (kidx >= FRONT)
                pt = jnp.where(mask, jnp.exp(jnp.where(mask, t, NEG) + ce), 0.0)
            else:
                pt = jnp.exp(t + ce)
            dv_scr[h] += jnp.dot(pt.astype(dob.dtype), dob, preferred_element_type=F32)
            dpt = lax.dot_general(vh[h], dob, dn, preferred_element_type=F32)
            dst = pt * (dpt - dl_ref[0, i, h:h + 1, :])
            dk_scr[h] += jnp.dot(dst.astype(qb.dtype), qh, preferred_element_type=F32)

        def step(i, _):
            edge = (i == jb) | (jb == 0)
            for h in range(2):
                live = skip_ref[((2 * pr + h) * nb + i) * nb + jb] == 0
                pl.when(live & edge)(functools.partial(tile, i, h, True))
                pl.when(live & jnp.logical_not(edge))(functools.partial(tile, i, h, False))
            return 0

        lax.fori_loop(jb, ilast_ref[pr * nb + jb] + 1, step, 0)
        dk_ref[...] = (jnp.where(sel0, dk_scr[0], dk_scr[1]) * scale).astype(dk_ref.dtype)
        dv_ref[...] = jnp.where(sel0, dv_scr[0], dv_scr[1]).astype(dv_ref.dtype)
        dc_ref[0, :, 0:1] = -dk_scr[0][:, HD:HD + 1]
        dc_ref[0, :, 1:2] = -dk_scr[1][:, 0:1]

    blk = lambda off: pl.BlockSpec((tb, 2 * HD), lambda p, j, *_: (j, off + p))
    full = lambda off: pl.BlockSpec((T, 2 * HD), lambda p, j, *_: (0, off + p))
    rows = pl.BlockSpec((1, nb, 2, tb), lambda p, j, *_: (p, 0, 0, 0))
    return pl.pallas_call(
        body, name="fox_bwd_dkv",
        grid_spec=pltpu.PrefetchScalarGridSpec(
            num_scalar_prefetch=3, grid=(npair, nb),
            in_specs=[full(0), blk(npair), blk(2 * npair), full(0), blk(0), rows, rows],
            out_specs=[blk(0), blk(0), _pair_cols(tb)],
            scratch_shapes=[pltpu.VMEM((2, tb, 2 * HD), F32), pltpu.VMEM((2, tb, 2 * HD), F32)]),
        out_shape=[jax.ShapeDtypeStruct((T, FW), _BF)] * 2 + [jax.ShapeDtypeStruct((npair, T, 2), F32)],
        compiler_params=_cp("parallel", "arbitrary"))(*plan, qkv, qkv, qkv, do, c_cols, ce_rows, delta_rows)


CONV_RC = 32
CONV_TC = 256


def _rc(r):
    return pl.ds(r * CONV_RC if isinstance(r, int) else pl.multiple_of(r * CONV_RC, CONV_RC), CONV_RC)


def _prev_rows(x_ref, h_ref, first):
    def at(r):
        if isinstance(r, int):
            assert r == 0
            return jnp.concatenate([jnp.where(first, 0.0, h_ref[...]), x_ref[0:CONV_RC, :]], axis=0)
        return x_ref[pl.ds(pl.multiple_of(r * CONV_RC - 8, 8), CONV_RC + 8), :]
    return at


def _windows_prev(x, kw):
    return [x[8:] if k == kw - 1 else pltpu.roll(x, kw - 1 - k, 0)[8:] for k in range(kw)]


def _fold8(a):
    return a.reshape(CONV_RC // 8, 8, a.shape[-1]).sum(axis=0)


def _halo_prev(tb, tc, off=0):
    return pl.BlockSpec((8, tc), lambda j, i: (jnp.maximum(i * (tb // 8) - 1, 0), j + off))


def _gconv_fwd(rest, w):
    T = rest.shape[0]
    C = 3 * FW
    kw = w.shape[0]
    tb, tc = _div(T, 1280, CONV_RC), CONV_TC

    def body(x_ref, h_ref, w_ref, o_ref):
        at = _prev_rows(x_ref, h_ref, pl.program_id(1) == 0)
        wv = w_ref[...]

        def chunk(r, _):
            win = _windows_prev(at(r), kw)
            u = sum(wv[k:k + 1, :] * win[k] for k in range(kw))
            o_ref[_rc(r), :] = u * _sig(u)
            return 0

        chunk(0, 0)
        lax.fori_loop(1, tb // CONV_RC, chunk, 0)

    return pl.pallas_call(
        body, name="gconv_fwd", grid=(C // tc, T // tb),
        in_specs=[pl.BlockSpec((tb, tc), lambda j, i: (i, j)), _halo_prev(tb, tc), pl.BlockSpec((kw, tc), lambda j, i: (0, j))],
        out_specs=pl.BlockSpec((tb, tc), lambda j, i: (i, j)), out_shape=jax.ShapeDtypeStruct((T, C), F32),
        compiler_params=_cp("parallel", "arbitrary"))(rest, rest, w)


def _gconv_bwd_du(rest, w, dy):
    T = rest.shape[0]
    C = 3 * FW
    kw = w.shape[0]
    tb, tc = _div(T, 1280, CONV_RC), CONV_TC

    def body(x_ref, h_ref, w_ref, dy_ref, du_ref, dw_ref):
        i = pl.program_id(1)
        at = _prev_rows(x_ref, h_ref, i == 0)
        wv = w_ref[...]

        def chunk(r, acc):
            rows = _rc(r)
            win = _windows_prev(at(r), kw)
            u = sum(wv[k:k + 1, :] * win[k] for k in range(kw))
            sg = _sig(u)
            du = dy_ref[rows, :] * sg * (1.0 + u * (1.0 - sg))
            du_ref[rows, :] = du
            return tuple(acc[k] + _fold8(du * win[k]) for k in range(kw))

        acc = chunk(0, tuple(jnp.zeros((8, tc), F32) for _ in range(kw)))
        acc = lax.fori_loop(1, tb // CONV_RC, chunk, acc)
        part = jnp.concatenate([jnp.sum(a, axis=0, keepdims=True) for a in acc], axis=0)

        @pl.when(i == 0)
        def _():
            dw_ref[...] = part

        @pl.when(i > 0)
        def _():
            dw_ref[...] += part

    blk = pl.BlockSpec((tb, tc), lambda j, i: (i, j))
    wsp = pl.BlockSpec((kw, tc), lambda j, i: (0, j))
    return pl.pallas_call(
        body, name="gconv_bwd_du", grid=(C // tc, T // tb), in_specs=[blk, _halo_prev(tb, tc), wsp, blk],
        out_specs=[blk, wsp], out_shape=[jax.ShapeDtypeStruct((T, C), F32), jax.ShapeDtypeStruct((kw, C), F32)],
        compiler_params=_cp("parallel", "arbitrary"))(rest, rest, w, dy)


def _conv_bwd_dx(du, w, out_dtype, name):
    T, C = du.shape
    kw = w.shape[0]
    tb = _div(T, 1280, CONV_RC)
    tc = CONV_TC
    nb = T // tb

    last = tb // CONV_RC - 1

    def body(x_ref, h_ref, w_ref, o_ref):
        wv = w_ref[...]

        def chunk(r, _):
            if isinstance(r, int):
                tail = jnp.where(pl.program_id(1) == nb - 1, 0.0, h_ref[...])
                x = jnp.concatenate([x_ref[r * CONV_RC:(r + 1) * CONV_RC, :], tail], axis=0)
            else:
                x = x_ref[pl.ds(pl.multiple_of(r * CONV_RC, CONV_RC), CONV_RC + 8), :]
            acc = wv[kw - 1:kw, :] * x[:CONV_RC]
            for k in range(kw - 1):
                acc = acc + wv[k:k + 1, :] * pltpu.roll(x, CONV_RC + 8 - (kw - 1 - k), 0)[:CONV_RC]
            o_ref[_rc(r), :] = acc.astype(o_ref.dtype)
            return 0

        lax.fori_loop(0, last, chunk, 0)
        chunk(last, 0)

    halo = pl.BlockSpec((8, tc), lambda j, i: (jnp.minimum((i + 1) * (tb // 8), T // 8 - 1), j))
    return pl.pallas_call(
        body, name=name, grid=(C // tc, nb),
        in_specs=[pl.BlockSpec((tb, tc), lambda j, i: (i, j)), halo, pl.BlockSpec((kw, tc), lambda j, i: (0, j))],
        out_specs=pl.BlockSpec((tb, tc), lambda j, i: (i, j)), out_shape=jax.ShapeDtypeStruct((T, C), out_dtype),
        compiler_params=_cp("parallel", "arbitrary"))(du, du, w)


def _glu_fwd(up, w, b):
    T = up.shape[0]
    kw = w.shape[0]
    tb, tc = _div(T, 1280, CONV_RC), CONV_TC
    nc = DFF // tc

    def body(xg, hg, xu, hu, wg, wu, bg, bu, o_ref):
        first = pl.program_id(1) == 0
        atg, atu = _prev_rows(xg, hg, first), _prev_rows(xu, hu, first)
        wgv, wuv, bgv, buv = wg[...], wu[...], bg[...], bu[...]

        def chunk(r, _):
            wing, winu = _windows_prev(atg(r), kw), _windows_prev(atu(r), kw)
            ug = bgv + sum(wgv[k:k + 1, :] * wing[k] for k in range(kw))
            uu = buv + sum(wuv[k:k + 1, :] * winu[k] for k in range(kw))
            o_ref[_rc(r), :] = (ug * _sig(ug) * uu).astype(o_ref.dtype)
            return 0

        chunk(0, 0)
        lax.fori_loop(1, tb // CONV_RC, chunk, 0)

    blk = lambda off: pl.BlockSpec((tb, tc), lambda j, i: (i, j + off))
    wsp = lambda off: pl.BlockSpec((kw, tc), lambda j, i: (0, j + off))
    bsp = lambda off: pl.BlockSpec((1, tc), lambda j, i: (0, j + off))
    return pl.pallas_call(
        body, name="glu_fwd", grid=(nc, T // tb),
        in_specs=[blk(0), _halo_prev(tb, tc), blk(nc), _halo_prev(tb, tc, nc), wsp(0), wsp(nc), bsp(0), bsp(nc)],
        out_specs=blk(0), out_shape=jax.ShapeDtypeStruct((T, DFF), _BF),
        compiler_params=_cp("parallel", "arbitrary"))(up, up, up, up, w, w, b, b)


def _glu_bwd_du(up, w, b, df):
    T = up.shape[0]
    kw = w.shape[0]
    tb, tc = _div(T, 1280, CONV_RC), CONV_TC
    nc = DFF // tc

    def body(xg, hg, xu, hu, wg, wu, bg, bu, df_ref, dug_ref, duu_ref, dwg_ref, dwu_ref, dbg_ref, dbu_ref):
        i = pl.program_id(1)
        atg, atu = _prev_rows(xg, hg, i == 0), _prev_rows(xu, hu, i == 0)
        wgv, wuv, bgv, buv = wg[...], wu[...], bg[...], bu[...]

        def chunk(r, acc):
            rows = _rc(r)
            wing, winu = _windows_prev(atg(r), kw), _windows_prev(atu(r), kw)
            ug = bgv + sum(wgv[k:k + 1, :] * wing[k] for k in range(kw))
            uu = buv + sum(wuv[k:k + 1, :] * winu[k] for k in range(kw))
            s = _sig(ug)
            df = df_ref[rows, :]
            dug = df * uu * s * (1.0 + ug * (1.0 - s))
            duu = df * ug * s
            dug_ref[rows, :] = dug
            duu_ref[rows, :] = duu
            new = [dug * wing[k] for k in range(kw)] + [duu * winu[k] for k in range(kw)] + [dug, duu]
            return tuple(a + _fold8(v) for a, v in zip(acc, new))

        acc = chunk(0, tuple(jnp.zeros((8, tc), F32) for _ in range(2 * kw + 2)))
        acc = lax.fori_loop(1, tb // CONV_RC, chunk, acc)
        col = [jnp.sum(a, axis=0, keepdims=True) for a in acc]
        parts = (jnp.concatenate(col[:kw], axis=0), jnp.concatenate(col[kw:2 * kw], axis=0), col[2 * kw], col[2 * kw + 1])
        accs = (dwg_ref, dwu_ref, dbg_ref, dbu_ref)

        @pl.when(i == 0)
        def _():
            for r, p in zip(accs, parts):
                r[...] = p

        @pl.when(i > 0)
        def _():
            for r, p in zip(accs, parts):
                r[...] += p

    blk = lambda off: pl.BlockSpec((tb, tc), lambda j, i: (i, j + off))
    wsp = lambda off: pl.BlockSpec((kw, tc), lambda j, i: (0, j + off))
    bsp = lambda off: pl.BlockSpec((1, tc), lambda j, i: (0, j + off))
    return pl.pallas_call(
        body, name="glu_bwd_du", grid=(nc, T // tb),
        in_specs=[blk(0), _halo_prev(tb, tc), blk(nc), _halo_prev(tb, tc, nc), wsp(0), wsp(nc), bsp(0), bsp(nc), blk(0)],
        out_specs=[blk(0), blk(0), wsp(0), wsp(0), bsp(0), bsp(0)],
        out_shape=[jax.ShapeDtypeStruct((T, DFF), F32)] * 2 + [jax.ShapeDtypeStruct((kw, DFF), F32)] * 2
        + [jax.ShapeDtypeStruct((1, DFF), F32)] * 2,
        compiler_params=_cp("parallel", "arbitrary"))(up, up, up, up, w, w, b, b, df)


def _mix_fwd(rest, gate_bias, y_fox, y_gdn):
    T = rest.shape[0]
    tb, tc = _div(T, ROW_TILE, 8), 512
    nc = D // tc
    og = OFF_GATES // tc

    def body(gf, gg, bf, bg, yf, yg, o_ref):
        o_ref[...] = (_sig(gf[...] + bf[...]) * yf[...] + _sig(gg[...] + bg[...]) * yg[...]).astype(o_ref.dtype)

    blk = lambda off: pl.BlockSpec((tb, tc), lambda i, j: (i, j + off))
    bsp = lambda off: pl.BlockSpec((1, tc), lambda i, j: (0, j + off))
    return pl.pallas_call(
        body, name="mix_fwd", grid=(T // tb, nc),
        in_specs=[blk(og), blk(og + nc), bsp(0), bsp(nc), blk(0), blk(0)], out_specs=blk(0),
        out_shape=jax.ShapeDtypeStruct((T, D), _BF), compiler_params=_cp("parallel", "parallel"))(
            rest, rest, gate_bias, gate_bias, y_fox, y_gdn)


def _mix_bwd(rest, gate_bias, y_fox, y_gdn, dmix):
    T = rest.shape[0]
    tb, tc = _div(T, ROW_TILE, 8), 512
    nc = D // tc
    og = OFF_GATES // tc

    def body(gf, gg, bf, bg, yf, yg, dm, dyf, dyg, dgf, dgg, dbf, dbg):
        i = pl.program_id(1)
        d = dm[...]
        sf = _sig(gf[...] + bf[...])
        sg = _sig(gg[...] + bg[...])
        dyf[...] = (d * sf).astype(dyf.dtype)
        dyg[...] = (d * sg).astype(dyg.dtype)
        a = d * yf[...] * sf * (1.0 - sf)
        b = d * yg[...] * sg * (1.0 - sg)
        dgf[...] = a.astype(dgf.dtype)
        dgg[...] = b.astype(dgg.dtype)
        pa = jnp.sum(a, axis=0, keepdims=True)
        pb = jnp.sum(b, axis=0, keepdims=True)

        @pl.when(i == 0)
        def _():
            dbf[...] = pa
            dbg[...] = pb

        @pl.when(i > 0)
        def _():
            dbf[...] += pa
            dbg[...] += pb

    blk = lambda off: pl.BlockSpec((tb, tc), lambda j, i: (i, j + off))
    bsp = lambda off: pl.BlockSpec((1, tc), lambda j, i: (0, j + off))
    return pl.pallas_call(
        body, name="mix_bwd", grid=(nc, T // tb),
        in_specs=[blk(og), blk(og + nc), bsp(0), bsp(nc), blk(0), blk(0), blk(0)],
        out_specs=[blk(0), blk(0), blk(0), blk(0), bsp(0), bsp(0)],
        out_shape=[jax.ShapeDtypeStruct((T, D), _BF)] * 4 + [jax.ShapeDtypeStruct((1, D), F32)] * 2,
        compiler_params=_cp("parallel", "arbitrary"))(rest, rest, gate_bias, gate_bias, y_fox, y_gdn, dmix)


def _loss_head(h2, w, target, n_valid):
    T = h2.shape[0]
    tb = _div(T, ROW_TILE, 8)

    def body(h_ref, w_ref, t_ref, dh_ref, loss_ref, dw_ref):
        i = pl.program_id(0)
        x = h_ref[...]
        r = lax.rsqrt(jnp.mean(x * x, axis=-1, keepdims=True) + EPS)
        xh = x * r
        row = i * tb + lax.broadcasted_iota(jnp.int32, (tb, 1), 0)
        valid = (row >= X0) & (row < X0 + n_valid)
        e = jnp.where(valid, xh * w_ref[...] - t_ref[...], 0.0)
        dy = e * (1.0 / D)
        g = dy * w_ref[...]
        dh_ref[...] = r * (g - xh * jnp.mean(xh * g, axis=-1, keepdims=True))
        lpart = 0.5 * jnp.sum(jnp.sum(e * e, axis=-1, keepdims=True) * (1.0 / D), axis=0, keepdims=True)
        wpart = jnp.sum(dy * xh, axis=0, keepdims=True)

        @pl.when(i == 0)
        def _():
            loss_ref[...] = lpart
            dw_ref[...] = wpart

        @pl.when(i > 0)
        def _():
            loss_ref[...] += lpart
            dw_ref[...] += wpart

    row = pl.BlockSpec((tb, D), lambda i: (i, 0))
    one = pl.BlockSpec((1, D), lambda i: (0, 0))
    return pl.pallas_call(
        body, name="loss_head", grid=(T // tb,), in_specs=[row, one, row],
        out_specs=[row, pl.BlockSpec((1, 1), lambda i: (0, 0)), one],
        out_shape=[jax.ShapeDtypeStruct((T, D), F32), jax.ShapeDtypeStruct((1, 1), F32), jax.ShapeDtypeStruct((1, D), F32)],
        compiler_params=_cp("arbitrary"))(h2, w, target)


def _bd_lo(a, b, ca, cb):
    return lax.dot_general(a.astype(_BF), b.astype(_BF), (((ca,), (cb,)), ((0,), (0,))), preferred_element_type=F32)


def _split2(a):
    hi = a.astype(_BF)
    return hi, (a - hi.astype(F32)).astype(_BF)


def _bd_hi(a, b, ca, cb, exact_a=False):
    dn = (((ca,), (cb,)), ((0,), (0,)))
    dot = lambda x, y: lax.dot_general(x, y, dn, preferred_element_type=F32)
    bh, bl = _split2(b)
    if exact_a:
        ah = a.astype(_BF)
        return dot(ah, bh) + dot(ah, bl)
    ah, al = _split2(a)
    return dot(ah, bh) + (dot(ah, bl) + dot(al, bh))


def _row_to_col(x):
    eye = lax.broadcasted_iota(jnp.int32, (1, CH, CH), 1) == lax.broadcasted_iota(jnp.int32, (1, CH, CH), 2)
    return jnp.sum(jnp.where(eye, jnp.broadcast_to(x, (HEADS, CH, CH)), 0.0), axis=2, keepdims=True)


def _col_to_row(x):
    eye = lax.broadcasted_iota(jnp.int32, (1, CH, CH), 1) == lax.broadcasted_iota(jnp.int32, (1, CH, CH), 2)
    return jnp.sum(jnp.where(eye, jnp.broadcast_to(x, (HEADS, CH, CH)), 0.0), axis=1, keepdims=True)


def _gdn_chunk(q, k, v, bpre, apre, alog, dtb):
    H = HEADS
    r = lax.broadcasted_iota(jnp.int32, (1, CH, CH), 1)
    c = lax.broadcasted_iota(jnp.int32, (1, CH, CH), 2)
    tril, strict = r >= c, r > c
    lb = jnp.broadcast_to(tril.astype(F32), (H, CH, CH))
    rq = lax.rsqrt(jnp.sum(q * q, axis=-1, keepdims=True) + EPS)
    rk = lax.rsqrt(jnp.sum(k * k, axis=-1, keepdims=True) + EPS)
    qh = q * rq
    qn = qh * (HD ** -0.5)
    kn = k * rk
    beta = _sig(bpre)
    x = apre + dtb
    ea = jnp.exp(alog)
    g = -ea * jax.nn.softplus(x)
    gb = jnp.broadcast_to(g, (H, CH, HD))
    gc = _bd_hi(lb, gb, 2, 1, True)
    dm = gc - _col_to_row(gc[:, :, 0:1])
    decay = jnp.where(tril, jnp.exp(jnp.where(tril, dm, 0.0)), 0.0)
    eg = jnp.exp(gc)
    gl = gc[:, CH - 1:CH, :]
    egl = jnp.exp(gl - gc)
    cd = jnp.exp(gl)
    kb = kn * beta
    vb = v * beta
    both = _bd_lo(_rows2(kb, qn), kn, 2, 2)
    kk, qk = both[:, :CH], both[:, CH:]
    pw = -jnp.where(strict, kk * decay, 0.0)
    tm = jnp.where(r == c, 1.0, 0.0) + pw
    pw = _bd_hi(pw, pw, 2, 1)
    for it in range(4):
        mul = _bd_hi if it < 2 else _bd_lo
        both = mul(_rows2(pw, tm), pw, 2, 1)
        pw, tm = both[:, :CH], tm + both[:, CH:]
    tm = tm + _bd_lo(tm, pw, 2, 1)
    kbg = kb * eg
    sol = _bd_hi(tm, _lanes2(vb, kbg), 2, 1)
    value, kcd = sol[:, :, :HD], sol[:, :, HD:]
    attn = jnp.where(tril, qk * decay, 0.0)
    return dict(tril=tril, strict=strict, lb=lb, rq=rq, rk=rk, qh=qh, qn=qn, kn=kn, beta=beta, x=x, ea=ea, g=g,
                decay=decay, eg=eg, egl=egl, cd=cd, kb=kb, vb=vb, kk=kk, tm=tm, kbg=kbg, value=value, kcd=kcd, qk=qk,
                attn=attn, qd=qn * eg, kt=kn * egl, sol=sol)


def _rows2(a, b):
    return jnp.concatenate([a, b], axis=1)


def _lanes2(a, b):
    return jnp.concatenate([a, b], axis=2)


GDN_CPS = 2
GDN_CPS_FWD = 4


def _gdn_specs(T, rev):
    G = GDN_CPS if rev else GDN_CPS_FWD
    ns = T // (G * CH)
    pos = (lambda n: ns - 1 - n) if rev else (lambda n: n)
    mat = pl.BlockSpec((HEADS, G * CH, HD), lambda n: (0, pos(n), 0))
    col = pl.BlockSpec((G, HEADS, 1, CH), lambda n: (pos(n), 0, 0, 0))
    sca = pl.BlockSpec((HEADS, 1, 1), lambda n: (0, 0, 0))
    nw = pl.BlockSpec((1, 1, HD), lambda n: (0, 0, 0))
    st = pl.BlockSpec((G, HEADS, HD, HD), lambda n: (pos(n), 0, 0, 0))
    tok = lambda width, off: pl.BlockSpec((G * CH, width), lambda n: (pos(n), off))
    return ns, mat, col, sca, nw, st, tok


def _split_heads(x):
    return [jnp.stack([x[:, (g * HEADS + h) * HD:(g * HEADS + h + 1) * HD] for h in range(HEADS)])
            for g in range(x.shape[1] // FW)]


def _store_heads(ref, rows, g, val):
    for h in range(HEADS):
        ref[rows, (g * HEADS + h) * HD:(g * HEADS + h + 1) * HD] = val[h]


def _gdn_fwd(conv, rest, bpre, apre, alog, dtb, nw, gather=()):
    T = conv.shape[0]
    ns, mat, col, sca, nws, st, tok = _gdn_specs(T, False)
    rows = [slice(g * CH, (g + 1) * CH) for g in range(GDN_CPS_FWD)]
    ng = len(gather)

    def body(*refs):
        c_ref, z_ref, b_ref, a_ref, al_ref, dt_ref, nw_ref = refs[:7]
        o_ref, og_ref, st_ref = refs[7 + ng:10 + ng]
        s_scr = refs[10 + 2 * ng]
        comm = (refs[7:7 + ng], refs[10 + ng:10 + 2 * ng], refs[11 + 2 * ng:])

        @pl.when(pl.program_id(0) == 0)
        def _():
            s_scr[...] = jnp.zeros_like(s_scr)
            if ng:
                _gather_phase(*comm, True)

        locs = [_gdn_chunk(*_split_heads(c_ref[rows[g], :]), _row_to_col(b_ref[g]), _row_to_col(a_ref[g]), al_ref[...],
                           dt_ref[...]) for g in range(GDN_CPS_FWD)]
        s = s_scr[...]
        for g, L in enumerate(locs):
            st_ref[g] = s
            both = _bd_lo(_rows2(L["kcd"], L["qd"]), s, 2, 1)
            v_new = L["value"] - both[:, :CH]
            o = both[:, CH:] + _bd_lo(L["attn"], v_new, 2, 1)
            s = s * L["cd"] + _bd_lo(L["kt"], v_new, 1, 1)
            o_ref[:, rows[g], :] = o
            zz, = _split_heads(z_ref[rows[g], :])
            rs = lax.rsqrt(jnp.mean(o * o, axis=-1, keepdims=True) + EPS)
            _store_heads(og_ref, rows[g], 0, o * rs * nw_ref[...] * zz * _sig(zz))
        s_scr[...] = s
        if ng:
            pl.when(pl.program_id(0) == ns - 1)(functools.partial(_gather_phase, *comm, False))

    return pl.pallas_call(
        body, name="gdn_fwd", grid=(ns,),
        in_specs=[tok(3 * FW, 0), tok(FW, OFF_Z // FW), col, col, sca, sca, nws] + [HBM_SPEC] * ng,
        out_specs=[mat, tok(FW, 0), st] + [HBM_SPEC] * ng,
        out_shape=[jax.ShapeDtypeStruct((HEADS, T, HD), F32), jax.ShapeDtypeStruct((T, FW), F32),
                   jax.ShapeDtypeStruct((T // CH, HEADS, HD, HD), F32)] + _gather_shapes(gather),
        scratch_shapes=[pltpu.VMEM((HEADS, HD, HD), F32)] + (_gather_sems(ng) if ng else []),
        compiler_params=_cp("arbitrary"))(conv, rest, bpre, apre, alog, dtb, nw, *gather)


def _gdn_bwd(conv, rest, bpre, apre, alog, dtb, nw, states, o, dog, scatter=()):
    T = conv.shape[0]
    ns, mat, col, sca, nws, st, tok = _gdn_specs(T, True)
    rows = [slice(g * CH, (g + 1) * CH) for g in range(GDN_CPS)]
    ng = len(scatter)

    def body(*refs):
        c_ref, z_ref, b_ref, a_ref, al_ref, dt_ref, nw_ref, st_ref, o_ref, dog_ref = refs[:10]
        dc_ref, dz_ref, db_ref, da_ref, dal_ref, ddt_ref, dnw_ref = refs[10 + ng:17 + ng]
        ds_scr = refs[17 + 2 * ng]
        comm = (refs[10:10 + ng], refs[17 + ng:17 + 2 * ng], refs[18 + 2 * ng:])

        @pl.when(pl.program_id(0) == 0)
        def _():
            ds_scr[...] = jnp.zeros_like(ds_scr)
            dal_ref[...] = jnp.zeros_like(dal_ref)
            ddt_ref[...] = jnp.zeros_like(ddt_ref)
            dnw_ref[...] = jnp.zeros_like(dnw_ref)
            if ng:
                _scatter_phase(*comm, True)

        splits = [_split_heads(c_ref[rows[g], :]) for g in range(GDN_CPS)]
        locs = [_gdn_chunk(*splits[g], _row_to_col(b_ref[g]), _row_to_col(a_ref[g]), al_ref[...], dt_ref[...])
                for g in range(GDN_CPS)]
        dsn = ds_scr[...]
        acc_al, acc_dt, acc_nw = 0.0, 0.0, 0.0
        for g in reversed(range(GDN_CPS)):
            L, vv = locs[g], splits[g][2]
            tril, strict, lb = L["tril"], L["strict"], L["lb"]
            qn, kn, kb, beta, decay, eg, egl, cd = L["qn"], L["kn"], L["kb"], L["beta"], L["decay"], L["eg"], L["egl"], L["cd"]
            value, kcd, attn, qd, kt, tm = L["value"], L["kcd"], L["attn"], L["qd"], L["kt"], L["tm"]
            s = st_ref[g]
            v_new = value - _bd_lo(kcd, s, 2, 1)
            oo = o_ref[:, rows[g], :]
            zz, = _split_heads(z_ref[rows[g], :])
            dog_, = _split_heads(dog_ref[rows[g], :])
            sz = _sig(zz)
            rs = lax.rsqrt(jnp.mean(oo * oo, axis=-1, keepdims=True) + EPS)
            oh = oo * rs
            _store_heads(dz_ref, rows[g], 0, dog_ * oh * nw_ref[...] * sz * (1.0 + zz * (1.0 - sz)))
            don = dog_ * zz * sz
            gdy = don * nw_ref[...]
            do = rs * (gdy - oh * jnp.mean(oh * gdy, axis=-1, keepdims=True))
            acc_nw = acc_nw + jnp.sum(don * oh, axis=(0, 1), keepdims=True)
            d_vnew = _bd_lo(attn, do, 1, 1) + _bd_lo(kt, dsn, 2, 1)
            both = _bd_lo(_rows2(do, d_vnew), s, 2, 2)
            d_qd, d_kcd = both[:, :CH], -both[:, CH:]
            d_attn = jnp.where(tril, _bd_lo(do, v_new, 2, 2), 0.0)
            d_kt = _bd_lo(v_new, dsn, 2, 2)
            d_cd = jnp.sum(s * dsn, axis=(1, 2), keepdims=True)
            dsn_next = cd * dsn + _bd_lo(_rows2(qd, kcd), _rows2(do, -d_vnew), 1, 1)
            dsol = _bd_hi(tm, _lanes2(d_vnew, d_kcd), 1, 1)
            d_vb, d_kbg = dsol[:, :, :HD], dsol[:, :, HD:]
            da = -jnp.where(strict, _bd_lo(dsol, L["sol"], 2, 2), 0.0)
            dkk = da * decay
            dqk = d_attn * decay
            d_decay = da * L["kk"] + d_attn * L["qk"]
            both = _bd_lo(_rows2(dkk, dqk), kn, 2, 1)
            d_kb = both[:, :CH] + d_kbg * eg
            d_qn = both[:, CH:] + d_qd * eg
            d_kn = _bd_lo(_rows2(dkk, dqk), _rows2(kb, qn), 1, 1) + d_kt * egl
            dd = d_decay * decay
            d_gc = jnp.sum(dd + d_qd * qd + d_kbg * L["kbg"] - d_kt * kt, axis=-1, keepdims=True) \
                - _row_to_col(jnp.sum(dd, axis=1, keepdims=True))
            d_gl = jnp.sum(d_kt * kt, axis=(1, 2), keepdims=True) + d_cd * cd[:, :, 0:1]
            last = lax.broadcasted_iota(jnp.int32, (1, CH, 1), 1) == CH - 1
            d_gc = d_gc + jnp.where(last, d_gl, 0.0)
            dg = _bd_hi(lb, jnp.broadcast_to(d_gc, (HEADS, CH, HD)), 1, 1, True)[:, :, 0:1]
            d_apre = -dg * L["ea"] * _sig(L["x"])
            da_ref[g] = _col_to_row(d_apre)
            acc_al = acc_al + jnp.sum(dg * L["g"], axis=1, keepdims=True)
            acc_dt = acc_dt + jnp.sum(d_apre, axis=1, keepdims=True)
            d_beta = jnp.sum(d_kb * kn + d_vb * vv, axis=-1, keepdims=True)
            db_ref[g] = _col_to_row(d_beta * beta * (1.0 - beta))
            d_kn = d_kn + d_kb * beta
            qh = L["qh"]
            _store_heads(dc_ref, rows[g], 0, (HD ** -0.5) * L["rq"] * (d_qn - qh * jnp.sum(qh * d_qn, axis=-1, keepdims=True)))
            _store_heads(dc_ref, rows[g], 1, L["rk"] * (d_kn - kn * jnp.sum(kn * d_kn, axis=-1, keepdims=True)))
            _store_heads(dc_ref, rows[g], 2, d_vb * beta)
            dsn = dsn_next
        ds_scr[...] = dsn
        dal_ref[...] += acc_al
        ddt_ref[...] += acc_dt
        dnw_ref[...] += acc_nw
        if ng:
            pl.when(pl.program_id(0) == ns - 1)(functools.partial(_scatter_phase, *comm, False))

    c3 = jax.ShapeDtypeStruct((T // CH, HEADS, 1, CH), F32)
    s3 = jax.ShapeDtypeStruct((HEADS, 1, 1), F32)
    return pl.pallas_call(
        body, name="gdn_bwd", grid=(ns,),
        in_specs=[tok(3 * FW, 0), tok(FW, OFF_Z // FW), col, col, sca, sca, nws, st, mat, tok(FW, 0)] + [HBM_SPEC] * ng,
        out_specs=[tok(3 * FW, 0), tok(FW, 0), col, col, sca, sca, nws] + [HBM_SPEC] * ng,
        out_shape=[jax.ShapeDtypeStruct((T, 3 * FW), F32), jax.ShapeDtypeStruct((T, FW), F32), c3, c3, s3, s3,
                   jax.ShapeDtypeStruct((1, 1, HD), F32)] + _scatter_shapes(scatter),
        scratch_shapes=[pltpu.VMEM((HEADS, HD, HD), F32)] + (_scatter_sems(ng) if ng else []),
        compiler_params=_cp("arbitrary"))(conv, rest, bpre, apre, alog, dtb, nw, states, o, dog, *scatter)


LATE_KEYS = {"w_branch_fox": "w_bfox", "w_branch_gdn": "w_bgdn", "w_out": "w_out", "ffn_w_up": "w_up", "ffn_w_down": "w_down"}


def _local_step(x, target, w, late=None):
    seq = x.shape[0]
    T = -(-(X0 + seq) // T_ALIGN) * T_ALIGN
    back = T - X0 - seq
    tb_att = _div(T, ATT_TILE)
    h0 = jnp.concatenate([jnp.zeros((FRONT, D), F32), w["meta"], x, jnp.zeros((back, D), F32)], axis=0)
    tgt = jnp.concatenate([jnp.zeros((X0, D), F32), target, jnp.zeros((back, D), F32)], axis=0)
    row = lambda v: v.reshape(1, -1)
    nmix, nffn, nfin = row(w["norm_mix"]), row(w["norm_ffn"]), row(w["norm_final"])
    gate_b = row(w["gate_bias"])
    fconv_b = row(w["ffn_conv_b"])
    bias128 = jnp.zeros((1, SMALL_W), F32).at[0, :HEADS].set(w["fgt_bias"])
    alog = w["a_log"].reshape(HEADS, 1, 1)
    dtb = w["dt_bias"].reshape(HEADS, 1, 1)
    gnw = w["gdn_norm"].reshape(1, 1, HD)

    a = _rms_fwd(h0, nmix, "rms_mix")
    pf = _mm(a, w["w_fox"], out_dtype=_BF, name="proj_fox")
    rest = _mm(a, w["w_rest"], name="proj_rest")
    small = rest[:, OFF_SMALL:]
    c8 = _fgate_fwd(small, bias128)[:, :HEADS]
    c_cols, c_rows = _att_cols(c8), _att_rows(c8, tb_att)
    ones_blk = jnp.asarray(np.kron(np.eye(HEADS, dtype=np.float32), np.ones((HD, HD), np.float32)))
    plan = _att_plan(_att_stats(pf, c_cols, ones_blk))
    o_fox, lse, lse2 = _fox_fwd(pf, c_cols, c_rows, plan)
    conv = _gconv_fwd(rest, w["gdn_conv"])
    chunk_rows = lambda a8: a8.reshape(T // CH, CH, HEADS).transpose(0, 2, 1)[:, :, None, :]
    bpre = chunk_rows(small[:, HEADS:2 * HEADS])
    apre = chunk_rows(small[:, 2 * HEADS:3 * HEADS])
    if late is None:
        o_raw, og, states = _gdn_fwd(conv, rest, bpre, apre, alog, dtb, gnw)
    else:
        o_raw, og, states, *got = _gdn_fwd(conv, rest, bpre, apre, alog, dtb, gnw, gather=list(late.values()))
        w = dict(w, **{LATE_KEYS[n]: _from_chips(g, AXIS[n]) for n, g in zip(late, got)})
    w_up, w_down = w["w_up"], w["w_down"]
    y_fox = _mm(o_fox, w["w_bfox"], name="y_fox")
    y_gdn = _mm(og, w["w_bgdn"], name="y_gdn")
    mix = _mix_fwd(rest, gate_b, y_fox, y_gdn)
    h1 = _mm(mix, w["w_out"], resid=h0, name="out_proj")
    b = _rms_fwd(h1, nffn, "rms_ffn")
    up = _mm(b, w_up, name="ffn_up")
    f = _glu_fwd(up, w["ffn_conv"], fconv_b)
    h2 = _mm(f, w_down, resid=h1, name="ffn_down")
    dh2, loss, d_nfin = _loss_head(h2, nfin, tgt, seq)

    d_f = _mm(dh2, w_down, tb=True, name="d_f")
    g_down = _mm_t(f, dh2, "g_down")
    dug, duu, dwg, dwu, dbg, dbu = _glu_bwd_du(up, w["ffn_conv"], fconv_b, d_f)
    dxg = _conv_bwd_dx(dug, w["ffn_conv"][:, :DFF], _BF, "fconv_dx_gate")
    dxu = _conv_bwd_dx(duu, w["ffn_conv"][:, DFF:], _BF, "fconv_dx_up")
    d_b = _mm(dxg, w_up[:, :DFF], tb=True, name="d_b_gate")
    d_b = _mm(dxu, w_up[:, DFF:], tb=True, resid=d_b, name="d_b_up")
    g_up = jnp.concatenate([_mm_t(b, dxg, "g_up_gate"), _mm_t(b, dxu, "g_up_up")], axis=1)
    dh1, d_nffn = _rms_bwd(h1, nffn, d_b, dh2, "rms_ffn_bwd")

    dmix = _mm(dh1, w["w_out"], tb=True, name="d_mix")
    g_out = _mm_t(mix, dh1, "g_out")
    dyf, dyg, dgf, dgg, dgbf, dgbg = _mix_bwd(rest, gate_b, y_fox, y_gdn, dmix)
    do_fox = _mm(dyf, w["w_bfox"], tb=True, out_dtype=_BF, name="d_o_fox")
    g_bfox = _mm_t(o_fox, dyf, "g_bfox")
    d_og = _mm(dyg, w["w_bgdn"], tb=True, name="d_o_gdn")
    g_bgdn = _mm_t(og, dyg, "g_bgdn")

    dq, dcq, delta2 = _fox_bwd_dq(pf, do_fox, c_cols, c_rows, lse, o_fox, plan)
    dk, dv, dck = _fox_bwd_dkv(pf, do_fox, c_cols, _att_rows(c8 - _from_pairs(lse2), tb_att),
                               _att_rows(_from_pairs(delta2), tb_att), plan)
    dc = jnp.pad(_from_pairs(dcq + dck), ((0, 0), (0, SMALL_W - HEADS)))
    dfp, d_fb = _fgate_bwd(small, bias128, dc)

    if late is None:
        late_recv = None
        dconv, dz, dbp, dap, d_alog, d_dtb, d_gnw = _gdn_bwd(conv, rest, bpre, apre, alog, dtb, gnw, states, o_raw, d_og)
    else:
        ready = dict(w_branch_fox=g_bfox, w_branch_gdn=g_bgdn, w_out=g_out, ffn_w_up=g_up, ffn_w_down=g_down)
        send = [_by_chip(ready[n].astype(jnp.bfloat16), AXIS[n]) for n in late]
        dconv, dz, dbp, dap, d_alog, d_dtb, d_gnw, *late_recv = _gdn_bwd(
            conv, rest, bpre, apre, alog, dtb, gnw, states, o_raw, d_og, scatter=send)
    du_g, g_gconv = _gconv_bwd_du(rest, w["gdn_conv"], dconv)
    dgx = _conv_bwd_dx(du_g, w["gdn_conv"], _BF, "gconv_dx")
    token_rows = lambda a: a[:, :, 0, :].transpose(0, 2, 1).reshape(T, HEADS)
    dsmall = jnp.concatenate([dfp[:, :HEADS], token_rows(dbp), token_rows(dap),
                              jnp.zeros((T, SMALL_W - 3 * HEADS), F32)], axis=1)
    drest = jnp.concatenate([dgx, dz.astype(_BF), dgf, dgg, dsmall.astype(_BF)], axis=1)
    dfox = jnp.concatenate([dq, dk, dv], axis=1)
    g_fox = _mm_t(a, dfox, "g_w_fox")
    g_rest = _mm_t(a, drest, "g_w_rest")
    sm = lambda lo: g_rest[:, OFF_SMALL + lo:OFF_SMALL + lo + HEADS]
    g_w_in = jnp.concatenate([g_fox, sm(0), g_rest[:, :3 * FW], g_rest[:, OFF_Z:OFF_Z + FW], sm(HEADS), sm(2 * HEADS),
                              g_rest[:, OFF_GATES:OFF_GATES + 2 * D]], axis=1)
    d_a = _mm(dfox, w["w_fox"], tb=True, name="d_a_fox")
    if late is None:
        d_a = _mm(drest, w["w_rest"], tb=True, resid=d_a, name="d_a_rest")
    else:
        d_a, w_in_recv = _mm(drest, w["w_rest"], tb=True, resid=d_a, name="d_a_rest",
                             scatter=[_by_chip(g_w_in.astype(jnp.bfloat16), AXIS["w_in"])])
        late_recv = [w_in_recv] + late_recv
    dh0, d_nmix = _rms_bwd(h0, nmix, d_a, dh1, "rms_mix_bwd")

    grads = dict(
        meta_tokens=dh0[FRONT:X0], w_in=g_w_in, fgt_bias=d_fb[0, :HEADS], gdn_conv_w=g_gconv,
        gdn_a_log=d_alog.reshape(HEADS), gdn_dt_bias=d_dtb.reshape(HEADS), gdn_norm_w=d_gnw.reshape(HD),
        gate_bias=jnp.concatenate([dgbf, dgbg], axis=1).reshape(2 * D), w_branch_fox=g_bfox, w_branch_gdn=g_bgdn,
        w_out=g_out, norm_mix_w=d_nmix.reshape(D), norm_ffn_w=d_nffn.reshape(D), ffn_w_up=g_up,
        ffn_conv_w=jnp.concatenate([dwg, dwu], axis=1), ffn_conv_b=jnp.concatenate([dbg, dbu], axis=1).reshape(2 * DFF),
        ffn_w_down=g_down, norm_final_w=d_nfin.reshape(D), late_recv=late_recv)
    return loss, dh0[X0:X0 + seq], grads


N_CHIPS = 4
PACK_W = 1024
PACK_ROW_ALIGN = 32
BIG_EARLY = ("w_in",)
BIG_LATE = ("w_branch_fox", "w_branch_gdn", "w_out", "ffn_w_up", "ffn_w_down")
BIG = BIG_EARLY + BIG_LATE
WEIGHTS = (
    ("meta_tokens", (N_META, D), 1), ("w_in", (D, 3 * FW + HEADS + 4 * FW + 2 * HEADS + 2 * D), 1), ("fgt_bias", (1, HEADS), None),
    ("gdn_conv_w", (4, 3 * FW), 1), ("gdn_a_log", (1, HEADS), None), ("gdn_dt_bias", (1, HEADS), None),
    ("gdn_norm_w", (1, HD), None), ("gate_bias", (1, 2 * D), None), ("w_branch_fox", (FW, D), 1),
    ("w_branch_gdn", (FW, D), 1), ("w_out", (D, D), 0), ("norm_mix_w", (1, D), None), ("norm_ffn_w", (1, D), None),
    ("ffn_w_up", (D, 2 * DFF), 1), ("ffn_conv_w", (3, 2 * DFF), 1), ("ffn_conv_b", (1, 2 * DFF), None),
    ("ffn_w_down", (DFF, D), 0), ("norm_final_w", (1, D), None))
SPLIT_F32 = ("meta_tokens", "gdn_conv_w", "ffn_conv_w")


def _shard_shape(shape, axis):
    if axis is None:
        return shape
    return tuple(s // N_CHIPS if a == axis else s for a, s in enumerate(shape))


def _shard_of(full, axis, q):
    if axis is None:
        return full
    n = full.shape[axis] // N_CHIPS
    return lax.slice_in_dim(full, q * n, (q + 1) * n, axis=axis)


def _pack_rows(n_elems):
    rows = -(-n_elems // PACK_W)
    return -(-rows // PACK_ROW_ALIGN) * PACK_ROW_ALIGN


def _pack(pieces, dtype):
    flat = jnp.concatenate([p.reshape(-1).astype(dtype) for p in pieces])
    rows = _pack_rows(flat.shape[0])
    return jnp.pad(flat, (0, rows * PACK_W - flat.shape[0])).reshape(rows, PACK_W)


def _unpack(slab, shapes):
    flat = slab.reshape(-1)
    out, off = [], 0
    for s in shapes:
        n = int(np.prod(s))
        out.append(flat[off:off + n].reshape(s))
        off += n
    return out


HBM_SPEC = pl.BlockSpec(memory_space=pltpu.HBM)
MESH_ID = pl.DeviceIdType.MESH


def _scatter_chips(srcs, name):
    n = len(srcs)

    def body(*refs):
        _scatter_phase(refs[:n], refs[n:2 * n], refs[2 * n:], True)
        _scatter_phase(refs[:n], refs[n:2 * n], refs[2 * n:], False)

    return pl.pallas_call(body, name=name, in_specs=[HBM_SPEC] * n, out_specs=[HBM_SPEC] * n,
                          out_shape=_scatter_shapes(srcs), scratch_shapes=_scatter_sems(n))(*srcs)


def _scatter_shapes(srcs):
    return [jax.ShapeDtypeStruct(s.shape, s.dtype) for s in srcs]


def _scatter_sems(n):
    return [pltpu.SemaphoreType.DMA((3 * n,)), pltpu.SemaphoreType.DMA((3 * n,)), pltpu.SemaphoreType.DMA((n,))]


def _scatter_phase(src_refs, out_refs, sems, issue):
    n = len(src_refs)
    send_sems, recv_sems, local_sems = sems
    x, y, c = lax.axis_index("x"), lax.axis_index("y"), lax.axis_index("c")
    q = 2 * x + y
    peers = [(1 - x, y), (x, 1 - y), (1 - x, 1 - y)]

    def remote(a, k, src_slot, dst_slot):
        px, py = peers[k]
        return pltpu.make_async_remote_copy(
            src_ref=src_refs[a].at[src_slot], dst_ref=out_refs[a].at[dst_slot], send_sem=send_sems.at[3 * a + k],
            recv_sem=recv_sems.at[3 * a + k], device_id=(px, py, c), device_id_type=MESH_ID)

    mine = [pltpu.make_async_copy(src_refs[a].at[q], out_refs[a].at[q], local_sems.at[a]) for a in range(n)]
    sends = [remote(a, k, 2 * px + py, q) for a in range(n) for k, (px, py) in enumerate(peers)]
    if issue:
        for cp in mine + sends:
            cp.start()
        return
    for a in range(n):
        for k, (px, py) in enumerate(peers):
            remote(a, k, 0, 2 * px + py).wait_recv()
    for cp in sends:
        cp.wait_send()
    for cp in mine:
        cp.wait()


def _gather_chips(srcs, name):
    n = len(srcs)

    def body(*refs):
        _gather_phase(refs[:n], refs[n:2 * n], refs[2 * n:], True)
        _gather_phase(refs[:n], refs[n:2 * n], refs[2 * n:], False)

    return pl.pallas_call(body, name=name, in_specs=[HBM_SPEC] * n, out_specs=[HBM_SPEC] * n,
                          out_shape=_gather_shapes(srcs), scratch_shapes=_gather_sems(n))(*srcs)


def _gather_shapes(srcs):
    return [jax.ShapeDtypeStruct((N_CHIPS,) + s.shape, s.dtype) for s in srcs]


def _gather_sems(n):
    return [pltpu.SemaphoreType.DMA((3 * n,))] * 4 + [pltpu.SemaphoreType.DMA((n,))]


def _gather_phase(src_refs, out_refs, sems, issue):
    n = len(src_refs)
    ici_send, ici_recv, d2d_send, d2d_recv, local_sems = sems
    x, y, c = lax.axis_index("x"), lax.axis_index("y"), lax.axis_index("c")
    q = 2 * x + y
    peers = [(1 - x, y), (x, 1 - y), (1 - x, 1 - y)]

    def half(a, which):
        r = src_refs[a].shape[0] // 2
        return pl.ds(which * r, r)

    def ici(a, k, slot):
        px, py = peers[k]
        return pltpu.make_async_remote_copy(
            src_ref=src_refs[a].at[half(a, c)], dst_ref=out_refs[a].at[slot, half(a, c)], send_sem=ici_send.at[3 * a + k],
            recv_sem=ici_recv.at[3 * a + k], device_id=(px, py, c), device_id_type=MESH_ID)

    def d2d(a, k, which):
        px, py = peers[k]
        rows = out_refs[a].at[2 * px + py, half(a, which)]
        return pltpu.make_async_remote_copy(
            src_ref=rows, dst_ref=rows, send_sem=d2d_send.at[3 * a + k], recv_sem=d2d_recv.at[3 * a + k],
            device_id=(x, y, 1 - c), device_id_type=MESH_ID)

    mine = [pltpu.make_async_copy(src_refs[a], out_refs[a].at[q], local_sems.at[a]) for a in range(n)]
    sends = [ici(a, k, q) for a in range(n) for k in range(3)]
    if issue:
        for cp in mine + sends:
            cp.start()
        return
    passed = []
    for a in range(n):
        for k, (px, py) in enumerate(peers):
            ici(a, k, 2 * px + py).wait_recv()
            passed.append(d2d(a, k, c))
            passed[-1].start()
    for a in range(n):
        for k in range(3):
            d2d(a, k, 1 - c).wait_recv()
    for cp in sends + passed:
        cp.wait_send()
    for cp in mine:
        cp.wait()


def _sibling_swap(slabs, name):
    n = len(slabs)

    def body(*refs):
        src_refs, out_refs, send_sems, recv_sems = refs[:n], refs[n:2 * n], refs[2 * n], refs[2 * n + 1]
        x, y, c = lax.axis_index("x"), lax.axis_index("y"), lax.axis_index("c")
        cps = [pltpu.make_async_remote_copy(src_ref=src_refs[a], dst_ref=out_refs[a], send_sem=send_sems.at[a],
                                            recv_sem=recv_sems.at[a], device_id=(x, y, 1 - c), device_id_type=MESH_ID)
               for a in range(n)]
        for cp in cps:
            cp.start()
        for cp in cps:
            cp.wait_recv()
        for cp in cps:
            cp.wait_send()

    return pl.pallas_call(
        body, name=name, in_specs=[HBM_SPEC] * n, out_specs=[HBM_SPEC] * n,
        out_shape=[jax.ShapeDtypeStruct(s.shape, s.dtype) for s in slabs],
        scratch_shapes=[pltpu.SemaphoreType.DMA((n,)), pltpu.SemaphoreType.DMA((n,))])(*slabs)


def _sum_chips(r, name):
    rows, cols = r.shape[1:]
    tb = _div(rows, 256, 16)

    def body(r0, r1, r2, r3, o_ref):
        o_ref[...] = ((r0[0].astype(F32) + r1[0].astype(F32)) + r2[0].astype(F32)) + r3[0].astype(F32)

    spec = lambda j: pl.BlockSpec((1, tb, cols), lambda i: (j, i, 0))
    return pl.pallas_call(
        body, name=name, grid=(rows // tb,), in_specs=[spec(0), spec(1), spec(2), spec(3)],
        out_specs=pl.BlockSpec((tb, cols), lambda i: (i, 0)), out_shape=jax.ShapeDtypeStruct((rows, cols), F32),
        compiler_params=_cp("parallel"))(r, r, r, r)


def _adamw(w, m, v, p, q, name):
    rows, cols = w.shape
    tb = _div(rows, 256, 8)

    def body(w_ref, m_ref, v_ref, p_ref, q_ref, g_ref, d_ref, nm_ref, nv_ref):
        g = p_ref[...] + q_ref[...]
        m_new = B1 * m_ref[...] + (1.0 - B1) * g
        v_new = B2 * v_ref[...] + (1.0 - B2) * (g * g)
        g_ref[...] = g
        nm_ref[...] = m_new
        nv_ref[...] = v_new
        m_hat = m_new / (1.0 - B1 ** STEP)
        v_hat = v_new / (1.0 - B2 ** STEP)
        d_ref[...] = -LR * (m_hat / (jnp.sqrt(v_hat) + AEPS) + WD * w_ref[...])

    spec = pl.BlockSpec((tb, cols), lambda i: (i, 0))
    return pl.pallas_call(
        body, name=name, grid=(rows // tb,), in_specs=[spec] * 5, out_specs=[spec] * 4,
        out_shape=[jax.ShapeDtypeStruct((rows, cols), F32)] * 4, compiler_params=_cp("parallel"))(w, m, v, p, q)


def _split_w_in(w_in):
    o1 = 3 * FW
    o2 = o1 + HEADS
    o3 = o2 + 3 * FW
    o4 = o3 + FW
    o5 = o4 + HEADS
    o6 = o5 + HEADS
    pad = jnp.zeros((w_in.shape[0], SMALL_W - 3 * HEADS), w_in.dtype)
    rest = jnp.concatenate([w_in[:, o2:o3], w_in[:, o3:o4], w_in[:, o6:], w_in[:, o1:o2], w_in[:, o4:o5], w_in[:, o5:o6], pad],
                           axis=1)
    return w_in[:, :o1], rest


AXIS = {n: a for n, _, a in WEIGHTS}
SMALL = tuple(n for n, _, _ in WEIGHTS if n not in BIG)


def _by_chip(full, axis):
    rows, cols = full.shape
    if axis == 0:
        return full.reshape(N_CHIPS, rows // N_CHIPS, cols)
    return full.reshape(rows, N_CHIPS, cols // N_CHIPS).transpose(1, 0, 2)


def _from_chips(parts, axis):
    _, r, c = parts.shape
    if axis == 0:
        return parts.reshape(N_CHIPS * r, c)
    return parts.transpose(1, 0, 2).reshape(r, N_CHIPS * c)


def _gather_weights(shards):
    hi = {n: shards[n].astype(jnp.bfloat16) for n in SPLIT_F32}
    lo = [(shards[n] - hi[n].astype(F32)).astype(jnp.bfloat16) for n in SPLIT_F32]
    slab = _pack([hi[n] for n in SPLIT_F32] + lo, jnp.bfloat16)
    got = _gather_chips([shards[n].astype(jnp.bfloat16) for n in BIG_EARLY] + [slab], "gather_weights")
    full = {n: _from_chips(g, AXIS[n]) for n, g in zip(BIG_EARLY, got)}
    shapes = [shards[n].shape for n in SPLIT_F32] * 2
    per_chip = [_unpack(got[-1][j], shapes) for j in range(N_CHIPS)]
    for i, n in enumerate(SPLIT_F32):
        join = lambda off: jnp.concatenate([per_chip[j][off + i] for j in range(N_CHIPS)], axis=1).astype(F32)
        full[n] = join(0) + join(len(SPLIT_F32))
    return full


def kernel(x, meta_tokens, w_in, fgt_bias, gdn_conv_w, gdn_a_log, gdn_dt_bias, gdn_norm_w, gate_bias, w_branch_fox, w_branch_gdn, w_out, norm_mix_w, norm_ffn_w, ffn_w_up, ffn_conv_w, ffn_conv_b, ffn_w_down, norm_final_w, loss_target, m_meta_tokens, m_w_in, m_fgt_bias, m_gdn_conv_w, m_gdn_a_log, m_gdn_dt_bias, m_gdn_norm_w, m_gate_bias, m_w_branch_fox, m_w_branch_gdn, m_w_out, m_norm_mix_w, m_norm_ffn_w, m_ffn_w_up, m_ffn_conv_w, m_ffn_conv_b, m_ffn_w_down, m_norm_final_w, v_meta_tokens, v_w_in, v_fgt_bias, v_gdn_conv_w, v_gdn_a_log, v_gdn_dt_bias, v_gdn_norm_w, v_gate_bias, v_w_branch_fox, v_w_branch_gdn, v_w_out, v_norm_mix_w, v_norm_ffn_w, v_ffn_w_up, v_ffn_conv_w, v_ffn_conv_b, v_ffn_w_down, v_norm_final_w):
    weights = dict(meta_tokens=meta_tokens, w_in=w_in, fgt_bias=fgt_bias, gdn_conv_w=gdn_conv_w, gdn_a_log=gdn_a_log, gdn_dt_bias=gdn_dt_bias, gdn_norm_w=gdn_norm_w, gate_bias=gate_bias, w_branch_fox=w_branch_fox, w_branch_gdn=w_branch_gdn, w_out=w_out, norm_mix_w=norm_mix_w, norm_ffn_w=norm_ffn_w, ffn_w_up=ffn_w_up, ffn_conv_w=ffn_conv_w, ffn_conv_b=ffn_conv_b, ffn_w_down=ffn_w_down, norm_final_w=norm_final_w)
    m_in = dict(meta_tokens=m_meta_tokens, w_in=m_w_in, fgt_bias=m_fgt_bias, gdn_conv_w=m_gdn_conv_w, gdn_a_log=m_gdn_a_log, gdn_dt_bias=m_gdn_dt_bias, gdn_norm_w=m_gdn_norm_w, gate_bias=m_gate_bias, w_branch_fox=m_w_branch_fox, w_branch_gdn=m_w_branch_gdn, w_out=m_w_out, norm_mix_w=m_norm_mix_w, norm_ffn_w=m_norm_ffn_w, ffn_w_up=m_ffn_w_up, ffn_conv_w=m_ffn_conv_w, ffn_conv_b=m_ffn_conv_b, ffn_w_down=m_ffn_w_down, norm_final_w=m_norm_final_w)
    v_in = dict(meta_tokens=v_meta_tokens, w_in=v_w_in, fgt_bias=v_fgt_bias, gdn_conv_w=v_gdn_conv_w, gdn_a_log=v_gdn_a_log, gdn_dt_bias=v_gdn_dt_bias, gdn_norm_w=v_gdn_norm_w, gate_bias=v_gate_bias, w_branch_fox=v_w_branch_fox, w_branch_gdn=v_w_branch_gdn, w_out=v_w_out, norm_mix_w=v_norm_mix_w, norm_ffn_w=v_norm_ffn_w, ffn_w_up=v_ffn_w_up, ffn_conv_w=v_ffn_conv_w, ffn_conv_b=v_ffn_conv_b, ffn_w_down=v_ffn_w_down, norm_final_w=v_norm_final_w)
    shard2d = {n: _shard_shape(s, a) for n, s, a in WEIGHTS}
    as2d = lambda d: {n: d[n].reshape(shard2d[n]) for n, _, _ in WEIGHTS}
    w2, m2, v2 = as2d(weights), as2d(m_in), as2d(v_in)

    full = _gather_weights(w2)
    w_fox, w_rest = _split_w_in(full["w_in"])
    flat = lambda n: w2[n].reshape(-1)
    local_w = dict(
        meta=full["meta_tokens"], w_fox=w_fox, w_rest=w_rest, fgt_bias=flat("fgt_bias"), gdn_conv=full["gdn_conv_w"],
        a_log=flat("gdn_a_log"), dt_bias=flat("gdn_dt_bias"), gdn_norm=flat("gdn_norm_w"), gate_bias=flat("gate_bias"),
        norm_mix=flat("norm_mix_w"), norm_ffn=flat("norm_ffn_w"), ffn_conv=full["ffn_conv_w"], ffn_conv_b=flat("ffn_conv_b"),
        norm_final=flat("norm_final_w"))

    loss, grad_x, grads = _local_step(x[0], loss_target[0], local_w, {n: w2[n].astype(jnp.bfloat16) for n in BIG_LATE})

    g2 = {n: grads[n].reshape(s) for n, s, _ in WEIGHTS}
    small_send = jnp.stack([_pack([_shard_of(g2[n], AXIS[n], j) for n in SMALL], F32) for j in range(N_CHIPS)])
    recv = list(grads["late_recv"]) + list(_scatter_chips([small_send], "scatter_grads"))
    parts = [_sum_chips(r, "sum_" + n) for r, n in zip(recv, BIG + ("small",))]
    others = _sibling_swap(parts, "swap_grads")
    slab = lambda d: _pack([d[n] for n in SMALL], F32)
    state = [(w2[n], m2[n], v2[n]) for n in BIG] + [(slab(w2), slab(m2), slab(v2))]
    outs = [_adamw(w, m, v, p, q, "adamw_" + n) for (w, m, v), p, q, n in zip(state, parts, others, BIG + ("small",))]
    small = [_unpack(o, [weights[n].shape for n in SMALL]) for o in outs[-1]]
    result = []
    for kind in range(4):
        by_name = {n: outs[i][kind].reshape(weights[n].shape) for i, n in enumerate(BIG)}
        by_name.update(zip(SMALL, small[kind]))
        result += [by_name[n] for n, _, _ in WEIGHTS]
    total = lax.psum(loss[0, 0], ("x", "y", "c"))
    return (total, grad_x[None], *result)
```

```python
import functools

import numpy as np
import jax
import jax.numpy as jnp
from jax import lax
from jax.experimental import pallas as pl
from jax.experimental.pallas import tpu as pltpu

F32 = jnp.float32
_BF = jnp.bfloat16
HI = lax.Precision.HIGHEST

D = 1024
N_META = 16
CH = 64
FRONT = CH - N_META
X0 = CH
HEADS = 8
HD = 64
FW = HEADS * HD
DFF = 2816
EPS = 1e-6
NEG = -1e30
T_ALIGN = 256
SMALL_W = 128
REST_W = 3 * FW + FW + 2 * D + SMALL_W
OFF_Z = 3 * FW
OFF_GATES = 4 * FW
OFF_SMALL = 4 * FW + 2 * D

LR, B1, B2, AEPS, WD, STEP = 0.001, 0.9, 0.999, 1e-08, 0.01, 10

VMEM_LIMIT = 56 * 1024 * 1024
ROW_TILE = 640
MM_TM, MM_TN, MM_TK = 1280, 512, 2816
ATT_TILE = 640


def _div(n, target, mult=128):
    if n <= target:
        return n
    best = None
    for d in range(mult, target + 1, mult):
        if n % d == 0:
            best = d
    assert best is not None, (n, target, mult)
    return best


def _cp(*sem):
    return pltpu.CompilerParams(dimension_semantics=sem, vmem_limit_bytes=VMEM_LIMIT)


def _sig(x):
    return 0.5 * jnp.tanh(0.5 * x) + 0.5


MM_VMEM_BUDGET = 40 * 1024 * 1024
MM_STEP_BYTES = 1 << 20


def _mm_tiles(m, n, k, sa, sb, so, has_resid):
    divs = lambda d, cap: [t for t in range(128, min(d, cap) + 1, 128) if d % t == 0] or [d]
    best = None
    for tm in divs(m, MM_TM * 2):
        for tn in divs(n, 4608):
            fixed = tm * tn * (4 + 2 * so + (8 if has_resid else 0))
            tks = [t for t in divs(k, MM_TK) if 2 * t * (tm * sa + tn * sb) + fixed <= MM_VMEM_BUDGET]
            if not tks:
                continue
            tk = tks[-1]
            steps = (m // tm) * (n // tn) * (k // tk)
            rmw = (k // tk - 1) * m * n * 4
            cost = (n // tn) * m * k * sa + (m // tm) * k * n * sb + steps * MM_STEP_BYTES + rmw
            if best is None or cost < best[0]:
                best = (cost, tm, tn, tk)
    assert best is not None, (m, n, k)
    return best[1:]


def _mm_t(a, b, name):
    return _mm(a, b, ta=True, name=name)


def _mm(a, b, *, ta=False, tb=False, out_dtype=F32, resid=None, name, scatter=()):
    K, M = a.shape if ta else a.shape[::-1]
    N = b.shape[0] if tb else b.shape[1]
    assert (b.shape[1] if tb else b.shape[0]) == K
    tm, tn, tk = _mm_tiles(M, N, K, a.dtype.itemsize, b.dtype.itemsize, jnp.dtype(out_dtype).itemsize, resid is not None)
    nk = K // tk
    grid = (M // tm, N // tn, nk)
    dims = (((0 if ta else 1,), (1 if tb else 0,)), ((), ()))
    mxu = _BF
    nr = 1 if resid is not None else 0
    ng = len(scatter)

    def body(*refs):
        a_ref, b_ref = refs[:2]
        r_ref = refs[2] if nr else None
        o_ref = refs[2 + nr + ng]
        acc = refs[3 + nr + 2 * ng]
        comm = (refs[2 + nr:2 + nr + ng], refs[3 + nr + ng:3 + nr + 2 * ng], refs[4 + nr + 2 * ng:])
        k = pl.program_id(2)
        if ng:
            step = (pl.program_id(0) * grid[1] + pl.program_id(1)) * grid[2] + k
            pl.when(step == 0)(functools.partial(_scatter_phase, *comm, True))
        part = lax.dot_general(a_ref[...].astype(mxu), b_ref[...].astype(mxu), dims, preferred_element_type=F32)

        @pl.when(k == 0)
        def _():
            acc[...] = part

        @pl.when(k > 0)
        def _():
            acc[...] += part

        @pl.when(k == nk - 1)
        def _():
            r = acc[...]
            if r_ref is not None:
                r = r + r_ref[...]
            o_ref[...] = r.astype(o_ref.dtype)

        if ng:
            pl.when(step == grid[0] * grid[1] * grid[2] - 1)(functools.partial(_scatter_phase, *comm, False))

    a_spec = pl.BlockSpec((tk, tm), lambda i, j, k: (k, i)) if ta else pl.BlockSpec((tm, tk), lambda i, j, k: (i, k))
    b_spec = pl.BlockSpec((tn, tk), lambda i, j, k: (j, k)) if tb else pl.BlockSpec((tk, tn), lambda i, j, k: (k, j))
    o_spec = pl.BlockSpec((tm, tn), lambda i, j, k: (i, j))
    in_specs = [a_spec, b_spec] + ([o_spec] if nr else []) + [HBM_SPEC] * ng
    args = (a, b) + ((resid,) if nr else ()) + tuple(scatter)
    out = jax.ShapeDtypeStruct((M, N), out_dtype)
    sem = ("arbitrary",) * 3 if ng else ("parallel", "parallel", "arbitrary")
    return pl.pallas_call(
        body, name=name, grid=grid, in_specs=in_specs, out_specs=[o_spec] + [HBM_SPEC] * ng if ng else o_spec,
        out_shape=[out] + _scatter_shapes(scatter) if ng else out,
        scratch_shapes=[pltpu.VMEM((tm, tn), F32)] + (_scatter_sems(ng) if ng else []),
        compiler_params=_cp(*sem))(*args)


def _rms_fwd(h, w, name):
    T = h.shape[0]
    tb = _div(T, ROW_TILE, 8)

    def body(h_ref, w_ref, o_ref):
        x = h_ref[...]
        r = lax.rsqrt(jnp.mean(x * x, axis=-1, keepdims=True) + EPS)
        o_ref[...] = (x * r * w_ref[...]).astype(o_ref.dtype)

    return pl.pallas_call(
        body, name=name, grid=(T // tb,),
        in_specs=[pl.BlockSpec((tb, D), lambda i: (i, 0)), pl.BlockSpec((1, D), lambda i: (0, 0))],
        out_specs=pl.BlockSpec((tb, D), lambda i: (i, 0)), out_shape=jax.ShapeDtypeStruct((T, D), _BF),
        compiler_params=_cp("parallel"))(h, w)


def _rms_bwd(h, w, dy, dres, name):
    T = h.shape[0]
    tb = _div(T, ROW_TILE, 8)

    def body(h_ref, w_ref, dy_ref, dr_ref, dh_ref, dw_ref):
        x = h_ref[...]
        r = lax.rsqrt(jnp.mean(x * x, axis=-1, keepdims=True) + EPS)
        xh = x * r
        dy = dy_ref[...]
        g = dy * w_ref[...]
        dh_ref[...] = dr_ref[...] + r * (g - xh * jnp.mean(xh * g, axis=-1, keepdims=True))
        part = jnp.sum(dy * xh, axis=0, keepdims=True)

        @pl.when(pl.program_id(0) == 0)
        def _():
            dw_ref[...] = part

        @pl.when(pl.program_id(0) > 0)
        def _():
            dw_ref[...] += part

    row = pl.BlockSpec((tb, D), lambda i: (i, 0))
    one = pl.BlockSpec((1, D), lambda i: (0, 0))
    return pl.pallas_call(
        body, name=name, grid=(T // tb,), in_specs=[row, one, row, row], out_specs=[row, one],
        out_shape=[jax.ShapeDtypeStruct((T, D), F32), jax.ShapeDtypeStruct((1, D), F32)],
        compiler_params=_cp("arbitrary"))(h, w, dy, dres)


def _fgate_fwd(small, bias):
    T = small.shape[0]
    tb = _div(T, ROW_TILE, 8)

    def body(s_ref, b_ref, c_ref, carry):
        @pl.when(pl.program_id(0) == 0)
        def _():
            carry[...] = jnp.zeros_like(carry)

        lf = jax.nn.log_sigmoid(s_ref[...] + b_ref[...])
        r = lax.broadcasted_iota(jnp.int32, (tb, tb), 0)
        c = lax.broadcasted_iota(jnp.int32, (tb, tb), 1)
        tri = (r >= c).astype(F32)
        cs = jnp.dot(tri, lf, precision=HI, preferred_element_type=F32) + carry[...]
        c_ref[...] = cs
        carry[...] = cs[tb - 1:tb, :]

    return pl.pallas_call(
        body, name="fgate_fwd", grid=(T // tb,),
        in_specs=[pl.BlockSpec((tb, SMALL_W), lambda i: (i, 0)), pl.BlockSpec((1, SMALL_W), lambda i: (0, 0))],
        out_specs=pl.BlockSpec((tb, SMALL_W), lambda i: (i, 0)), out_shape=jax.ShapeDtypeStruct((T, SMALL_W), F32),
        scratch_shapes=[pltpu.VMEM((1, SMALL_W), F32)], compiler_params=_cp("arbitrary"))(small, bias)


def _fgate_bwd(small, bias, dc):
    T = small.shape[0]
    tb = _div(T, ROW_TILE, 8)
    nb = T // tb

    def body(s_ref, b_ref, dc_ref, df_ref, db_ref, carry):
        @pl.when(pl.program_id(0) == 0)
        def _():
            carry[...] = jnp.zeros_like(carry)

        r = lax.broadcasted_iota(jnp.int32, (tb, tb), 0)
        c = lax.broadcasted_iota(jnp.int32, (tb, tb), 1)
        tri = (r <= c).astype(F32)
        dlf = jnp.dot(tri, dc_ref[...], precision=HI, preferred_element_type=F32) + carry[...]
        carry[...] = dlf[0:1, :]
        df = dlf * _sig(-(s_ref[...] + b_ref[...]))
        df_ref[...] = df
        part = jnp.sum(df, axis=0, keepdims=True)

        @pl.when(pl.program_id(0) == 0)
        def _():
            db_ref[...] = part

        @pl.when(pl.program_id(0) > 0)
        def _():
            db_ref[...] += part

    rev = pl.BlockSpec((tb, SMALL_W), lambda i: (nb - 1 - i, 0))
    one = pl.BlockSpec((1, SMALL_W), lambda i: (0, 0))
    return pl.pallas_call(
        body, name="fgate_bwd", grid=(nb,), in_specs=[rev, one, rev], out_specs=[rev, one],
        out_shape=[jax.ShapeDtypeStruct((T, SMALL_W), F32), jax.ShapeDtypeStruct((1, SMALL_W), F32)],
        scratch_shapes=[pltpu.VMEM((1, SMALL_W), F32)], compiler_params=_cp("arbitrary"))(small, bias, dc)


def _att_rows(a8, tb):
    T = a8.shape[0]
    return a8.T.reshape(HEADS // 2, 2, T // tb, tb).transpose(0, 2, 1, 3)


def _att_cols(a8):
    return jnp.repeat(a8, HD, axis=1)


def _pair_cols(tb):
    return pl.BlockSpec((1, tb, 2), lambda p, i, *_: (p, i, 0))


def _from_pairs(a):
    return a.transpose(1, 0, 2).reshape(a.shape[1], HEADS)


EXP_ZERO = -104.0
SKIP_SLACK = 2.0
NORM_SLACK = 1.02


def _att_stats(qkv, c_cols, ones_blk):
    T = qkv.shape[0]
    tb = _div(T, ATT_TILE)

    def body(q_ref, k_ref, c_ref, e_ref, o_ref):
        q = q_ref[...].astype(F32)
        k = k_ref[...].astype(F32)
        e = e_ref[...].astype(_BF)
        qn = jnp.max(jnp.dot((q * q).astype(_BF), e, preferred_element_type=F32), axis=0, keepdims=True)
        kn = jnp.max(jnp.dot((k * k).astype(_BF), e, preferred_element_type=F32), axis=0, keepdims=True)
        c = c_ref[...]
        o_ref[0] = jnp.concatenate([jnp.sqrt(qn), jnp.sqrt(kn), jnp.max(c, axis=0, keepdims=True),
                                    jnp.min(c, axis=0, keepdims=True), jnp.zeros((4, FW), F32)], axis=0)

    blk = lambda off: pl.BlockSpec((tb, FW), lambda i: (i, off))
    return pl.pallas_call(
        body, name="att_stats", grid=(T // tb,),
        in_specs=[blk(0), blk(1), blk(0), pl.BlockSpec((FW, FW), lambda i: (0, 0))],
        out_specs=pl.BlockSpec((1, 8, FW), lambda i: (i, 0, 0)), out_shape=jax.ShapeDtypeStruct((T // tb, 8, FW), F32),
        compiler_params=_cp("parallel"))(qkv, qkv, c_cols, ones_blk)


def _att_plan(stats):
    nb = stats.shape[0]
    st = stats[:, :4, ::HD]
    qmax, kmax, cmax, cmin = (st[:, r, :].T for r in range(4))
    bound = (HD ** -0.5) * NORM_SLACK * qmax[:, :, None] * (kmax[:, None, :] + kmax[:, :, None]) \
        + cmax[:, :, None] - cmin[:, None, :] + SKIP_SLACK
    ii = lax.broadcasted_iota(jnp.int32, (nb, nb), 0)
    jj = lax.broadcasted_iota(jnp.int32, (nb, nb), 1)
    skip = (bound < EXP_ZERO) & (jj < ii)[None]
    live = (~skip & (jj <= ii)[None]).reshape(HEADS // 2, 2, nb, nb).any(axis=1)
    jfirst = jnp.argmax(live, axis=2).astype(jnp.int32)
    ilast = (nb - 1 - jnp.argmax(live[:, ::-1, :], axis=1)).astype(jnp.int32)
    return skip.astype(jnp.int32).reshape(-1), jfirst.reshape(-1), ilast.reshape(-1)


def _fox_fwd(qkv, c_cols, c_rows, plan):
    T = qkv.shape[0]
    tb = _div(T, ATT_TILE)
    nb = T // tb
    npair = HEADS // 2
    scale = HD ** -0.5
    dn = (((1,), (1,)), ((), ()))

    def body(skip_ref, jfirst_ref, ilast_ref, q_ref, k_ref, v_ref, cq_ref, ck_ref, o_ref, l2_ref, m_scr, acc_scr):
        pr = pl.program_id(0)
        i = pl.program_id(1)
        q = q_ref[...]
        lane = lax.broadcasted_iota(jnp.int32, (1, 2 * HD), 1)
        sel0 = lane < HD
        zero = jnp.zeros_like(q)
        qh = (jnp.where(sel0, q, zero) * scale, jnp.where(sel0, zero, q) * scale)
        cq = (cq_ref[:, 0:1], cq_ref[:, HD:HD + 1])
        row = i * tb + lax.broadcasted_iota(jnp.int32, (tb, 1), 0)
        m_scr[...] = jnp.full(m_scr.shape, NEG, F32)
        acc_scr[...] = jnp.zeros(acc_scr.shape, F32)

        def tile(j, h, masked):
            start = pl.multiple_of(j * tb, tb)
            kb = k_ref[pl.ds(start, tb), :]
            vb = v_ref[pl.ds(start, tb), :]
            one = jnp.ones_like(vb)
            vh = jnp.where(sel0, vb, one) if h == 0 else jnp.where(sel0, one, vb)
            t = lax.dot_general(qh[h], kb, dn, preferred_element_type=F32) - ck_ref[0, j, h:h + 1, :]
            if masked:
                col = j * tb + lax.broadcasted_iota(jnp.int32, (1, tb), 1)
                mask = (col <= row) & (col >= FRONT)
                t = jnp.where(mask, t, NEG)
            m = m_scr[h]
            m_new = jnp.maximum(m, jnp.max(t, axis=-1, keepdims=True) + cq[h])
            p = jnp.exp(t + (cq[h] - m_new))
            if masked:
                p = jnp.where(mask, p, 0.0)
            m_scr[h] = m_new
            acc_scr[h] = acc_scr[h] * jnp.exp(m - m_new) + jnp.dot(p.astype(vb.dtype), vh, preferred_element_type=F32)

        def step(j, _):
            edge = (j == 0) | (j == i)
            for h in range(2):
                live = skip_ref[((2 * pr + h) * nb + i) * nb + j] == 0
                pl.when(live & edge)(functools.partial(tile, j, h, True))
                pl.when(live & jnp.logical_not(edge))(functools.partial(tile, j, h, False))
            return 0

        lax.fori_loop(jfirst_ref[pr * nb + i], i + 1, step, 0)
        outs, lses = [], []
        for h in range(2):
            l = acc_scr[h][:, (1 - h) * HD:(1 - h) * HD + 1]
            ok = l > 0.0
            ls = jnp.where(ok, l, 1.0)
            outs.append(jnp.where(ok, acc_scr[h] / ls, 0.0))
            lses.append(jnp.where(ok, m_scr[h] + jnp.log(ls), 0.0))
        o_ref[...] = jnp.where(sel0, outs[0], outs[1])
        l2_ref[0, :, 0:1] = lses[0]
        l2_ref[0, :, 1:2] = lses[1]

    blk = lambda off: pl.BlockSpec((tb, 2 * HD), lambda p, i, *_: (i, off + p))
    full = lambda off: pl.BlockSpec((T, 2 * HD), lambda p, i, *_: (0, off + p))
    rows = pl.BlockSpec((1, nb, 2, tb), lambda p, i, *_: (p, 0, 0, 0))
    return pl.pallas_call(
        body, name="fox_fwd",
        grid_spec=pltpu.PrefetchScalarGridSpec(
            num_scalar_prefetch=3, grid=(npair, nb),
            in_specs=[blk(0), full(npair), full(2 * npair), blk(0), rows], out_specs=[blk(0), _pair_cols(tb)],
            scratch_shapes=[pltpu.VMEM((2, tb, 1), F32), pltpu.VMEM((2, tb, 2 * HD), F32)]),
        out_shape=[jax.ShapeDtypeStruct((T, FW), F32), jax.ShapeDtypeStruct((npair, T, 2), F32)],
        compiler_params=_cp("parallel", "arbitrary"))(*plan, qkv, qkv, qkv, c_cols, c_rows)


def _fox_bwd_dq(qkv, do, c_cols, c_rows, lse, o, plan):
    T = qkv.shape[0]
    tb = _div(T, ATT_TILE)
    nb = T // tb
    npair = HEADS // 2
    scale = HD ** -0.5
    dn = (((1,), (1,)), ((), ()))

    def body(skip_ref, jfirst_ref, ilast_ref, q_ref, k_ref, v_ref, do_ref, cq_ref, ck_ref, l_ref, o_ref, dq_ref, dc_ref,
             d2_ref, dq_scr):
        pr = pl.program_id(0)
        i = pl.program_id(1)
        q = q_ref[...]
        do = do_ref[...]
        lane = lax.broadcasted_iota(jnp.int32, (1, 2 * HD), 1)
        sel0 = lane < HD
        qh = (jnp.where(sel0, q, jnp.zeros_like(q)) * scale, jnp.where(sel0, jnp.zeros_like(q), q) * scale)
        doh = (jnp.where(sel0, do, jnp.zeros_like(do)), jnp.where(sel0, jnp.zeros_like(do), do))
        ce = (cq_ref[:, 0:1] - l_ref[0, :, 0:1], cq_ref[:, HD:HD + 1] - l_ref[0, :, 1:2])
        prod = do.astype(F32) * o_ref[...]
        dl = (jnp.sum(jnp.where(sel0, prod, 0.0), axis=-1, keepdims=True),
              jnp.sum(jnp.where(sel0, 0.0, prod), axis=-1, keepdims=True))
        d2_ref[0, :, 0:1] = dl[0]
        d2_ref[0, :, 1:2] = dl[1]
        row = i * tb + lax.broadcasted_iota(jnp.int32, (tb, 1), 0)
        dq_scr[...] = jnp.zeros(dq_scr.shape, F32)

        def tile(j, h, masked):
            start = pl.multiple_of(j * tb, tb)
            kb = k_ref[pl.ds(start, tb), :]
            vb = v_ref[pl.ds(start, tb), :]
            one = jnp.ones_like(kb)
            kh = jnp.where(sel0, kb, one) if h == 0 else jnp.where(sel0, one, kb)
            t = lax.dot_general(qh[h], kb, dn, preferred_element_type=F32) - ck_ref[0, j, h:h + 1, :]
            if masked:
                col = j * tb + lax.broadcasted_iota(jnp.int32, (1, tb), 1)
                mask = (col <= row) & (col >= FRONT)
                p = jnp.where(mask, jnp.exp(jnp.where(mask, t, NEG) + ce[h]), 0.0)
            else:
                p = jnp.exp(t + ce[h])
            dp = lax.dot_general(doh[h], vb, dn, preferred_element_type=F32)
            ds = p * (dp - dl[h])
            dq_scr[h] += jnp.dot(ds.astype(kb.dtype), kh, preferred_element_type=F32)

        def step(j, _):
            edge = (j == 0) | (j == i)
            for h in range(2):
                live = skip_ref[((2 * pr + h) * nb + i) * nb + j] == 0
                pl.when(live & edge)(functools.partial(tile, j, h, True))
                pl.when(live & jnp.logical_not(edge))(functools.partial(tile, j, h, False))
            return 0

        lax.fori_loop(jfirst_ref[pr * nb + i], i + 1, step, 0)
        dq_ref[...] = (jnp.where(sel0, dq_scr[0], dq_scr[1]) * scale).astype(dq_ref.dtype)
        dc_ref[0, :, 0:1] = dq_scr[0][:, HD:HD + 1]
        dc_ref[0, :, 1:2] = dq_scr[1][:, 0:1]

    blk = lambda off: pl.BlockSpec((tb, 2 * HD), lambda p, i, *_: (i, off + p))
    full = lambda off: pl.BlockSpec((T, 2 * HD), lambda p, i, *_: (0, off + p))
    rows = pl.BlockSpec((1, nb, 2, tb), lambda p, i, *_: (p, 0, 0, 0))
    return pl.pallas_call(
        body, name="fox_bwd_dq",
        grid_spec=pltpu.PrefetchScalarGridSpec(
            num_scalar_prefetch=3, grid=(npair, nb),
            in_specs=[blk(0), full(npair), full(2 * npair), blk(0), blk(0), rows, _pair_cols(tb), blk(0)],
            out_specs=[blk(0), _pair_cols(tb), _pair_cols(tb)],
            scratch_shapes=[pltpu.VMEM((2, tb, 2 * HD), F32)]),
        out_shape=[jax.ShapeDtypeStruct((T, FW), _BF)] + [jax.ShapeDtypeStruct((npair, T, 2), F32)] * 2,
        compiler_params=_cp("parallel", "arbitrary"))(*plan, qkv, qkv, qkv, do, c_cols, c_rows, lse, o)


def _fox_bwd_dkv(qkv, do, c_cols, ce_rows, delta_rows, plan):
    T = qkv.shape[0]
    tb = _div(T, ATT_TILE)
    nb = T // tb
    npair = HEADS // 2
    scale = HD ** -0.5
    dn = (((1,), (1,)), ((), ()))

    def body(skip_ref, jfirst_ref, ilast_ref, q_ref, k_ref, v_ref, do_ref, ck_ref, ce_ref, dl_ref,
             dk_ref, dv_ref, dc_ref, dk_scr, dv_scr):
        pr = pl.program_id(0)
        jb = pl.program_id(1)
        k = k_ref[...]
        v = v_ref[...]
        lane = lax.broadcasted_iota(jnp.int32, (1, 2 * HD), 1)
        sel0 = lane < HD
        kh = (jnp.where(sel0, k, jnp.zeros_like(k)) * scale, jnp.where(sel0, jnp.zeros_like(k), k) * scale)
        vh = (jnp.where(sel0, v, jnp.zeros_like(v)), jnp.where(sel0, jnp.zeros_like(v), v))
        ck = (ck_ref[:, 0:1], ck_ref[:, HD:HD + 1])
        kidx = jb * tb + lax.broadcasted_iota(jnp.int32, (tb, 1), 0)
        dk_scr[...] = jnp.zeros(dk_scr.shape, F32)
        dv_scr[...] = jnp.zeros(dv_scr.shape, F32)

        def tile(i, h, masked):
            start = pl.multiple_of(i * tb, tb)
            qb = q_ref[pl.ds(start, tb), :]
            dob = do_ref[pl.ds(start, tb), :]
            one = jnp.ones_like(qb)
            qh = jnp.where(sel0, qb, one) if h == 0 else jnp.where(sel0, one, qb)
            t = lax.dot_general(kh[h], qb, dn, preferred_element_type=F32) - ck[h]
            ce = ce_ref[0, i, h:h + 1, :]
            if masked:
                qidx = i * tb + lax.broadcasted_iota(jnp.int32, (1, tb), 1)
                mask = (kidx <= qidx) & (kidx >= FRONT)
                pt = jnp.where(mask, jnp.exp(jnp.where(mask, t, NEG) + ce), 0.0)
            else:
                pt = jnp.exp(t + ce)
            dv_scr[h] += jnp.dot(pt.astype(dob.dtype), dob, preferred_element_type=F32)
            dpt = lax.dot_general(vh[h], dob, dn, preferred_element_type=F32)
            dst = pt * (dpt - dl_ref[0, i, h:h + 1, :])
            dk_scr[h] += jnp.dot(dst.astype(qb.dtype), qh, preferred_element_type=F32)

        def step(i, _):
            edge = (i == jb) | (jb == 0)
            for h in range(2):
                live = skip_ref[((2 * pr + h) * nb + i) * nb + jb] == 0
                pl.when(live & edge)(functools.partial(tile, i, h, True))
                pl.when(live & jnp.logical_not(edge))(functools.partial(tile, i, h, False))
            return 0

        lax.fori_loop(jb, ilast_ref[pr * nb + jb] + 1, step, 0)
        dk_ref[...] = (jnp.where(sel0, dk_scr[0], dk_scr[1]) * scale).astype(dk_ref.dtype)
        dv_ref[...] = jnp.where(sel0, dv_scr[0], dv_scr[1]).astype(dv_ref.dtype)
        dc_ref[0, :, 0:1] = -dk_scr[0][:, HD:HD + 1]
        dc_ref[0, :, 1:2] = -dk_scr[1][:, 0:1]

    blk = lambda off: pl.BlockSpec((tb, 2 * HD), lambda p, j, *_: (j, off + p))
    full = lambda off: pl.BlockSpec((T, 2 * HD), lambda p, j, *_: (0, off + p))
    rows = pl.BlockSpec((1, nb, 2, tb), lambda p, j, *_: (p, 0, 0, 0))
    return pl.pallas_call(
        body, name="fox_bwd_dkv",
        grid_spec=pltpu.PrefetchScalarGridSpec(
            num_scalar_prefetch=3, grid=(npair, nb),
            in_specs=[full(0), blk(npair), blk(2 * npair), full(0), blk(0), rows, rows],
            out_specs=[blk(0), blk(0), _pair_cols(tb)],
            scratch_shapes=[pltpu.VMEM((2, tb, 2 * HD), F32), pltpu.VMEM((2, tb, 2 * HD), F32)]),
        out_shape=[jax.ShapeDtypeStruct((T, FW), _BF)] * 2 + [jax.ShapeDtypeStruct((npair, T, 2), F32)],
        compiler_params=_cp("parallel", "arbitrary"))(*plan, qkv, qkv, qkv, do, c_cols, ce_rows, delta_rows)


CONV_RC = 32
CONV_TC = 256


def _rc(r):
    return pl.ds(r * CONV_RC if isinstance(r, int) else pl.multiple_of(r * CONV_RC, CONV_RC), CONV_RC)


def _prev_rows(x_ref, h_ref, first):
    def at(r):
        if isinstance(r, int):
            assert r == 0
            return jnp.concatenate([jnp.where(first, 0.0, h_ref[...]), x_ref[0:CONV_RC, :]], axis=0)
        return x_ref[pl.ds(pl.multiple_of(r * CONV_RC - 8, 8), CONV_RC + 8), :]
    return at


def _windows_prev(x, kw):
    return [x[8:] if k == kw - 1 else pltpu.roll(x, kw - 1 - k, 0)[8:] for k in range(kw)]


def _fold8(a):
    return a.reshape(CONV_RC // 8, 8, a.shape[-1]).sum(axis=0)


def _halo_prev(tb, tc, off=0):
    return pl.BlockSpec((8, tc), lambda j, i: (jnp.maximum(i * (tb // 8) - 1, 0), j + off))


def _gconv_fwd(rest, w):
    T = rest.shape[0]
    C = 3 * FW
    kw = w.shape[0]
    tb, tc = _div(T, 1280, CONV_RC), CONV_TC

    def body(x_ref, h_ref, w_ref, o_ref):
        at = _prev_rows(x_ref, h_ref, pl.program_id(1) == 0)
        wv = w_ref[...]

        def chunk(r, _):
            win = _windows_prev(at(r), kw)
            u = sum(wv[k:k + 1, :] * win[k] for k in range(kw))
            o_ref[_rc(r), :] = u * _sig(u)
            return 0

        chunk(0, 0)
        lax.fori_loop(1, tb // CONV_RC, chunk, 0)

    return pl.pallas_call(
        body, name="gconv_fwd", grid=(C // tc, T // tb),
        in_specs=[pl.BlockSpec((tb, tc), lambda j, i: (i, j)), _halo_prev(tb, tc), pl.BlockSpec((kw, tc), lambda j, i: (0, j))],
        out_specs=pl.BlockSpec((tb, tc), lambda j, i: (i, j)), out_shape=jax.ShapeDtypeStruct((T, C), F32),
        compiler_params=_cp("parallel", "arbitrary"))(rest, rest, w)


def _gconv_bwd_du(rest, w, dy):
    T = rest.shape[0]
    C = 3 * FW
    kw = w.shape[0]
    tb, tc = _div(T, 1280, CONV_RC), CONV_TC

    def body(x_ref, h_ref, w_ref, dy_ref, du_ref, dw_ref):
        i = pl.program_id(1)
        at = _prev_rows(x_ref, h_ref, i == 0)
        wv = w_ref[...]

        def chunk(r, acc):
            rows = _rc(r)
            win = _windows_prev(at(r), kw)
            u = sum(wv[k:k + 1, :] * win[k] for k in range(kw))
            sg = _sig(u)
            du = dy_ref[rows, :] * sg * (1.0 + u * (1.0 - sg))
            du_ref[rows, :] = du
            return tuple(acc[k] + _fold8(du * win[k]) for k in range(kw))

        acc = chunk(0, tuple(jnp.zeros((8, tc), F32) for _ in range(kw)))
        acc = lax.fori_loop(1, tb // CONV_RC, chunk, acc)
        part = jnp.concatenate([jnp.sum(a, axis=0, keepdims=True) for a in acc], axis=0)

        @pl.when(i == 0)
        def _():
            dw_ref[...] = part

        @pl.when(i > 0)
        def _():
            dw_ref[...] += part

    blk = pl.BlockSpec((tb, tc), lambda j, i: (i, j))
    wsp = pl.BlockSpec((kw, tc), lambda j, i: (0, j))
    return pl.pallas_call(
        body, name="gconv_bwd_du", grid=(C // tc, T // tb), in_specs=[blk, _halo_prev(tb, tc), wsp, blk],
        out_specs=[blk, wsp], out_shape=[jax.ShapeDtypeStruct((T, C), F32), jax.ShapeDtypeStruct((kw, C), F32)],
        compiler_params=_cp("parallel", "arbitrary"))(rest, rest, w, dy)


def _conv_bwd_dx(du, w, out_dtype, name):
    T, C = du.shape
    kw = w.shape[0]
    tb = _div(T, 1280, CONV_RC)
    tc = CONV_TC
    nb = T // tb

    last = tb // CONV_RC - 1

    def body(x_ref, h_ref, w_ref, o_ref):
        wv = w_ref[...]

        def chunk(r, _):
            if isinstance(r, int):
                tail = jnp.where(pl.program_id(1) == nb - 1, 0.0, h_ref[...])
                x = jnp.concatenate([x_ref[r * CONV_RC:(r + 1) * CONV_RC, :], tail], axis=0)
            else:
                x = x_ref[pl.ds(pl.multiple_of(r * CONV_RC, CONV_RC), CONV_RC + 8), :]
            acc = wv[kw - 1:kw, :] * x[:CONV_RC]
            for k in range(kw - 1):
                acc = acc + wv[k:k + 1, :] * pltpu.roll(x, CONV_RC + 8 - (kw - 1 - k), 0)[:CONV_RC]
            o_ref[_rc(r), :] = acc.astype(o_ref.dtype)
            return 0

        lax.fori_loop(0, last, chunk, 0)
        chunk(last, 0)

    halo = pl.BlockSpec((8, tc), lambda j, i: (jnp.minimum((i + 1) * (tb // 8), T // 8 - 1), j))
    return pl.pallas_call(
        body, name=name, grid=(C // tc, nb),
        in_specs=[pl.BlockSpec((tb, tc), lambda j, i: (i, j)), halo, pl.BlockSpec((kw, tc), lambda j, i: (0, j))],
        out_specs=pl.BlockSpec((tb, tc), lambda j, i: (i, j)), out_shape=jax.ShapeDtypeStruct((T, C), out_dtype),
        compiler_params=_cp("parallel", "arbitrary"))(du, du, w)


def _glu_fwd(up, w, b):
    T = up.shape[0]
    kw = w.shape[0]
    tb, tc = _div(T, 1280, CONV_RC), CONV_TC
    nc = DFF // tc

    def body(xg, hg, xu, hu, wg, wu, bg, bu, o_ref):
        first = pl.program_id(1) == 0
        atg, atu = _prev_rows(xg, hg, first), _prev_rows(xu, hu, first)
        wgv, wuv, bgv, buv = wg[...], wu[...], bg[...], bu[...]

        def chunk(r, _):
            wing, winu = _windows_prev(atg(r), kw), _windows_prev(atu(r), kw)
            ug = bgv + sum(wgv[k:k + 1, :] * wing[k] for k in range(kw))
            uu = buv + sum(wuv[k:k + 1, :] * winu[k] for k in range(kw))
            o_ref[_rc(r), :] = (ug * _sig(ug) * uu).astype(o_ref.dtype)
            return 0

        chunk(0, 0)
        lax.fori_loop(1, tb // CONV_RC, chunk, 0)

    blk = lambda off: pl.BlockSpec((tb, tc), lambda j, i: (i, j + off))
    wsp = lambda off: pl.BlockSpec((kw, tc), lambda j, i: (0, j + off))
    bsp = lambda off: pl.BlockSpec((1, tc), lambda j, i: (0, j + off))
    return pl.pallas_call(
        body, name="glu_fwd", grid=(nc, T // tb),
        in_specs=[blk(0), _halo_prev(tb, tc), blk(nc), _halo_prev(tb, tc, nc), wsp(0), wsp(nc), bsp(0), bsp(nc)],
        out_specs=blk(0), out_shape=jax.ShapeDtypeStruct((T, DFF), _BF),
        compiler_params=_cp("parallel", "arbitrary"))(up, up, up, up, w, w, b, b)


def _glu_bwd_du(up, w, b, df):
    T = up.shape[0]
    kw = w.shape[0]
    tb, tc = _div(T, 1280, CONV_RC), CONV_TC
    nc = DFF // tc

    def body(xg, hg, xu, hu, wg, wu, bg, bu, df_ref, dug_ref, duu_ref, dwg_ref, dwu_ref, dbg_ref, dbu_ref):
        i = pl.program_id(1)
        atg, atu = _prev_rows(xg, hg, i == 0), _prev_rows(xu, hu, i == 0)
        wgv, wuv, bgv, buv = wg[...], wu[...], bg[...], bu[...]

        def chunk(r, acc):
            rows = _rc(r)
            wing, winu = _windows_prev(atg(r), kw), _windows_prev(atu(r), kw)
            ug = bgv + sum(wgv[k:k + 1, :] * wing[k] for k in range(kw))
            uu = buv + sum(wuv[k:k + 1, :] * winu[k] for k in range(kw))
            s = _sig(ug)
            df = df_ref[rows, :]
            dug = df * uu * s * (1.0 + ug * (1.0 - s))
            duu = df * ug * s
            dug_ref[rows, :] = dug
            duu_ref[rows, :] = duu
            new = [dug * wing[k] for k in range(kw)] + [duu * winu[k] for k in range(kw)] + [dug, duu]
            return tuple(a + _fold8(v) for a, v in zip(acc, new))

        acc = chunk(0, tuple(jnp.zeros((8, tc), F32) for _ in range(2 * kw + 2)))
        acc = lax.fori_loop(1, tb // CONV_RC, chunk, acc)
        col = [jnp.sum(a, axis=0, keepdims=True) for a in acc]
        parts = (jnp.concatenate(col[:kw], axis=0), jnp.concatenate(col[kw:2 * kw], axis=0), col[2 * kw], col[2 * kw + 1])
        accs = (dwg_ref, dwu_ref, dbg_ref, dbu_ref)

        @pl.when(i == 0)
        def _():
            for r, p in zip(accs, parts):
                r[...] = p

        @pl.when(i > 0)
        def _():
            for r, p in zip(accs, parts):
                r[...] += p

    blk = lambda off: pl.BlockSpec((tb, tc), lambda j, i: (i, j + off))
    wsp = lambda off: pl.BlockSpec((kw, tc), lambda j, i: (0, j + off))
    bsp = lambda off: pl.BlockSpec((1, tc), lambda j, i: (0, j + off))
    return pl.pallas_call(
        body, name="glu_bwd_du", grid=(nc, T // tb),
        in_specs=[blk(0), _halo_prev(tb, tc), blk(nc), _halo_prev(tb, tc, nc), wsp(0), wsp(nc), bsp(0), bsp(nc), blk(0)],
        out_specs=[blk(0), blk(0), wsp(0), wsp(0), bsp(0), bsp(0)],
        out_shape=[jax.ShapeDtypeStruct((T, DFF), F32)] * 2 + [jax.ShapeDtypeStruct((kw, DFF), F32)] * 2
        + [jax.ShapeDtypeStruct((1, DFF), F32)] * 2,
        compiler_params=_cp("parallel", "arbitrary"))(up, up, up, up, w, w, b, b, df)


def _mix_fwd(rest, gate_bias, y_fox, y_gdn):
    T = rest.shape[0]
    tb, tc = _div(T, ROW_TILE, 8), 512
    nc = D // tc
    og = OFF_GATES // tc

    def body(gf, gg, bf, bg, yf, yg, o_ref):
        o_ref[...] = (_sig(gf[...] + bf[...]) * yf[...] + _sig(gg[...] + bg[...]) * yg[...]).astype(o_ref.dtype)

    blk = lambda off: pl.BlockSpec((tb, tc), lambda i, j: (i, j + off))
    bsp = lambda off: pl.BlockSpec((1, tc), lambda i, j: (0, j + off))
    return pl.pallas_call(
        body, name="mix_fwd", grid=(T // tb, nc),
        in_specs=[blk(og), blk(og + nc), bsp(0), bsp(nc), blk(0), blk(0)], out_specs=blk(0),
        out_shape=jax.ShapeDtypeStruct((T, D), _BF), compiler_params=_cp("parallel", "parallel"))(
            rest, rest, gate_bias, gate_bias, y_fox, y_gdn)


def _mix_bwd(rest, gate_bias, y_fox, y_gdn, dmix):
    T = rest.shape[0]
    tb, tc = _div(T, ROW_TILE, 8), 512
    nc = D // tc
    og = OFF_GATES // tc

    def body(gf, gg, bf, bg, yf, yg, dm, dyf, dyg, dgf, dgg, dbf, dbg):
        i = pl.program_id(1)
        d = dm[...]
        sf = _sig(gf[...] + bf[...])
        sg = _sig(gg[...] + bg[...])
        dyf[...] = (d * sf).astype(dyf.dtype)
        dyg[...] = (d * sg).astype(dyg.dtype)
        a = d * yf[...] * sf * (1.0 - sf)
        b = d * yg[...] * sg * (1.0 - sg)
        dgf[...] = a.astype(dgf.dtype)
        dgg[...] = b.astype(dgg.dtype)
        pa = jnp.sum(a, axis=0, keepdims=True)
        pb = jnp.sum(b, axis=0, keepdims=True)

        @pl.when(i == 0)
        def _():
            dbf[...] = pa
            dbg[...] = pb

        @pl.when(i > 0)
        def _():
            dbf[...] += pa
            dbg[...] += pb

    blk = lambda off: pl.BlockSpec((tb, tc), lambda j, i: (i, j + off))
    bsp = lambda off: pl.BlockSpec((1, tc), lambda j, i: (0, j + off))
    return pl.pallas_call(
        body, name="mix_bwd", grid=(nc, T // tb),
        in_specs=[blk(og), blk(og + nc), bsp(0), bsp(nc), blk(0), blk(0), blk(0)],
        out_specs=[blk(0), blk(0), blk(0), blk(0), bsp(0), bsp(0)],
        out_shape=[jax.ShapeDtypeStruct((T, D), _BF)] * 4 + [jax.ShapeDtypeStruct((1, D), F32)] * 2,
        compiler_params=_cp("parallel", "arbitrary"))(rest, rest, gate_bias, gate_bias, y_fox, y_gdn, dmix)


def _loss_head(h2, w, target, n_valid):
    T = h2.shape[0]
    tb = _div(T, ROW_TILE, 8)

    def body(h_ref, w_ref, t_ref, dh_ref, loss_ref, dw_ref):
        i = pl.program_id(0)
        x = h_ref[...]
        r = lax.rsqrt(jnp.mean(x * x, axis=-1, keepdims=True) + EPS)
        xh = x * r
        row = i * tb + lax.broadcasted_iota(jnp.int32, (tb, 1), 0)
        valid = (row >= X0) & (row < X0 + n_valid)
        e = jnp.where(valid, xh * w_ref[...] - t_ref[...], 0.0)
        dy = e * (1.0 / D)
        g = dy * w_ref[...]
        dh_ref[...] = r * (g - xh * jnp.mean(xh * g, axis=-1, keepdims=True))
        lpart = 0.5 * jnp.sum(jnp.sum(e * e, axis=-1, keepdims=True) * (1.0 / D), axis=0, keepdims=True)
        wpart = jnp.sum(dy * xh, axis=0, keepdims=True)

        @pl.when(i == 0)
        def _():
            loss_ref[...] = lpart
            dw_ref[...] = wpart

        @pl.when(i > 0)
        def _():
            loss_ref[...] += lpart
            dw_ref[...] += wpart

    row = pl.BlockSpec((tb, D), lambda i: (i, 0))
    one = pl.BlockSpec((1, D), lambda i: (0, 0))
    return pl.pallas_call(
        body, name="loss_head", grid=(T // tb,), in_specs=[row, one, row],
        out_specs=[row, pl.BlockSpec((1, 1), lambda i: (0, 0)), one],
        out_shape=[jax.ShapeDtypeStruct((T, D), F32), jax.ShapeDtypeStruct((1, 1), F32), jax.ShapeDtypeStruct((1, D), F32)],
        compiler_params=_cp("arbitrary"))(h2, w, target)


def _bd_lo(a, b, ca, cb):
    return lax.dot_general(a.astype(_BF), b.astype(_BF), (((ca,), (cb,)), ((0,), (0,))), preferred_element_type=F32)


def _split2(a):
    hi = a.astype(_BF)
    return hi, (a - hi.astype(F32)).astype(_BF)


def _bd_hi(a, b, ca, cb, exact_a=False):
    dn = (((ca,), (cb,)), ((0,), (0,)))
    dot = lambda x, y: lax.dot_general(x, y, dn, preferred_element_type=F32)
    bh, bl = _split2(b)
    if exact_a:
        ah = a.astype(_BF)
        return dot(ah, bh) + dot(ah, bl)
    ah, al = _split2(a)
    return dot(ah, bh) + (dot(ah, bl) + dot(al, bh))


def _row_to_col(x):
    eye = lax.broadcasted_iota(jnp.int32, (1, CH, CH), 1) == lax.broadcasted_iota(jnp.int32, (1, CH, CH), 2)
    return jnp.sum(jnp.where(eye, jnp.broadcast_to(x, (HEADS, CH, CH)), 0.0), axis=2, keepdims=True)


def _col_to_row(x):
    eye = lax.broadcasted_iota(jnp.int32, (1, CH, CH), 1) == lax.broadcasted_iota(jnp.int32, (1, CH, CH), 2)
    return jnp.sum(jnp.where(eye, jnp.broadcast_to(x, (HEADS, CH, CH)), 0.0), axis=1, keepdims=True)


def _gdn_chunk(q, k, v, bpre, apre, alog, dtb):
    H = HEADS
    r = lax.broadcasted_iota(jnp.int32, (1, CH, CH), 1)
    c = lax.broadcasted_iota(jnp.int32, (1, CH, CH), 2)
    tril, strict = r >= c, r > c
    lb = jnp.broadcast_to(tril.astype(F32), (H, CH, CH))
    rq = lax.rsqrt(jnp.sum(q * q, axis=-1, keepdims=True) + EPS)
    rk = lax.rsqrt(jnp.sum(k * k, axis=-1, keepdims=True) + EPS)
    qh = q * rq
    qn = qh * (HD ** -0.5)
    kn = k * rk
    beta = _sig(bpre)
    x = apre + dtb
    ea = jnp.exp(alog)
    g = -ea * jax.nn.softplus(x)
    gb = jnp.broadcast_to(g, (H, CH, HD))
    gc = _bd_hi(lb, gb, 2, 1, True)
    dm = gc - _col_to_row(gc[:, :, 0:1])
    decay = jnp.where(tril, jnp.exp(jnp.where(tril, dm, 0.0)), 0.0)
    eg = jnp.exp(gc)
    gl = gc[:, CH - 1:CH, :]
    egl = jnp.exp(gl - gc)
    cd = jnp.exp(gl)
    kb = kn * beta
    vb = v * beta
    both = _bd_lo(_rows2(kb, qn), kn, 2, 2)
    kk, qk = both[:, :CH], both[:, CH:]
    pw = -jnp.where(strict, kk * decay, 0.0)
    tm = jnp.where(r == c, 1.0, 0.0) + pw
    pw = _bd_hi(pw, pw, 2, 1)
    for it in range(4):
        mul = _bd_hi if it < 2 else _bd_lo
        both = mul(_rows2(pw, tm), pw, 2, 1)
        pw, tm = both[:, :CH], tm + both[:, CH:]
    tm = tm + _bd_lo(tm, pw, 2, 1)
    kbg = kb * eg
    sol = _bd_hi(tm, _lanes2(vb, kbg), 2, 1)
    value, kcd = sol[:, :, :HD], sol[:, :, HD:]
    attn = jnp.where(tril, qk * decay, 0.0)
    return dict(tril=tril, strict=strict, lb=lb, rq=rq, rk=rk, qh=qh, qn=qn, kn=kn, beta=beta, x=x, ea=ea, g=g,
                decay=decay, eg=eg, egl=egl, cd=cd, kb=kb, vb=vb, kk=kk, tm=tm, kbg=kbg, value=value, kcd=kcd, qk=qk,
                attn=attn, qd=qn * eg, kt=kn * egl, sol=sol)


def _rows2(a, b):
    return jnp.concatenate([a, b], axis=1)


def _lanes2(a, b):
    return jnp.concatenate([a, b], axis=2)


GDN_CPS = 2
GDN_CPS_FWD = 4


def _gdn_specs(T, rev):
    G = GDN_CPS if rev else GDN_CPS_FWD
    ns = T // (G * CH)
    pos = (lambda n: ns - 1 - n) if rev else (lambda n: n)
    mat = pl.BlockSpec((HEADS, G * CH, HD), lambda n: (0, pos(n), 0))
    col = pl.BlockSpec((G, HEADS, 1, CH), lambda n: (pos(n), 0, 0, 0))
    sca = pl.BlockSpec((HEADS, 1, 1), lambda n: (0, 0, 0))
    nw = pl.BlockSpec((1, 1, HD), lambda n: (0, 0, 0))
    st = pl.BlockSpec((G, HEADS, HD, HD), lambda n: (pos(n), 0, 0, 0))
    tok = lambda width, off: pl.BlockSpec((G * CH, width), lambda n: (pos(n), off))
    return ns, mat, col, sca, nw, st, tok


def _split_heads(x):
    return [jnp.stack([x[:, (g * HEADS + h) * HD:(g * HEADS + h + 1) * HD] for h in range(HEADS)])
            for g in range(x.shape[1] // FW)]


def _store_heads(ref, rows, g, val):
    for h in range(HEADS):
        ref[rows, (g * HEADS + h) * HD:(g * HEADS + h + 1) * HD] = val[h]


def _gdn_fwd(conv, rest, bpre, apre, alog, dtb, nw, gather=()):
    T = conv.shape[0]
    ns, mat, col, sca, nws, st, tok = _gdn_specs(T, False)
    rows = [slice(g * CH, (g + 1) * CH) for g in range(GDN_CPS_FWD)]
    ng = len(gather)

    def body(*refs):
        c_ref, z_ref, b_ref, a_ref, al_ref, dt_ref, nw_ref = refs[:7]
        o_ref, og_ref, st_ref = refs[7 + ng:10 + ng]
        s_scr = refs[10 + 2 * ng]
        comm = (refs[7:7 + ng], refs[10 + ng:10 + 2 * ng], refs[11 + 2 * ng:])

        @pl.when(pl.program_id(0) == 0)
        def _():
            s_scr[...] = jnp.zeros_like(s_scr)
            if ng:
                _gather_phase(*comm, True)

        locs = [_gdn_chunk(*_split_heads(c_ref[rows[g], :]), _row_to_col(b_ref[g]), _row_to_col(a_ref[g]), al_ref[...],
                           dt_ref[...]) for g in range(GDN_CPS_FWD)]
        s = s_scr[...]
        for g, L in enumerate(locs):
            st_ref[g] = s
            both = _bd_lo(_rows2(L["kcd"], L["qd"]), s, 2, 1)
            v_new = L["value"] - both[:, :CH]
            o = both[:, CH:] + _bd_lo(L["attn"], v_new, 2, 1)
            s = s * L["cd"] + _bd_lo(L["kt"], v_new, 1, 1)
            o_ref[:, rows[g], :] = o
            zz, = _split_heads(z_ref[rows[g], :])
            rs = lax.rsqrt(jnp.mean(o * o, axis=-1, keepdims=True) + EPS)
            _store_heads(og_ref, rows[g], 0, o * rs * nw_ref[...] * zz * _sig(zz))
        s_scr[...] = s
        if ng:
            pl.when(pl.program_id(0) == ns - 1)(functools.partial(_gather_phase, *comm, False))

    return pl.pallas_call(
        body, name="gdn_fwd", grid=(ns,),
        in_specs=[tok(3 * FW, 0), tok(FW, OFF_Z // FW), col, col, sca, sca, nws] + [HBM_SPEC] * ng,
        out_specs=[mat, tok(FW, 0), st] + [HBM_SPEC] * ng,
        out_shape=[jax.ShapeDtypeStruct((HEADS, T, HD), F32), jax.ShapeDtypeStruct((T, FW), F32),
                   jax.ShapeDtypeStruct((T // CH, HEADS, HD, HD), F32)] + _gather_shapes(gather),
        scratch_shapes=[pltpu.VMEM((HEADS, HD, HD), F32)] + (_gather_sems(ng) if ng else []),
        compiler_params=_cp("arbitrary"))(conv, rest, bpre, apre, alog, dtb, nw, *gather)


def _gdn_bwd(conv, rest, bpre, apre, alog, dtb, nw, states, o, dog, scatter=()):
    T = conv.shape[0]
    ns, mat, col, sca, nws, st, tok = _gdn_specs(T, True)
    rows = [slice(g * CH, (g + 1) * CH) for g in range(GDN_CPS)]
    ng = len(scatter)

    def body(*refs):
        c_ref, z_ref, b_ref, a_ref, al_ref, dt_ref, nw_ref, st_ref, o_ref, dog_ref = refs[:10]
        dc_ref, dz_ref, db_ref, da_ref, dal_ref, ddt_ref, dnw_ref = refs[10 + ng:17 + ng]
        ds_scr = refs[17 + 2 * ng]
        comm = (refs[10:10 + ng], refs[17 + ng:17 + 2 * ng], refs[18 + 2 * ng:])

        @pl.when(pl.program_id(0) == 0)
        def _():
            ds_scr[...] = jnp.zeros_like(ds_scr)
            dal_ref[...] = jnp.zeros_like(dal_ref)
            ddt_ref[...] = jnp.zeros_like(ddt_ref)
            dnw_ref[...] = jnp.zeros_like(dnw_ref)
            if ng:
                _scatter_phase(*comm, True)

        splits = [_split_heads(c_ref[rows[g], :]) for g in range(GDN_CPS)]
        locs = [_gdn_chunk(*splits[g], _row_to_col(b_ref[g]), _row_to_col(a_ref[g]), al_ref[...], dt_ref[...])
                for g in range(GDN_CPS)]
        dsn = ds_scr[...]
        acc_al, acc_dt, acc_nw = 0.0, 0.0, 0.0
        for g in reversed(range(GDN_CPS)):
            L, vv = locs[g], splits[g][2]
            tril, strict, lb = L["tril"], L["strict"], L["lb"]
            qn, kn, kb, beta, decay, eg, egl, cd = L["qn"], L["kn"], L["kb"], L["beta"], L["decay"], L["eg"], L["egl"], L["cd"]
            value, kcd, attn, qd, kt, tm = L["value"], L["kcd"], L["attn"], L["qd"], L["kt"], L["tm"]
            s = st_ref[g]
            v_new = value - _bd_lo(kcd, s, 2, 1)
            oo = o_ref[:, rows[g], :]
            zz, = _split_heads(z_ref[rows[g], :])
            dog_, = _split_heads(dog_ref[rows[g], :])
            sz = _sig(zz)
            rs = lax.rsqrt(jnp.mean(oo * oo, axis=-1, keepdims=True) + EPS)
            oh = oo * rs
            _store_heads(dz_ref, rows[g], 0, dog_ * oh * nw_ref[...] * sz * (1.0 + zz * (1.0 - sz)))
            don = dog_ * zz * sz
            gdy = don * nw_ref[...]
            do = rs * (gdy - oh * jnp.mean(oh * gdy, axis=-1, keepdims=True))
            acc_nw = acc_nw + jnp.sum(don * oh, axis=(0, 1), keepdims=True)
            d_vnew = _bd_lo(attn, do, 1, 1) + _bd_lo(kt, dsn, 2, 1)
            both = _bd_lo(_rows2(do, d_vnew), s, 2, 2)
            d_qd, d_kcd = both[:, :CH], -both[:, CH:]
            d_attn = jnp.where(tril, _bd_lo(do, v_new, 2, 2), 0.0)
            d_kt = _bd_lo(v_new, dsn, 2, 2)
            d_cd = jnp.sum(s * dsn, axis=(1, 2), keepdims=True)
            dsn_next = cd * dsn + _bd_lo(_rows2(qd, kcd), _rows2(do, -d_vnew), 1, 1)
            dsol = _bd_hi(tm, _lanes2(d_vnew, d_kcd), 1, 1)
            d_vb, d_kbg = dsol[:, :, :HD], dsol[:, :, HD:]
            da = -jnp.where(strict, _bd_lo(dsol, L["sol"], 2, 2), 0.0)
            dkk = da * decay
            dqk = d_attn * decay
            d_decay = da * L["kk"] + d_attn * L["qk"]
            both = _bd_lo(_rows2(dkk, dqk), kn, 2, 1)
            d_kb = both[:, :CH] + d_kbg * eg
            d_qn = both[:, CH:] + d_qd * eg
            d_kn = _bd_lo(_rows2(dkk, dqk), _rows2(kb, qn), 1, 1) + d_kt * egl
            dd = d_decay * decay
            d_gc = jnp.sum(dd + d_qd * qd + d_kbg * L["kbg"] - d_kt * kt, axis=-1, keepdims=True) \
                - _row_to_col(jnp.sum(dd, axis=1, keepdims=True))
            d_gl = jnp.sum(d_kt * kt, axis=(1, 2), keepdims=True) + d_cd * cd[:, :, 0:1]
            last = lax.broadcasted_iota(jnp.int32, (1, CH, 1), 1) == CH - 1
            d_gc = d_gc + jnp.where(last, d_gl, 0.0)
            dg = _bd_hi(lb, jnp.broadcast_to(d_gc, (HEADS, CH, HD)), 1, 1, True)[:, :, 0:1]
            d_apre = -dg * L["ea"] * _sig(L["x"])
            da_ref[g] = _col_to_row(d_apre)
            acc_al = acc_al + jnp.sum(dg * L["g"], axis=1, keepdims=True)
            acc_dt = acc_dt + jnp.sum(d_apre, axis=1, keepdims=True)
            d_beta = jnp.sum(d_kb * kn + d_vb * vv, axis=-1, keepdims=True)
            db_ref[g] = _col_to_row(d_beta * beta * (1.0 - beta))
            d_kn = d_kn + d_kb * beta
            qh = L["qh"]
            _store_heads(dc_ref, rows[g], 0, (HD ** -0.5) * L["rq"] * (d_qn - qh * jnp.sum(qh * d_qn, axis=-1, keepdims=True)))
            _store_heads(dc_ref, rows[g], 1, L["rk"] * (d_kn - kn * jnp.sum(kn * d_kn, axis=-1, keepdims=True)))
            _store_heads(dc_ref, rows[g], 2, d_vb * beta)
            dsn = dsn_next
        ds_scr[...] = dsn
        dal_ref[...] += acc_al
        ddt_ref[...] += acc_dt
        dnw_ref[...] += acc_nw
        if ng:
            pl.when(pl.program_id(0) == ns - 1)(functools.partial(_scatter_phase, *comm, False))

    c3 = jax.ShapeDtypeStruct((T // CH, HEADS, 1, CH), F32)
    s3 = jax.ShapeDtypeStruct((HEADS, 1, 1), F32)
    return pl.pallas_call(
        body, name="gdn_bwd", grid=(ns,),
        in_specs=[tok(3 * FW, 0), tok(FW, OFF_Z // FW), col, col, sca, sca, nws, st, mat, tok(FW, 0)] + [HBM_SPEC] * ng,
        out_specs=[tok(3 * FW, 0), tok(FW, 0), col, col, sca, sca, nws] + [HBM_SPEC] * ng,
        out_shape=[jax.ShapeDtypeStruct((T, 3 * FW), F32), jax.ShapeDtypeStruct((T, FW), F32), c3, c3, s3, s3,
                   jax.ShapeDtypeStruct((1, 1, HD), F32)] + _scatter_shapes(scatter),
        scratch_shapes=[pltpu.VMEM((HEADS, HD, HD), F32)] + (_scatter_sems(ng) if ng else []),
        compiler_params=_cp("arbitrary"))(conv, rest, bpre, apre, alog, dtb, nw, states, o, dog, *scatter)


LATE_KEYS = {"w_branch_fox": "w_bfox", "w_branch_gdn": "w_bgdn", "w_out": "w_out", "ffn_w_up": "w_up", "ffn_w_down": "w_down"}


def _local_step(x, target, w, late=None):
    seq = x.shape[0]
    T = -(-(X0 + seq) // T_ALIGN) * T_ALIGN
    back = T - X0 - seq
    tb_att = _div(T, ATT_TILE)
    h0 = jnp.concatenate([jnp.zeros((FRONT, D), F32), w["meta"], x, jnp.zeros((back, D), F32)], axis=0)
    tgt = jnp.concatenate([jnp.zeros((X0, D), F32), target, jnp.zeros((back, D), F32)], axis=0)
    row = lambda v: v.reshape(1, -1)
    nmix, nffn, nfin = row(w["norm_mix"]), row(w["norm_ffn"]), row(w["norm_final"])
    gate_b = row(w["gate_bias"])
    fconv_b = row(w["ffn_conv_b"])
    bias128 = jnp.zeros((1, SMALL_W), F32).at[0, :HEADS].set(w["fgt_bias"])
    alog = w["a_log"].reshape(HEADS, 1, 1)
    dtb = w["dt_bias"].reshape(HEADS, 1, 1)
    gnw = w["gdn_norm"].reshape(1, 1, HD)

    a = _rms_fwd(h0, nmix, "rms_mix")
    pf = _mm(a, w["w_fox"], out_dtype=_BF, name="proj_fox")
    rest = _mm(a, w["w_rest"], name="proj_rest")
    small = rest[:, OFF_SMALL:]
    c8 = _fgate_fwd(small, bias128)[:, :HEADS]
    c_cols, c_rows = _att_cols(c8), _att_rows(c8, tb_att)
    ones_blk = jnp.asarray(np.kron(np.eye(HEADS, dtype=np.float32), np.ones((HD, HD), np.float32)))
    plan = _att_plan(_att_stats(pf, c_cols, ones_blk))
    o_fox, lse2 = _fox_fwd(pf, c_cols, c_rows, plan)
    conv = _gconv_fwd(rest, w["gdn_conv"])
    chunk_rows = lambda a8: a8.reshape(T // CH, CH, HEADS).transpose(0, 2, 1)[:, :, None, :]
    bpre = chunk_rows(small[:, HEADS:2 * HEADS])
    apre = chunk_rows(small[:, 2 * HEADS:3 * HEADS])
    if late is None:
        o_raw, og, states = _gdn_fwd(conv, rest, bpre, apre, alog, dtb, gnw)
    else:
        o_raw, og, states, *got = _gdn_fwd(conv, rest, bpre, apre, alog, dtb, gnw, gather=list(late.values()))
        w = dict(w, **{LATE_KEYS[n]: _from_chips(g, AXIS[n]) for n, g in zip(late, got)})
    w_up, w_down = w["w_up"], w["w_down"]
    y_fox = _mm(o_fox, w["w_bfox"], name="y_fox")
    y_gdn = _mm(og, w["w_bgdn"], name="y_gdn")
    mix = _mix_fwd(rest, gate_b, y_fox, y_gdn)
    h1 = _mm(mix, w["w_out"], resid=h0, name="out_proj")
    b = _rms_fwd(h1, nffn, "rms_ffn")
    up = _mm(b, w_up, name="ffn_up")
    f = _glu_fwd(up, w["ffn_conv"], fconv_b)
    h2 = _mm(f, w_down, resid=h1, name="ffn_down")
    dh2, loss, d_nfin = _loss_head(h2, nfin, tgt, seq)

    d_f = _mm(dh2, w_down, tb=True, name="d_f")
    g_down = _mm_t(f, dh2, "g_down")
    dug, duu, dwg, dwu, dbg, dbu = _glu_bwd_du(up, w["ffn_conv"], fconv_b, d_f)
    dxg = _conv_bwd_dx(dug, w["ffn_conv"][:, :DFF], _BF, "fconv_dx_gate")
    dxu = _conv_bwd_dx(duu, w["ffn_conv"][:, DFF:], _BF, "fconv_dx_up")
    d_b = _mm(dxg, w_up[:, :DFF], tb=True, name="d_b_gate")
    d_b = _mm(dxu, w_up[:, DFF:], tb=True, resid=d_b, name="d_b_up")
    g_up = jnp.concatenate([_mm_t(b, dxg, "g_up_gate"), _mm_t(b, dxu, "g_up_up")], axis=1)
    dh1, d_nffn = _rms_bwd(h1, nffn, d_b, dh2, "rms_ffn_bwd")

    dmix = _mm(dh1, w["w_out"], tb=True, name="d_mix")
    g_out = _mm_t(mix, dh1, "g_out")
    dyf, dyg, dgf, dgg, dgbf, dgbg = _mix_bwd(rest, gate_b, y_fox, y_gdn, dmix)
    do_fox = _mm(dyf, w["w_bfox"], tb=True, out_dtype=_BF, name="d_o_fox")
    g_bfox = _mm_t(o_fox, dyf, "g_bfox")
    d_og = _mm(dyg, w["w_bgdn"], tb=True, name="d_o_gdn")
    g_bgdn = _mm_t(og, dyg, "g_bgdn")

    dq, dcq, delta2 = _fox_bwd_dq(pf, do_fox, c_cols, c_rows, lse2, o_fox, plan)
    dk, dv, dck = _fox_bwd_dkv(pf, do_fox, c_cols, _att_rows(c8 - _from_pairs(lse2), tb_att),
                               _att_rows(_from_pairs(delta2), tb_att), plan)
    dc = jnp.pad(_from_pairs(dcq + dck), ((0, 0), (0, SMALL_W - HEADS)))
    dfp, d_fb = _fgate_bwd(small, bias128, dc)

    if late is None:
        late_recv = None
        dconv, dz, dbp, dap, d_alog, d_dtb, d_gnw = _gdn_bwd(conv, rest, bpre, apre, alog, dtb, gnw, states, o_raw, d_og)
    else:
        ready = dict(w_branch_fox=g_bfox, w_branch_gdn=g_bgdn, w_out=g_out, ffn_w_up=g_up, ffn_w_down=g_down)
        send = [_by_chip(ready[n].astype(jnp.bfloat16), AXIS[n]) for n in late]
        dconv, dz, dbp, dap, d_alog, d_dtb, d_gnw, *late_recv = _gdn_bwd(
            conv, rest, bpre, apre, alog, dtb, gnw, states, o_raw, d_og, scatter=send)
    du_g, g_gconv = _gconv_bwd_du(rest, w["gdn_conv"], dconv)
    dgx = _conv_bwd_dx(du_g, w["gdn_conv"], _BF, "gconv_dx")
    token_rows = lambda a: a[:, :, 0, :].transpose(0, 2, 1).reshape(T, HEADS)
    dsmall = jnp.concatenate([dfp[:, :HEADS], token_rows(dbp), token_rows(dap),
                              jnp.zeros((T, SMALL_W - 3 * HEADS), F32)], axis=1)
    drest = jnp.concatenate([dgx, dz.astype(_BF), dgf, dgg, dsmall.astype(_BF)], axis=1)
    dfox = jnp.concatenate([dq, dk, dv], axis=1)
    g_fox = _mm_t(a, dfox, "g_w_fox")
    g_rest = _mm_t(a, drest, "g_w_rest")
    sm = lambda lo: g_rest[:, OFF_SMALL + lo:OFF_SMALL + lo + HEADS]
    g_w_in = jnp.concatenate([g_fox, sm(0), g_rest[:, :3 * FW], g_rest[:, OFF_Z:OFF_Z + FW], sm(HEADS), sm(2 * HEADS),
                              g_rest[:, OFF_GATES:OFF_GATES + 2 * D]], axis=1)
    d_a = _mm(dfox, w["w_fox"], tb=True, name="d_a_fox")
    if late is None:
        d_a = _mm(drest, w["w_rest"], tb=True, resid=d_a, name="d_a_rest")
    else:
        d_a, w_in_recv = _mm(drest, w["w_rest"], tb=True, resid=d_a, name="d_a_rest",
                             scatter=[_by_chip(g_w_in.astype(jnp.bfloat16), AXIS["w_in"])])
        late_recv = [w_in_recv] + late_recv
    dh0, d_nmix = _rms_bwd(h0, nmix, d_a, dh1, "rms_mix_bwd")

    grads = dict(
        meta_tokens=dh0[FRONT:X0], w_in=g_w_in, fgt_bias=d_fb[0, :HEADS], gdn_conv_w=g_gconv,
        gdn_a_log=d_alog.reshape(HEADS), gdn_dt_bias=d_dtb.reshape(HEADS), gdn_norm_w=d_gnw.reshape(HD),
        gate_bias=jnp.concatenate([dgbf, dgbg], axis=1).reshape(2 * D), w_branch_fox=g_bfox, w_branch_gdn=g_bgdn,
        w_out=g_out, norm_mix_w=d_nmix.reshape(D), norm_ffn_w=d_nffn.reshape(D), ffn_w_up=g_up,
        ffn_conv_w=jnp.concatenate([dwg, dwu], axis=1), ffn_conv_b=jnp.concatenate([dbg, dbu], axis=1).reshape(2 * DFF),
        ffn_w_down=g_down, norm_final_w=d_nfin.reshape(D), late_recv=late_recv)
    return loss, dh0[X0:X0 + seq], grads


N_CHIPS = 4
PACK_W = 1024
PACK_ROW_ALIGN = 32
BIG_EARLY = ("w_in",)
BIG_LATE = ("w_branch_fox", "w_branch_gdn", "w_out", "ffn_w_up", "ffn_w_down")
BIG = BIG_EARLY + BIG_LATE
WEIGHTS = (
    ("meta_tokens", (N_META, D), 1), ("w_in", (D, 3 * FW + HEADS + 4 * FW + 2 * HEADS + 2 * D), 1), ("fgt_bias", (1, HEADS), None),
    ("gdn_conv_w", (4, 3 * FW), 1), ("gdn_a_log", (1, HEADS), None), ("gdn_dt_bias", (1, HEADS), None),
    ("gdn_norm_w", (1, HD), None), ("gate_bias", (1, 2 * D), None), ("w_branch_fox", (FW, D), 1),
    ("w_branch_gdn", (FW, D), 1), ("w_out", (D, D), 0), ("norm_mix_w", (1, D), None), ("norm_ffn_w", (1, D), None),
    ("ffn_w_up", (D, 2 * DFF), 1), ("ffn_conv_w", (3, 2 * DFF), 1), ("ffn_conv_b", (1, 2 * DFF), None),
    ("ffn_w_down", (DFF, D), 0), ("norm_final_w", (1, D), None))
SPLIT_F32 = ("meta_tokens", "gdn_conv_w", "ffn_conv_w")


def _shard_shape(shape, axis):
    if axis is None:
        return shape
    return tuple(s // N_CHIPS if a == axis else s for a, s in enumerate(shape))


def _shard_of(full, axis, q):
    if axis is None:
        return full
    n = full.shape[axis] // N_CHIPS
    return lax.slice_in_dim(full, q * n, (q + 1) * n, axis=axis)


def _pack_rows(n_elems):
    rows = -(-n_elems // PACK_W)
    return -(-rows // PACK_ROW_ALIGN) * PACK_ROW_ALIGN


def _pack(pieces, dtype):
    flat = jnp.concatenate([p.reshape(-1).astype(dtype) for p in pieces])
    rows = _pack_rows(flat.shape[0])
    return jnp.pad(flat, (0, rows * PACK_W - flat.shape[0])).reshape(rows, PACK_W)


def _unpack(slab, shapes):
    flat = slab.reshape(-1)
    out, off = [], 0
    for s in shapes:
        n = int(np.prod(s))
        out.append(flat[off:off + n].reshape(s))
        off += n
    return out


HBM_SPEC = pl.BlockSpec(memory_space=pltpu.HBM)
MESH_ID = pl.DeviceIdType.MESH


def _scatter_chips(srcs, name):
    n = len(srcs)

    def body(*refs):
        _scatter_phase(refs[:n], refs[n:2 * n], refs[2 * n:], True)
        _scatter_phase(refs[:n], refs[n:2 * n], refs[2 * n:], False)

    return pl.pallas_call(body, name=name, in_specs=[HBM_SPEC] * n, out_specs=[HBM_SPEC] * n,
                          out_shape=_scatter_shapes(srcs), scratch_shapes=_scatter_sems(n))(*srcs)


def _scatter_shapes(srcs):
    return [jax.ShapeDtypeStruct(s.shape, s.dtype) for s in srcs]


def _scatter_sems(n):
    return [pltpu.SemaphoreType.DMA((3 * n,)), pltpu.SemaphoreType.DMA((3 * n,)), pltpu.SemaphoreType.DMA((n,))]


def _scatter_phase(src_refs, out_refs, sems, issue):
    n = len(src_refs)
    send_sems, recv_sems, local_sems = sems
    x, y, c = lax.axis_index("x"), lax.axis_index("y"), lax.axis_index("c")
    q = 2 * x + y
    peers = [(1 - x, y), (x, 1 - y), (1 - x, 1 - y)]

    def remote(a, k, src_slot, dst_slot):
        px, py = peers[k]
        return pltpu.make_async_remote_copy(
            src_ref=src_refs[a].at[src_slot], dst_ref=out_refs[a].at[dst_slot], send_sem=send_sems.at[3 * a + k],
            recv_sem=recv_sems.at[3 * a + k], device_id=(px, py, c), device_id_type=MESH_ID)

    mine = [pltpu.make_async_copy(src_refs[a].at[q], out_refs[a].at[q], local_sems.at[a]) for a in range(n)]
    sends = [remote(a, k, 2 * px + py, q) for a in range(n) for k, (px, py) in enumerate(peers)]
    if issue:
        for cp in mine + sends:
            cp.start()
        return
    for a in range(n):
        for k, (px, py) in enumerate(peers):
            remote(a, k, 0, 2 * px + py).wait_recv()
    for cp in sends:
        cp.wait_send()
    for cp in mine:
        cp.wait()


def _gather_chips(srcs, name):
    n = len(srcs)

    def body(*refs):
        _gather_phase(refs[:n], refs[n:2 * n], refs[2 * n:], True)
        _gather_phase(refs[:n], refs[n:2 * n], refs[2 * n:], False)

    return pl.pallas_call(body, name=name, in_specs=[HBM_SPEC] * n, out_specs=[HBM_SPEC] * n,
                          out_shape=_gather_shapes(srcs), scratch_shapes=_gather_sems(n))(*srcs)


def _gather_shapes(srcs):
    return [jax.ShapeDtypeStruct((N_CHIPS,) + s.shape, s.dtype) for s in srcs]


def _gather_sems(n):
    return [pltpu.SemaphoreType.DMA((3 * n,))] * 4 + [pltpu.SemaphoreType.DMA((n,))]


def _gather_phase(src_refs, out_refs, sems, issue):
    n = len(src_refs)
    ici_send, ici_recv, d2d_send, d2d_recv, local_sems = sems
    x, y, c = lax.axis_index("x"), lax.axis_index("y"), lax.axis_index("c")
    q = 2 * x + y
    peers = [(1 - x, y), (x, 1 - y), (1 - x, 1 - y)]

    def half(a, which):
        r = src_refs[a].shape[0] // 2
        return pl.ds(which * r, r)

    def ici(a, k, slot):
        px, py = peers[k]
        return pltpu.make_async_remote_copy(
            src_ref=src_refs[a].at[half(a, c)], dst_ref=out_refs[a].at[slot, half(a, c)], send_sem=ici_send.at[3 * a + k],
            recv_sem=ici_recv.at[3 * a + k], device_id=(px, py, c), device_id_type=MESH_ID)

    def d2d(a, k, which):
        px, py = peers[k]
        rows = out_refs[a].at[2 * px + py, half(a, which)]
        return pltpu.make_async_remote_copy(
            src_ref=rows, dst_ref=rows, send_sem=d2d_send.at[3 * a + k], recv_sem=d2d_recv.at[3 * a + k],
            device_id=(x, y, 1 - c), device_id_type=MESH_ID)

    mine = [pltpu.make_async_copy(src_refs[a], out_refs[a].at[q], local_sems.at[a]) for a in range(n)]
    sends = [ici(a, k, q) for a in range(n) for k in range(3)]
    if issue:
        for cp in mine + sends:
            cp.start()
        return
    passed = []
    for a in range(n):
        for k, (px, py) in enumerate(peers):
            ici(a, k, 2 * px + py).wait_recv()
            passed.append(d2d(a, k, c))
            passed[-1].start()
    for a in range(n):
        for k in range(3):
            d2d(a, k, 1 - c).wait_recv()
    for cp in sends + passed:
        cp.wait_send()
    for cp in mine:
        cp.wait()


def _sibling_swap(slabs, name):
    n = len(slabs)

    def body(*refs):
        src_refs, out_refs, send_sems, recv_sems = refs[:n], refs[n:2 * n], refs[2 * n], refs[2 * n + 1]
        x, y, c = lax.axis_index("x"), lax.axis_index("y"), lax.axis_index("c")
        cps = [pltpu.make_async_remote_copy(src_ref=src_refs[a], dst_ref=out_refs[a], send_sem=send_sems.at[a],
                                            recv_sem=recv_sems.at[a], device_id=(x, y, 1 - c), device_id_type=MESH_ID)
               for a in range(n)]
        for cp in cps:
            cp.start()
        for cp in cps:
            cp.wait_recv()
        for cp in cps:
            cp.wait_send()

    return pl.pallas_call(
        body, name=name, in_specs=[HBM_SPEC] * n, out_specs=[HBM_SPEC] * n,
        out_shape=[jax.ShapeDtypeStruct(s.shape, s.dtype) for s in slabs],
        scratch_shapes=[pltpu.SemaphoreType.DMA((n,)), pltpu.SemaphoreType.DMA((n,))])(*slabs)


def _sum_chips(r, name):
    rows, cols = r.shape[1:]
    tb = _div(rows, 256, 16)

    def body(r0, r1, r2, r3, o_ref):
        o_ref[...] = ((r0[0].astype(F32) + r1[0].astype(F32)) + r2[0].astype(F32)) + r3[0].astype(F32)

    spec = lambda j: pl.BlockSpec((1, tb, cols), lambda i: (j, i, 0))
    return pl.pallas_call(
        body, name=name, grid=(rows // tb,), in_specs=[spec(0), spec(1), spec(2), spec(3)],
        out_specs=pl.BlockSpec((tb, cols), lambda i: (i, 0)), out_shape=jax.ShapeDtypeStruct((rows, cols), F32),
        compiler_params=_cp("parallel"))(r, r, r, r)


def _adamw(w, m, v, p, q, name):
    rows, cols = w.shape
    tb = _div(rows, 256, 8)

    def body(w_ref, m_ref, v_ref, p_ref, q_ref, g_ref, d_ref, nm_ref, nv_ref):
        g = p_ref[...] + q_ref[...]
        m_new = B1 * m_ref[...] + (1.0 - B1) * g
        v_new = B2 * v_ref[...] + (1.0 - B2) * (g * g)
        g_ref[...] = g
        nm_ref[...] = m_new
        nv_ref[...] = v_new
        m_hat = m_new / (1.0 - B1 ** STEP)
        v_hat = v_new / (1.0 - B2 ** STEP)
        d_ref[...] = -LR * (m_hat / (jnp.sqrt(v_hat) + AEPS) + WD * w_ref[...])

    spec = pl.BlockSpec((tb, cols), lambda i: (i, 0))
    return pl.pallas_call(
        body, name=name, grid=(rows // tb,), in_specs=[spec] * 5, out_specs=[spec] * 4,
        out_shape=[jax.ShapeDtypeStruct((rows, cols), F32)] * 4, compiler_params=_cp("parallel"))(w, m, v, p, q)


def _split_w_in(w_in):
    o1 = 3 * FW
    o2 = o1 + HEADS
    o3 = o2 + 3 * FW
    o4 = o3 + FW
    o5 = o4 + HEADS
    o6 = o5 + HEADS
    pad = jnp.zeros((w_in.shape[0], SMALL_W - 3 * HEADS), w_in.dtype)
    rest = jnp.concatenate([w_in[:, o2:o3], w_in[:, o3:o4], w_in[:, o6:], w_in[:, o1:o2], w_in[:, o4:o5], w_in[:, o5:o6], pad],
                           axis=1)
    return w_in[:, :o1], rest


AXIS = {n: a for n, _, a in WEIGHTS}
SMALL = tuple(n for n, _, _ in WEIGHTS if n not in BIG)


def _by_chip(full, axis):
    rows, cols = full.shape
    if axis == 0:
        return full.reshape(N_CHIPS, rows // N_CHIPS, cols)
    return full.reshape(rows, N_CHIPS, cols // N_CHIPS).transpose(1, 0, 2)


def _from_chips(parts, axis):
    _, r, c = parts.shape
    if axis == 0:
        return parts.reshape(N_CHIPS * r, c)
    return parts.transpose(1, 0, 2).reshape(r, N_CHIPS * c)


def _gather_weights(shards):
    hi = {n: shards[n].astype(jnp.bfloat16) for n in SPLIT_F32}
    lo = [(shards[n] - hi[n].astype(F32)).astype(jnp.bfloat16) for n in SPLIT_F32]
    slab = _pack([hi[n] for n in SPLIT_F32] + lo, jnp.bfloat16)
    got = _gather_chips([shards[n].astype(jnp.bfloat16) for n in BIG_EARLY] + [slab], "gather_weights")
    full = {n: _from_chips(g, AXIS[n]) for n, g in zip(BIG_EARLY, got)}
    shapes = [shards[n].shape for n in SPLIT_F32] * 2
    per_chip = [_unpack(got[-1][j], shapes) for j in range(N_CHIPS)]
    for i, n in enumerate(SPLIT_F32):
        join = lambda off: jnp.concatenate([per_chip[j][off + i] for j in range(N_CHIPS)], axis=1).astype(F32)
        full[n] = join(0) + join(len(SPLIT_F32))
    return full


def kernel(x, meta_tokens, w_in, fgt_bias, gdn_conv_w, gdn_a_log, gdn_dt_bias, gdn_norm_w, gate_bias, w_branch_fox, w_branch_gdn, w_out, norm_mix_w, norm_ffn_w, ffn_w_up, ffn_conv_w, ffn_conv_b, ffn_w_down, norm_final_w, loss_target, m_meta_tokens, m_w_in, m_fgt_bias, m_gdn_conv_w, m_gdn_a_log, m_gdn_dt_bias, m_gdn_norm_w, m_gate_bias, m_w_branch_fox, m_w_branch_gdn, m_w_out, m_norm_mix_w, m_norm_ffn_w, m_ffn_w_up, m_ffn_conv_w, m_ffn_conv_b, m_ffn_w_down, m_norm_final_w, v_meta_tokens, v_w_in, v_fgt_bias, v_gdn_conv_w, v_gdn_a_log, v_gdn_dt_bias, v_gdn_norm_w, v_gate_bias, v_w_branch_fox, v_w_branch_gdn, v_w_out, v_norm_mix_w, v_norm_ffn_w, v_ffn_w_up, v_ffn_conv_w, v_ffn_conv_b, v_ffn_w_down, v_norm_final_w):
    weights = dict(meta_tokens=meta_tokens, w_in=w_in, fgt_bias=fgt_bias, gdn_conv_w=gdn_conv_w, gdn_a_log=gdn_a_log, gdn_dt_bias=gdn_dt_bias, gdn_norm_w=gdn_norm_w, gate_bias=gate_bias, w_branch_fox=w_branch_fox, w_branch_gdn=w_branch_gdn, w_out=w_out, norm_mix_w=norm_mix_w, norm_ffn_w=norm_ffn_w, ffn_w_up=ffn_w_up, ffn_conv_w=ffn_conv_w, ffn_conv_b=ffn_conv_b, ffn_w_down=ffn_w_down, norm_final_w=norm_final_w)
    m_in = dict(meta_tokens=m_meta_tokens, w_in=m_w_in, fgt_bias=m_fgt_bias, gdn_conv_w=m_gdn_conv_w, gdn_a_log=m_gdn_a_log, gdn_dt_bias=m_gdn_dt_bias, gdn_norm_w=m_gdn_norm_w, gate_bias=m_gate_bias, w_branch_fox=m_w_branch_fox, w_branch_gdn=m_w_branch_gdn, w_out=m_w_out, norm_mix_w=m_norm_mix_w, norm_ffn_w=m_norm_ffn_w, ffn_w_up=m_ffn_w_up, ffn_conv_w=m_ffn_conv_w, ffn_conv_b=m_ffn_conv_b, ffn_w_down=m_ffn_w_down, norm_final_w=m_norm_final_w)
    v_in = dict(meta_tokens=v_meta_tokens, w_in=v_w_in, fgt_bias=v_fgt_bias, gdn_conv_w=v_gdn_conv_w, gdn_a_log=v_gdn_a_log, gdn_dt_bias=v_gdn_dt_bias, gdn_norm_w=v_gdn_norm_w, gate_bias=v_gate_bias, w_branch_fox=v_w_branch_fox, w_branch_gdn=v_w_branch_gdn, w_out=v_w_out, norm_mix_w=v_norm_mix_w, norm_ffn_w=v_norm_ffn_w, ffn_w_up=v_ffn_w_up, ffn_conv_w=v_ffn_conv_w, ffn_conv_b=v_ffn_conv_b, ffn_w_down=v_ffn_w_down, norm_final_w=v_norm_final_w)
    shard2d = {n: _shard_shape(s, a) for n, s, a in WEIGHTS}
    as2d = lambda d: {n: d[n].reshape(shard2d[n]) for n, _, _ in WEIGHTS}
    w2, m2, v2 = as2d(weights), as2d(m_in), as2d(v_in)

    full = _gather_weights(w2)
    w_fox, w_rest = _split_w_in(full["w_in"])
    flat = lambda n: w2[n].reshape(-1)
    local_w = dict(
        meta=full["meta_tokens"], w_fox=w_fox, w_rest=w_rest, fgt_bias=flat("fgt_bias"), gdn_conv=full["gdn_conv_w"],
        a_log=flat("gdn_a_log"), dt_bias=flat("gdn_dt_bias"), gdn_norm=flat("gdn_norm_w"), gate_bias=flat("gate_bias"),
        norm_mix=flat("norm_mix_w"), norm_ffn=flat("norm_ffn_w"), ffn_conv=full["ffn_conv_w"], ffn_conv_b=flat("ffn_conv_b"),
        norm_final=flat("norm_final_w"))

    loss, grad_x, grads = _local_step(x[0], loss_target[0], local_w, {n: w2[n].astype(jnp.bfloat16) for n in BIG_LATE})

    g2 = {n: grads[n].reshape(s) for n, s, _ in WEIGHTS}
    small_send = jnp.stack([_pack([_shard_of(g2[n], AXIS[n], j) for n in SMALL], F32) for j in range(N_CHIPS)])
    recv = list(grads["late_recv"]) + list(_scatter_chips([small_send], "scatter_grads"))
    parts = [_sum_chips(r, "sum_" + n) for r, n in zip(recv, BIG + ("small",))]
    others = _sibling_swap(parts, "swap_grads")
    slab = lambda d: _pack([d[n] for n in SMALL], F32)
    state = [(w2[n], m2[n], v2[n]) for n in BIG] + [(slab(w2), slab(m2), slab(v2))]
    outs = [_adamw(w, m, v, p, q, "adamw_" + n) for (w, m, v), p, q, n in zip(state, parts, others, BIG + ("small",))]
    small = [_unpack(o, [weights[n].shape for n in SMALL]) for o in outs[-1]]
    result = []
    for kind in range(4):
        by_name = {n: outs[i][kind].reshape(weights[n].shape) for i, n in enumerate(BIG)}
        by_name.update(zip(SMALL, small[kind]))
        result += [by_name[n] for n, _, _ in WEIGHTS]
    total = lax.psum(loss[0, 0], ("x", "y", "c"))
    return (total, grad_x[None], *result)
```
